```python
import math
import jax, jax.numpy as jnp
from jax import lax
import numpy as np

D_MODEL = 2048
BATCH = 8
SEQ = 4096
DEPTH = 1

MIX_WIDTH = D_MODEL
HEAD_DIM = 128
ATTN_WIDTH = MIX_WIDTH // 2
ATTN_HEADS = ATTN_WIDTH // HEAD_DIM
KV_HEADS = 2
GQA_GROUP = ATTN_HEADS // KV_HEADS
WINDOW = 128
ATTN_BLOCK = 128
ROPE_THETA = 10000.0

GLA_WIDTH = MIX_WIDTH - ATTN_WIDTH
GLA_HEADS = 4
GLA_KEY_WIDTH = GLA_WIDTH // 2
GLA_DK = GLA_KEY_WIDTH // GLA_HEADS
GLA_DV = GLA_WIDTH // GLA_HEADS
GLA_DECAY_RANK = 16
GLA_GATE_NORMALIZER = 16.0
GLA_CHUNK = 64

D_FF = 5632
CONV_WIDTH = 3
NORM_EPS = 1e-6

IN_WIDTHS = (ATTN_WIDTH,
             KV_HEADS * HEAD_DIM,
             KV_HEADS * HEAD_DIM,
             GLA_KEY_WIDTH,
             GLA_KEY_WIDTH,
             GLA_WIDTH,
             GLA_WIDTH,
             GLA_DECAY_RANK,
             GLA_DECAY_RANK)
IN_TOTAL = sum(IN_WIDTHS)

kernel_name = "hymba_style_swa_gla_convffn_encoder"


def rms_norm(x, g):
    xf = x.astype(jnp.float32)
    y = xf * lax.rsqrt(jnp.mean(xf * xf, axis=-1, keepdims=True) + NORM_EPS)
    return (y * g.astype(jnp.float32)).astype(x.dtype)


def rope(x, pos):
    half = x.shape[-1] // 2
    inv = 1.0 / (ROPE_THETA ** (jnp.arange(half, dtype=jnp.float32) / half))
    ang = pos.astype(jnp.float32)[:, None] * inv[None, :]
    cos = jnp.cos(ang)[:, None, :]
    sin = jnp.sin(ang)[:, None, :]
    xf = x.astype(jnp.float32)
    x1, x2 = xf[..., :half], xf[..., half:]
    return jnp.concatenate([x1 * cos - x2 * sin, x2 * cos + x1 * sin], axis=-1).astype(x.dtype)


def window_attention(q, k, v, sink):
    B, T, HQ, D = q.shape
    nb = T // ATTN_BLOCK
    qb = q.reshape(B, nb, ATTN_BLOCK, KV_HEADS, GQA_GROUP, D)
    pad = ((0, 0), (ATTN_BLOCK, ATTN_BLOCK), (0, 0), (0, 0))
    kb = jnp.pad(k, pad).reshape(B, nb + 2, ATTN_BLOCK, KV_HEADS, D)
    vb = jnp.pad(v, pad).reshape(B, nb + 2, ATTN_BLOCK, KV_HEADS, D)
    kw = jnp.concatenate([kb[:, :-2], kb[:, 1:-1], kb[:, 2:]], axis=2)
    vw = jnp.concatenate([vb[:, :-2], vb[:, 1:-1], vb[:, 2:]], axis=2)
    s = jnp.einsum('bnqhgd,bnshd->bhgnqs', qb, kw).astype(jnp.float32) * (D ** -0.5)
    qi = jnp.arange(ATTN_BLOCK)[:, None]
    sj = jnp.arange(3 * ATTN_BLOCK)[None, :]
    rel = sj - ATTN_BLOCK - qi
    kpos = jnp.arange(nb)[:, None, None] * ATTN_BLOCK - ATTN_BLOCK + sj[None]
    mask = (jnp.abs(rel) <= WINDOW)[None] & (kpos >= 0) & (kpos < T)
    s = jnp.where(mask, s, -jnp.inf)
    sk = sink.astype(jnp.float32).reshape(KV_HEADS, GQA_GROUP)[None, :, :, None, None, None]
    m = jnp.maximum(jnp.max(s, axis=-1, keepdims=True), sk)
    p = jnp.exp(s - m)
    p = p / (jnp.sum(p, axis=-1, keepdims=True) + jnp.exp(sk - m))
    o = jnp.einsum('bhgnqs,bnshd->bnqhgd', p.astype(v.dtype), vw)
    return o.reshape(B, T, HQ * D)


def gla_chunked(q, k, v, g, strict):
    B, H, T, DK = q.shape
    DV = v.shape[-1]
    C = GLA_CHUNK
    n = T // C
    q = q.reshape(B, H, n, C, DK)
    k = k.reshape(B, H, n, C, DK)
    g = g.reshape(B, H, n, C, DK)
    v = v.reshape(B, H, n, C, DV)
    b = jnp.cumsum(g, axis=3)
    b_last = b[:, :, :, -1:, :]
    b_ref = b[:, :, :, C // 2:C // 2 + 1, :]
    a = jnp.einsum('bhnid,bhnjd->bhnij', q * jnp.exp(b - b_ref), k * jnp.exp(b_ref - b))
    mask = jnp.tril(jnp.ones((C, C), dtype=bool), -1 if strict else 0)
    a = jnp.where(mask, a, 0.0)
    o_intra = jnp.einsum('bhnij,bhnje->bhnie', a, v)
    q_in = q * jnp.exp(b)
    k_out = k * jnp.exp(b_last - b)
    chunk_decay = jnp.exp(b_last[:, :, :, 0, :])

    def step(state, inp):
        qc, kc, vc, dc = inp
        o = jnp.einsum('bhid,bhde->bhie', qc, state)
        state = state * dc[..., None] + jnp.einsum('bhjd,bhje->bhde', kc, vc)
        return state, o

    s0 = jnp.zeros((B, H, DK, DV), jnp.float32)
    xs = (jnp.moveaxis(q_in, 2, 0), jnp.moveaxis(k_out, 2, 0),
          jnp.moveaxis(v, 2, 0), jnp.moveaxis(chunk_decay, 2, 0))
    _, o_inter = lax.scan(step, s0, xs)
    o_inter = jnp.moveaxis(o_inter, 0, 2)
    return (o_intra + o_inter).reshape(B, H, T, DV)


def bidirectional_gla(q, k, v, lr_f, lr_b, wa2_f, ba_f, wa2_b, ba_b, gate, out_norm_g):
    B, T, _ = q.shape
    f32 = jnp.float32

    def heads(t, d):
        return jnp.transpose(t.astype(f32).reshape(B, T, GLA_HEADS, d), (0, 2, 1, 3))

    qh = heads(q, GLA_DK) * (GLA_DK ** -0.5)
    kh = heads(k, GLA_DK)
    vh = heads(v, GLA_DV)
    g_f = jax.nn.log_sigmoid((lr_f @ wa2_f + ba_f).astype(f32)) / GLA_GATE_NORMALIZER
    g_b = jax.nn.log_sigmoid((lr_b @ wa2_b + ba_b).astype(f32)) / GLA_GATE_NORMALIZER
    gf = heads(g_f, GLA_DK)
    gb = heads(g_b, GLA_DK)
    o_fwd = gla_chunked(qh, kh, vh, gf, strict=False)
    flip = lambda t: jnp.flip(t, axis=2)
    o_bwd = flip(gla_chunked(flip(qh), flip(kh), flip(vh), flip(gb), strict=True))
    o = jnp.transpose(o_fwd + o_bwd, (0, 2, 1, 3)).astype(v.dtype)
    o = rms_norm(o, out_norm_g)
    o = o * jax.nn.silu(gate.reshape(B, T, GLA_HEADS, GLA_DV))
    return o.reshape(B, T, GLA_WIDTH)


def conv_gated_ffn(h, w_up, conv_w, conv_b, w_down):
    u = h @ w_up
    up_ = jnp.pad(u, ((0, 0), (1, 1), (0, 0)))
    u = up_[:, :-2] * conv_w[0] + up_[:, 1:-1] * conv_w[1] + up_[:, 2:] * conv_w[2] + conv_b
    gate, val = jnp.split(u, 2, axis=-1)
    return (jax.nn.silu(gate) * val) @ w_down


def hybrid_layer(x, norm1_g, w_in, q_norm_g, k_norm_g, sink, wa2_f, ba_f, wa2_b, ba_b,
                 gla_out_norm_g, w_out, norm2_g, w_up, conv_w, conv_b, w_down):
    B, T, _ = x.shape
    pos = jnp.arange(T, dtype=jnp.int32)
    h = rms_norm(x, norm1_g)
    proj = h @ w_in
    offsets = []
    acc = 0
    for w in IN_WIDTHS[:-1]:
        acc += w
        offsets.append(acc)
    q_a, k_a, v_a, q_g, k_g, v_g, gate_g, lr_f, lr_b = jnp.split(proj, offsets, axis=-1)
    qa = rope(rms_norm(q_a.reshape(B, T, ATTN_HEADS, HEAD_DIM), q_norm_g), pos)
    ka = rope(rms_norm(k_a.reshape(B, T, KV_HEADS, HEAD_DIM), k_norm_g), pos)
    va = v_a.reshape(B, T, KV_HEADS, HEAD_DIM)
    o_attn = window_attention(qa, ka, va, sink)
    o_gla = bidirectional_gla(q_g, k_g, v_g, lr_f, lr_b, wa2_f, ba_f, wa2_b, ba_b,
                              gate_g, gla_out_norm_g)
    x = x + jnp.concatenate([o_attn, o_gla], axis=-1) @ w_out
    x = x + conv_gated_ffn(rms_norm(x, norm2_g), w_up, conv_w, conv_b, w_down)
    return x


def _fwd_setup_inputs(seed: int = 0) -> dict:
    key = jax.random.key(seed)
    ks = jax.random.split(key, 20)
    f32 = jnp.float32
    L = DEPTH

    def nrm(k, shape, scale):
        return jax.random.normal(k, shape, f32) * scale

    return {
        "x": nrm(ks[0], (BATCH, SEQ, D_MODEL), 1.0),
        "norm1_g": 1.0 + nrm(ks[1], (L, D_MODEL), 0.02),
        "w_in": nrm(ks[2], (L, D_MODEL, IN_TOTAL), D_MODEL ** -0.5),
        "attn_q_norm_g": 1.0 + nrm(ks[3], (L, HEAD_DIM), 0.02),
        "attn_k_norm_g": 1.0 + nrm(ks[4], (L, HEAD_DIM), 0.02),
        "attn_sink": nrm(ks[5], (L, ATTN_HEADS), 0.5),
        "gla_wa2_fwd": nrm(ks[6], (L, GLA_DECAY_RANK, GLA_KEY_WIDTH), GLA_DECAY_RANK ** -0.5),
        "gla_ba_fwd": nrm(ks[7], (L, GLA_KEY_WIDTH), 0.1),
        "gla_wa2_bwd": nrm(ks[8], (L, GLA_DECAY_RANK, GLA_KEY_WIDTH), GLA_DECAY_RANK ** -0.5),
        "gla_ba_bwd": nrm(ks[9], (L, GLA_KEY_WIDTH), 0.1),
        "gla_out_norm_g": 1.0 + nrm(ks[10], (L, GLA_DV), 0.02),
        "w_out": nrm(ks[11], (L, MIX_WIDTH, D_MODEL), MIX_WIDTH ** -0.5),
        "norm2_g": 1.0 + nrm(ks[12], (L, D_MODEL), 0.02),
        "w_up": nrm(ks[13], (L, D_MODEL, 2 * D_FF), D_MODEL ** -0.5),
        "conv_w": nrm(ks[14], (L, CONV_WIDTH, 2 * D_FF), CONV_WIDTH ** -0.5),
        "conv_b": nrm(ks[15], (L, 2 * D_FF), 0.02),
        "w_down": nrm(ks[16], (L, D_FF, D_MODEL), D_FF ** -0.5),
    }


def _fwd_reference(x, norm1_g, w_in, attn_q_norm_g, attn_k_norm_g, attn_sink, gla_wa2_fwd,
              gla_ba_fwd, gla_wa2_bwd, gla_ba_bwd, gla_out_norm_g, w_out, norm2_g,
              w_up, conv_w, conv_b, w_down):
    for l in range(DEPTH):
        x = hybrid_layer(x, norm1_g[l], w_in[l], attn_q_norm_g[l], attn_k_norm_g[l],
                         attn_sink[l], gla_wa2_fwd[l], gla_ba_fwd[l], gla_wa2_bwd[l],
                         gla_ba_bwd[l], gla_out_norm_g[l], w_out[l], norm2_g[l],
                         w_up[l], conv_w[l], conv_b[l], w_down[l])
    return x


import jax as _jax
import jax.numpy as _jnp

TWIN_FORMAT = 'train_step'
FWD_PARAMS = ['x', 'norm1_g', 'w_in', 'attn_q_norm_g', 'attn_k_norm_g', 'attn_sink', 'gla_wa2_fwd', 'gla_ba_fwd', 'gla_wa2_bwd', 'gla_ba_bwd', 'gla_out_norm_g', 'w_out', 'norm2_g', 'w_up', 'conv_w', 'conv_b', 'w_down']
TWIN_WEIGHTS = ['norm1_g', 'w_in', 'attn_q_norm_g', 'attn_k_norm_g', 'attn_sink', 'gla_wa2_fwd', 'gla_ba_fwd', 'gla_wa2_bwd', 'gla_ba_bwd', 'gla_out_norm_g', 'w_out', 'norm2_g', 'w_up', 'conv_w', 'conv_b', 'w_down']
TWIN_DIFF_INPUT = 'x'
TWIN_INPUTS = ['x', 'norm1_g', 'w_in', 'attn_q_norm_g', 'attn_k_norm_g', 'attn_sink', 'gla_wa2_fwd', 'gla_ba_fwd', 'gla_wa2_bwd', 'gla_ba_bwd', 'gla_out_norm_g', 'w_out', 'norm2_g', 'w_up', 'conv_w', 'conv_b', 'w_down', 'loss_target', 'm_norm1_g', 'm_w_in', 'm_attn_q_norm_g', 'm_attn_k_norm_g', 'm_attn_sink', 'm_gla_wa2_fwd', 'm_gla_ba_fwd', 'm_gla_wa2_bwd', 'm_gla_ba_bwd', 'm_gla_out_norm_g', 'm_w_out', 'm_norm2_g', 'm_w_up', 'm_conv_w', 'm_conv_b', 'm_w_down', 'v_norm1_g', 'v_w_in', 'v_attn_q_norm_g', 'v_attn_k_norm_g', 'v_attn_sink', 'v_gla_wa2_fwd', 'v_gla_ba_fwd', 'v_gla_wa2_bwd', 'v_gla_ba_bwd', 'v_gla_out_norm_g', 'v_w_out', 'v_norm2_g', 'v_w_up', 'v_conv_w', 'v_conv_b', 'v_w_down']
TWIN_OUTPUTS = ['loss', 'grad_x', 'grad_norm1_g', 'grad_w_in', 'grad_attn_q_norm_g', 'grad_attn_k_norm_g', 'grad_attn_sink', 'grad_gla_wa2_fwd', 'grad_gla_ba_fwd', 'grad_gla_wa2_bwd', 'grad_gla_ba_bwd', 'grad_gla_out_norm_g', 'grad_w_out', 'grad_norm2_g', 'grad_w_up', 'grad_conv_w', 'grad_conv_b', 'grad_w_down', 'delta_norm1_g', 'delta_w_in', 'delta_attn_q_norm_g', 'delta_attn_k_norm_g', 'delta_attn_sink', 'delta_gla_wa2_fwd', 'delta_gla_ba_fwd', 'delta_gla_wa2_bwd', 'delta_gla_ba_bwd', 'delta_gla_out_norm_g', 'delta_w_out', 'delta_norm2_g', 'delta_w_up', 'delta_conv_w', 'delta_conv_b', 'delta_w_down', 'new_m_norm1_g', 'new_m_w_in', 'new_m_attn_q_norm_g', 'new_m_attn_k_norm_g', 'new_m_attn_sink', 'new_m_gla_wa2_fwd', 'new_m_gla_ba_fwd', 'new_m_gla_wa2_bwd', 'new_m_gla_ba_bwd', 'new_m_gla_out_norm_g', 'new_m_w_out', 'new_m_norm2_g', 'new_m_w_up', 'new_m_conv_w', 'new_m_conv_b', 'new_m_w_down', 'new_v_norm1_g', 'new_v_w_in', 'new_v_attn_q_norm_g', 'new_v_attn_k_norm_g', 'new_v_attn_sink', 'new_v_gla_wa2_fwd', 'new_v_gla_ba_fwd', 'new_v_gla_wa2_bwd', 'new_v_gla_ba_bwd', 'new_v_gla_out_norm_g', 'new_v_w_out', 'new_v_norm2_g', 'new_v_w_up', 'new_v_conv_w', 'new_v_conv_b', 'new_v_w_down']
TWIN_LEAF_KINDS = {'loss': 'loss', 'grad_x': 'grad_x', 'grad_norm1_g': 'grad_w', 'grad_w_in': 'grad_w', 'grad_attn_q_norm_g': 'grad_w', 'grad_attn_k_norm_g': 'grad_w', 'grad_attn_sink': 'grad_w', 'grad_gla_wa2_fwd': 'grad_w', 'grad_gla_ba_fwd': 'grad_w', 'grad_gla_wa2_bwd': 'grad_w', 'grad_gla_ba_bwd': 'grad_w', 'grad_gla_out_norm_g': 'grad_w', 'grad_w_out': 'grad_w', 'grad_norm2_g': 'grad_w', 'grad_w_up': 'grad_w', 'grad_conv_w': 'grad_w', 'grad_conv_b': 'grad_w', 'grad_w_down': 'grad_w', 'delta_norm1_g': 'delta_w', 'delta_w_in': 'delta_w', 'delta_attn_q_norm_g': 'delta_w', 'delta_attn_k_norm_g': 'delta_w', 'delta_attn_sink': 'delta_w', 'delta_gla_wa2_fwd': 'delta_w', 'delta_gla_ba_fwd': 'delta_w', 'delta_gla_wa2_bwd': 'delta_w', 'delta_gla_ba_bwd': 'delta_w', 'delta_gla_out_norm_g': 'delta_w', 'delta_w_out': 'delta_w', 'delta_norm2_g': 'delta_w', 'delta_w_up': 'delta_w', 'delta_conv_w': 'delta_w', 'delta_conv_b': 'delta_w', 'delta_w_down': 'delta_w', 'new_m_norm1_g': 'new_m', 'new_m_w_in': 'new_m', 'new_m_attn_q_norm_g': 'new_m', 'new_m_attn_k_norm_g': 'new_m', 'new_m_attn_sink': 'new_m', 'new_m_gla_wa2_fwd': 'new_m', 'new_m_gla_ba_fwd': 'new_m', 'new_m_gla_wa2_bwd': 'new_m', 'new_m_gla_ba_bwd': 'new_m', 'new_m_gla_out_norm_g': 'new_m', 'new_m_w_out': 'new_m', 'new_m_norm2_g': 'new_m', 'new_m_w_up': 'new_m', 'new_m_conv_w': 'new_m', 'new_m_conv_b': 'new_m', 'new_m_w_down': 'new_m', 'new_v_norm1_g': 'new_v', 'new_v_w_in': 'new_v', 'new_v_attn_q_norm_g': 'new_v', 'new_v_attn_k_norm_g': 'new_v', 'new_v_attn_sink': 'new_v', 'new_v_gla_wa2_fwd': 'new_v', 'new_v_gla_ba_fwd': 'new_v', 'new_v_gla_wa2_bwd': 'new_v', 'new_v_gla_ba_bwd': 'new_v', 'new_v_gla_out_norm_g': 'new_v', 'new_v_w_out': 'new_v', 'new_v_norm2_g': 'new_v', 'new_v_w_up': 'new_v', 'new_v_conv_w': 'new_v', 'new_v_conv_b': 'new_v', 'new_v_w_down': 'new_v'}


def _forward(args):
    return _fwd_reference(*[args[k] for k in FWD_PARAMS])


def _output_shape():
    def fwd():
        inp = _fwd_setup_inputs(0)
        return _fwd_reference(*[inp[k] for k in FWD_PARAMS])
    out = _jax.eval_shape(fwd)
    return out.shape, out.dtype

N_MICROBATCH = 1
ADAM_LR = 0.001
ADAM_B1 = 0.9
ADAM_B2 = 0.999
ADAM_EPS = 1e-08
ADAM_WD = 0.01
ADAM_STEP = 10
PER_EXAMPLE_BATCH_AXIS = {'x': 0, 'loss_target': 0}
SHARED_INPUTS = []
_WEIGHT_DTYPES = {'norm1_g': _jnp.float32, 'w_in': _jnp.float32, 'attn_q_norm_g': _jnp.float32, 'attn_k_norm_g': _jnp.float32, 'attn_sink': _jnp.float32, 'gla_wa2_fwd': _jnp.float32, 'gla_ba_fwd': _jnp.float32, 'gla_wa2_bwd': _jnp.float32, 'gla_ba_bwd': _jnp.float32, 'gla_out_norm_g': _jnp.float32, 'w_out': _jnp.float32, 'norm2_g': _jnp.float32, 'w_up': _jnp.float32, 'conv_w': _jnp.float32, 'conv_b': _jnp.float32, 'w_down': _jnp.float32}
MOMENT_SCALE = {'norm1_g': 3.346458e+00, 'w_in': 1.575361e-01, 'attn_q_norm_g': 1.030510e+00, 'attn_k_norm_g': 1.032389e+00, 'attn_sink': 5.649807e-02, 'gla_wa2_fwd': 1.765399e-02, 'gla_ba_fwd': 6.611046e-02, 'gla_wa2_bwd': 1.763038e-02, 'gla_ba_bwd': 6.759382e-02, 'gla_out_norm_g': 2.273076e+01, 'w_out': 1.192419e-01, 'norm2_g': 1.280968e+01, 'w_up': 1.033052e-01, 'conv_w': 1.780033e+00, 'conv_b': 1.568581e+00, 'w_down': 1.377954e-01}


def _to_microbatches(a, axis):
    t = _jnp.moveaxis(a, axis, 0)
    t = t.reshape((N_MICROBATCH, t.shape[0] // N_MICROBATCH) + t.shape[1:])
    return _jnp.moveaxis(t, 1, axis + 1)


def setup_inputs(seed: int = 0) -> dict:
    inp = _fwd_setup_inputs(seed)
    key = _jax.random.fold_in(_jax.random.key(seed), 7919)
    shape, _ = _output_shape()
    out = dict(inp)
    out["loss_target"] = _jax.random.normal(_jax.random.fold_in(key, 0), shape, _jnp.float32)
    for i, name in enumerate(TWIN_WEIGHTS):
        w = inp[name].astype(_jnp.float32)
        if MOMENT_SCALE is None:
            s = _jnp.sqrt(_jnp.mean(_jnp.square(w)) + 1e-30)
        else:
            s = MOMENT_SCALE[name]
        km, kv = _jax.random.split(_jax.random.fold_in(key, i + 1))
        out[name] = w
        out["m_" + name] = s * _jax.random.normal(km, w.shape, _jnp.float32)
        out["v_" + name] = (s * s) * _jax.random.uniform(kv, w.shape, _jnp.float32, 0.5, 1.5)
    if N_MICROBATCH > 1:
        for name, axis in PER_EXAMPLE_BATCH_AXIS.items():
            out[name] = _to_microbatches(out[name], axis)
    return {'x': out['x'], 'norm1_g': out['norm1_g'], 'w_in': out['w_in'], 'attn_q_norm_g': out['attn_q_norm_g'], 'attn_k_norm_g': out['attn_k_norm_g'], 'attn_sink': out['attn_sink'], 'gla_wa2_fwd': out['gla_wa2_fwd'], 'gla_ba_fwd': out['gla_ba_fwd'], 'gla_wa2_bwd': out['gla_wa2_bwd'], 'gla_ba_bwd': out['gla_ba_bwd'], 'gla_out_norm_g': out['gla_out_norm_g'], 'w_out': out['w_out'], 'norm2_g': out['norm2_g'], 'w_up': out['w_up'], 'conv_w': out['conv_w'], 'conv_b': out['conv_b'], 'w_down': out['w_down'], 'loss_target': out['loss_target'], 'm_norm1_g': out['m_norm1_g'], 'm_w_in': out['m_w_in'], 'm_attn_q_norm_g': out['m_attn_q_norm_g'], 'm_attn_k_norm_g': out['m_attn_k_norm_g'], 'm_attn_sink': out['m_attn_sink'], 'm_gla_wa2_fwd': out['m_gla_wa2_fwd'], 'm_gla_ba_fwd': out['m_gla_ba_fwd'], 'm_gla_wa2_bwd': out['m_gla_wa2_bwd'], 'm_gla_ba_bwd': out['m_gla_ba_bwd'], 'm_gla_out_norm_g': out['m_gla_out_norm_g'], 'm_w_out': out['m_w_out'], 'm_norm2_g': out['m_norm2_g'], 'm_w_up': out['m_w_up'], 'm_conv_w': out['m_conv_w'], 'm_conv_b': out['m_conv_b'], 'm_w_down': out['m_w_down'], 'v_norm1_g': out['v_norm1_g'], 'v_w_in': out['v_w_in'], 'v_attn_q_norm_g': out['v_attn_q_norm_g'], 'v_attn_k_norm_g': out['v_attn_k_norm_g'], 'v_attn_sink': out['v_attn_sink'], 'v_gla_wa2_fwd': out['v_gla_wa2_fwd'], 'v_gla_ba_fwd': out['v_gla_ba_fwd'], 'v_gla_wa2_bwd': out['v_gla_wa2_bwd'], 'v_gla_ba_bwd': out['v_gla_ba_bwd'], 'v_gla_out_norm_g': out['v_gla_out_norm_g'], 'v_w_out': out['v_w_out'], 'v_norm2_g': out['v_norm2_g'], 'v_w_up': out['v_w_up'], 'v_conv_w': out['v_conv_w'], 'v_conv_b': out['v_conv_b'], 'v_w_down': out['v_w_down']}


def _loss(weights, diff, rest, loss_target):
    with _jax.named_scope("forward"):
        args = {**rest, TWIN_DIFF_INPUT: diff, **{k: w.astype(_WEIGHT_DTYPES[k]) for k, w in weights.items()}}
        y = _forward(args)
    with _jax.named_scope("loss_head"):
        err = _jnp.square(y.astype(_jnp.float32) - loss_target)
        return 0.5 * _jnp.sum(_jnp.mean(err, axis=-1)) if err.ndim else 0.5 * err


def _adamw(w, g, m, v):
    m = ADAM_B1 * m + (1.0 - ADAM_B1) * g
    v = ADAM_B2 * v + (1.0 - ADAM_B2) * _jnp.square(g)
    m_hat = m / (1.0 - ADAM_B1 ** ADAM_STEP)
    v_hat = v / (1.0 - ADAM_B2 ** ADAM_STEP)
    delta = -ADAM_LR * (m_hat / (_jnp.sqrt(v_hat) + ADAM_EPS) + ADAM_WD * w)
    return delta, m, v


def reference(x, norm1_g, w_in, attn_q_norm_g, attn_k_norm_g, attn_sink, gla_wa2_fwd, gla_ba_fwd, gla_wa2_bwd, gla_ba_bwd, gla_out_norm_g, w_out, norm2_g, w_up, conv_w, conv_b, w_down, loss_target, m_norm1_g, m_w_in, m_attn_q_norm_g, m_attn_k_norm_g, m_attn_sink, m_gla_wa2_fwd, m_gla_ba_fwd, m_gla_wa2_bwd, m_gla_ba_bwd, m_gla_out_norm_g, m_w_out, m_norm2_g, m_w_up, m_conv_w, m_conv_b, m_w_down, v_norm1_g, v_w_in, v_attn_q_norm_g, v_attn_k_norm_g, v_attn_sink, v_gla_wa2_fwd, v_gla_ba_fwd, v_gla_wa2_bwd, v_gla_ba_bwd, v_gla_out_norm_g, v_w_out, v_norm2_g, v_w_up, v_conv_w, v_conv_b, v_w_down):
    given = dict(x=x, norm1_g=norm1_g, w_in=w_in, attn_q_norm_g=attn_q_norm_g, attn_k_norm_g=attn_k_norm_g, attn_sink=attn_sink, gla_wa2_fwd=gla_wa2_fwd, gla_ba_fwd=gla_ba_fwd, gla_wa2_bwd=gla_wa2_bwd, gla_ba_bwd=gla_ba_bwd, gla_out_norm_g=gla_out_norm_g, w_out=w_out, norm2_g=norm2_g, w_up=w_up, conv_w=conv_w, conv_b=conv_b, w_down=w_down, loss_target=loss_target, m_norm1_g=m_norm1_g, m_w_in=m_w_in, m_attn_q_norm_g=m_attn_q_norm_g, m_attn_k_norm_g=m_attn_k_norm_g, m_attn_sink=m_attn_sink, m_gla_wa2_fwd=m_gla_wa2_fwd, m_gla_ba_fwd=m_gla_ba_fwd, m_gla_wa2_bwd=m_gla_wa2_bwd, m_gla_ba_bwd=m_gla_ba_bwd, m_gla_out_norm_g=m_gla_out_norm_g, m_w_out=m_w_out, m_norm2_g=m_norm2_g, m_w_up=m_w_up, m_conv_w=m_conv_w, m_conv_b=m_conv_b, m_w_down=m_w_down, v_norm1_g=v_norm1_g, v_w_in=v_w_in, v_attn_q_norm_g=v_attn_q_norm_g, v_attn_k_norm_g=v_attn_k_norm_g, v_attn_sink=v_attn_sink, v_gla_wa2_fwd=v_gla_wa2_fwd, v_gla_ba_fwd=v_gla_ba_fwd, v_gla_wa2_bwd=v_gla_wa2_bwd, v_gla_ba_bwd=v_gla_ba_bwd, v_gla_out_norm_g=v_gla_out_norm_g, v_w_out=v_w_out, v_norm2_g=v_norm2_g, v_w_up=v_w_up, v_conv_w=v_conv_w, v_conv_b=v_conv_b, v_w_down=v_w_down)
    weights = {n: given[n] for n in TWIN_WEIGHTS}
    shared = {n: given[n] for n in SHARED_INPUTS}
    per_example = {n: given[n] for n in ['x']}
    grad_fn = _jax.value_and_grad(_loss, argnums=(0, 1))

    def one_microbatch(ex, loss_target):
        ex = dict(ex)
        diff = ex.pop(TWIN_DIFF_INPUT)
        return grad_fn(weights, diff, {**shared, **ex}, loss_target)

    if N_MICROBATCH == 1:
        loss, (grad_w, grad_x) = one_microbatch(per_example, given["loss_target"])
    else:
        def body(carry, xs):
            loss_sum, grad_sum = carry
            l_k, (gw_k, gx_k) = one_microbatch(xs[0], xs[1])
            with _jax.named_scope("update"):
                return (loss_sum + l_k, _jax.tree.map(_jnp.add, grad_sum, gw_k)), gx_k

        init = (_jnp.zeros((), _jnp.float32), _jax.tree.map(_jnp.zeros_like, weights))
        (loss, grad_w), grad_x = _jax.lax.scan(body, init, (per_example, given["loss_target"]))
    with _jax.named_scope("update"):
        delta_w, new_m, new_v = {}, {}, {}
        for n in TWIN_WEIGHTS:
            delta_w[n], new_m[n], new_v[n] = _adamw(weights[n], grad_w[n], given["m_" + n], given["v_" + n])
    return (loss, grad_x, *[grad_w[n] for n in TWIN_WEIGHTS], *[delta_w[n] for n in TWIN_WEIGHTS],
            *[new_m[n] for n in TWIN_WEIGHTS], *[new_v[n] for n in TWIN_WEIGHTS])
```

```python
import functools

import jax
import jax.numpy as jnp
from jax import lax
from jax.experimental import pallas as pl
from jax.experimental.pallas import tpu as pltpu

F32 = jnp.float32
BF16 = jnp.bfloat16

D_MODEL = 2048
HEAD_DIM = 128
ATTN_WIDTH = 1024
ATTN_HEADS = 8
KV_HEADS = 2
GQA_GROUP = 4
KV_WIDTH = KV_HEADS * HEAD_DIM
ATTN_BLOCK = 128
WINDOW = 128
ROPE_THETA = 10000.0
GLA_HEADS = 4
GLA_DK = 128
GLA_DV = 256
GLA_KEY_WIDTH = 512
GLA_WIDTH = 1024
GLA_RANK = 16
GLA_GATE_NORMALIZER = 16.0
GLA_CHUNK = 64
D_FF = 5632
NORM_EPS = 1e-6
IN_TOTAL = 4640
IN_MAIN = 4608
LR_PAD = 128
N_DEV = 8
N_CHIP = 4

ADAM_LR = 0.001
ADAM_B1 = 0.9
ADAM_B2 = 0.999
ADAM_EPS = 1e-08
ADAM_WD = 0.01
ADAM_STEP = 10

SEGMENTS = {
    "qa": (0, 0, 1024),
    "gate": (1024, 3584, 1024),
    "vg": (2048, 2560, 1024),
    "qg": (3072, 1536, 512),
    "kg": (3584, 2048, 512),
    "ka": (4096, 1024, 256),
    "va": (4352, 1280, 256),
}

VMEM_LIMIT = 56 * 1024 * 1024
MESH = pl.DeviceIdType.MESH


def _params(semantics=None, vmem=None):
    return pltpu.CompilerParams(dimension_semantics=semantics, vmem_limit_bytes=vmem)


_DIMS = {
    "nn": (((1,), (0,)), ((), ())),
    "nt": (((1,), (1,)), ((), ())),
    "tn": (((0,), (0,)), ((), ())),
}


def _mxu(a, b, mode):
    return lax.dot_general(a.astype(BF16), b.astype(BF16), _DIMS[mode], preferred_element_type=F32)


@functools.partial(jax.custom_vjp, nondiff_argnums=(2,))
def bdot(a, b, mode):
    return _mxu(a, b, mode)


def _bdot_fwd(a, b, mode):
    return _mxu(a, b, mode), (a, b)


def _bdot_bwd(mode, res, g):
    a, b = res
    if mode == "nn":
        return _mxu(g, b, "nt"), _mxu(a, g, "tn")
    if mode == "nt":
        return _mxu(g, b, "nn"), _mxu(g, a, "tn")
    return _mxu(b, g, "nt"), _mxu(a, g, "nn")


bdot.defvjp(_bdot_fwd, _bdot_bwd)


def _rms(x, g):
    return x * lax.rsqrt(jnp.mean(x * x, axis=-1, keepdims=True) + NORM_EPS) * g


def _rope(x, cos, sin_signed):
    return x * cos + pltpu.roll(x, HEAD_DIM // 2, 1) * sin_signed


def _rope_transposed(d, cos, sin_signed):
    return d * cos + pltpu.roll(d * sin_signed, HEAD_DIM // 2, 1)


def _silu(x):
    return x * jax.nn.sigmoid(x)


def _log_sigmoid(z):
    return -(jnp.maximum(-z, 0.0) + jnp.log(1.0 + jnp.exp(-jnp.abs(z))))


def _matmul(a, b, mode, *, tm, tn, tk, out_dtype=F32, res=None, name):
    if mode == "nn":
        (m, k), (k2, n) = a.shape, b.shape
    elif mode == "nt":
        (m, k), (n, k2) = a.shape, b.shape
    else:
        (k, m), (k2, n) = a.shape, b.shape
    assert k == k2 and m % tm == 0 and n % tn == 0 and k % tk == 0, (name, a.shape, b.shape, tm, tn, tk)
    nk = k // tk
    if mode == "tn":
        a_spec = pl.BlockSpec((tk, tm), lambda i, j, kk: (kk, i))
    else:
        a_spec = pl.BlockSpec((tm, tk), lambda i, j, kk: (i, kk))
    if mode == "nt":
        b_spec = pl.BlockSpec((tn, tk), lambda i, j, kk: (j, kk))
    else:
        b_spec = pl.BlockSpec((tk, tn), lambda i, j, kk: (kk, j))
    o_spec = pl.BlockSpec((tm, tn), lambda i, j, kk: (i, j))
    in_specs = [a_spec, b_spec]
    args = [a, b]
    if res is not None:
        in_specs.append(o_spec)
        args.append(res)
    dims = _DIMS[mode]

    def body(*refs):
        if res is not None:
            a_ref, b_ref, r_ref, o_ref = refs[:4]
            rest = refs[4:]
        else:
            a_ref, b_ref, o_ref = refs[:3]
            r_ref = None
            rest = refs[3:]
        part = lax.dot_general(a_ref[...], b_ref[...], dims, preferred_element_type=F32)

        def finish(acc):
            if r_ref is not None:
                acc = acc + r_ref[...]
            o_ref[...] = acc.astype(out_dtype)

        if nk == 1:
            finish(part)
        else:
            acc_ref = rest[0]
            kk = pl.program_id(2)

            @pl.when(kk == 0)
            def _():
                acc_ref[...] = part

            @pl.when(kk > 0)
            def _():
                acc_ref[...] += part

            @pl.when(kk == nk - 1)
            def _():
                finish(acc_ref[...])

    return pl.pallas_call(
        body,
        name=name,
        grid=(m // tm, n // tn, nk),
        in_specs=in_specs,
        out_specs=o_spec,
        out_shape=jax.ShapeDtypeStruct((m, n), out_dtype),
        scratch_shapes=[pltpu.VMEM((tm, tn), F32)] if nk > 1 else [],
        compiler_params=_params(("parallel", "parallel", "arbitrary"), VMEM_LIMIT),
    )(*args)


def _rmsnorm_fwd(x, g, *, name, tr=512):
    t, d = x.shape

    def body(x_ref, g_ref, h_ref):
        h_ref[...] = _rms(x_ref[...], g_ref[...]).astype(BF16)

    return pl.pallas_call(
        body, name=name, grid=(t // tr,),
        in_specs=[pl.BlockSpec((tr, d), lambda i: (i, 0)), pl.BlockSpec((1, d), lambda i: (0, 0))],
        out_specs=pl.BlockSpec((tr, d), lambda i: (i, 0)),
        out_shape=jax.ShapeDtypeStruct((t, d), BF16),
        compiler_params=_params(("parallel",), VMEM_LIMIT),
    )(x, g)


def _rmsnorm_bwd(x, g, dh, dres, *, name, tr=256):
    t, d = x.shape

    def body(x_ref, g_ref, dh_ref, dres_ref, dx_ref, dxb_ref, dg_ref):
        _, vjp = jax.vjp(_rms, x_ref[...], g_ref[...])
        dx, dg = vjp(dh_ref[...])
        dx = dx + dres_ref[...]
        dx_ref[...] = dx
        dxb_ref[...] = dx.astype(BF16)

        @pl.when(pl.program_id(0) == 0)
        def _():
            dg_ref[...] = jnp.zeros_like(dg_ref)

        dg_ref[...] += dg

    row = pl.BlockSpec((tr, d), lambda i: (i, 0))
    vec = pl.BlockSpec((1, d), lambda i: (0, 0))
    return pl.pallas_call(
        body, name=name, grid=(t // tr,),
        in_specs=[row, vec, row, row],
        out_specs=[row, row, vec],
        out_shape=[jax.ShapeDtypeStruct((t, d), F32), jax.ShapeDtypeStruct((t, d), BF16),
                   jax.ShapeDtypeStruct((1, d), F32)],
        compiler_params=_params(("arbitrary",), VMEM_LIMIT),
    )(x, g, dh, dres)


def _seg_block(name, width):
    off = SEGMENTS[name][0]
    assert off % width == 0
    return off // width


def _attn_prep_fwd(proj, cos, sin_signed, gq, gk, *, tr=256):
    t = proj.shape[0]

    def body(q_ref, k_ref, v_ref, cos_ref, sin_ref, gq_ref, gk_ref, qo_ref, ko_ref, vo_ref):
        cos_t, sin_t = cos_ref[...], sin_ref[...]
        for h in range(ATTN_HEADS):
            cols = slice(h * HEAD_DIM, (h + 1) * HEAD_DIM)
            qo_ref[:, cols] = _rope(_rms(q_ref[:, cols], gq_ref[...]), cos_t, sin_t).astype(BF16)
        for h in range(KV_HEADS):
            cols = slice(h * HEAD_DIM, (h + 1) * HEAD_DIM)
            ko_ref[:, cols] = _rope(_rms(k_ref[:, cols], gk_ref[...]), cos_t, sin_t).astype(BF16)
        vo_ref[...] = v_ref[...].astype(BF16)

    qb, kb, vb = _seg_block("qa", ATTN_WIDTH), _seg_block("ka", KV_WIDTH), _seg_block("va", KV_WIDTH)
    tab = pl.BlockSpec((tr, HEAD_DIM), lambda i: (i, 0))
    vec = pl.BlockSpec((1, HEAD_DIM), lambda i: (0, 0))
    return pl.pallas_call(
        body, name="attn_prep_fwd", grid=(t // tr,),
        in_specs=[pl.BlockSpec((tr, ATTN_WIDTH), lambda i: (i, qb)),
                  pl.BlockSpec((tr, KV_WIDTH), lambda i: (i, kb)),
                  pl.BlockSpec((tr, KV_WIDTH), lambda i: (i, vb)),
                  tab, tab, vec, vec],
        out_specs=[pl.BlockSpec((tr, ATTN_WIDTH), lambda i: (i, 0)),
                   pl.BlockSpec((tr, KV_WIDTH), lambda i: (i, 0)),
                   pl.BlockSpec((tr, KV_WIDTH), lambda i: (i, 0))],
        out_shape=[jax.ShapeDtypeStruct((t, ATTN_WIDTH), BF16),
                   jax.ShapeDtypeStruct((t, KV_WIDTH), BF16),
                   jax.ShapeDtypeStruct((t, KV_WIDTH), BF16)],
        compiler_params=_params(("parallel",), VMEM_LIMIT),
    )(proj, proj, proj, cos, sin_signed, gq, gk)


def _attn_head(q, kcat, vcat, sink_col, valid):
    s = bdot(q, kcat, "nt") * (HEAD_DIM ** -0.5)
    s = jnp.where(valid, s, -jnp.inf)
    m = lax.stop_gradient(jnp.maximum(jnp.max(s, axis=-1, keepdims=True), sink_col))
    p = jnp.exp(s - m)
    p = p / (jnp.sum(p, axis=-1, keepdims=True) + jnp.exp(sink_col - m))
    return bdot(p, vcat, "nn")


def _attn_valid(n, t):
    qi = lax.broadcasted_iota(jnp.int32, (ATTN_BLOCK, 3 * ATTN_BLOCK), 0)
    sj = lax.broadcasted_iota(jnp.int32, (ATTN_BLOCK, 3 * ATTN_BLOCK), 1)
    kpos = n * ATTN_BLOCK - ATTN_BLOCK + sj
    return (jnp.abs(sj - ATTN_BLOCK - qi) <= WINDOW) & (kpos >= 0) & (kpos < t)


def _attn_specs(nb):
    q_spec = pl.BlockSpec((ATTN_BLOCK, GQA_GROUP * HEAD_DIM), lambda h, n: (n, h))
    kv_specs = [
        pl.BlockSpec((ATTN_BLOCK, HEAD_DIM), lambda h, n: (jnp.maximum(n - 1, 0), h)),
        pl.BlockSpec((ATTN_BLOCK, HEAD_DIM), lambda h, n: (n, h)),
        pl.BlockSpec((ATTN_BLOCK, HEAD_DIM), lambda h, n: (jnp.minimum(n + 1, nb - 1), h)),
    ]
    return q_spec, kv_specs


def _attn_fwd(q, k, v, sink):
    t = q.shape[0]
    nb = t // ATTN_BLOCK

    def body(sink_ref, q_ref, kp_ref, kc_ref, kn_ref, vp_ref, vc_ref, vn_ref, o_ref):
        h, n = pl.program_id(0), pl.program_id(1)
        valid = _attn_valid(n, t)
        kcat = jnp.concatenate([kp_ref[...], kc_ref[...], kn_ref[...]], axis=0).astype(F32)
        vcat = jnp.concatenate([vp_ref[...], vc_ref[...], vn_ref[...]], axis=0).astype(F32)
        for g in range(GQA_GROUP):
            cols = slice(g * HEAD_DIM, (g + 1) * HEAD_DIM)
            sink_col = jnp.full((ATTN_BLOCK, 1), sink_ref[h * GQA_GROUP + g], F32)
            o_ref[:, cols] = _attn_head(q_ref[:, cols].astype(F32), kcat, vcat, sink_col, valid).astype(BF16)

    q_spec, kv_specs = _attn_specs(nb)
    return pl.pallas_call(
        body, name="attn_fwd", grid=(KV_HEADS, nb),
        in_specs=[pl.BlockSpec(memory_space=pltpu.SMEM), q_spec] + kv_specs + kv_specs,
        out_specs=q_spec,
        out_shape=jax.ShapeDtypeStruct((t, ATTN_WIDTH), BF16),
        compiler_params=_params(("parallel", "parallel"), VMEM_LIMIT),
    )(sink, q, k, k, k, v, v, v)


def _attn_bwd(q, k, v, sink, dmix):
    t = q.shape[0]
    nb = t // ATTN_BLOCK

    def body(sink_ref, q_ref, kp_ref, kc_ref, kn_ref, vp_ref, vc_ref, vn_ref, do_ref,
             dq_ref, dk_lo, dk_mid, dk_hi, dv_lo, dv_mid, dv_hi, dsink_ref):
        h, n = pl.program_id(0), pl.program_id(1)
        valid = _attn_valid(n, t)
        kcat = jnp.concatenate([kp_ref[...], kc_ref[...], kn_ref[...]], axis=0).astype(F32)
        vcat = jnp.concatenate([vp_ref[...], vc_ref[...], vn_ref[...]], axis=0).astype(F32)
        dk = jnp.zeros((3 * ATTN_BLOCK, HEAD_DIM), F32)
        dv = jnp.zeros((3 * ATTN_BLOCK, HEAD_DIM), F32)
        row = lax.broadcasted_iota(jnp.int32, (8, HEAD_DIM), 0)
        dsink = jnp.zeros((8, HEAD_DIM), F32)
        for g in range(GQA_GROUP):
            cols = slice(g * HEAD_DIM, (g + 1) * HEAD_DIM)
            sink_col = jnp.full((ATTN_BLOCK, 1), sink_ref[h * GQA_GROUP + g], F32)
            _, vjp = jax.vjp(functools.partial(_attn_head, valid=valid),
                             q_ref[:, cols].astype(F32), kcat, vcat, sink_col)
            dq_g, dk_g, dv_g, dsink_g = vjp(do_ref[:, cols])
            dq_ref[:, cols] = dq_g
            dk, dv = dk + dk_g, dv + dv_g
            dsink = dsink + jnp.where(row == g, jnp.sum(dsink_g), 0.0)
        for i, (dk_ref, dv_ref) in enumerate(((dk_lo, dv_lo), (dk_mid, dv_mid), (dk_hi, dv_hi))):
            rows = slice(i * ATTN_BLOCK, (i + 1) * ATTN_BLOCK)
            dk_ref[...] = dk[rows]
            dv_ref[...] = dv[rows]

        @pl.when(n == 0)
        def _():
            dsink_ref[...] = jnp.zeros_like(dsink_ref)

        dsink_ref[...] += dsink

    q_spec, kv_specs = _attn_specs(nb)
    kv_out = pl.BlockSpec((ATTN_BLOCK, HEAD_DIM), lambda h, n: (n, h))
    kv_shape = jax.ShapeDtypeStruct((t, KV_WIDTH), F32)
    return pl.pallas_call(
        body, name="attn_bwd", grid=(KV_HEADS, nb),
        in_specs=[pl.BlockSpec(memory_space=pltpu.SMEM), q_spec] + kv_specs + kv_specs + [q_spec],
        out_specs=[q_spec] + [kv_out] * 6 + [pl.BlockSpec((None, 8, HEAD_DIM), lambda h, n: (h, 0, 0))],
        out_shape=[jax.ShapeDtypeStruct((t, ATTN_WIDTH), F32)] + [kv_shape] * 6
                  + [jax.ShapeDtypeStruct((KV_HEADS, 8, HEAD_DIM), F32)],
        compiler_params=_params(("parallel", "arbitrary"), VMEM_LIMIT),
    )(sink, q, k, k, k, v, v, v, dmix)


def _attn_prep_bwd(proj, cos, sin_signed, gq, gk, dq, dks, dvs):
    t = proj.shape[0]
    tr = ATTN_BLOCK
    nb = t // tr

    def body(q_ref, k_ref, cos_ref, sin_ref, gq_ref, gk_ref, dq_ref,
             dk_lo, dk_mid, dk_hi, dv_lo, dv_mid, dv_hi,
             dqo_ref, dko_ref, dvo_ref, dgq_ref, dgk_ref):
        n = pl.program_id(0)
        cos_t, sin_t = cos_ref[...], sin_ref[...]
        has_next = (n < nb - 1).astype(F32)
        has_prev = (n > 0).astype(F32)
        dk = dk_lo[...] * has_next + dk_mid[...] + dk_hi[...] * has_prev
        dv = dv_lo[...] * has_next + dv_mid[...] + dv_hi[...] * has_prev
        dvo_ref[...] = dv.astype(BF16)
        dgq = jnp.zeros((1, HEAD_DIM), F32)
        dgk = jnp.zeros((1, HEAD_DIM), F32)
        for h in range(ATTN_HEADS):
            cols = slice(h * HEAD_DIM, (h + 1) * HEAD_DIM)
            _, vjp = jax.vjp(_rms, q_ref[:, cols], gq_ref[...])
            dx, dg = vjp(_rope_transposed(dq_ref[:, cols], cos_t, sin_t))
            dqo_ref[:, cols] = dx.astype(BF16)
            dgq = dgq + dg
        for h in range(KV_HEADS):
            cols = slice(h * HEAD_DIM, (h + 1) * HEAD_DIM)
            _, vjp = jax.vjp(_rms, k_ref[:, cols], gk_ref[...])
            dx, dg = vjp(_rope_transposed(dk[:, cols], cos_t, sin_t))
            dko_ref[:, cols] = dx.astype(BF16)
            dgk = dgk + dg

        @pl.when(n == 0)
        def _():
            dgq_ref[...] = jnp.zeros_like(dgq_ref)
            dgk_ref[...] = jnp.zeros_like(dgk_ref)

        dgq_ref[...] += dgq
        dgk_ref[...] += dgk

    qb, kb = _seg_block("qa", ATTN_WIDTH), _seg_block("ka", KV_WIDTH)
    tab = pl.BlockSpec((tr, HEAD_DIM), lambda i: (i, 0))
    vec = pl.BlockSpec((1, HEAD_DIM), lambda i: (0, 0))
    kv = [pl.BlockSpec((tr, KV_WIDTH), lambda i: (jnp.minimum(i + 1, nb - 1), 0)),
          pl.BlockSpec((tr, KV_WIDTH), lambda i: (i, 0)),
          pl.BlockSpec((tr, KV_WIDTH), lambda i: (jnp.maximum(i - 1, 0), 0))]
    wide = pl.BlockSpec((tr, ATTN_WIDTH), lambda i: (i, 0))
    narrow = pl.BlockSpec((tr, KV_WIDTH), lambda i: (i, 0))
    return pl.pallas_call(
        body, name="attn_prep_bwd", grid=(nb,),
        in_specs=[pl.BlockSpec((tr, ATTN_WIDTH), lambda i: (i, qb)),
                  pl.BlockSpec((tr, KV_WIDTH), lambda i: (i, kb)),
                  tab, tab, vec, vec, wide] + kv + kv,
        out_specs=[wide, narrow, narrow, vec, vec],
        out_shape=[jax.ShapeDtypeStruct((t, ATTN_WIDTH), BF16),
                   jax.ShapeDtypeStruct((t, KV_WIDTH), BF16),
                   jax.ShapeDtypeStruct((t, KV_WIDTH), BF16),
                   jax.ShapeDtypeStruct((1, HEAD_DIM), F32),
                   jax.ShapeDtypeStruct((1, HEAD_DIM), F32)],
        compiler_params=_params(("arbitrary",), VMEM_LIMIT),
    )(proj, proj, cos, sin_signed, gq, gk, dq, *dks, *dvs)


def _decay_fn(lr, w2, ba):
    return _log_sigmoid(bdot(lr, w2, "nn") + ba) / GLA_GATE_NORMALIZER


def _gla_prep_fwd(proj_lr, w2, ba2, *, tr=512):
    t = proj_lr.shape[0]
    width = 2 * GLA_KEY_WIDTH

    def body(lr_ref, w2_ref, ba_ref, g_ref):
        g_ref[...] = _decay_fn(lr_ref[...], w2_ref[...], ba_ref[...])

    return pl.pallas_call(
        body, name="gla_prep_fwd", grid=(t // tr,),
        in_specs=[pl.BlockSpec((tr, LR_PAD), lambda i: (i, 0)),
                  pl.BlockSpec((LR_PAD, width), lambda i: (0, 0)),
                  pl.BlockSpec((1, width), lambda i: (0, 0))],
        out_specs=pl.BlockSpec((tr, width), lambda i: (i, 0)),
        out_shape=jax.ShapeDtypeStruct((t, width), F32),
        compiler_params=_params(("parallel",), VMEM_LIMIT),
    )(proj_lr, w2, ba2)


def _gla_prep_bwd(proj_lr, w2, ba2, dg_f, dg_b, *, tr=512):
    t = proj_lr.shape[0]
    width = 2 * GLA_KEY_WIDTH

    def body(lr_ref, w2_ref, ba_ref, dgf_ref, dgb_ref, dlr_ref, dw2_ref, dba_ref):
        _, vjp = jax.vjp(_decay_fn, lr_ref[...], w2_ref[...], ba_ref[...])
        dlr, dw2, dba = vjp(jnp.concatenate([dgf_ref[...], dgb_ref[...]], axis=1))
        dlr_ref[...] = dlr.astype(BF16)

        @pl.when(pl.program_id(0) == 0)
        def _():
            dw2_ref[...] = jnp.zeros_like(dw2_ref)
            dba_ref[...] = jnp.zeros_like(dba_ref)

        dw2_ref[...] += dw2
        dba_ref[...] += dba

    half = pl.BlockSpec((tr, GLA_KEY_WIDTH), lambda i: (i, 0))
    return pl.pallas_call(
        body, name="gla_prep_bwd", grid=(t // tr,),
        in_specs=[pl.BlockSpec((tr, LR_PAD), lambda i: (i, 0)),
                  pl.BlockSpec((LR_PAD, width), lambda i: (0, 0)),
                  pl.BlockSpec((1, width), lambda i: (0, 0)), half, half],
        out_specs=[pl.BlockSpec((tr, LR_PAD), lambda i: (i, 0)),
                   pl.BlockSpec((LR_PAD, width), lambda i: (0, 0)),
                   pl.BlockSpec((1, width), lambda i: (0, 0))],
        out_shape=[jax.ShapeDtypeStruct((t, LR_PAD), BF16),
                   jax.ShapeDtypeStruct((LR_PAD, width), F32),
                   jax.ShapeDtypeStruct((1, width), F32)],
        compiler_params=_params(("arbitrary",), VMEM_LIMIT),
    )(proj_lr, w2, ba2, dg_f, dg_b)


def _gla_chunk(q, k, v, g, state, forward):
    c = GLA_CHUNK
    row = lax.broadcasted_iota(jnp.int32, (c, c), 0)
    col = lax.broadcasted_iota(jnp.int32, (c, c), 1)
    rid = lax.broadcasted_iota(jnp.int32, (c, GLA_DK), 0)
    q = q * (GLA_DK ** -0.5)
    if forward:
        cum = (row >= col).astype(F32)
        see = row >= col
        upto_ref = rid <= c // 2
    else:
        cum = (row <= col).astype(F32)
        see = row < col
        upto_ref = rid >= c - 1 - c // 2
    b = lax.dot_general(cum, g, _DIMS["nn"], precision=lax.Precision.HIGHEST, preferred_element_type=F32)
    b_last = jnp.sum(g, axis=0, keepdims=True)
    b_ref = jnp.sum(jnp.where(upto_ref, g, 0.0), axis=0, keepdims=True)
    a = bdot(q * jnp.exp(b - b_ref), k * jnp.exp(b_ref - b), "nt")
    a = jnp.where(see, a, 0.0)
    o = bdot(a, v, "nn") + bdot(q * jnp.exp(b), state, "nt")
    new_state = state * jnp.exp(b_last) + bdot(v, k * jnp.exp(b_last - b), "tn")
    return o, new_state


def _gla_fwd(proj, g):
    t = proj.shape[0]
    c = GLA_CHUNK
    nchunk = t // c
    qb, kb, vb = _seg_block("qg", GLA_KEY_WIDTH), _seg_block("kg", GLA_KEY_WIDTH), _seg_block("vg", GLA_WIDTH)

    def body(qf, kf, vf, gf, qr, kr, vr, gr, of_ref, ob_ref, sf_ref, sb_ref, state):
        @pl.when(pl.program_id(0) == 0)
        def _():
            state[...] = jnp.zeros_like(state)

        for d, (q_ref, k_ref, v_ref, g_ref, o_ref, s_ref) in enumerate(
                ((qf, kf, vf, gf, of_ref, sf_ref), (qr, kr, vr, gr, ob_ref, sb_ref))):
            for h in range(GLA_HEADS):
                kc = slice(h * GLA_DK, (h + 1) * GLA_DK)
                vc = slice(h * GLA_DV, (h + 1) * GLA_DV)
                s_in = state[d, h]
                s_ref[h] = s_in
                o, s_out = _gla_chunk(q_ref[:, kc], k_ref[:, kc], v_ref[:, vc], g_ref[:, kc], s_in, d == 0)
                o_ref[:, vc] = o
                state[d, h] = s_out

    specs, outs = [], []
    for d in range(2):
        ci = (lambda i: i) if d == 0 else (lambda i: nchunk - 1 - i)
        specs += [pl.BlockSpec((c, GLA_KEY_WIDTH), lambda i, ci=ci: (ci(i), qb)),
                  pl.BlockSpec((c, GLA_KEY_WIDTH), lambda i, ci=ci: (ci(i), kb)),
                  pl.BlockSpec((c, GLA_WIDTH), lambda i, ci=ci: (ci(i), vb)),
                  pl.BlockSpec((c, GLA_KEY_WIDTH), lambda i, ci=ci, d=d: (ci(i), d))]
        outs.append(pl.BlockSpec((c, GLA_WIDTH), lambda i, ci=ci: (ci(i), 0)))
    for d in range(2):
        ci = (lambda i: i) if d == 0 else (lambda i: nchunk - 1 - i)
        outs.append(pl.BlockSpec((None, GLA_HEADS, GLA_DV, GLA_DK), lambda i, ci=ci: (ci(i), 0, 0, 0)))
    o_shape = jax.ShapeDtypeStruct((t, GLA_WIDTH), F32)
    s_shape = jax.ShapeDtypeStruct((nchunk, GLA_HEADS, GLA_DV, GLA_DK), F32)
    return pl.pallas_call(
        body, name="gla_fwd", grid=(nchunk,),
        in_specs=specs, out_specs=outs,
        out_shape=[o_shape, o_shape, s_shape, s_shape],
        scratch_shapes=[pltpu.VMEM((2, GLA_HEADS, GLA_DV, GLA_DK), F32)],
        compiler_params=_params(("arbitrary",), VMEM_LIMIT),
    )(proj, proj, proj, g, proj, proj, proj, g)


def _gla_bwd(proj, g, s_f, s_b, do):
    t = proj.shape[0]
    c = GLA_CHUNK
    nchunk = t // c
    qb, kb, vb = _seg_block("qg", GLA_KEY_WIDTH), _seg_block("kg", GLA_KEY_WIDTH), _seg_block("vg", GLA_WIDTH)

    def body(*refs):
        ins, outs, dstate = refs[:12], refs[12:20], refs[20]

        @pl.when(pl.program_id(0) == 0)
        def _():
            dstate[...] = jnp.zeros_like(dstate)

        for d in range(2):
            q_ref, k_ref, v_ref, g_ref, s_ref, do_ref = ins[6 * d:6 * d + 6]
            dq_ref, dk_ref, dv_ref, dg_ref = outs[4 * d:4 * d + 4]
            for h in range(GLA_HEADS):
                kc = slice(h * GLA_DK, (h + 1) * GLA_DK)
                vc = slice(h * GLA_DV, (h + 1) * GLA_DV)
                _, vjp = jax.vjp(functools.partial(_gla_chunk, forward=(d == 0)),
                                 q_ref[:, kc], k_ref[:, kc], v_ref[:, vc], g_ref[:, kc], s_ref[h])
                dq, dk, dv, dg, ds = vjp((do_ref[:, vc], dstate[d, h]))
                dq_ref[:, kc] = dq
                dk_ref[:, kc] = dk
                dv_ref[:, vc] = dv
                dg_ref[:, kc] = dg
                dstate[d, h] = ds

    specs, outs, shapes = [], [], []
    for d in range(2):
        ci = (lambda i: nchunk - 1 - i) if d == 0 else (lambda i: i)
        specs += [pl.BlockSpec((c, GLA_KEY_WIDTH), lambda i, ci=ci: (ci(i), qb)),
                  pl.BlockSpec((c, GLA_KEY_WIDTH), lambda i, ci=ci: (ci(i), kb)),
                  pl.BlockSpec((c, GLA_WIDTH), lambda i, ci=ci: (ci(i), vb)),
                  pl.BlockSpec((c, GLA_KEY_WIDTH), lambda i, ci=ci, d=d: (ci(i), d)),
                  pl.BlockSpec((None, GLA_HEADS, GLA_DV, GLA_DK), lambda i, ci=ci: (ci(i), 0, 0, 0)),
                  pl.BlockSpec((c, GLA_WIDTH), lambda i, ci=ci: (ci(i), 0))]
        key = pl.BlockSpec((c, GLA_KEY_WIDTH), lambda i, ci=ci: (ci(i), 0))
        val = pl.BlockSpec((c, GLA_WIDTH), lambda i, ci=ci: (ci(i), 0))
        outs += [key, key, val, key]
        shapes += [jax.ShapeDtypeStruct((t, GLA_KEY_WIDTH), F32), jax.ShapeDtypeStruct((t, GLA_KEY_WIDTH), F32),
                   jax.ShapeDtypeStruct((t, GLA_WIDTH), F32), jax.ShapeDtypeStruct((t, GLA_KEY_WIDTH), F32)]
    return pl.pallas_call(
        body, name="gla_bwd", grid=(nchunk,),
        in_specs=specs, out_specs=outs, out_shape=shapes,
        scratch_shapes=[pltpu.VMEM((2, GLA_HEADS, GLA_DV, GLA_DK), F32)],
        compiler_params=_params(("arbitrary",), VMEM_LIMIT),
    )(proj, proj, proj, g, s_f, do, proj, proj, proj, g, s_b, do)


def _gla_out_head(o_f, o_b, gate, gn):
    return _rms(o_f + o_b, gn) * _silu(gate)


def _gla_out_fwd(o_f, o_b, proj, gn, *, tr=256):
    t = o_f.shape[0]
    gb = _seg_block("gate", GLA_WIDTH)

    def body(of_ref, ob_ref, gate_ref, gn_ref, out_ref):
        for h in range(GLA_HEADS):
            vc = slice(h * GLA_DV, (h + 1) * GLA_DV)
            out_ref[:, vc] = _gla_out_head(of_ref[:, vc], ob_ref[:, vc], gate_ref[:, vc], gn_ref[...]).astype(BF16)

    wide = pl.BlockSpec((tr, GLA_WIDTH), lambda i: (i, 0))
    return pl.pallas_call(
        body, name="gla_out_fwd", grid=(t // tr,),
        in_specs=[wide, wide, pl.BlockSpec((tr, GLA_WIDTH), lambda i: (i, gb)),
                  pl.BlockSpec((1, GLA_DV), lambda i: (0, 0))],
        out_specs=wide,
        out_shape=jax.ShapeDtypeStruct((t, GLA_WIDTH), BF16),
        compiler_params=_params(("parallel",), VMEM_LIMIT),
    )(o_f, o_b, proj, gn)


def _gla_out_bwd(o_f, o_b, proj, gn, dmix, *, tr=256):
    t = o_f.shape[0]
    gb = _seg_block("gate", GLA_WIDTH)

    def body(of_ref, ob_ref, gate_ref, gn_ref, dout_ref, do_ref, dgate_ref, dgn_ref):
        dgn = jnp.zeros((1, GLA_DV), F32)
        for h in range(GLA_HEADS):
            vc = slice(h * GLA_DV, (h + 1) * GLA_DV)
            _, vjp = jax.vjp(_gla_out_head, of_ref[:, vc], ob_ref[:, vc], gate_ref[:, vc], gn_ref[...])
            do, _, dgate, dg = vjp(dout_ref[:, vc])
            do_ref[:, vc] = do
            dgate_ref[:, vc] = dgate.astype(BF16)
            dgn = dgn + dg

        @pl.when(pl.program_id(0) == 0)
        def _():
            dgn_ref[...] = jnp.zeros_like(dgn_ref)

        dgn_ref[...] += dgn

    wide = pl.BlockSpec((tr, GLA_WIDTH), lambda i: (i, 0))
    vec = pl.BlockSpec((1, GLA_DV), lambda i: (0, 0))
    return pl.pallas_call(
        body, name="gla_out_bwd", grid=(t // tr,),
        in_specs=[wide, wide, pl.BlockSpec((tr, GLA_WIDTH), lambda i: (i, gb)), vec,
                  pl.BlockSpec((tr, GLA_WIDTH), lambda i: (i, 1))],
        out_specs=[wide, wide, vec],
        out_shape=[jax.ShapeDtypeStruct((t, GLA_WIDTH), F32), jax.ShapeDtypeStruct((t, GLA_WIDTH), BF16),
                   jax.ShapeDtypeStruct((1, GLA_DV), F32)],
        compiler_params=_params(("arbitrary",), VMEM_LIMIT),
    )(o_f, o_b, proj, gn, dmix)


CONV_TR = 512
CONV_TC = 256
HALO = 8
HALO16 = 16


def _conv3(u, w, b):
    n = u.shape[0]
    return pltpu.roll(u, 1, 0) * w[0:1] + u * w[1:2] + pltpu.roll(u, n - 1, 0) * w[2:3] + b


def _conv_ext(main_ref, prev_ref, next_ref, r, nr):
    prev = prev_ref[...].astype(F32)[-HALO:] * (r > 0).astype(F32)
    nxt = next_ref[...].astype(F32)[:HALO] * (r < nr - 1).astype(F32)
    return jnp.concatenate([prev, main_ref[...].astype(F32), nxt], axis=0)


def _conv_specs(t, halo, col_of):
    per = CONV_TR // halo
    last = t // halo - 1
    return [pl.BlockSpec((CONV_TR, CONV_TC), lambda j, r: (r, col_of(j))),
            pl.BlockSpec((halo, CONV_TC), lambda j, r: (jnp.maximum(r * per - 1, 0), col_of(j))),
            pl.BlockSpec((halo, CONV_TC), lambda j, r: (jnp.minimum((r + 1) * per, last), col_of(j)))]


def _ffn_mid_fwd(u_g, u_v, cw_g, cw_v, cb_g, cb_v):
    t, f = u_g.shape
    nr = t // CONV_TR

    def body(ug, ugp, ugn, uv, uvp, uvn, wg, wv, bg, bv, a_ref):
        r = pl.program_id(1)
        gate = _conv3(_conv_ext(ug, ugp, ugn, r, nr), wg[...], bg[...])[HALO:HALO + CONV_TR]
        val = _conv3(_conv_ext(uv, uvp, uvn, r, nr), wv[...], bv[...])[HALO:HALO + CONV_TR]
        a_ref[...] = (_silu(gate) * val).astype(BF16)

    same = lambda j: j
    w_spec = pl.BlockSpec((3, CONV_TC), lambda j, r: (0, j))
    b_spec = pl.BlockSpec((1, CONV_TC), lambda j, r: (0, j))
    return pl.pallas_call(
        body, name="ffn_mid_fwd", grid=(f // CONV_TC, nr),
        in_specs=_conv_specs(t, HALO, same) + _conv_specs(t, HALO, same) + [w_spec, w_spec, b_spec, b_spec],
        out_specs=pl.BlockSpec((CONV_TR, CONV_TC), lambda j, r: (r, j)),
        out_shape=jax.ShapeDtypeStruct((t, f), BF16),
        compiler_params=_params(("parallel", "parallel"), VMEM_LIMIT),
    )(u_g, u_g, u_g, u_v, u_v, u_v, cw_g, cw_v, cb_g, cb_v)


def _ffn_mid_bwd(u_g, u_v, cw_g, cw_v, cb_g, cb_v, da):
    t, f = u_g.shape
    nr = t // CONV_TR
    ext = CONV_TR + 2 * HALO

    def body(ug, ugp, ugn, uv, uvp, uvn, dam, dap, dan, wg, wv, bg, bv,
             dug_ref, duv_ref, dwg_ref, dwv_ref, dbg_ref, dbv_ref):
        r = pl.program_id(1)
        xg = _conv_ext(ug, ugp, ugn, r, nr)
        xv = _conv_ext(uv, uvp, uvn, r, nr)
        da_x = _conv_ext(dam, dap, dan, r, nr)
        gate = _conv3(xg, wg[...], bg[...])
        val = _conv3(xv, wv[...], bv[...])
        pos = r * CONV_TR - HALO + lax.broadcasted_iota(jnp.int32, (ext, CONV_TC), 0)
        inside = (pos >= 0) & (pos < t)
        sig = jax.nn.sigmoid(gate)
        silu = gate * sig
        d_val = jnp.where(inside, da_x * silu, 0.0)
        d_gate = jnp.where(inside, da_x * val * (sig + silu * (1.0 - sig)), 0.0)
        own = lax.broadcasted_iota(jnp.int32, (ext, CONV_TC), 0)
        own = (own >= HALO) & (own < HALO + CONV_TR)
        for x, d, w, du_ref, dw_ref, db_ref in ((xg, d_gate, wg, dug_ref, dwg_ref, dbg_ref),
                                                (xv, d_val, wv, duv_ref, dwv_ref, dbv_ref)):
            wt = w[...]
            du = pltpu.roll(d, ext - 1, 0) * wt[0:1] + d * wt[1:2] + pltpu.roll(d, 1, 0) * wt[2:3]
            du_ref[...] = du[HALO:HALO + CONV_TR].astype(BF16)
            d_own = jnp.where(own, d, 0.0)
            dw = jnp.concatenate([
                jnp.sum(pltpu.roll(x, 1, 0) * d_own, axis=0, keepdims=True),
                jnp.sum(x * d_own, axis=0, keepdims=True),
                jnp.sum(pltpu.roll(x, ext - 1, 0) * d_own, axis=0, keepdims=True)], axis=0)
            db = jnp.sum(d_own, axis=0, keepdims=True)

            @pl.when(r == 0)
            def _():
                dw_ref[...] = jnp.zeros_like(dw_ref)
                db_ref[...] = jnp.zeros_like(db_ref)

            dw_ref[...] += dw
            db_ref[...] += db

    same = lambda j: j
    w_spec = pl.BlockSpec((3, CONV_TC), lambda j, r: (0, j))
    b_spec = pl.BlockSpec((1, CONV_TC), lambda j, r: (0, j))
    tile = pl.BlockSpec((CONV_TR, CONV_TC), lambda j, r: (r, j))
    return pl.pallas_call(
        body, name="ffn_mid_bwd", grid=(f // CONV_TC, nr),
        in_specs=(_conv_specs(t, HALO, same) + _conv_specs(t, HALO, same) + _conv_specs(t, HALO16, same)
                  + [w_spec, w_spec, b_spec, b_spec]),
        out_specs=[tile, tile, w_spec, w_spec, b_spec, b_spec],
        out_shape=[jax.ShapeDtypeStruct((t, f), BF16), jax.ShapeDtypeStruct((t, f), BF16),
                   jax.ShapeDtypeStruct((3, f), F32), jax.ShapeDtypeStruct((3, f), F32),
                   jax.ShapeDtypeStruct((1, f), F32), jax.ShapeDtypeStruct((1, f), F32)],
        compiler_params=_params(("parallel", "arbitrary"), VMEM_LIMIT),
    )(u_g, u_g, u_g, u_v, u_v, u_v, da, da, da, cw_g, cw_v, cb_g, cb_v)


def _loss_head(y, target, *, tr=256):
    t, d = y.shape

    def body(y_ref, t_ref, loss_ref, dy_ref, dyb_ref):
        err = y_ref[...] - t_ref[...]
        dy = err * (1.0 / d)
        dy_ref[...] = dy
        dyb_ref[...] = dy.astype(BF16)
        part = 0.5 * jnp.sum(jnp.sum(err * err, axis=-1, keepdims=True) * (1.0 / d), axis=0, keepdims=True)

        @pl.when(pl.program_id(0) == 0)
        def _():
            loss_ref[...] = jnp.zeros_like(loss_ref)

        loss_ref[...] += jnp.broadcast_to(part, loss_ref.shape)

    row = pl.BlockSpec((tr, d), lambda i: (i, 0))
    return pl.pallas_call(
        body, name="loss_head", grid=(t // tr,),
        in_specs=[row, row],
        out_specs=[pl.BlockSpec((1, 128), lambda i: (0, 0)), row, row],
        out_shape=[jax.ShapeDtypeStruct((1, 128), F32), jax.ShapeDtypeStruct((t, d), F32),
                   jax.ShapeDtypeStruct((t, d), BF16)],
        compiler_params=_params(("arbitrary",), VMEM_LIMIT),
    )(y, target)


ANY = pl.BlockSpec(memory_space=pl.ANY)


def _position():
    return lax.axis_index("x"), lax.axis_index("y"), lax.axis_index("c")


def _other_chips(x, y):
    return [(1 - x, y), (x, 1 - y), (1 - x, 1 - y)]


def _all_gather(blocks, *, name):
    na = len(blocks)

    def body(*refs):
        ins, outs = refs[:na], refs[na:2 * na]
        send_sems, recv_sems, local_sems = refs[2 * na:]
        x, y, c = _position()
        sibling = (x, y, 1 - c)
        chips = _other_chips(x, y)

        def index(px, py, pc):
            return 4 * px + 2 * py + pc

        def copy(a, k, block, to, src=None):
            dst = outs[a].at[index(*block)]
            return pltpu.make_async_remote_copy(
                src_ref=dst if src is None else src, dst_ref=dst,
                send_sem=send_sems.at[a, k], recv_sem=recv_sems.at[a, k],
                device_id=to, device_id_type=MESH)

        pending = []
        for a in range(na):
            mine = pltpu.make_async_copy(ins[a], outs[a].at[index(x, y, c)], local_sems.at[a])
            mine.start()
            pending.append(mine)
        first = []
        for a in range(na):
            first.append(copy(a, 0, (x, y, c), sibling, src=ins[a]))
            first += [copy(a, 1 + j, (x, y, c), (*chip, c), src=ins[a]) for j, chip in enumerate(chips)]
        for cp in first:
            cp.start()
        passed = []
        for j, chip in enumerate(chips):
            for a in range(na):
                copy(a, 1 + j, (*chip, c), (x, y, c)).wait_recv()
                fwd = copy(a, 4 + j, (*chip, c), sibling)
                fwd.start()
                passed.append(fwd)
        for a in range(na):
            copy(a, 0, sibling, (x, y, c)).wait_recv()
            for j, chip in enumerate(chips):
                copy(a, 4 + j, (*chip, 1 - c), (x, y, c)).wait_recv()
        for cp in first + passed:
            cp.wait_send()
        for cp in pending:
            cp.wait()

    return pl.pallas_call(
        body, name=name,
        in_specs=[ANY] * na, out_specs=[ANY] * na,
        out_shape=[jax.ShapeDtypeStruct((N_DEV,) + b.shape, b.dtype) for b in blocks],
        scratch_shapes=[pltpu.SemaphoreType.DMA((na, 7)), pltpu.SemaphoreType.DMA((na, 7)),
                        pltpu.SemaphoreType.DMA((na,))],
    )(*blocks)


def _swap_with_sibling(grads, *, name):
    na = len(grads)

    def body(*refs):
        ins, outs = refs[:na], refs[na:2 * na]
        send_sems, recv_sems = refs[2 * na:]
        x, y, c = _position()
        copies = []
        for a in range(na):
            for k in range(N_CHIP):
                cp = pltpu.make_async_remote_copy(
                    src_ref=ins[a].at[k, 1 - c], dst_ref=outs[a].at[k],
                    send_sem=send_sems.at[a, k], recv_sem=recv_sems.at[a, k],
                    device_id=(x, y, 1 - c), device_id_type=MESH)
                cp.start()
                copies.append(cp)
        for cp in copies:
            cp.wait()

    return pl.pallas_call(
        body, name=name,
        in_specs=[ANY] * na, out_specs=[ANY] * na,
        out_shape=[jax.ShapeDtypeStruct((N_CHIP,) + g.shape[2:], g.dtype) for g in grads],
        scratch_shapes=[pltpu.SemaphoreType.DMA((na, N_CHIP)), pltpu.SemaphoreType.DMA((na, N_CHIP))],
    )(*grads)


def _pair_sum(grad, theirs, core, *, tr, name):
    _, _, r, w = grad.shape
    assert r % tr == 0

    def body(core_ref, mine_ref, theirs_ref, out_ref):
        out_ref[...] = mine_ref[...] + theirs_ref[...]

    return pl.pallas_call(
        body, name=name,
        grid_spec=pltpu.PrefetchScalarGridSpec(
            num_scalar_prefetch=1, grid=(N_CHIP, r // tr),
            in_specs=[pl.BlockSpec((None, None, tr, w), lambda k, i, core_ref: (k, core_ref[0], i, 0)),
                      pl.BlockSpec((None, tr, w), lambda k, i, core_ref: (k, i, 0))],
            out_specs=pl.BlockSpec((None, tr, w), lambda k, i, core_ref: (k, i, 0))),
        out_shape=jax.ShapeDtypeStruct((N_CHIP, r, w), F32),
        compiler_params=_params(("parallel", "parallel"), VMEM_LIMIT),
    )(core, grad, theirs)


def _exchange_between_chips(parts, *, name):
    na = len(parts)

    def body(*refs):
        ins, outs = refs[:na], refs[na:2 * na]
        send_sems, recv_sems, local_sems = refs[2 * na:]
        x, y, c = _position()
        me = 2 * x + y
        copies = []
        for a in range(na):
            own = pltpu.make_async_copy(ins[a].at[me], outs[a].at[me], local_sems.at[a])
            own.start()
            copies.append(own)
            for j, (px, py) in enumerate(_other_chips(x, y)):
                cp = pltpu.make_async_remote_copy(
                    src_ref=ins[a].at[2 * px + py], dst_ref=outs[a].at[me],
                    send_sem=send_sems.at[a, j], recv_sem=recv_sems.at[a, j],
                    device_id=(px, py, c), device_id_type=MESH)
                cp.start()
                copies.append(cp)
        for cp in copies:
            cp.wait()

    return pl.pallas_call(
        body, name=name,
        in_specs=[ANY] * na, out_specs=[ANY] * na,
        out_shape=[jax.ShapeDtypeStruct(p.shape, p.dtype) for p in parts],
        scratch_shapes=[pltpu.SemaphoreType.DMA((na, 3)), pltpu.SemaphoreType.DMA((na, 3)),
                        pltpu.SemaphoreType.DMA((na,))],
    )(*parts)


def _adamw(parts, w, m, v, *, tr, name):
    n, r, cols = parts.shape
    assert r % tr == 0 and w.shape == (r, cols)
    c1 = 1.0 - ADAM_B1 ** ADAM_STEP
    c2 = 1.0 - ADAM_B2 ** ADAM_STEP

    def body(p_ref, w_ref, m_ref, v_ref, g_ref, d_ref, nm_ref, nv_ref):
        g = p_ref[0]
        for k in range(1, n):
            g = g + p_ref[k]
        new_m = ADAM_B1 * m_ref[...] + (1.0 - ADAM_B1) * g
        new_v = ADAM_B2 * v_ref[...] + (1.0 - ADAM_B2) * (g * g)
        m_hat = new_m / c1
        v_hat = new_v / c2
        g_ref[...] = g
        d_ref[...] = -ADAM_LR * (m_hat / (jnp.sqrt(v_hat) + ADAM_EPS) + ADAM_WD * w_ref[...])
        nm_ref[...] = new_m
        nv_ref[...] = new_v

    tile = pl.BlockSpec((tr, cols), lambda i: (i, 0))
    shape = jax.ShapeDtypeStruct((r, cols), F32)
    return pl.pallas_call(
        body, name=name, grid=(r // tr,),
        in_specs=[pl.BlockSpec((n, tr, cols), lambda i: (0, i, 0)), tile, tile, tile],
        out_specs=[tile] * 4, out_shape=[shape] * 4,
        compiler_params=_params(("parallel",), VMEM_LIMIT),
    )(parts, w, m, v)


def _rope_tables(t):
    half = HEAD_DIM // 2
    inv = 1.0 / (ROPE_THETA ** (jnp.arange(half, dtype=F32) / half))
    ang = jnp.arange(t, dtype=jnp.int32).astype(F32)[:, None] * inv[None, :]
    cos, sin = jnp.cos(ang), jnp.sin(ang)
    return jnp.concatenate([cos, cos], axis=1), jnp.concatenate([-sin, sin], axis=1)


def _to_kernel_columns(w_full):
    order = sorted(SEGMENTS.values())
    main = jnp.concatenate([w_full[:, src:src + width] for _, src, width in order], axis=1)
    lr = jnp.pad(w_full[:, IN_MAIN:IN_TOTAL], ((0, 0), (0, LR_PAD - (IN_TOTAL - IN_MAIN))))
    return main, lr


def _from_kernel_columns(main, lr):
    by_src = sorted(SEGMENTS.values(), key=lambda s: s[1])
    return jnp.concatenate([main[:, dst:dst + width] for dst, _, width in by_src]
                           + [lr[:, :IN_TOTAL - IN_MAIN]], axis=1)


def _pack(pieces):
    flat = []
    for p in pieces:
        p = p.reshape(-1)
        flat.append(jnp.pad(p, (0, (-p.shape[0]) % 128)))
    return jnp.concatenate(flat).reshape(-1, 128)


def _unpack(packed, shapes):
    flat = packed.reshape(-1)
    out, off = [], 0
    for s in shapes:
        size = 1
        for dim in s:
            size *= dim
        out.append(flat[off:off + size].reshape(s))
        off += size + (-size) % 128
    return out


def _local_step(xs, target, norm1_g, w_main, w_lr, gq, gk, attn_sink, w2, ba2, gla_norm_g, w_out_full, norm2_g,
                w_ug, w_uv, cw_g, cw_v, cb_g, cb_v, w_down_full):
    t = xs.shape[0]
    tm = min(1024, t)
    cos, sin_signed = _rope_tables(t)
    sink = attn_sink.reshape(ATTN_HEADS)

    h1 = _rmsnorm_fwd(xs, norm1_g, name="norm1_fwd")
    proj = _matmul(h1, w_main, "nn", tm=tm, tn=512, tk=D_MODEL, name="proj_main")
    proj_lr = _matmul(h1, w_lr, "nn", tm=tm, tn=LR_PAD, tk=D_MODEL, name="proj_lr")
    qa, ka, va = _attn_prep_fwd(proj, cos, sin_signed, gq, gk)
    o_attn = _attn_fwd(qa, ka, va, sink)
    g_dec = _gla_prep_fwd(proj_lr, w2, ba2)
    o_f, o_b, s_f, s_b = _gla_fwd(proj, g_dec)
    o_gla = _gla_out_fwd(o_f, o_b, proj, gla_norm_g)
    mix = jnp.concatenate([o_attn, o_gla], axis=1)
    x1 = _matmul(mix, w_out_full, "nn", tm=tm, tn=512, tk=D_MODEL, res=xs, name="out_proj")
    h2 = _rmsnorm_fwd(x1, norm2_g, name="norm2_fwd")
    u_g = _matmul(h2, w_ug, "nn", tm=tm, tn=512, tk=D_MODEL, name="up_gate")
    u_v = _matmul(h2, w_uv, "nn", tm=tm, tn=512, tk=D_MODEL, name="up_value")
    act = _ffn_mid_fwd(u_g, u_v, cw_g, cw_v, cb_g, cb_v)
    y = _matmul(act, w_down_full, "nn", tm=tm, tn=512, tk=D_FF // 2, res=x1, name="down_proj")
    loss_part, dy, dy_b = _loss_head(y, target)

    d_act = _matmul(dy_b, w_down_full, "nt", tm=tm, tn=512, tk=D_MODEL, out_dtype=BF16, name="d_act")
    dw_down = _matmul(act, dy_b, "tn", tm=512, tn=1024, tk=t, name="dw_down")
    du_g, du_v, dcw_g, dcw_v, dcb_g, dcb_v = _ffn_mid_bwd(u_g, u_v, cw_g, cw_v, cb_g, cb_v, d_act)
    dh2 = _matmul(du_g, w_ug, "nt", tm=tm, tn=512, tk=D_FF // 2, name="dh2_gate")
    dh2 = _matmul(du_v, w_uv, "nt", tm=tm, tn=512, tk=D_FF // 2, res=dh2, name="dh2_value")
    dw_ug = _matmul(h2, du_g, "tn", tm=1024, tn=512, tk=t, name="dw_up_gate")
    dw_uv = _matmul(h2, du_v, "tn", tm=1024, tn=512, tk=t, name="dw_up_value")
    dx1, dx1_b, d_norm2 = _rmsnorm_bwd(x1, norm2_g, dh2, dy, name="norm2_bwd")
    dmix = _matmul(dx1_b, w_out_full, "nt", tm=tm, tn=512, tk=D_MODEL, name="d_mix")
    dw_out = _matmul(mix, dx1_b, "tn", tm=1024, tn=512, tk=t, name="dw_out")
    do_gla, d_gate, d_gla_norm = _gla_out_bwd(o_f, o_b, proj, gla_norm_g, dmix)
    (dq_f, dk_f, dv_f, dg_f, dq_b, dk_b, dv_b, dg_b) = _gla_bwd(proj, g_dec, s_f, s_b, do_gla)
    d_lr, d_w2, d_ba2 = _gla_prep_bwd(proj_lr, w2, ba2, dg_f, dg_b)
    dqa, dk_lo, dk_mid, dk_hi, dv_lo, dv_mid, dv_hi, d_sink8 = _attn_bwd(qa, ka, va, sink, dmix)
    d_qa, d_ka, d_va, d_qn, d_kn = _attn_prep_bwd(proj, cos, sin_signed, gq, gk, dqa,
                                                  (dk_lo, dk_mid, dk_hi), (dv_lo, dv_mid, dv_hi))
    d_seg = {"qa": d_qa, "gate": d_gate, "vg": (dv_f + dv_b).astype(BF16), "qg": (dq_f + dq_b).astype(BF16),
             "kg": (dk_f + dk_b).astype(BF16), "ka": d_ka, "va": d_va}
    d_proj = jnp.concatenate([d_seg[k] for k in sorted(SEGMENTS, key=lambda k: SEGMENTS[k][0])], axis=1)
    dh1 = _matmul(d_lr, w_lr, "nt", tm=tm, tn=512, tk=LR_PAD, name="dh1_lr")
    dh1 = _matmul(d_proj, w_main, "nt", tm=tm, tn=512, tk=IN_MAIN // 2, res=dh1, name="dh1_main")
    dw_main = _matmul(h1, d_proj, "tn", tm=1024, tn=512, tk=t, name="dw_in_main")
    dw_lr = _matmul(h1, d_lr, "tn", tm=1024, tn=LR_PAD, tk=t, name="dw_in_lr")
    grad_x, _, d_norm1 = _rmsnorm_bwd(xs, norm1_g, dh1, dx1, name="norm1_bwd")
    return (loss_part, grad_x, dw_main, dw_lr, dw_out, dw_ug, dw_uv, dw_down, dcw_g, dcw_v, dcb_g, dcb_v,
            d_w2, d_ba2, d_norm1, d_norm2, d_qn, d_kn, d_sink8, d_gla_norm)


def kernel(x, norm1_g, w_in, attn_q_norm_g, attn_k_norm_g, attn_sink, gla_wa2_fwd, gla_ba_fwd, gla_wa2_bwd, gla_ba_bwd, gla_out_norm_g, w_out, norm2_g, w_up, conv_w, conv_b, w_down, loss_target, m_norm1_g, m_w_in, m_attn_q_norm_g, m_attn_k_norm_g, m_attn_sink, m_gla_wa2_fwd, m_gla_ba_fwd, m_gla_wa2_bwd, m_gla_ba_bwd, m_gla_out_norm_g, m_w_out, m_norm2_g, m_w_up, m_conv_w, m_conv_b, m_w_down, v_norm1_g, v_w_in, v_attn_q_norm_g, v_attn_k_norm_g, v_attn_sink, v_gla_wa2_fwd, v_gla_ba_fwd, v_gla_wa2_bwd, v_gla_ba_bwd, v_gla_out_norm_g, v_w_out, v_norm2_g, v_w_up, v_conv_w, v_conv_b, v_w_down):
    t = x.shape[1]
    xs = x.reshape(t, D_MODEL)
    target = loss_target.reshape(t, D_MODEL)
    core = lax.axis_index("c").astype(jnp.int32).reshape(1)

    sharded_small = [conv_w[0], gla_wa2_fwd[0], gla_wa2_bwd[0]]
    small_shapes = [s.shape for s in sharded_small]
    g_in, g_out, g_up, g_down, g_small = _all_gather(
        [w_in[0].astype(BF16), w_out[0].astype(BF16), w_up[0].astype(BF16), w_down[0].astype(BF16),
         _pack(sharded_small)], name="gather_weights")
    w_in_full = jnp.transpose(g_in, (1, 0, 2)).reshape(D_MODEL, IN_TOTAL)
    w_main, w_lr = _to_kernel_columns(w_in_full)
    w_out_full = g_out.reshape(D_MODEL, D_MODEL)
    w_ug = jnp.transpose(g_up[:N_CHIP], (1, 0, 2)).reshape(D_MODEL, D_FF)
    w_uv = jnp.transpose(g_up[N_CHIP:], (1, 0, 2)).reshape(D_MODEL, D_FF)
    w_down_full = g_down.reshape(D_FF, D_MODEL)
    small_full = [_unpack(g_small[d], small_shapes) for d in range(N_DEV)]
    conv_w_full = jnp.concatenate([s[0] for s in small_full], axis=1)
    wa2_f = jnp.concatenate([s[1] for s in small_full], axis=1)
    wa2_b = jnp.concatenate([s[2] for s in small_full], axis=1)
    cw_g, cw_v = conv_w_full[:, :D_FF], conv_w_full[:, D_FF:]
    cb_g, cb_v = conv_b[:, :D_FF], conv_b[:, D_FF:]
    w2 = jnp.zeros((LR_PAD, 2 * GLA_KEY_WIDTH), F32)
    w2 = w2.at[:GLA_RANK, :GLA_KEY_WIDTH].set(wa2_f).at[GLA_RANK:2 * GLA_RANK, GLA_KEY_WIDTH:].set(wa2_b)
    ba2 = jnp.concatenate([gla_ba_fwd, gla_ba_bwd], axis=1)
    (loss_part, grad_x, dw_main, dw_lr, dw_out, dw_ug, dw_uv, dw_down, dcw_g, dcw_v, dcb_g, dcb_v, d_w2, d_ba2,
     d_norm1, d_norm2, d_qn, d_kn, d_sink8, d_gla_norm) = _local_step(
        xs, target, norm1_g, w_main, w_lr, attn_q_norm_g, attn_k_norm_g, attn_sink, w2, ba2, gla_out_norm_g,
        w_out_full, norm2_g, w_ug, w_uv, cw_g, cw_v, cb_g, cb_v, w_down_full)
    loss = lax.psum(loss_part[0, 0], ("x", "y", "c"))

    dw_in = _from_kernel_columns(dw_main, dw_lr)
    per_in = IN_TOTAL // N_DEV
    per_up = 2 * D_FF // N_DEV
    dconv_w = jnp.concatenate([dcw_g, dcw_v], axis=1)
    d_wa2_f = d_w2[:GLA_RANK, :GLA_KEY_WIDTH]
    d_wa2_b = d_w2[GLA_RANK:2 * GLA_RANK, GLA_KEY_WIDTH:]
    per_wa = GLA_KEY_WIDTH // N_DEV
    small_grad = jnp.stack([
        _pack([dconv_w[:, d * per_up:(d + 1) * per_up], d_wa2_f[:, d * per_wa:(d + 1) * per_wa],
               d_wa2_b[:, d * per_wa:(d + 1) * per_wa]]) for d in range(N_DEV)])
    grads = [
        jnp.transpose(dw_in.reshape(D_MODEL, N_DEV, per_in), (1, 0, 2)),
        dw_out.reshape(N_DEV, D_MODEL // N_DEV, D_MODEL),
        jnp.concatenate([jnp.transpose(dw_ug.reshape(D_MODEL, N_CHIP, per_up), (1, 0, 2)),
                         jnp.transpose(dw_uv.reshape(D_MODEL, N_CHIP, per_up), (1, 0, 2))], axis=0),
        dw_down.reshape(N_DEV, D_FF // N_DEV, D_MODEL),
        small_grad,
    ]
    grads = [g.reshape((N_CHIP, 2) + g.shape[1:]) for g in grads]
    theirs = _swap_with_sibling(grads, name="grad_swap_cores")
    rows = [256, 256, 256, 64, grads[4].shape[2]]
    names = ["w_in", "w_out", "w_up", "w_down", "small"]
    pair = [_pair_sum(g, s, core, tr=r, name="pair_sum_" + n) for g, s, r, n in zip(grads, theirs, rows, names)]
    parts = _exchange_between_chips(pair, name="grad_swap_chips")
    m_small = _pack([m_conv_w[0], m_gla_wa2_fwd[0], m_gla_wa2_bwd[0]])
    v_small = _pack([v_conv_w[0], v_gla_wa2_fwd[0], v_gla_wa2_bwd[0]])
    upd_in = _adamw(parts[0], w_in[0], m_w_in[0], v_w_in[0], tr=256, name="adamw_w_in")
    upd_out = _adamw(parts[1], w_out[0], m_w_out[0], v_w_out[0], tr=256, name="adamw_w_out")
    upd_up = _adamw(parts[2], w_up[0], m_w_up[0], v_w_up[0], tr=256, name="adamw_w_up")
    upd_down = _adamw(parts[3], w_down[0], m_w_down[0], v_w_down[0], tr=64, name="adamw_w_down")
    upd_small = _adamw(parts[4], _pack(sharded_small), m_small, v_small, tr=parts[4].shape[1], name="adamw_small")
    upd_small = [_unpack(u, small_shapes) for u in upd_small]

    rep_names = ["norm1_g", "attn_q_norm_g", "attn_k_norm_g", "attn_sink", "gla_ba_fwd", "gla_ba_bwd",
                 "gla_out_norm_g", "norm2_g", "conv_b"]
    rep_w = [norm1_g, attn_q_norm_g, attn_k_norm_g, attn_sink, gla_ba_fwd, gla_ba_bwd, gla_out_norm_g, norm2_g, conv_b]
    rep_m = [m_norm1_g, m_attn_q_norm_g, m_attn_k_norm_g, m_attn_sink, m_gla_ba_fwd, m_gla_ba_bwd,
             m_gla_out_norm_g, m_norm2_g, m_conv_b]
    rep_v = [v_norm1_g, v_attn_q_norm_g, v_attn_k_norm_g, v_attn_sink, v_gla_ba_fwd, v_gla_ba_bwd,
             v_gla_out_norm_g, v_norm2_g, v_conv_b]
    d_sink = d_sink8[:, :GQA_GROUP, 0].reshape(1, ATTN_HEADS)
    rep_g = [d_norm1, d_qn, d_kn, d_sink, d_ba2[:, :GLA_KEY_WIDTH], d_ba2[:, GLA_KEY_WIDTH:], d_gla_norm, d_norm2,
             jnp.concatenate([dcb_g, dcb_v], axis=1)]
    rep_shapes = [w.shape for w in rep_w]
    (rep_terms,) = _all_gather([_pack(rep_g)], name="gather_small_grads")
    upd_rep = _adamw(rep_terms, _pack(rep_w), _pack(rep_m), _pack(rep_v), tr=rep_terms.shape[1], name="adamw_replicated")
    upd_rep = [_unpack(u, rep_shapes) for u in upd_rep]

    order = ["norm1_g", "w_in", "attn_q_norm_g", "attn_k_norm_g", "attn_sink", "gla_wa2_fwd", "gla_ba_fwd",
             "gla_wa2_bwd", "gla_ba_bwd", "gla_out_norm_g", "w_out", "norm2_g", "w_up", "conv_w", "conv_b", "w_down"]
    outs = [loss, grad_x.reshape(1, t, D_MODEL)]
    for kind in range(4):
        by_name = {n: upd_rep[kind][i] for i, n in enumerate(rep_names)}
        by_name["w_in"] = upd_in[kind][None]
        by_name["w_out"] = upd_out[kind][None]
        by_name["w_up"] = upd_up[kind][None]
        by_name["w_down"] = upd_down[kind][None]
        by_name["conv_w"] = upd_small[kind][0][None]
        by_name["gla_wa2_fwd"] = upd_small[kind][1][None]
        by_name["gla_wa2_bwd"] = upd_small[kind][2][None]
        outs += [by_name[n] for n in order]
    return tuple(outs)
```

```python
import functools

import jax
import jax.numpy as jnp
from jax import lax
from jax.experimental import pallas as pl
from jax.experimental.pallas import tpu as pltpu
from jax.experimental.pallas import tpu_sc as plsc

F32 = jnp.float32
BF16 = jnp.bfloat16

D_MODEL = 2048
HEAD_DIM = 128
ATTN_WIDTH = 1024
ATTN_HEADS = 8
KV_HEADS = 2
GQA_GROUP = 4
KV_WIDTH = KV_HEADS * HEAD_DIM
ATTN_BLOCK = 128
WINDOW = 128
ROPE_THETA = 10000.0
GLA_HEADS = 4
GLA_DK = 128
GLA_DV = 256
GLA_KEY_WIDTH = 512
GLA_WIDTH = 1024
GLA_RANK = 16
GLA_GATE_NORMALIZER = 16.0
GLA_CHUNK = 64
D_FF = 5632
NORM_EPS = 1e-6
IN_TOTAL = 4640
IN_MAIN = 4608
LR_PAD = 128
N_DEV = 8
N_CHIP = 4

ADAM_LR = 0.001
ADAM_B1 = 0.9
ADAM_B2 = 0.999
ADAM_EPS = 1e-08
ADAM_WD = 0.01
ADAM_STEP = 10

SEGMENTS = {
    "qa": (0, 0, 1024),
    "gate": (1024, 3584, 1024),
    "vg": (2048, 2560, 1024),
    "qg": (3072, 1536, 512),
    "kg": (3584, 2048, 512),
    "ka": (4096, 1024, 256),
    "va": (4352, 1280, 256),
}

VMEM_LIMIT = 56 * 1024 * 1024
MESH = pl.DeviceIdType.MESH


def _params(semantics=None, vmem=None):
    return pltpu.CompilerParams(dimension_semantics=semantics, vmem_limit_bytes=vmem)


_DIMS = {
    "nn": (((1,), (0,)), ((), ())),
    "nt": (((1,), (1,)), ((), ())),
    "tn": (((0,), (0,)), ((), ())),
}


def _mxu(a, b, mode):
    return lax.dot_general(a.astype(BF16), b.astype(BF16), _DIMS[mode], preferred_element_type=F32)


@functools.partial(jax.custom_vjp, nondiff_argnums=(2,))
def bdot(a, b, mode):
    return _mxu(a, b, mode)


def _bdot_fwd(a, b, mode):
    return _mxu(a, b, mode), (a, b)


def _bdot_bwd(mode, res, g):
    a, b = res
    if mode == "nn":
        return _mxu(g, b, "nt"), _mxu(a, g, "tn")
    if mode == "nt":
        return _mxu(g, b, "nn"), _mxu(g, a, "tn")
    return _mxu(b, g, "nt"), _mxu(a, g, "nn")


bdot.defvjp(_bdot_fwd, _bdot_bwd)


def _rms(x, g):
    return x * lax.rsqrt(jnp.mean(x * x, axis=-1, keepdims=True) + NORM_EPS) * g


def _rope(x, cos, sin_signed):
    return x * cos + pltpu.roll(x, HEAD_DIM // 2, 1) * sin_signed


def _rope_transposed(d, cos, sin_signed):
    return d * cos + pltpu.roll(d * sin_signed, HEAD_DIM // 2, 1)


def _silu(x):
    return x * jax.nn.sigmoid(x)


def _log_sigmoid(z):
    return -(jnp.maximum(-z, 0.0) + jnp.log(1.0 + jnp.exp(-jnp.abs(z))))


def _matmul(a, b, mode, *, tm, tn, tk, out_dtype=F32, res=None, name):
    if mode == "nn":
        (m, k), (k2, n) = a.shape, b.shape
    elif mode == "nt":
        (m, k), (n, k2) = a.shape, b.shape
    else:
        (k, m), (k2, n) = a.shape, b.shape
    assert k == k2 and m % tm == 0 and n % tn == 0 and k % tk == 0, (name, a.shape, b.shape, tm, tn, tk)
    nk = k // tk
    if mode == "tn":
        a_spec = pl.BlockSpec((tk, tm), lambda i, j, kk: (kk, i))
    else:
        a_spec = pl.BlockSpec((tm, tk), lambda i, j, kk: (i, kk))
    if mode == "nt":
        b_spec = pl.BlockSpec((tn, tk), lambda i, j, kk: (j, kk))
    else:
        b_spec = pl.BlockSpec((tk, tn), lambda i, j, kk: (kk, j))
    o_spec = pl.BlockSpec((tm, tn), lambda i, j, kk: (i, j))
    in_specs = [a_spec, b_spec]
    args = [a, b]
    if res is not None:
        in_specs.append(o_spec)
        args.append(res)
    dims = _DIMS[mode]

    def body(*refs):
        if res is not None:
            a_ref, b_ref, r_ref, o_ref = refs[:4]
            rest = refs[4:]
        else:
            a_ref, b_ref, o_ref = refs[:3]
            r_ref = None
            rest = refs[3:]
        part = lax.dot_general(a_ref[...], b_ref[...], dims, preferred_element_type=F32)

        def finish(acc):
            if r_ref is not None:
                acc = acc + r_ref[...]
            o_ref[...] = acc.astype(out_dtype)

        if nk == 1:
            finish(part)
        else:
            acc_ref = rest[0]
            kk = pl.program_id(2)

            @pl.when(kk == 0)
            def _():
                acc_ref[...] = part

            @pl.when(kk > 0)
            def _():
                acc_ref[...] += part

            @pl.when(kk == nk - 1)
            def _():
                finish(acc_ref[...])

    return pl.pallas_call(
        body,
        name=name,
        grid=(m // tm, n // tn, nk),
        in_specs=in_specs,
        out_specs=o_spec,
        out_shape=jax.ShapeDtypeStruct((m, n), out_dtype),
        scratch_shapes=[pltpu.VMEM((tm, tn), F32)] if nk > 1 else [],
        compiler_params=_params(("parallel", "parallel", "arbitrary"), VMEM_LIMIT),
    )(*args)


def _rmsnorm_fwd(x, g, *, name, tr=512):
    t, d = x.shape

    def body(x_ref, g_ref, h_ref):
        h_ref[...] = _rms(x_ref[...], g_ref[...]).astype(BF16)

    return pl.pallas_call(
        body, name=name, grid=(t // tr,),
        in_specs=[pl.BlockSpec((tr, d), lambda i: (i, 0)), pl.BlockSpec((1, d), lambda i: (0, 0))],
        out_specs=pl.BlockSpec((tr, d), lambda i: (i, 0)),
        out_shape=jax.ShapeDtypeStruct((t, d), BF16),
        compiler_params=_params(("parallel",), VMEM_LIMIT),
    )(x, g)


def _rmsnorm_bwd(x, g, dh, dres, *, name, tr=256):
    t, d = x.shape

    def body(x_ref, g_ref, dh_ref, dres_ref, dx_ref, dxb_ref, dg_ref):
        _, vjp = jax.vjp(_rms, x_ref[...], g_ref[...])
        dx, dg = vjp(dh_ref[...])
        dx = dx + dres_ref[...]
        dx_ref[...] = dx
        dxb_ref[...] = dx.astype(BF16)

        @pl.when(pl.program_id(0) == 0)
        def _():
            dg_ref[...] = jnp.zeros_like(dg_ref)

        dg_ref[...] += dg

    row = pl.BlockSpec((tr, d), lambda i: (i, 0))
    vec = pl.BlockSpec((1, d), lambda i: (0, 0))
    return pl.pallas_call(
        body, name=name, grid=(t // tr,),
        in_specs=[row, vec, row, row],
        out_specs=[row, row, vec],
        out_shape=[jax.ShapeDtypeStruct((t, d), F32), jax.ShapeDtypeStruct((t, d), BF16),
                   jax.ShapeDtypeStruct((1, d), F32)],
        compiler_params=_params(("arbitrary",), VMEM_LIMIT),
    )(x, g, dh, dres)


def _seg_block(name, width):
    off = SEGMENTS[name][0]
    assert off % width == 0
    return off // width


def _attn_prep_fwd(proj, cos, sin_signed, gq, gk, *, tr=256):
    t = proj.shape[0]

    def body(q_ref, k_ref, v_ref, cos_ref, sin_ref, gq_ref, gk_ref, qo_ref, ko_ref, vo_ref):
        cos_t, sin_t = cos_ref[...], sin_ref[...]
        for h in range(ATTN_HEADS):
            cols = slice(h * HEAD_DIM, (h + 1) * HEAD_DIM)
            qo_ref[:, cols] = _rope(_rms(q_ref[:, cols], gq_ref[...]), cos_t, sin_t).astype(BF16)
        for h in range(KV_HEADS):
            cols = slice(h * HEAD_DIM, (h + 1) * HEAD_DIM)
            ko_ref[:, cols] = _rope(_rms(k_ref[:, cols], gk_ref[...]), cos_t, sin_t).astype(BF16)
        vo_ref[...] = v_ref[...].astype(BF16)

    qb, kb, vb = _seg_block("qa", ATTN_WIDTH), _seg_block("ka", KV_WIDTH), _seg_block("va", KV_WIDTH)
    tab = pl.BlockSpec((tr, HEAD_DIM), lambda i: (i, 0))
    vec = pl.BlockSpec((1, HEAD_DIM), lambda i: (0, 0))
    return pl.pallas_call(
        body, name="attn_prep_fwd", grid=(t // tr,),
        in_specs=[pl.BlockSpec((tr, ATTN_WIDTH), lambda i: (i, qb)),
                  pl.BlockSpec((tr, KV_WIDTH), lambda i: (i, kb)),
                  pl.BlockSpec((tr, KV_WIDTH), lambda i: (i, vb)),
                  tab, tab, vec, vec],
        out_specs=[pl.BlockSpec((tr, ATTN_WIDTH), lambda i: (i, 0)),
                   pl.BlockSpec((tr, KV_WIDTH), lambda i: (i, 0)),
                   pl.BlockSpec((tr, KV_WIDTH), lambda i: (i, 0))],
        out_shape=[jax.ShapeDtypeStruct((t, ATTN_WIDTH), BF16),
                   jax.ShapeDtypeStruct((t, KV_WIDTH), BF16),
                   jax.ShapeDtypeStruct((t, KV_WIDTH), BF16)],
        compiler_params=_params(("parallel",), VMEM_LIMIT),
    )(proj, proj, proj, cos, sin_signed, gq, gk)


def _attn_head(q, kcat, vcat, sink_col, valid):
    s = bdot(q, kcat, "nt") * (HEAD_DIM ** -0.5)
    s = jnp.where(valid, s, -jnp.inf)
    m = lax.stop_gradient(jnp.maximum(jnp.max(s, axis=-1, keepdims=True), sink_col))
    p = jnp.exp(s - m)
    p = p / (jnp.sum(p, axis=-1, keepdims=True) + jnp.exp(sink_col - m))
    return bdot(p, vcat, "nn")


def _attn_valid(n, t):
    qi = lax.broadcasted_iota(jnp.int32, (ATTN_BLOCK, 3 * ATTN_BLOCK), 0)
    sj = lax.broadcasted_iota(jnp.int32, (ATTN_BLOCK, 3 * ATTN_BLOCK), 1)
    kpos = n * ATTN_BLOCK - ATTN_BLOCK + sj
    return (jnp.abs(sj - ATTN_BLOCK - qi) <= WINDOW) & (kpos >= 0) & (kpos < t)


def _attn_specs(nb):
    q_spec = pl.BlockSpec((ATTN_BLOCK, GQA_GROUP * HEAD_DIM), lambda h, n: (n, h))
    kv_specs = [
        pl.BlockSpec((ATTN_BLOCK, HEAD_DIM), lambda h, n: (jnp.maximum(n - 1, 0), h)),
        pl.BlockSpec((ATTN_BLOCK, HEAD_DIM), lambda h, n: (n, h)),
        pl.BlockSpec((ATTN_BLOCK, HEAD_DIM), lambda h, n: (jnp.minimum(n + 1, nb - 1), h)),
    ]
    return q_spec, kv_specs


def _attn_fwd(q, k, v, sink):
    t = q.shape[0]
    nb = t // ATTN_BLOCK

    def body(sink_ref, q_ref, kp_ref, kc_ref, kn_ref, vp_ref, vc_ref, vn_ref, o_ref):
        h, n = pl.program_id(0), pl.program_id(1)
        valid = _attn_valid(n, t)
        kcat = jnp.concatenate([kp_ref[...], kc_ref[...], kn_ref[...]], axis=0).astype(F32)
        vcat = jnp.concatenate([vp_ref[...], vc_ref[...], vn_ref[...]], axis=0).astype(F32)
        for g in range(GQA_GROUP):
            cols = slice(g * HEAD_DIM, (g + 1) * HEAD_DIM)
            sink_col = jnp.full((ATTN_BLOCK, 1), sink_ref[h * GQA_GROUP + g], F32)
            o_ref[:, cols] = _attn_head(q_ref[:, cols].astype(F32), kcat, vcat, sink_col, valid).astype(BF16)

    q_spec, kv_specs = _attn_specs(nb)
    return pl.pallas_call(
        body, name="attn_fwd", grid=(KV_HEADS, nb),
        in_specs=[pl.BlockSpec(memory_space=pltpu.SMEM), q_spec] + kv_specs + kv_specs,
        out_specs=q_spec,
        out_shape=jax.ShapeDtypeStruct((t, ATTN_WIDTH), BF16),
        compiler_params=_params(("parallel", "parallel"), VMEM_LIMIT),
    )(sink, q, k, k, k, v, v, v)


def _attn_bwd(q, k, v, sink, dmix):
    t = q.shape[0]
    nb = t // ATTN_BLOCK

    def body(sink_ref, q_ref, kp_ref, kc_ref, kn_ref, vp_ref, vc_ref, vn_ref, do_ref,
             dq_ref, dk_lo, dk_mid, dk_hi, dv_lo, dv_mid, dv_hi, dsink_ref):
        h, n = pl.program_id(0), pl.program_id(1)
        valid = _attn_valid(n, t)
        kcat = jnp.concatenate([kp_ref[...], kc_ref[...], kn_ref[...]], axis=0).astype(F32)
        vcat = jnp.concatenate([vp_ref[...], vc_ref[...], vn_ref[...]], axis=0).astype(F32)
        dk = jnp.zeros((3 * ATTN_BLOCK, HEAD_DIM), F32)
        dv = jnp.zeros((3 * ATTN_BLOCK, HEAD_DIM), F32)
        row = lax.broadcasted_iota(jnp.int32, (8, HEAD_DIM), 0)
        dsink = jnp.zeros((8, HEAD_DIM), F32)
        for g in range(GQA_GROUP):
            cols = slice(g * HEAD_DIM, (g + 1) * HEAD_DIM)
            sink_col = jnp.full((ATTN_BLOCK, 1), sink_ref[h * GQA_GROUP + g], F32)
            _, vjp = jax.vjp(functools.partial(_attn_head, valid=valid),
                             q_ref[:, cols].astype(F32), kcat, vcat, sink_col)
            dq_g, dk_g, dv_g, dsink_g = vjp(do_ref[:, cols])
            dq_ref[:, cols] = dq_g
            dk, dv = dk + dk_g, dv + dv_g
            dsink = dsink + jnp.where(row == g, jnp.sum(dsink_g), 0.0)
        for i, (dk_ref, dv_ref) in enumerate(((dk_lo, dv_lo), (dk_mid, dv_mid), (dk_hi, dv_hi))):
            rows = slice(i * ATTN_BLOCK, (i + 1) * ATTN_BLOCK)
            dk_ref[...] = dk[rows]
            dv_ref[...] = dv[rows]

        @pl.when(n == 0)
        def _():
            dsink_ref[...] = jnp.zeros_like(dsink_ref)

        dsink_ref[...] += dsink

    q_spec, kv_specs = _attn_specs(nb)
    kv_out = pl.BlockSpec((ATTN_BLOCK, HEAD_DIM), lambda h, n: (n, h))
    kv_shape = jax.ShapeDtypeStruct((t, KV_WIDTH), F32)
    return pl.pallas_call(
        body, name="attn_bwd", grid=(KV_HEADS, nb),
        in_specs=[pl.BlockSpec(memory_space=pltpu.SMEM), q_spec] + kv_specs + kv_specs + [q_spec],
        out_specs=[q_spec] + [kv_out] * 6 + [pl.BlockSpec((None, 8, HEAD_DIM), lambda h, n: (h, 0, 0))],
        out_shape=[jax.ShapeDtypeStruct((t, ATTN_WIDTH), F32)] + [kv_shape] * 6
                  + [jax.ShapeDtypeStruct((KV_HEADS, 8, HEAD_DIM), F32)],
        compiler_params=_params(("parallel", "arbitrary"), VMEM_LIMIT),
    )(sink, q, k, k, k, v, v, v, dmix)


def _attn_prep_bwd(proj, cos, sin_signed, gq, gk, dq, dks, dvs):
    t = proj.shape[0]
    tr = ATTN_BLOCK
    nb = t // tr

    def body(q_ref, k_ref, cos_ref, sin_ref, gq_ref, gk_ref, dq_ref,
             dk_lo, dk_mid, dk_hi, dv_lo, dv_mid, dv_hi,
             dqo_ref, dko_ref, dvo_ref, dgq_ref, dgk_ref):
        n = pl.program_id(0)
        cos_t, sin_t = cos_ref[...], sin_ref[...]
        has_next = (n < nb - 1).astype(F32)
        has_prev = (n > 0).astype(F32)
        dk = dk_lo[...] * has_next + dk_mid[...] + dk_hi[...] * has_prev
        dv = dv_lo[...] * has_next + dv_mid[...] + dv_hi[...] * has_prev
        dvo_ref[...] = dv.astype(BF16)
        dgq = jnp.zeros((1, HEAD_DIM), F32)
        dgk = jnp.zeros((1, HEAD_DIM), F32)
        for h in range(ATTN_HEADS):
            cols = slice(h * HEAD_DIM, (h + 1) * HEAD_DIM)
            _, vjp = jax.vjp(_rms, q_ref[:, cols], gq_ref[...])
            dx, dg = vjp(_rope_transposed(dq_ref[:, cols], cos_t, sin_t))
            dqo_ref[:, cols] = dx.astype(BF16)
            dgq = dgq + dg
        for h in range(KV_HEADS):
            cols = slice(h * HEAD_DIM, (h + 1) * HEAD_DIM)
            _, vjp = jax.vjp(_rms, k_ref[:, cols], gk_ref[...])
            dx, dg = vjp(_rope_transposed(dk[:, cols], cos_t, sin_t))
            dko_ref[:, cols] = dx.astype(BF16)
            dgk = dgk + dg

        @pl.when(n == 0)
        def _():
            dgq_ref[...] = jnp.zeros_like(dgq_ref)
            dgk_ref[...] = jnp.zeros_like(dgk_ref)

        dgq_ref[...] += dgq
        dgk_ref[...] += dgk

    qb, kb = _seg_block("qa", ATTN_WIDTH), _seg_block("ka", KV_WIDTH)
    tab = pl.BlockSpec((tr, HEAD_DIM), lambda i: (i, 0))
    vec = pl.BlockSpec((1, HEAD_DIM), lambda i: (0, 0))
    kv = [pl.BlockSpec((tr, KV_WIDTH), lambda i: (jnp.minimum(i + 1, nb - 1), 0)),
          pl.BlockSpec((tr, KV_WIDTH), lambda i: (i, 0)),
          pl.BlockSpec((tr, KV_WIDTH), lambda i: (jnp.maximum(i - 1, 0), 0))]
    wide = pl.BlockSpec((tr, ATTN_WIDTH), lambda i: (i, 0))
    narrow = pl.BlockSpec((tr, KV_WIDTH), lambda i: (i, 0))
    return pl.pallas_call(
        body, name="attn_prep_bwd", grid=(nb,),
        in_specs=[pl.BlockSpec((tr, ATTN_WIDTH), lambda i: (i, qb)),
                  pl.BlockSpec((tr, KV_WIDTH), lambda i: (i, kb)),
                  tab, tab, vec, vec, wide] + kv + kv,
        out_specs=[wide, narrow, narrow, vec, vec],
        out_shape=[jax.ShapeDtypeStruct((t, ATTN_WIDTH), BF16),
                   jax.ShapeDtypeStruct((t, KV_WIDTH), BF16),
                   jax.ShapeDtypeStruct((t, KV_WIDTH), BF16),
                   jax.ShapeDtypeStruct((1, HEAD_DIM), F32),
                   jax.ShapeDtypeStruct((1, HEAD_DIM), F32)],
        compiler_params=_params(("arbitrary",), VMEM_LIMIT),
    )(proj, proj, cos, sin_signed, gq, gk, dq, *dks, *dvs)


def _decay_fn(lr, w2, ba):
    return _log_sigmoid(bdot(lr, w2, "nn") + ba) / GLA_GATE_NORMALIZER


def _gla_prep_fwd(proj_lr, w2, ba2, *, tr=512):
    t = proj_lr.shape[0]
    width = 2 * GLA_KEY_WIDTH

    def body(lr_ref, w2_ref, ba_ref, g_ref):
        g_ref[...] = _decay_fn(lr_ref[...], w2_ref[...], ba_ref[...])

    return pl.pallas_call(
        body, name="gla_prep_fwd", grid=(t // tr,),
        in_specs=[pl.BlockSpec((tr, LR_PAD), lambda i: (i, 0)),
                  pl.BlockSpec((LR_PAD, width), lambda i: (0, 0)),
                  pl.BlockSpec((1, width), lambda i: (0, 0))],
        out_specs=pl.BlockSpec((tr, width), lambda i: (i, 0)),
        out_shape=jax.ShapeDtypeStruct((t, width), F32),
        compiler_params=_params(("parallel",), VMEM_LIMIT),
    )(proj_lr, w2, ba2)


def _gla_prep_bwd(proj_lr, w2, ba2, dg_f, dg_b, *, tr=512):
    t = proj_lr.shape[0]
    width = 2 * GLA_KEY_WIDTH

    def body(lr_ref, w2_ref, ba_ref, dgf_ref, dgb_ref, dlr_ref, dw2_ref, dba_ref):
        _, vjp = jax.vjp(_decay_fn, lr_ref[...], w2_ref[...], ba_ref[...])
        dlr, dw2, dba = vjp(jnp.concatenate([dgf_ref[...], dgb_ref[...]], axis=1))
        dlr_ref[...] = dlr.astype(BF16)

        @pl.when(pl.program_id(0) == 0)
        def _():
            dw2_ref[...] = jnp.zeros_like(dw2_ref)
            dba_ref[...] = jnp.zeros_like(dba_ref)

        dw2_ref[...] += dw2
        dba_ref[...] += dba

    half = pl.BlockSpec((tr, GLA_KEY_WIDTH), lambda i: (i, 0))
    return pl.pallas_call(
        body, name="gla_prep_bwd", grid=(t // tr,),
        in_specs=[pl.BlockSpec((tr, LR_PAD), lambda i: (i, 0)),
                  pl.BlockSpec((LR_PAD, width), lambda i: (0, 0)),
                  pl.BlockSpec((1, width), lambda i: (0, 0)), half, half],
        out_specs=[pl.BlockSpec((tr, LR_PAD), lambda i: (i, 0)),
                   pl.BlockSpec((LR_PAD, width), lambda i: (0, 0)),
                   pl.BlockSpec((1, width), lambda i: (0, 0))],
        out_shape=[jax.ShapeDtypeStruct((t, LR_PAD), BF16),
                   jax.ShapeDtypeStruct((LR_PAD, width), F32),
                   jax.ShapeDtypeStruct((1, width), F32)],
        compiler_params=_params(("arbitrary",), VMEM_LIMIT),
    )(proj_lr, w2, ba2, dg_f, dg_b)


def _gla_chunk(q, k, v, g, state, forward):
    c = GLA_CHUNK
    row = lax.broadcasted_iota(jnp.int32, (c, c), 0)
    col = lax.broadcasted_iota(jnp.int32, (c, c), 1)
    rid = lax.broadcasted_iota(jnp.int32, (c, GLA_DK), 0)
    q = q * (GLA_DK ** -0.5)
    if forward:
        cum = (row >= col).astype(F32)
        see = row >= col
        upto_ref = rid <= c // 2
    else:
        cum = (row <= col).astype(F32)
        see = row < col
        upto_ref = rid >= c - 1 - c // 2
    b = lax.dot_general(cum, g, _DIMS["nn"], precision=lax.Precision.HIGHEST, preferred_element_type=F32)
    b_last = jnp.sum(g, axis=0, keepdims=True)
    b_ref = jnp.sum(jnp.where(upto_ref, g, 0.0), axis=0, keepdims=True)
    a = bdot(q * jnp.exp(b - b_ref), k * jnp.exp(b_ref - b), "nt")
    a = jnp.where(see, a, 0.0)
    o = bdot(a, v, "nn") + bdot(q * jnp.exp(b), state, "nt")
    new_state = state * jnp.exp(b_last) + bdot(v, k * jnp.exp(b_last - b), "tn")
    return o, new_state


def _gla_fwd(proj, g):
    t = proj.shape[0]
    c = GLA_CHUNK
    nchunk = t // c
    qb, kb, vb = _seg_block("qg", GLA_KEY_WIDTH), _seg_block("kg", GLA_KEY_WIDTH), _seg_block("vg", GLA_WIDTH)

    def body(qf, kf, vf, gf, qr, kr, vr, gr, of_ref, ob_ref, sf_ref, sb_ref, state):
        @pl.when(pl.program_id(0) == 0)
        def _():
            state[...] = jnp.zeros_like(state)

        for d, (q_ref, k_ref, v_ref, g_ref, o_ref, s_ref) in enumerate(
                ((qf, kf, vf, gf, of_ref, sf_ref), (qr, kr, vr, gr, ob_ref, sb_ref))):
            for h in range(GLA_HEADS):
                kc = slice(h * GLA_DK, (h + 1) * GLA_DK)
                vc = slice(h * GLA_DV, (h + 1) * GLA_DV)
                s_in = state[d, h]
                s_ref[h] = s_in
                o, s_out = _gla_chunk(q_ref[:, kc], k_ref[:, kc], v_ref[:, vc], g_ref[:, kc], s_in, d == 0)
                o_ref[:, vc] = o
                state[d, h] = s_out

    specs, outs = [], []
    for d in range(2):
        ci = (lambda i: i) if d == 0 else (lambda i: nchunk - 1 - i)
        specs += [pl.BlockSpec((c, GLA_KEY_WIDTH), lambda i, ci=ci: (ci(i), qb)),
                  pl.BlockSpec((c, GLA_KEY_WIDTH), lambda i, ci=ci: (ci(i), kb)),
                  pl.BlockSpec((c, GLA_WIDTH), lambda i, ci=ci: (ci(i), vb)),
                  pl.BlockSpec((c, GLA_KEY_WIDTH), lambda i, ci=ci, d=d: (ci(i), d))]
        outs.append(pl.BlockSpec((c, GLA_WIDTH), lambda i, ci=ci: (ci(i), 0)))
    for d in range(2):
        ci = (lambda i: i) if d == 0 else (lambda i: nchunk - 1 - i)
        outs.append(pl.BlockSpec((None, GLA_HEADS, GLA_DV, GLA_DK), lambda i, ci=ci: (ci(i), 0, 0, 0)))
    o_shape = jax.ShapeDtypeStruct((t, GLA_WIDTH), F32)
    s_shape = jax.ShapeDtypeStruct((nchunk, GLA_HEADS, GLA_DV, GLA_DK), F32)
    return pl.pallas_call(
        body, name="gla_fwd", grid=(nchunk,),
        in_specs=specs, out_specs=outs,
        out_shape=[o_shape, o_shape, s_shape, s_shape],
        scratch_shapes=[pltpu.VMEM((2, GLA_HEADS, GLA_DV, GLA_DK), F32)],
        compiler_params=_params(("arbitrary",), VMEM_LIMIT),
    )(proj, proj, proj, g, proj, proj, proj, g)


def _gla_bwd(proj, g, s_f, s_b, do):
    t = proj.shape[0]
    c = GLA_CHUNK
    nchunk = t // c
    qb, kb, vb = _seg_block("qg", GLA_KEY_WIDTH), _seg_block("kg", GLA_KEY_WIDTH), _seg_block("vg", GLA_WIDTH)

    def body(*refs):
        ins, outs, dstate = refs[:12], refs[12:20], refs[20]

        @pl.when(pl.program_id(0) == 0)
        def _():
            dstate[...] = jnp.zeros_like(dstate)

        for d in range(2):
            q_ref, k_ref, v_ref, g_ref, s_ref, do_ref = ins[6 * d:6 * d + 6]
            dq_ref, dk_ref, dv_ref, dg_ref = outs[4 * d:4 * d + 4]
            for h in range(GLA_HEADS):
                kc = slice(h * GLA_DK, (h + 1) * GLA_DK)
                vc = slice(h * GLA_DV, (h + 1) * GLA_DV)
                _, vjp = jax.vjp(functools.partial(_gla_chunk, forward=(d == 0)),
                                 q_ref[:, kc], k_ref[:, kc], v_ref[:, vc], g_ref[:, kc], s_ref[h])
                dq, dk, dv, dg, ds = vjp((do_ref[:, vc], dstate[d, h]))
                dq_ref[:, kc] = dq
                dk_ref[:, kc] = dk
                dv_ref[:, vc] = dv
                dg_ref[:, kc] = dg
                dstate[d, h] = ds

    specs, outs, shapes = [], [], []
    for d in range(2):
        ci = (lambda i: nchunk - 1 - i) if d == 0 else (lambda i: i)
        specs += [pl.BlockSpec((c, GLA_KEY_WIDTH), lambda i, ci=ci: (ci(i), qb)),
                  pl.BlockSpec((c, GLA_KEY_WIDTH), lambda i, ci=ci: (ci(i), kb)),
                  pl.BlockSpec((c, GLA_WIDTH), lambda i, ci=ci: (ci(i), vb)),
                  pl.BlockSpec((c, GLA_KEY_WIDTH), lambda i, ci=ci, d=d: (ci(i), d)),
                  pl.BlockSpec((None, GLA_HEADS, GLA_DV, GLA_DK), lambda i, ci=ci: (ci(i), 0, 0, 0)),
                  pl.BlockSpec((c, GLA_WIDTH), lambda i, ci=ci: (ci(i), 0))]
        key = pl.BlockSpec((c, GLA_KEY_WIDTH), lambda i, ci=ci: (ci(i), 0))
        val = pl.BlockSpec((c, GLA_WIDTH), lambda i, ci=ci: (ci(i), 0))
        outs += [key, key, val, key]
        shapes += [jax.ShapeDtypeStruct((t, GLA_KEY_WIDTH), F32), jax.ShapeDtypeStruct((t, GLA_KEY_WIDTH), F32),
                   jax.ShapeDtypeStruct((t, GLA_WIDTH), F32), jax.ShapeDtypeStruct((t, GLA_KEY_WIDTH), F32)]
    return pl.pallas_call(
        body, name="gla_bwd", grid=(nchunk,),
        in_specs=specs, out_specs=outs, out_shape=shapes,
        scratch_shapes=[pltpu.VMEM((2, GLA_HEADS, GLA_DV, GLA_DK), F32)],
        compiler_params=_params(("arbitrary",), VMEM_LIMIT),
    )(proj, proj, proj, g, s_f, do, proj, proj, proj, g, s_b, do)


def _gla_out_head(o_f, o_b, gate, gn):
    return _rms(o_f + o_b, gn) * _silu(gate)


def _gla_out_fwd(o_f, o_b, proj, gn, *, tr=256):
    t = o_f.shape[0]
    gb = _seg_block("gate", GLA_WIDTH)

    def body(of_ref, ob_ref, gate_ref, gn_ref, out_ref):
        for h in range(GLA_HEADS):
            vc = slice(h * GLA_DV, (h + 1) * GLA_DV)
            out_ref[:, vc] = _gla_out_head(of_ref[:, vc], ob_ref[:, vc], gate_ref[:, vc], gn_ref[...]).astype(BF16)

    wide = pl.BlockSpec((tr, GLA_WIDTH), lambda i: (i, 0))
    return pl.pallas_call(
        body, name="gla_out_fwd", grid=(t // tr,),
        in_specs=[wide, wide, pl.BlockSpec((tr, GLA_WIDTH), lambda i: (i, gb)),
                  pl.BlockSpec((1, GLA_DV), lambda i: (0, 0))],
        out_specs=wide,
        out_shape=jax.ShapeDtypeStruct((t, GLA_WIDTH), BF16),
        compiler_params=_params(("parallel",), VMEM_LIMIT),
    )(o_f, o_b, proj, gn)


def _gla_out_bwd(o_f, o_b, proj, gn, dmix, *, tr=256):
    t = o_f.shape[0]
    gb = _seg_block("gate", GLA_WIDTH)

    def body(of_ref, ob_ref, gate_ref, gn_ref, dout_ref, do_ref, dgate_ref, dgn_ref):
        dgn = jnp.zeros((1, GLA_DV), F32)
        for h in range(GLA_HEADS):
            vc = slice(h * GLA_DV, (h + 1) * GLA_DV)
            _, vjp = jax.vjp(_gla_out_head, of_ref[:, vc], ob_ref[:, vc], gate_ref[:, vc], gn_ref[...])
            do, _, dgate, dg = vjp(dout_ref[:, vc])
            do_ref[:, vc] = do
            dgate_ref[:, vc] = dgate.astype(BF16)
            dgn = dgn + dg

        @pl.when(pl.program_id(0) == 0)
        def _():
            dgn_ref[...] = jnp.zeros_like(dgn_ref)

        dgn_ref[...] += dgn

    wide = pl.BlockSpec((tr, GLA_WIDTH), lambda i: (i, 0))
    vec = pl.BlockSpec((1, GLA_DV), lambda i: (0, 0))
    return pl.pallas_call(
        body, name="gla_out_bwd", grid=(t // tr,),
        in_specs=[wide, wide, pl.BlockSpec((tr, GLA_WIDTH), lambda i: (i, gb)), vec,
                  pl.BlockSpec((tr, GLA_WIDTH), lambda i: (i, 1))],
        out_specs=[wide, wide, vec],
        out_shape=[jax.ShapeDtypeStruct((t, GLA_WIDTH), F32), jax.ShapeDtypeStruct((t, GLA_WIDTH), BF16),
                   jax.ShapeDtypeStruct((1, GLA_DV), F32)],
        compiler_params=_params(("arbitrary",), VMEM_LIMIT),
    )(o_f, o_b, proj, gn, dmix)


CONV_TR = 512
CONV_TC = 256
HALO = 8
HALO16 = 16


def _conv3(u, w, b):
    n = u.shape[0]
    return pltpu.roll(u, 1, 0) * w[0:1] + u * w[1:2] + pltpu.roll(u, n - 1, 0) * w[2:3] + b


def _conv_ext(main_ref, prev_ref, next_ref, r, nr):
    prev = prev_ref[...].astype(F32)[-HALO:] * (r > 0).astype(F32)
    nxt = next_ref[...].astype(F32)[:HALO] * (r < nr - 1).astype(F32)
    return jnp.concatenate([prev, main_ref[...].astype(F32), nxt], axis=0)


def _conv_specs(t, halo, col_of):
    per = CONV_TR // halo
    last = t // halo - 1
    return [pl.BlockSpec((CONV_TR, CONV_TC), lambda j, r: (r, col_of(j))),
            pl.BlockSpec((halo, CONV_TC), lambda j, r: (jnp.maximum(r * per - 1, 0), col_of(j))),
            pl.BlockSpec((halo, CONV_TC), lambda j, r: (jnp.minimum((r + 1) * per, last), col_of(j)))]


def _ffn_mid_fwd(u_g, u_v, cw_g, cw_v, cb_g, cb_v):
    t, f = u_g.shape
    nr = t // CONV_TR

    def body(ug, ugp, ugn, uv, uvp, uvn, wg, wv, bg, bv, a_ref):
        r = pl.program_id(1)
        gate = _conv3(_conv_ext(ug, ugp, ugn, r, nr), wg[...], bg[...])[HALO:HALO + CONV_TR]
        val = _conv3(_conv_ext(uv, uvp, uvn, r, nr), wv[...], bv[...])[HALO:HALO + CONV_TR]
        a_ref[...] = (_silu(gate) * val).astype(BF16)

    same = lambda j: j
    w_spec = pl.BlockSpec((3, CONV_TC), lambda j, r: (0, j))
    b_spec = pl.BlockSpec((1, CONV_TC), lambda j, r: (0, j))
    return pl.pallas_call(
        body, name="ffn_mid_fwd", grid=(f // CONV_TC, nr),
        in_specs=_conv_specs(t, HALO, same) + _conv_specs(t, HALO, same) + [w_spec, w_spec, b_spec, b_spec],
        out_specs=pl.BlockSpec((CONV_TR, CONV_TC), lambda j, r: (r, j)),
        out_shape=jax.ShapeDtypeStruct((t, f), BF16),
        compiler_params=_params(("parallel", "parallel"), VMEM_LIMIT),
    )(u_g, u_g, u_g, u_v, u_v, u_v, cw_g, cw_v, cb_g, cb_v)


def _ffn_mid_bwd(u_g, u_v, cw_g, cw_v, cb_g, cb_v, da):
    t, f = u_g.shape
    nr = t // CONV_TR
    ext = CONV_TR + 2 * HALO

    def body(ug, ugp, ugn, uv, uvp, uvn, dam, dap, dan, wg, wv, bg, bv,
             dug_ref, duv_ref, dwg_ref, dwv_ref, dbg_ref, dbv_ref):
        r = pl.program_id(1)
        xg = _conv_ext(ug, ugp, ugn, r, nr)
        xv = _conv_ext(uv, uvp, uvn, r, nr)
        da_x = _conv_ext(dam, dap, dan, r, nr)
        gate = _conv3(xg, wg[...], bg[...])
        val = _conv3(xv, wv[...], bv[...])
        pos = r * CONV_TR - HALO + lax.broadcasted_iota(jnp.int32, (ext, CONV_TC), 0)
        inside = (pos >= 0) & (pos < t)
        sig = jax.nn.sigmoid(gate)
        silu = gate * sig
        d_val = jnp.where(inside, da_x * silu, 0.0)
        d_gate = jnp.where(inside, da_x * val * (sig + silu * (1.0 - sig)), 0.0)
        own = lax.broadcasted_iota(jnp.int32, (ext, CONV_TC), 0)
        own = (own >= HALO) & (own < HALO + CONV_TR)
        for x, d, w, du_ref, dw_ref, db_ref in ((xg, d_gate, wg, dug_ref, dwg_ref, dbg_ref),
                                                (xv, d_val, wv, duv_ref, dwv_ref, dbv_ref)):
            wt = w[...]
            du = pltpu.roll(d, ext - 1, 0) * wt[0:1] + d * wt[1:2] + pltpu.roll(d, 1, 0) * wt[2:3]
            du_ref[...] = du[HALO:HALO + CONV_TR].astype(BF16)
            d_own = jnp.where(own, d, 0.0)
            dw = jnp.concatenate([
                jnp.sum(pltpu.roll(x, 1, 0) * d_own, axis=0, keepdims=True),
                jnp.sum(x * d_own, axis=0, keepdims=True),
                jnp.sum(pltpu.roll(x, ext - 1, 0) * d_own, axis=0, keepdims=True)], axis=0)
            db = jnp.sum(d_own, axis=0, keepdims=True)

            @pl.when(r == 0)
            def _():
                dw_ref[...] = jnp.zeros_like(dw_ref)
                db_ref[...] = jnp.zeros_like(db_ref)

            dw_ref[...] += dw
            db_ref[...] += db

    same = lambda j: j
    w_spec = pl.BlockSpec((3, CONV_TC), lambda j, r: (0, j))
    b_spec = pl.BlockSpec((1, CONV_TC), lambda j, r: (0, j))
    tile = pl.BlockSpec((CONV_TR, CONV_TC), lambda j, r: (r, j))
    return pl.pallas_call(
        body, name="ffn_mid_bwd", grid=(f // CONV_TC, nr),
        in_specs=(_conv_specs(t, HALO, same) + _conv_specs(t, HALO, same) + _conv_specs(t, HALO16, same)
                  + [w_spec, w_spec, b_spec, b_spec]),
        out_specs=[tile, tile, w_spec, w_spec, b_spec, b_spec],
        out_shape=[jax.ShapeDtypeStruct((t, f), BF16), jax.ShapeDtypeStruct((t, f), BF16),
                   jax.ShapeDtypeStruct((3, f), F32), jax.ShapeDtypeStruct((3, f), F32),
                   jax.ShapeDtypeStruct((1, f), F32), jax.ShapeDtypeStruct((1, f), F32)],
        compiler_params=_params(("parallel", "arbitrary"), VMEM_LIMIT),
    )(u_g, u_g, u_g, u_v, u_v, u_v, da, da, da, cw_g, cw_v, cb_g, cb_v)


def _loss_head(y, target, *, tr=256):
    t, d = y.shape

    def body(y_ref, t_ref, loss_ref, dy_ref, dyb_ref):
        err = y_ref[...] - t_ref[...]
        dy = err * (1.0 / d)
        dy_ref[...] = dy
        dyb_ref[...] = dy.astype(BF16)
        part = 0.5 * jnp.sum(jnp.sum(err * err, axis=-1, keepdims=True) * (1.0 / d), axis=0, keepdims=True)

        @pl.when(pl.program_id(0) == 0)
        def _():
            loss_ref[...] = jnp.zeros_like(loss_ref)

        loss_ref[...] += jnp.broadcast_to(part, loss_ref.shape)

    row = pl.BlockSpec((tr, d), lambda i: (i, 0))
    return pl.pallas_call(
        body, name="loss_head", grid=(t // tr,),
        in_specs=[row, row],
        out_specs=[pl.BlockSpec((1, 128), lambda i: (0, 0)), row, row],
        out_shape=[jax.ShapeDtypeStruct((1, 128), F32), jax.ShapeDtypeStruct((t, d), F32),
                   jax.ShapeDtypeStruct((t, d), BF16)],
        compiler_params=_params(("arbitrary",), VMEM_LIMIT),
    )(y, target)


ANY = pl.BlockSpec(memory_space=pl.ANY)


def _position():
    return lax.axis_index("x"), lax.axis_index("y"), lax.axis_index("c")


def _other_chips(x, y):
    return [(1 - x, y), (x, 1 - y), (1 - x, 1 - y)]


def _all_gather(blocks, *, name):
    na = len(blocks)

    def body(*refs):
        ins, outs = refs[:na], refs[na:2 * na]
        send_sems, recv_sems, local_sems = refs[2 * na:]
        x, y, c = _position()
        sibling = (x, y, 1 - c)
        chips = _other_chips(x, y)

        def index(px, py, pc):
            return 4 * px + 2 * py + pc

        def copy(a, k, block, to, src=None):
            dst = outs[a].at[index(*block)]
            return pltpu.make_async_remote_copy(
                src_ref=dst if src is None else src, dst_ref=dst,
                send_sem=send_sems.at[a, k], recv_sem=recv_sems.at[a, k],
                device_id=to, device_id_type=MESH)

        pending = []
        for a in range(na):
            mine = pltpu.make_async_copy(ins[a], outs[a].at[index(x, y, c)], local_sems.at[a])
            mine.start()
            pending.append(mine)
        first = []
        for a in range(na):
            first.append(copy(a, 0, (x, y, c), sibling, src=ins[a]))
            first += [copy(a, 1 + j, (x, y, c), (*chip, c), src=ins[a]) for j, chip in enumerate(chips)]
        for cp in first:
            cp.start()
        passed = []
        for j, chip in enumerate(chips):
            for a in range(na):
                copy(a, 1 + j, (*chip, c), (x, y, c)).wait_recv()
                fwd = copy(a, 4 + j, (*chip, c), sibling)
                fwd.start()
                passed.append(fwd)
        for a in range(na):
            copy(a, 0, sibling, (x, y, c)).wait_recv()
            for j, chip in enumerate(chips):
                copy(a, 4 + j, (*chip, 1 - c), (x, y, c)).wait_recv()
        for cp in first + passed:
            cp.wait_send()
        for cp in pending:
            cp.wait()

    return pl.pallas_call(
        body, name=name,
        in_specs=[ANY] * na, out_specs=[ANY] * na,
        out_shape=[jax.ShapeDtypeStruct((N_DEV,) + b.shape, b.dtype) for b in blocks],
        scratch_shapes=[pltpu.SemaphoreType.DMA((na, 7)), pltpu.SemaphoreType.DMA((na, 7)),
                        pltpu.SemaphoreType.DMA((na,))],
    )(*blocks)


def _handshake(peers):
    barrier = pltpu.get_barrier_semaphore()
    for peer in peers:
        pl.semaphore_signal(barrier, inc=1, device_id=peer, device_id_type=MESH)
    pl.semaphore_wait(barrier, len(peers))


def _all_gather_sequencer(blocks, *, name, collective_id):
    na = len(blocks)

    def body(*refs):
        ins, outs = refs[:na], refs[na:2 * na]
        send_sems, recv_sems, local_sems = refs[2 * na:]
        x, y, c = _position()
        sibling = (x, y, 1 - c)
        chips = _other_chips(x, y)
        _handshake([sibling] + [(*chip, c) for chip in chips])

        def index(px, py, pc):
            return 4 * px + 2 * py + pc

        def copy(a, k, block, to, src=None):
            dst = outs[a].at[index(*block)]
            return pltpu.make_async_remote_copy(
                src_ref=dst if src is None else src, dst_ref=dst,
                send_sem=send_sems.at[a, k], recv_sem=recv_sems.at[a, k],
                device_id=to, device_id_type=MESH)

        pending = []
        for a in range(na):
            mine = pltpu.make_async_copy(ins[a], outs[a].at[index(x, y, c)], local_sems.at[a])
            mine.start()
            pending.append(mine)
        first = []
        for a in range(na):
            first.append(copy(a, 0, (x, y, c), sibling, src=ins[a]))
            first += [copy(a, 1 + j, (x, y, c), (*chip, c), src=ins[a]) for j, chip in enumerate(chips)]
        for cp in first:
            cp.start()
        passed = []
        for j, chip in enumerate(chips):
            for a in range(na):
                copy(a, 1 + j, (*chip, c), (x, y, c)).wait_recv()
                fwd = copy(a, 4 + j, (*chip, c), sibling)
                fwd.start()
                passed.append(fwd)
        for a in range(na):
            copy(a, 0, sibling, (x, y, c)).wait_recv()
            for j, chip in enumerate(chips):
                copy(a, 4 + j, (*chip, 1 - c), (x, y, c)).wait_recv()
        for cp in first + passed:
            cp.wait_send()
        for cp in pending:
            cp.wait()

    return pl.kernel(
        body, name=name,
        out_type=[jax.ShapeDtypeStruct((N_DEV,) + b.shape, b.dtype) for b in blocks],
        mesh=plsc.ScalarSubcoreMesh(axis_name="sequencer", num_cores=1),
        scratch_types=[pltpu.SemaphoreType.DMA((na, 7)), pltpu.SemaphoreType.DMA((na, 7)),
                       pltpu.SemaphoreType.DMA((na,))],
        compiler_params=pltpu.CompilerParams(collective_id=collective_id),
    )(*blocks)


def _swap_with_sibling(grads, *, name):
    na = len(grads)

    def body(*refs):
        ins, outs = refs[:na], refs[na:2 * na]
        send_sems, recv_sems = refs[2 * na:]
        x, y, c = _position()
        copies = []
        for a in range(na):
            for k in range(N_CHIP):
                cp = pltpu.make_async_remote_copy(
                    src_ref=ins[a].at[k, 1 - c], dst_ref=outs[a].at[k],
                    send_sem=send_sems.at[a, k], recv_sem=recv_sems.at[a, k],
                    device_id=(x, y, 1 - c), device_id_type=MESH)
                cp.start()
                copies.append(cp)
        for cp in copies:
            cp.wait()

    return pl.pallas_call(
        body, name=name,
        in_specs=[ANY] * na, out_specs=[ANY] * na,
        out_shape=[jax.ShapeDtypeStruct((N_CHIP,) + g.shape[2:], g.dtype) for g in grads],
        scratch_shapes=[pltpu.SemaphoreType.DMA((na, N_CHIP)), pltpu.SemaphoreType.DMA((na, N_CHIP))],
    )(*grads)


def _pair_sum(grad, theirs, core, *, tr, name):
    _, _, r, w = grad.shape
    assert r % tr == 0

    def body(core_ref, mine_ref, theirs_ref, out_ref):
        out_ref[...] = mine_ref[...] + theirs_ref[...]

    return pl.pallas_call(
        body, name=name,
        grid_spec=pltpu.PrefetchScalarGridSpec(
            num_scalar_prefetch=1, grid=(N_CHIP, r // tr),
            in_specs=[pl.BlockSpec((None, None, tr, w), lambda k, i, core_ref: (k, core_ref[0], i, 0)),
                      pl.BlockSpec((None, tr, w), lambda k, i, core_ref: (k, i, 0))],
            out_specs=pl.BlockSpec((None, tr, w), lambda k, i, core_ref: (k, i, 0))),
        out_shape=jax.ShapeDtypeStruct((N_CHIP, r, w), F32),
        compiler_params=_params(("parallel", "parallel"), VMEM_LIMIT),
    )(core, grad, theirs)


def _exchange_between_chips(parts, *, name):
    na = len(parts)

    def body(*refs):
        ins, outs = refs[:na], refs[na:2 * na]
        send_sems, recv_sems, local_sems = refs[2 * na:]
        x, y, c = _position()
        me = 2 * x + y
        copies = []
        for a in range(na):
            own = pltpu.make_async_copy(ins[a].at[me], outs[a].at[me], local_sems.at[a])
            own.start()
            copies.append(own)
            for j, (px, py) in enumerate(_other_chips(x, y)):
                cp = pltpu.make_async_remote_copy(
                    src_ref=ins[a].at[2 * px + py], dst_ref=outs[a].at[me],
                    send_sem=send_sems.at[a, j], recv_sem=recv_sems.at[a, j],
                    device_id=(px, py, c), device_id_type=MESH)
                cp.start()
                copies.append(cp)
        for cp in copies:
            cp.wait()

    return pl.pallas_call(
        body, name=name,
        in_specs=[ANY] * na, out_specs=[ANY] * na,
        out_shape=[jax.ShapeDtypeStruct(p.shape, p.dtype) for p in parts],
        scratch_shapes=[pltpu.SemaphoreType.DMA((na, 3)), pltpu.SemaphoreType.DMA((na, 3)),
                        pltpu.SemaphoreType.DMA((na,))],
    )(*parts)


def _adamw(parts, w, m, v, *, tr, name):
    n, r, cols = parts.shape
    assert r % tr == 0 and w.shape == (r, cols)
    c1 = 1.0 - ADAM_B1 ** ADAM_STEP
    c2 = 1.0 - ADAM_B2 ** ADAM_STEP

    def body(p_ref, w_ref, m_ref, v_ref, g_ref, d_ref, nm_ref, nv_ref):
        g = p_ref[0]
        for k in range(1, n):
            g = g + p_ref[k]
        new_m = ADAM_B1 * m_ref[...] + (1.0 - ADAM_B1) * g
        new_v = ADAM_B2 * v_ref[...] + (1.0 - ADAM_B2) * (g * g)
        m_hat = new_m / c1
        v_hat = new_v / c2
        g_ref[...] = g
        d_ref[...] = -ADAM_LR * (m_hat / (jnp.sqrt(v_hat) + ADAM_EPS) + ADAM_WD * w_ref[...])
        nm_ref[...] = new_m
        nv_ref[...] = new_v

    tile = pl.BlockSpec((tr, cols), lambda i: (i, 0))
    shape = jax.ShapeDtypeStruct((r, cols), F32)
    return pl.pallas_call(
        body, name=name, grid=(r // tr,),
        in_specs=[pl.BlockSpec((n, tr, cols), lambda i: (0, i, 0)), tile, tile, tile],
        out_specs=[tile] * 4, out_shape=[shape] * 4,
        compiler_params=_params(("parallel",), VMEM_LIMIT),
    )(parts, w, m, v)


def _rope_tables(t):
    half = HEAD_DIM // 2
    inv = 1.0 / (ROPE_THETA ** (jnp.arange(half, dtype=F32) / half))
    ang = jnp.arange(t, dtype=jnp.int32).astype(F32)[:, None] * inv[None, :]
    cos, sin = jnp.cos(ang), jnp.sin(ang)
    return jnp.concatenate([cos, cos], axis=1), jnp.concatenate([-sin, sin], axis=1)


def _to_kernel_columns(w_full):
    order = sorted(SEGMENTS.values())
    main = jnp.concatenate([w_full[:, src:src + width] for _, src, width in order], axis=1)
    lr = jnp.pad(w_full[:, IN_MAIN:IN_TOTAL], ((0, 0), (0, LR_PAD - (IN_TOTAL - IN_MAIN))))
    return main, lr


def _from_kernel_columns(main, lr):
    by_src = sorted(SEGMENTS.values(), key=lambda s: s[1])
    return jnp.concatenate([main[:, dst:dst + width] for dst, _, width in by_src]
                           + [lr[:, :IN_TOTAL - IN_MAIN]], axis=1)


def _pack(pieces):
    flat = []
    for p in pieces:
        p = p.reshape(-1)
        flat.append(jnp.pad(p, (0, (-p.shape[0]) % 128)))
    return jnp.concatenate(flat).reshape(-1, 128)


def _unpack(packed, shapes):
    flat = packed.reshape(-1)
    out, off = [], 0
    for s in shapes:
        size = 1
        for dim in s:
            size *= dim
        out.append(flat[off:off + size].reshape(s))
        off += size + (-size) % 128
    return out


def _local_step(xs, target, norm1_g, w_main, w_lr, gq, gk, attn_sink, w2, ba2, gla_norm_g, w_out_full, norm2_g,
                w_ug, w_uv, cw_g, cw_v, cb_g, cb_v, w_down_full):
    t = xs.shape[0]
    tm = min(1024, t)
    cos, sin_signed = _rope_tables(t)
    sink = attn_sink.reshape(ATTN_HEADS)

    h1 = _rmsnorm_fwd(xs, norm1_g, name="norm1_fwd")
    proj = _matmul(h1, w_main, "nn", tm=tm, tn=512, tk=D_MODEL, name="proj_main")
    proj_lr = _matmul(h1, w_lr, "nn", tm=tm, tn=LR_PAD, tk=D_MODEL, name="proj_lr")
    qa, ka, va = _attn_prep_fwd(proj, cos, sin_signed, gq, gk)
    o_attn = _attn_fwd(qa, ka, va, sink)
    g_dec = _gla_prep_fwd(proj_lr, w2, ba2)
    o_f, o_b, s_f, s_b = _gla_fwd(proj, g_dec)
    o_gla = _gla_out_fwd(o_f, o_b, proj, gla_norm_g)
    mix = jnp.concatenate([o_attn, o_gla], axis=1)
    x1 = _matmul(mix, w_out_full, "nn", tm=tm, tn=512, tk=D_MODEL, res=xs, name="out_proj")
    h2 = _rmsnorm_fwd(x1, norm2_g, name="norm2_fwd")
    u_g = _matmul(h2, w_ug, "nn", tm=tm, tn=512, tk=D_MODEL, name="up_gate")
    u_v = _matmul(h2, w_uv, "nn", tm=tm, tn=512, tk=D_MODEL, name="up_value")
    act = _ffn_mid_fwd(u_g, u_v, cw_g, cw_v, cb_g, cb_v)
    y = _matmul(act, w_down_full, "nn", tm=tm, tn=512, tk=D_FF // 2, res=x1, name="down_proj")
    loss_part, dy, dy_b = _loss_head(y, target)

    d_act = _matmul(dy_b, w_down_full, "nt", tm=tm, tn=512, tk=D_MODEL, out_dtype=BF16, name="d_act")
    dw_down = _matmul(act, dy_b, "tn", tm=512, tn=1024, tk=t, name="dw_down")
    du_g, du_v, dcw_g, dcw_v, dcb_g, dcb_v = _ffn_mid_bwd(u_g, u_v, cw_g, cw_v, cb_g, cb_v, d_act)
    dh2 = _matmul(du_g, w_ug, "nt", tm=tm, tn=512, tk=D_FF // 2, name="dh2_gate")
    dh2 = _matmul(du_v, w_uv, "nt", tm=tm, tn=512, tk=D_FF // 2, res=dh2, name="dh2_value")
    dw_ug = _matmul(h2, du_g, "tn", tm=1024, tn=512, tk=t, name="dw_up_gate")
    dw_uv = _matmul(h2, du_v, "tn", tm=1024, tn=512, tk=t, name="dw_up_value")
    dx1, dx1_b, d_norm2 = _rmsnorm_bwd(x1, norm2_g, dh2, dy, name="norm2_bwd")
    dmix = _matmul(dx1_b, w_out_full, "nt", tm=tm, tn=512, tk=D_MODEL, name="d_mix")
    dw_out = _matmul(mix, dx1_b, "tn", tm=1024, tn=512, tk=t, name="dw_out")
    do_gla, d_gate, d_gla_norm = _gla_out_bwd(o_f, o_b, proj, gla_norm_g, dmix)
    (dq_f, dk_f, dv_f, dg_f, dq_b, dk_b, dv_b, dg_b) = _gla_bwd(proj, g_dec, s_f, s_b, do_gla)
    d_lr, d_w2, d_ba2 = _gla_prep_bwd(proj_lr, w2, ba2, dg_f, dg_b)
    dqa, dk_lo, dk_mid, dk_hi, dv_lo, dv_mid, dv_hi, d_sink8 = _attn_bwd(qa, ka, va, sink, dmix)
    d_qa, d_ka, d_va, d_qn, d_kn = _attn_prep_bwd(proj, cos, sin_signed, gq, gk, dqa,
                                                  (dk_lo, dk_mid, dk_hi), (dv_lo, dv_mid, dv_hi))
    d_seg = {"qa": d_qa, "gate": d_gate, "vg": (dv_f + dv_b).astype(BF16), "qg": (dq_f + dq_b).astype(BF16),
             "kg": (dk_f + dk_b).astype(BF16), "ka": d_ka, "va": d_va}
    d_proj = jnp.concatenate([d_seg[k] for k in sorted(SEGMENTS, key=lambda k: SEGMENTS[k][0])], axis=1)
    dh1 = _matmul(d_lr, w_lr, "nt", tm=tm, tn=512, tk=LR_PAD, name="dh1_lr")
    dh1 = _matmul(d_proj, w_main, "nt", tm=tm, tn=512, tk=IN_MAIN // 2, res=dh1, name="dh1_main")
    dw_main = _matmul(h1, d_proj, "tn", tm=1024, tn=512, tk=t, name="dw_in_main")
    dw_lr = _matmul(h1, d_lr, "tn", tm=1024, tn=LR_PAD, tk=t, name="dw_in_lr")
    grad_x, _, d_norm1 = _rmsnorm_bwd(xs, norm1_g, dh1, dx1, name="norm1_bwd")
    return (loss_part, grad_x, dw_main, dw_lr, dw_out, dw_ug, dw_uv, dw_down, dcw_g, dcw_v, dcb_g, dcb_v,
            d_w2, d_ba2, d_norm1, d_norm2, d_qn, d_kn, d_sink8, d_gla_norm)


def kernel(x, norm1_g, w_in, attn_q_norm_g, attn_k_norm_g, attn_sink, gla_wa2_fwd, gla_ba_fwd, gla_wa2_bwd, gla_ba_bwd, gla_out_norm_g, w_out, norm2_g, w_up, conv_w, conv_b, w_down, loss_target, m_norm1_g, m_w_in, m_attn_q_norm_g, m_attn_k_norm_g, m_attn_sink, m_gla_wa2_fwd, m_gla_ba_fwd, m_gla_wa2_bwd, m_gla_ba_bwd, m_gla_out_norm_g, m_w_out, m_norm2_g, m_w_up, m_conv_w, m_conv_b, m_w_down, v_norm1_g, v_w_in, v_attn_q_norm_g, v_attn_k_norm_g, v_attn_sink, v_gla_wa2_fwd, v_gla_ba_fwd, v_gla_wa2_bwd, v_gla_ba_bwd, v_gla_out_norm_g, v_w_out, v_norm2_g, v_w_up, v_conv_w, v_conv_b, v_w_down):
    t = x.shape[1]
    xs = x.reshape(t, D_MODEL)
    target = loss_target.reshape(t, D_MODEL)
    core = lax.axis_index("c").astype(jnp.int32).reshape(1)

    sharded_small = [conv_w[0], gla_wa2_fwd[0], gla_wa2_bwd[0]]
    small_shapes = [s.shape for s in sharded_small]
    g_in, g_small = _all_gather([w_in[0].astype(BF16), _pack(sharded_small)], name="gather_w_in")
    g_in, later = lax.optimization_barrier(
        (g_in, [w_out[0].astype(BF16), w_up[0].astype(BF16), w_down[0].astype(BF16)]))
    g_out, g_up, g_down = _all_gather_sequencer(later, name="gather_later_weights", collective_id=1)
    w_in_full = jnp.transpose(g_in, (1, 0, 2)).reshape(D_MODEL, IN_TOTAL)
    w_main, w_lr = _to_kernel_columns(w_in_full)
    w_out_full = g_out.reshape(D_MODEL, D_MODEL)
    w_ug = jnp.transpose(g_up[:N_CHIP], (1, 0, 2)).reshape(D_MODEL, D_FF)
    w_uv = jnp.transpose(g_up[N_CHIP:], (1, 0, 2)).reshape(D_MODEL, D_FF)
    w_down_full = g_down.reshape(D_FF, D_MODEL)
    small_full = [_unpack(g_small[d], small_shapes) for d in range(N_DEV)]
    conv_w_full = jnp.concatenate([s[0] for s in small_full], axis=1)
    wa2_f = jnp.concatenate([s[1] for s in small_full], axis=1)
    wa2_b = jnp.concatenate([s[2] for s in small_full], axis=1)
    cw_g, cw_v = conv_w_full[:, :D_FF], conv_w_full[:, D_FF:]
    cb_g, cb_v = conv_b[:, :D_FF], conv_b[:, D_FF:]
    w2 = jnp.zeros((LR_PAD, 2 * GLA_KEY_WIDTH), F32)
    w2 = w2.at[:GLA_RANK, :GLA_KEY_WIDTH].set(wa2_f).at[GLA_RANK:2 * GLA_RANK, GLA_KEY_WIDTH:].set(wa2_b)
    ba2 = jnp.concatenate([gla_ba_fwd, gla_ba_bwd], axis=1)
    (loss_part, grad_x, dw_main, dw_lr, dw_out, dw_ug, dw_uv, dw_down, dcw_g, dcw_v, dcb_g, dcb_v, d_w2, d_ba2,
     d_norm1, d_norm2, d_qn, d_kn, d_sink8, d_gla_norm) = _local_step(
        xs, target, norm1_g, w_main, w_lr, attn_q_norm_g, attn_k_norm_g, attn_sink, w2, ba2, gla_out_norm_g,
        w_out_full, norm2_g, w_ug, w_uv, cw_g, cw_v, cb_g, cb_v, w_down_full)
    loss = lax.psum(loss_part[0, 0], ("x", "y", "c"))

    dw_in = _from_kernel_columns(dw_main, dw_lr)
    per_in = IN_TOTAL // N_DEV
    per_up = 2 * D_FF // N_DEV
    dconv_w = jnp.concatenate([dcw_g, dcw_v], axis=1)
    d_wa2_f = d_w2[:GLA_RANK, :GLA_KEY_WIDTH]
    d_wa2_b = d_w2[GLA_RANK:2 * GLA_RANK, GLA_KEY_WIDTH:]
    per_wa = GLA_KEY_WIDTH // N_DEV
    small_grad = jnp.stack([
        _pack([dconv_w[:, d * per_up:(d + 1) * per_up], d_wa2_f[:, d * per_wa:(d + 1) * per_wa],
               d_wa2_b[:, d * per_wa:(d + 1) * per_wa]]) for d in range(N_DEV)])
    grads = [
        jnp.transpose(dw_in.reshape(D_MODEL, N_DEV, per_in), (1, 0, 2)),
        dw_out.reshape(N_DEV, D_MODEL // N_DEV, D_MODEL),
        jnp.concatenate([jnp.transpose(dw_ug.reshape(D_MODEL, N_CHIP, per_up), (1, 0, 2)),
                         jnp.transpose(dw_uv.reshape(D_MODEL, N_CHIP, per_up), (1, 0, 2))], axis=0),
        dw_down.reshape(N_DEV, D_FF // N_DEV, D_MODEL),
        small_grad,
    ]
    grads = [g.reshape((N_CHIP, 2) + g.shape[1:]) for g in grads]
    theirs = _swap_with_sibling(grads, name="grad_swap_cores")
    rows = [256, 256, 256, 64, grads[4].shape[2]]
    names = ["w_in", "w_out", "w_up", "w_down", "small"]
    pair = [_pair_sum(g, s, core, tr=r, name="pair_sum_" + n) for g, s, r, n in zip(grads, theirs, rows, names)]
    parts = _exchange_between_chips(pair, name="grad_swap_chips")
    m_small = _pack([m_conv_w[0], m_gla_wa2_fwd[0], m_gla_wa2_bwd[0]])
    v_small = _pack([v_conv_w[0], v_gla_wa2_fwd[0], v_gla_wa2_bwd[0]])
    upd_in = _adamw(parts[0], w_in[0], m_w_in[0], v_w_in[0], tr=256, name="adamw_w_in")
    upd_out = _adamw(parts[1], w_out[0], m_w_out[0], v_w_out[0], tr=256, name="adamw_w_out")
    upd_up = _adamw(parts[2], w_up[0], m_w_up[0], v_w_up[0], tr=256, name="adamw_w_up")
    upd_down = _adamw(parts[3], w_down[0], m_w_down[0], v_w_down[0], tr=64, name="adamw_w_down")
    upd_small = _adamw(parts[4], _pack(sharded_small), m_small, v_small, tr=parts[4].shape[1], name="adamw_small")
    upd_small = [_unpack(u, small_shapes) for u in upd_small]

    rep_names = ["norm1_g", "attn_q_norm_g", "attn_k_norm_g", "attn_sink", "gla_ba_fwd", "gla_ba_bwd",
                 "gla_out_norm_g", "norm2_g", "conv_b"]
    rep_w = [norm1_g, attn_q_norm_g, attn_k_norm_g, attn_sink, gla_ba_fwd, gla_ba_bwd, gla_out_norm_g, norm2_g, conv_b]
    rep_m = [m_norm1_g, m_attn_q_norm_g, m_attn_k_norm_g, m_attn_sink, m_gla_ba_fwd, m_gla_ba_bwd,
             m_gla_out_norm_g, m_norm2_g, m_conv_b]
    rep_v = [v_norm1_g, v_attn_q_norm_g, v_attn_k_norm_g, v_attn_sink, v_gla_ba_fwd, v_gla_ba_bwd,
             v_gla_out_norm_g, v_norm2_g, v_conv_b]
    d_sink = d_sink8[:, :GQA_GROUP, 0].reshape(1, ATTN_HEADS)
    rep_g = [d_norm1, d_qn, d_kn, d_sink, d_ba2[:, :GLA_KEY_WIDTH], d_ba2[:, GLA_KEY_WIDTH:], d_gla_norm, d_norm2,
             jnp.concatenate([dcb_g, dcb_v], axis=1)]
    rep_shapes = [w.shape for w in rep_w]
    (rep_terms,) = _all_gather([_pack(rep_g)], name="gather_small_grads")
    upd_rep = _adamw(rep_terms, _pack(rep_w), _pack(rep_m), _pack(rep_v), tr=rep_terms.shape[1], name="adamw_replicated")
    upd_rep = [_unpack(u, rep_shapes) for u in upd_rep]

    order = ["norm1_g", "w_in", "attn_q_norm_g", "attn_k_norm_g", "attn_sink", "gla_wa2_fwd", "gla_ba_fwd",
             "gla_wa2_bwd", "gla_ba_bwd", "gla_out_norm_g", "w_out", "norm2_g", "w_up", "conv_w", "conv_b", "w_down"]
    outs = [loss, grad_x.reshape(1, t, D_MODEL)]
    for kind in range(4):
        by_name = {n: upd_rep[kind][i] for i, n in enumerate(rep_names)}
        by_name["w_in"] = upd_in[kind][None]
        by_name["w_out"] = upd_out[kind][None]
        by_name["w_up"] = upd_up[kind][None]
        by_name["w_down"] = upd_down[kind][None]
        by_name["conv_w"] = upd_small[kind][0][None]
        by_name["gla_wa2_fwd"] = upd_small[kind][1][None]
        by_name["gla_wa2_bwd"] = upd_small[kind][2][None]
        outs += [by_name[n] for n in order]
    return tuple(outs)
```

```python
import functools

import jax
import jax.numpy as jnp
from jax import lax
from jax.experimental import pallas as pl
from jax.experimental.pallas import tpu as pltpu
from jax.experimental.pallas import tpu_sc as plsc

F32 = jnp.float32
BF16 = jnp.bfloat16

D_MODEL = 2048
HEAD_DIM = 128
ATTN_WIDTH = 1024
ATTN_HEADS = 8
KV_HEADS = 2
GQA_GROUP = 4
KV_WIDTH = KV_HEADS * HEAD_DIM
ATTN_BLOCK = 128
WINDOW = 128
ROPE_THETA = 10000.0
GLA_HEADS = 4
GLA_DK = 128
GLA_DV = 256
GLA_KEY_WIDTH = 512
GLA_WIDTH = 1024
GLA_RANK = 16
GLA_GATE_NORMALIZER = 16.0
GLA_CHUNK = 64
D_FF = 5632
NORM_EPS = 1e-6
IN_TOTAL = 4640
IN_MAIN = 4608
LR_PAD = 128
N_DEV = 8
N_CHIP = 4

ADAM_LR = 0.001
ADAM_B1 = 0.9
ADAM_B2 = 0.999
ADAM_EPS = 1e-08
ADAM_WD = 0.01
ADAM_STEP = 10

SEGMENTS = {
    "qa": (0, 0, 1024),
    "gate": (1024, 3584, 1024),
    "vg": (2048, 2560, 1024),
    "qg": (3072, 1536, 512),
    "kg": (3584, 2048, 512),
    "ka": (4096, 1024, 256),
    "va": (4352, 1280, 256),
}

VMEM_LIMIT = 56 * 1024 * 1024
MESH = pl.DeviceIdType.MESH


def _params(semantics=None, vmem=None):
    return pltpu.CompilerParams(dimension_semantics=semantics, vmem_limit_bytes=vmem)


_DIMS = {
    "nn": (((1,), (0,)), ((), ())),
    "nt": (((1,), (1,)), ((), ())),
    "tn": (((0,), (0,)), ((), ())),
}


def _mxu(a, b, mode):
    return lax.dot_general(a.astype(BF16), b.astype(BF16), _DIMS[mode], preferred_element_type=F32)


@functools.partial(jax.custom_vjp, nondiff_argnums=(2,))
def bdot(a, b, mode):
    return _mxu(a, b, mode)


def _bdot_fwd(a, b, mode):
    return _mxu(a, b, mode), (a, b)


def _bdot_bwd(mode, res, g):
    a, b = res
    if mode == "nn":
        return _mxu(g, b, "nt"), _mxu(a, g, "tn")
    if mode == "nt":
        return _mxu(g, b, "nn"), _mxu(g, a, "tn")
    return _mxu(b, g, "nt"), _mxu(a, g, "nn")


bdot.defvjp(_bdot_fwd, _bdot_bwd)


def _rms(x, g):
    return x * lax.rsqrt(jnp.mean(x * x, axis=-1, keepdims=True) + NORM_EPS) * g


def _rope(x, cos, sin_signed):
    return x * cos + pltpu.roll(x, HEAD_DIM // 2, 1) * sin_signed


def _rope_transposed(d, cos, sin_signed):
    return d * cos + pltpu.roll(d * sin_signed, HEAD_DIM // 2, 1)


def _silu(x):
    return x * jax.nn.sigmoid(x)


def _log_sigmoid(z):
    return -(jnp.maximum(-z, 0.0) + jnp.log(1.0 + jnp.exp(-jnp.abs(z))))


def _matmul(a, b, mode, *, tm, tn, tk, out_dtype=F32, res=None, name):
    if mode == "nn":
        (m, k), (k2, n) = a.shape, b.shape
    elif mode == "nt":
        (m, k), (n, k2) = a.shape, b.shape
    else:
        (k, m), (k2, n) = a.shape, b.shape
    assert k == k2 and m % tm == 0 and n % tn == 0 and k % tk == 0, (name, a.shape, b.shape, tm, tn, tk)
    nk = k // tk
    if mode == "tn":
        a_spec = pl.BlockSpec((tk, tm), lambda i, j, kk: (kk, i))
    else:
        a_spec = pl.BlockSpec((tm, tk), lambda i, j, kk: (i, kk))
    if mode == "nt":
        b_spec = pl.BlockSpec((tn, tk), lambda i, j, kk: (j, kk))
    else:
        b_spec = pl.BlockSpec((tk, tn), lambda i, j, kk: (kk, j))
    o_spec = pl.BlockSpec((tm, tn), lambda i, j, kk: (i, j))
    in_specs = [a_spec, b_spec]
    args = [a, b]
    if res is not None:
        in_specs.append(o_spec)
        args.append(res)
    dims = _DIMS[mode]

    def body(*refs):
        if res is not None:
            a_ref, b_ref, r_ref, o_ref = refs[:4]
            rest = refs[4:]
        else:
            a_ref, b_ref, o_ref = refs[:3]
            r_ref = None
            rest = refs[3:]
        part = lax.dot_general(a_ref[...], b_ref[...], dims, preferred_element_type=F32)

        def finish(acc):
            if r_ref is not None:
                acc = acc + r_ref[...]
            o_ref[...] = acc.astype(out_dtype)

        if nk == 1:
            finish(part)
        else:
            acc_ref = rest[0]
            kk = pl.program_id(2)

            @pl.when(kk == 0)
            def _():
                acc_ref[...] = part

            @pl.when(kk > 0)
            def _():
                acc_ref[...] += part

            @pl.when(kk == nk - 1)
            def _():
                finish(acc_ref[...])

    return pl.pallas_call(
        body,
        name=name,
        grid=(m // tm, n // tn, nk),
        in_specs=in_specs,
        out_specs=o_spec,
        out_shape=jax.ShapeDtypeStruct((m, n), out_dtype),
        scratch_shapes=[pltpu.VMEM((tm, tn), F32)] if nk > 1 else [],
        compiler_params=_params(("parallel", "parallel", "arbitrary"), VMEM_LIMIT),
    )(*args)


def _rmsnorm_fwd(x, g, *, name, tr=512):
    t, d = x.shape

    def body(x_ref, g_ref, h_ref):
        h_ref[...] = _rms(x_ref[...], g_ref[...]).astype(BF16)

    return pl.pallas_call(
        body, name=name, grid=(t // tr,),
        in_specs=[pl.BlockSpec((tr, d), lambda i: (i, 0)), pl.BlockSpec((1, d), lambda i: (0, 0))],
        out_specs=pl.BlockSpec((tr, d), lambda i: (i, 0)),
        out_shape=jax.ShapeDtypeStruct((t, d), BF16),
        compiler_params=_params(("parallel",), VMEM_LIMIT),
    )(x, g)


def _rmsnorm_bwd(x, g, dh, dres, *, name, tr=256):
    t, d = x.shape

    def body(x_ref, g_ref, dh_ref, dres_ref, dx_ref, dxb_ref, dg_ref):
        _, vjp = jax.vjp(_rms, x_ref[...], g_ref[...])
        dx, dg = vjp(dh_ref[...])
        dx = dx + dres_ref[...]
        dx_ref[...] = dx
        dxb_ref[...] = dx.astype(BF16)

        @pl.when(pl.program_id(0) == 0)
        def _():
            dg_ref[...] = jnp.zeros_like(dg_ref)

        dg_ref[...] += dg

    row = pl.BlockSpec((tr, d), lambda i: (i, 0))
    vec = pl.BlockSpec((1, d), lambda i: (0, 0))
    return pl.pallas_call(
        body, name=name, grid=(t // tr,),
        in_specs=[row, vec, row, row],
        out_specs=[row, row, vec],
        out_shape=[jax.ShapeDtypeStruct((t, d), F32), jax.ShapeDtypeStruct((t, d), BF16),
                   jax.ShapeDtypeStruct((1, d), F32)],
        compiler_params=_params(("arbitrary",), VMEM_LIMIT),
    )(x, g, dh, dres)


def _seg_block(name, width):
    off = SEGMENTS[name][0]
    assert off % width == 0
    return off // width


def _attn_prep_fwd(proj, cos, sin_signed, gq, gk, *, tr=256):
    t = proj.shape[0]

    def body(q_ref, k_ref, v_ref, cos_ref, sin_ref, gq_ref, gk_ref, qo_ref, ko_ref, vo_ref):
        cos_t, sin_t = cos_ref[...], sin_ref[...]
        for h in range(ATTN_HEADS):
            cols = slice(h * HEAD_DIM, (h + 1) * HEAD_DIM)
            qo_ref[:, cols] = _rope(_rms(q_ref[:, cols], gq_ref[...]), cos_t, sin_t).astype(BF16)
        for h in range(KV_HEADS):
            cols = slice(h * HEAD_DIM, (h + 1) * HEAD_DIM)
            ko_ref[:, cols] = _rope(_rms(k_ref[:, cols], gk_ref[...]), cos_t, sin_t).astype(BF16)
        vo_ref[...] = v_ref[...].astype(BF16)

    qb, kb, vb = _seg_block("qa", ATTN_WIDTH), _seg_block("ka", KV_WIDTH), _seg_block("va", KV_WIDTH)
    tab = pl.BlockSpec((tr, HEAD_DIM), lambda i: (i, 0))
    vec = pl.BlockSpec((1, HEAD_DIM), lambda i: (0, 0))
    return pl.pallas_call(
        body, name="attn_prep_fwd", grid=(t // tr,),
        in_specs=[pl.BlockSpec((tr, ATTN_WIDTH), lambda i: (i, qb)),
                  pl.BlockSpec((tr, KV_WIDTH), lambda i: (i, kb)),
                  pl.BlockSpec((tr, KV_WIDTH), lambda i: (i, vb)),
                  tab, tab, vec, vec],
        out_specs=[pl.BlockSpec((tr, ATTN_WIDTH), lambda i: (i, 0)),
                   pl.BlockSpec((tr, KV_WIDTH), lambda i: (i, 0)),
                   pl.BlockSpec((tr, KV_WIDTH), lambda i: (i, 0))],
        out_shape=[jax.ShapeDtypeStruct((t, ATTN_WIDTH), BF16),
                   jax.ShapeDtypeStruct((t, KV_WIDTH), BF16),
                   jax.ShapeDtypeStruct((t, KV_WIDTH), BF16)],
        compiler_params=_params(("parallel",), VMEM_LIMIT),
    )(proj, proj, proj, cos, sin_signed, gq, gk)


def _attn_head(q, kcat, vcat, sink_col, valid):
    s = bdot(q, kcat, "nt") * (HEAD_DIM ** -0.5)
    s = jnp.where(valid, s, -jnp.inf)
    m = lax.stop_gradient(jnp.maximum(jnp.max(s, axis=-1, keepdims=True), sink_col))
    p = jnp.exp(s - m)
    p = p / (jnp.sum(p, axis=-1, keepdims=True) + jnp.exp(sink_col - m))
    return bdot(p, vcat, "nn")


def _attn_valid(n, t):
    qi = lax.broadcasted_iota(jnp.int32, (ATTN_BLOCK, 3 * ATTN_BLOCK), 0)
    sj = lax.broadcasted_iota(jnp.int32, (ATTN_BLOCK, 3 * ATTN_BLOCK), 1)
    kpos = n * ATTN_BLOCK - ATTN_BLOCK + sj
    return (jnp.abs(sj - ATTN_BLOCK - qi) <= WINDOW) & (kpos >= 0) & (kpos < t)


def _attn_specs(nb):
    q_spec = pl.BlockSpec((ATTN_BLOCK, GQA_GROUP * HEAD_DIM), lambda h, n: (n, h))
    kv_specs = [
        pl.BlockSpec((ATTN_BLOCK, HEAD_DIM), lambda h, n: (jnp.maximum(n - 1, 0), h)),
        pl.BlockSpec((ATTN_BLOCK, HEAD_DIM), lambda h, n: (n, h)),
        pl.BlockSpec((ATTN_BLOCK, HEAD_DIM), lambda h, n: (jnp.minimum(n + 1, nb - 1), h)),
    ]
    return q_spec, kv_specs


def _attn_fwd(q, k, v, sink):
    t = q.shape[0]
    nb = t // ATTN_BLOCK

    def body(sink_ref, q_ref, kp_ref, kc_ref, kn_ref, vp_ref, vc_ref, vn_ref, o_ref):
        h, n = pl.program_id(0), pl.program_id(1)
        valid = _attn_valid(n, t)
        kcat = jnp.concatenate([kp_ref[...], kc_ref[...], kn_ref[...]], axis=0).astype(F32)
        vcat = jnp.concatenate([vp_ref[...], vc_ref[...], vn_ref[...]], axis=0).astype(F32)
        for g in range(GQA_GROUP):
            cols = slice(g * HEAD_DIM, (g + 1) * HEAD_DIM)
            sink_col = jnp.full((ATTN_BLOCK, 1), sink_ref[h * GQA_GROUP + g], F32)
            o_ref[:, cols] = _attn_head(q_ref[:, cols].astype(F32), kcat, vcat, sink_col, valid).astype(BF16)

    q_spec, kv_specs = _attn_specs(nb)
    return pl.pallas_call(
        body, name="attn_fwd", grid=(KV_HEADS, nb),
        in_specs=[pl.BlockSpec(memory_space=pltpu.SMEM), q_spec] + kv_specs + kv_specs,
        out_specs=q_spec,
        out_shape=jax.ShapeDtypeStruct((t, ATTN_WIDTH), BF16),
        compiler_params=_params(("parallel", "parallel"), VMEM_LIMIT),
    )(sink, q, k, k, k, v, v, v)


def _attn_bwd(q, k, v, sink, dmix):
    t = q.shape[0]
    nb = t // ATTN_BLOCK

    def body(sink_ref, q_ref, kp_ref, kc_ref, kn_ref, vp_ref, vc_ref, vn_ref, do_ref,
             dq_ref, dk_lo, dk_mid, dk_hi, dv_lo, dv_mid, dv_hi, dsink_ref):
        h, n = pl.program_id(0), pl.program_id(1)
        valid = _attn_valid(n, t)
        kcat = jnp.concatenate([kp_ref[...], kc_ref[...], kn_ref[...]], axis=0).astype(F32)
        vcat = jnp.concatenate([vp_ref[...], vc_ref[...], vn_ref[...]], axis=0).astype(F32)
        dk = jnp.zeros((3 * ATTN_BLOCK, HEAD_DIM), F32)
        dv = jnp.zeros((3 * ATTN_BLOCK, HEAD_DIM), F32)
        row = lax.broadcasted_iota(jnp.int32, (8, HEAD_DIM), 0)
        dsink = jnp.zeros((8, HEAD_DIM), F32)
        for g in range(GQA_GROUP):
            cols = slice(g * HEAD_DIM, (g + 1) * HEAD_DIM)
            sink_col = jnp.full((ATTN_BLOCK, 1), sink_ref[h * GQA_GROUP + g], F32)
            _, vjp = jax.vjp(functools.partial(_attn_head, valid=valid),
                             q_ref[:, cols].astype(F32), kcat, vcat, sink_col)
            dq_g, dk_g, dv_g, dsink_g = vjp(do_ref[:, cols])
            dq_ref[:, cols] = dq_g
            dk, dv = dk + dk_g, dv + dv_g
            dsink = dsink + jnp.where(row == g, jnp.sum(dsink_g), 0.0)
        for i, (dk_ref, dv_ref) in enumerate(((dk_lo, dv_lo), (dk_mid, dv_mid), (dk_hi, dv_hi))):
            rows = slice(i * ATTN_BLOCK, (i + 1) * ATTN_BLOCK)
            dk_ref[...] = dk[rows]
            dv_ref[...] = dv[rows]

        @pl.when(n == 0)
        def _():
            dsink_ref[...] = jnp.zeros_like(dsink_ref)

        dsink_ref[...] += dsink

    q_spec, kv_specs = _attn_specs(nb)
    kv_out = pl.BlockSpec((ATTN_BLOCK, HEAD_DIM), lambda h, n: (n, h))
    kv_shape = jax.ShapeDtypeStruct((t, KV_WIDTH), F32)
    return pl.pallas_call(
        body, name="attn_bwd", grid=(KV_HEADS, nb),
        in_specs=[pl.BlockSpec(memory_space=pltpu.SMEM), q_spec] + kv_specs + kv_specs + [q_spec],
        out_specs=[q_spec] + [kv_out] * 6 + [pl.BlockSpec((None, 8, HEAD_DIM), lambda h, n: (h, 0, 0))],
        out_shape=[jax.ShapeDtypeStruct((t, ATTN_WIDTH), F32)] + [kv_shape] * 6
                  + [jax.ShapeDtypeStruct((KV_HEADS, 8, HEAD_DIM), F32)],
        compiler_params=_params(("parallel", "arbitrary"), VMEM_LIMIT),
    )(sink, q, k, k, k, v, v, v, dmix)


def _attn_prep_bwd(proj, cos, sin_signed, gq, gk, dq, dks, dvs):
    t = proj.shape[0]
    tr = ATTN_BLOCK
    nb = t // tr

    def body(q_ref, k_ref, cos_ref, sin_ref, gq_ref, gk_ref, dq_ref,
             dk_lo, dk_mid, dk_hi, dv_lo, dv_mid, dv_hi,
             dqo_ref, dko_ref, dvo_ref, dgq_ref, dgk_ref):
        n = pl.program_id(0)
        cos_t, sin_t = cos_ref[...], sin_ref[...]
        has_next = (n < nb - 1).astype(F32)
        has_prev = (n > 0).astype(F32)
        dk = dk_lo[...] * has_next + dk_mid[...] + dk_hi[...] * has_prev
        dv = dv_lo[...] * has_next + dv_mid[...] + dv_hi[...] * has_prev
        dvo_ref[...] = dv.astype(BF16)
        dgq = jnp.zeros((1, HEAD_DIM), F32)
        dgk = jnp.zeros((1, HEAD_DIM), F32)
        for h in range(ATTN_HEADS):
            cols = slice(h * HEAD_DIM, (h + 1) * HEAD_DIM)
            _, vjp = jax.vjp(_rms, q_ref[:, cols], gq_ref[...])
            dx, dg = vjp(_rope_transposed(dq_ref[:, cols], cos_t, sin_t))
            dqo_ref[:, cols] = dx.astype(BF16)
            dgq = dgq + dg
        for h in range(KV_HEADS):
            cols = slice(h * HEAD_DIM, (h + 1) * HEAD_DIM)
            _, vjp = jax.vjp(_rms, k_ref[:, cols], gk_ref[...])
            dx, dg = vjp(_rope_transposed(dk[:, cols], cos_t, sin_t))
            dko_ref[:, cols] = dx.astype(BF16)
            dgk = dgk + dg

        @pl.when(n == 0)
        def _():
            dgq_ref[...] = jnp.zeros_like(dgq_ref)
            dgk_ref[...] = jnp.zeros_like(dgk_ref)

        dgq_ref[...] += dgq
        dgk_ref[...] += dgk

    qb, kb = _seg_block("qa", ATTN_WIDTH), _seg_block("ka", KV_WIDTH)
    tab = pl.BlockSpec((tr, HEAD_DIM), lambda i: (i, 0))
    vec = pl.BlockSpec((1, HEAD_DIM), lambda i: (0, 0))
    kv = [pl.BlockSpec((tr, KV_WIDTH), lambda i: (jnp.minimum(i + 1, nb - 1), 0)),
          pl.BlockSpec((tr, KV_WIDTH), lambda i: (i, 0)),
          pl.BlockSpec((tr, KV_WIDTH), lambda i: (jnp.maximum(i - 1, 0), 0))]
    wide = pl.BlockSpec((tr, ATTN_WIDTH), lambda i: (i, 0))
    narrow = pl.BlockSpec((tr, KV_WIDTH), lambda i: (i, 0))
    return pl.pallas_call(
        body, name="attn_prep_bwd", grid=(nb,),
        in_specs=[pl.BlockSpec((tr, ATTN_WIDTH), lambda i: (i, qb)),
                  pl.BlockSpec((tr, KV_WIDTH), lambda i: (i, kb)),
                  tab, tab, vec, vec, wide] + kv + kv,
        out_specs=[wide, narrow, narrow, vec, vec],
        out_shape=[jax.ShapeDtypeStruct((t, ATTN_WIDTH), BF16),
                   jax.ShapeDtypeStruct((t, KV_WIDTH), BF16),
                   jax.ShapeDtypeStruct((t, KV_WIDTH), BF16),
                   jax.ShapeDtypeStruct((1, HEAD_DIM), F32),
                   jax.ShapeDtypeStruct((1, HEAD_DIM), F32)],
        compiler_params=_params(("arbitrary",), VMEM_LIMIT),
    )(proj, proj, cos, sin_signed, gq, gk, dq, *dks, *dvs)


def _decay_fn(lr, w2, ba):
    return _log_sigmoid(bdot(lr, w2, "nn") + ba) / GLA_GATE_NORMALIZER


def _gla_prep_fwd(proj_lr, w2, ba2, *, tr=512):
    t = proj_lr.shape[0]
    width = 2 * GLA_KEY_WIDTH

    def body(lr_ref, w2_ref, ba_ref, g_ref):
        g_ref[...] = _decay_fn(lr_ref[...], w2_ref[...], ba_ref[...])

    return pl.pallas_call(
        body, name="gla_prep_fwd", grid=(t // tr,),
        in_specs=[pl.BlockSpec((tr, LR_PAD), lambda i: (i, 0)),
                  pl.BlockSpec((LR_PAD, width), lambda i: (0, 0)),
                  pl.BlockSpec((1, width), lambda i: (0, 0))],
        out_specs=pl.BlockSpec((tr, width), lambda i: (i, 0)),
        out_shape=jax.ShapeDtypeStruct((t, width), F32),
        compiler_params=_params(("parallel",), VMEM_LIMIT),
    )(proj_lr, w2, ba2)


def _gla_prep_bwd(proj_lr, w2, ba2, dg_f, dg_b, *, tr=512):
    t = proj_lr.shape[0]
    width = 2 * GLA_KEY_WIDTH

    def body(lr_ref, w2_ref, ba_ref, dgf_ref, dgb_ref, dlr_ref, dw2_ref, dba_ref):
        _, vjp = jax.vjp(_decay_fn, lr_ref[...], w2_ref[...], ba_ref[...])
        dlr, dw2, dba = vjp(jnp.concatenate([dgf_ref[...], dgb_ref[...]], axis=1))
        dlr_ref[...] = dlr.astype(BF16)

        @pl.when(pl.program_id(0) == 0)
        def _():
            dw2_ref[...] = jnp.zeros_like(dw2_ref)
            dba_ref[...] = jnp.zeros_like(dba_ref)

        dw2_ref[...] += dw2
        dba_ref[...] += dba

    half = pl.BlockSpec((tr, GLA_KEY_WIDTH), lambda i: (i, 0))
    return pl.pallas_call(
        body, name="gla_prep_bwd", grid=(t // tr,),
        in_specs=[pl.BlockSpec((tr, LR_PAD), lambda i: (i, 0)),
                  pl.BlockSpec((LR_PAD, width), lambda i: (0, 0)),
                  pl.BlockSpec((1, width), lambda i: (0, 0)), half, half],
        out_specs=[pl.BlockSpec((tr, LR_PAD), lambda i: (i, 0)),
                   pl.BlockSpec((LR_PAD, width), lambda i: (0, 0)),
                   pl.BlockSpec((1, width), lambda i: (0, 0))],
        out_shape=[jax.ShapeDtypeStruct((t, LR_PAD), BF16),
                   jax.ShapeDtypeStruct((LR_PAD, width), F32),
                   jax.ShapeDtypeStruct((1, width), F32)],
        compiler_params=_params(("arbitrary",), VMEM_LIMIT),
    )(proj_lr, w2, ba2, dg_f, dg_b)


def _gla_chunk(q, k, v, g, state, forward):
    c = GLA_CHUNK
    row = lax.broadcasted_iota(jnp.int32, (c, c), 0)
    col = lax.broadcasted_iota(jnp.int32, (c, c), 1)
    rid = lax.broadcasted_iota(jnp.int32, (c, GLA_DK), 0)
    q = q * (GLA_DK ** -0.5)
    if forward:
        cum = (row >= col).astype(F32)
        see = row >= col
        upto_ref = rid <= c // 2
    else:
        cum = (row <= col).astype(F32)
        see = row < col
        upto_ref = rid >= c - 1 - c // 2
    b = lax.dot_general(cum, g, _DIMS["nn"], precision=lax.Precision.HIGHEST, preferred_element_type=F32)
    b_last = jnp.sum(g, axis=0, keepdims=True)
    b_ref = jnp.sum(jnp.where(upto_ref, g, 0.0), axis=0, keepdims=True)
    a = bdot(q * jnp.exp(b - b_ref), k * jnp.exp(b_ref - b), "nt")
    a = jnp.where(see, a, 0.0)
    o = bdot(a, v, "nn") + bdot(q * jnp.exp(b), state, "nt")
    new_state = state * jnp.exp(b_last) + bdot(v, k * jnp.exp(b_last - b), "tn")
    return o, new_state


def _gla_fwd(proj, g):
    t = proj.shape[0]
    c = GLA_CHUNK
    nchunk = t // c
    qb, kb, vb = _seg_block("qg", GLA_KEY_WIDTH), _seg_block("kg", GLA_KEY_WIDTH), _seg_block("vg", GLA_WIDTH)

    def body(qf, kf, vf, gf, qr, kr, vr, gr, of_ref, ob_ref, sf_ref, sb_ref, state):
        @pl.when(pl.program_id(0) == 0)
        def _():
            state[...] = jnp.zeros_like(state)

        for d, (q_ref, k_ref, v_ref, g_ref, o_ref, s_ref) in enumerate(
                ((qf, kf, vf, gf, of_ref, sf_ref), (qr, kr, vr, gr, ob_ref, sb_ref))):
            for h in range(GLA_HEADS):
                kc = slice(h * GLA_DK, (h + 1) * GLA_DK)
                vc = slice(h * GLA_DV, (h + 1) * GLA_DV)
                s_in = state[d, h]
                s_ref[h] = s_in
                o, s_out = _gla_chunk(q_ref[:, kc], k_ref[:, kc], v_ref[:, vc], g_ref[:, kc], s_in, d == 0)
                o_ref[:, vc] = o
                state[d, h] = s_out

    specs, outs = [], []
    for d in range(2):
        ci = (lambda i: i) if d == 0 else (lambda i: nchunk - 1 - i)
        specs += [pl.BlockSpec((c, GLA_KEY_WIDTH), lambda i, ci=ci: (ci(i), qb)),
                  pl.BlockSpec((c, GLA_KEY_WIDTH), lambda i, ci=ci: (ci(i), kb)),
                  pl.BlockSpec((c, GLA_WIDTH), lambda i, ci=ci: (ci(i), vb)),
                  pl.BlockSpec((c, GLA_KEY_WIDTH), lambda i, ci=ci, d=d: (ci(i), d))]
        outs.append(pl.BlockSpec((c, GLA_WIDTH), lambda i, ci=ci: (ci(i), 0)))
    for d in range(2):
        ci = (lambda i: i) if d == 0 else (lambda i: nchunk - 1 - i)
        outs.append(pl.BlockSpec((None, GLA_HEADS, GLA_DV, GLA_DK), lambda i, ci=ci: (ci(i), 0, 0, 0)))
    o_shape = jax.ShapeDtypeStruct((t, GLA_WIDTH), F32)
    s_shape = jax.ShapeDtypeStruct((nchunk, GLA_HEADS, GLA_DV, GLA_DK), F32)
    return pl.pallas_call(
        body, name="gla_fwd", grid=(nchunk,),
        in_specs=specs, out_specs=outs,
        out_shape=[o_shape, o_shape, s_shape, s_shape],
        scratch_shapes=[pltpu.VMEM((2, GLA_HEADS, GLA_DV, GLA_DK), F32)],
        compiler_params=_params(("arbitrary",), VMEM_LIMIT),
    )(proj, proj, proj, g, proj, proj, proj, g)


def _gla_bwd(proj, g, s_f, s_b, do):
    t = proj.shape[0]
    c = GLA_CHUNK
    nchunk = t // c
    qb, kb, vb = _seg_block("qg", GLA_KEY_WIDTH), _seg_block("kg", GLA_KEY_WIDTH), _seg_block("vg", GLA_WIDTH)

    def body(*refs):
        ins, outs, dstate = refs[:12], refs[12:20], refs[20]

        @pl.when(pl.program_id(0) == 0)
        def _():
            dstate[...] = jnp.zeros_like(dstate)

        for d in range(2):
            q_ref, k_ref, v_ref, g_ref, s_ref, do_ref = ins[6 * d:6 * d + 6]
            dq_ref, dk_ref, dv_ref, dg_ref = outs[4 * d:4 * d + 4]
            for h in range(GLA_HEADS):
                kc = slice(h * GLA_DK, (h + 1) * GLA_DK)
                vc = slice(h * GLA_DV, (h + 1) * GLA_DV)
                _, vjp = jax.vjp(functools.partial(_gla_chunk, forward=(d == 0)),
                                 q_ref[:, kc], k_ref[:, kc], v_ref[:, vc], g_ref[:, kc], s_ref[h])
                dq, dk, dv, dg, ds = vjp((do_ref[:, vc], dstate[d, h]))
                dq_ref[:, kc] = dq
                dk_ref[:, kc] = dk
                dv_ref[:, vc] = dv
                dg_ref[:, kc] = dg
                dstate[d, h] = ds

    specs, outs, shapes = [], [], []
    for d in range(2):
        ci = (lambda i: nchunk - 1 - i) if d == 0 else (lambda i: i)
        specs += [pl.BlockSpec((c, GLA_KEY_WIDTH), lambda i, ci=ci: (ci(i), qb)),
                  pl.BlockSpec((c, GLA_KEY_WIDTH), lambda i, ci=ci: (ci(i), kb)),
                  pl.BlockSpec((c, GLA_WIDTH), lambda i, ci=ci: (ci(i), vb)),
                  pl.BlockSpec((c, GLA_KEY_WIDTH), lambda i, ci=ci, d=d: (ci(i), d)),
                  pl.BlockSpec((None, GLA_HEADS, GLA_DV, GLA_DK), lambda i, ci=ci: (ci(i), 0, 0, 0)),
                  pl.BlockSpec((c, GLA_WIDTH), lambda i, ci=ci: (ci(i), 0))]
        key = pl.BlockSpec((c, GLA_KEY_WIDTH), lambda i, ci=ci: (ci(i), 0))
        val = pl.BlockSpec((c, GLA_WIDTH), lambda i, ci=ci: (ci(i), 0))
        outs += [key, key, val, key]
        shapes += [jax.ShapeDtypeStruct((t, GLA_KEY_WIDTH), F32), jax.ShapeDtypeStruct((t, GLA_KEY_WIDTH), F32),
                   jax.ShapeDtypeStruct((t, GLA_WIDTH), F32), jax.ShapeDtypeStruct((t, GLA_KEY_WIDTH), F32)]
    return pl.pallas_call(
        body, name="gla_bwd", grid=(nchunk,),
        in_specs=specs, out_specs=outs, out_shape=shapes,
        scratch_shapes=[pltpu.VMEM((2, GLA_HEADS, GLA_DV, GLA_DK), F32)],
        compiler_params=_params(("arbitrary",), VMEM_LIMIT),
    )(proj, proj, proj, g, s_f, do, proj, proj, proj, g, s_b, do)


def _gla_out_head(o_f, o_b, gate, gn):
    return _rms(o_f + o_b, gn) * _silu(gate)


def _gla_out_fwd(o_f, o_b, proj, gn, *, tr=256):
    t = o_f.shape[0]
    gb = _seg_block("gate", GLA_WIDTH)

    def body(of_ref, ob_ref, gate_ref, gn_ref, out_ref):
        for h in range(GLA_HEADS):
            vc = slice(h * GLA_DV, (h + 1) * GLA_DV)
            out_ref[:, vc] = _gla_out_head(of_ref[:, vc], ob_ref[:, vc], gate_ref[:, vc], gn_ref[...]).astype(BF16)

    wide = pl.BlockSpec((tr, GLA_WIDTH), lambda i: (i, 0))
    return pl.pallas_call(
        body, name="gla_out_fwd", grid=(t // tr,),
        in_specs=[wide, wide, pl.BlockSpec((tr, GLA_WIDTH), lambda i: (i, gb)),
                  pl.BlockSpec((1, GLA_DV), lambda i: (0, 0))],
        out_specs=wide,
        out_shape=jax.ShapeDtypeStruct((t, GLA_WIDTH), BF16),
        compiler_params=_params(("parallel",), VMEM_LIMIT),
    )(o_f, o_b, proj, gn)


def _gla_out_bwd(o_f, o_b, proj, gn, dmix, *, tr=256):
    t = o_f.shape[0]
    gb = _seg_block("gate", GLA_WIDTH)

    def body(of_ref, ob_ref, gate_ref, gn_ref, dout_ref, do_ref, dgate_ref, dgn_ref):
        dgn = jnp.zeros((1, GLA_DV), F32)
        for h in range(GLA_HEADS):
            vc = slice(h * GLA_DV, (h + 1) * GLA_DV)
            _, vjp = jax.vjp(_gla_out_head, of_ref[:, vc], ob_ref[:, vc], gate_ref[:, vc], gn_ref[...])
            do, _, dgate, dg = vjp(dout_ref[:, vc])
            do_ref[:, vc] = do
            dgate_ref[:, vc] = dgate.astype(BF16)
            dgn = dgn + dg

        @pl.when(pl.program_id(0) == 0)
        def _():
            dgn_ref[...] = jnp.zeros_like(dgn_ref)

        dgn_ref[...] += dgn

    wide = pl.BlockSpec((tr, GLA_WIDTH), lambda i: (i, 0))
    vec = pl.BlockSpec((1, GLA_DV), lambda i: (0, 0))
    return pl.pallas_call(
        body, name="gla_out_bwd", grid=(t // tr,),
        in_specs=[wide, wide, pl.BlockSpec((tr, GLA_WIDTH), lambda i: (i, gb)), vec,
                  pl.BlockSpec((tr, GLA_WIDTH), lambda i: (i, 1))],
        out_specs=[wide, wide, vec],
        out_shape=[jax.ShapeDtypeStruct((t, GLA_WIDTH), F32), jax.ShapeDtypeStruct((t, GLA_WIDTH), BF16),
                   jax.ShapeDtypeStruct((1, GLA_DV), F32)],
        compiler_params=_params(("arbitrary",), VMEM_LIMIT),
    )(o_f, o_b, proj, gn, dmix)


CONV_TR = 512
CONV_TC = 256
HALO = 8
HALO16 = 16


def _conv3(u, w, b):
    n = u.shape[0]
    return pltpu.roll(u, 1, 0) * w[0:1] + u * w[1:2] + pltpu.roll(u, n - 1, 0) * w[2:3] + b


def _conv_ext(main_ref, prev_ref, next_ref, r, nr):
    prev = prev_ref[...].astype(F32)[-HALO:] * (r > 0).astype(F32)
    nxt = next_ref[...].astype(F32)[:HALO] * (r < nr - 1).astype(F32)
    return jnp.concatenate([prev, main_ref[...].astype(F32), nxt], axis=0)


def _conv_specs(t, halo, col_of):
    per = CONV_TR // halo
    last = t // halo - 1
    return [pl.BlockSpec((CONV_TR, CONV_TC), lambda j, r: (r, col_of(j))),
            pl.BlockSpec((halo, CONV_TC), lambda j, r: (jnp.maximum(r * per - 1, 0), col_of(j))),
            pl.BlockSpec((halo, CONV_TC), lambda j, r: (jnp.minimum((r + 1) * per, last), col_of(j)))]


def _ffn_mid_fwd(u_g, u_v, cw_g, cw_v, cb_g, cb_v):
    t, f = u_g.shape
    nr = t // CONV_TR

    def body(ug, ugp, ugn, uv, uvp, uvn, wg, wv, bg, bv, a_ref):
        r = pl.program_id(1)
        gate = _conv3(_conv_ext(ug, ugp, ugn, r, nr), wg[...], bg[...])[HALO:HALO + CONV_TR]
        val = _conv3(_conv_ext(uv, uvp, uvn, r, nr), wv[...], bv[...])[HALO:HALO + CONV_TR]
        a_ref[...] = (_silu(gate) * val).astype(BF16)

    same = lambda j: j
    w_spec = pl.BlockSpec((3, CONV_TC), lambda j, r: (0, j))
    b_spec = pl.BlockSpec((1, CONV_TC), lambda j, r: (0, j))
    return pl.pallas_call(
        body, name="ffn_mid_fwd", grid=(f // CONV_TC, nr),
        in_specs=_conv_specs(t, HALO, same) + _conv_specs(t, HALO, same) + [w_spec, w_spec, b_spec, b_spec],
        out_specs=pl.BlockSpec((CONV_TR, CONV_TC), lambda j, r: (r, j)),
        out_shape=jax.ShapeDtypeStruct((t, f), BF16),
        compiler_params=_params(("parallel", "parallel"), VMEM_LIMIT),
    )(u_g, u_g, u_g, u_v, u_v, u_v, cw_g, cw_v, cb_g, cb_v)


def _ffn_mid_bwd(u_g, u_v, cw_g, cw_v, cb_g, cb_v, da):
    t, f = u_g.shape
    nr = t // CONV_TR
    ext = CONV_TR + 2 * HALO

    def body(ug, ugp, ugn, uv, uvp, uvn, dam, dap, dan, wg, wv, bg, bv,
             dug_ref, duv_ref, dwg_ref, dwv_ref, dbg_ref, dbv_ref):
        r = pl.program_id(1)
        xg = _conv_ext(ug, ugp, ugn, r, nr)
        xv = _conv_ext(uv, uvp, uvn, r, nr)
        da_x = _conv_ext(dam, dap, dan, r, nr)
        gate = _conv3(xg, wg[...], bg[...])
        val = _conv3(xv, wv[...], bv[...])
        pos = r * CONV_TR - HALO + lax.broadcasted_iota(jnp.int32, (ext, CONV_TC), 0)
        inside = (pos >= 0) & (pos < t)
        sig = jax.nn.sigmoid(gate)
        silu = gate * sig
        d_val = jnp.where(inside, da_x * silu, 0.0)
        d_gate = jnp.where(inside, da_x * val * (sig + silu * (1.0 - sig)), 0.0)
        own = lax.broadcasted_iota(jnp.int32, (ext, CONV_TC), 0)
        own = (own >= HALO) & (own < HALO + CONV_TR)
        for x, d, w, du_ref, dw_ref, db_ref in ((xg, d_gate, wg, dug_ref, dwg_ref, dbg_ref),
                                                (xv, d_val, wv, duv_ref, dwv_ref, dbv_ref)):
            wt = w[...]
            du = pltpu.roll(d, ext - 1, 0) * wt[0:1] + d * wt[1:2] + pltpu.roll(d, 1, 0) * wt[2:3]
            du_ref[...] = du[HALO:HALO + CONV_TR].astype(BF16)
            d_own = jnp.where(own, d, 0.0)
            dw = jnp.concatenate([
                jnp.sum(pltpu.roll(x, 1, 0) * d_own, axis=0, keepdims=True),
                jnp.sum(x * d_own, axis=0, keepdims=True),
                jnp.sum(pltpu.roll(x, ext - 1, 0) * d_own, axis=0, keepdims=True)], axis=0)
            db = jnp.sum(d_own, axis=0, keepdims=True)

            @pl.when(r == 0)
            def _():
                dw_ref[...] = jnp.zeros_like(dw_ref)
                db_ref[...] = jnp.zeros_like(db_ref)

            dw_ref[...] += dw
            db_ref[...] += db

    same = lambda j: j
    w_spec = pl.BlockSpec((3, CONV_TC), lambda j, r: (0, j))
    b_spec = pl.BlockSpec((1, CONV_TC), lambda j, r: (0, j))
    tile = pl.BlockSpec((CONV_TR, CONV_TC), lambda j, r: (r, j))
    return pl.pallas_call(
        body, name="ffn_mid_bwd", grid=(f // CONV_TC, nr),
        in_specs=(_conv_specs(t, HALO, same) + _conv_specs(t, HALO, same) + _conv_specs(t, HALO16, same)
                  + [w_spec, w_spec, b_spec, b_spec]),
        out_specs=[tile, tile, w_spec, w_spec, b_spec, b_spec],
        out_shape=[jax.ShapeDtypeStruct((t, f), BF16), jax.ShapeDtypeStruct((t, f), BF16),
                   jax.ShapeDtypeStruct((3, f), F32), jax.ShapeDtypeStruct((3, f), F32),
                   jax.ShapeDtypeStruct((1, f), F32), jax.ShapeDtypeStruct((1, f), F32)],
        compiler_params=_params(("parallel", "arbitrary"), VMEM_LIMIT),
    )(u_g, u_g, u_g, u_v, u_v, u_v, da, da, da, cw_g, cw_v, cb_g, cb_v)


def _loss_head(y, target, *, tr=256):
    t, d = y.shape

    def body(y_ref, t_ref, loss_ref, dy_ref, dyb_ref):
        err = y_ref[...] - t_ref[...]
        dy = err * (1.0 / d)
        dy_ref[...] = dy
        dyb_ref[...] = dy.astype(BF16)
        part = 0.5 * jnp.sum(jnp.sum(err * err, axis=-1, keepdims=True) * (1.0 / d), axis=0, keepdims=True)

        @pl.when(pl.program_id(0) == 0)
        def _():
            loss_ref[...] = jnp.zeros_like(loss_ref)

        loss_ref[...] += jnp.broadcast_to(part, loss_ref.shape)

    row = pl.BlockSpec((tr, d), lambda i: (i, 0))
    return pl.pallas_call(
        body, name="loss_head", grid=(t // tr,),
        in_specs=[row, row],
        out_specs=[pl.BlockSpec((1, 128), lambda i: (0, 0)), row, row],
        out_shape=[jax.ShapeDtypeStruct((1, 128), F32), jax.ShapeDtypeStruct((t, d), F32),
                   jax.ShapeDtypeStruct((t, d), BF16)],
        compiler_params=_params(("arbitrary",), VMEM_LIMIT),
    )(y, target)


ANY = pl.BlockSpec(memory_space=pl.ANY)


def _position():
    return lax.axis_index("x"), lax.axis_index("y"), lax.axis_index("c")


def _other_chips(x, y):
    return [(1 - x, y), (x, 1 - y), (1 - x, 1 - y)]


def _handshake(peers):
    barrier = pltpu.get_barrier_semaphore()
    for peer in peers:
        pl.semaphore_signal(barrier, inc=1, device_id=peer, device_id_type=MESH)
    pl.semaphore_wait(barrier, len(peers))


def _exchange(body, operands, out_shapes, sems, *, name, collective_id):
    n_in, n_out = len(operands), len(out_shapes)

    def run(*refs):
        body(refs[:n_in], refs[n_in:n_in + n_out], *refs[n_in + n_out:])

    if collective_id is None:
        return pl.pallas_call(run, name=name, in_specs=[ANY] * n_in, out_specs=[ANY] * n_out,
                              out_shape=out_shapes, scratch_shapes=sems)(*operands)
    return pl.kernel(run, name=name, out_type=out_shapes,
                     mesh=plsc.ScalarSubcoreMesh(axis_name="sequencer", num_cores=1), scratch_types=sems,
                     compiler_params=pltpu.CompilerParams(collective_id=collective_id))(*operands)


def _all_gather(blocks, *, name, collective_id=None):
    na = len(blocks)

    def body(ins, outs, send_sems, recv_sems, local_sems):
        x, y, c = _position()
        sibling = (x, y, 1 - c)
        chips = _other_chips(x, y)
        if collective_id is not None:
            _handshake([sibling] + [(*chip, c) for chip in chips])

        def index(px, py, pc):
            return 4 * px + 2 * py + pc

        def copy(a, k, block, to, src=None):
            dst = outs[a].at[index(*block)]
            return pltpu.make_async_remote_copy(
                src_ref=dst if src is None else src, dst_ref=dst,
                send_sem=send_sems.at[a, k], recv_sem=recv_sems.at[a, k],
                device_id=to, device_id_type=MESH)

        pending = []
        for a in range(na):
            mine = pltpu.make_async_copy(ins[a], outs[a].at[index(x, y, c)], local_sems.at[a])
            mine.start()
            pending.append(mine)
        first = []
        for a in range(na):
            first.append(copy(a, 0, (x, y, c), sibling, src=ins[a]))
            first += [copy(a, 1 + j, (x, y, c), (*chip, c), src=ins[a]) for j, chip in enumerate(chips)]
        for cp in first:
            cp.start()
        passed = []
        for j, chip in enumerate(chips):
            for a in range(na):
                copy(a, 1 + j, (*chip, c), (x, y, c)).wait_recv()
                fwd = copy(a, 4 + j, (*chip, c), sibling)
                fwd.start()
                passed.append(fwd)
        for a in range(na):
            copy(a, 0, sibling, (x, y, c)).wait_recv()
            for j, chip in enumerate(chips):
                copy(a, 4 + j, (*chip, 1 - c), (x, y, c)).wait_recv()
        for cp in first + passed:
            cp.wait_send()
        for cp in pending:
            cp.wait()

    return _exchange(
        body, blocks, [jax.ShapeDtypeStruct((N_DEV,) + b.shape, b.dtype) for b in blocks],
        [pltpu.SemaphoreType.DMA((na, 7)), pltpu.SemaphoreType.DMA((na, 7)), pltpu.SemaphoreType.DMA((na,))],
        name=name, collective_id=collective_id)


def _grad_exchange(grads, parts, *, name, collective_id):
    ng, npart = len(grads), len(parts)

    def body(ins, outs, core_send, core_recv, chip_send, chip_recv, local_sems):
        x, y, c = _position()
        sibling = (x, y, 1 - c)
        chips = _other_chips(x, y)
        _handshake([sibling] + [(px, py, c) for px, py in chips])
        me = 2 * x + y
        copies = []
        for b in range(npart):
            src, dst = ins[ng + b], outs[ng + b]
            own = pltpu.make_async_copy(src.at[me], dst.at[me], local_sems.at[b])
            own.start()
            copies.append(own)
            for j, (px, py) in enumerate(chips):
                cp = pltpu.make_async_remote_copy(
                    src_ref=src.at[2 * px + py], dst_ref=dst.at[me],
                    send_sem=chip_send.at[b, j], recv_sem=chip_recv.at[b, j],
                    device_id=(px, py, c), device_id_type=MESH)
                cp.start()
                copies.append(cp)
        for a in range(ng):
            for k in range(N_CHIP):
                cp = pltpu.make_async_remote_copy(
                    src_ref=ins[a].at[k, 1 - c], dst_ref=outs[a].at[k],
                    send_sem=core_send.at[a, k], recv_sem=core_recv.at[a, k],
                    device_id=sibling, device_id_type=MESH)
                cp.start()
                copies.append(cp)
        for cp in copies:
            cp.wait()

    shapes = ([jax.ShapeDtypeStruct((N_CHIP,) + g.shape[2:], g.dtype) for g in grads]
              + [jax.ShapeDtypeStruct(p.shape, p.dtype) for p in parts])
    sems = [pltpu.SemaphoreType.DMA((max(ng, 1), N_CHIP)), pltpu.SemaphoreType.DMA((max(ng, 1), N_CHIP)),
            pltpu.SemaphoreType.DMA((max(npart, 1), 3)), pltpu.SemaphoreType.DMA((max(npart, 1), 3)),
            pltpu.SemaphoreType.DMA((max(npart, 1),))]
    out = _exchange(body, list(grads) + list(parts), shapes, sems, name=name, collective_id=collective_id)
    return out[:ng], out[ng:]


def _pair_sum(grad, theirs, core, *, tr, name):
    _, _, r, w = grad.shape
    assert r % tr == 0

    def body(core_ref, mine_ref, theirs_ref, out_ref):
        out_ref[...] = mine_ref[...] + theirs_ref[...]

    return pl.pallas_call(
        body, name=name,
        grid_spec=pltpu.PrefetchScalarGridSpec(
            num_scalar_prefetch=1, grid=(N_CHIP, r // tr),
            in_specs=[pl.BlockSpec((None, None, tr, w), lambda k, i, core_ref: (k, core_ref[0], i, 0)),
                      pl.BlockSpec((None, tr, w), lambda k, i, core_ref: (k, i, 0))],
            out_specs=pl.BlockSpec((None, tr, w), lambda k, i, core_ref: (k, i, 0))),
        out_shape=jax.ShapeDtypeStruct((N_CHIP, r, w), F32),
        compiler_params=_params(("parallel", "parallel"), VMEM_LIMIT),
    )(core, grad, theirs)


class _ReduceScatter:
    def __init__(self, core):
        self.core = core
        self.pending = None
        self.results = {}
        self.launches = 0

    def push(self, tag, grads, rows, then):
        pair, prev_tag = [], None
        if self.pending is not None:
            prev_tag, prev, theirs, prev_rows = self.pending
            pair = [_pair_sum(g, s, self.core, tr=r, name=f"pair_sum_{prev_tag}_{i}")
                    for i, (g, s, r) in enumerate(zip(prev, theirs, prev_rows))]
        grads, pair, then = lax.optimization_barrier((list(grads), pair, then))
        grads = [g.reshape((N_CHIP, 2) + g.shape[1:]) for g in grads]
        self.launches += 1
        theirs, parts = _grad_exchange(grads, pair, name=f"grad_exchange_{self.launches}",
                                       collective_id=1 + self.launches)
        if prev_tag is not None:
            self.results[prev_tag] = parts
        self.pending = (tag, grads, theirs, rows) if tag is not None else None
        return then

    def result(self, tag):
        return self.results[tag]


def _adamw(parts, w, m, v, *, tr, name):
    n, r, cols = parts.shape
    assert r % tr == 0 and w.shape == (r, cols)
    c1 = 1.0 - ADAM_B1 ** ADAM_STEP
    c2 = 1.0 - ADAM_B2 ** ADAM_STEP

    def body(p_ref, w_ref, m_ref, v_ref, g_ref, d_ref, nm_ref, nv_ref):
        g = p_ref[0]
        for k in range(1, n):
            g = g + p_ref[k]
        new_m = ADAM_B1 * m_ref[...] + (1.0 - ADAM_B1) * g
        new_v = ADAM_B2 * v_ref[...] + (1.0 - ADAM_B2) * (g * g)
        m_hat = new_m / c1
        v_hat = new_v / c2
        g_ref[...] = g
        d_ref[...] = -ADAM_LR * (m_hat / (jnp.sqrt(v_hat) + ADAM_EPS) + ADAM_WD * w_ref[...])
        nm_ref[...] = new_m
        nv_ref[...] = new_v

    tile = pl.BlockSpec((tr, cols), lambda i: (i, 0))
    shape = jax.ShapeDtypeStruct((r, cols), F32)
    return pl.pallas_call(
        body, name=name, grid=(r // tr,),
        in_specs=[pl.BlockSpec((n, tr, cols), lambda i: (0, i, 0)), tile, tile, tile],
        out_specs=[tile] * 4, out_shape=[shape] * 4,
        compiler_params=_params(("parallel",), VMEM_LIMIT),
    )(parts, w, m, v)


def _rope_tables(t):
    half = HEAD_DIM // 2
    inv = 1.0 / (ROPE_THETA ** (jnp.arange(half, dtype=F32) / half))
    ang = jnp.arange(t, dtype=jnp.int32).astype(F32)[:, None] * inv[None, :]
    cos, sin = jnp.cos(ang), jnp.sin(ang)
    return jnp.concatenate([cos, cos], axis=1), jnp.concatenate([-sin, sin], axis=1)


def _to_kernel_columns(w_full):
    order = sorted(SEGMENTS.values())
    main = jnp.concatenate([w_full[:, src:src + width] for _, src, width in order], axis=1)
    lr = jnp.pad(w_full[:, IN_MAIN:IN_TOTAL], ((0, 0), (0, LR_PAD - (IN_TOTAL - IN_MAIN))))
    return main, lr


def _from_kernel_columns(main, lr):
    by_src = sorted(SEGMENTS.values(), key=lambda s: s[1])
    return jnp.concatenate([main[:, dst:dst + width] for dst, _, width in by_src]
                           + [lr[:, :IN_TOTAL - IN_MAIN]], axis=1)


def _pack(pieces):
    flat = []
    for p in pieces:
        p = p.reshape(-1)
        flat.append(jnp.pad(p, (0, (-p.shape[0]) % 128)))
    return jnp.concatenate(flat).reshape(-1, 128)


def _unpack(packed, shapes):
    flat = packed.reshape(-1)
    out, off = [], 0
    for s in shapes:
        size = 1
        for dim in s:
            size *= dim
        out.append(flat[off:off + size].reshape(s))
        off += size + (-size) % 128
    return out


def _local_step(xs, target, norm1_g, w_main, w_lr, gq, gk, attn_sink, w2, ba2, gla_norm_g, w_out_full, norm2_g,
                w_ug, w_uv, cw_g, cw_v, cb_g, cb_v, w_down_full, rs=None):
    t = xs.shape[0]
    tm = min(1024, t)
    cos, sin_signed = _rope_tables(t)
    sink = attn_sink.reshape(ATTN_HEADS)

    h1 = _rmsnorm_fwd(xs, norm1_g, name="norm1_fwd")
    proj = _matmul(h1, w_main, "nn", tm=tm, tn=512, tk=D_MODEL, name="proj_main")
    proj_lr = _matmul(h1, w_lr, "nn", tm=tm, tn=LR_PAD, tk=D_MODEL, name="proj_lr")
    qa, ka, va = _attn_prep_fwd(proj, cos, sin_signed, gq, gk)
    o_attn = _attn_fwd(qa, ka, va, sink)
    g_dec = _gla_prep_fwd(proj_lr, w2, ba2)
    o_f, o_b, s_f, s_b = _gla_fwd(proj, g_dec)
    o_gla = _gla_out_fwd(o_f, o_b, proj, gla_norm_g)
    mix = jnp.concatenate([o_attn, o_gla], axis=1)
    x1 = _matmul(mix, w_out_full, "nn", tm=tm, tn=512, tk=D_MODEL, res=xs, name="out_proj")
    h2 = _rmsnorm_fwd(x1, norm2_g, name="norm2_fwd")
    u_g = _matmul(h2, w_ug, "nn", tm=tm, tn=512, tk=D_MODEL, name="up_gate")
    u_v = _matmul(h2, w_uv, "nn", tm=tm, tn=512, tk=D_MODEL, name="up_value")
    act = _ffn_mid_fwd(u_g, u_v, cw_g, cw_v, cb_g, cb_v)
    y = _matmul(act, w_down_full, "nn", tm=tm, tn=512, tk=D_FF // 2, res=x1, name="down_proj")
    loss_part, dy, dy_b = _loss_head(y, target)

    per_up = 2 * D_FF // N_DEV
    d_act = _matmul(dy_b, w_down_full, "nt", tm=tm, tn=512, tk=D_MODEL, out_dtype=BF16, name="d_act")
    dw_down = _matmul(act, dy_b, "tn", tm=512, tn=1024, tk=t, name="dw_down")
    if rs is not None:
        d_act = rs.push("w_down", [dw_down.reshape(N_DEV, D_FF // N_DEV, D_MODEL)], [64], d_act)
    du_g, du_v, dcw_g, dcw_v, dcb_g, dcb_v = _ffn_mid_bwd(u_g, u_v, cw_g, cw_v, cb_g, cb_v, d_act)
    dw_ug = _matmul(h2, du_g, "tn", tm=1024, tn=512, tk=t, name="dw_up_gate")
    dw_uv = _matmul(h2, du_v, "tn", tm=1024, tn=512, tk=t, name="dw_up_value")
    if rs is not None:
        dw_up = jnp.concatenate([jnp.transpose(dw_ug.reshape(D_MODEL, N_CHIP, per_up), (1, 0, 2)),
                                 jnp.transpose(dw_uv.reshape(D_MODEL, N_CHIP, per_up), (1, 0, 2))], axis=0)
        du_g, du_v = rs.push("w_up", [dw_up], [256], (du_g, du_v))
    dh2 = _matmul(du_g, w_ug, "nt", tm=tm, tn=512, tk=D_FF // 2, name="dh2_gate")
    dh2 = _matmul(du_v, w_uv, "nt", tm=tm, tn=512, tk=D_FF // 2, res=dh2, name="dh2_value")
    dx1, dx1_b, d_norm2 = _rmsnorm_bwd(x1, norm2_g, dh2, dy, name="norm2_bwd")
    dmix = _matmul(dx1_b, w_out_full, "nt", tm=tm, tn=512, tk=D_MODEL, name="d_mix")
    dw_out = _matmul(mix, dx1_b, "tn", tm=1024, tn=512, tk=t, name="dw_out")
    if rs is not None:
        dmix = rs.push("w_out", [dw_out.reshape(N_DEV, D_MODEL // N_DEV, D_MODEL)], [256], dmix)
    do_gla, d_gate, d_gla_norm = _gla_out_bwd(o_f, o_b, proj, gla_norm_g, dmix)
    (dq_f, dk_f, dv_f, dg_f, dq_b, dk_b, dv_b, dg_b) = _gla_bwd(proj, g_dec, s_f, s_b, do_gla)
    d_lr, d_w2, d_ba2 = _gla_prep_bwd(proj_lr, w2, ba2, dg_f, dg_b)
    dqa, dk_lo, dk_mid, dk_hi, dv_lo, dv_mid, dv_hi, d_sink8 = _attn_bwd(qa, ka, va, sink, dmix)
    d_qa, d_ka, d_va, d_qn, d_kn = _attn_prep_bwd(proj, cos, sin_signed, gq, gk, dqa,
                                                  (dk_lo, dk_mid, dk_hi), (dv_lo, dv_mid, dv_hi))
    d_seg = {"qa": d_qa, "gate": d_gate, "vg": (dv_f + dv_b).astype(BF16), "qg": (dq_f + dq_b).astype(BF16),
             "kg": (dk_f + dk_b).astype(BF16), "ka": d_ka, "va": d_va}
    d_proj = jnp.concatenate([d_seg[k] for k in sorted(SEGMENTS, key=lambda k: SEGMENTS[k][0])], axis=1)
    dw_main = _matmul(h1, d_proj, "tn", tm=1024, tn=512, tk=t, name="dw_in_main")
    dw_lr = _matmul(h1, d_lr, "tn", tm=1024, tn=LR_PAD, tk=t, name="dw_in_lr")
    if rs is not None:
        per_in, per_wa = IN_TOTAL // N_DEV, GLA_KEY_WIDTH // N_DEV
        dw_in = _from_kernel_columns(dw_main, dw_lr)
        dconv_w = jnp.concatenate([dcw_g, dcw_v], axis=1)
        d_wa2_f = d_w2[:GLA_RANK, :GLA_KEY_WIDTH]
        d_wa2_b = d_w2[GLA_RANK:2 * GLA_RANK, GLA_KEY_WIDTH:]
        small_grad = jnp.stack([
            _pack([dconv_w[:, d * per_up:(d + 1) * per_up], d_wa2_f[:, d * per_wa:(d + 1) * per_wa],
                   d_wa2_b[:, d * per_wa:(d + 1) * per_wa]]) for d in range(N_DEV)])
        d_proj, d_lr = rs.push(
            "w_in", [jnp.transpose(dw_in.reshape(D_MODEL, N_DEV, per_in), (1, 0, 2)), small_grad],
            [256, small_grad.shape[1]], (d_proj, d_lr))
    dh1 = _matmul(d_lr, w_lr, "nt", tm=tm, tn=512, tk=LR_PAD, name="dh1_lr")
    dh1 = _matmul(d_proj, w_main, "nt", tm=tm, tn=512, tk=IN_MAIN // 2, res=dh1, name="dh1_main")
    grad_x, _, d_norm1 = _rmsnorm_bwd(xs, norm1_g, dh1, dx1, name="norm1_bwd")
    if rs is not None:
        grad_x = rs.push(None, [], [], grad_x)
    return (loss_part, grad_x, dw_main, dw_lr, dw_out, dw_ug, dw_uv, dw_down, dcw_g, dcw_v, dcb_g, dcb_v,
            d_w2, d_ba2, d_norm1, d_norm2, d_qn, d_kn, d_sink8, d_gla_norm)


def kernel(x, norm1_g, w_in, attn_q_norm_g, attn_k_norm_g, attn_sink, gla_wa2_fwd, gla_ba_fwd, gla_wa2_bwd, gla_ba_bwd, gla_out_norm_g, w_out, norm2_g, w_up, conv_w, conv_b, w_down, loss_target, m_norm1_g, m_w_in, m_attn_q_norm_g, m_attn_k_norm_g, m_attn_sink, m_gla_wa2_fwd, m_gla_ba_fwd, m_gla_wa2_bwd, m_gla_ba_bwd, m_gla_out_norm_g, m_w_out, m_norm2_g, m_w_up, m_conv_w, m_conv_b, m_w_down, v_norm1_g, v_w_in, v_attn_q_norm_g, v_attn_k_norm_g, v_attn_sink, v_gla_wa2_fwd, v_gla_ba_fwd, v_gla_wa2_bwd, v_gla_ba_bwd, v_gla_out_norm_g, v_w_out, v_norm2_g, v_w_up, v_conv_w, v_conv_b, v_w_down):
    t = x.shape[1]
    xs = x.reshape(t, D_MODEL)
    target = loss_target.reshape(t, D_MODEL)
    core = lax.axis_index("c").astype(jnp.int32).reshape(1)

    sharded_small = [conv_w[0], gla_wa2_fwd[0], gla_wa2_bwd[0]]
    small_shapes = [s.shape for s in sharded_small]
    g_in, g_small = _all_gather([w_in[0].astype(BF16), _pack(sharded_small)], name="gather_w_in")
    g_in, later = lax.optimization_barrier(
        (g_in, [w_out[0].astype(BF16), w_up[0].astype(BF16), w_down[0].astype(BF16)]))
    g_out, g_up, g_down = _all_gather(later, name="gather_later_weights", collective_id=1)
    w_in_full = jnp.transpose(g_in, (1, 0, 2)).reshape(D_MODEL, IN_TOTAL)
    w_main, w_lr = _to_kernel_columns(w_in_full)
    w_out_full = g_out.reshape(D_MODEL, D_MODEL)
    w_ug = jnp.transpose(g_up[:N_CHIP], (1, 0, 2)).reshape(D_MODEL, D_FF)
    w_uv = jnp.transpose(g_up[N_CHIP:], (1, 0, 2)).reshape(D_MODEL, D_FF)
    w_down_full = g_down.reshape(D_FF, D_MODEL)
    small_full = [_unpack(g_small[d], small_shapes) for d in range(N_DEV)]
    conv_w_full = jnp.concatenate([s[0] for s in small_full], axis=1)
    wa2_f = jnp.concatenate([s[1] for s in small_full], axis=1)
    wa2_b = jnp.concatenate([s[2] for s in small_full], axis=1)
    cw_g, cw_v = conv_w_full[:, :D_FF], conv_w_full[:, D_FF:]
    cb_g, cb_v = conv_b[:, :D_FF], conv_b[:, D_FF:]
    w2 = jnp.zeros((LR_PAD, 2 * GLA_KEY_WIDTH), F32)
    w2 = w2.at[:GLA_RANK, :GLA_KEY_WIDTH].set(wa2_f).at[GLA_RANK:2 * GLA_RANK, GLA_KEY_WIDTH:].set(wa2_b)
    ba2 = jnp.concatenate([gla_ba_fwd, gla_ba_bwd], axis=1)
    rs = _ReduceScatter(core)
    (loss_part, grad_x, _, _, _, _, _, _, _, _, dcb_g, dcb_v, _, d_ba2,
     d_norm1, d_norm2, d_qn, d_kn, d_sink8, d_gla_norm) = _local_step(
        xs, target, norm1_g, w_main, w_lr, attn_q_norm_g, attn_k_norm_g, attn_sink, w2, ba2, gla_out_norm_g,
        w_out_full, norm2_g, w_ug, w_uv, cw_g, cw_v, cb_g, cb_v, w_down_full, rs=rs)
    loss = lax.psum(loss_part[0, 0], ("x", "y", "c"))

    (part_down,), (part_up,), (part_out,) = rs.result("w_down"), rs.result("w_up"), rs.result("w_out")
    part_in, part_small = rs.result("w_in")
    m_small = _pack([m_conv_w[0], m_gla_wa2_fwd[0], m_gla_wa2_bwd[0]])
    v_small = _pack([v_conv_w[0], v_gla_wa2_fwd[0], v_gla_wa2_bwd[0]])
    upd_in = _adamw(part_in, w_in[0], m_w_in[0], v_w_in[0], tr=256, name="adamw_w_in")
    upd_out = _adamw(part_out, w_out[0], m_w_out[0], v_w_out[0], tr=256, name="adamw_w_out")
    upd_up = _adamw(part_up, w_up[0], m_w_up[0], v_w_up[0], tr=256, name="adamw_w_up")
    upd_down = _adamw(part_down, w_down[0], m_w_down[0], v_w_down[0], tr=64, name="adamw_w_down")
    upd_small = _adamw(part_small, _pack(sharded_small), m_small, v_small, tr=part_small.shape[1], name="adamw_small")
    upd_small = [_unpack(u, small_shapes) for u in upd_small]

    rep_names = ["norm1_g", "attn_q_norm_g", "attn_k_norm_g", "attn_sink", "gla_ba_fwd", "gla_ba_bwd",
                 "gla_out_norm_g", "norm2_g", "conv_b"]
    rep_w = [norm1_g, attn_q_norm_g, attn_k_norm_g, attn_sink, gla_ba_fwd, gla_ba_bwd, gla_out_norm_g, norm2_g, conv_b]
    rep_m = [m_norm1_g, m_attn_q_norm_g, m_attn_k_norm_g, m_attn_sink, m_gla_ba_fwd, m_gla_ba_bwd,
             m_gla_out_norm_g, m_norm2_g, m_conv_b]
    rep_v = [v_norm1_g, v_attn_q_norm_g, v_attn_k_norm_g, v_attn_sink, v_gla_ba_fwd, v_gla_ba_bwd,
             v_gla_out_norm_g, v_norm2_g, v_conv_b]
    d_sink = d_sink8[:, :GQA_GROUP, 0].reshape(1, ATTN_HEADS)
    rep_g = [d_norm1, d_qn, d_kn, d_sink, d_ba2[:, :GLA_KEY_WIDTH], d_ba2[:, GLA_KEY_WIDTH:], d_gla_norm, d_norm2,
             jnp.concatenate([dcb_g, dcb_v], axis=1)]
    rep_shapes = [w.shape for w in rep_w]
    (rep_terms,) = _all_gather([_pack(rep_g)], name="gather_small_grads")
    upd_rep = _adamw(rep_terms, _pack(rep_w), _pack(rep_m), _pack(rep_v), tr=rep_terms.shape[1], name="adamw_replicated")
    upd_rep = [_unpack(u, rep_shapes) for u in upd_rep]

    order = ["norm1_g", "w_in", "attn_q_norm_g", "attn_k_norm_g", "attn_sink", "gla_wa2_fwd", "gla_ba_fwd",
             "gla_wa2_bwd", "gla_ba_bwd", "gla_out_norm_g", "w_out", "norm2_g", "w_up", "conv_w", "conv_b", "w_down"]
    outs = [loss, grad_x.reshape(1, t, D_MODEL)]
    for kind in range(4):
        by_name = {n: upd_rep[kind][i] for i, n in enumerate(rep_names)}
        by_name["w_in"] = upd_in[kind][None]
        by_name["w_out"] = upd_out[kind][None]
        by_name["w_up"] = upd_up[kind][None]
        by_name["w_down"] = upd_down[kind][None]
        by_name["conv_w"] = upd_small[kind][0][None]
        by_name["gla_wa2_fwd"] = upd_small[kind][1][None]
        by_name["gla_wa2_bwd"] = upd_small[kind][2][None]
        outs += [by_name[n] for n in order]
    return tuple(outs)
```

```python
import functools

import jax
import jax.numpy as jnp
from jax import lax
from jax.experimental import pallas as pl
from jax.experimental.pallas import tpu as pltpu
from jax.experimental.pallas import tpu_sc as plsc

F32 = jnp.float32
BF16 = jnp.bfloat16

D_MODEL = 2048
HEAD_DIM = 128
ATTN_WIDTH = 1024
ATTN_HEADS = 8
KV_HEADS = 2
GQA_GROUP = 4
KV_WIDTH = KV_HEADS * HEAD_DIM
ATTN_BLOCK = 128
WINDOW = 128
ROPE_THETA = 10000.0
GLA_HEADS = 4
GLA_DK = 128
GLA_DV = 256
GLA_KEY_WIDTH = 512
GLA_WIDTH = 1024
GLA_RANK = 16
GLA_GATE_NORMALIZER = 16.0
GLA_CHUNK = 64
D_FF = 5632
NORM_EPS = 1e-6
IN_TOTAL = 4640
IN_MAIN = 4608
LR_PAD = 128
N_DEV = 8
N_CHIP = 4

ADAM_LR = 0.001
ADAM_B1 = 0.9
ADAM_B2 = 0.999
ADAM_EPS = 1e-08
ADAM_WD = 0.01
ADAM_STEP = 10

SEGMENTS = {
    "qa": (0, 0, 1024),
    "gate": (1024, 3584, 1024),
    "vg": (2048, 2560, 1024),
    "qg": (3072, 1536, 512),
    "kg": (3584, 2048, 512),
    "ka": (4096, 1024, 256),
    "va": (4352, 1280, 256),
}

VMEM_LIMIT = 56 * 1024 * 1024
MESH = pl.DeviceIdType.MESH


def _params(semantics=None, vmem=None):
    return pltpu.CompilerParams(dimension_semantics=semantics, vmem_limit_bytes=vmem)


_DIMS = {
    "nn": (((1,), (0,)), ((), ())),
    "nt": (((1,), (1,)), ((), ())),
    "tn": (((0,), (0,)), ((), ())),
}


def _mxu(a, b, mode):
    return lax.dot_general(a.astype(BF16), b.astype(BF16), _DIMS[mode], preferred_element_type=F32)


@functools.partial(jax.custom_vjp, nondiff_argnums=(2,))
def bdot(a, b, mode):
    return _mxu(a, b, mode)


def _bdot_fwd(a, b, mode):
    return _mxu(a, b, mode), (a, b)


def _bdot_bwd(mode, res, g):
    a, b = res
    if mode == "nn":
        return _mxu(g, b, "nt"), _mxu(a, g, "tn")
    if mode == "nt":
        return _mxu(g, b, "nn"), _mxu(g, a, "tn")
    return _mxu(b, g, "nt"), _mxu(a, g, "nn")


bdot.defvjp(_bdot_fwd, _bdot_bwd)


def _rms(x, g):
    return x * lax.rsqrt(jnp.mean(x * x, axis=-1, keepdims=True) + NORM_EPS) * g


def _rope(x, cos, sin_signed):
    return x * cos + pltpu.roll(x, HEAD_DIM // 2, 1) * sin_signed


def _rope_transposed(d, cos, sin_signed):
    return d * cos + pltpu.roll(d * sin_signed, HEAD_DIM // 2, 1)


def _silu(x):
    return x * jax.nn.sigmoid(x)


def _log_sigmoid(z):
    return -(jnp.maximum(-z, 0.0) + jnp.log(1.0 + jnp.exp(-jnp.abs(z))))


def _matmul_call(args, in_specs, o_spec, out_shape, grid, mode, nk, acc_shape, *, name, has_res=False,
                 prefetch=None):
    dims = _DIMS[mode]
    out_dtype = out_shape.dtype
    n_pre = 0 if prefetch is None else 1

    def body(*refs):
        refs = refs[n_pre:]
        if has_res:
            a_ref, b_ref, r_ref, o_ref = refs[:4]
            rest = refs[4:]
        else:
            a_ref, b_ref, o_ref = refs[:3]
            r_ref = None
            rest = refs[3:]
        part = lax.dot_general(a_ref[...], b_ref[...], dims, preferred_element_type=F32)

        def finish(acc):
            if r_ref is not None:
                acc = acc + r_ref[...]
            o_ref[...] = acc.astype(out_dtype)

        if nk == 1:
            finish(part)
        else:
            acc_ref = rest[0]
            kk = pl.program_id(2)

            @pl.when(kk == 0)
            def _():
                acc_ref[...] = part

            @pl.when(kk > 0)
            def _():
                acc_ref[...] += part

            @pl.when(kk == nk - 1)
            def _():
                finish(acc_ref[...])

    scratch = [pltpu.VMEM(acc_shape, F32)] if nk > 1 else []
    params = _params(("parallel", "parallel", "arbitrary"), VMEM_LIMIT)
    if prefetch is None:
        return pl.pallas_call(body, name=name, grid=grid, in_specs=in_specs, out_specs=o_spec, out_shape=out_shape,
                              scratch_shapes=scratch, compiler_params=params)(*args)
    return pl.pallas_call(
        body, name=name,
        grid_spec=pltpu.PrefetchScalarGridSpec(num_scalar_prefetch=1, grid=grid, in_specs=in_specs,
                                               out_specs=o_spec, scratch_shapes=scratch),
        out_shape=out_shape, compiler_params=params)(prefetch, *args)


def _matmul(a, b, mode, *, tm, tn, tk, out_dtype=F32, res=None, name):
    if mode == "nn":
        (m, k), (k2, n) = a.shape, b.shape
    elif mode == "nt":
        (m, k), (n, k2) = a.shape, b.shape
    else:
        (k, m), (k2, n) = a.shape, b.shape
    assert k == k2 and m % tm == 0 and n % tn == 0 and k % tk == 0, (name, a.shape, b.shape, tm, tn, tk)
    if mode == "tn":
        a_spec = pl.BlockSpec((tk, tm), lambda i, j, kk: (kk, i))
    else:
        a_spec = pl.BlockSpec((tm, tk), lambda i, j, kk: (i, kk))
    if mode == "nt":
        b_spec = pl.BlockSpec((tn, tk), lambda i, j, kk: (j, kk))
    else:
        b_spec = pl.BlockSpec((tk, tn), lambda i, j, kk: (kk, j))
    o_spec = pl.BlockSpec((tm, tn), lambda i, j, kk: (i, j))
    in_specs, args = [a_spec, b_spec], [a, b]
    if res is not None:
        in_specs.append(o_spec)
        args.append(res)
    return _matmul_call(args, in_specs, o_spec, jax.ShapeDtypeStruct((m, n), out_dtype),
                        (m // tm, n // tn, k // tk), mode, k // tk, (tm, tn), name=name, has_res=res is not None)


UP_BLOCK = 2 * D_FF // N_DEV


def _up_proj(h2, w_up8, *, tm):
    t = h2.shape[0]
    return _matmul_call(
        [h2, w_up8],
        [pl.BlockSpec((tm, D_MODEL), lambda i, j, kk: (i, 0)),
         pl.BlockSpec((None, D_MODEL, UP_BLOCK), lambda i, j, kk: (j, 0, 0))],
        pl.BlockSpec((None, tm, UP_BLOCK), lambda i, j, kk: (j // N_CHIP, i, j % N_CHIP)),
        jax.ShapeDtypeStruct((2, t, D_FF), F32), (t // tm, N_DEV, 1), "nn", 1, None, name="up_proj")


def _up_proj_dx(du, w_up8, *, tm, tn):
    t = du.shape[1]
    return _matmul_call(
        [du, w_up8],
        [pl.BlockSpec((None, tm, UP_BLOCK), lambda i, j, kk: (kk // N_CHIP, i, kk % N_CHIP)),
         pl.BlockSpec((None, tn, UP_BLOCK), lambda i, j, kk: (kk, j, 0))],
        pl.BlockSpec((tm, tn), lambda i, j, kk: (i, j)),
        jax.ShapeDtypeStruct((t, D_MODEL), F32), (t // tm, D_MODEL // tn, N_DEV), "nt", N_DEV, (tm, tn),
        name="up_proj_dx")


def _up_proj_dw(h2, du, *, tm, tk):
    t = h2.shape[0]
    return _matmul_call(
        [h2, du],
        [pl.BlockSpec((tk, tm), lambda i, j, kk: (kk, i)),
         pl.BlockSpec((None, tk, UP_BLOCK), lambda i, j, kk: (j // N_CHIP, kk, j % N_CHIP))],
        pl.BlockSpec((None, tm, UP_BLOCK), lambda i, j, kk: (j, i, 0)),
        jax.ShapeDtypeStruct((N_DEV, D_MODEL, UP_BLOCK), F32), (D_MODEL // tm, N_DEV, t // tk), "tn", t // tk,
        (tm, UP_BLOCK), name="up_proj_dw")


IN_TILE = 256


def _in_proj_dw(d_proj, h1, *, tn):
    t = h1.shape[0]
    table = [0] * (IN_MAIN // IN_TILE)
    for dst, src, width in SEGMENTS.values():
        for k in range(width // IN_TILE):
            table[dst // IN_TILE + k] = src // IN_TILE + k
    return _matmul_call(
        [d_proj, h1],
        [pl.BlockSpec((t, IN_TILE), lambda j, i, kk, tab: (0, i)),
         pl.BlockSpec((t, tn), lambda j, i, kk, tab: (0, j))],
        pl.BlockSpec((IN_TILE, tn), lambda j, i, kk, tab: (tab[i], j)),
        jax.ShapeDtypeStruct((IN_TOTAL, D_MODEL), F32), (D_MODEL // tn, IN_MAIN // IN_TILE, 1), "tn", 1, None,
        name="in_proj_dw", prefetch=jnp.asarray(table, jnp.int32))


def _in_proj_dw_lr(dw_t, d_lr, h1):
    t = h1.shape[0]
    n_lr = IN_TOTAL - IN_MAIN
    tn = 512

    def body(dw_ref, dlr_ref, h1_ref, out_ref):
        full = lax.dot_general(dlr_ref[...], h1_ref[...], _DIMS["tn"], preferred_element_type=F32)
        out_ref[...] = full[:n_lr]

    return pl.pallas_call(
        body, name="in_proj_dw_lr", grid=(D_MODEL // tn,),
        in_specs=[pl.BlockSpec(memory_space=pl.ANY),
                  pl.BlockSpec((t, LR_PAD), lambda j: (0, 0)),
                  pl.BlockSpec((t, tn), lambda j: (0, j))],
        out_specs=pl.BlockSpec((n_lr, tn), lambda j: (IN_MAIN // n_lr, j)),
        out_shape=jax.ShapeDtypeStruct(dw_t.shape, F32),
        input_output_aliases={0: 0},
        compiler_params=_params(("parallel",), VMEM_LIMIT),
    )(dw_t, d_lr, h1)


def _rmsnorm_fwd(x, g, *, name, tr=512):
    t, d = x.shape

    def body(x_ref, g_ref, h_ref):
        h_ref[...] = _rms(x_ref[...], g_ref[...]).astype(BF16)

    return pl.pallas_call(
        body, name=name, grid=(t // tr,),
        in_specs=[pl.BlockSpec((tr, d), lambda i: (i, 0)), pl.BlockSpec((1, d), lambda i: (0, 0))],
        out_specs=pl.BlockSpec((tr, d), lambda i: (i, 0)),
        out_shape=jax.ShapeDtypeStruct((t, d), BF16),
        compiler_params=_params(("parallel",), VMEM_LIMIT),
    )(x, g)


def _rmsnorm_bwd(x, g, dh, dres, *, name, tr=256):
    t, d = x.shape

    def body(x_ref, g_ref, dh_ref, dres_ref, dx_ref, dxb_ref, dg_ref):
        _, vjp = jax.vjp(_rms, x_ref[...], g_ref[...])
        dx, dg = vjp(dh_ref[...])
        dx = dx + dres_ref[...]
        dx_ref[...] = dx
        dxb_ref[...] = dx.astype(BF16)

        @pl.when(pl.program_id(0) == 0)
        def _():
            dg_ref[...] = jnp.zeros_like(dg_ref)

        dg_ref[...] += dg

    row = pl.BlockSpec((tr, d), lambda i: (i, 0))
    vec = pl.BlockSpec((1, d), lambda i: (0, 0))
    return pl.pallas_call(
        body, name=name, grid=(t // tr,),
        in_specs=[row, vec, row, row],
        out_specs=[row, row, vec],
        out_shape=[jax.ShapeDtypeStruct((t, d), F32), jax.ShapeDtypeStruct((t, d), BF16),
                   jax.ShapeDtypeStruct((1, d), F32)],
        compiler_params=_params(("arbitrary",), VMEM_LIMIT),
    )(x, g, dh, dres)


def _seg_block(name, width):
    off = SEGMENTS[name][0]
    assert off % width == 0
    return off // width


def _attn_prep_fwd(proj, cos, sin_signed, gq, gk, *, tr=256):
    t = proj.shape[0]

    def body(q_ref, k_ref, v_ref, cos_ref, sin_ref, gq_ref, gk_ref, qo_ref, ko_ref, vo_ref):
        cos_t, sin_t = cos_ref[...], sin_ref[...]
        for h in range(ATTN_HEADS):
            cols = slice(h * HEAD_DIM, (h + 1) * HEAD_DIM)
            qo_ref[:, cols] = _rope(_rms(q_ref[:, cols], gq_ref[...]), cos_t, sin_t).astype(BF16)
        for h in range(KV_HEADS):
            cols = slice(h * HEAD_DIM, (h + 1) * HEAD_DIM)
            ko_ref[:, cols] = _rope(_rms(k_ref[:, cols], gk_ref[...]), cos_t, sin_t).astype(BF16)
        vo_ref[...] = v_ref[...].astype(BF16)

    qb, kb, vb = _seg_block("qa", ATTN_WIDTH), _seg_block("ka", KV_WIDTH), _seg_block("va", KV_WIDTH)
    tab = pl.BlockSpec((tr, HEAD_DIM), lambda i: (i, 0))
    vec = pl.BlockSpec((1, HEAD_DIM), lambda i: (0, 0))
    return pl.pallas_call(
        body, name="attn_prep_fwd", grid=(t // tr,),
        in_specs=[pl.BlockSpec((tr, ATTN_WIDTH), lambda i: (i, qb)),
                  pl.BlockSpec((tr, KV_WIDTH), lambda i: (i, kb)),
                  pl.BlockSpec((tr, KV_WIDTH), lambda i: (i, vb)),
                  tab, tab, vec, vec],
        out_specs=[pl.BlockSpec((tr, ATTN_WIDTH), lambda i: (i, 0)),
                   pl.BlockSpec((tr, KV_WIDTH), lambda i: (i, 0)),
                   pl.BlockSpec((tr, KV_WIDTH), lambda i: (i, 0))],
        out_shape=[jax.ShapeDtypeStruct((t, ATTN_WIDTH), BF16),
                   jax.ShapeDtypeStruct((t, KV_WIDTH), BF16),
                   jax.ShapeDtypeStruct((t, KV_WIDTH), BF16)],
        compiler_params=_params(("parallel",), VMEM_LIMIT),
    )(proj, proj, proj, cos, sin_signed, gq, gk)


def _attn_head(q, kcat, vcat, sink_col, valid):
    s = bdot(q, kcat, "nt") * (HEAD_DIM ** -0.5)
    s = jnp.where(valid, s, -jnp.inf)
    m = lax.stop_gradient(jnp.maximum(jnp.max(s, axis=-1, keepdims=True), sink_col))
    p = jnp.exp(s - m)
    p = p / (jnp.sum(p, axis=-1, keepdims=True) + jnp.exp(sink_col - m))
    return bdot(p, vcat, "nn")


def _attn_valid(n, t):
    qi = lax.broadcasted_iota(jnp.int32, (ATTN_BLOCK, 3 * ATTN_BLOCK), 0)
    sj = lax.broadcasted_iota(jnp.int32, (ATTN_BLOCK, 3 * ATTN_BLOCK), 1)
    kpos = n * ATTN_BLOCK - ATTN_BLOCK + sj
    return (jnp.abs(sj - ATTN_BLOCK - qi) <= WINDOW) & (kpos >= 0) & (kpos < t)


def _attn_specs(nb):
    q_spec = pl.BlockSpec((ATTN_BLOCK, GQA_GROUP * HEAD_DIM), lambda h, n: (n, h))
    kv_specs = [
        pl.BlockSpec((ATTN_BLOCK, HEAD_DIM), lambda h, n: (jnp.maximum(n - 1, 0), h)),
        pl.BlockSpec((ATTN_BLOCK, HEAD_DIM), lambda h, n: (n, h)),
        pl.BlockSpec((ATTN_BLOCK, HEAD_DIM), lambda h, n: (jnp.minimum(n + 1, nb - 1), h)),
    ]
    return q_spec, kv_specs


def _attn_fwd(q, k, v, sink):
    t = q.shape[0]
    nb = t // ATTN_BLOCK

    def body(sink_ref, q_ref, kp_ref, kc_ref, kn_ref, vp_ref, vc_ref, vn_ref, o_ref):
        h, n = pl.program_id(0), pl.program_id(1)
        valid = _attn_valid(n, t)
        kcat = jnp.concatenate([kp_ref[...], kc_ref[...], kn_ref[...]], axis=0).astype(F32)
        vcat = jnp.concatenate([vp_ref[...], vc_ref[...], vn_ref[...]], axis=0).astype(F32)
        for g in range(GQA_GROUP):
            cols = slice(g * HEAD_DIM, (g + 1) * HEAD_DIM)
            sink_col = jnp.full((ATTN_BLOCK, 1), sink_ref[h * GQA_GROUP + g], F32)
            o_ref[:, cols] = _attn_head(q_ref[:, cols].astype(F32), kcat, vcat, sink_col, valid).astype(BF16)

    q_spec, kv_specs = _attn_specs(nb)
    return pl.pallas_call(
        body, name="attn_fwd", grid=(KV_HEADS, nb),
        in_specs=[pl.BlockSpec(memory_space=pltpu.SMEM), q_spec] + kv_specs + kv_specs,
        out_specs=q_spec,
        out_shape=jax.ShapeDtypeStruct((t, ATTN_WIDTH), BF16),
        compiler_params=_params(("parallel", "parallel"), VMEM_LIMIT),
    )(sink, q, k, k, k, v, v, v)


def _attn_bwd(q, k, v, sink, dmix):
    t = q.shape[0]
    nb = t // ATTN_BLOCK

    def body(sink_ref, q_ref, kp_ref, kc_ref, kn_ref, vp_ref, vc_ref, vn_ref, do_ref,
             dq_ref, dk_lo, dk_mid, dk_hi, dv_lo, dv_mid, dv_hi, dsink_ref):
        h, n = pl.program_id(0), pl.program_id(1)
        valid = _attn_valid(n, t)
        kcat = jnp.concatenate([kp_ref[...], kc_ref[...], kn_ref[...]], axis=0).astype(F32)
        vcat = jnp.concatenate([vp_ref[...], vc_ref[...], vn_ref[...]], axis=0).astype(F32)
        dk = jnp.zeros((3 * ATTN_BLOCK, HEAD_DIM), F32)
        dv = jnp.zeros((3 * ATTN_BLOCK, HEAD_DIM), F32)
        row = lax.broadcasted_iota(jnp.int32, (8, HEAD_DIM), 0)
        dsink = jnp.zeros((8, HEAD_DIM), F32)
        for g in range(GQA_GROUP):
            cols = slice(g * HEAD_DIM, (g + 1) * HEAD_DIM)
            sink_col = jnp.full((ATTN_BLOCK, 1), sink_ref[h * GQA_GROUP + g], F32)
            _, vjp = jax.vjp(functools.partial(_attn_head, valid=valid),
                             q_ref[:, cols].astype(F32), kcat, vcat, sink_col)
            dq_g, dk_g, dv_g, dsink_g = vjp(do_ref[:, cols])
            dq_ref[:, cols] = dq_g
            dk, dv = dk + dk_g, dv + dv_g
            dsink = dsink + jnp.where(row == g, jnp.sum(dsink_g), 0.0)
        for i, (dk_ref, dv_ref) in enumerate(((dk_lo, dv_lo), (dk_mid, dv_mid), (dk_hi, dv_hi))):
            rows = slice(i * ATTN_BLOCK, (i + 1) * ATTN_BLOCK)
            dk_ref[...] = dk[rows]
            dv_ref[...] = dv[rows]

        @pl.when(n == 0)
        def _():
            dsink_ref[...] = jnp.zeros_like(dsink_ref)

        dsink_ref[...] += dsink

    q_spec, kv_specs = _attn_specs(nb)
    kv_out = pl.BlockSpec((ATTN_BLOCK, HEAD_DIM), lambda h, n: (n, h))
    kv_shape = jax.ShapeDtypeStruct((t, KV_WIDTH), F32)
    return pl.pallas_call(
        body, name="attn_bwd", grid=(KV_HEADS, nb),
        in_specs=[pl.BlockSpec(memory_space=pltpu.SMEM), q_spec] + kv_specs + kv_specs + [q_spec],
        out_specs=[q_spec] + [kv_out] * 6 + [pl.BlockSpec((None, 8, HEAD_DIM), lambda h, n: (h, 0, 0))],
        out_shape=[jax.ShapeDtypeStruct((t, ATTN_WIDTH), F32)] + [kv_shape] * 6
                  + [jax.ShapeDtypeStruct((KV_HEADS, 8, HEAD_DIM), F32)],
        compiler_params=_params(("parallel", "arbitrary"), VMEM_LIMIT),
    )(sink, q, k, k, k, v, v, v, dmix)


def _attn_prep_bwd(proj, cos, sin_signed, gq, gk, dq, dks, dvs):
    t = proj.shape[0]
    tr = ATTN_BLOCK
    nb = t // tr

    def body(q_ref, k_ref, cos_ref, sin_ref, gq_ref, gk_ref, dq_ref,
             dk_lo, dk_mid, dk_hi, dv_lo, dv_mid, dv_hi,
             dqo_ref, dko_ref, dvo_ref, dgq_ref, dgk_ref):
        n = pl.program_id(0)
        cos_t, sin_t = cos_ref[...], sin_ref[...]
        has_next = (n < nb - 1).astype(F32)
        has_prev = (n > 0).astype(F32)
        dk = dk_lo[...] * has_next + dk_mid[...] + dk_hi[...] * has_prev
        dv = dv_lo[...] * has_next + dv_mid[...] + dv_hi[...] * has_prev
        dvo_ref[...] = dv.astype(BF16)
        dgq = jnp.zeros((1, HEAD_DIM), F32)
        dgk = jnp.zeros((1, HEAD_DIM), F32)
        for h in range(ATTN_HEADS):
            cols = slice(h * HEAD_DIM, (h + 1) * HEAD_DIM)
            _, vjp = jax.vjp(_rms, q_ref[:, cols], gq_ref[...])
            dx, dg = vjp(_rope_transposed(dq_ref[:, cols], cos_t, sin_t))
            dqo_ref[:, cols] = dx.astype(BF16)
            dgq = dgq + dg
        for h in range(KV_HEADS):
            cols = slice(h * HEAD_DIM, (h + 1) * HEAD_DIM)
            _, vjp = jax.vjp(_rms, k_ref[:, cols], gk_ref[...])
            dx, dg = vjp(_rope_transposed(dk[:, cols], cos_t, sin_t))
            dko_ref[:, cols] = dx.astype(BF16)
            dgk = dgk + dg

        @pl.when(n == 0)
        def _():
            dgq_ref[...] = jnp.zeros_like(dgq_ref)
            dgk_ref[...] = jnp.zeros_like(dgk_ref)

        dgq_ref[...] += dgq
        dgk_ref[...] += dgk

    qb, kb = _seg_block("qa", ATTN_WIDTH), _seg_block("ka", KV_WIDTH)
    tab = pl.BlockSpec((tr, HEAD_DIM), lambda i: (i, 0))
    vec = pl.BlockSpec((1, HEAD_DIM), lambda i: (0, 0))
    kv = [pl.BlockSpec((tr, KV_WIDTH), lambda i: (jnp.minimum(i + 1, nb - 1), 0)),
          pl.BlockSpec((tr, KV_WIDTH), lambda i: (i, 0)),
          pl.BlockSpec((tr, KV_WIDTH), lambda i: (jnp.maximum(i - 1, 0), 0))]
    wide = pl.BlockSpec((tr, ATTN_WIDTH), lambda i: (i, 0))
    narrow = pl.BlockSpec((tr, KV_WIDTH), lambda i: (i, 0))
    return pl.pallas_call(
        body, name="attn_prep_bwd", grid=(nb,),
        in_specs=[pl.BlockSpec((tr, ATTN_WIDTH), lambda i: (i, qb)),
                  pl.BlockSpec((tr, KV_WIDTH), lambda i: (i, kb)),
                  tab, tab, vec, vec, wide] + kv + kv,
        out_specs=[wide, narrow, narrow, vec, vec],
        out_shape=[jax.ShapeDtypeStruct((t, ATTN_WIDTH), BF16),
                   jax.ShapeDtypeStruct((t, KV_WIDTH), BF16),
                   jax.ShapeDtypeStruct((t, KV_WIDTH), BF16),
                   jax.ShapeDtypeStruct((1, HEAD_DIM), F32),
                   jax.ShapeDtypeStruct((1, HEAD_DIM), F32)],
        compiler_params=_params(("arbitrary",), VMEM_LIMIT),
    )(proj, proj, cos, sin_signed, gq, gk, dq, *dks, *dvs)


def _decay_fn(lr, w2, ba):
    return _log_sigmoid(bdot(lr, w2, "nn") + ba) / GLA_GATE_NORMALIZER


def _gla_prep_fwd(proj_lr, w2, ba2, *, tr=512):
    t = proj_lr.shape[0]
    width = 2 * GLA_KEY_WIDTH

    def body(lr_ref, w2_ref, ba_ref, g_ref):
        g_ref[...] = _decay_fn(lr_ref[...], w2_ref[...], ba_ref[...])

    return pl.pallas_call(
        body, name="gla_prep_fwd", grid=(t // tr,),
        in_specs=[pl.BlockSpec((tr, LR_PAD), lambda i: (i, 0)),
                  pl.BlockSpec((LR_PAD, width), lambda i: (0, 0)),
                  pl.BlockSpec((1, width), lambda i: (0, 0))],
        out_specs=pl.BlockSpec((tr, width), lambda i: (i, 0)),
        out_shape=jax.ShapeDtypeStruct((t, width), F32),
        compiler_params=_params(("parallel",), VMEM_LIMIT),
    )(proj_lr, w2, ba2)


def _gla_prep_bwd(proj_lr, w2, ba2, dg_f, dg_b, *, tr=512):
    t = proj_lr.shape[0]
    width = 2 * GLA_KEY_WIDTH

    def body(lr_ref, w2_ref, ba_ref, dgf_ref, dgb_ref, dlr_ref, dw2_ref, dba_ref):
        _, vjp = jax.vjp(_decay_fn, lr_ref[...], w2_ref[...], ba_ref[...])
        dlr, dw2, dba = vjp(jnp.concatenate([dgf_ref[...], dgb_ref[...]], axis=1))
        dlr_ref[...] = dlr.astype(BF16)

        @pl.when(pl.program_id(0) == 0)
        def _():
            dw2_ref[...] = jnp.zeros_like(dw2_ref)
            dba_ref[...] = jnp.zeros_like(dba_ref)

        dw2_ref[...] += dw2
        dba_ref[...] += dba

    half = pl.BlockSpec((tr, GLA_KEY_WIDTH), lambda i: (i, 0))
    return pl.pallas_call(
        body, name="gla_prep_bwd", grid=(t // tr,),
        in_specs=[pl.BlockSpec((tr, LR_PAD), lambda i: (i, 0)),
                  pl.BlockSpec((LR_PAD, width), lambda i: (0, 0)),
                  pl.BlockSpec((1, width), lambda i: (0, 0)), half, half],
        out_specs=[pl.BlockSpec((tr, LR_PAD), lambda i: (i, 0)),
                   pl.BlockSpec((LR_PAD, width), lambda i: (0, 0)),
                   pl.BlockSpec((1, width), lambda i: (0, 0))],
        out_shape=[jax.ShapeDtypeStruct((t, LR_PAD), BF16),
                   jax.ShapeDtypeStruct((LR_PAD, width), F32),
                   jax.ShapeDtypeStruct((1, width), F32)],
        compiler_params=_params(("arbitrary",), VMEM_LIMIT),
    )(proj_lr, w2, ba2, dg_f, dg_b)


def _gla_chunk(q, k, v, g, state, forward):
    c = GLA_CHUNK
    row = lax.broadcasted_iota(jnp.int32, (c, c), 0)
    col = lax.broadcasted_iota(jnp.int32, (c, c), 1)
    rid = lax.broadcasted_iota(jnp.int32, (c, GLA_DK), 0)
    q = q * (GLA_DK ** -0.5)
    if forward:
        cum = (row >= col).astype(F32)
        see = row >= col
        upto_ref = rid <= c // 2
    else:
        cum = (row <= col).astype(F32)
        see = row < col
        upto_ref = rid >= c - 1 - c // 2
    b = lax.dot_general(cum, g, _DIMS["nn"], precision=lax.Precision.HIGHEST, preferred_element_type=F32)
    b_last = jnp.sum(g, axis=0, keepdims=True)
    b_ref = jnp.sum(jnp.where(upto_ref, g, 0.0), axis=0, keepdims=True)
    a = bdot(q * jnp.exp(b - b_ref), k * jnp.exp(b_ref - b), "nt")
    a = jnp.where(see, a, 0.0)
    o = bdot(a, v, "nn") + bdot(q * jnp.exp(b), state, "nt")
    new_state = state * jnp.exp(b_last) + bdot(v, k * jnp.exp(b_last - b), "tn")
    return o, new_state


def _gla_fwd(proj, g):
    t = proj.shape[0]
    c = GLA_CHUNK
    nchunk = t // c
    qb, kb, vb = _seg_block("qg", GLA_KEY_WIDTH), _seg_block("kg", GLA_KEY_WIDTH), _seg_block("vg", GLA_WIDTH)

    def body(qf, kf, vf, gf, qr, kr, vr, gr, of_ref, ob_ref, sf_ref, sb_ref, state):
        @pl.when(pl.program_id(0) == 0)
        def _():
            state[...] = jnp.zeros_like(state)

        for d, (q_ref, k_ref, v_ref, g_ref, o_ref, s_ref) in enumerate(
                ((qf, kf, vf, gf, of_ref, sf_ref), (qr, kr, vr, gr, ob_ref, sb_ref))):
            for h in range(GLA_HEADS):
                kc = slice(h * GLA_DK, (h + 1) * GLA_DK)
                vc = slice(h * GLA_DV, (h + 1) * GLA_DV)
                s_in = state[d, h]
                s_ref[h] = s_in
                o, s_out = _gla_chunk(q_ref[:, kc], k_ref[:, kc], v_ref[:, vc], g_ref[:, kc], s_in, d == 0)
                o_ref[:, vc] = o
                state[d, h] = s_out

    specs, outs = [], []
    for d in range(2):
        ci = (lambda i: i) if d == 0 else (lambda i: nchunk - 1 - i)
        specs += [pl.BlockSpec((c, GLA_KEY_WIDTH), lambda i, ci=ci: (ci(i), qb)),
                  pl.BlockSpec((c, GLA_KEY_WIDTH), lambda i, ci=ci: (ci(i), kb)),
                  pl.BlockSpec((c, GLA_WIDTH), lambda i, ci=ci: (ci(i), vb)),
                  pl.BlockSpec((c, GLA_KEY_WIDTH), lambda i, ci=ci, d=d: (ci(i), d))]
        outs.append(pl.BlockSpec((c, GLA_WIDTH), lambda i, ci=ci: (ci(i), 0)))
    for d in range(2):
        ci = (lambda i: i) if d == 0 else (lambda i: nchunk - 1 - i)
        outs.append(pl.BlockSpec((None, GLA_HEADS, GLA_DV, GLA_DK), lambda i, ci=ci: (ci(i), 0, 0, 0)))
    o_shape = jax.ShapeDtypeStruct((t, GLA_WIDTH), F32)
    s_shape = jax.ShapeDtypeStruct((nchunk, GLA_HEADS, GLA_DV, GLA_DK), F32)
    return pl.pallas_call(
        body, name="gla_fwd", grid=(nchunk,),
        in_specs=specs, out_specs=outs,
        out_shape=[o_shape, o_shape, s_shape, s_shape],
        scratch_shapes=[pltpu.VMEM((2, GLA_HEADS, GLA_DV, GLA_DK), F32)],
        compiler_params=_params(("arbitrary",), VMEM_LIMIT),
    )(proj, proj, proj, g, proj, proj, proj, g)


def _gla_bwd(proj, g, s_f, s_b, do):
    t = proj.shape[0]
    c = GLA_CHUNK
    nchunk = t // c
    qb, kb, vb = _seg_block("qg", GLA_KEY_WIDTH), _seg_block("kg", GLA_KEY_WIDTH), _seg_block("vg", GLA_WIDTH)

    def body(*refs):
        ins, outs, dstate = refs[:12], refs[12:20], refs[20]

        @pl.when(pl.program_id(0) == 0)
        def _():
            dstate[...] = jnp.zeros_like(dstate)

        for d in range(2):
            q_ref, k_ref, v_ref, g_ref, s_ref, do_ref = ins[6 * d:6 * d + 6]
            dq_ref, dk_ref, dv_ref, dg_ref = outs[4 * d:4 * d + 4]
            for h in range(GLA_HEADS):
                kc = slice(h * GLA_DK, (h + 1) * GLA_DK)
                vc = slice(h * GLA_DV, (h + 1) * GLA_DV)
                _, vjp = jax.vjp(functools.partial(_gla_chunk, forward=(d == 0)),
                                 q_ref[:, kc], k_ref[:, kc], v_ref[:, vc], g_ref[:, kc], s_ref[h])
                dq, dk, dv, dg, ds = vjp((do_ref[:, vc], dstate[d, h]))
                dq_ref[:, kc] = dq
                dk_ref[:, kc] = dk
                dv_ref[:, vc] = dv
                dg_ref[:, kc] = dg
                dstate[d, h] = ds

    specs, outs, shapes = [], [], []
    for d in range(2):
        ci = (lambda i: nchunk - 1 - i) if d == 0 else (lambda i: i)
        specs += [pl.BlockSpec((c, GLA_KEY_WIDTH), lambda i, ci=ci: (ci(i), qb)),
                  pl.BlockSpec((c, GLA_KEY_WIDTH), lambda i, ci=ci: (ci(i), kb)),
                  pl.BlockSpec((c, GLA_WIDTH), lambda i, ci=ci: (ci(i), vb)),
                  pl.BlockSpec((c, GLA_KEY_WIDTH), lambda i, ci=ci, d=d: (ci(i), d)),
                  pl.BlockSpec((None, GLA_HEADS, GLA_DV, GLA_DK), lambda i, ci=ci: (ci(i), 0, 0, 0)),
                  pl.BlockSpec((c, GLA_WIDTH), lambda i, ci=ci: (ci(i), 0))]
        key = pl.BlockSpec((c, GLA_KEY_WIDTH), lambda i, ci=ci: (ci(i), 0))
        val = pl.BlockSpec((c, GLA_WIDTH), lambda i, ci=ci: (ci(i), 0))
        outs += [key, key, val, key]
        shapes += [jax.ShapeDtypeStruct((t, GLA_KEY_WIDTH), F32), jax.ShapeDtypeStruct((t, GLA_KEY_WIDTH), F32),
                   jax.ShapeDtypeStruct((t, GLA_WIDTH), F32), jax.ShapeDtypeStruct((t, GLA_KEY_WIDTH), F32)]
    return pl.pallas_call(
        body, name="gla_bwd", grid=(nchunk,),
        in_specs=specs, out_specs=outs, out_shape=shapes,
        scratch_shapes=[pltpu.VMEM((2, GLA_HEADS, GLA_DV, GLA_DK), F32)],
        compiler_params=_params(("arbitrary",), VMEM_LIMIT),
    )(proj, proj, proj, g, s_f, do, proj, proj, proj, g, s_b, do)


def _gla_out_head(o_f, o_b, gate, gn):
    return _rms(o_f + o_b, gn) * _silu(gate)


def _gla_out_fwd(o_f, o_b, proj, gn, *, tr=256):
    t = o_f.shape[0]
    gb = _seg_block("gate", GLA_WIDTH)

    def body(of_ref, ob_ref, gate_ref, gn_ref, out_ref):
        for h in range(GLA_HEADS):
            vc = slice(h * GLA_DV, (h + 1) * GLA_DV)
            out_ref[:, vc] = _gla_out_head(of_ref[:, vc], ob_ref[:, vc], gate_ref[:, vc], gn_ref[...]).astype(BF16)

    wide = pl.BlockSpec((tr, GLA_WIDTH), lambda i: (i, 0))
    return pl.pallas_call(
        body, name="gla_out_fwd", grid=(t // tr,),
        in_specs=[wide, wide, pl.BlockSpec((tr, GLA_WIDTH), lambda i: (i, gb)),
                  pl.BlockSpec((1, GLA_DV), lambda i: (0, 0))],
        out_specs=wide,
        out_shape=jax.ShapeDtypeStruct((t, GLA_WIDTH), BF16),
        compiler_params=_params(("parallel",), VMEM_LIMIT),
    )(o_f, o_b, proj, gn)


def _gla_out_bwd(o_f, o_b, proj, gn, dmix, *, tr=256):
    t = o_f.shape[0]
    gb = _seg_block("gate", GLA_WIDTH)

    def body(of_ref, ob_ref, gate_ref, gn_ref, dout_ref, do_ref, dgate_ref, dgn_ref):
        dgn = jnp.zeros((1, GLA_DV), F32)
        for h in range(GLA_HEADS):
            vc = slice(h * GLA_DV, (h + 1) * GLA_DV)
            _, vjp = jax.vjp(_gla_out_head, of_ref[:, vc], ob_ref[:, vc], gate_ref[:, vc], gn_ref[...])
            do, _, dgate, dg = vjp(dout_ref[:, vc])
            do_ref[:, vc] = do
            dgate_ref[:, vc] = dgate.astype(BF16)
            dgn = dgn + dg

        @pl.when(pl.program_id(0) == 0)
        def _():
            dgn_ref[...] = jnp.zeros_like(dgn_ref)

        dgn_ref[...] += dgn

    wide = pl.BlockSpec((tr, GLA_WIDTH), lambda i: (i, 0))
    vec = pl.BlockSpec((1, GLA_DV), lambda i: (0, 0))
    return pl.pallas_call(
        body, name="gla_out_bwd", grid=(t // tr,),
        in_specs=[wide, wide, pl.BlockSpec((tr, GLA_WIDTH), lambda i: (i, gb)), vec,
                  pl.BlockSpec((tr, GLA_WIDTH), lambda i: (i, 1))],
        out_specs=[wide, wide, vec],
        out_shape=[jax.ShapeDtypeStruct((t, GLA_WIDTH), F32), jax.ShapeDtypeStruct((t, GLA_WIDTH), BF16),
                   jax.ShapeDtypeStruct((1, GLA_DV), F32)],
        compiler_params=_params(("arbitrary",), VMEM_LIMIT),
    )(o_f, o_b, proj, gn, dmix)


CONV_TR = 512
CONV_TC = 256
HALO = 8
HALO16 = 16


def _conv3(u, w, b):
    n = u.shape[0]
    return pltpu.roll(u, 1, 0) * w[0:1] + u * w[1:2] + pltpu.roll(u, n - 1, 0) * w[2:3] + b


def _conv_ext(main_ref, prev_ref, next_ref, r, nr):
    prev = prev_ref[...].astype(F32)[-HALO:] * (r > 0).astype(F32)
    nxt = next_ref[...].astype(F32)[:HALO] * (r < nr - 1).astype(F32)
    return jnp.concatenate([prev, main_ref[...].astype(F32), nxt], axis=0)


def _conv_specs(t, halo, half=None):
    per = CONV_TR // halo
    last = t // halo - 1
    lead = () if half is None else (None,)
    at = (lambda *ix: ix) if half is None else (lambda *ix: (half,) + ix)
    return [pl.BlockSpec(lead + (CONV_TR, CONV_TC), lambda j, r: at(r, j)),
            pl.BlockSpec(lead + (halo, CONV_TC), lambda j, r: at(jnp.maximum(r * per - 1, 0), j)),
            pl.BlockSpec(lead + (halo, CONV_TC), lambda j, r: at(jnp.minimum((r + 1) * per, last), j))]


def _ffn_mid_fwd(u, cw_g, cw_v, cb_g, cb_v):
    _, t, f = u.shape
    nr = t // CONV_TR

    def body(ug, ugp, ugn, uv, uvp, uvn, wg, wv, bg, bv, a_ref):
        r = pl.program_id(1)
        gate = _conv3(_conv_ext(ug, ugp, ugn, r, nr), wg[...], bg[...])[HALO:HALO + CONV_TR]
        val = _conv3(_conv_ext(uv, uvp, uvn, r, nr), wv[...], bv[...])[HALO:HALO + CONV_TR]
        a_ref[...] = (_silu(gate) * val).astype(BF16)

    w_spec = pl.BlockSpec((3, CONV_TC), lambda j, r: (0, j))
    b_spec = pl.BlockSpec((1, CONV_TC), lambda j, r: (0, j))
    return pl.pallas_call(
        body, name="ffn_mid_fwd", grid=(f // CONV_TC, nr),
        in_specs=_conv_specs(t, HALO, 0) + _conv_specs(t, HALO, 1) + [w_spec, w_spec, b_spec, b_spec],
        out_specs=pl.BlockSpec((CONV_TR, CONV_TC), lambda j, r: (r, j)),
        out_shape=jax.ShapeDtypeStruct((t, f), BF16),
        compiler_params=_params(("parallel", "parallel"), VMEM_LIMIT),
    )(u, u, u, u, u, u, cw_g, cw_v, cb_g, cb_v)


def _ffn_mid_bwd(u, cw_g, cw_v, cb_g, cb_v, da):
    _, t, f = u.shape
    nr = t // CONV_TR
    ext = CONV_TR + 2 * HALO

    def body(ug, ugp, ugn, uv, uvp, uvn, dam, dap, dan, wg, wv, bg, bv,
             du_ref, dwg_ref, dwv_ref, dbg_ref, dbv_ref):
        r = pl.program_id(1)
        xg = _conv_ext(ug, ugp, ugn, r, nr)
        xv = _conv_ext(uv, uvp, uvn, r, nr)
        da_x = _conv_ext(dam, dap, dan, r, nr)
        gate = _conv3(xg, wg[...], bg[...])
        val = _conv3(xv, wv[...], bv[...])
        pos = r * CONV_TR - HALO + lax.broadcasted_iota(jnp.int32, (ext, CONV_TC), 0)
        inside = (pos >= 0) & (pos < t)
        sig = jax.nn.sigmoid(gate)
        silu = gate * sig
        d_val = jnp.where(inside, da_x * silu, 0.0)
        d_gate = jnp.where(inside, da_x * val * (sig + silu * (1.0 - sig)), 0.0)
        own = lax.broadcasted_iota(jnp.int32, (ext, CONV_TC), 0)
        own = (own >= HALO) & (own < HALO + CONV_TR)
        for half, (x, d, w, dw_ref, db_ref) in enumerate(((xg, d_gate, wg, dwg_ref, dbg_ref),
                                                          (xv, d_val, wv, dwv_ref, dbv_ref))):
            wt = w[...]
            du = pltpu.roll(d, ext - 1, 0) * wt[0:1] + d * wt[1:2] + pltpu.roll(d, 1, 0) * wt[2:3]
            du_ref[half] = du[HALO:HALO + CONV_TR].astype(BF16)
            d_own = jnp.where(own, d, 0.0)
            dw = jnp.concatenate([
                jnp.sum(pltpu.roll(x, 1, 0) * d_own, axis=0, keepdims=True),
                jnp.sum(x * d_own, axis=0, keepdims=True),
                jnp.sum(pltpu.roll(x, ext - 1, 0) * d_own, axis=0, keepdims=True)], axis=0)
            db = jnp.sum(d_own, axis=0, keepdims=True)

            @pl.when(r == 0)
            def _():
                dw_ref[...] = jnp.zeros_like(dw_ref)
                db_ref[...] = jnp.zeros_like(db_ref)

            dw_ref[...] += dw
            db_ref[...] += db

    w_spec = pl.BlockSpec((3, CONV_TC), lambda j, r: (0, j))
    b_spec = pl.BlockSpec((1, CONV_TC), lambda j, r: (0, j))
    return pl.pallas_call(
        body, name="ffn_mid_bwd", grid=(f // CONV_TC, nr),
        in_specs=(_conv_specs(t, HALO, 0) + _conv_specs(t, HALO, 1) + _conv_specs(t, HALO16)
                  + [w_spec, w_spec, b_spec, b_spec]),
        out_specs=[pl.BlockSpec((2, CONV_TR, CONV_TC), lambda j, r: (0, r, j)), w_spec, w_spec, b_spec, b_spec],
        out_shape=[jax.ShapeDtypeStruct((2, t, f), BF16),
                   jax.ShapeDtypeStruct((3, f), F32), jax.ShapeDtypeStruct((3, f), F32),
                   jax.ShapeDtypeStruct((1, f), F32), jax.ShapeDtypeStruct((1, f), F32)],
        compiler_params=_params(("parallel", "arbitrary"), VMEM_LIMIT),
    )(u, u, u, u, u, u, da, da, da, cw_g, cw_v, cb_g, cb_v)


def _loss_head(y, target, *, tr=256):
    t, d = y.shape

    def body(y_ref, t_ref, loss_ref, dy_ref, dyb_ref):
        err = y_ref[...] - t_ref[...]
        dy = err * (1.0 / d)
        dy_ref[...] = dy
        dyb_ref[...] = dy.astype(BF16)
        part = 0.5 * jnp.sum(jnp.sum(err * err, axis=-1, keepdims=True) * (1.0 / d), axis=0, keepdims=True)

        @pl.when(pl.program_id(0) == 0)
        def _():
            loss_ref[...] = jnp.zeros_like(loss_ref)

        loss_ref[...] += jnp.broadcast_to(part, loss_ref.shape)

    row = pl.BlockSpec((tr, d), lambda i: (i, 0))
    return pl.pallas_call(
        body, name="loss_head", grid=(t // tr,),
        in_specs=[row, row],
        out_specs=[pl.BlockSpec((1, 128), lambda i: (0, 0)), row, row],
        out_shape=[jax.ShapeDtypeStruct((1, 128), F32), jax.ShapeDtypeStruct((t, d), F32),
                   jax.ShapeDtypeStruct((t, d), BF16)],
        compiler_params=_params(("arbitrary",), VMEM_LIMIT),
    )(y, target)


ANY = pl.BlockSpec(memory_space=pl.ANY)


def _position():
    return lax.axis_index("x"), lax.axis_index("y"), lax.axis_index("c")


def _other_chips(x, y):
    return [(1 - x, y), (x, 1 - y), (1 - x, 1 - y)]


def _handshake(peers):
    barrier = pltpu.get_barrier_semaphore()
    for peer in peers:
        pl.semaphore_signal(barrier, inc=1, device_id=peer, device_id_type=MESH)
    pl.semaphore_wait(barrier, len(peers))


def _exchange(body, operands, out_shapes, sems, *, name, collective_id):
    n_in, n_out = len(operands), len(out_shapes)

    def run(*refs):
        body(refs[:n_in], refs[n_in:n_in + n_out], *refs[n_in + n_out:])

    if collective_id is None:
        return pl.pallas_call(run, name=name, in_specs=[ANY] * n_in, out_specs=[ANY] * n_out,
                              out_shape=out_shapes, scratch_shapes=sems)(*operands)
    return pl.kernel(run, name=name, out_type=out_shapes,
                     mesh=plsc.ScalarSubcoreMesh(axis_name="sequencer", num_cores=1), scratch_types=sems,
                     compiler_params=pltpu.CompilerParams(collective_id=collective_id))(*operands)


def _all_gather(blocks, *, name, collective_id=None):
    na = len(blocks)

    def body(ins, outs, send_sems, recv_sems, local_sems):
        x, y, c = _position()
        sibling = (x, y, 1 - c)
        chips = _other_chips(x, y)
        if collective_id is not None:
            _handshake([sibling] + [(*chip, c) for chip in chips])

        def index(px, py, pc):
            return 4 * px + 2 * py + pc

        def copy(a, k, block, to, src=None):
            dst = outs[a].at[index(*block)]
            return pltpu.make_async_remote_copy(
                src_ref=dst if src is None else src, dst_ref=dst,
                send_sem=send_sems.at[a, k], recv_sem=recv_sems.at[a, k],
                device_id=to, device_id_type=MESH)

        pending = []
        for a in range(na):
            mine = pltpu.make_async_copy(ins[a], outs[a].at[index(x, y, c)], local_sems.at[a])
            mine.start()
            pending.append(mine)
        first = []
        for a in range(na):
            first.append(copy(a, 0, (x, y, c), sibling, src=ins[a]))
            first += [copy(a, 1 + j, (x, y, c), (*chip, c), src=ins[a]) for j, chip in enumerate(chips)]
        for cp in first:
            cp.start()
        passed = []
        for j, chip in enumerate(chips):
            for a in range(na):
                copy(a, 1 + j, (*chip, c), (x, y, c)).wait_recv()
                fwd = copy(a, 4 + j, (*chip, c), sibling)
                fwd.start()
                passed.append(fwd)
        for a in range(na):
            copy(a, 0, sibling, (x, y, c)).wait_recv()
            for j, chip in enumerate(chips):
                copy(a, 4 + j, (*chip, 1 - c), (x, y, c)).wait_recv()
        for cp in first + passed:
            cp.wait_send()
        for cp in pending:
            cp.wait()

    return _exchange(
        body, blocks, [jax.ShapeDtypeStruct((N_DEV,) + b.shape, b.dtype) for b in blocks],
        [pltpu.SemaphoreType.DMA((na, 7)), pltpu.SemaphoreType.DMA((na, 7)), pltpu.SemaphoreType.DMA((na,))],
        name=name, collective_id=collective_id)


def _grad_exchange(grads, parts, *, name, collective_id):
    ng, npart = len(grads), len(parts)

    def body(ins, outs, core_send, core_recv, chip_send, chip_recv, local_sems):
        x, y, c = _position()
        sibling = (x, y, 1 - c)
        chips = _other_chips(x, y)
        _handshake([sibling] + [(px, py, c) for px, py in chips])
        me = 2 * x + y
        copies = []
        for b in range(npart):
            src, dst = ins[ng + b], outs[ng + b]
            own = pltpu.make_async_copy(src.at[me], dst.at[me], local_sems.at[b])
            own.start()
            copies.append(own)
            for j, (px, py) in enumerate(chips):
                cp = pltpu.make_async_remote_copy(
                    src_ref=src.at[2 * px + py], dst_ref=dst.at[me],
                    send_sem=chip_send.at[b, j], recv_sem=chip_recv.at[b, j],
                    device_id=(px, py, c), device_id_type=MESH)
                cp.start()
                copies.append(cp)
        for a in range(ng):
            for k in range(N_CHIP):
                cp = pltpu.make_async_remote_copy(
                    src_ref=ins[a].at[k, 1 - c], dst_ref=outs[a].at[k],
                    send_sem=core_send.at[a, k], recv_sem=core_recv.at[a, k],
                    device_id=sibling, device_id_type=MESH)
                cp.start()
                copies.append(cp)
        for cp in copies:
            cp.wait()

    shapes = ([jax.ShapeDtypeStruct((N_CHIP,) + g.shape[2:], g.dtype) for g in grads]
              + [jax.ShapeDtypeStruct(p.shape, p.dtype) for p in parts])
    sems = [pltpu.SemaphoreType.DMA((max(ng, 1), N_CHIP)), pltpu.SemaphoreType.DMA((max(ng, 1), N_CHIP)),
            pltpu.SemaphoreType.DMA((max(npart, 1), 3)), pltpu.SemaphoreType.DMA((max(npart, 1), 3)),
            pltpu.SemaphoreType.DMA((max(npart, 1),))]
    out = _exchange(body, list(grads) + list(parts), shapes, sems, name=name, collective_id=collective_id)
    return out[:ng], out[ng:]


def _pair_sum(grad, theirs, core, *, tile, name):
    _, _, r, w = grad.shape
    tr, tw = tile
    assert r % tr == 0 and w % tw == 0

    def body(core_ref, mine_ref, theirs_ref, out_ref):
        out_ref[...] = mine_ref[...] + theirs_ref[...]

    return pl.pallas_call(
        body, name=name,
        grid_spec=pltpu.PrefetchScalarGridSpec(
            num_scalar_prefetch=1, grid=(N_CHIP, r // tr, w // tw),
            in_specs=[pl.BlockSpec((None, None, tr, tw), lambda k, i, j, core_ref: (k, core_ref[0], i, j)),
                      pl.BlockSpec((None, tr, tw), lambda k, i, j, core_ref: (k, i, j))],
            out_specs=pl.BlockSpec((None, tr, tw), lambda k, i, j, core_ref: (k, i, j))),
        out_shape=jax.ShapeDtypeStruct((N_CHIP, r, w), F32),
        compiler_params=_params(("parallel", "parallel", "parallel"), VMEM_LIMIT),
    )(core, grad, theirs)


class _ReduceScatter:
    def __init__(self, core):
        self.core = core
        self.pending = None
        self.results = {}
        self.launches = 0

    def push(self, tag, grads, rows, then):
        pair, prev_tag = [], None
        if self.pending is not None:
            prev_tag, prev, theirs, prev_rows = self.pending
            pair = [_pair_sum(g, s, self.core, tile=tile, name=f"pair_sum_{prev_tag}_{i}")
                    for i, (g, s, tile) in enumerate(zip(prev, theirs, prev_rows))]
        grads, pair, then = lax.optimization_barrier((list(grads), pair, then))
        grads = [g.reshape((N_CHIP, 2) + g.shape[1:]) for g in grads]
        self.launches += 1
        theirs, parts = _grad_exchange(grads, pair, name=f"grad_exchange_{self.launches}",
                                       collective_id=1 + self.launches)
        if prev_tag is not None:
            self.results[prev_tag] = parts
        self.pending = (tag, grads, theirs, rows) if tag is not None else None
        return then

    def result(self, tag):
        return self.results[tag]


def _adamw(parts, w, m, v, *, tile, name):
    n, r, cols = parts.shape
    tr, tw = tile
    assert r % tr == 0 and cols % tw == 0 and w.shape == (r, cols)
    c1 = 1.0 - ADAM_B1 ** ADAM_STEP
    c2 = 1.0 - ADAM_B2 ** ADAM_STEP

    def body(p_ref, w_ref, m_ref, v_ref, g_ref, d_ref, nm_ref, nv_ref):
        g = p_ref[0]
        for k in range(1, n):
            g = g + p_ref[k]
        new_m = ADAM_B1 * m_ref[...] + (1.0 - ADAM_B1) * g
        new_v = ADAM_B2 * v_ref[...] + (1.0 - ADAM_B2) * (g * g)
        m_hat = new_m / c1
        v_hat = new_v / c2
        g_ref[...] = g
        d_ref[...] = -ADAM_LR * (m_hat / (jnp.sqrt(v_hat) + ADAM_EPS) + ADAM_WD * w_ref[...])
        nm_ref[...] = new_m
        nv_ref[...] = new_v

    spec = pl.BlockSpec((tr, tw), lambda i, j: (i, j))
    shape = jax.ShapeDtypeStruct((r, cols), F32)
    return pl.pallas_call(
        body, name=name, grid=(r // tr, cols // tw),
        in_specs=[pl.BlockSpec((n, tr, tw), lambda i, j: (0, i, j)), spec, spec, spec],
        out_specs=[spec] * 4, out_shape=[shape] * 4,
        compiler_params=_params(("parallel", "parallel"), VMEM_LIMIT),
    )(parts, w, m, v)


def _rope_tables(t):
    half = HEAD_DIM // 2
    inv = 1.0 / (ROPE_THETA ** (jnp.arange(half, dtype=F32) / half))
    ang = jnp.arange(t, dtype=jnp.int32).astype(F32)[:, None] * inv[None, :]
    cos, sin = jnp.cos(ang), jnp.sin(ang)
    return jnp.concatenate([cos, cos], axis=1), jnp.concatenate([-sin, sin], axis=1)


def _to_kernel_rows(w_t):
    order = sorted(SEGMENTS.values())
    main = jnp.concatenate([w_t[src:src + width] for _, src, width in order], axis=0)
    lr = jnp.pad(w_t[IN_MAIN:IN_TOTAL], ((0, LR_PAD - (IN_TOTAL - IN_MAIN)), (0, 0)))
    return main, lr


def _pack(pieces):
    flat = []
    for p in pieces:
        p = p.reshape(-1)
        flat.append(jnp.pad(p, (0, (-p.shape[0]) % 128)))
    return jnp.concatenate(flat).reshape(-1, 128)


def _unpack(packed, shapes):
    flat = packed.reshape(-1)
    out, off = [], 0
    for s in shapes:
        size = 1
        for dim in s:
            size *= dim
        out.append(flat[off:off + size].reshape(s))
        off += size + (-size) % 128
    return out


def _local_step(xs, target, norm1_g, w_main_t, w_lr_t, gq, gk, attn_sink, w2, ba2, gla_norm_g, w_out_full, norm2_g,
                w_up8, cw_g, cw_v, cb_g, cb_v, w_down_full, rs=None):
    t = xs.shape[0]
    tm = min(1024, t)
    cos, sin_signed = _rope_tables(t)
    sink = attn_sink.reshape(ATTN_HEADS)

    h1 = _rmsnorm_fwd(xs, norm1_g, name="norm1_fwd")
    proj = _matmul(h1, w_main_t, "nt", tm=tm, tn=512, tk=D_MODEL, name="proj_main")
    proj_lr = _matmul(h1, w_lr_t, "nt", tm=tm, tn=LR_PAD, tk=D_MODEL, name="proj_lr")
    qa, ka, va = _attn_prep_fwd(proj, cos, sin_signed, gq, gk)
    o_attn = _attn_fwd(qa, ka, va, sink)
    g_dec = _gla_prep_fwd(proj_lr, w2, ba2)
    o_f, o_b, s_f, s_b = _gla_fwd(proj, g_dec)
    o_gla = _gla_out_fwd(o_f, o_b, proj, gla_norm_g)
    mix = jnp.concatenate([o_attn, o_gla], axis=1)
    x1 = _matmul(mix, w_out_full, "nn", tm=tm, tn=512, tk=D_MODEL, res=xs, name="out_proj")
    h2 = _rmsnorm_fwd(x1, norm2_g, name="norm2_fwd")
    u = _up_proj(h2, w_up8, tm=tm)
    act = _ffn_mid_fwd(u, cw_g, cw_v, cb_g, cb_v)
    y = _matmul(act, w_down_full, "nn", tm=tm, tn=512, tk=D_FF // 2, res=x1, name="down_proj")
    loss_part, dy, dy_b = _loss_head(y, target)

    d_act = _matmul(dy_b, w_down_full, "nt", tm=tm, tn=512, tk=D_MODEL, out_dtype=BF16, name="d_act")
    dw_down = _matmul(act, dy_b, "tn", tm=512, tn=1024, tk=t, name="dw_down")
    if rs is not None:
        d_act = rs.push("w_down", [dw_down.reshape(N_DEV, D_FF // N_DEV, D_MODEL)], [(64, D_MODEL)], d_act)
    du, dcw_g, dcw_v, dcb_g, dcb_v = _ffn_mid_bwd(u, cw_g, cw_v, cb_g, cb_v, d_act)
    dw_up8 = _up_proj_dw(h2, du, tm=1024, tk=min(2048, t))
    if rs is not None:
        du = rs.push("w_up", [dw_up8], [(256, UP_BLOCK)], du)
    dh2 = _up_proj_dx(du, w_up8, tm=tm, tn=512)
    dx1, dx1_b, d_norm2 = _rmsnorm_bwd(x1, norm2_g, dh2, dy, name="norm2_bwd")
    dmix = _matmul(dx1_b, w_out_full, "nt", tm=tm, tn=512, tk=D_MODEL, name="d_mix")
    dw_out = _matmul(mix, dx1_b, "tn", tm=1024, tn=512, tk=t, name="dw_out")
    if rs is not None:
        dmix = rs.push("w_out", [dw_out.reshape(N_DEV, D_MODEL // N_DEV, D_MODEL)], [(256, D_MODEL)], dmix)
    do_gla, d_gate, d_gla_norm = _gla_out_bwd(o_f, o_b, proj, gla_norm_g, dmix)
    (dq_f, dk_f, dv_f, dg_f, dq_b, dk_b, dv_b, dg_b) = _gla_bwd(proj, g_dec, s_f, s_b, do_gla)
    d_lr, d_w2, d_ba2 = _gla_prep_bwd(proj_lr, w2, ba2, dg_f, dg_b)
    dqa, dk_lo, dk_mid, dk_hi, dv_lo, dv_mid, dv_hi, d_sink8 = _attn_bwd(qa, ka, va, sink, dmix)
    d_qa, d_ka, d_va, d_qn, d_kn = _attn_prep_bwd(proj, cos, sin_signed, gq, gk, dqa,
                                                  (dk_lo, dk_mid, dk_hi), (dv_lo, dv_mid, dv_hi))
    d_seg = {"qa": d_qa, "gate": d_gate, "vg": (dv_f + dv_b).astype(BF16), "qg": (dq_f + dq_b).astype(BF16),
             "kg": (dk_f + dk_b).astype(BF16), "ka": d_ka, "va": d_va}
    d_proj = jnp.concatenate([d_seg[k] for k in sorted(SEGMENTS, key=lambda k: SEGMENTS[k][0])], axis=1)
    dw_in_t = _in_proj_dw_lr(_in_proj_dw(d_proj, h1, tn=512), d_lr, h1)
    if rs is not None:
        per_in, per_wa = IN_TOTAL // N_DEV, GLA_KEY_WIDTH // N_DEV
        dconv_w = jnp.concatenate([dcw_g, dcw_v], axis=1)
        d_wa2_f = d_w2[:GLA_RANK, :GLA_KEY_WIDTH]
        d_wa2_b = d_w2[GLA_RANK:2 * GLA_RANK, GLA_KEY_WIDTH:]
        small_grad = jnp.stack([
            _pack([dconv_w[:, d * UP_BLOCK:(d + 1) * UP_BLOCK], d_wa2_f[:, d * per_wa:(d + 1) * per_wa],
                   d_wa2_b[:, d * per_wa:(d + 1) * per_wa]]) for d in range(N_DEV)])
        d_proj, d_lr = rs.push("w_in", [dw_in_t.reshape(N_DEV, per_in, D_MODEL), small_grad],
                               [(per_in, 512), small_grad.shape[1:]], (d_proj, d_lr))
    dh1 = _matmul(d_lr, w_lr_t, "nn", tm=tm, tn=512, tk=LR_PAD, name="dh1_lr")
    dh1 = _matmul(d_proj, w_main_t, "nn", tm=tm, tn=512, tk=IN_MAIN // 2, res=dh1, name="dh1_main")
    if rs is not None:
        dh1 = rs.push(None, [], [], dh1)
    grad_x, _, d_norm1 = _rmsnorm_bwd(xs, norm1_g, dh1, dx1, name="norm1_bwd")
    return (loss_part, grad_x, dw_in_t, dw_out, dw_up8, dw_down, dcw_g, dcw_v, dcb_g, dcb_v,
            d_w2, d_ba2, d_norm1, d_norm2, d_qn, d_kn, d_sink8, d_gla_norm)


def kernel(x, norm1_g, w_in, attn_q_norm_g, attn_k_norm_g, attn_sink, gla_wa2_fwd, gla_ba_fwd, gla_wa2_bwd, gla_ba_bwd, gla_out_norm_g, w_out, norm2_g, w_up, conv_w, conv_b, w_down, loss_target, m_norm1_g, m_w_in, m_attn_q_norm_g, m_attn_k_norm_g, m_attn_sink, m_gla_wa2_fwd, m_gla_ba_fwd, m_gla_wa2_bwd, m_gla_ba_bwd, m_gla_out_norm_g, m_w_out, m_norm2_g, m_w_up, m_conv_w, m_conv_b, m_w_down, v_norm1_g, v_w_in, v_attn_q_norm_g, v_attn_k_norm_g, v_attn_sink, v_gla_wa2_fwd, v_gla_ba_fwd, v_gla_wa2_bwd, v_gla_ba_bwd, v_gla_out_norm_g, v_w_out, v_norm2_g, v_w_up, v_conv_w, v_conv_b, v_w_down):
    t = x.shape[1]
    xs = x.reshape(t, D_MODEL)
    target = loss_target.reshape(t, D_MODEL)
    core = lax.axis_index("c").astype(jnp.int32).reshape(1)

    sharded_small = [conv_w[0], gla_wa2_fwd[0], gla_wa2_bwd[0]]
    small_shapes = [s.shape for s in sharded_small]
    w_in_t, m_in_t, v_in_t = (jnp.swapaxes(a[0], 0, 1) for a in (w_in, m_w_in, v_w_in))
    g_in, g_small = _all_gather([w_in_t.astype(BF16), _pack(sharded_small)], name="gather_w_in")
    g_in, later = lax.optimization_barrier(
        (g_in, [w_out[0].astype(BF16), w_up[0].astype(BF16), w_down[0].astype(BF16)]))
    g_out, w_up8, g_down = _all_gather(later, name="gather_later_weights", collective_id=1)
    w_main_t, w_lr_t = _to_kernel_rows(g_in.reshape(IN_TOTAL, D_MODEL))
    w_out_full = g_out.reshape(D_MODEL, D_MODEL)
    w_down_full = g_down.reshape(D_FF, D_MODEL)
    small_full = [_unpack(g_small[d], small_shapes) for d in range(N_DEV)]
    conv_w_full = jnp.concatenate([s[0] for s in small_full], axis=1)
    wa2_f = jnp.concatenate([s[1] for s in small_full], axis=1)
    wa2_b = jnp.concatenate([s[2] for s in small_full], axis=1)
    cw_g, cw_v = conv_w_full[:, :D_FF], conv_w_full[:, D_FF:]
    cb_g, cb_v = conv_b[:, :D_FF], conv_b[:, D_FF:]
    w2 = jnp.zeros((LR_PAD, 2 * GLA_KEY_WIDTH), F32)
    w2 = w2.at[:GLA_RANK, :GLA_KEY_WIDTH].set(wa2_f).at[GLA_RANK:2 * GLA_RANK, GLA_KEY_WIDTH:].set(wa2_b)
    ba2 = jnp.concatenate([gla_ba_fwd, gla_ba_bwd], axis=1)
    rs = _ReduceScatter(core)
    (loss_part, grad_x, _, _, _, _, _, _, dcb_g, dcb_v, _, d_ba2,
     d_norm1, d_norm2, d_qn, d_kn, d_sink8, d_gla_norm) = _local_step(
        xs, target, norm1_g, w_main_t, w_lr_t, attn_q_norm_g, attn_k_norm_g, attn_sink, w2, ba2, gla_out_norm_g,
        w_out_full, norm2_g, w_up8, cw_g, cw_v, cb_g, cb_v, w_down_full, rs=rs)
    loss = lax.psum(loss_part[0, 0], ("x", "y", "c"))

    (part_down,), (part_up,), (part_out,) = rs.result("w_down"), rs.result("w_up"), rs.result("w_out")
    part_in, part_small = rs.result("w_in")
    m_small = _pack([m_conv_w[0], m_gla_wa2_fwd[0], m_gla_wa2_bwd[0]])
    v_small = _pack([v_conv_w[0], v_gla_wa2_fwd[0], v_gla_wa2_bwd[0]])
    upd_in = _adamw(part_in, w_in_t, m_in_t, v_in_t, tile=(IN_TOTAL // N_DEV, 512), name="adamw_w_in")
    upd_in = [jnp.swapaxes(u, 0, 1) for u in upd_in]
    upd_out = _adamw(part_out, w_out[0], m_w_out[0], v_w_out[0], tile=(256, D_MODEL), name="adamw_w_out")
    upd_up = _adamw(part_up, w_up[0], m_w_up[0], v_w_up[0], tile=(256, UP_BLOCK), name="adamw_w_up")
    upd_down = _adamw(part_down, w_down[0], m_w_down[0], v_w_down[0], tile=(64, D_MODEL), name="adamw_w_down")
    upd_small = _adamw(part_small, _pack(sharded_small), m_small, v_small, tile=part_small.shape[1:],
                       name="adamw_small")
    upd_small = [_unpack(u, small_shapes) for u in upd_small]

    rep_names = ["norm1_g", "attn_q_norm_g", "attn_k_norm_g", "attn_sink", "gla_ba_fwd", "gla_ba_bwd",
                 "gla_out_norm_g", "norm2_g", "conv_b"]
    rep_w = [norm1_g, attn_q_norm_g, attn_k_norm_g, attn_sink, gla_ba_fwd, gla_ba_bwd, gla_out_norm_g, norm2_g, conv_b]
    rep_m = [m_norm1_g, m_attn_q_norm_g, m_attn_k_norm_g, m_attn_sink, m_gla_ba_fwd, m_gla_ba_bwd,
             m_gla_out_norm_g, m_norm2_g, m_conv_b]
    rep_v = [v_norm1_g, v_attn_q_norm_g, v_attn_k_norm_g, v_attn_sink, v_gla_ba_fwd, v_gla_ba_bwd,
             v_gla_out_norm_g, v_norm2_g, v_conv_b]
    d_sink = d_sink8[:, :GQA_GROUP, 0].reshape(1, ATTN_HEADS)
    rep_g = [d_norm1, d_qn, d_kn, d_sink, d_ba2[:, :GLA_KEY_WIDTH], d_ba2[:, GLA_KEY_WIDTH:], d_gla_norm, d_norm2,
             jnp.concatenate([dcb_g, dcb_v], axis=1)]
    rep_shapes = [w.shape for w in rep_w]
    (rep_terms,) = _all_gather([_pack(rep_g)], name="gather_small_grads")
    upd_rep = _adamw(rep_terms, _pack(rep_w), _pack(rep_m), _pack(rep_v), tile=rep_terms.shape[1:],
                     name="adamw_replicated")
    upd_rep = [_unpack(u, rep_shapes) for u in upd_rep]

    order = ["norm1_g", "w_in", "attn_q_norm_g", "attn_k_norm_g", "attn_sink", "gla_wa2_fwd", "gla_ba_fwd",
             "gla_wa2_bwd", "gla_ba_bwd", "gla_out_norm_g", "w_out", "norm2_g", "w_up", "conv_w", "conv_b", "w_down"]
    outs = [loss, grad_x.reshape(1, t, D_MODEL)]
    for kind in range(4):
        by_name = {n: upd_rep[kind][i] for i, n in enumerate(rep_names)}
        by_name["w_in"] = upd_in[kind][None]
        by_name["w_out"] = upd_out[kind][None]
        by_name["w_up"] = upd_up[kind][None]
        by_name["w_down"] = upd_down[kind][None]
        by_name["conv_w"] = upd_small[kind][0][None]
        by_name["gla_wa2_fwd"] = upd_small[kind][1][None]
        by_name["gla_wa2_bwd"] = upd_small[kind][2][None]
        outs += [by_name[n] for n in order]
    return tuple(outs)
```

```python
import functools

import jax
import jax.numpy as jnp
from jax import lax
from jax.experimental import pallas as pl
from jax.experimental.pallas import tpu as pltpu
from jax.experimental.pallas import tpu_sc as plsc

F32 = jnp.float32
BF16 = jnp.bfloat16

D_MODEL = 2048
HEAD_DIM = 128
ATTN_WIDTH = 1024
ATTN_HEADS = 8
KV_HEADS = 2
GQA_GROUP = 4
KV_WIDTH = KV_HEADS * HEAD_DIM
ATTN_BLOCK = 128
WINDOW = 128
ROPE_THETA = 10000.0
GLA_HEADS = 4
GLA_DK = 128
GLA_DV = 256
GLA_KEY_WIDTH = 512
GLA_WIDTH = 1024
GLA_RANK = 16
GLA_GATE_NORMALIZER = 16.0
GLA_CHUNK = 64
D_FF = 5632
NORM_EPS = 1e-6
IN_TOTAL = 4640
IN_MAIN = 4608
LR_PAD = 128
N_DEV = 8
N_CHIP = 4

ADAM_LR = 0.001
ADAM_B1 = 0.9
ADAM_B2 = 0.999
ADAM_EPS = 1e-08
ADAM_WD = 0.01
ADAM_STEP = 10

SEGMENTS = {
    "qa": (0, 0, 1024),
    "gate": (1024, 3584, 1024),
    "vg": (2048, 2560, 1024),
    "qg": (3072, 1536, 512),
    "kg": (3584, 2048, 512),
    "ka": (4096, 1024, 256),
    "va": (4352, 1280, 256),
}

VMEM_LIMIT = 56 * 1024 * 1024
MESH = pl.DeviceIdType.MESH


def _params(semantics=None, vmem=None):
    return pltpu.CompilerParams(dimension_semantics=semantics, vmem_limit_bytes=vmem)


_DIMS = {
    "nn": (((1,), (0,)), ((), ())),
    "nt": (((1,), (1,)), ((), ())),
    "tn": (((0,), (0,)), ((), ())),
}


def _mxu(a, b, mode):
    return lax.dot_general(a.astype(BF16), b.astype(BF16), _DIMS[mode], preferred_element_type=F32)


@functools.partial(jax.custom_vjp, nondiff_argnums=(2,))
def bdot(a, b, mode):
    return _mxu(a, b, mode)


def _bdot_fwd(a, b, mode):
    return _mxu(a, b, mode), (a, b)


def _bdot_bwd(mode, res, g):
    a, b = res
    if mode == "nn":
        return _mxu(g, b, "nt"), _mxu(a, g, "tn")
    if mode == "nt":
        return _mxu(g, b, "nn"), _mxu(g, a, "tn")
    return _mxu(b, g, "nt"), _mxu(a, g, "nn")


bdot.defvjp(_bdot_fwd, _bdot_bwd)


def _rms(x, g):
    return x * lax.rsqrt(jnp.mean(x * x, axis=-1, keepdims=True) + NORM_EPS) * g


def _rope(x, cos, sin_signed):
    return x * cos + pltpu.roll(x, HEAD_DIM // 2, 1) * sin_signed


def _rope_transposed(d, cos, sin_signed):
    return d * cos + pltpu.roll(d * sin_signed, HEAD_DIM // 2, 1)


def _silu(x):
    return x * jax.nn.sigmoid(x)


def _log_sigmoid(z):
    return -(jnp.maximum(-z, 0.0) + jnp.log(1.0 + jnp.exp(-jnp.abs(z))))


def _matmul_call(args, in_specs, o_spec, out_shape, grid, mode, nk, acc_shape, *, name, has_res=False,
                 prefetch=None):
    dims = _DIMS[mode]
    out_dtype = out_shape.dtype
    n_pre = 0 if prefetch is None else 1

    def body(*refs):
        refs = refs[n_pre:]
        if has_res:
            a_ref, b_ref, r_ref, o_ref = refs[:4]
            rest = refs[4:]
        else:
            a_ref, b_ref, o_ref = refs[:3]
            r_ref = None
            rest = refs[3:]
        part = lax.dot_general(a_ref[...], b_ref[...], dims, preferred_element_type=F32)

        def finish(acc):
            if r_ref is not None:
                acc = acc + r_ref[...]
            o_ref[...] = acc.astype(out_dtype)

        if nk == 1:
            finish(part)
        else:
            acc_ref = rest[0]
            kk = pl.program_id(2)

            @pl.when(kk == 0)
            def _():
                acc_ref[...] = part

            @pl.when(kk > 0)
            def _():
                acc_ref[...] += part

            @pl.when(kk == nk - 1)
            def _():
                finish(acc_ref[...])

    scratch = [pltpu.VMEM(acc_shape, F32)] if nk > 1 else []
    params = _params(("parallel", "parallel", "arbitrary"), VMEM_LIMIT)
    if prefetch is None:
        return pl.pallas_call(body, name=name, grid=grid, in_specs=in_specs, out_specs=o_spec, out_shape=out_shape,
                              scratch_shapes=scratch, compiler_params=params)(*args)
    return pl.pallas_call(
        body, name=name,
        grid_spec=pltpu.PrefetchScalarGridSpec(num_scalar_prefetch=1, grid=grid, in_specs=in_specs,
                                               out_specs=o_spec, scratch_shapes=scratch),
        out_shape=out_shape, compiler_params=params)(prefetch, *args)


def _matmul(a, b, mode, *, tm, tn, tk, out_dtype=F32, res=None, name):
    if mode == "nn":
        (m, k), (k2, n) = a.shape, b.shape
    elif mode == "nt":
        (m, k), (n, k2) = a.shape, b.shape
    else:
        (k, m), (k2, n) = a.shape, b.shape
    assert k == k2 and m % tm == 0 and n % tn == 0 and k % tk == 0, (name, a.shape, b.shape, tm, tn, tk)
    if mode == "tn":
        a_spec = pl.BlockSpec((tk, tm), lambda i, j, kk: (kk, i))
    else:
        a_spec = pl.BlockSpec((tm, tk), lambda i, j, kk: (i, kk))
    if mode == "nt":
        b_spec = pl.BlockSpec((tn, tk), lambda i, j, kk: (j, kk))
    else:
        b_spec = pl.BlockSpec((tk, tn), lambda i, j, kk: (kk, j))
    o_spec = pl.BlockSpec((tm, tn), lambda i, j, kk: (i, j))
    in_specs, args = [a_spec, b_spec], [a, b]
    if res is not None:
        in_specs.append(o_spec)
        args.append(res)
    return _matmul_call(args, in_specs, o_spec, jax.ShapeDtypeStruct((m, n), out_dtype),
                        (m // tm, n // tn, k // tk), mode, k // tk, (tm, tn), name=name, has_res=res is not None)


UP_BLOCK = 2 * D_FF // N_DEV


def _up_proj(h2, w_up8, *, tm):
    t = h2.shape[0]
    return _matmul_call(
        [h2, w_up8],
        [pl.BlockSpec((tm, D_MODEL), lambda i, j, kk: (i, 0)),
         pl.BlockSpec((None, D_MODEL, UP_BLOCK), lambda i, j, kk: (j, 0, 0))],
        pl.BlockSpec((None, tm, UP_BLOCK), lambda i, j, kk: (j // N_CHIP, i, j % N_CHIP)),
        jax.ShapeDtypeStruct((2, t, D_FF), F32), (t // tm, N_DEV, 1), "nn", 1, None, name="up_proj")


def _up_proj_dx(du, w_up8, *, tm, tn):
    t = du.shape[1]
    return _matmul_call(
        [du, w_up8],
        [pl.BlockSpec((None, tm, UP_BLOCK), lambda i, j, kk: (kk // N_CHIP, i, kk % N_CHIP)),
         pl.BlockSpec((None, tn, UP_BLOCK), lambda i, j, kk: (kk, j, 0))],
        pl.BlockSpec((tm, tn), lambda i, j, kk: (i, j)),
        jax.ShapeDtypeStruct((t, D_MODEL), F32), (t // tm, D_MODEL // tn, N_DEV), "nt", N_DEV, (tm, tn),
        name="up_proj_dx")


def _up_proj_dw(h2, du, *, tm, tk):
    t = h2.shape[0]
    return _matmul_call(
        [h2, du],
        [pl.BlockSpec((tk, tm), lambda j, i, kk: (kk, i)),
         pl.BlockSpec((None, tk, UP_BLOCK), lambda j, i, kk: (j // N_CHIP, kk, j % N_CHIP))],
        pl.BlockSpec((None, tm, UP_BLOCK), lambda j, i, kk: (j, i, 0)),
        jax.ShapeDtypeStruct((N_DEV, D_MODEL, UP_BLOCK), F32), (N_DEV, D_MODEL // tm, t // tk), "tn", t // tk,
        (tm, UP_BLOCK), name="up_proj_dw")


IN_TILE = 512


def _in_proj_dw(d_proj, h1, *, tn):
    t = h1.shape[0]
    table = []
    for tile in range(IN_MAIN // IN_TILE):
        dst, src, _ = max(s for s in SEGMENTS.values() if s[0] <= tile * IN_TILE)
        assert (src + tile * IN_TILE - dst) % IN_TILE == 0
        table.append((src + tile * IN_TILE - dst) // IN_TILE)
    assert sorted(table) == list(range(IN_MAIN // IN_TILE))
    return _matmul_call(
        [d_proj, h1],
        [pl.BlockSpec((t, IN_TILE), lambda j, i, kk, tab: (0, i)),
         pl.BlockSpec((t, tn), lambda j, i, kk, tab: (0, j))],
        pl.BlockSpec((IN_TILE, tn), lambda j, i, kk, tab: (tab[i], j)),
        jax.ShapeDtypeStruct((IN_TOTAL, D_MODEL), F32), (D_MODEL // tn, IN_MAIN // IN_TILE, 1), "tn", 1, None,
        name="in_proj_dw", prefetch=jnp.asarray(table, jnp.int32))


def _in_proj_dw_lr(dw_t, d_lr, h1):
    t = h1.shape[0]
    n_lr = IN_TOTAL - IN_MAIN
    tn = 512

    def body(dw_ref, dlr_ref, h1_ref, out_ref):
        full = lax.dot_general(dlr_ref[...], h1_ref[...], _DIMS["tn"], preferred_element_type=F32)
        out_ref[...] = full[:n_lr]

    return pl.pallas_call(
        body, name="in_proj_dw_lr", grid=(D_MODEL // tn,),
        in_specs=[pl.BlockSpec(memory_space=pl.ANY),
                  pl.BlockSpec((t, LR_PAD), lambda j: (0, 0)),
                  pl.BlockSpec((t, tn), lambda j: (0, j))],
        out_specs=pl.BlockSpec((n_lr, tn), lambda j: (IN_MAIN // n_lr, j)),
        out_shape=jax.ShapeDtypeStruct(dw_t.shape, F32),
        input_output_aliases={0: 0},
        compiler_params=_params(("parallel",), VMEM_LIMIT),
    )(dw_t, d_lr, h1)


def _rmsnorm_fwd(x, g, *, name, tr=512):
    t, d = x.shape

    def body(x_ref, g_ref, h_ref):
        h_ref[...] = _rms(x_ref[...], g_ref[...]).astype(BF16)

    return pl.pallas_call(
        body, name=name, grid=(t // tr,),
        in_specs=[pl.BlockSpec((tr, d), lambda i: (i, 0)), pl.BlockSpec((1, d), lambda i: (0, 0))],
        out_specs=pl.BlockSpec((tr, d), lambda i: (i, 0)),
        out_shape=jax.ShapeDtypeStruct((t, d), BF16),
        compiler_params=_params(("parallel",), VMEM_LIMIT),
    )(x, g)


def _rmsnorm_bwd(x, g, dh, dres, *, name, tr=256):
    t, d = x.shape

    def body(x_ref, g_ref, dh_ref, dres_ref, dx_ref, dxb_ref, dg_ref):
        _, vjp = jax.vjp(_rms, x_ref[...], g_ref[...])
        dx, dg = vjp(dh_ref[...])
        dx = dx + dres_ref[...]
        dx_ref[...] = dx
        dxb_ref[...] = dx.astype(BF16)

        @pl.when(pl.program_id(0) == 0)
        def _():
            dg_ref[...] = jnp.zeros_like(dg_ref)

        dg_ref[...] += dg

    row = pl.BlockSpec((tr, d), lambda i: (i, 0))
    vec = pl.BlockSpec((1, d), lambda i: (0, 0))
    return pl.pallas_call(
        body, name=name, grid=(t // tr,),
        in_specs=[row, vec, row, row],
        out_specs=[row, row, vec],
        out_shape=[jax.ShapeDtypeStruct((t, d), F32), jax.ShapeDtypeStruct((t, d), BF16),
                   jax.ShapeDtypeStruct((1, d), F32)],
        compiler_params=_params(("arbitrary",), VMEM_LIMIT),
    )(x, g, dh, dres)


def _seg_block(name, width):
    off = SEGMENTS[name][0]
    assert off % width == 0
    return off // width


def _attn_prep_fwd(proj, cos, sin_signed, gq, gk, *, tr=256):
    t = proj.shape[0]

    def body(q_ref, k_ref, v_ref, cos_ref, sin_ref, gq_ref, gk_ref, qo_ref, ko_ref, vo_ref):
        cos_t, sin_t = cos_ref[...], sin_ref[...]
        for h in range(ATTN_HEADS):
            cols = slice(h * HEAD_DIM, (h + 1) * HEAD_DIM)
            qo_ref[:, cols] = _rope(_rms(q_ref[:, cols], gq_ref[...]), cos_t, sin_t).astype(BF16)
        for h in range(KV_HEADS):
            cols = slice(h * HEAD_DIM, (h + 1) * HEAD_DIM)
            ko_ref[:, cols] = _rope(_rms(k_ref[:, cols], gk_ref[...]), cos_t, sin_t).astype(BF16)
        vo_ref[...] = v_ref[...].astype(BF16)

    qb, kb, vb = _seg_block("qa", ATTN_WIDTH), _seg_block("ka", KV_WIDTH), _seg_block("va", KV_WIDTH)
    tab = pl.BlockSpec((tr, HEAD_DIM), lambda i: (i, 0))
    vec = pl.BlockSpec((1, HEAD_DIM), lambda i: (0, 0))
    return pl.pallas_call(
        body, name="attn_prep_fwd", grid=(t // tr,),
        in_specs=[pl.BlockSpec((tr, ATTN_WIDTH), lambda i: (i, qb)),
                  pl.BlockSpec((tr, KV_WIDTH), lambda i: (i, kb)),
                  pl.BlockSpec((tr, KV_WIDTH), lambda i: (i, vb)),
                  tab, tab, vec, vec],
        out_specs=[pl.BlockSpec((tr, ATTN_WIDTH), lambda i: (i, 0)),
                   pl.BlockSpec((tr, KV_WIDTH), lambda i: (i, 0)),
                   pl.BlockSpec((tr, KV_WIDTH), lambda i: (i, 0))],
        out_shape=[jax.ShapeDtypeStruct((t, ATTN_WIDTH), BF16),
                   jax.ShapeDtypeStruct((t, KV_WIDTH), BF16),
                   jax.ShapeDtypeStruct((t, KV_WIDTH), BF16)],
        compiler_params=_params(("parallel",), VMEM_LIMIT),
    )(proj, proj, proj, cos, sin_signed, gq, gk)


def _attn_heads(q, kcat, vcat, sink_col, valid):
    s = bdot(q, kcat, "nt") * (HEAD_DIM ** -0.5)
    s = jnp.where(valid, s, -jnp.inf)
    m = lax.stop_gradient(jnp.maximum(jnp.max(s, axis=-1, keepdims=True), sink_col))
    p = jnp.exp(s - m)
    p = p / (jnp.sum(p, axis=-1, keepdims=True) + jnp.exp(sink_col - m))
    return bdot(p, vcat, "nn")


def _attn_valid(n, t):
    shape = (GQA_GROUP * ATTN_BLOCK, 3 * ATTN_BLOCK)
    qi = lax.broadcasted_iota(jnp.int32, shape, 0) % ATTN_BLOCK
    sj = lax.broadcasted_iota(jnp.int32, shape, 1)
    kpos = n * ATTN_BLOCK - ATTN_BLOCK + sj
    return (jnp.abs(sj - ATTN_BLOCK - qi) <= WINDOW) & (kpos >= 0) & (kpos < t)


def _head_rows(g):
    return slice(g * ATTN_BLOCK, (g + 1) * ATTN_BLOCK)


def _head_cols(g):
    return slice(g * HEAD_DIM, (g + 1) * HEAD_DIM)


def _stack_heads(ref):
    return jnp.concatenate([ref[:, _head_cols(g)] for g in range(GQA_GROUP)], axis=0).astype(F32)


def _sink_column(sink_ref, h):
    return jnp.concatenate([jnp.full((ATTN_BLOCK, 1), sink_ref[h * GQA_GROUP + g], F32)
                            for g in range(GQA_GROUP)], axis=0)


def _attn_specs(nb):
    q_spec = pl.BlockSpec((ATTN_BLOCK, GQA_GROUP * HEAD_DIM), lambda h, n: (n, h))
    kv_specs = [
        pl.BlockSpec((ATTN_BLOCK, HEAD_DIM), lambda h, n: (jnp.maximum(n - 1, 0), h)),
        pl.BlockSpec((ATTN_BLOCK, HEAD_DIM), lambda h, n: (n, h)),
        pl.BlockSpec((ATTN_BLOCK, HEAD_DIM), lambda h, n: (jnp.minimum(n + 1, nb - 1), h)),
    ]
    return q_spec, kv_specs


def _attn_fwd(q, k, v, sink):
    t = q.shape[0]
    nb = t // ATTN_BLOCK

    def body(sink_ref, q_ref, kp_ref, kc_ref, kn_ref, vp_ref, vc_ref, vn_ref, o_ref):
        h, n = pl.program_id(0), pl.program_id(1)
        valid = _attn_valid(n, t)
        kcat = jnp.concatenate([kp_ref[...], kc_ref[...], kn_ref[...]], axis=0).astype(F32)
        vcat = jnp.concatenate([vp_ref[...], vc_ref[...], vn_ref[...]], axis=0).astype(F32)
        o = _attn_heads(_stack_heads(q_ref), kcat, vcat, _sink_column(sink_ref, h), valid).astype(BF16)
        for g in range(GQA_GROUP):
            o_ref[:, _head_cols(g)] = o[_head_rows(g)]

    q_spec, kv_specs = _attn_specs(nb)
    return pl.pallas_call(
        body, name="attn_fwd", grid=(KV_HEADS, nb),
        in_specs=[pl.BlockSpec(memory_space=pltpu.SMEM), q_spec] + kv_specs + kv_specs,
        out_specs=q_spec,
        out_shape=jax.ShapeDtypeStruct((t, ATTN_WIDTH), BF16),
        compiler_params=_params(("parallel", "parallel"), VMEM_LIMIT),
    )(sink, q, k, k, k, v, v, v)


def _attn_bwd(q, k, v, sink, dmix):
    t = q.shape[0]
    nb = t // ATTN_BLOCK

    def body(sink_ref, q_ref, kp_ref, kc_ref, kn_ref, vp_ref, vc_ref, vn_ref, do_ref,
             dq_ref, dk_lo, dk_mid, dk_hi, dv_lo, dv_mid, dv_hi, dsink_ref):
        h, n = pl.program_id(0), pl.program_id(1)
        valid = _attn_valid(n, t)
        kcat = jnp.concatenate([kp_ref[...], kc_ref[...], kn_ref[...]], axis=0).astype(F32)
        vcat = jnp.concatenate([vp_ref[...], vc_ref[...], vn_ref[...]], axis=0).astype(F32)
        _, vjp = jax.vjp(functools.partial(_attn_heads, valid=valid),
                         _stack_heads(q_ref), kcat, vcat, _sink_column(sink_ref, h))
        dq, dk, dv, dsink_col = vjp(_stack_heads(do_ref))
        row = lax.broadcasted_iota(jnp.int32, (8, HEAD_DIM), 0)
        dsink = jnp.zeros((8, HEAD_DIM), F32)
        for g in range(GQA_GROUP):
            dq_ref[:, _head_cols(g)] = dq[_head_rows(g)]
            dsink = dsink + jnp.where(row == g, jnp.sum(dsink_col[_head_rows(g)]), 0.0)
        for i, (dk_ref, dv_ref) in enumerate(((dk_lo, dv_lo), (dk_mid, dv_mid), (dk_hi, dv_hi))):
            rows = slice(i * ATTN_BLOCK, (i + 1) * ATTN_BLOCK)
            dk_ref[...] = dk[rows]
            dv_ref[...] = dv[rows]

        @pl.when(n == 0)
        def _():
            dsink_ref[...] = jnp.zeros_like(dsink_ref)

        dsink_ref[...] += dsink

    q_spec, kv_specs = _attn_specs(nb)
    kv_out = pl.BlockSpec((ATTN_BLOCK, HEAD_DIM), lambda h, n: (n, h))
    kv_shape = jax.ShapeDtypeStruct((t, KV_WIDTH), F32)
    return pl.pallas_call(
        body, name="attn_bwd", grid=(KV_HEADS, nb),
        in_specs=[pl.BlockSpec(memory_space=pltpu.SMEM), q_spec] + kv_specs + kv_specs + [q_spec],
        out_specs=[q_spec] + [kv_out] * 6 + [pl.BlockSpec((None, 8, HEAD_DIM), lambda h, n: (h, 0, 0))],
        out_shape=[jax.ShapeDtypeStruct((t, ATTN_WIDTH), F32)] + [kv_shape] * 6
                  + [jax.ShapeDtypeStruct((KV_HEADS, 8, HEAD_DIM), F32)],
        compiler_params=_params(("parallel", "arbitrary"), VMEM_LIMIT),
    )(sink, q, k, k, k, v, v, v, dmix)


def _attn_prep_bwd(proj, cos, sin_signed, gq, gk, dq, dks, dvs):
    t = proj.shape[0]
    tr = ATTN_BLOCK
    nb = t // tr

    def body(q_ref, k_ref, cos_ref, sin_ref, gq_ref, gk_ref, dq_ref,
             dk_lo, dk_mid, dk_hi, dv_lo, dv_mid, dv_hi,
             dqo_ref, dko_ref, dvo_ref, dgq_ref, dgk_ref):
        n = pl.program_id(0)
        cos_t, sin_t = cos_ref[...], sin_ref[...]
        has_next = (n < nb - 1).astype(F32)
        has_prev = (n > 0).astype(F32)
        dk = dk_lo[...] * has_next + dk_mid[...] + dk_hi[...] * has_prev
        dv = dv_lo[...] * has_next + dv_mid[...] + dv_hi[...] * has_prev
        dvo_ref[...] = dv.astype(BF16)
        dgq = jnp.zeros((1, HEAD_DIM), F32)
        dgk = jnp.zeros((1, HEAD_DIM), F32)
        for h in range(ATTN_HEADS):
            cols = slice(h * HEAD_DIM, (h + 1) * HEAD_DIM)
            _, vjp = jax.vjp(_rms, q_ref[:, cols], gq_ref[...])
            dx, dg = vjp(_rope_transposed(dq_ref[:, cols], cos_t, sin_t))
            dqo_ref[:, cols] = dx.astype(BF16)
            dgq = dgq + dg
        for h in range(KV_HEADS):
            cols = slice(h * HEAD_DIM, (h + 1) * HEAD_DIM)
            _, vjp = jax.vjp(_rms, k_ref[:, cols], gk_ref[...])
            dx, dg = vjp(_rope_transposed(dk[:, cols], cos_t, sin_t))
            dko_ref[:, cols] = dx.astype(BF16)
            dgk = dgk + dg

        @pl.when(n == 0)
        def _():
            dgq_ref[...] = jnp.zeros_like(dgq_ref)
            dgk_ref[...] = jnp.zeros_like(dgk_ref)

        dgq_ref[...] += dgq
        dgk_ref[...] += dgk

    qb, kb = _seg_block("qa", ATTN_WIDTH), _seg_block("ka", KV_WIDTH)
    tab = pl.BlockSpec((tr, HEAD_DIM), lambda i: (i, 0))
    vec = pl.BlockSpec((1, HEAD_DIM), lambda i: (0, 0))
    kv = [pl.BlockSpec((tr, KV_WIDTH), lambda i: (jnp.minimum(i + 1, nb - 1), 0)),
          pl.BlockSpec((tr, KV_WIDTH), lambda i: (i, 0)),
          pl.BlockSpec((tr, KV_WIDTH), lambda i: (jnp.maximum(i - 1, 0), 0))]
    wide = pl.BlockSpec((tr, ATTN_WIDTH), lambda i: (i, 0))
    narrow = pl.BlockSpec((tr, KV_WIDTH), lambda i: (i, 0))
    return pl.pallas_call(
        body, name="attn_prep_bwd", grid=(nb,),
        in_specs=[pl.BlockSpec((tr, ATTN_WIDTH), lambda i: (i, qb)),
                  pl.BlockSpec((tr, KV_WIDTH), lambda i: (i, kb)),
                  tab, tab, vec, vec, wide] + kv + kv,
        out_specs=[wide, narrow, narrow, vec, vec],
        out_shape=[jax.ShapeDtypeStruct((t, ATTN_WIDTH), BF16),
                   jax.ShapeDtypeStruct((t, KV_WIDTH), BF16),
                   jax.ShapeDtypeStruct((t, KV_WIDTH), BF16),
                   jax.ShapeDtypeStruct((1, HEAD_DIM), F32),
                   jax.ShapeDtypeStruct((1, HEAD_DIM), F32)],
        compiler_params=_params(("arbitrary",), VMEM_LIMIT),
    )(proj, proj, cos, sin_signed, gq, gk, dq, *dks, *dvs)


def _decay_fn(lr, w2, ba):
    return _log_sigmoid(bdot(lr, w2, "nn") + ba) / GLA_GATE_NORMALIZER


def _gla_prep_fwd(proj_lr, w2, ba2, *, tr=512):
    t = proj_lr.shape[0]
    width = 2 * GLA_KEY_WIDTH

    def body(lr_ref, w2_ref, ba_ref, g_ref):
        g_ref[...] = _decay_fn(lr_ref[...], w2_ref[...], ba_ref[...])

    return pl.pallas_call(
        body, name="gla_prep_fwd", grid=(t // tr,),
        in_specs=[pl.BlockSpec((tr, LR_PAD), lambda i: (i, 0)),
                  pl.BlockSpec((LR_PAD, width), lambda i: (0, 0)),
                  pl.BlockSpec((1, width), lambda i: (0, 0))],
        out_specs=pl.BlockSpec((tr, width), lambda i: (i, 0)),
        out_shape=jax.ShapeDtypeStruct((t, width), F32),
        compiler_params=_params(("parallel",), VMEM_LIMIT),
    )(proj_lr, w2, ba2)


def _gla_prep_bwd(proj_lr, w2, ba2, dg_f, dg_b, *, tr=512):
    t = proj_lr.shape[0]
    width = 2 * GLA_KEY_WIDTH

    def body(lr_ref, w2_ref, ba_ref, dgf_ref, dgb_ref, dlr_ref, dw2_ref, dba_ref):
        _, vjp = jax.vjp(_decay_fn, lr_ref[...], w2_ref[...], ba_ref[...])
        dlr, dw2, dba = vjp(jnp.concatenate([dgf_ref[...], dgb_ref[...]], axis=1))
        dlr_ref[...] = dlr.astype(BF16)

        @pl.when(pl.program_id(0) == 0)
        def _():
            dw2_ref[...] = jnp.zeros_like(dw2_ref)
            dba_ref[...] = jnp.zeros_like(dba_ref)

        dw2_ref[...] += dw2
        dba_ref[...] += dba

    half = pl.BlockSpec((tr, GLA_KEY_WIDTH), lambda i: (i, 0))
    return pl.pallas_call(
        body, name="gla_prep_bwd", grid=(t // tr,),
        in_specs=[pl.BlockSpec((tr, LR_PAD), lambda i: (i, 0)),
                  pl.BlockSpec((LR_PAD, width), lambda i: (0, 0)),
                  pl.BlockSpec((1, width), lambda i: (0, 0)), half, half],
        out_specs=[pl.BlockSpec((tr, LR_PAD), lambda i: (i, 0)),
                   pl.BlockSpec((LR_PAD, width), lambda i: (0, 0)),
                   pl.BlockSpec((1, width), lambda i: (0, 0))],
        out_shape=[jax.ShapeDtypeStruct((t, LR_PAD), BF16),
                   jax.ShapeDtypeStruct((LR_PAD, width), F32),
                   jax.ShapeDtypeStruct((1, width), F32)],
        compiler_params=_params(("arbitrary",), VMEM_LIMIT),
    )(proj_lr, w2, ba2, dg_f, dg_b)


def _gla_k(h):
    return slice(h * GLA_DK, (h + 1) * GLA_DK)


def _gla_v(h):
    return slice(h * GLA_DV, (h + 1) * GLA_DV)


def _running_sum(x, downward):
    n = x.shape[0]
    row = lax.broadcasted_iota(jnp.int32, x.shape, 0)
    step = 1
    while step < n:
        if downward:
            x = x + jnp.where(row >= step, pltpu.roll(x, step, 0), 0.0)
        else:
            x = x + jnp.where(row < n - step, pltpu.roll(x, n - step, 0), 0.0)
        step *= 2
    return x


@functools.partial(jax.custom_vjp, nondiff_argnums=(1,))
def _cumsum_rows(x, downward):
    return _running_sum(x, downward)


def _cumsum_rows_fwd(x, downward):
    return _running_sum(x, downward), None


def _cumsum_rows_bwd(downward, _, ct):
    return (_running_sum(ct, not downward),)


_cumsum_rows.defvjp(_cumsum_rows_fwd, _cumsum_rows_bwd)


def _gla_chunk(q, k, v, g, state, forward):
    c = GLA_CHUNK
    row = lax.broadcasted_iota(jnp.int32, (c, c), 0)
    col = lax.broadcasted_iota(jnp.int32, (c, c), 1)
    rid = lax.broadcasted_iota(jnp.int32, (c, GLA_DK), 0)
    q = q * (GLA_DK ** -0.5)
    if forward:
        see = row >= col
        upto_ref = rid <= c // 2
    else:
        see = row < col
        upto_ref = rid >= c - 1 - c // 2
    b = _cumsum_rows(g, forward)
    b_last = jnp.sum(g, axis=0, keepdims=True)
    b_ref = jnp.sum(jnp.where(upto_ref, g, 0.0), axis=0, keepdims=True)
    a = bdot(q * jnp.exp(b - b_ref), k * jnp.exp(b_ref - b), "nt")
    a = jnp.where(see, a, 0.0)
    o = bdot(a, v, "nn") + bdot(q * jnp.exp(b), state, "nt")
    new_state = state * jnp.exp(b_last) + bdot(v, k * jnp.exp(b_last - b), "tn")
    return o, new_state


def _gla_fwd(proj, g):
    t = proj.shape[0]
    c = GLA_CHUNK
    nchunk = t // c
    qb, kb, vb = _seg_block("qg", GLA_KEY_WIDTH), _seg_block("kg", GLA_KEY_WIDTH), _seg_block("vg", GLA_WIDTH)

    def body(qf, kf, vf, gf, qr, kr, vr, gr, of_ref, ob_ref, sf_ref, sb_ref, state):
        @pl.when(pl.program_id(0) == 0)
        def _():
            state[...] = jnp.zeros_like(state)

        dirs = ((qf, kf, vf, gf, of_ref, sf_ref), (qr, kr, vr, gr, ob_ref, sb_ref))
        args = [(q_ref[:, _gla_k(h)], k_ref[:, _gla_k(h)], v_ref[:, _gla_v(h)], g_ref[:, _gla_k(h)], state[d, h])
                for d, (q_ref, k_ref, v_ref, g_ref, _, _) in enumerate(dirs) for h in range(GLA_HEADS)]
        results = [_gla_chunk(*a, forward=(i < GLA_HEADS)) for i, a in enumerate(args)]
        for i, (a, (o, s_out)) in enumerate(zip(args, results)):
            d, h = divmod(i, GLA_HEADS)
            dirs[d][5][h] = a[4]
            dirs[d][4][:, _gla_v(h)] = o
            state[d, h] = s_out

    specs, outs = [], []
    for d in range(2):
        ci = (lambda i: i) if d == 0 else (lambda i: nchunk - 1 - i)
        specs += [pl.BlockSpec((c, GLA_KEY_WIDTH), lambda i, ci=ci: (ci(i), qb)),
                  pl.BlockSpec((c, GLA_KEY_WIDTH), lambda i, ci=ci: (ci(i), kb)),
                  pl.BlockSpec((c, GLA_WIDTH), lambda i, ci=ci: (ci(i), vb)),
                  pl.BlockSpec((c, GLA_KEY_WIDTH), lambda i, ci=ci, d=d: (ci(i), d))]
        outs.append(pl.BlockSpec((c, GLA_WIDTH), lambda i, ci=ci: (ci(i), 0)))
    for d in range(2):
        ci = (lambda i: i) if d == 0 else (lambda i: nchunk - 1 - i)
        outs.append(pl.BlockSpec((None, GLA_HEADS, GLA_DV, GLA_DK), lambda i, ci=ci: (ci(i), 0, 0, 0)))
    o_shape = jax.ShapeDtypeStruct((t, GLA_WIDTH), F32)
    s_shape = jax.ShapeDtypeStruct((nchunk, GLA_HEADS, GLA_DV, GLA_DK), F32)
    return pl.pallas_call(
        body, name="gla_fwd", grid=(nchunk,),
        in_specs=specs, out_specs=outs,
        out_shape=[o_shape, o_shape, s_shape, s_shape],
        scratch_shapes=[pltpu.VMEM((2, GLA_HEADS, GLA_DV, GLA_DK), F32)],
        compiler_params=_params(("arbitrary",), VMEM_LIMIT),
    )(proj, proj, proj, g, proj, proj, proj, g)


def _gla_bwd(proj, g, s_f, s_b, do):
    t = proj.shape[0]
    c = GLA_CHUNK
    nchunk = t // c
    qb, kb, vb = _seg_block("qg", GLA_KEY_WIDTH), _seg_block("kg", GLA_KEY_WIDTH), _seg_block("vg", GLA_WIDTH)

    def body(*refs):
        ins, outs, dstate = refs[:12], refs[12:20], refs[20]

        @pl.when(pl.program_id(0) == 0)
        def _():
            dstate[...] = jnp.zeros_like(dstate)

        loaded = []
        for d in range(2):
            q_ref, k_ref, v_ref, g_ref, s_ref, do_ref = ins[6 * d:6 * d + 6]
            for h in range(GLA_HEADS):
                loaded.append(((q_ref[:, _gla_k(h)], k_ref[:, _gla_k(h)], v_ref[:, _gla_v(h)], g_ref[:, _gla_k(h)],
                                s_ref[h]), (do_ref[:, _gla_v(h)], dstate[d, h])))
        grads = []
        for i, (primals, cotangents) in enumerate(loaded):
            _, vjp = jax.vjp(functools.partial(_gla_chunk, forward=(i < GLA_HEADS)), *primals)
            grads.append(vjp(cotangents))
        for i, (dq, dk, dv, dg, ds) in enumerate(grads):
            d, h = divmod(i, GLA_HEADS)
            dq_ref, dk_ref, dv_ref, dg_ref = outs[4 * d:4 * d + 4]
            dq_ref[:, _gla_k(h)] = dq
            dk_ref[:, _gla_k(h)] = dk
            dv_ref[:, _gla_v(h)] = dv
            dg_ref[:, _gla_k(h)] = dg
            dstate[d, h] = ds

    specs, outs, shapes = [], [], []
    for d in range(2):
        ci = (lambda i: nchunk - 1 - i) if d == 0 else (lambda i: i)
        specs += [pl.BlockSpec((c, GLA_KEY_WIDTH), lambda i, ci=ci: (ci(i), qb)),
                  pl.BlockSpec((c, GLA_KEY_WIDTH), lambda i, ci=ci: (ci(i), kb)),
                  pl.BlockSpec((c, GLA_WIDTH), lambda i, ci=ci: (ci(i), vb)),
                  pl.BlockSpec((c, GLA_KEY_WIDTH), lambda i, ci=ci, d=d: (ci(i), d)),
                  pl.BlockSpec((None, GLA_HEADS, GLA_DV, GLA_DK), lambda i, ci=ci: (ci(i), 0, 0, 0)),
                  pl.BlockSpec((c, GLA_WIDTH), lambda i, ci=ci: (ci(i), 0))]
        key = pl.BlockSpec((c, GLA_KEY_WIDTH), lambda i, ci=ci: (ci(i), 0))
        val = pl.BlockSpec((c, GLA_WIDTH), lambda i, ci=ci: (ci(i), 0))
        outs += [key, key, val, key]
        shapes += [jax.ShapeDtypeStruct((t, GLA_KEY_WIDTH), F32), jax.ShapeDtypeStruct((t, GLA_KEY_WIDTH), F32),
                   jax.ShapeDtypeStruct((t, GLA_WIDTH), F32), jax.ShapeDtypeStruct((t, GLA_KEY_WIDTH), F32)]
    return pl.pallas_call(
        body, name="gla_bwd", grid=(nchunk,),
        in_specs=specs, out_specs=outs, out_shape=shapes,
        scratch_shapes=[pltpu.VMEM((2, GLA_HEADS, GLA_DV, GLA_DK), F32)],
        compiler_params=_params(("arbitrary",), VMEM_LIMIT),
    )(proj, proj, proj, g, s_f, do, proj, proj, proj, g, s_b, do)


def _gla_out_head(o_f, o_b, gate, gn):
    return _rms(o_f + o_b, gn) * _silu(gate)


def _gla_out_fwd(o_f, o_b, proj, gn, *, tr=256):
    t = o_f.shape[0]
    gb = _seg_block("gate", GLA_WIDTH)

    def body(of_ref, ob_ref, gate_ref, gn_ref, out_ref):
        for h in range(GLA_HEADS):
            vc = slice(h * GLA_DV, (h + 1) * GLA_DV)
            out_ref[:, vc] = _gla_out_head(of_ref[:, vc], ob_ref[:, vc], gate_ref[:, vc], gn_ref[...]).astype(BF16)

    wide = pl.BlockSpec((tr, GLA_WIDTH), lambda i: (i, 0))
    return pl.pallas_call(
        body, name="gla_out_fwd", grid=(t // tr,),
        in_specs=[wide, wide, pl.BlockSpec((tr, GLA_WIDTH), lambda i: (i, gb)),
                  pl.BlockSpec((1, GLA_DV), lambda i: (0, 0))],
        out_specs=wide,
        out_shape=jax.ShapeDtypeStruct((t, GLA_WIDTH), BF16),
        compiler_params=_params(("parallel",), VMEM_LIMIT),
    )(o_f, o_b, proj, gn)


def _gla_out_bwd(o_f, o_b, proj, gn, dmix, *, tr=256):
    t = o_f.shape[0]
    gb = _seg_block("gate", GLA_WIDTH)

    def body(of_ref, ob_ref, gate_ref, gn_ref, dout_ref, do_ref, dgate_ref, dgn_ref):
        dgn = jnp.zeros((1, GLA_DV), F32)
        for h in range(GLA_HEADS):
            vc = slice(h * GLA_DV, (h + 1) * GLA_DV)
            _, vjp = jax.vjp(_gla_out_head, of_ref[:, vc], ob_ref[:, vc], gate_ref[:, vc], gn_ref[...])
            do, _, dgate, dg = vjp(dout_ref[:, vc])
            do_ref[:, vc] = do
            dgate_ref[:, vc] = dgate.astype(BF16)
            dgn = dgn + dg

        @pl.when(pl.program_id(0) == 0)
        def _():
            dgn_ref[...] = jnp.zeros_like(dgn_ref)

        dgn_ref[...] += dgn

    wide = pl.BlockSpec((tr, GLA_WIDTH), lambda i: (i, 0))
    vec = pl.BlockSpec((1, GLA_DV), lambda i: (0, 0))
    return pl.pallas_call(
        body, name="gla_out_bwd", grid=(t // tr,),
        in_specs=[wide, wide, pl.BlockSpec((tr, GLA_WIDTH), lambda i: (i, gb)), vec,
                  pl.BlockSpec((tr, GLA_WIDTH), lambda i: (i, 1))],
        out_specs=[wide, wide, vec],
        out_shape=[jax.ShapeDtypeStruct((t, GLA_WIDTH), F32), jax.ShapeDtypeStruct((t, GLA_WIDTH), BF16),
                   jax.ShapeDtypeStruct((1, GLA_DV), F32)],
        compiler_params=_params(("arbitrary",), VMEM_LIMIT),
    )(o_f, o_b, proj, gn, dmix)


CONV_TR = 512
CONV_TC = 512
HALO = 8
HALO16 = 16


def _conv3(u, w, b):
    n = u.shape[0]
    return pltpu.roll(u, 1, 0) * w[0:1] + u * w[1:2] + pltpu.roll(u, n - 1, 0) * w[2:3] + b


def _conv_ext(main_ref, prev_ref, next_ref, r, nr):
    prev = prev_ref[...].astype(F32)[-HALO:] * (r > 0).astype(F32)
    nxt = next_ref[...].astype(F32)[:HALO] * (r < nr - 1).astype(F32)
    return jnp.concatenate([prev, main_ref[...].astype(F32), nxt], axis=0)


def _conv_specs(t, halo, half=None):
    per = CONV_TR // halo
    last = t // halo - 1
    lead = () if half is None else (None,)
    at = (lambda *ix: ix) if half is None else (lambda *ix: (half,) + ix)
    return [pl.BlockSpec(lead + (CONV_TR, CONV_TC), lambda j, r: at(r, j)),
            pl.BlockSpec(lead + (halo, CONV_TC), lambda j, r: at(jnp.maximum(r * per - 1, 0), j)),
            pl.BlockSpec(lead + (halo, CONV_TC), lambda j, r: at(jnp.minimum((r + 1) * per, last), j))]


def _ffn_mid_fwd(u, cw_g, cw_v, cb_g, cb_v):
    _, t, f = u.shape
    nr = t // CONV_TR

    def body(ug, ugp, ugn, uv, uvp, uvn, wg, wv, bg, bv, a_ref):
        r = pl.program_id(1)
        gate = _conv3(_conv_ext(ug, ugp, ugn, r, nr), wg[...], bg[...])[HALO:HALO + CONV_TR]
        val = _conv3(_conv_ext(uv, uvp, uvn, r, nr), wv[...], bv[...])[HALO:HALO + CONV_TR]
        a_ref[...] = (_silu(gate) * val).astype(BF16)

    w_spec = pl.BlockSpec((3, CONV_TC), lambda j, r: (0, j))
    b_spec = pl.BlockSpec((1, CONV_TC), lambda j, r: (0, j))
    return pl.pallas_call(
        body, name="ffn_mid_fwd", grid=(f // CONV_TC, nr),
        in_specs=_conv_specs(t, HALO, 0) + _conv_specs(t, HALO, 1) + [w_spec, w_spec, b_spec, b_spec],
        out_specs=pl.BlockSpec((CONV_TR, CONV_TC), lambda j, r: (r, j)),
        out_shape=jax.ShapeDtypeStruct((t, f), BF16),
        compiler_params=_params(("parallel", "parallel"), VMEM_LIMIT),
    )(u, u, u, u, u, u, cw_g, cw_v, cb_g, cb_v)


def _ffn_mid_bwd(u, cw_g, cw_v, cb_g, cb_v, da):
    _, t, f = u.shape
    nr = t // CONV_TR
    ext = CONV_TR + 2 * HALO

    def body(ug, ugp, ugn, uv, uvp, uvn, dam, dap, dan, wg, wv, bg, bv,
             du_ref, dwg_ref, dwv_ref, dbg_ref, dbv_ref):
        r = pl.program_id(1)
        xg = _conv_ext(ug, ugp, ugn, r, nr)
        xv = _conv_ext(uv, uvp, uvn, r, nr)
        da_x = _conv_ext(dam, dap, dan, r, nr)
        gate = _conv3(xg, wg[...], bg[...])
        val = _conv3(xv, wv[...], bv[...])
        sig = jax.nn.sigmoid(gate)
        silu = gate * sig
        d_val = da_x * silu
        d_gate = da_x * val * (sig + silu * (1.0 - sig))
        own = slice(HALO, HALO + CONV_TR)
        for half, (x, d, w, dw_ref, db_ref) in enumerate(((xg, d_gate, wg, dwg_ref, dbg_ref),
                                                          (xv, d_val, wv, dwv_ref, dbv_ref))):
            wt = w[...]
            du = pltpu.roll(d, ext - 1, 0) * wt[0:1] + d * wt[1:2] + pltpu.roll(d, 1, 0) * wt[2:3]
            du_ref[half] = du[own].astype(BF16)
            d_own = d[own]
            dw = jnp.concatenate([
                jnp.sum(pltpu.roll(x, 1, 0)[own] * d_own, axis=0, keepdims=True),
                jnp.sum(x[own] * d_own, axis=0, keepdims=True),
                jnp.sum(pltpu.roll(x, ext - 1, 0)[own] * d_own, axis=0, keepdims=True)], axis=0)
            db = jnp.sum(d_own, axis=0, keepdims=True)

            @pl.when(r == 0)
            def _():
                dw_ref[...] = jnp.zeros_like(dw_ref)
                db_ref[...] = jnp.zeros_like(db_ref)

            dw_ref[...] += dw
            db_ref[...] += db

    w_spec = pl.BlockSpec((3, CONV_TC), lambda j, r: (0, j))
    b_spec = pl.BlockSpec((1, CONV_TC), lambda j, r: (0, j))
    return pl.pallas_call(
        body, name="ffn_mid_bwd", grid=(f // CONV_TC, nr),
        in_specs=(_conv_specs(t, HALO, 0) + _conv_specs(t, HALO, 1) + _conv_specs(t, HALO16)
                  + [w_spec, w_spec, b_spec, b_spec]),
        out_specs=[pl.BlockSpec((2, CONV_TR, CONV_TC), lambda j, r: (0, r, j)), w_spec, w_spec, b_spec, b_spec],
        out_shape=[jax.ShapeDtypeStruct((2, t, f), BF16),
                   jax.ShapeDtypeStruct((3, f), F32), jax.ShapeDtypeStruct((3, f), F32),
                   jax.ShapeDtypeStruct((1, f), F32), jax.ShapeDtypeStruct((1, f), F32)],
        compiler_params=_params(("parallel", "arbitrary"), VMEM_LIMIT),
    )(u, u, u, u, u, u, da, da, da, cw_g, cw_v, cb_g, cb_v)


def _loss_head(y, target, *, tr=256):
    t, d = y.shape

    def body(y_ref, t_ref, loss_ref, dy_ref, dyb_ref):
        err = y_ref[...] - t_ref[...]
        dy = err * (1.0 / d)
        dy_ref[...] = dy
        dyb_ref[...] = dy.astype(BF16)
        part = 0.5 * jnp.sum(jnp.sum(err * err, axis=-1, keepdims=True) * (1.0 / d), axis=0, keepdims=True)

        @pl.when(pl.program_id(0) == 0)
        def _():
            loss_ref[...] = jnp.zeros_like(loss_ref)

        loss_ref[...] += jnp.broadcast_to(part, loss_ref.shape)

    row = pl.BlockSpec((tr, d), lambda i: (i, 0))
    return pl.pallas_call(
        body, name="loss_head", grid=(t // tr,),
        in_specs=[row, row],
        out_specs=[pl.BlockSpec((1, 128), lambda i: (0, 0)), row, row],
        out_shape=[jax.ShapeDtypeStruct((1, 128), F32), jax.ShapeDtypeStruct((t, d), F32),
                   jax.ShapeDtypeStruct((t, d), BF16)],
        compiler_params=_params(("arbitrary",), VMEM_LIMIT),
    )(y, target)


ANY = pl.BlockSpec(memory_space=pl.ANY)


def _position():
    return lax.axis_index("x"), lax.axis_index("y"), lax.axis_index("c")


def _other_chips(x, y):
    return [(1 - x, y), (x, 1 - y), (1 - x, 1 - y)]


def _handshake(peers):
    barrier = pltpu.get_barrier_semaphore()
    for peer in peers:
        pl.semaphore_signal(barrier, inc=1, device_id=peer, device_id_type=MESH)
    pl.semaphore_wait(barrier, len(peers))


def _exchange(body, operands, out_shapes, sems, *, name, collective_id):
    n_in, n_out = len(operands), len(out_shapes)

    def run(*refs):
        body(refs[:n_in], refs[n_in:n_in + n_out], *refs[n_in + n_out:])

    if collective_id is None:
        return pl.pallas_call(run, name=name, in_specs=[ANY] * n_in, out_specs=[ANY] * n_out,
                              out_shape=out_shapes, scratch_shapes=sems)(*operands)
    return pl.kernel(run, name=name, out_type=out_shapes,
                     mesh=plsc.ScalarSubcoreMesh(axis_name="sequencer", num_cores=1), scratch_types=sems,
                     compiler_params=pltpu.CompilerParams(collective_id=collective_id))(*operands)


def _all_gather(blocks, *, name, collective_id=None):
    na = len(blocks)

    def body(ins, outs, send_sems, recv_sems, local_sems):
        x, y, c = _position()
        sibling = (x, y, 1 - c)
        chips = _other_chips(x, y)
        if collective_id is not None:
            _handshake([sibling] + [(*chip, c) for chip in chips])

        def index(px, py, pc):
            return 4 * px + 2 * py + pc

        def copy(a, k, block, to, src=None):
            dst = outs[a].at[index(*block)]
            return pltpu.make_async_remote_copy(
                src_ref=dst if src is None else src, dst_ref=dst,
                send_sem=send_sems.at[a, k], recv_sem=recv_sems.at[a, k],
                device_id=to, device_id_type=MESH)

        pending = []
        for a in range(na):
            mine = pltpu.make_async_copy(ins[a], outs[a].at[index(x, y, c)], local_sems.at[a])
            mine.start()
            pending.append(mine)
        first = []
        for a in range(na):
            first.append(copy(a, 0, (x, y, c), sibling, src=ins[a]))
            first += [copy(a, 1 + j, (x, y, c), (*chip, c), src=ins[a]) for j, chip in enumerate(chips)]
        for cp in first:
            cp.start()
        passed = []
        for j, chip in enumerate(chips):
            for a in range(na):
                copy(a, 1 + j, (*chip, c), (x, y, c)).wait_recv()
                fwd = copy(a, 4 + j, (*chip, c), sibling)
                fwd.start()
                passed.append(fwd)
        for a in range(na):
            copy(a, 0, sibling, (x, y, c)).wait_recv()
            for j, chip in enumerate(chips):
                copy(a, 4 + j, (*chip, 1 - c), (x, y, c)).wait_recv()
        for cp in first + passed:
            cp.wait_send()
        for cp in pending:
            cp.wait()

    return _exchange(
        body, blocks, [jax.ShapeDtypeStruct((N_DEV,) + b.shape, b.dtype) for b in blocks],
        [pltpu.SemaphoreType.DMA((na, 7)), pltpu.SemaphoreType.DMA((na, 7)), pltpu.SemaphoreType.DMA((na,))],
        name=name, collective_id=collective_id)


def _grad_exchange(grads, parts, *, name, collective_id):
    ng, npart = len(grads), len(parts)

    def body(ins, outs, core_send, core_recv, chip_send, chip_recv, local_sems):
        x, y, c = _position()
        sibling = (x, y, 1 - c)
        chips = _other_chips(x, y)
        _handshake([sibling] + [(px, py, c) for px, py in chips])
        me = 2 * x + y
        copies = []
        for b in range(npart):
            src, dst = ins[ng + b], outs[ng + b]
            own = pltpu.make_async_copy(src.at[me], dst.at[me], local_sems.at[b])
            own.start()
            copies.append(own)
            for j, (px, py) in enumerate(chips):
                cp = pltpu.make_async_remote_copy(
                    src_ref=src.at[2 * px + py], dst_ref=dst.at[me],
                    send_sem=chip_send.at[b, j], recv_sem=chip_recv.at[b, j],
                    device_id=(px, py, c), device_id_type=MESH)
                cp.start()
                copies.append(cp)
        for a in range(ng):
            for k in range(N_CHIP):
                cp = pltpu.make_async_remote_copy(
                    src_ref=ins[a].at[k, 1 - c], dst_ref=outs[a].at[k],
                    send_sem=core_send.at[a, k], recv_sem=core_recv.at[a, k],
                    device_id=sibling, device_id_type=MESH)
                cp.start()
                copies.append(cp)
        for cp in copies:
            cp.wait()

    shapes = ([jax.ShapeDtypeStruct((N_CHIP,) + g.shape[2:], g.dtype) for g in grads]
              + [jax.ShapeDtypeStruct(p.shape, p.dtype) for p in parts])
    sems = [pltpu.SemaphoreType.DMA((max(ng, 1), N_CHIP)), pltpu.SemaphoreType.DMA((max(ng, 1), N_CHIP)),
            pltpu.SemaphoreType.DMA((max(npart, 1), 3)), pltpu.SemaphoreType.DMA((max(npart, 1), 3)),
            pltpu.SemaphoreType.DMA((max(npart, 1),))]
    out = _exchange(body, list(grads) + list(parts), shapes, sems, name=name, collective_id=collective_id)
    return out[:ng], out[ng:]


def _pair_sum(grad, theirs, core, *, tile, name):
    _, _, r, w = grad.shape
    tr, tw = tile
    assert r % tr == 0 and w % tw == 0

    def body(core_ref, mine_ref, theirs_ref, out_ref):
        out_ref[...] = mine_ref[...] + theirs_ref[...]

    return pl.pallas_call(
        body, name=name,
        grid_spec=pltpu.PrefetchScalarGridSpec(
            num_scalar_prefetch=1, grid=(N_CHIP, r // tr, w // tw),
            in_specs=[pl.BlockSpec((None, None, tr, tw), lambda k, i, j, core_ref: (k, core_ref[0], i, j)),
                      pl.BlockSpec((None, tr, tw), lambda k, i, j, core_ref: (k, i, j))],
            out_specs=pl.BlockSpec((None, tr, tw), lambda k, i, j, core_ref: (k, i, j))),
        out_shape=jax.ShapeDtypeStruct((N_CHIP, r, w), F32),
        compiler_params=_params(("parallel", "parallel", "parallel"), VMEM_LIMIT),
    )(core, grad, theirs)


class _ReduceScatter:
    def __init__(self, core):
        self.core = core
        self.pending = None
        self.results = {}
        self.launches = 0

    def push(self, tag, grads, rows, then):
        pair, prev_tag = [], None
        if self.pending is not None:
            prev_tag, prev, theirs, prev_rows = self.pending
            pair = [_pair_sum(g, s, self.core, tile=tile, name=f"pair_sum_{prev_tag}_{i}")
                    for i, (g, s, tile) in enumerate(zip(prev, theirs, prev_rows))]
        grads, pair, then = lax.optimization_barrier((list(grads), pair, then))
        grads = [g.reshape((N_CHIP, 2) + g.shape[1:]) for g in grads]
        self.launches += 1
        theirs, parts = _grad_exchange(grads, pair, name=f"grad_exchange_{self.launches}",
                                       collective_id=1 + self.launches)
        if prev_tag is not None:
            self.results[prev_tag] = parts
        self.pending = (tag, grads, theirs, rows) if tag is not None else None
        return then

    def result(self, tag):
        return self.results[tag]


def _adamw(parts, w, m, v, *, tile, name):
    n, r, cols = parts.shape
    tr, tw = tile
    assert r % tr == 0 and cols % tw == 0 and w.shape == (r, cols)
    c1 = 1.0 - ADAM_B1 ** ADAM_STEP
    c2 = 1.0 - ADAM_B2 ** ADAM_STEP

    def body(p_ref, w_ref, m_ref, v_ref, g_ref, d_ref, nm_ref, nv_ref):
        g = p_ref[0]
        for k in range(1, n):
            g = g + p_ref[k]
        new_m = ADAM_B1 * m_ref[...] + (1.0 - ADAM_B1) * g
        new_v = ADAM_B2 * v_ref[...] + (1.0 - ADAM_B2) * (g * g)
        m_hat = new_m / c1
        v_hat = new_v / c2
        g_ref[...] = g
        d_ref[...] = -ADAM_LR * (m_hat / (jnp.sqrt(v_hat) + ADAM_EPS) + ADAM_WD * w_ref[...])
        nm_ref[...] = new_m
        nv_ref[...] = new_v

    spec = pl.BlockSpec((tr, tw), lambda i, j: (i, j))
    shape = jax.ShapeDtypeStruct((r, cols), F32)
    return pl.pallas_call(
        body, name=name, grid=(r // tr, cols // tw),
        in_specs=[pl.BlockSpec((n, tr, tw), lambda i, j: (0, i, j)), spec, spec, spec],
        out_specs=[spec] * 4, out_shape=[shape] * 4,
        compiler_params=_params(("parallel", "parallel"), VMEM_LIMIT),
    )(parts, w, m, v)


def _rope_tables(t):
    half = HEAD_DIM // 2
    inv = 1.0 / (ROPE_THETA ** (jnp.arange(half, dtype=F32) / half))
    ang = jnp.arange(t, dtype=jnp.int32).astype(F32)[:, None] * inv[None, :]
    cos, sin = jnp.cos(ang), jnp.sin(ang)
    return jnp.concatenate([cos, cos], axis=1), jnp.concatenate([-sin, sin], axis=1)


def _to_kernel_rows(w_t):
    order = sorted(SEGMENTS.values())
    main = jnp.concatenate([w_t[src:src + width] for _, src, width in order], axis=0)
    lr = jnp.pad(w_t[IN_MAIN:IN_TOTAL], ((0, LR_PAD - (IN_TOTAL - IN_MAIN)), (0, 0)))
    return main, lr


def _pack(pieces):
    flat = []
    for p in pieces:
        p = p.reshape(-1)
        flat.append(jnp.pad(p, (0, (-p.shape[0]) % 128)))
    return jnp.concatenate(flat).reshape(-1, 128)


def _unpack(packed, shapes):
    flat = packed.reshape(-1)
    out, off = [], 0
    for s in shapes:
        size = 1
        for dim in s:
            size *= dim
        out.append(flat[off:off + size].reshape(s))
        off += size + (-size) % 128
    return out


def _local_step(xs, target, norm1_g, w_main_t, w_lr_t, gq, gk, attn_sink, w2, ba2, gla_norm_g, w_out_full, norm2_g,
                w_up8, cw_g, cw_v, cb_g, cb_v, w_down_full, rs=None):
    t = xs.shape[0]
    tm = min(1024, t)
    cos, sin_signed = _rope_tables(t)
    sink = attn_sink.reshape(ATTN_HEADS)

    h1 = _rmsnorm_fwd(xs, norm1_g, name="norm1_fwd")
    proj = _matmul(h1, w_main_t, "nt", tm=tm, tn=512, tk=D_MODEL, name="proj_main")
    proj_lr = _matmul(h1, w_lr_t, "nt", tm=tm, tn=LR_PAD, tk=D_MODEL, name="proj_lr")
    qa, ka, va = _attn_prep_fwd(proj, cos, sin_signed, gq, gk)
    o_attn = _attn_fwd(qa, ka, va, sink)
    g_dec = _gla_prep_fwd(proj_lr, w2, ba2)
    o_f, o_b, s_f, s_b = _gla_fwd(proj, g_dec)
    o_gla = _gla_out_fwd(o_f, o_b, proj, gla_norm_g)
    mix = jnp.concatenate([o_attn, o_gla], axis=1)
    x1 = _matmul(mix, w_out_full, "nn", tm=tm, tn=512, tk=D_MODEL, res=xs, name="out_proj")
    h2 = _rmsnorm_fwd(x1, norm2_g, name="norm2_fwd")
    u = _up_proj(h2, w_up8, tm=tm)
    act = _ffn_mid_fwd(u, cw_g, cw_v, cb_g, cb_v)
    y = _matmul(act, w_down_full, "nn", tm=tm, tn=512, tk=D_FF, res=x1, name="down_proj")
    loss_part, dy, dy_b = _loss_head(y, target)

    d_act = _matmul(dy_b, w_down_full, "nt", tm=tm, tn=512, tk=D_MODEL, out_dtype=BF16, name="d_act")
    dw_down = _matmul(act, dy_b, "tn", tm=512, tn=1024, tk=t, name="dw_down")
    if rs is not None:
        d_act = rs.push("w_down", [dw_down.reshape(N_DEV, D_FF // N_DEV, D_MODEL)], [(64, D_MODEL)], d_act)
    du, dcw_g, dcw_v, dcb_g, dcb_v = _ffn_mid_bwd(u, cw_g, cw_v, cb_g, cb_v, d_act)
    dw_up8 = _up_proj_dw(h2, du, tm=512, tk=t)
    if rs is not None:
        du = rs.push("w_up", [dw_up8], [(256, UP_BLOCK)], du)
    dh2 = _up_proj_dx(du, w_up8, tm=tm, tn=1024)
    dx1, dx1_b, d_norm2 = _rmsnorm_bwd(x1, norm2_g, dh2, dy, name="norm2_bwd")
    dmix = _matmul(dx1_b, w_out_full, "nt", tm=tm, tn=512, tk=D_MODEL, name="d_mix")
    dw_out = _matmul(mix, dx1_b, "tn", tm=1024, tn=512, tk=t, name="dw_out")
    if rs is not None:
        dmix = rs.push("w_out", [dw_out.reshape(N_DEV, D_MODEL // N_DEV, D_MODEL)], [(256, D_MODEL)], dmix)
    do_gla, d_gate, d_gla_norm = _gla_out_bwd(o_f, o_b, proj, gla_norm_g, dmix)
    (dq_f, dk_f, dv_f, dg_f, dq_b, dk_b, dv_b, dg_b) = _gla_bwd(proj, g_dec, s_f, s_b, do_gla)
    d_lr, d_w2, d_ba2 = _gla_prep_bwd(proj_lr, w2, ba2, dg_f, dg_b)
    dqa, dk_lo, dk_mid, dk_hi, dv_lo, dv_mid, dv_hi, d_sink8 = _attn_bwd(qa, ka, va, sink, dmix)
    d_qa, d_ka, d_va, d_qn, d_kn = _attn_prep_bwd(proj, cos, sin_signed, gq, gk, dqa,
                                                  (dk_lo, dk_mid, dk_hi), (dv_lo, dv_mid, dv_hi))
    d_seg = {"qa": d_qa, "gate": d_gate, "vg": (dv_f + dv_b).astype(BF16), "qg": (dq_f + dq_b).astype(BF16),
             "kg": (dk_f + dk_b).astype(BF16), "ka": d_ka, "va": d_va}
    d_proj = jnp.concatenate([d_seg[k] for k in sorted(SEGMENTS, key=lambda k: SEGMENTS[k][0])], axis=1)
    dw_in_t = _in_proj_dw_lr(_in_proj_dw(d_proj, h1, tn=1024), d_lr, h1)
    if rs is not None:
        per_in, per_wa = IN_TOTAL // N_DEV, GLA_KEY_WIDTH // N_DEV
        dconv_w = jnp.concatenate([dcw_g, dcw_v], axis=1)
        d_wa2_f = d_w2[:GLA_RANK, :GLA_KEY_WIDTH]
        d_wa2_b = d_w2[GLA_RANK:2 * GLA_RANK, GLA_KEY_WIDTH:]
        small_grad = jnp.stack([
            _pack([dconv_w[:, d * UP_BLOCK:(d + 1) * UP_BLOCK], d_wa2_f[:, d * per_wa:(d + 1) * per_wa],
                   d_wa2_b[:, d * per_wa:(d + 1) * per_wa]]) for d in range(N_DEV)])
        d_proj, d_lr = rs.push("w_in", [dw_in_t.reshape(N_DEV, per_in, D_MODEL), small_grad],
                               [(per_in, 512), small_grad.shape[1:]], (d_proj, d_lr))
    dh1 = _matmul(d_lr, w_lr_t, "nn", tm=tm, tn=512, tk=LR_PAD, name="dh1_lr")
    dh1 = _matmul(d_proj, w_main_t, "nn", tm=tm, tn=512, tk=IN_MAIN, res=dh1, name="dh1_main")
    if rs is not None:
        dh1 = rs.push(None, [], [], dh1)
    grad_x, _, d_norm1 = _rmsnorm_bwd(xs, norm1_g, dh1, dx1, name="norm1_bwd")
    return (loss_part, grad_x, dw_in_t, dw_out, dw_up8, dw_down, dcw_g, dcw_v, dcb_g, dcb_v,
            d_w2, d_ba2, d_norm1, d_norm2, d_qn, d_kn, d_sink8, d_gla_norm)


def kernel(x, norm1_g, w_in, attn_q_norm_g, attn_k_norm_g, attn_sink, gla_wa2_fwd, gla_ba_fwd, gla_wa2_bwd, gla_ba_bwd, gla_out_norm_g, w_out, norm2_g, w_up, conv_w, conv_b, w_down, loss_target, m_norm1_g, m_w_in, m_attn_q_norm_g, m_attn_k_norm_g, m_attn_sink, m_gla_wa2_fwd, m_gla_ba_fwd, m_gla_wa2_bwd, m_gla_ba_bwd, m_gla_out_norm_g, m_w_out, m_norm2_g, m_w_up, m_conv_w, m_conv_b, m_w_down, v_norm1_g, v_w_in, v_attn_q_norm_g, v_attn_k_norm_g, v_attn_sink, v_gla_wa2_fwd, v_gla_ba_fwd, v_gla_wa2_bwd, v_gla_ba_bwd, v_gla_out_norm_g, v_w_out, v_norm2_g, v_w_up, v_conv_w, v_conv_b, v_w_down):
    t = x.shape[1]
    xs = x.reshape(t, D_MODEL)
    target = loss_target.reshape(t, D_MODEL)
    core = lax.axis_index("c").astype(jnp.int32).reshape(1)

    sharded_small = [conv_w[0], gla_wa2_fwd[0], gla_wa2_bwd[0]]
    small_shapes = [s.shape for s in sharded_small]
    w_in_t, m_in_t, v_in_t = (jnp.swapaxes(a[0], 0, 1) for a in (w_in, m_w_in, v_w_in))
    g_in, g_small = _all_gather([w_in_t.astype(BF16), _pack(sharded_small)], name="gather_w_in")
    g_in, later = lax.optimization_barrier(
        (g_in, [w_out[0].astype(BF16), w_up[0].astype(BF16), w_down[0].astype(BF16)]))
    g_out, w_up8, g_down = _all_gather(later, name="gather_later_weights", collective_id=1)
    w_main_t, w_lr_t = _to_kernel_rows(g_in.reshape(IN_TOTAL, D_MODEL))
    w_out_full = g_out.reshape(D_MODEL, D_MODEL)
    w_down_full = g_down.reshape(D_FF, D_MODEL)
    small_full = [_unpack(g_small[d], small_shapes) for d in range(N_DEV)]
    conv_w_full = jnp.concatenate([s[0] for s in small_full], axis=1)
    wa2_f = jnp.concatenate([s[1] for s in small_full], axis=1)
    wa2_b = jnp.concatenate([s[2] for s in small_full], axis=1)
    cw_g, cw_v = conv_w_full[:, :D_FF], conv_w_full[:, D_FF:]
    cb_g, cb_v = conv_b[:, :D_FF], conv_b[:, D_FF:]
    w2 = jnp.zeros((LR_PAD, 2 * GLA_KEY_WIDTH), F32)
    w2 = w2.at[:GLA_RANK, :GLA_KEY_WIDTH].set(wa2_f).at[GLA_RANK:2 * GLA_RANK, GLA_KEY_WIDTH:].set(wa2_b)
    ba2 = jnp.concatenate([gla_ba_fwd, gla_ba_bwd], axis=1)
    rs = _ReduceScatter(core)
    (loss_part, grad_x, _, _, _, _, _, _, dcb_g, dcb_v, _, d_ba2,
     d_norm1, d_norm2, d_qn, d_kn, d_sink8, d_gla_norm) = _local_step(
        xs, target, norm1_g, w_main_t, w_lr_t, attn_q_norm_g, attn_k_norm_g, attn_sink, w2, ba2, gla_out_norm_g,
        w_out_full, norm2_g, w_up8, cw_g, cw_v, cb_g, cb_v, w_down_full, rs=rs)
    loss = lax.psum(loss_part[0, 0], ("x", "y", "c"))

    (part_down,), (part_up,), (part_out,) = rs.result("w_down"), rs.result("w_up"), rs.result("w_out")
    part_in, part_small = rs.result("w_in")
    m_small = _pack([m_conv_w[0], m_gla_wa2_fwd[0], m_gla_wa2_bwd[0]])
    v_small = _pack([v_conv_w[0], v_gla_wa2_fwd[0], v_gla_wa2_bwd[0]])
    upd_in = _adamw(part_in, w_in_t, m_in_t, v_in_t, tile=(IN_TOTAL // N_DEV, 512), name="adamw_w_in")
    upd_in = [jnp.swapaxes(u, 0, 1) for u in upd_in]
    upd_out = _adamw(part_out, w_out[0], m_w_out[0], v_w_out[0], tile=(256, D_MODEL), name="adamw_w_out")
    upd_up = _adamw(part_up, w_up[0], m_w_up[0], v_w_up[0], tile=(256, UP_BLOCK), name="adamw_w_up")
    upd_down = _adamw(part_down, w_down[0], m_w_down[0], v_w_down[0], tile=(64, D_MODEL), name="adamw_w_down")
    upd_small = _adamw(part_small, _pack(sharded_small), m_small, v_small, tile=part_small.shape[1:],
                       name="adamw_small")
    upd_small = [_unpack(u, small_shapes) for u in upd_small]

    rep_names = ["norm1_g", "attn_q_norm_g", "attn_k_norm_g", "attn_sink", "gla_ba_fwd", "gla_ba_bwd",
                 "gla_out_norm_g", "norm2_g", "conv_b"]
    rep_w = [norm1_g, attn_q_norm_g, attn_k_norm_g, attn_sink, gla_ba_fwd, gla_ba_bwd, gla_out_norm_g, norm2_g, conv_b]
    rep_m = [m_norm1_g, m_attn_q_norm_g, m_attn_k_norm_g, m_attn_sink, m_gla_ba_fwd, m_gla_ba_bwd,
             m_gla_out_norm_g, m_norm2_g, m_conv_b]
    rep_v = [v_norm1_g, v_attn_q_norm_g, v_attn_k_norm_g, v_attn_sink, v_gla_ba_fwd, v_gla_ba_bwd,
             v_gla_out_norm_g, v_norm2_g, v_conv_b]
    d_sink = d_sink8[:, :GQA_GROUP, 0].reshape(1, ATTN_HEADS)
    rep_g = [d_norm1, d_qn, d_kn, d_sink, d_ba2[:, :GLA_KEY_WIDTH], d_ba2[:, GLA_KEY_WIDTH:], d_gla_norm, d_norm2,
             jnp.concatenate([dcb_g, dcb_v], axis=1)]
    rep_shapes = [w.shape for w in rep_w]
    (rep_terms,) = _all_gather([_pack(rep_g)], name="gather_small_grads")
    upd_rep = _adamw(rep_terms, _pack(rep_w), _pack(rep_m), _pack(rep_v), tile=rep_terms.shape[1:],
                     name="adamw_replicated")
    upd_rep = [_unpack(u, rep_shapes) for u in upd_rep]

    order = ["norm1_g", "w_in", "attn_q_norm_g", "attn_k_norm_g", "attn_sink", "gla_wa2_fwd", "gla_ba_fwd",
             "gla_wa2_bwd", "gla_ba_bwd", "gla_out_norm_g", "w_out", "norm2_g", "w_up", "conv_w", "conv_b", "w_down"]
    outs = [loss, grad_x.reshape(1, t, D_MODEL)]
    for kind in range(4):
        by_name = {n: upd_rep[kind][i] for i, n in enumerate(rep_names)}
        by_name["w_in"] = upd_in[kind][None]
        by_name["w_out"] = upd_out[kind][None]
        by_name["w_up"] = upd_up[kind][None]
        by_name["w_down"] = upd_down[kind][None]
        by_name["conv_w"] = upd_small[kind][0][None]
        by_name["gla_wa2_fwd"] = upd_small[kind][1][None]
        by_name["gla_wa2_bwd"] = upd_small[kind][2][None]
        outs += [by_name[n] for n in order]
    return tuple(outs)
```

```python
import functools

import jax
import jax.numpy as jnp
from jax import lax
from jax.experimental import pallas as pl
from jax.experimental.pallas import tpu as pltpu
from jax.experimental.pallas import tpu_sc as plsc

F32 = jnp.float32
BF16 = jnp.bfloat16

D_MODEL = 2048
HEAD_DIM = 128
ATTN_WIDTH = 1024
ATTN_HEADS = 8
KV_HEADS = 2
GQA_GROUP = 4
KV_WIDTH = KV_HEADS * HEAD_DIM
ATTN_BLOCK = 128
WINDOW = 128
ROPE_THETA = 10000.0
GLA_HEADS = 4
GLA_DK = 128
GLA_DV = 256
GLA_KEY_WIDTH = 512
GLA_WIDTH = 1024
GLA_RANK = 16
GLA_GATE_NORMALIZER = 16.0
GLA_CHUNK = 64
D_FF = 5632
NORM_EPS = 1e-6
IN_TOTAL = 4640
IN_MAIN = 4608
LR_PAD = 128
N_DEV = 8
N_CHIP = 4

ADAM_LR = 0.001
ADAM_B1 = 0.9
ADAM_B2 = 0.999
ADAM_EPS = 1e-08
ADAM_WD = 0.01
ADAM_STEP = 10

SEGMENTS = {
    "qa": (0, 0, 1024),
    "gate": (1024, 3584, 1024),
    "vg": (2048, 2560, 1024),
    "qg": (3072, 1536, 512),
    "kg": (3584, 2048, 512),
    "ka": (4096, 1024, 256),
    "va": (4352, 1280, 256),
}

VMEM_LIMIT = 56 * 1024 * 1024
MESH = pl.DeviceIdType.MESH


def _params(semantics=None, vmem=None):
    return pltpu.CompilerParams(dimension_semantics=semantics, vmem_limit_bytes=vmem)


_DIMS = {
    "nn": (((1,), (0,)), ((), ())),
    "nt": (((1,), (1,)), ((), ())),
    "tn": (((0,), (0,)), ((), ())),
}


def _mxu(a, b, mode):
    return lax.dot_general(a.astype(BF16), b.astype(BF16), _DIMS[mode], preferred_element_type=F32)


@functools.partial(jax.custom_vjp, nondiff_argnums=(2,))
def bdot(a, b, mode):
    return _mxu(a, b, mode)


def _bdot_fwd(a, b, mode):
    return _mxu(a, b, mode), (a, b)


def _bdot_bwd(mode, res, g):
    a, b = res
    if mode == "nn":
        return _mxu(g, b, "nt"), _mxu(a, g, "tn")
    if mode == "nt":
        return _mxu(g, b, "nn"), _mxu(g, a, "tn")
    return _mxu(b, g, "nt"), _mxu(a, g, "nn")


bdot.defvjp(_bdot_fwd, _bdot_bwd)


def _rms(x, g):
    return x * lax.rsqrt(jnp.mean(x * x, axis=-1, keepdims=True) + NORM_EPS) * g


def _rope(x, cos, sin_signed):
    return x * cos + pltpu.roll(x, HEAD_DIM // 2, 1) * sin_signed


def _rope_transposed(d, cos, sin_signed):
    return d * cos + pltpu.roll(d * sin_signed, HEAD_DIM // 2, 1)


def _silu(x):
    return x * jax.nn.sigmoid(x)


def _log_sigmoid(z):
    return -(jnp.maximum(-z, 0.0) + jnp.log(1.0 + jnp.exp(-jnp.abs(z))))


def _matmul_call(args, in_specs, o_spec, out_shape, grid, mode, nk, acc_shape, *, name, has_res=False,
                 prefetch=None, load_b=lambda ref: ref[...]):
    dims = _DIMS[mode]
    out_dtype = out_shape.dtype
    n_pre = 0 if prefetch is None else 1

    def body(*refs):
        refs = refs[n_pre:]
        if has_res:
            a_ref, b_ref, r_ref, o_ref = refs[:4]
            rest = refs[4:]
        else:
            a_ref, b_ref, o_ref = refs[:3]
            r_ref = None
            rest = refs[3:]
        part = lax.dot_general(a_ref[...], load_b(b_ref), dims, preferred_element_type=F32)

        def finish(acc):
            if r_ref is not None:
                acc = acc + r_ref[...]
            o_ref[...] = acc.astype(out_dtype)

        if nk == 1:
            finish(part)
        else:
            acc_ref = rest[0]
            kk = pl.program_id(2)

            @pl.when(kk == 0)
            def _():
                acc_ref[...] = part

            @pl.when(kk > 0)
            def _():
                acc_ref[...] += part

            @pl.when(kk == nk - 1)
            def _():
                finish(acc_ref[...])

    scratch = [pltpu.VMEM(acc_shape, F32)] if nk > 1 else []
    params = _params(("parallel", "parallel", "arbitrary"), VMEM_LIMIT)
    if prefetch is None:
        return pl.pallas_call(body, name=name, grid=grid, in_specs=in_specs, out_specs=o_spec, out_shape=out_shape,
                              scratch_shapes=scratch, compiler_params=params)(*args)
    return pl.pallas_call(
        body, name=name,
        grid_spec=pltpu.PrefetchScalarGridSpec(num_scalar_prefetch=1, grid=grid, in_specs=in_specs,
                                               out_specs=o_spec, scratch_shapes=scratch),
        out_shape=out_shape, compiler_params=params)(prefetch, *args)


def _matmul(a, b, mode, *, tm, tn, tk, out_dtype=F32, res=None, name, n_out=None):
    if mode == "nn":
        (m, k), (k2, n) = a.shape, b.shape
    elif mode == "nt":
        (m, k), (n, k2) = a.shape, b.shape
    else:
        (k, m), (k2, n) = a.shape, b.shape
    n = n if n_out is None else n_out
    assert k == k2 and m % tm == 0 and n % tn == 0 and k % tk == 0, (name, a.shape, b.shape, tm, tn, tk)
    if mode == "tn":
        a_spec = pl.BlockSpec((tk, tm), lambda i, j, kk: (kk, i))
    else:
        a_spec = pl.BlockSpec((tm, tk), lambda i, j, kk: (i, kk))
    if mode == "nt":
        b_spec = pl.BlockSpec((tn, tk), lambda i, j, kk: (j, kk))
    else:
        b_spec = pl.BlockSpec((tk, tn), lambda i, j, kk: (kk, j))
    o_spec = pl.BlockSpec((tm, tn), lambda i, j, kk: (i, j))
    in_specs, args = [a_spec, b_spec], [a, b]
    if res is not None:
        in_specs.append(o_spec)
        args.append(res)
    return _matmul_call(args, in_specs, o_spec, jax.ShapeDtypeStruct((m, n), out_dtype),
                        (m // tm, n // tn, k // tk), mode, k // tk, (tm, tn), name=name, has_res=res is not None)


UP_BLOCK = 2 * D_FF // N_DEV


def _up_proj(h2, w_up8, *, tm):
    t = h2.shape[0]
    return _matmul_call(
        [h2, w_up8],
        [pl.BlockSpec((tm, D_MODEL), lambda i, j, kk: (i, 0)),
         pl.BlockSpec((None, D_MODEL, UP_BLOCK), lambda i, j, kk: (j, 0, 0))],
        pl.BlockSpec((None, tm, UP_BLOCK), lambda i, j, kk: (j // N_CHIP, i, j % N_CHIP)),
        jax.ShapeDtypeStruct((2, t, D_FF), F32), (t // tm, N_DEV, 1), "nn", 1, None, name="up_proj")


def _up_proj_dx(du, w_up8, *, tm, tn):
    t = du.shape[1]
    pair = 2
    return _matmul_call(
        [du, w_up8],
        [pl.BlockSpec((None, tm, pair * UP_BLOCK), lambda i, j, kk: (kk // 2, i, kk % 2)),
         pl.BlockSpec((pair, tn, UP_BLOCK), lambda i, j, kk: (kk, j, 0))],
        pl.BlockSpec((tm, tn), lambda i, j, kk: (i, j)),
        jax.ShapeDtypeStruct((t, D_MODEL), F32), (t // tm, D_MODEL // tn, N_DEV // pair), "nt", N_DEV // pair,
        (tm, tn), name="up_proj_dx", load_b=lambda ref: jnp.concatenate([ref[0], ref[1]], axis=1))


def _up_proj_dw(h2, du, *, tm, tk):
    t = h2.shape[0]
    return _matmul_call(
        [h2, du],
        [pl.BlockSpec((tk, tm), lambda j, i, kk: (kk, i)),
         pl.BlockSpec((None, tk, UP_BLOCK), lambda j, i, kk: (j // N_CHIP, kk, j % N_CHIP))],
        pl.BlockSpec((None, tm, UP_BLOCK), lambda j, i, kk: (j, i, 0)),
        jax.ShapeDtypeStruct((N_DEV, D_MODEL, UP_BLOCK), F32), (N_DEV, D_MODEL // tm, t // tk), "tn", t // tk,
        (tm, UP_BLOCK), name="up_proj_dw")


IN_TILE = 512


def _in_proj_dw(d_proj, h1, *, tn):
    t = h1.shape[0]
    table = []
    for tile in range(IN_MAIN // IN_TILE):
        dst, src, _ = max(s for s in SEGMENTS.values() if s[0] <= tile * IN_TILE)
        assert (src + tile * IN_TILE - dst) % IN_TILE == 0
        table.append((src + tile * IN_TILE - dst) // IN_TILE)
    assert sorted(table) == list(range(IN_MAIN // IN_TILE))
    return _matmul_call(
        [d_proj, h1],
        [pl.BlockSpec((t, IN_TILE), lambda j, i, kk, tab: (0, i)),
         pl.BlockSpec((t, tn), lambda j, i, kk, tab: (0, j))],
        pl.BlockSpec((IN_TILE, tn), lambda j, i, kk, tab: (tab[i], j)),
        jax.ShapeDtypeStruct((IN_TOTAL, D_MODEL), F32), (D_MODEL // tn, IN_MAIN // IN_TILE, 1), "tn", 1, None,
        name="in_proj_dw", prefetch=jnp.asarray(table, jnp.int32))


def _in_proj_dw_lr(dw_t, d_lr, h1):
    t = h1.shape[0]
    n_lr = IN_TOTAL - IN_MAIN
    tn = 512

    def body(dw_ref, dlr_ref, h1_ref, out_ref):
        full = lax.dot_general(dlr_ref[...], h1_ref[...], _DIMS["tn"], preferred_element_type=F32)
        out_ref[...] = full[:n_lr]

    return pl.pallas_call(
        body, name="in_proj_dw_lr", grid=(D_MODEL // tn,),
        in_specs=[pl.BlockSpec(memory_space=pl.ANY),
                  pl.BlockSpec((t, LR_PAD), lambda j: (0, 0)),
                  pl.BlockSpec((t, tn), lambda j: (0, j))],
        out_specs=pl.BlockSpec((n_lr, tn), lambda j: (IN_MAIN // n_lr, j)),
        out_shape=jax.ShapeDtypeStruct(dw_t.shape, F32),
        input_output_aliases={0: 0},
        compiler_params=_params(("parallel",), VMEM_LIMIT),
    )(dw_t, d_lr, h1)


def _rmsnorm_fwd(x, g, *, name, tr=512):
    t, d = x.shape

    def body(x_ref, g_ref, h_ref):
        h_ref[...] = _rms(x_ref[...], g_ref[...]).astype(BF16)

    return pl.pallas_call(
        body, name=name, grid=(t // tr,),
        in_specs=[pl.BlockSpec((tr, d), lambda i: (i, 0)), pl.BlockSpec((1, d), lambda i: (0, 0))],
        out_specs=pl.BlockSpec((tr, d), lambda i: (i, 0)),
        out_shape=jax.ShapeDtypeStruct((t, d), BF16),
        compiler_params=_params(("parallel",), VMEM_LIMIT),
    )(x, g)


def _rmsnorm_bwd(x, g, dh, dres, *, name, tr=256):
    t, d = x.shape

    def body(x_ref, g_ref, dh_ref, dres_ref, dx_ref, dxb_ref, dg_ref):
        _, vjp = jax.vjp(_rms, x_ref[...], g_ref[...])
        dx, dg = vjp(dh_ref[...])
        dx = dx + dres_ref[...]
        dx_ref[...] = dx
        dxb_ref[...] = dx.astype(BF16)

        @pl.when(pl.program_id(0) == 0)
        def _():
            dg_ref[...] = jnp.zeros_like(dg_ref)

        dg_ref[...] += dg

    row = pl.BlockSpec((tr, d), lambda i: (i, 0))
    vec = pl.BlockSpec((1, d), lambda i: (0, 0))
    return pl.pallas_call(
        body, name=name, grid=(t // tr,),
        in_specs=[row, vec, row, row],
        out_specs=[row, row, vec],
        out_shape=[jax.ShapeDtypeStruct((t, d), F32), jax.ShapeDtypeStruct((t, d), BF16),
                   jax.ShapeDtypeStruct((1, d), F32)],
        compiler_params=_params(("arbitrary",), VMEM_LIMIT),
    )(x, g, dh, dres)


def _seg_block(name, width):
    off = SEGMENTS[name][0]
    assert off % width == 0
    return off // width


def _attn_prep_fwd(proj, cos, sin_signed, gq, gk, *, tr=256):
    t = proj.shape[0]

    def body(q_ref, k_ref, v_ref, cos_ref, sin_ref, gq_ref, gk_ref, qo_ref, ko_ref, vo_ref):
        cos_t, sin_t = cos_ref[...], sin_ref[...]
        for h in range(ATTN_HEADS):
            cols = slice(h * HEAD_DIM, (h + 1) * HEAD_DIM)
            qo_ref[:, cols] = _rope(_rms(q_ref[:, cols], gq_ref[...]), cos_t, sin_t).astype(BF16)
        for h in range(KV_HEADS):
            cols = slice(h * HEAD_DIM, (h + 1) * HEAD_DIM)
            ko_ref[:, cols] = _rope(_rms(k_ref[:, cols], gk_ref[...]), cos_t, sin_t).astype(BF16)
        vo_ref[...] = v_ref[...].astype(BF16)

    qb, kb, vb = _seg_block("qa", ATTN_WIDTH), _seg_block("ka", KV_WIDTH), _seg_block("va", KV_WIDTH)
    tab = pl.BlockSpec((tr, HEAD_DIM), lambda i: (i, 0))
    vec = pl.BlockSpec((1, HEAD_DIM), lambda i: (0, 0))
    return pl.pallas_call(
        body, name="attn_prep_fwd", grid=(t // tr,),
        in_specs=[pl.BlockSpec((tr, ATTN_WIDTH), lambda i: (i, qb)),
                  pl.BlockSpec((tr, KV_WIDTH), lambda i: (i, kb)),
                  pl.BlockSpec((tr, KV_WIDTH), lambda i: (i, vb)),
                  tab, tab, vec, vec],
        out_specs=[pl.BlockSpec((tr, ATTN_WIDTH), lambda i: (i, 0)),
                   pl.BlockSpec((tr, KV_WIDTH), lambda i: (i, 0)),
                   pl.BlockSpec((tr, KV_WIDTH), lambda i: (i, 0))],
        out_shape=[jax.ShapeDtypeStruct((t, ATTN_WIDTH), BF16),
                   jax.ShapeDtypeStruct((t, KV_WIDTH), BF16),
                   jax.ShapeDtypeStruct((t, KV_WIDTH), BF16)],
        compiler_params=_params(("parallel",), VMEM_LIMIT),
    )(proj, proj, proj, cos, sin_signed, gq, gk)


def _attn_heads(q, kcat, vcat, sink_col, valid):
    s = bdot(q, kcat, "nt") * (HEAD_DIM ** -0.5)
    s = jnp.where(valid, s, -jnp.inf)
    m = lax.stop_gradient(jnp.maximum(jnp.max(s, axis=-1, keepdims=True), sink_col))
    p = jnp.exp(s - m)
    p = p / (jnp.sum(p, axis=-1, keepdims=True) + jnp.exp(sink_col - m))
    return bdot(p, vcat, "nn")


def _attn_valid(n, t):
    shape = (GQA_GROUP * ATTN_BLOCK, 3 * ATTN_BLOCK)
    qi = lax.broadcasted_iota(jnp.int32, shape, 0) % ATTN_BLOCK
    sj = lax.broadcasted_iota(jnp.int32, shape, 1)
    kpos = n * ATTN_BLOCK - ATTN_BLOCK + sj
    return (jnp.abs(sj - ATTN_BLOCK - qi) <= WINDOW) & (kpos >= 0) & (kpos < t)


def _head_rows(g):
    return slice(g * ATTN_BLOCK, (g + 1) * ATTN_BLOCK)


def _head_cols(g):
    return slice(g * HEAD_DIM, (g + 1) * HEAD_DIM)


def _stack_heads(ref):
    return jnp.concatenate([ref[:, _head_cols(g)] for g in range(GQA_GROUP)], axis=0).astype(F32)


def _sink_column(sink_ref, h):
    return jnp.concatenate([jnp.full((ATTN_BLOCK, 1), sink_ref[h * GQA_GROUP + g], F32)
                            for g in range(GQA_GROUP)], axis=0)


def _attn_specs(nb):
    q_spec = pl.BlockSpec((ATTN_BLOCK, GQA_GROUP * HEAD_DIM), lambda h, n: (n, h))
    kv_specs = [
        pl.BlockSpec((ATTN_BLOCK, HEAD_DIM), lambda h, n: (jnp.maximum(n - 1, 0), h)),
        pl.BlockSpec((ATTN_BLOCK, HEAD_DIM), lambda h, n: (n, h)),
        pl.BlockSpec((ATTN_BLOCK, HEAD_DIM), lambda h, n: (jnp.minimum(n + 1, nb - 1), h)),
    ]
    return q_spec, kv_specs


def _attn_fwd(q, k, v, sink):
    t = q.shape[0]
    nb = t // ATTN_BLOCK

    def body(sink_ref, q_ref, kp_ref, kc_ref, kn_ref, vp_ref, vc_ref, vn_ref, o_ref):
        h, n = pl.program_id(0), pl.program_id(1)
        valid = _attn_valid(n, t)
        kcat = jnp.concatenate([kp_ref[...], kc_ref[...], kn_ref[...]], axis=0).astype(F32)
        vcat = jnp.concatenate([vp_ref[...], vc_ref[...], vn_ref[...]], axis=0).astype(F32)
        o = _attn_heads(_stack_heads(q_ref), kcat, vcat, _sink_column(sink_ref, h), valid).astype(BF16)
        for g in range(GQA_GROUP):
            o_ref[:, _head_cols(g)] = o[_head_rows(g)]

    q_spec, kv_specs = _attn_specs(nb)
    return pl.pallas_call(
        body, name="attn_fwd", grid=(KV_HEADS, nb),
        in_specs=[pl.BlockSpec(memory_space=pltpu.SMEM), q_spec] + kv_specs + kv_specs,
        out_specs=q_spec,
        out_shape=jax.ShapeDtypeStruct((t, ATTN_WIDTH), BF16),
        compiler_params=_params(("parallel", "parallel"), VMEM_LIMIT),
    )(sink, q, k, k, k, v, v, v)


def _attn_bwd(q, k, v, sink, dmix):
    t = q.shape[0]
    nb = t // ATTN_BLOCK

    def body(sink_ref, q_ref, kp_ref, kc_ref, kn_ref, vp_ref, vc_ref, vn_ref, do_ref,
             dq_ref, dk_lo, dk_mid, dk_hi, dv_lo, dv_mid, dv_hi, dsink_ref):
        h, n = pl.program_id(0), pl.program_id(1)
        valid = _attn_valid(n, t)
        kcat = jnp.concatenate([kp_ref[...], kc_ref[...], kn_ref[...]], axis=0).astype(F32)
        vcat = jnp.concatenate([vp_ref[...], vc_ref[...], vn_ref[...]], axis=0).astype(F32)
        _, vjp = jax.vjp(functools.partial(_attn_heads, valid=valid),
                         _stack_heads(q_ref), kcat, vcat, _sink_column(sink_ref, h))
        dq, dk, dv, dsink_col = vjp(_stack_heads(do_ref))
        row = lax.broadcasted_iota(jnp.int32, (8, HEAD_DIM), 0)
        dsink = jnp.zeros((8, HEAD_DIM), F32)
        for g in range(GQA_GROUP):
            dq_ref[:, _head_cols(g)] = dq[_head_rows(g)]
            dsink = dsink + jnp.where(row == g, jnp.sum(dsink_col[_head_rows(g)]), 0.0)
        for i, (dk_ref, dv_ref) in enumerate(((dk_lo, dv_lo), (dk_mid, dv_mid), (dk_hi, dv_hi))):
            rows = slice(i * ATTN_BLOCK, (i + 1) * ATTN_BLOCK)
            dk_ref[...] = dk[rows]
            dv_ref[...] = dv[rows]

        @pl.when(n == 0)
        def _():
            dsink_ref[...] = jnp.zeros_like(dsink_ref)

        dsink_ref[...] += dsink

    q_spec, kv_specs = _attn_specs(nb)
    kv_out = pl.BlockSpec((ATTN_BLOCK, HEAD_DIM), lambda h, n: (n, h))
    kv_shape = jax.ShapeDtypeStruct((t, KV_WIDTH), F32)
    return pl.pallas_call(
        body, name="attn_bwd", grid=(KV_HEADS, nb),
        in_specs=[pl.BlockSpec(memory_space=pltpu.SMEM), q_spec] + kv_specs + kv_specs + [q_spec],
        out_specs=[q_spec] + [kv_out] * 6 + [pl.BlockSpec((None, 8, HEAD_DIM), lambda h, n: (h, 0, 0))],
        out_shape=[jax.ShapeDtypeStruct((t, ATTN_WIDTH), F32)] + [kv_shape] * 6
                  + [jax.ShapeDtypeStruct((KV_HEADS, 8, HEAD_DIM), F32)],
        compiler_params=_params(("parallel", "arbitrary"), VMEM_LIMIT),
    )(sink, q, k, k, k, v, v, v, dmix)


def _attn_prep_bwd(proj, cos, sin_signed, gq, gk, dq, dks, dvs):
    t = proj.shape[0]
    tr = ATTN_BLOCK
    nb = t // tr

    def body(q_ref, k_ref, cos_ref, sin_ref, gq_ref, gk_ref, dq_ref,
             dk_lo, dk_mid, dk_hi, dv_lo, dv_mid, dv_hi,
             dqo_ref, dko_ref, dvo_ref, dgq_ref, dgk_ref):
        n = pl.program_id(0)
        cos_t, sin_t = cos_ref[...], sin_ref[...]
        has_next = (n < nb - 1).astype(F32)
        has_prev = (n > 0).astype(F32)
        dk = dk_lo[...] * has_next + dk_mid[...] + dk_hi[...] * has_prev
        dv = dv_lo[...] * has_next + dv_mid[...] + dv_hi[...] * has_prev
        dvo_ref[...] = dv.astype(BF16)
        dgq = jnp.zeros((1, HEAD_DIM), F32)
        dgk = jnp.zeros((1, HEAD_DIM), F32)
        for h in range(ATTN_HEADS):
            cols = slice(h * HEAD_DIM, (h + 1) * HEAD_DIM)
            _, vjp = jax.vjp(_rms, q_ref[:, cols], gq_ref[...])
            dx, dg = vjp(_rope_transposed(dq_ref[:, cols], cos_t, sin_t))
            dqo_ref[:, cols] = dx.astype(BF16)
            dgq = dgq + dg
        for h in range(KV_HEADS):
            cols = slice(h * HEAD_DIM, (h + 1) * HEAD_DIM)
            _, vjp = jax.vjp(_rms, k_ref[:, cols], gk_ref[...])
            dx, dg = vjp(_rope_transposed(dk[:, cols], cos_t, sin_t))
            dko_ref[:, cols] = dx.astype(BF16)
            dgk = dgk + dg

        @pl.when(n == 0)
        def _():
            dgq_ref[...] = jnp.zeros_like(dgq_ref)
            dgk_ref[...] = jnp.zeros_like(dgk_ref)

        dgq_ref[...] += dgq
        dgk_ref[...] += dgk

    qb, kb = _seg_block("qa", ATTN_WIDTH), _seg_block("ka", KV_WIDTH)
    tab = pl.BlockSpec((tr, HEAD_DIM), lambda i: (i, 0))
    vec = pl.BlockSpec((1, HEAD_DIM), lambda i: (0, 0))
    kv = [pl.BlockSpec((tr, KV_WIDTH), lambda i: (jnp.minimum(i + 1, nb - 1), 0)),
          pl.BlockSpec((tr, KV_WIDTH), lambda i: (i, 0)),
          pl.BlockSpec((tr, KV_WIDTH), lambda i: (jnp.maximum(i - 1, 0), 0))]
    wide = pl.BlockSpec((tr, ATTN_WIDTH), lambda i: (i, 0))
    narrow = pl.BlockSpec((tr, KV_WIDTH), lambda i: (i, 0))
    return pl.pallas_call(
        body, name="attn_prep_bwd", grid=(nb,),
        in_specs=[pl.BlockSpec((tr, ATTN_WIDTH), lambda i: (i, qb)),
                  pl.BlockSpec((tr, KV_WIDTH), lambda i: (i, kb)),
                  tab, tab, vec, vec, wide] + kv + kv,
        out_specs=[wide, narrow, narrow, vec, vec],
        out_shape=[jax.ShapeDtypeStruct((t, ATTN_WIDTH), BF16),
                   jax.ShapeDtypeStruct((t, KV_WIDTH), BF16),
                   jax.ShapeDtypeStruct((t, KV_WIDTH), BF16),
                   jax.ShapeDtypeStruct((1, HEAD_DIM), F32),
                   jax.ShapeDtypeStruct((1, HEAD_DIM), F32)],
        compiler_params=_params(("arbitrary",), VMEM_LIMIT),
    )(proj, proj, cos, sin_signed, gq, gk, dq, *dks, *dvs)


def _decay_fn(lr, w2, ba):
    return _log_sigmoid(bdot(lr, w2, "nn") + ba) / GLA_GATE_NORMALIZER


def _gla_prep_fwd(proj_lr, w2, ba2, *, tr=512):
    t = proj_lr.shape[0]
    width = 2 * GLA_KEY_WIDTH

    def body(lr_ref, w2_ref, ba_ref, g_ref):
        g_ref[...] = _decay_fn(lr_ref[...], w2_ref[...], ba_ref[...])

    return pl.pallas_call(
        body, name="gla_prep_fwd", grid=(t // tr,),
        in_specs=[pl.BlockSpec((tr, LR_PAD), lambda i: (i, 0)),
                  pl.BlockSpec((LR_PAD, width), lambda i: (0, 0)),
                  pl.BlockSpec((1, width), lambda i: (0, 0))],
        out_specs=pl.BlockSpec((tr, width), lambda i: (i, 0)),
        out_shape=jax.ShapeDtypeStruct((t, width), F32),
        compiler_params=_params(("parallel",), VMEM_LIMIT),
    )(proj_lr, w2, ba2)


def _gla_prep_bwd(proj_lr, w2, ba2, dg_f, dg_b, *, tr=512):
    t = proj_lr.shape[0]
    width = 2 * GLA_KEY_WIDTH

    def body(lr_ref, w2_ref, ba_ref, dgf_ref, dgb_ref, dlr_ref, dw2_ref, dba_ref):
        _, vjp = jax.vjp(_decay_fn, lr_ref[...], w2_ref[...], ba_ref[...])
        dlr, dw2, dba = vjp(jnp.concatenate([dgf_ref[...], dgb_ref[...]], axis=1))
        dlr_ref[...] = dlr.astype(BF16)

        @pl.when(pl.program_id(0) == 0)
        def _():
            dw2_ref[...] = jnp.zeros_like(dw2_ref)
            dba_ref[...] = jnp.zeros_like(dba_ref)

        dw2_ref[...] += dw2
        dba_ref[...] += dba

    half = pl.BlockSpec((tr, GLA_KEY_WIDTH), lambda i: (i, 0))
    return pl.pallas_call(
        body, name="gla_prep_bwd", grid=(t // tr,),
        in_specs=[pl.BlockSpec((tr, LR_PAD), lambda i: (i, 0)),
                  pl.BlockSpec((LR_PAD, width), lambda i: (0, 0)),
                  pl.BlockSpec((1, width), lambda i: (0, 0)), half, half],
        out_specs=[pl.BlockSpec((tr, LR_PAD), lambda i: (i, 0)),
                   pl.BlockSpec((LR_PAD, width), lambda i: (0, 0)),
                   pl.BlockSpec((1, width), lambda i: (0, 0))],
        out_shape=[jax.ShapeDtypeStruct((t, LR_PAD), BF16),
                   jax.ShapeDtypeStruct((LR_PAD, width), F32),
                   jax.ShapeDtypeStruct((1, width), F32)],
        compiler_params=_params(("arbitrary",), VMEM_LIMIT),
    )(proj_lr, w2, ba2, dg_f, dg_b)


def _gla_k(h):
    return slice(h * GLA_DK, (h + 1) * GLA_DK)


def _gla_v(h):
    return slice(h * GLA_DV, (h + 1) * GLA_DV)


def _running_sum(x, downward):
    n = x.shape[0]
    row = lax.broadcasted_iota(jnp.int32, x.shape, 0)
    step = 1
    while step < n:
        if downward:
            x = x + jnp.where(row >= step, pltpu.roll(x, step, 0), 0.0)
        else:
            x = x + jnp.where(row < n - step, pltpu.roll(x, n - step, 0), 0.0)
        step *= 2
    return x


@functools.partial(jax.custom_vjp, nondiff_argnums=(1,))
def _cumsum_rows(x, downward):
    return _running_sum(x, downward)


def _cumsum_rows_fwd(x, downward):
    return _running_sum(x, downward), None


def _cumsum_rows_bwd(downward, _, ct):
    return (_running_sum(ct, not downward),)


_cumsum_rows.defvjp(_cumsum_rows_fwd, _cumsum_rows_bwd)


def _gla_chunk(q, k, v, g, state, forward):
    c = GLA_CHUNK
    row = lax.broadcasted_iota(jnp.int32, (c, c), 0)
    col = lax.broadcasted_iota(jnp.int32, (c, c), 1)
    rid = lax.broadcasted_iota(jnp.int32, (c, GLA_DK), 0)
    q = q * (GLA_DK ** -0.5)
    if forward:
        see = row >= col
        upto_ref = rid <= c // 2
    else:
        see = row < col
        upto_ref = rid >= c - 1 - c // 2
    b = _cumsum_rows(g, forward)
    b_last = jnp.sum(g, axis=0, keepdims=True)
    b_ref = jnp.sum(jnp.where(upto_ref, g, 0.0), axis=0, keepdims=True)
    a = bdot(q * jnp.exp(b - b_ref), k * jnp.exp(b_ref - b), "nt")
    a = jnp.where(see, a, 0.0)
    o = bdot(a, v, "nn") + bdot(q * jnp.exp(b), state, "nt")
    new_state = state * jnp.exp(b_last) + bdot(v, k * jnp.exp(b_last - b), "tn")
    return o, new_state


def _gla_fwd(proj, g):
    t = proj.shape[0]
    c = GLA_CHUNK
    nchunk = t // c
    qb, kb, vb = _seg_block("qg", GLA_KEY_WIDTH), _seg_block("kg", GLA_KEY_WIDTH), _seg_block("vg", GLA_WIDTH)

    def body(qf, kf, vf, gf, qr, kr, vr, gr, of_ref, ob_ref, sf_ref, sb_ref, state):
        @pl.when(pl.program_id(0) == 0)
        def _():
            state[...] = jnp.zeros_like(state)

        dirs = ((qf, kf, vf, gf, of_ref, sf_ref), (qr, kr, vr, gr, ob_ref, sb_ref))
        args = [(q_ref[:, _gla_k(h)], k_ref[:, _gla_k(h)], v_ref[:, _gla_v(h)], g_ref[:, _gla_k(h)], state[d, h])
                for d, (q_ref, k_ref, v_ref, g_ref, _, _) in enumerate(dirs) for h in range(GLA_HEADS)]
        results = [_gla_chunk(*a, forward=(i < GLA_HEADS)) for i, a in enumerate(args)]
        for i, (a, (o, s_out)) in enumerate(zip(args, results)):
            d, h = divmod(i, GLA_HEADS)
            dirs[d][5][h] = a[4]
            dirs[d][4][:, _gla_v(h)] = o
            state[d, h] = s_out

    specs, outs = [], []
    for d in range(2):
        ci = (lambda i: i) if d == 0 else (lambda i: nchunk - 1 - i)
        specs += [pl.BlockSpec((c, GLA_KEY_WIDTH), lambda i, ci=ci: (ci(i), qb)),
                  pl.BlockSpec((c, GLA_KEY_WIDTH), lambda i, ci=ci: (ci(i), kb)),
                  pl.BlockSpec((c, GLA_WIDTH), lambda i, ci=ci: (ci(i), vb)),
                  pl.BlockSpec((c, GLA_KEY_WIDTH), lambda i, ci=ci, d=d: (ci(i), d))]
        outs.append(pl.BlockSpec((c, GLA_WIDTH), lambda i, ci=ci: (ci(i), 0)))
    for d in range(2):
        ci = (lambda i: i) if d == 0 else (lambda i: nchunk - 1 - i)
        outs.append(pl.BlockSpec((None, GLA_HEADS, GLA_DV, GLA_DK), lambda i, ci=ci: (ci(i), 0, 0, 0)))
    o_shape = jax.ShapeDtypeStruct((t, GLA_WIDTH), F32)
    s_shape = jax.ShapeDtypeStruct((nchunk, GLA_HEADS, GLA_DV, GLA_DK), F32)
    return pl.pallas_call(
        body, name="gla_fwd", grid=(nchunk,),
        in_specs=specs, out_specs=outs,
        out_shape=[o_shape, o_shape, s_shape, s_shape],
        scratch_shapes=[pltpu.VMEM((2, GLA_HEADS, GLA_DV, GLA_DK), F32)],
        compiler_params=_params(("arbitrary",), VMEM_LIMIT),
    )(proj, proj, proj, g, proj, proj, proj, g)


def _gla_bwd(proj, g, s_f, s_b, do):
    t = proj.shape[0]
    c = GLA_CHUNK
    nchunk = t // c
    qb, kb, vb = _seg_block("qg", GLA_KEY_WIDTH), _seg_block("kg", GLA_KEY_WIDTH), _seg_block("vg", GLA_WIDTH)

    def body(*refs):
        ins, outs, dstate = refs[:12], refs[12:20], refs[20]

        @pl.when(pl.program_id(0) == 0)
        def _():
            dstate[...] = jnp.zeros_like(dstate)

        loaded = []
        for d in range(2):
            q_ref, k_ref, v_ref, g_ref, s_ref, do_ref = ins[6 * d:6 * d + 6]
            for h in range(GLA_HEADS):
                loaded.append(((q_ref[:, _gla_k(h)], k_ref[:, _gla_k(h)], v_ref[:, _gla_v(h)], g_ref[:, _gla_k(h)],
                                s_ref[h]), (do_ref[:, _gla_v(h)], dstate[d, h])))
        grads = []
        for i, (primals, cotangents) in enumerate(loaded):
            _, vjp = jax.vjp(functools.partial(_gla_chunk, forward=(i < GLA_HEADS)), *primals)
            grads.append(vjp(cotangents))
        for i, (dq, dk, dv, dg, ds) in enumerate(grads):
            d, h = divmod(i, GLA_HEADS)
            dq_ref, dk_ref, dv_ref, dg_ref = outs[4 * d:4 * d + 4]
            dq_ref[:, _gla_k(h)] = dq
            dk_ref[:, _gla_k(h)] = dk
            dv_ref[:, _gla_v(h)] = dv
            dg_ref[:, _gla_k(h)] = dg
            dstate[d, h] = ds

    specs, outs, shapes = [], [], []
    for d in range(2):
        ci = (lambda i: nchunk - 1 - i) if d == 0 else (lambda i: i)
        specs += [pl.BlockSpec((c, GLA_KEY_WIDTH), lambda i, ci=ci: (ci(i), qb)),
                  pl.BlockSpec((c, GLA_KEY_WIDTH), lambda i, ci=ci: (ci(i), kb)),
                  pl.BlockSpec((c, GLA_WIDTH), lambda i, ci=ci: (ci(i), vb)),
                  pl.BlockSpec((c, GLA_KEY_WIDTH), lambda i, ci=ci, d=d: (ci(i), d)),
                  pl.BlockSpec((None, GLA_HEADS, GLA_DV, GLA_DK), lambda i, ci=ci: (ci(i), 0, 0, 0)),
                  pl.BlockSpec((c, GLA_WIDTH), lambda i, ci=ci: (ci(i), 0))]
        key = pl.BlockSpec((c, GLA_KEY_WIDTH), lambda i, ci=ci: (ci(i), 0))
        val = pl.BlockSpec((c, GLA_WIDTH), lambda i, ci=ci: (ci(i), 0))
        outs += [key, key, val, key]
        shapes += [jax.ShapeDtypeStruct((t, GLA_KEY_WIDTH), F32), jax.ShapeDtypeStruct((t, GLA_KEY_WIDTH), F32),
                   jax.ShapeDtypeStruct((t, GLA_WIDTH), F32), jax.ShapeDtypeStruct((t, GLA_KEY_WIDTH), F32)]
    return pl.pallas_call(
        body, name="gla_bwd", grid=(nchunk,),
        in_specs=specs, out_specs=outs, out_shape=shapes,
        scratch_shapes=[pltpu.VMEM((2, GLA_HEADS, GLA_DV, GLA_DK), F32)],
        compiler_params=_params(("arbitrary",), VMEM_LIMIT),
    )(proj, proj, proj, g, s_f, do, proj, proj, proj, g, s_b, do)


def _gla_out_head(o_f, o_b, gate, gn):
    return _rms(o_f + o_b, gn) * _silu(gate)


def _gla_out_fwd(o_f, o_b, proj, gn, *, tr=256):
    t = o_f.shape[0]
    gb = _seg_block("gate", GLA_WIDTH)

    def body(of_ref, ob_ref, gate_ref, gn_ref, out_ref):
        for h in range(GLA_HEADS):
            vc = slice(h * GLA_DV, (h + 1) * GLA_DV)
            out_ref[:, vc] = _gla_out_head(of_ref[:, vc], ob_ref[:, vc], gate_ref[:, vc], gn_ref[...]).astype(BF16)

    wide = pl.BlockSpec((tr, GLA_WIDTH), lambda i: (i, 0))
    return pl.pallas_call(
        body, name="gla_out_fwd", grid=(t // tr,),
        in_specs=[wide, wide, pl.BlockSpec((tr, GLA_WIDTH), lambda i: (i, gb)),
                  pl.BlockSpec((1, GLA_DV), lambda i: (0, 0))],
        out_specs=wide,
        out_shape=jax.ShapeDtypeStruct((t, GLA_WIDTH), BF16),
        compiler_params=_params(("parallel",), VMEM_LIMIT),
    )(o_f, o_b, proj, gn)


def _gla_out_bwd(o_f, o_b, proj, gn, dmix, *, tr=256):
    t = o_f.shape[0]
    gb = _seg_block("gate", GLA_WIDTH)

    def body(of_ref, ob_ref, gate_ref, gn_ref, dout_ref, do_ref, dgate_ref, dgn_ref):
        dgn = jnp.zeros((1, GLA_DV), F32)
        for h in range(GLA_HEADS):
            vc = slice(h * GLA_DV, (h + 1) * GLA_DV)
            _, vjp = jax.vjp(_gla_out_head, of_ref[:, vc], ob_ref[:, vc], gate_ref[:, vc], gn_ref[...])
            do, _, dgate, dg = vjp(dout_ref[:, vc])
            do_ref[:, vc] = do
            dgate_ref[:, vc] = dgate.astype(BF16)
            dgn = dgn + dg

        @pl.when(pl.program_id(0) == 0)
        def _():
            dgn_ref[...] = jnp.zeros_like(dgn_ref)

        dgn_ref[...] += dgn

    wide = pl.BlockSpec((tr, GLA_WIDTH), lambda i: (i, 0))
    vec = pl.BlockSpec((1, GLA_DV), lambda i: (0, 0))
    return pl.pallas_call(
        body, name="gla_out_bwd", grid=(t // tr,),
        in_specs=[wide, wide, pl.BlockSpec((tr, GLA_WIDTH), lambda i: (i, gb)), vec,
                  pl.BlockSpec((tr, GLA_WIDTH), lambda i: (i, 1))],
        out_specs=[wide, wide, vec],
        out_shape=[jax.ShapeDtypeStruct((t, GLA_WIDTH), F32), jax.ShapeDtypeStruct((t, GLA_WIDTH), BF16),
                   jax.ShapeDtypeStruct((1, GLA_DV), F32)],
        compiler_params=_params(("arbitrary",), VMEM_LIMIT),
    )(o_f, o_b, proj, gn, dmix)


CONV_TR = 512
CONV_TC = 512
HALO = 8
HALO16 = 16


def _conv3(u, w, b):
    n = u.shape[0]
    return pltpu.roll(u, 1, 0) * w[0:1] + u * w[1:2] + pltpu.roll(u, n - 1, 0) * w[2:3] + b


def _conv_ext(main_ref, prev_ref, next_ref, r, nr):
    prev = prev_ref[...].astype(F32)[-HALO:] * (r > 0).astype(F32)
    nxt = next_ref[...].astype(F32)[:HALO] * (r < nr - 1).astype(F32)
    return jnp.concatenate([prev, main_ref[...].astype(F32), nxt], axis=0)


def _conv_specs(t, halo, half=None):
    per = CONV_TR // halo
    last = t // halo - 1
    lead = () if half is None else (None,)
    at = (lambda *ix: ix) if half is None else (lambda *ix: (half,) + ix)
    return [pl.BlockSpec(lead + (CONV_TR, CONV_TC), lambda j, r: at(r, j)),
            pl.BlockSpec(lead + (halo, CONV_TC), lambda j, r: at(jnp.maximum(r * per - 1, 0), j)),
            pl.BlockSpec(lead + (halo, CONV_TC), lambda j, r: at(jnp.minimum((r + 1) * per, last), j))]


def _ffn_mid_fwd(u, cw_g, cw_v, cb_g, cb_v):
    _, t, f = u.shape
    nr = t // CONV_TR

    def body(ug, ugp, ugn, uv, uvp, uvn, wg, wv, bg, bv, a_ref, c_ref):
        r = pl.program_id(1)
        gate = _conv3(_conv_ext(ug, ugp, ugn, r, nr), wg[...], bg[...])[HALO:HALO + CONV_TR]
        val = _conv3(_conv_ext(uv, uvp, uvn, r, nr), wv[...], bv[...])[HALO:HALO + CONV_TR]
        c_ref[0] = gate
        c_ref[1] = val
        a_ref[...] = (_silu(gate) * val).astype(BF16)

    w_spec = pl.BlockSpec((3, CONV_TC), lambda j, r: (0, j))
    b_spec = pl.BlockSpec((1, CONV_TC), lambda j, r: (0, j))
    return pl.pallas_call(
        body, name="ffn_mid_fwd", grid=(f // CONV_TC, nr),
        in_specs=_conv_specs(t, HALO, 0) + _conv_specs(t, HALO, 1) + [w_spec, w_spec, b_spec, b_spec],
        out_specs=[pl.BlockSpec((CONV_TR, CONV_TC), lambda j, r: (r, j)),
                   pl.BlockSpec((2, CONV_TR, CONV_TC), lambda j, r: (0, r, j))],
        out_shape=[jax.ShapeDtypeStruct((t, f), BF16), jax.ShapeDtypeStruct((2, t, f), F32)],
        compiler_params=_params(("parallel", "parallel"), VMEM_LIMIT),
    )(u, u, u, u, u, u, cw_g, cw_v, cb_g, cb_v)


def _ffn_mid_bwd(u, c, cw_g, cw_v, da):
    _, t, f = u.shape
    nr = t // CONV_TR
    ext = CONV_TR + 2 * HALO

    def body(cg, cgp, cgn, cv, cvp, cvn, dam, dap, dan, ug, uv, wg, wv,
             du_ref, dwg_ref, dwv_ref, dbg_ref, dbv_ref):
        r = pl.program_id(1)
        gate = _conv_ext(cg, cgp, cgn, r, nr)
        val = _conv_ext(cv, cvp, cvn, r, nr)
        da_x = _conv_ext(dam, dap, dan, r, nr)
        sig = jax.nn.sigmoid(gate)
        silu = gate * sig
        d_val = da_x * silu
        d_gate = da_x * val * (sig + silu * (1.0 - sig))
        own = slice(HALO, HALO + CONV_TR)
        for half, (x_ref, d, w, dw_ref, db_ref) in enumerate(((ug, d_gate, wg, dwg_ref, dbg_ref),
                                                              (uv, d_val, wv, dwv_ref, dbv_ref))):
            wt = w[...]
            d_next, d_own, d_prev = pltpu.roll(d, ext - 1, 0)[own], d[own], pltpu.roll(d, 1, 0)[own]
            du_ref[half] = (d_next * wt[0:1] + d_own * wt[1:2] + d_prev * wt[2:3]).astype(BF16)
            x = x_ref[...]
            dw = jnp.concatenate([
                jnp.sum(x * d_next, axis=0, keepdims=True),
                jnp.sum(x * d_own, axis=0, keepdims=True),
                jnp.sum(x * d_prev, axis=0, keepdims=True)], axis=0)
            db = jnp.sum(d_own, axis=0, keepdims=True)

            @pl.when(r == 0)
            def _():
                dw_ref[...] = jnp.zeros_like(dw_ref)
                db_ref[...] = jnp.zeros_like(db_ref)

            dw_ref[...] += dw
            db_ref[...] += db

    w_spec = pl.BlockSpec((3, CONV_TC), lambda j, r: (0, j))
    b_spec = pl.BlockSpec((1, CONV_TC), lambda j, r: (0, j))
    return pl.pallas_call(
        body, name="ffn_mid_bwd", grid=(f // CONV_TC, nr),
        in_specs=(_conv_specs(t, HALO, 0) + _conv_specs(t, HALO, 1) + _conv_specs(t, HALO16)
                  + [_conv_specs(t, HALO, 0)[0], _conv_specs(t, HALO, 1)[0], w_spec, w_spec]),
        out_specs=[pl.BlockSpec((2, CONV_TR, CONV_TC), lambda j, r: (0, r, j)), w_spec, w_spec, b_spec, b_spec],
        out_shape=[jax.ShapeDtypeStruct((2, t, f), BF16),
                   jax.ShapeDtypeStruct((3, f), F32), jax.ShapeDtypeStruct((3, f), F32),
                   jax.ShapeDtypeStruct((1, f), F32), jax.ShapeDtypeStruct((1, f), F32)],
        compiler_params=_params(("parallel", "arbitrary"), VMEM_LIMIT),
    )(c, c, c, c, c, c, da, da, da, u, u, cw_g, cw_v)


def _loss_head(y, target, *, tr=256):
    t, d = y.shape

    def body(y_ref, t_ref, loss_ref, dy_ref, dyb_ref):
        err = y_ref[...] - t_ref[...]
        dy = err * (1.0 / d)
        dy_ref[...] = dy
        dyb_ref[...] = dy.astype(BF16)
        part = 0.5 * jnp.sum(jnp.sum(err * err, axis=-1, keepdims=True) * (1.0 / d), axis=0, keepdims=True)

        @pl.when(pl.program_id(0) == 0)
        def _():
            loss_ref[...] = jnp.zeros_like(loss_ref)

        loss_ref[...] += jnp.broadcast_to(part, loss_ref.shape)

    row = pl.BlockSpec((tr, d), lambda i: (i, 0))
    return pl.pallas_call(
        body, name="loss_head", grid=(t // tr,),
        in_specs=[row, row],
        out_specs=[pl.BlockSpec((1, 128), lambda i: (0, 0)), row, row],
        out_shape=[jax.ShapeDtypeStruct((1, 128), F32), jax.ShapeDtypeStruct((t, d), F32),
                   jax.ShapeDtypeStruct((t, d), BF16)],
        compiler_params=_params(("arbitrary",), VMEM_LIMIT),
    )(y, target)


ANY = pl.BlockSpec(memory_space=pl.ANY)


def _position():
    return lax.axis_index("x"), lax.axis_index("y"), lax.axis_index("c")


def _other_chips(x, y):
    return [(1 - x, y), (x, 1 - y), (1 - x, 1 - y)]


def _handshake(peers):
    barrier = pltpu.get_barrier_semaphore()
    for peer in peers:
        pl.semaphore_signal(barrier, inc=1, device_id=peer, device_id_type=MESH)
    pl.semaphore_wait(barrier, len(peers))


def _exchange(body, operands, out_shapes, sems, *, name, collective_id):
    n_in, n_out = len(operands), len(out_shapes)

    def run(*refs):
        body(refs[:n_in], refs[n_in:n_in + n_out], *refs[n_in + n_out:])

    if collective_id is None:
        return pl.pallas_call(run, name=name, in_specs=[ANY] * n_in, out_specs=[ANY] * n_out,
                              out_shape=out_shapes, scratch_shapes=sems)(*operands)
    return pl.kernel(run, name=name, out_type=out_shapes,
                     mesh=plsc.ScalarSubcoreMesh(axis_name="sequencer", num_cores=1), scratch_types=sems,
                     compiler_params=pltpu.CompilerParams(collective_id=collective_id))(*operands)


def _all_gather(blocks, *, name, collective_id=None):
    na = len(blocks)

    def body(ins, outs, send_sems, recv_sems, local_sems):
        x, y, c = _position()
        sibling = (x, y, 1 - c)
        chips = _other_chips(x, y)
        if collective_id is not None:
            _handshake([sibling] + [(*chip, c) for chip in chips])

        def index(px, py, pc):
            return 4 * px + 2 * py + pc

        def copy(a, k, block, to, src=None):
            dst = outs[a].at[index(*block)]
            return pltpu.make_async_remote_copy(
                src_ref=dst if src is None else src, dst_ref=dst,
                send_sem=send_sems.at[a, k], recv_sem=recv_sems.at[a, k],
                device_id=to, device_id_type=MESH)

        pending = []
        for a in range(na):
            mine = pltpu.make_async_copy(ins[a], outs[a].at[index(x, y, c)], local_sems.at[a])
            mine.start()
            pending.append(mine)
        first = []
        for a in range(na):
            first.append(copy(a, 0, (x, y, c), sibling, src=ins[a]))
            first += [copy(a, 1 + j, (x, y, c), (*chip, c), src=ins[a]) for j, chip in enumerate(chips)]
        for cp in first:
            cp.start()
        passed = []
        for j, chip in enumerate(chips):
            for a in range(na):
                copy(a, 1 + j, (*chip, c), (x, y, c)).wait_recv()
                fwd = copy(a, 4 + j, (*chip, c), sibling)
                fwd.start()
                passed.append(fwd)
        for a in range(na):
            copy(a, 0, sibling, (x, y, c)).wait_recv()
            for j, chip in enumerate(chips):
                copy(a, 4 + j, (*chip, 1 - c), (x, y, c)).wait_recv()
        for cp in first + passed:
            cp.wait_send()
        for cp in pending:
            cp.wait()

    return _exchange(
        body, blocks, [jax.ShapeDtypeStruct((N_DEV,) + b.shape, b.dtype) for b in blocks],
        [pltpu.SemaphoreType.DMA((na, 7)), pltpu.SemaphoreType.DMA((na, 7)), pltpu.SemaphoreType.DMA((na,))],
        name=name, collective_id=collective_id)


def _grad_exchange(grads, parts, *, name, collective_id):
    ng, npart = len(grads), len(parts)

    def body(ins, outs, core_send, core_recv, chip_send, chip_recv, local_sems):
        x, y, c = _position()
        sibling = (x, y, 1 - c)
        chips = _other_chips(x, y)
        _handshake([sibling] + [(px, py, c) for px, py in chips])
        me = 2 * x + y
        copies = []
        for b in range(npart):
            src, dst = ins[ng + b], outs[ng + b]
            own = pltpu.make_async_copy(src.at[me], dst.at[me], local_sems.at[b])
            own.start()
            copies.append(own)
            for j, (px, py) in enumerate(chips):
                cp = pltpu.make_async_remote_copy(
                    src_ref=src.at[2 * px + py], dst_ref=dst.at[me],
                    send_sem=chip_send.at[b, j], recv_sem=chip_recv.at[b, j],
                    device_id=(px, py, c), device_id_type=MESH)
                cp.start()
                copies.append(cp)
        for a in range(ng):
            for k in range(N_CHIP):
                cp = pltpu.make_async_remote_copy(
                    src_ref=ins[a].at[k, 1 - c], dst_ref=outs[a].at[k],
                    send_sem=core_send.at[a, k], recv_sem=core_recv.at[a, k],
                    device_id=sibling, device_id_type=MESH)
                cp.start()
                copies.append(cp)
        for cp in copies:
            cp.wait()

    shapes = ([jax.ShapeDtypeStruct((N_CHIP,) + g.shape[2:], g.dtype) for g in grads]
              + [jax.ShapeDtypeStruct(p.shape, p.dtype) for p in parts])
    sems = [pltpu.SemaphoreType.DMA((max(ng, 1), N_CHIP)), pltpu.SemaphoreType.DMA((max(ng, 1), N_CHIP)),
            pltpu.SemaphoreType.DMA((max(npart, 1), 3)), pltpu.SemaphoreType.DMA((max(npart, 1), 3)),
            pltpu.SemaphoreType.DMA((max(npart, 1),))]
    out = _exchange(body, list(grads) + list(parts), shapes, sems, name=name, collective_id=collective_id)
    return out[:ng], out[ng:]


def _pair_sum(grad, theirs, core, *, tile, name):
    _, _, r, w = grad.shape
    tr, tw = tile
    assert r % tr == 0 and w % tw == 0

    def body(core_ref, mine_ref, theirs_ref, out_ref):
        out_ref[...] = mine_ref[...] + theirs_ref[...]

    return pl.pallas_call(
        body, name=name,
        grid_spec=pltpu.PrefetchScalarGridSpec(
            num_scalar_prefetch=1, grid=(N_CHIP, r // tr, w // tw),
            in_specs=[pl.BlockSpec((None, None, tr, tw), lambda k, i, j, core_ref: (k, core_ref[0], i, j)),
                      pl.BlockSpec((None, tr, tw), lambda k, i, j, core_ref: (k, i, j))],
            out_specs=pl.BlockSpec((None, tr, tw), lambda k, i, j, core_ref: (k, i, j))),
        out_shape=jax.ShapeDtypeStruct((N_CHIP, r, w), F32),
        compiler_params=_params(("parallel", "parallel", "parallel"), VMEM_LIMIT),
    )(core, grad, theirs)


class _ReduceScatter:
    def __init__(self, core):
        self.core = core
        self.pending = None
        self.results = {}
        self.launches = 0

    def push(self, tag, grads, rows, then):
        pair, prev_tag = [], None
        if self.pending is not None:
            prev_tag, prev, theirs, prev_rows = self.pending
            pair = [_pair_sum(g, s, self.core, tile=tile, name=f"pair_sum_{prev_tag}_{i}")
                    for i, (g, s, tile) in enumerate(zip(prev, theirs, prev_rows))]
        grads, pair, then = lax.optimization_barrier((list(grads), pair, then))
        grads = [g.reshape((N_CHIP, 2) + g.shape[1:]) for g in grads]
        self.launches += 1
        theirs, parts = _grad_exchange(grads, pair, name=f"grad_exchange_{self.launches}",
                                       collective_id=1 + self.launches)
        if prev_tag is not None:
            self.results[prev_tag] = parts
        self.pending = (tag, grads, theirs, rows) if tag is not None else None
        return then

    def result(self, tag):
        return self.results[tag]


def _adamw(parts, w, m, v, *, tile, name):
    n, r, cols = parts.shape
    tr, tw = tile
    assert r % tr == 0 and cols % tw == 0 and w.shape == (r, cols)
    c1 = 1.0 - ADAM_B1 ** ADAM_STEP
    c2 = 1.0 - ADAM_B2 ** ADAM_STEP

    def body(p_ref, w_ref, m_ref, v_ref, g_ref, d_ref, nm_ref, nv_ref):
        g = p_ref[0]
        for k in range(1, n):
            g = g + p_ref[k]
        new_m = ADAM_B1 * m_ref[...] + (1.0 - ADAM_B1) * g
        new_v = ADAM_B2 * v_ref[...] + (1.0 - ADAM_B2) * (g * g)
        m_hat = new_m / c1
        v_hat = new_v / c2
        g_ref[...] = g
        d_ref[...] = -ADAM_LR * (m_hat / (jnp.sqrt(v_hat) + ADAM_EPS) + ADAM_WD * w_ref[...])
        nm_ref[...] = new_m
        nv_ref[...] = new_v

    spec = pl.BlockSpec((tr, tw), lambda i, j: (i, j))
    shape = jax.ShapeDtypeStruct((r, cols), F32)
    return pl.pallas_call(
        body, name=name, grid=(r // tr, cols // tw),
        in_specs=[pl.BlockSpec((n, tr, tw), lambda i, j: (0, i, j)), spec, spec, spec],
        out_specs=[spec] * 4, out_shape=[shape] * 4,
        compiler_params=_params(("parallel", "parallel"), VMEM_LIMIT),
    )(parts, w, m, v)


def _rope_tables(t):
    half = HEAD_DIM // 2
    inv = 1.0 / (ROPE_THETA ** (jnp.arange(half, dtype=F32) / half))
    ang = jnp.arange(t, dtype=jnp.int32).astype(F32)[:, None] * inv[None, :]
    cos, sin = jnp.cos(ang), jnp.sin(ang)
    return jnp.concatenate([cos, cos], axis=1), jnp.concatenate([-sin, sin], axis=1)


IN_KERNEL = IN_MAIN + LR_PAD


def _to_kernel_rows(w_t):
    order = sorted(SEGMENTS.values())
    pad = jnp.zeros((LR_PAD - (IN_TOTAL - IN_MAIN), w_t.shape[1]), w_t.dtype)
    return jnp.concatenate([w_t[src:src + width] for _, src, width in order] + [w_t[IN_MAIN:IN_TOTAL], pad], axis=0)


def _pack(pieces):
    flat = []
    for p in pieces:
        p = p.reshape(-1)
        flat.append(jnp.pad(p, (0, (-p.shape[0]) % 128)))
    return jnp.concatenate(flat).reshape(-1, 128)


def _unpack(packed, shapes):
    flat = packed.reshape(-1)
    out, off = [], 0
    for s in shapes:
        size = 1
        for dim in s:
            size *= dim
        out.append(flat[off:off + size].reshape(s))
        off += size + (-size) % 128
    return out


def _local_step(xs, target, norm1_g, w_in_k, gq, gk, attn_sink, w2, ba2, gla_norm_g, w_out_full, norm2_g,
                w_up8, cw_g, cw_v, cb_g, cb_v, w_down_full, rs=None):
    t = xs.shape[0]
    tm = min(1024, t)
    cos, sin_signed = _rope_tables(t)
    sink = attn_sink.reshape(ATTN_HEADS)

    h1 = _rmsnorm_fwd(xs, norm1_g, name="norm1_fwd")
    proj = _matmul(h1, w_in_k, "nt", tm=tm, tn=IN_MAIN // 4, tk=D_MODEL, n_out=IN_MAIN, name="proj_main")
    proj_lr = _matmul(h1, w_in_k[IN_MAIN:], "nt", tm=tm, tn=LR_PAD, tk=D_MODEL, name="proj_lr")
    qa, ka, va = _attn_prep_fwd(proj, cos, sin_signed, gq, gk)
    o_attn = _attn_fwd(qa, ka, va, sink)
    g_dec = _gla_prep_fwd(proj_lr, w2, ba2)
    o_f, o_b, s_f, s_b = _gla_fwd(proj, g_dec)
    o_gla = _gla_out_fwd(o_f, o_b, proj, gla_norm_g)
    mix = jnp.concatenate([o_attn, o_gla], axis=1)
    x1 = _matmul(mix, w_out_full, "nn", tm=tm, tn=1024, tk=D_MODEL, res=xs, name="out_proj")
    h2 = _rmsnorm_fwd(x1, norm2_g, name="norm2_fwd")
    u = _up_proj(h2, w_up8, tm=tm)
    act, conv_out = _ffn_mid_fwd(u, cw_g, cw_v, cb_g, cb_v)
    y = _matmul(act, w_down_full, "nn", tm=tm, tn=512, tk=D_FF, res=x1, name="down_proj")
    loss_part, dy, dy_b = _loss_head(y, target)

    d_act = _matmul(dy_b, w_down_full, "nt", tm=tm, tn=D_FF // 4, tk=D_MODEL, out_dtype=BF16, name="d_act")
    dw_down = _matmul(act, dy_b, "tn", tm=512, tn=1024, tk=t, name="dw_down")
    if rs is not None:
        d_act = rs.push("w_down", [dw_down.reshape(N_DEV, D_FF // N_DEV, D_MODEL)], [(64, D_MODEL)], d_act)
    du, dcw_g, dcw_v, dcb_g, dcb_v = _ffn_mid_bwd(u, conv_out, cw_g, cw_v, d_act)
    dw_up8 = _up_proj_dw(h2, du, tm=512, tk=t)
    if rs is not None:
        du = rs.push("w_up", [dw_up8], [(256, UP_BLOCK)], du)
    dh2 = _up_proj_dx(du, w_up8, tm=tm, tn=1024)
    dx1, dx1_b, d_norm2 = _rmsnorm_bwd(x1, norm2_g, dh2, dy, name="norm2_bwd")
    dmix = _matmul(dx1_b, w_out_full, "nt", tm=tm, tn=1024, tk=D_MODEL, name="d_mix")
    dw_out = _matmul(mix, dx1_b, "tn", tm=1024, tn=1024, tk=t, name="dw_out")
    if rs is not None:
        dmix = rs.push("w_out", [dw_out.reshape(N_DEV, D_MODEL // N_DEV, D_MODEL)], [(256, D_MODEL)], dmix)
    do_gla, d_gate, d_gla_norm = _gla_out_bwd(o_f, o_b, proj, gla_norm_g, dmix)
    (dq_f, dk_f, dv_f, dg_f, dq_b, dk_b, dv_b, dg_b) = _gla_bwd(proj, g_dec, s_f, s_b, do_gla)
    d_lr, d_w2, d_ba2 = _gla_prep_bwd(proj_lr, w2, ba2, dg_f, dg_b)
    dqa, dk_lo, dk_mid, dk_hi, dv_lo, dv_mid, dv_hi, d_sink8 = _attn_bwd(qa, ka, va, sink, dmix)
    d_qa, d_ka, d_va, d_qn, d_kn = _attn_prep_bwd(proj, cos, sin_signed, gq, gk, dqa,
                                                  (dk_lo, dk_mid, dk_hi), (dv_lo, dv_mid, dv_hi))
    d_seg = {"qa": d_qa, "gate": d_gate, "vg": (dv_f + dv_b).astype(BF16), "qg": (dq_f + dq_b).astype(BF16),
             "kg": (dk_f + dk_b).astype(BF16), "ka": d_ka, "va": d_va}
    d_proj = jnp.concatenate([d_seg[k] for k in sorted(SEGMENTS, key=lambda k: SEGMENTS[k][0])] + [d_lr], axis=1)
    dw_in_t = _in_proj_dw_lr(_in_proj_dw(d_proj, h1, tn=1024), d_lr, h1)
    if rs is not None:
        per_in, per_wa = IN_TOTAL // N_DEV, GLA_KEY_WIDTH // N_DEV
        dconv_w = jnp.concatenate([dcw_g, dcw_v], axis=1)
        d_wa2_f = d_w2[:GLA_RANK, :GLA_KEY_WIDTH]
        d_wa2_b = d_w2[GLA_RANK:2 * GLA_RANK, GLA_KEY_WIDTH:]
        small_grad = jnp.stack([
            _pack([dconv_w[:, d * UP_BLOCK:(d + 1) * UP_BLOCK], d_wa2_f[:, d * per_wa:(d + 1) * per_wa],
                   d_wa2_b[:, d * per_wa:(d + 1) * per_wa]]) for d in range(N_DEV)])
        d_proj, d_lr = rs.push("w_in", [dw_in_t.reshape(N_DEV, per_in, D_MODEL), small_grad],
                               [(per_in, 512), small_grad.shape[1:]], (d_proj, d_lr))
    dh1 = _matmul(d_proj, w_in_k, "nn", tm=tm, tn=512, tk=IN_KERNEL, name="dh1")
    if rs is not None:
        dh1 = rs.push(None, [], [], dh1)
    grad_x, _, d_norm1 = _rmsnorm_bwd(xs, norm1_g, dh1, dx1, name="norm1_bwd")
    return (loss_part, grad_x, dw_in_t, dw_out, dw_up8, dw_down, dcw_g, dcw_v, dcb_g, dcb_v,
            d_w2, d_ba2, d_norm1, d_norm2, d_qn, d_kn, d_sink8, d_gla_norm)


def kernel(x, norm1_g, w_in, attn_q_norm_g, attn_k_norm_g, attn_sink, gla_wa2_fwd, gla_ba_fwd, gla_wa2_bwd, gla_ba_bwd, gla_out_norm_g, w_out, norm2_g, w_up, conv_w, conv_b, w_down, loss_target, m_norm1_g, m_w_in, m_attn_q_norm_g, m_attn_k_norm_g, m_attn_sink, m_gla_wa2_fwd, m_gla_ba_fwd, m_gla_wa2_bwd, m_gla_ba_bwd, m_gla_out_norm_g, m_w_out, m_norm2_g, m_w_up, m_conv_w, m_conv_b, m_w_down, v_norm1_g, v_w_in, v_attn_q_norm_g, v_attn_k_norm_g, v_attn_sink, v_gla_wa2_fwd, v_gla_ba_fwd, v_gla_wa2_bwd, v_gla_ba_bwd, v_gla_out_norm_g, v_w_out, v_norm2_g, v_w_up, v_conv_w, v_conv_b, v_w_down):
    t = x.shape[1]
    xs = x.reshape(t, D_MODEL)
    target = loss_target.reshape(t, D_MODEL)
    core = lax.axis_index("c").astype(jnp.int32).reshape(1)

    sharded_small = [conv_w[0], gla_wa2_fwd[0], gla_wa2_bwd[0]]
    small_shapes = [s.shape for s in sharded_small]
    w_in_t, m_in_t, v_in_t = (jnp.swapaxes(a[0], 0, 1) for a in (w_in, m_w_in, v_w_in))
    g_in, g_small = _all_gather([w_in_t.astype(BF16), _pack(sharded_small)], name="gather_w_in")
    g_in, later = lax.optimization_barrier(
        (g_in, [w_out[0].astype(BF16), w_up[0].astype(BF16), w_down[0].astype(BF16)]))
    g_out, w_up8, g_down = _all_gather(later, name="gather_later_weights", collective_id=1)
    w_in_k = _to_kernel_rows(g_in.reshape(IN_TOTAL, D_MODEL))
    w_out_full = g_out.reshape(D_MODEL, D_MODEL)
    w_down_full = g_down.reshape(D_FF, D_MODEL)
    small_full = [_unpack(g_small[d], small_shapes) for d in range(N_DEV)]
    conv_w_full = jnp.concatenate([s[0] for s in small_full], axis=1)
    wa2_f = jnp.concatenate([s[1] for s in small_full], axis=1)
    wa2_b = jnp.concatenate([s[2] for s in small_full], axis=1)
    cw_g, cw_v = conv_w_full[:, :D_FF], conv_w_full[:, D_FF:]
    cb_g, cb_v = conv_b[:, :D_FF], conv_b[:, D_FF:]
    w2 = jnp.zeros((LR_PAD, 2 * GLA_KEY_WIDTH), F32)
    w2 = w2.at[:GLA_RANK, :GLA_KEY_WIDTH].set(wa2_f).at[GLA_RANK:2 * GLA_RANK, GLA_KEY_WIDTH:].set(wa2_b)
    ba2 = jnp.concatenate([gla_ba_fwd, gla_ba_bwd], axis=1)
    rs = _ReduceScatter(core)
    (loss_part, grad_x, _, _, _, _, _, _, dcb_g, dcb_v, _, d_ba2,
     d_norm1, d_norm2, d_qn, d_kn, d_sink8, d_gla_norm) = _local_step(
        xs, target, norm1_g, w_in_k, attn_q_norm_g, attn_k_norm_g, attn_sink, w2, ba2, gla_out_norm_g,
        w_out_full, norm2_g, w_up8, cw_g, cw_v, cb_g, cb_v, w_down_full, rs=rs)
    loss = lax.psum(loss_part[0, 0], ("x", "y", "c"))

    (part_down,), (part_up,), (part_out,) = rs.result("w_down"), rs.result("w_up"), rs.result("w_out")
    part_in, part_small = rs.result("w_in")
    m_small = _pack([m_conv_w[0], m_gla_wa2_fwd[0], m_gla_wa2_bwd[0]])
    v_small = _pack([v_conv_w[0], v_gla_wa2_fwd[0], v_gla_wa2_bwd[0]])
    upd_in = _adamw(part_in, w_in_t, m_in_t, v_in_t, tile=(IN_TOTAL // N_DEV, 512), name="adamw_w_in")
    upd_in = [jnp.swapaxes(u, 0, 1) for u in upd_in]
    upd_out = _adamw(part_out, w_out[0], m_w_out[0], v_w_out[0], tile=(256, D_MODEL), name="adamw_w_out")
    upd_up = _adamw(part_up, w_up[0], m_w_up[0], v_w_up[0], tile=(256, UP_BLOCK), name="adamw_w_up")
    upd_down = _adamw(part_down, w_down[0], m_w_down[0], v_w_down[0], tile=(64, D_MODEL), name="adamw_w_down")
    upd_small = _adamw(part_small, _pack(sharded_small), m_small, v_small, tile=part_small.shape[1:],
                       name="adamw_small")
    upd_small = [_unpack(u, small_shapes) for u in upd_small]

    rep_names = ["norm1_g", "attn_q_norm_g", "attn_k_norm_g", "attn_sink", "gla_ba_fwd", "gla_ba_bwd",
                 "gla_out_norm_g", "norm2_g", "conv_b"]
    rep_w = [norm1_g, attn_q_norm_g, attn_k_norm_g, attn_sink, gla_ba_fwd, gla_ba_bwd, gla_out_norm_g, norm2_g, conv_b]
    rep_m = [m_norm1_g, m_attn_q_norm_g, m_attn_k_norm_g, m_attn_sink, m_gla_ba_fwd, m_gla_ba_bwd,
             m_gla_out_norm_g, m_norm2_g, m_conv_b]
    rep_v = [v_norm1_g, v_attn_q_norm_g, v_attn_k_norm_g, v_attn_sink, v_gla_ba_fwd, v_gla_ba_bwd,
             v_gla_out_norm_g, v_norm2_g, v_conv_b]
    d_sink = d_sink8[:, :GQA_GROUP, 0].reshape(1, ATTN_HEADS)
    rep_g = [d_norm1, d_qn, d_kn, d_sink, d_ba2[:, :GLA_KEY_WIDTH], d_ba2[:, GLA_KEY_WIDTH:], d_gla_norm, d_norm2,
             jnp.concatenate([dcb_g, dcb_v], axis=1)]
    rep_shapes = [w.shape for w in rep_w]
    (rep_terms,) = _all_gather([_pack(rep_g)], name="gather_small_grads")
    upd_rep = _adamw(rep_terms, _pack(rep_w), _pack(rep_m), _pack(rep_v), tile=rep_terms.shape[1:],
                     name="adamw_replicated")
    upd_rep = [_unpack(u, rep_shapes) for u in upd_rep]

    order = ["norm1_g", "w_in", "attn_q_norm_g", "attn_k_norm_g", "attn_sink", "gla_wa2_fwd", "gla_ba_fwd",
             "gla_wa2_bwd", "gla_ba_bwd", "gla_out_norm_g", "w_out", "norm2_g", "w_up", "conv_w", "conv_b", "w_down"]
    outs = [loss, grad_x.reshape(1, t, D_MODEL)]
    for kind in range(4):
        by_name = {n: upd_rep[kind][i] for i, n in enumerate(rep_names)}
        by_name["w_in"] = upd_in[kind][None]
        by_name["w_out"] = upd_out[kind][None]
        by_name["w_up"] = upd_up[kind][None]
        by_name["w_down"] = upd_down[kind][None]
        by_name["conv_w"] = upd_small[kind][0][None]
        by_name["gla_wa2_fwd"] = upd_small[kind][1][None]
        by_name["gla_wa2_bwd"] = upd_small[kind][2][None]
        outs += [by_name[n] for n in order]
    return tuple(outs)
```

```python
import functools

import jax
import jax.numpy as jnp
from jax import lax
from jax.experimental import pallas as pl
from jax.experimental.pallas import tpu as pltpu
from jax.experimental.pallas import tpu_sc as plsc

F32 = jnp.float32
BF16 = jnp.bfloat16

D_MODEL = 2048
HEAD_DIM = 128
ATTN_WIDTH = 1024
ATTN_HEADS = 8
KV_HEADS = 2
GQA_GROUP = 4
KV_WIDTH = KV_HEADS * HEAD_DIM
ATTN_BLOCK = 128
WINDOW = 128
ROPE_THETA = 10000.0
GLA_HEADS = 4
GLA_DK = 128
GLA_DV = 256
GLA_KEY_WIDTH = 512
GLA_WIDTH = 1024
GLA_RANK = 16
GLA_GATE_NORMALIZER = 16.0
GLA_CHUNK = 64
D_FF = 5632
NORM_EPS = 1e-6
IN_TOTAL = 4640
IN_MAIN = 4608
LR_PAD = 128
N_DEV = 8
N_CHIP = 4

ADAM_LR = 0.001
ADAM_B1 = 0.9
ADAM_B2 = 0.999
ADAM_EPS = 1e-08
ADAM_WD = 0.01
ADAM_STEP = 10

SEGMENTS = {
    "qa": (0, 0, 1024),
    "gate": (1024, 3584, 1024),
    "vg": (2048, 2560, 1024),
    "qg": (3072, 1536, 512),
    "kg": (3584, 2048, 512),
    "ka": (4096, 1024, 256),
    "va": (4352, 1280, 256),
}

VMEM_LIMIT = 56 * 1024 * 1024
MESH = pl.DeviceIdType.MESH


def _params(semantics=None, vmem=None):
    return pltpu.CompilerParams(dimension_semantics=semantics, vmem_limit_bytes=vmem)


_DIMS = {
    "nn": (((1,), (0,)), ((), ())),
    "nt": (((1,), (1,)), ((), ())),
    "tn": (((0,), (0,)), ((), ())),
}


def _mxu(a, b, mode):
    return lax.dot_general(a.astype(BF16), b.astype(BF16), _DIMS[mode], preferred_element_type=F32)


@functools.partial(jax.custom_vjp, nondiff_argnums=(2,))
def bdot(a, b, mode):
    return _mxu(a, b, mode)


def _bdot_fwd(a, b, mode):
    return _mxu(a, b, mode), (a, b)


def _bdot_bwd(mode, res, g):
    a, b = res
    if mode == "nn":
        return _mxu(g, b, "nt"), _mxu(a, g, "tn")
    if mode == "nt":
        return _mxu(g, b, "nn"), _mxu(g, a, "tn")
    return _mxu(b, g, "nt"), _mxu(a, g, "nn")


bdot.defvjp(_bdot_fwd, _bdot_bwd)


def _rms(x, g):
    return x * lax.rsqrt(jnp.mean(x * x, axis=-1, keepdims=True) + NORM_EPS) * g


def _rope(x, cos, sin_signed):
    return x * cos + pltpu.roll(x, HEAD_DIM // 2, 1) * sin_signed


def _rope_transposed(d, cos, sin_signed):
    return d * cos + pltpu.roll(d * sin_signed, HEAD_DIM // 2, 1)


def _silu(x):
    return x * jax.nn.sigmoid(x)


def _log_sigmoid(z):
    return -(jnp.maximum(-z, 0.0) + jnp.log(1.0 + jnp.exp(-jnp.abs(z))))


def _matmul_call(args, in_specs, o_spec, out_shape, grid, mode, nk, acc_shape, *, name, has_res=False,
                 prefetch=None, load_b=lambda ref: ref[...]):
    dims = _DIMS[mode]
    out_dtype = out_shape.dtype
    n_pre = 0 if prefetch is None else 1

    def body(*refs):
        refs = refs[n_pre:]
        if has_res:
            a_ref, b_ref, r_ref, o_ref = refs[:4]
            rest = refs[4:]
        else:
            a_ref, b_ref, o_ref = refs[:3]
            r_ref = None
            rest = refs[3:]
        part = lax.dot_general(a_ref[...], load_b(b_ref), dims, preferred_element_type=F32)

        def finish(acc):
            if r_ref is not None:
                acc = acc + r_ref[...]
            o_ref[...] = acc.astype(out_dtype)

        if nk == 1:
            finish(part)
        else:
            acc_ref = rest[0]
            kk = pl.program_id(2)

            @pl.when(kk == 0)
            def _():
                acc_ref[...] = part

            @pl.when(kk > 0)
            def _():
                acc_ref[...] += part

            @pl.when(kk == nk - 1)
            def _():
                finish(acc_ref[...])

    scratch = [pltpu.VMEM(acc_shape, F32)] if nk > 1 else []
    params = _params(("parallel", "parallel", "arbitrary"), VMEM_LIMIT)
    if prefetch is None:
        return pl.pallas_call(body, name=name, grid=grid, in_specs=in_specs, out_specs=o_spec, out_shape=out_shape,
                              scratch_shapes=scratch, compiler_params=params)(*args)
    return pl.pallas_call(
        body, name=name,
        grid_spec=pltpu.PrefetchScalarGridSpec(num_scalar_prefetch=1, grid=grid, in_specs=in_specs,
                                               out_specs=o_spec, scratch_shapes=scratch),
        out_shape=out_shape, compiler_params=params)(prefetch, *args)


def _matmul(a, b, mode, *, tm, tn, tk, out_dtype=F32, res=None, name, n_out=None):
    if mode == "nn":
        (m, k), (k2, n) = a.shape, b.shape
    elif mode == "nt":
        (m, k), (n, k2) = a.shape, b.shape
    else:
        (k, m), (k2, n) = a.shape, b.shape
    n = n if n_out is None else n_out
    assert k == k2 and m % tm == 0 and n % tn == 0 and k % tk == 0, (name, a.shape, b.shape, tm, tn, tk)
    if mode == "tn":
        a_spec = pl.BlockSpec((tk, tm), lambda i, j, kk: (kk, i))
    else:
        a_spec = pl.BlockSpec((tm, tk), lambda i, j, kk: (i, kk))
    if mode == "nt":
        b_spec = pl.BlockSpec((tn, tk), lambda i, j, kk: (j, kk))
    else:
        b_spec = pl.BlockSpec((tk, tn), lambda i, j, kk: (kk, j))
    o_spec = pl.BlockSpec((tm, tn), lambda i, j, kk: (i, j))
    in_specs, args = [a_spec, b_spec], [a, b]
    if res is not None:
        in_specs.append(o_spec)
        args.append(res)
    return _matmul_call(args, in_specs, o_spec, jax.ShapeDtypeStruct((m, n), out_dtype),
                        (m // tm, n // tn, k // tk), mode, k // tk, (tm, tn), name=name, has_res=res is not None)


def _out_proj(o_attn, o_gla, w_out, x, *, tm, tn):
    t, ka = o_attn.shape
    kg = o_gla.shape[1]

    def body(a_ref, g_ref, w_ref, x_ref, o_ref):
        acc = lax.dot_general(a_ref[...], w_ref[:ka], _DIMS["nn"], preferred_element_type=F32)
        acc = acc + lax.dot_general(g_ref[...], w_ref[ka:], _DIMS["nn"], preferred_element_type=F32)
        o_ref[...] = acc + x_ref[...]

    tile = pl.BlockSpec((tm, tn), lambda i, j: (i, j))
    return pl.pallas_call(
        body, name="out_proj", grid=(t // tm, D_MODEL // tn),
        in_specs=[pl.BlockSpec((tm, ka), lambda i, j: (i, 0)), pl.BlockSpec((tm, kg), lambda i, j: (i, 0)),
                  pl.BlockSpec((ka + kg, tn), lambda i, j: (0, j)), tile],
        out_specs=tile, out_shape=jax.ShapeDtypeStruct((t, D_MODEL), F32),
        compiler_params=_params(("parallel", "parallel"), VMEM_LIMIT),
    )(o_attn, o_gla, w_out, x)


def _out_proj_dw(o_attn, o_gla, dx1, *, tn):
    t, ka = o_attn.shape
    assert o_gla.shape == (t, ka)

    def body(a_ref, g_ref, d_ref, o_ref):
        @pl.when(pl.program_id(0) == 0)
        def _():
            o_ref[...] = lax.dot_general(a_ref[...], d_ref[...], _DIMS["tn"], preferred_element_type=F32)

        @pl.when(pl.program_id(0) == 1)
        def _():
            o_ref[...] = lax.dot_general(g_ref[...], d_ref[...], _DIMS["tn"], preferred_element_type=F32)

    whole = pl.BlockSpec((t, ka), lambda i, j: (0, 0))
    return pl.pallas_call(
        body, name="dw_out", grid=(2, D_MODEL // tn),
        in_specs=[whole, whole, pl.BlockSpec((t, tn), lambda i, j: (0, j))],
        out_specs=pl.BlockSpec((ka, tn), lambda i, j: (i, j)),
        out_shape=jax.ShapeDtypeStruct((2 * ka, D_MODEL), F32),
        compiler_params=_params(("parallel", "parallel"), VMEM_LIMIT),
    )(o_attn, o_gla, dx1)


UP_BLOCK = 2 * D_FF // N_DEV


def _up_proj(h2, w_up8, *, tm):
    t = h2.shape[0]
    return _matmul_call(
        [h2, w_up8],
        [pl.BlockSpec((tm, D_MODEL), lambda i, j, kk: (i, 0)),
         pl.BlockSpec((None, D_MODEL, UP_BLOCK), lambda i, j, kk: (j, 0, 0))],
        pl.BlockSpec((None, tm, UP_BLOCK), lambda i, j, kk: (j // N_CHIP, i, j % N_CHIP)),
        jax.ShapeDtypeStruct((2, t, D_FF), F32), (t // tm, N_DEV, 1), "nn", 1, None, name="up_proj")


def _up_proj_dx(du, w_up8, *, tm, tn):
    t = du.shape[1]
    pair = 2
    return _matmul_call(
        [du, w_up8],
        [pl.BlockSpec((None, tm, pair * UP_BLOCK), lambda i, j, kk: (kk // 2, i, kk % 2)),
         pl.BlockSpec((pair, tn, UP_BLOCK), lambda i, j, kk: (kk, j, 0))],
        pl.BlockSpec((tm, tn), lambda i, j, kk: (i, j)),
        jax.ShapeDtypeStruct((t, D_MODEL), F32), (t // tm, D_MODEL // tn, N_DEV // pair), "nt", N_DEV // pair,
        (tm, tn), name="up_proj_dx", load_b=lambda ref: jnp.concatenate([ref[0], ref[1]], axis=1))


def _up_proj_dw(h2, du, *, tm, tk):
    t = h2.shape[0]
    return _matmul_call(
        [h2, du],
        [pl.BlockSpec((tk, tm), lambda j, i, kk: (kk, i)),
         pl.BlockSpec((None, tk, UP_BLOCK), lambda j, i, kk: (j // N_CHIP, kk, j % N_CHIP))],
        pl.BlockSpec((None, tm, UP_BLOCK), lambda j, i, kk: (j, i, 0)),
        jax.ShapeDtypeStruct((N_DEV, D_MODEL, UP_BLOCK), F32), (N_DEV, D_MODEL // tm, t // tk), "tn", t // tk,
        (tm, UP_BLOCK), name="up_proj_dw")


IN_TILE = 512


def _in_proj_dw(d_proj, h1, *, tn):
    t = h1.shape[0]
    table = []
    for tile in range(IN_MAIN // IN_TILE):
        dst, src, _ = max(s for s in SEGMENTS.values() if s[0] <= tile * IN_TILE)
        assert (src + tile * IN_TILE - dst) % IN_TILE == 0
        table.append((src + tile * IN_TILE - dst) // IN_TILE)
    assert sorted(table) == list(range(IN_MAIN // IN_TILE))
    return _matmul_call(
        [d_proj, h1],
        [pl.BlockSpec((t, IN_TILE), lambda j, i, kk, tab: (0, i)),
         pl.BlockSpec((t, tn), lambda j, i, kk, tab: (0, j))],
        pl.BlockSpec((IN_TILE, tn), lambda j, i, kk, tab: (tab[i], j)),
        jax.ShapeDtypeStruct((IN_TOTAL, D_MODEL), F32), (D_MODEL // tn, IN_MAIN // IN_TILE, 1), "tn", 1, None,
        name="in_proj_dw", prefetch=jnp.asarray(table, jnp.int32))


def _in_proj_dw_lr(dw_t, d_lr, h1):
    t = h1.shape[0]
    n_lr = IN_TOTAL - IN_MAIN
    tn = 512

    def body(dw_ref, dlr_ref, h1_ref, out_ref):
        full = lax.dot_general(dlr_ref[...], h1_ref[...], _DIMS["tn"], preferred_element_type=F32)
        out_ref[...] = full[:n_lr]

    return pl.pallas_call(
        body, name="in_proj_dw_lr", grid=(D_MODEL // tn,),
        in_specs=[pl.BlockSpec(memory_space=pl.ANY),
                  pl.BlockSpec((t, LR_PAD), lambda j: (0, 0)),
                  pl.BlockSpec((t, tn), lambda j: (0, j))],
        out_specs=pl.BlockSpec((n_lr, tn), lambda j: (IN_MAIN // n_lr, j)),
        out_shape=jax.ShapeDtypeStruct(dw_t.shape, F32),
        input_output_aliases={0: 0},
        compiler_params=_params(("parallel",), VMEM_LIMIT),
    )(dw_t, d_lr, h1)


def _rmsnorm_fwd(x, g, *, name, tr=512):
    t, d = x.shape

    def body(x_ref, g_ref, h_ref):
        h_ref[...] = _rms(x_ref[...], g_ref[...]).astype(BF16)

    return pl.pallas_call(
        body, name=name, grid=(t // tr,),
        in_specs=[pl.BlockSpec((tr, d), lambda i: (i, 0)), pl.BlockSpec((1, d), lambda i: (0, 0))],
        out_specs=pl.BlockSpec((tr, d), lambda i: (i, 0)),
        out_shape=jax.ShapeDtypeStruct((t, d), BF16),
        compiler_params=_params(("parallel",), VMEM_LIMIT),
    )(x, g)


def _rmsnorm_bwd(x, g, dh, dres, *, name, tr=256):
    t, d = x.shape

    def body(x_ref, g_ref, dh_ref, dres_ref, dx_ref, dxb_ref, dg_ref):
        _, vjp = jax.vjp(_rms, x_ref[...], g_ref[...])
        dx, dg = vjp(dh_ref[...])
        dx = dx + dres_ref[...]
        dx_ref[...] = dx
        dxb_ref[...] = dx.astype(BF16)

        @pl.when(pl.program_id(0) == 0)
        def _():
            dg_ref[...] = jnp.zeros_like(dg_ref)

        dg_ref[...] += dg

    row = pl.BlockSpec((tr, d), lambda i: (i, 0))
    vec = pl.BlockSpec((1, d), lambda i: (0, 0))
    return pl.pallas_call(
        body, name=name, grid=(t // tr,),
        in_specs=[row, vec, row, row],
        out_specs=[row, row, vec],
        out_shape=[jax.ShapeDtypeStruct((t, d), F32), jax.ShapeDtypeStruct((t, d), BF16),
                   jax.ShapeDtypeStruct((1, d), F32)],
        compiler_params=_params(("arbitrary",), VMEM_LIMIT),
    )(x, g, dh, dres)


def _seg_block(name, width):
    off = SEGMENTS[name][0]
    assert off % width == 0
    return off // width


def _attn_prep_fwd(proj, cos, sin_signed, gq, gk, *, tr=256):
    t = proj.shape[0]

    def body(q_ref, k_ref, v_ref, cos_ref, sin_ref, gq_ref, gk_ref, qo_ref, ko_ref, vo_ref):
        cos_t, sin_t = cos_ref[...], sin_ref[...]
        for h in range(ATTN_HEADS):
            cols = slice(h * HEAD_DIM, (h + 1) * HEAD_DIM)
            qo_ref[:, cols] = _rope(_rms(q_ref[:, cols], gq_ref[...]), cos_t, sin_t).astype(BF16)
        for h in range(KV_HEADS):
            cols = slice(h * HEAD_DIM, (h + 1) * HEAD_DIM)
            ko_ref[:, cols] = _rope(_rms(k_ref[:, cols], gk_ref[...]), cos_t, sin_t).astype(BF16)
        vo_ref[...] = v_ref[...].astype(BF16)

    qb, kb, vb = _seg_block("qa", ATTN_WIDTH), _seg_block("ka", KV_WIDTH), _seg_block("va", KV_WIDTH)
    tab = pl.BlockSpec((tr, HEAD_DIM), lambda i: (i, 0))
    vec = pl.BlockSpec((1, HEAD_DIM), lambda i: (0, 0))
    return pl.pallas_call(
        body, name="attn_prep_fwd", grid=(t // tr,),
        in_specs=[pl.BlockSpec((tr, ATTN_WIDTH), lambda i: (i, qb)),
                  pl.BlockSpec((tr, KV_WIDTH), lambda i: (i, kb)),
                  pl.BlockSpec((tr, KV_WIDTH), lambda i: (i, vb)),
                  tab, tab, vec, vec],
        out_specs=[pl.BlockSpec((tr, ATTN_WIDTH), lambda i: (i, 0)),
                   pl.BlockSpec((tr, KV_WIDTH), lambda i: (i, 0)),
                   pl.BlockSpec((tr, KV_WIDTH), lambda i: (i, 0))],
        out_shape=[jax.ShapeDtypeStruct((t, ATTN_WIDTH), BF16),
                   jax.ShapeDtypeStruct((t, KV_WIDTH), BF16),
                   jax.ShapeDtypeStruct((t, KV_WIDTH), BF16)],
        compiler_params=_params(("parallel",), VMEM_LIMIT),
    )(proj, proj, proj, cos, sin_signed, gq, gk)


def _attn_heads(q, kcat, vcat, sink_col, valid):
    s = bdot(q, kcat, "nt") * (HEAD_DIM ** -0.5)
    s = jnp.where(valid, s, -jnp.inf)
    m = lax.stop_gradient(jnp.maximum(jnp.max(s, axis=-1, keepdims=True), sink_col))
    p = jnp.exp(s - m)
    p = p / (jnp.sum(p, axis=-1, keepdims=True) + jnp.exp(sink_col - m))
    return bdot(p, vcat, "nn")


def _attn_valid(n, t):
    shape = (GQA_GROUP * ATTN_BLOCK, 3 * ATTN_BLOCK)
    qi = lax.broadcasted_iota(jnp.int32, shape, 0) % ATTN_BLOCK
    sj = lax.broadcasted_iota(jnp.int32, shape, 1)
    kpos = n * ATTN_BLOCK - ATTN_BLOCK + sj
    return (jnp.abs(sj - ATTN_BLOCK - qi) <= WINDOW) & (kpos >= 0) & (kpos < t)


def _head_rows(g):
    return slice(g * ATTN_BLOCK, (g + 1) * ATTN_BLOCK)


def _head_cols(g):
    return slice(g * HEAD_DIM, (g + 1) * HEAD_DIM)


def _stack_heads(ref):
    return jnp.concatenate([ref[:, _head_cols(g)] for g in range(GQA_GROUP)], axis=0).astype(F32)


def _sink_column(sink_ref, h):
    return jnp.concatenate([jnp.full((ATTN_BLOCK, 1), sink_ref[h * GQA_GROUP + g], F32)
                            for g in range(GQA_GROUP)], axis=0)


def _attn_specs(nb):
    q_spec = pl.BlockSpec((ATTN_BLOCK, GQA_GROUP * HEAD_DIM), lambda h, n: (n, h))
    kv_specs = [
        pl.BlockSpec((ATTN_BLOCK, HEAD_DIM), lambda h, n: (jnp.maximum(n - 1, 0), h)),
        pl.BlockSpec((ATTN_BLOCK, HEAD_DIM), lambda h, n: (n, h)),
        pl.BlockSpec((ATTN_BLOCK, HEAD_DIM), lambda h, n: (jnp.minimum(n + 1, nb - 1), h)),
    ]
    return q_spec, kv_specs


def _attn_fwd(q, k, v, sink):
    t = q.shape[0]
    nb = t // ATTN_BLOCK

    def body(sink_ref, q_ref, kp_ref, kc_ref, kn_ref, vp_ref, vc_ref, vn_ref, o_ref):
        h, n = pl.program_id(0), pl.program_id(1)
        valid = _attn_valid(n, t)
        kcat = jnp.concatenate([kp_ref[...], kc_ref[...], kn_ref[...]], axis=0).astype(F32)
        vcat = jnp.concatenate([vp_ref[...], vc_ref[...], vn_ref[...]], axis=0).astype(F32)
        o = _attn_heads(_stack_heads(q_ref), kcat, vcat, _sink_column(sink_ref, h), valid).astype(BF16)
        for g in range(GQA_GROUP):
            o_ref[:, _head_cols(g)] = o[_head_rows(g)]

    q_spec, kv_specs = _attn_specs(nb)
    return pl.pallas_call(
        body, name="attn_fwd", grid=(KV_HEADS, nb),
        in_specs=[pl.BlockSpec(memory_space=pltpu.SMEM), q_spec] + kv_specs + kv_specs,
        out_specs=q_spec,
        out_shape=jax.ShapeDtypeStruct((t, ATTN_WIDTH), BF16),
        compiler_params=_params(("parallel", "parallel"), VMEM_LIMIT),
    )(sink, q, k, k, k, v, v, v)


def _attn_bwd(q, k, v, sink, dmix):
    t = q.shape[0]
    nb = t // ATTN_BLOCK

    def body(sink_ref, q_ref, kp_ref, kc_ref, kn_ref, vp_ref, vc_ref, vn_ref, do_ref,
             dq_ref, dk_lo, dk_mid, dk_hi, dv_lo, dv_mid, dv_hi, dsink_ref):
        h, n = pl.program_id(0), pl.program_id(1)
        valid = _attn_valid(n, t)
        kcat = jnp.concatenate([kp_ref[...], kc_ref[...], kn_ref[...]], axis=0).astype(F32)
        vcat = jnp.concatenate([vp_ref[...], vc_ref[...], vn_ref[...]], axis=0).astype(F32)
        _, vjp = jax.vjp(functools.partial(_attn_heads, valid=valid),
                         _stack_heads(q_ref), kcat, vcat, _sink_column(sink_ref, h))
        dq, dk, dv, dsink_col = vjp(_stack_heads(do_ref))
        row = lax.broadcasted_iota(jnp.int32, (8, HEAD_DIM), 0)
        dsink = jnp.zeros((8, HEAD_DIM), F32)
        for g in range(GQA_GROUP):
            dq_ref[:, _head_cols(g)] = dq[_head_rows(g)]
            dsink = dsink + jnp.where(row == g, jnp.sum(dsink_col[_head_rows(g)]), 0.0)
        for i, (dk_ref, dv_ref) in enumerate(((dk_lo, dv_lo), (dk_mid, dv_mid), (dk_hi, dv_hi))):
            rows = slice(i * ATTN_BLOCK, (i + 1) * ATTN_BLOCK)
            dk_ref[...] = dk[rows]
            dv_ref[...] = dv[rows]

        @pl.when(n == 0)
        def _():
            dsink_ref[...] = jnp.zeros_like(dsink_ref)

        dsink_ref[...] += dsink

    q_spec, kv_specs = _attn_specs(nb)
    kv_out = pl.BlockSpec((ATTN_BLOCK, HEAD_DIM), lambda h, n: (n, h))
    kv_shape = jax.ShapeDtypeStruct((t, KV_WIDTH), F32)
    return pl.pallas_call(
        body, name="attn_bwd", grid=(KV_HEADS, nb),
        in_specs=[pl.BlockSpec(memory_space=pltpu.SMEM), q_spec] + kv_specs + kv_specs + [q_spec],
        out_specs=[q_spec] + [kv_out] * 6 + [pl.BlockSpec((None, 8, HEAD_DIM), lambda h, n: (h, 0, 0))],
        out_shape=[jax.ShapeDtypeStruct((t, ATTN_WIDTH), F32)] + [kv_shape] * 6
                  + [jax.ShapeDtypeStruct((KV_HEADS, 8, HEAD_DIM), F32)],
        compiler_params=_params(("parallel", "arbitrary"), VMEM_LIMIT),
    )(sink, q, k, k, k, v, v, v, dmix)


def _attn_prep_bwd(proj, cos, sin_signed, gq, gk, dq, dks, dvs):
    t = proj.shape[0]
    tr = ATTN_BLOCK
    nb = t // tr

    def body(q_ref, k_ref, cos_ref, sin_ref, gq_ref, gk_ref, dq_ref,
             dk_lo, dk_mid, dk_hi, dv_lo, dv_mid, dv_hi,
             dqo_ref, dko_ref, dvo_ref, dgq_ref, dgk_ref):
        n = pl.program_id(0)
        cos_t, sin_t = cos_ref[...], sin_ref[...]
        has_next = (n < nb - 1).astype(F32)
        has_prev = (n > 0).astype(F32)
        dk = dk_lo[...] * has_next + dk_mid[...] + dk_hi[...] * has_prev
        dv = dv_lo[...] * has_next + dv_mid[...] + dv_hi[...] * has_prev
        dvo_ref[...] = dv.astype(BF16)
        dgq = jnp.zeros((1, HEAD_DIM), F32)
        dgk = jnp.zeros((1, HEAD_DIM), F32)
        for h in range(ATTN_HEADS):
            cols = slice(h * HEAD_DIM, (h + 1) * HEAD_DIM)
            _, vjp = jax.vjp(_rms, q_ref[:, cols], gq_ref[...])
            dx, dg = vjp(_rope_transposed(dq_ref[:, cols], cos_t, sin_t))
            dqo_ref[:, cols] = dx.astype(BF16)
            dgq = dgq + dg
        for h in range(KV_HEADS):
            cols = slice(h * HEAD_DIM, (h + 1) * HEAD_DIM)
            _, vjp = jax.vjp(_rms, k_ref[:, cols], gk_ref[...])
            dx, dg = vjp(_rope_transposed(dk[:, cols], cos_t, sin_t))
            dko_ref[:, cols] = dx.astype(BF16)
            dgk = dgk + dg

        @pl.when(n == 0)
        def _():
            dgq_ref[...] = jnp.zeros_like(dgq_ref)
            dgk_ref[...] = jnp.zeros_like(dgk_ref)

        dgq_ref[...] += dgq
        dgk_ref[...] += dgk

    qb, kb = _seg_block("qa", ATTN_WIDTH), _seg_block("ka", KV_WIDTH)
    tab = pl.BlockSpec((tr, HEAD_DIM), lambda i: (i, 0))
    vec = pl.BlockSpec((1, HEAD_DIM), lambda i: (0, 0))
    kv = [pl.BlockSpec((tr, KV_WIDTH), lambda i: (jnp.minimum(i + 1, nb - 1), 0)),
          pl.BlockSpec((tr, KV_WIDTH), lambda i: (i, 0)),
          pl.BlockSpec((tr, KV_WIDTH), lambda i: (jnp.maximum(i - 1, 0), 0))]
    wide = pl.BlockSpec((tr, ATTN_WIDTH), lambda i: (i, 0))
    narrow = pl.BlockSpec((tr, KV_WIDTH), lambda i: (i, 0))
    return pl.pallas_call(
        body, name="attn_prep_bwd", grid=(nb,),
        in_specs=[pl.BlockSpec((tr, ATTN_WIDTH), lambda i: (i, qb)),
                  pl.BlockSpec((tr, KV_WIDTH), lambda i: (i, kb)),
                  tab, tab, vec, vec, wide] + kv + kv,
        out_specs=[wide, narrow, narrow, vec, vec],
        out_shape=[jax.ShapeDtypeStruct((t, ATTN_WIDTH), BF16),
                   jax.ShapeDtypeStruct((t, KV_WIDTH), BF16),
                   jax.ShapeDtypeStruct((t, KV_WIDTH), BF16),
                   jax.ShapeDtypeStruct((1, HEAD_DIM), F32),
                   jax.ShapeDtypeStruct((1, HEAD_DIM), F32)],
        compiler_params=_params(("arbitrary",), VMEM_LIMIT),
    )(proj, proj, cos, sin_signed, gq, gk, dq, *dks, *dvs)


def _decay_fn(lr, w2, ba):
    return _log_sigmoid(bdot(lr, w2, "nn") + ba) / GLA_GATE_NORMALIZER


def _gla_prep_fwd(proj_lr, w2, ba2, *, tr=512):
    t = proj_lr.shape[0]
    width = 2 * GLA_KEY_WIDTH

    def body(lr_ref, w2_ref, ba_ref, g_ref):
        g_ref[...] = _decay_fn(lr_ref[...], w2_ref[...], ba_ref[...])

    return pl.pallas_call(
        body, name="gla_prep_fwd", grid=(t // tr,),
        in_specs=[pl.BlockSpec((tr, LR_PAD), lambda i: (i, 0)),
                  pl.BlockSpec((LR_PAD, width), lambda i: (0, 0)),
                  pl.BlockSpec((1, width), lambda i: (0, 0))],
        out_specs=pl.BlockSpec((tr, width), lambda i: (i, 0)),
        out_shape=jax.ShapeDtypeStruct((t, width), F32),
        compiler_params=_params(("parallel",), VMEM_LIMIT),
    )(proj_lr, w2, ba2)


def _gla_prep_bwd(proj_lr, w2, ba2, dg_f, dg_b, *, tr=512):
    t = proj_lr.shape[0]
    width = 2 * GLA_KEY_WIDTH

    def body(lr_ref, w2_ref, ba_ref, dgf_ref, dgb_ref, dlr_ref, dw2_ref, dba_ref):
        _, vjp = jax.vjp(_decay_fn, lr_ref[...], w2_ref[...], ba_ref[...])
        dlr, dw2, dba = vjp(jnp.concatenate([dgf_ref[...], dgb_ref[...]], axis=1))
        dlr_ref[...] = dlr.astype(BF16)

        @pl.when(pl.program_id(0) == 0)
        def _():
            dw2_ref[...] = jnp.zeros_like(dw2_ref)
            dba_ref[...] = jnp.zeros_like(dba_ref)

        dw2_ref[...] += dw2
        dba_ref[...] += dba

    half = pl.BlockSpec((tr, GLA_KEY_WIDTH), lambda i: (i, 0))
    return pl.pallas_call(
        body, name="gla_prep_bwd", grid=(t // tr,),
        in_specs=[pl.BlockSpec((tr, LR_PAD), lambda i: (i, 0)),
                  pl.BlockSpec((LR_PAD, width), lambda i: (0, 0)),
                  pl.BlockSpec((1, width), lambda i: (0, 0)), half, half],
        out_specs=[pl.BlockSpec((tr, LR_PAD), lambda i: (i, 0)),
                   pl.BlockSpec((LR_PAD, width), lambda i: (0, 0)),
                   pl.BlockSpec((1, width), lambda i: (0, 0))],
        out_shape=[jax.ShapeDtypeStruct((t, LR_PAD), BF16),
                   jax.ShapeDtypeStruct((LR_PAD, width), F32),
                   jax.ShapeDtypeStruct((1, width), F32)],
        compiler_params=_params(("arbitrary",), VMEM_LIMIT),
    )(proj_lr, w2, ba2, dg_f, dg_b)


def _gla_k(h):
    return slice(h * GLA_DK, (h + 1) * GLA_DK)


def _gla_v(h):
    return slice(h * GLA_DV, (h + 1) * GLA_DV)


def _running_sum(x, downward):
    n = x.shape[0]
    row = lax.broadcasted_iota(jnp.int32, x.shape, 0)
    step = 1
    while step < n:
        if downward:
            x = x + jnp.where(row >= step, pltpu.roll(x, step, 0), 0.0)
        else:
            x = x + jnp.where(row < n - step, pltpu.roll(x, n - step, 0), 0.0)
        step *= 2
    return x


@functools.partial(jax.custom_vjp, nondiff_argnums=(1,))
def _cumsum_rows(x, downward):
    return _running_sum(x, downward)


def _cumsum_rows_fwd(x, downward):
    return _running_sum(x, downward), None


def _cumsum_rows_bwd(downward, _, ct):
    return (_running_sum(ct, not downward),)


_cumsum_rows.defvjp(_cumsum_rows_fwd, _cumsum_rows_bwd)


def _gla_chunk(q, k, v, g, state, forward):
    c = GLA_CHUNK
    row = lax.broadcasted_iota(jnp.int32, (c, c), 0)
    col = lax.broadcasted_iota(jnp.int32, (c, c), 1)
    rid = lax.broadcasted_iota(jnp.int32, (c, GLA_DK), 0)
    q = q * (GLA_DK ** -0.5)
    if forward:
        see = row >= col
        upto_ref = rid <= c // 2
    else:
        see = row < col
        upto_ref = rid >= c - 1 - c // 2
    b = _cumsum_rows(g, forward)
    b_last = jnp.sum(g, axis=0, keepdims=True)
    b_ref = jnp.sum(jnp.where(upto_ref, g, 0.0), axis=0, keepdims=True)
    a = bdot(q * jnp.exp(b - b_ref), k * jnp.exp(b_ref - b), "nt")
    a = jnp.where(see, a, 0.0)
    o = bdot(a, v, "nn") + bdot(q * jnp.exp(b), state, "nt")
    new_state = state * jnp.exp(b_last) + bdot(v, k * jnp.exp(b_last - b), "tn")
    return o, new_state


def _gla_fwd(proj, g):
    t = proj.shape[0]
    c = GLA_CHUNK
    nchunk = t // c
    qb, kb, vb = _seg_block("qg", GLA_KEY_WIDTH), _seg_block("kg", GLA_KEY_WIDTH), _seg_block("vg", GLA_WIDTH)

    def body(qf, kf, vf, gf, qr, kr, vr, gr, of_ref, ob_ref, sf_ref, sb_ref, state):
        @pl.when(pl.program_id(0) == 0)
        def _():
            state[...] = jnp.zeros_like(state)

        dirs = ((qf, kf, vf, gf, of_ref, sf_ref), (qr, kr, vr, gr, ob_ref, sb_ref))
        args = [(q_ref[:, _gla_k(h)], k_ref[:, _gla_k(h)], v_ref[:, _gla_v(h)], g_ref[:, _gla_k(h)], state[d, h])
                for d, (q_ref, k_ref, v_ref, g_ref, _, _) in enumerate(dirs) for h in range(GLA_HEADS)]
        results = [_gla_chunk(*a, forward=(i < GLA_HEADS)) for i, a in enumerate(args)]
        for i, (a, (o, s_out)) in enumerate(zip(args, results)):
            d, h = divmod(i, GLA_HEADS)
            dirs[d][5][h] = a[4]
            dirs[d][4][:, _gla_v(h)] = o
            state[d, h] = s_out

    specs, outs = [], []
    for d in range(2):
        ci = (lambda i: i) if d == 0 else (lambda i: nchunk - 1 - i)
        specs += [pl.BlockSpec((c, GLA_KEY_WIDTH), lambda i, ci=ci: (ci(i), qb)),
                  pl.BlockSpec((c, GLA_KEY_WIDTH), lambda i, ci=ci: (ci(i), kb)),
                  pl.BlockSpec((c, GLA_WIDTH), lambda i, ci=ci: (ci(i), vb)),
                  pl.BlockSpec((c, GLA_KEY_WIDTH), lambda i, ci=ci, d=d: (ci(i), d))]
        outs.append(pl.BlockSpec((c, GLA_WIDTH), lambda i, ci=ci: (ci(i), 0)))
    for d in range(2):
        ci = (lambda i: i) if d == 0 else (lambda i: nchunk - 1 - i)
        outs.append(pl.BlockSpec((None, GLA_HEADS, GLA_DV, GLA_DK), lambda i, ci=ci: (ci(i), 0, 0, 0)))
    o_shape = jax.ShapeDtypeStruct((t, GLA_WIDTH), F32)
    s_shape = jax.ShapeDtypeStruct((nchunk, GLA_HEADS, GLA_DV, GLA_DK), F32)
    return pl.pallas_call(
        body, name="gla_fwd", grid=(nchunk,),
        in_specs=specs, out_specs=outs,
        out_shape=[o_shape, o_shape, s_shape, s_shape],
        scratch_shapes=[pltpu.VMEM((2, GLA_HEADS, GLA_DV, GLA_DK), F32)],
        compiler_params=_params(("arbitrary",), VMEM_LIMIT),
    )(proj, proj, proj, g, proj, proj, proj, g)


def _gla_bwd(proj, g, s_f, s_b, do):
    t = proj.shape[0]
    c = GLA_CHUNK
    nchunk = t // c
    qb, kb, vb = _seg_block("qg", GLA_KEY_WIDTH), _seg_block("kg", GLA_KEY_WIDTH), _seg_block("vg", GLA_WIDTH)

    def body(*refs):
        ins, outs, dstate = refs[:12], refs[12:20], refs[20]

        @pl.when(pl.program_id(0) == 0)
        def _():
            dstate[...] = jnp.zeros_like(dstate)

        loaded = []
        for d in range(2):
            q_ref, k_ref, v_ref, g_ref, s_ref, do_ref = ins[6 * d:6 * d + 6]
            for h in range(GLA_HEADS):
                loaded.append(((q_ref[:, _gla_k(h)], k_ref[:, _gla_k(h)], v_ref[:, _gla_v(h)], g_ref[:, _gla_k(h)],
                                s_ref[h]), (do_ref[:, _gla_v(h)], dstate[d, h])))
        grads = []
        for i, (primals, cotangents) in enumerate(loaded):
            _, vjp = jax.vjp(functools.partial(_gla_chunk, forward=(i < GLA_HEADS)), *primals)
            grads.append(vjp(cotangents))
        for i, (dq, dk, dv, dg, ds) in enumerate(grads):
            d, h = divmod(i, GLA_HEADS)
            dq_ref, dk_ref, dv_ref, dg_ref = outs[4 * d:4 * d + 4]
            dq_ref[:, _gla_k(h)] = dq
            dk_ref[:, _gla_k(h)] = dk
            dv_ref[:, _gla_v(h)] = dv
            dg_ref[:, _gla_k(h)] = dg
            dstate[d, h] = ds

    specs, outs, shapes = [], [], []
    for d in range(2):
        ci = (lambda i: nchunk - 1 - i) if d == 0 else (lambda i: i)
        specs += [pl.BlockSpec((c, GLA_KEY_WIDTH), lambda i, ci=ci: (ci(i), qb)),
                  pl.BlockSpec((c, GLA_KEY_WIDTH), lambda i, ci=ci: (ci(i), kb)),
                  pl.BlockSpec((c, GLA_WIDTH), lambda i, ci=ci: (ci(i), vb)),
                  pl.BlockSpec((c, GLA_KEY_WIDTH), lambda i, ci=ci, d=d: (ci(i), d)),
                  pl.BlockSpec((None, GLA_HEADS, GLA_DV, GLA_DK), lambda i, ci=ci: (ci(i), 0, 0, 0)),
                  pl.BlockSpec((c, GLA_WIDTH), lambda i, ci=ci: (ci(i), 0))]
        key = pl.BlockSpec((c, GLA_KEY_WIDTH), lambda i, ci=ci: (ci(i), 0))
        val = pl.BlockSpec((c, GLA_WIDTH), lambda i, ci=ci: (ci(i), 0))
        outs += [key, key, val, key]
        shapes += [jax.ShapeDtypeStruct((t, GLA_KEY_WIDTH), F32), jax.ShapeDtypeStruct((t, GLA_KEY_WIDTH), F32),
                   jax.ShapeDtypeStruct((t, GLA_WIDTH), F32), jax.ShapeDtypeStruct((t, GLA_KEY_WIDTH), F32)]
    return pl.pallas_call(
        body, name="gla_bwd", grid=(nchunk,),
        in_specs=specs, out_specs=outs, out_shape=shapes,
        scratch_shapes=[pltpu.VMEM((2, GLA_HEADS, GLA_DV, GLA_DK), F32)],
        compiler_params=_params(("arbitrary",), VMEM_LIMIT),
    )(proj, proj, proj, g, s_f, do, proj, proj, proj, g, s_b, do)


def _gla_out_head(o_f, o_b, gate, gn):
    return _rms(o_f + o_b, gn) * _silu(gate)


def _gla_out_fwd(o_f, o_b, proj, gn, *, tr=256):
    t = o_f.shape[0]
    gb = _seg_block("gate", GLA_WIDTH)

    def body(of_ref, ob_ref, gate_ref, gn_ref, out_ref):
        for h in range(GLA_HEADS):
            vc = slice(h * GLA_DV, (h + 1) * GLA_DV)
            out_ref[:, vc] = _gla_out_head(of_ref[:, vc], ob_ref[:, vc], gate_ref[:, vc], gn_ref[...]).astype(BF16)

    wide = pl.BlockSpec((tr, GLA_WIDTH), lambda i: (i, 0))
    return pl.pallas_call(
        body, name="gla_out_fwd", grid=(t // tr,),
        in_specs=[wide, wide, pl.BlockSpec((tr, GLA_WIDTH), lambda i: (i, gb)),
                  pl.BlockSpec((1, GLA_DV), lambda i: (0, 0))],
        out_specs=wide,
        out_shape=jax.ShapeDtypeStruct((t, GLA_WIDTH), BF16),
        compiler_params=_params(("parallel",), VMEM_LIMIT),
    )(o_f, o_b, proj, gn)


def _gla_out_bwd(o_f, o_b, proj, gn, dmix, *, tr=256):
    t = o_f.shape[0]
    gb = _seg_block("gate", GLA_WIDTH)

    def body(of_ref, ob_ref, gate_ref, gn_ref, dout_ref, do_ref, dgate_ref, dgn_ref):
        dgn = jnp.zeros((1, GLA_DV), F32)
        for h in range(GLA_HEADS):
            vc = slice(h * GLA_DV, (h + 1) * GLA_DV)
            _, vjp = jax.vjp(_gla_out_head, of_ref[:, vc], ob_ref[:, vc], gate_ref[:, vc], gn_ref[...])
            do, _, dgate, dg = vjp(dout_ref[:, vc])
            do_ref[:, vc] = do
            dgate_ref[:, vc] = dgate.astype(BF16)
            dgn = dgn + dg

        @pl.when(pl.program_id(0) == 0)
        def _():
            dgn_ref[...] = jnp.zeros_like(dgn_ref)

        dgn_ref[...] += dgn

    wide = pl.BlockSpec((tr, GLA_WIDTH), lambda i: (i, 0))
    vec = pl.BlockSpec((1, GLA_DV), lambda i: (0, 0))
    return pl.pallas_call(
        body, name="gla_out_bwd", grid=(t // tr,),
        in_specs=[wide, wide, pl.BlockSpec((tr, GLA_WIDTH), lambda i: (i, gb)), vec,
                  pl.BlockSpec((tr, GLA_WIDTH), lambda i: (i, 1))],
        out_specs=[wide, wide, vec],
        out_shape=[jax.ShapeDtypeStruct((t, GLA_WIDTH), F32), jax.ShapeDtypeStruct((t, GLA_WIDTH), BF16),
                   jax.ShapeDtypeStruct((1, GLA_DV), F32)],
        compiler_params=_params(("arbitrary",), VMEM_LIMIT),
    )(o_f, o_b, proj, gn, dmix)


CONV_TR = 1024
CONV_TC = 512
HALO = 8
HALO16 = 16


def _conv3(u, w, b):
    n = u.shape[0]
    return pltpu.roll(u, 1, 0) * w[0:1] + u * w[1:2] + pltpu.roll(u, n - 1, 0) * w[2:3] + b


def _conv_ext(main_ref, prev_ref, next_ref, r, nr):
    prev = prev_ref[...].astype(F32)[-HALO:] * (r > 0).astype(F32)
    nxt = next_ref[...].astype(F32)[:HALO] * (r < nr - 1).astype(F32)
    return jnp.concatenate([prev, main_ref[...].astype(F32), nxt], axis=0)


def _conv_specs(t, halo, half=None):
    per = CONV_TR // halo
    last = t // halo - 1
    lead = () if half is None else (None,)
    at = (lambda *ix: ix) if half is None else (lambda *ix: (half,) + ix)
    return [pl.BlockSpec(lead + (CONV_TR, CONV_TC), lambda j, r: at(r, j)),
            pl.BlockSpec(lead + (halo, CONV_TC), lambda j, r: at(jnp.maximum(r * per - 1, 0), j)),
            pl.BlockSpec(lead + (halo, CONV_TC), lambda j, r: at(jnp.minimum((r + 1) * per, last), j))]


def _ffn_mid_fwd(u, cw_g, cw_v, cb_g, cb_v):
    _, t, f = u.shape
    nr = t // CONV_TR

    def body(ug, ugp, ugn, uv, uvp, uvn, wg, wv, bg, bv, a_ref):
        r = pl.program_id(1)
        gate = _conv3(_conv_ext(ug, ugp, ugn, r, nr), wg[...], bg[...])[HALO:HALO + CONV_TR]
        val = _conv3(_conv_ext(uv, uvp, uvn, r, nr), wv[...], bv[...])[HALO:HALO + CONV_TR]
        a_ref[...] = (_silu(gate) * val).astype(BF16)

    w_spec = pl.BlockSpec((3, CONV_TC), lambda j, r: (0, j))
    b_spec = pl.BlockSpec((1, CONV_TC), lambda j, r: (0, j))
    return pl.pallas_call(
        body, name="ffn_mid_fwd", grid=(f // CONV_TC, nr),
        in_specs=_conv_specs(t, HALO, 0) + _conv_specs(t, HALO, 1) + [w_spec, w_spec, b_spec, b_spec],
        out_specs=pl.BlockSpec((CONV_TR, CONV_TC), lambda j, r: (r, j)),
        out_shape=jax.ShapeDtypeStruct((t, f), BF16),
        compiler_params=_params(("parallel", "parallel"), VMEM_LIMIT),
    )(u, u, u, u, u, u, cw_g, cw_v, cb_g, cb_v)


def _ffn_mid_bwd(u, cw_g, cw_v, cb_g, cb_v, da):
    _, t, f = u.shape
    nr = t // CONV_TR
    ext = CONV_TR + 2 * HALO

    def body(ug, ugp, ugn, uv, uvp, uvn, dam, dap, dan, wg, wv, bg, bv,
             du_ref, dwg_ref, dwv_ref, dbg_ref, dbv_ref):
        r = pl.program_id(1)
        xg = _conv_ext(ug, ugp, ugn, r, nr)
        xv = _conv_ext(uv, uvp, uvn, r, nr)
        da_x = _conv_ext(dam, dap, dan, r, nr)
        gate = _conv3(xg, wg[...], bg[...])
        val = _conv3(xv, wv[...], bv[...])
        sig = jax.nn.sigmoid(gate)
        silu = gate * sig
        d_val = da_x * silu
        d_gate = da_x * val * (sig + silu * (1.0 - sig))
        own = slice(HALO, HALO + CONV_TR)
        for half, (x, d, w, dw_ref, db_ref) in enumerate(((xg, d_gate, wg, dwg_ref, dbg_ref),
                                                          (xv, d_val, wv, dwv_ref, dbv_ref))):
            wt = w[...]
            du = pltpu.roll(d, ext - 1, 0) * wt[0:1] + d * wt[1:2] + pltpu.roll(d, 1, 0) * wt[2:3]
            du_ref[half] = du[own].astype(BF16)
            d_own = d[own]
            dw = jnp.concatenate([
                jnp.sum(pltpu.roll(x, 1, 0)[own] * d_own, axis=0, keepdims=True),
                jnp.sum(x[own] * d_own, axis=0, keepdims=True),
                jnp.sum(pltpu.roll(x, ext - 1, 0)[own] * d_own, axis=0, keepdims=True)], axis=0)
            db = jnp.sum(d_own, axis=0, keepdims=True)

            @pl.when(r == 0)
            def _():
                dw_ref[...] = jnp.zeros_like(dw_ref)
                db_ref[...] = jnp.zeros_like(db_ref)

            dw_ref[...] += dw
            db_ref[...] += db

    w_spec = pl.BlockSpec((3, CONV_TC), lambda j, r: (0, j))
    b_spec = pl.BlockSpec((1, CONV_TC), lambda j, r: (0, j))
    return pl.pallas_call(
        body, name="ffn_mid_bwd", grid=(f // CONV_TC, nr),
        in_specs=(_conv_specs(t, HALO, 0) + _conv_specs(t, HALO, 1) + _conv_specs(t, HALO16)
                  + [w_spec, w_spec, b_spec, b_spec]),
        out_specs=[pl.BlockSpec((2, CONV_TR, CONV_TC), lambda j, r: (0, r, j)), w_spec, w_spec, b_spec, b_spec],
        out_shape=[jax.ShapeDtypeStruct((2, t, f), BF16),
                   jax.ShapeDtypeStruct((3, f), F32), jax.ShapeDtypeStruct((3, f), F32),
                   jax.ShapeDtypeStruct((1, f), F32), jax.ShapeDtypeStruct((1, f), F32)],
        compiler_params=_params(("parallel", "arbitrary"), VMEM_LIMIT),
    )(u, u, u, u, u, u, da, da, da, cw_g, cw_v, cb_g, cb_v)


def _loss_head(y, target, *, tr=256):
    t, d = y.shape

    def body(y_ref, t_ref, loss_ref, dy_ref, dyb_ref):
        err = y_ref[...] - t_ref[...]
        dy = err * (1.0 / d)
        dy_ref[...] = dy
        dyb_ref[...] = dy.astype(BF16)
        part = 0.5 * jnp.sum(jnp.sum(err * err, axis=-1, keepdims=True) * (1.0 / d), axis=0, keepdims=True)

        @pl.when(pl.program_id(0) == 0)
        def _():
            loss_ref[...] = jnp.zeros_like(loss_ref)

        loss_ref[...] += jnp.broadcast_to(part, loss_ref.shape)

    row = pl.BlockSpec((tr, d), lambda i: (i, 0))
    return pl.pallas_call(
        body, name="loss_head", grid=(t // tr,),
        in_specs=[row, row],
        out_specs=[pl.BlockSpec((1, 128), lambda i: (0, 0)), row, row],
        out_shape=[jax.ShapeDtypeStruct((1, 128), F32), jax.ShapeDtypeStruct((t, d), F32),
                   jax.ShapeDtypeStruct((t, d), BF16)],
        compiler_params=_params(("arbitrary",), VMEM_LIMIT),
    )(y, target)


ANY = pl.BlockSpec(memory_space=pl.ANY)


def _position():
    return lax.axis_index("x"), lax.axis_index("y"), lax.axis_index("c")


def _other_chips(x, y):
    return [(1 - x, y), (x, 1 - y), (1 - x, 1 - y)]


def _handshake(peers):
    barrier = pltpu.get_barrier_semaphore()
    for peer in peers:
        pl.semaphore_signal(barrier, inc=1, device_id=peer, device_id_type=MESH)
    pl.semaphore_wait(barrier, len(peers))


def _exchange(body, operands, out_shapes, sems, *, name, collective_id):
    n_in, n_out = len(operands), len(out_shapes)

    def run(*refs):
        body(refs[:n_in], refs[n_in:n_in + n_out], *refs[n_in + n_out:])

    if collective_id is None:
        return pl.pallas_call(run, name=name, in_specs=[ANY] * n_in, out_specs=[ANY] * n_out,
                              out_shape=out_shapes, scratch_shapes=sems)(*operands)
    return pl.kernel(run, name=name, out_type=out_shapes,
                     mesh=plsc.ScalarSubcoreMesh(axis_name="sequencer", num_cores=1), scratch_types=sems,
                     compiler_params=pltpu.CompilerParams(collective_id=collective_id))(*operands)


def _all_gather(blocks, *, name, collective_id=None):
    na = len(blocks)

    def body(ins, outs, send_sems, recv_sems, local_sems):
        x, y, c = _position()
        sibling = (x, y, 1 - c)
        chips = _other_chips(x, y)
        if collective_id is not None:
            _handshake([sibling] + [(*chip, c) for chip in chips])

        def index(px, py, pc):
            return 4 * px + 2 * py + pc

        def copy(a, k, block, to, src=None):
            dst = outs[a].at[index(*block)]
            return pltpu.make_async_remote_copy(
                src_ref=dst if src is None else src, dst_ref=dst,
                send_sem=send_sems.at[a, k], recv_sem=recv_sems.at[a, k],
                device_id=to, device_id_type=MESH)

        pending = []
        for a in range(na):
            mine = pltpu.make_async_copy(ins[a], outs[a].at[index(x, y, c)], local_sems.at[a])
            mine.start()
            pending.append(mine)
        first = []
        for a in range(na):
            first.append(copy(a, 0, (x, y, c), sibling, src=ins[a]))
            first += [copy(a, 1 + j, (x, y, c), (*chip, c), src=ins[a]) for j, chip in enumerate(chips)]
        for cp in first:
            cp.start()
        passed = []
        for j, chip in enumerate(chips):
            for a in range(na):
                copy(a, 1 + j, (*chip, c), (x, y, c)).wait_recv()
                fwd = copy(a, 4 + j, (*chip, c), sibling)
                fwd.start()
                passed.append(fwd)
        for a in range(na):
            copy(a, 0, sibling, (x, y, c)).wait_recv()
            for j, chip in enumerate(chips):
                copy(a, 4 + j, (*chip, 1 - c), (x, y, c)).wait_recv()
        for cp in first + passed:
            cp.wait_send()
        for cp in pending:
            cp.wait()

    return _exchange(
        body, blocks, [jax.ShapeDtypeStruct((N_DEV,) + b.shape, b.dtype) for b in blocks],
        [pltpu.SemaphoreType.DMA((na, 7)), pltpu.SemaphoreType.DMA((na, 7)), pltpu.SemaphoreType.DMA((na,))],
        name=name, collective_id=collective_id)


def _grad_exchange(grads, parts, *, name, collective_id):
    ng, npart = len(grads), len(parts)

    def body(ins, outs, core_send, core_recv, chip_send, chip_recv, local_sems):
        x, y, c = _position()
        sibling = (x, y, 1 - c)
        chips = _other_chips(x, y)
        _handshake([sibling] + [(px, py, c) for px, py in chips])
        me = 2 * x + y
        copies = []
        for b in range(npart):
            src, dst = ins[ng + b], outs[ng + b]
            own = pltpu.make_async_copy(src.at[me], dst.at[me], local_sems.at[b])
            own.start()
            copies.append(own)
            for j, (px, py) in enumerate(chips):
                cp = pltpu.make_async_remote_copy(
                    src_ref=src.at[2 * px + py], dst_ref=dst.at[me],
                    send_sem=chip_send.at[b, j], recv_sem=chip_recv.at[b, j],
                    device_id=(px, py, c), device_id_type=MESH)
                cp.start()
                copies.append(cp)
        for a in range(ng):
            for k in range(N_CHIP):
                cp = pltpu.make_async_remote_copy(
                    src_ref=ins[a].at[k, 1 - c], dst_ref=outs[a].at[k],
                    send_sem=core_send.at[a, k], recv_sem=core_recv.at[a, k],
                    device_id=sibling, device_id_type=MESH)
                cp.start()
                copies.append(cp)
        for cp in copies:
            cp.wait()

    shapes = ([jax.ShapeDtypeStruct((N_CHIP,) + g.shape[2:], g.dtype) for g in grads]
              + [jax.ShapeDtypeStruct(p.shape, p.dtype) for p in parts])
    sems = [pltpu.SemaphoreType.DMA((max(ng, 1), N_CHIP)), pltpu.SemaphoreType.DMA((max(ng, 1), N_CHIP)),
            pltpu.SemaphoreType.DMA((max(npart, 1), 3)), pltpu.SemaphoreType.DMA((max(npart, 1), 3)),
            pltpu.SemaphoreType.DMA((max(npart, 1),))]
    out = _exchange(body, list(grads) + list(parts), shapes, sems, name=name, collective_id=collective_id)
    return out[:ng], out[ng:]


def _pair_sum(grad, theirs, core, *, tile, name):
    _, _, r, w = grad.shape
    tr, tw = tile
    assert r % tr == 0 and w % tw == 0

    def body(core_ref, mine_ref, theirs_ref, out_ref):
        out_ref[...] = mine_ref[...] + theirs_ref[...]

    return pl.pallas_call(
        body, name=name,
        grid_spec=pltpu.PrefetchScalarGridSpec(
            num_scalar_prefetch=1, grid=(N_CHIP, r // tr, w // tw),
            in_specs=[pl.BlockSpec((None, None, tr, tw), lambda k, i, j, core_ref: (k, core_ref[0], i, j)),
                      pl.BlockSpec((None, tr, tw), lambda k, i, j, core_ref: (k, i, j))],
            out_specs=pl.BlockSpec((None, tr, tw), lambda k, i, j, core_ref: (k, i, j))),
        out_shape=jax.ShapeDtypeStruct((N_CHIP, r, w), F32),
        compiler_params=_params(("parallel", "parallel", "parallel"), VMEM_LIMIT),
    )(core, grad, theirs)


class _ReduceScatter:
    def __init__(self, core):
        self.core = core
        self.pending = None
        self.results = {}
        self.launches = 0

    def push(self, tag, grads, rows, then):
        pair, prev_tag = [], None
        if self.pending is not None:
            prev_tag, prev, theirs, prev_rows = self.pending
            pair = [_pair_sum(g, s, self.core, tile=tile, name=f"pair_sum_{prev_tag}_{i}")
                    for i, (g, s, tile) in enumerate(zip(prev, theirs, prev_rows))]
        grads, pair, then = lax.optimization_barrier((list(grads), pair, then))
        grads = [g.reshape((N_CHIP, 2) + g.shape[1:]) for g in grads]
        self.launches += 1
        theirs, parts = _grad_exchange(grads, pair, name=f"grad_exchange_{self.launches}",
                                       collective_id=1 + self.launches)
        if prev_tag is not None:
            self.results[prev_tag] = parts
        self.pending = (tag, grads, theirs, rows) if tag is not None else None
        return then

    def result(self, tag):
        return self.results[tag]


def _adamw(parts, w, m, v, *, tile, name):
    n, r, cols = parts.shape
    tr, tw = tile
    assert r % tr == 0 and cols % tw == 0 and w.shape == (r, cols)
    c1 = 1.0 - ADAM_B1 ** ADAM_STEP
    c2 = 1.0 - ADAM_B2 ** ADAM_STEP

    def body(p_ref, w_ref, m_ref, v_ref, g_ref, d_ref, nm_ref, nv_ref):
        g = p_ref[0]
        for k in range(1, n):
            g = g + p_ref[k]
        new_m = ADAM_B1 * m_ref[...] + (1.0 - ADAM_B1) * g
        new_v = ADAM_B2 * v_ref[...] + (1.0 - ADAM_B2) * (g * g)
        m_hat = new_m / c1
        v_hat = new_v / c2
        g_ref[...] = g
        d_ref[...] = -ADAM_LR * (m_hat / (jnp.sqrt(v_hat) + ADAM_EPS) + ADAM_WD * w_ref[...])
        nm_ref[...] = new_m
        nv_ref[...] = new_v

    spec = pl.BlockSpec((tr, tw), lambda i, j: (i, j))
    shape = jax.ShapeDtypeStruct((r, cols), F32)
    return pl.pallas_call(
        body, name=name, grid=(r // tr, cols // tw),
        in_specs=[pl.BlockSpec((n, tr, tw), lambda i, j: (0, i, j)), spec, spec, spec],
        out_specs=[spec] * 4, out_shape=[shape] * 4,
        compiler_params=_params(("parallel", "parallel"), VMEM_LIMIT),
    )(parts, w, m, v)


def _rope_tables(t):
    half = HEAD_DIM // 2
    inv = 1.0 / (ROPE_THETA ** (jnp.arange(half, dtype=F32) / half))
    ang = jnp.arange(t, dtype=jnp.int32).astype(F32)[:, None] * inv[None, :]
    cos, sin = jnp.cos(ang), jnp.sin(ang)
    return jnp.concatenate([cos, cos], axis=1), jnp.concatenate([-sin, sin], axis=1)


IN_KERNEL = IN_MAIN + LR_PAD


def _to_kernel_rows(w_t):
    order = sorted(SEGMENTS.values())
    pad = jnp.zeros((LR_PAD - (IN_TOTAL - IN_MAIN), w_t.shape[1]), w_t.dtype)
    return jnp.concatenate([w_t[src:src + width] for _, src, width in order] + [w_t[IN_MAIN:IN_TOTAL], pad], axis=0)


def _pack(pieces):
    flat = []
    for p in pieces:
        p = p.reshape(-1)
        flat.append(jnp.pad(p, (0, (-p.shape[0]) % 128)))
    return jnp.concatenate(flat).reshape(-1, 128)


def _unpack(packed, shapes):
    flat = packed.reshape(-1)
    out, off = [], 0
    for s in shapes:
        size = 1
        for dim in s:
            size *= dim
        out.append(flat[off:off + size].reshape(s))
        off += size + (-size) % 128
    return out


def _local_step(xs, target, norm1_g, w_in_k, gq, gk, attn_sink, w2, ba2, gla_norm_g, w_out_full, norm2_g,
                w_up8, cw_g, cw_v, cb_g, cb_v, w_down_full, rs=None):
    t = xs.shape[0]
    tm = min(1024, t)
    tall = min(2048, t)
    cos, sin_signed = _rope_tables(t)
    sink = attn_sink.reshape(ATTN_HEADS)

    h1 = _rmsnorm_fwd(xs, norm1_g, name="norm1_fwd")
    proj = _matmul(h1, w_in_k, "nt", tm=tall, tn=IN_MAIN // 4, tk=D_MODEL, n_out=IN_MAIN, name="proj_main")
    proj_lr = _matmul(h1, w_in_k[IN_MAIN:], "nt", tm=tm, tn=LR_PAD, tk=D_MODEL, name="proj_lr")
    qa, ka, va = _attn_prep_fwd(proj, cos, sin_signed, gq, gk)
    o_attn = _attn_fwd(qa, ka, va, sink)
    g_dec = _gla_prep_fwd(proj_lr, w2, ba2)
    o_f, o_b, s_f, s_b = _gla_fwd(proj, g_dec)
    o_gla = _gla_out_fwd(o_f, o_b, proj, gla_norm_g)
    x1 = _out_proj(o_attn, o_gla, w_out_full, xs, tm=tall, tn=512)
    h2 = _rmsnorm_fwd(x1, norm2_g, name="norm2_fwd")
    u = _up_proj(h2, w_up8, tm=tm)
    act = _ffn_mid_fwd(u, cw_g, cw_v, cb_g, cb_v)
    y = _matmul(act, w_down_full, "nn", tm=tm, tn=512, tk=D_FF, res=x1, name="down_proj")
    loss_part, dy, dy_b = _loss_head(y, target)

    d_act = _matmul(dy_b, w_down_full, "nt", tm=tall, tn=D_FF // 4, tk=D_MODEL, out_dtype=BF16, name="d_act")
    dw_down = _matmul(act, dy_b, "tn", tm=D_FF // 4, tn=512, tk=t, name="dw_down")
    if rs is not None:
        d_act = rs.push("w_down", [dw_down.reshape(N_DEV, D_FF // N_DEV, D_MODEL)], [(64, D_MODEL)], d_act)
    du, dcw_g, dcw_v, dcb_g, dcb_v = _ffn_mid_bwd(u, cw_g, cw_v, cb_g, cb_v, d_act)
    dw_up8 = _up_proj_dw(h2, du, tm=512, tk=t)
    if rs is not None:
        du = rs.push("w_up", [dw_up8], [(256, UP_BLOCK)], du)
    dh2 = _up_proj_dx(du, w_up8, tm=tm, tn=1024)
    dx1, dx1_b, d_norm2 = _rmsnorm_bwd(x1, norm2_g, dh2, dy, name="norm2_bwd")
    dmix = _matmul(dx1_b, w_out_full, "nt", tm=tall, tn=1024, tk=D_MODEL, name="d_mix")
    dw_out = _out_proj_dw(o_attn, o_gla, dx1_b, tn=512)
    if rs is not None:
        dmix = rs.push("w_out", [dw_out.reshape(N_DEV, D_MODEL // N_DEV, D_MODEL)], [(256, D_MODEL)], dmix)
    do_gla, d_gate, d_gla_norm = _gla_out_bwd(o_f, o_b, proj, gla_norm_g, dmix)
    (dq_f, dk_f, dv_f, dg_f, dq_b, dk_b, dv_b, dg_b) = _gla_bwd(proj, g_dec, s_f, s_b, do_gla)
    d_lr, d_w2, d_ba2 = _gla_prep_bwd(proj_lr, w2, ba2, dg_f, dg_b)
    dqa, dk_lo, dk_mid, dk_hi, dv_lo, dv_mid, dv_hi, d_sink8 = _attn_bwd(qa, ka, va, sink, dmix)
    d_qa, d_ka, d_va, d_qn, d_kn = _attn_prep_bwd(proj, cos, sin_signed, gq, gk, dqa,
                                                  (dk_lo, dk_mid, dk_hi), (dv_lo, dv_mid, dv_hi))
    d_seg = {"qa": d_qa, "gate": d_gate, "vg": (dv_f + dv_b).astype(BF16), "qg": (dq_f + dq_b).astype(BF16),
             "kg": (dk_f + dk_b).astype(BF16), "ka": d_ka, "va": d_va}
    d_proj = jnp.concatenate([d_seg[k] for k in sorted(SEGMENTS, key=lambda k: SEGMENTS[k][0])] + [d_lr], axis=1)
    dw_in_t = _in_proj_dw_lr(_in_proj_dw(d_proj, h1, tn=1024), d_lr, h1)
    if rs is not None:
        per_in, per_wa = IN_TOTAL // N_DEV, GLA_KEY_WIDTH // N_DEV
        dconv_w = jnp.concatenate([dcw_g, dcw_v], axis=1)
        d_wa2_f = d_w2[:GLA_RANK, :GLA_KEY_WIDTH]
        d_wa2_b = d_w2[GLA_RANK:2 * GLA_RANK, GLA_KEY_WIDTH:]
        small_grad = jnp.stack([
            _pack([dconv_w[:, d * UP_BLOCK:(d + 1) * UP_BLOCK], d_wa2_f[:, d * per_wa:(d + 1) * per_wa],
                   d_wa2_b[:, d * per_wa:(d + 1) * per_wa]]) for d in range(N_DEV)])
        d_proj, d_lr = rs.push("w_in", [dw_in_t.reshape(N_DEV, per_in, D_MODEL), small_grad],
                               [(per_in, 512), small_grad.shape[1:]], (d_proj, d_lr))
    dh1 = _matmul(d_proj, w_in_k, "nn", tm=tm, tn=512, tk=IN_KERNEL, name="dh1")
    if rs is not None:
        dh1 = rs.push(None, [], [], dh1)
    grad_x, _, d_norm1 = _rmsnorm_bwd(xs, norm1_g, dh1, dx1, name="norm1_bwd")
    return (loss_part, grad_x, dw_in_t, dw_out, dw_up8, dw_down, dcw_g, dcw_v, dcb_g, dcb_v,
            d_w2, d_ba2, d_norm1, d_norm2, d_qn, d_kn, d_sink8, d_gla_norm)


def kernel(x, norm1_g, w_in, attn_q_norm_g, attn_k_norm_g, attn_sink, gla_wa2_fwd, gla_ba_fwd, gla_wa2_bwd, gla_ba_bwd, gla_out_norm_g, w_out, norm2_g, w_up, conv_w, conv_b, w_down, loss_target, m_norm1_g, m_w_in, m_attn_q_norm_g, m_attn_k_norm_g, m_attn_sink, m_gla_wa2_fwd, m_gla_ba_fwd, m_gla_wa2_bwd, m_gla_ba_bwd, m_gla_out_norm_g, m_w_out, m_norm2_g, m_w_up, m_conv_w, m_conv_b, m_w_down, v_norm1_g, v_w_in, v_attn_q_norm_g, v_attn_k_norm_g, v_attn_sink, v_gla_wa2_fwd, v_gla_ba_fwd, v_gla_wa2_bwd, v_gla_ba_bwd, v_gla_out_norm_g, v_w_out, v_norm2_g, v_w_up, v_conv_w, v_conv_b, v_w_down):
    t = x.shape[1]
    xs = x.reshape(t, D_MODEL)
    target = loss_target.reshape(t, D_MODEL)
    core = lax.axis_index("c").astype(jnp.int32).reshape(1)

    sharded_small = [conv_w[0], gla_wa2_fwd[0], gla_wa2_bwd[0]]
    small_shapes = [s.shape for s in sharded_small]
    w_in_t, m_in_t, v_in_t = (jnp.swapaxes(a[0], 0, 1) for a in (w_in, m_w_in, v_w_in))
    g_in, g_small = _all_gather([w_in_t.astype(BF16), _pack(sharded_small)], name="gather_w_in")
    g_in, later = lax.optimization_barrier(
        (g_in, [w_out[0].astype(BF16), w_up[0].astype(BF16), w_down[0].astype(BF16)]))
    g_out, w_up8, g_down = _all_gather(later, name="gather_later_weights", collective_id=1)
    w_in_k = _to_kernel_rows(g_in.reshape(IN_TOTAL, D_MODEL))
    w_out_full = g_out.reshape(D_MODEL, D_MODEL)
    w_down_full = g_down.reshape(D_FF, D_MODEL)
    small_full = [_unpack(g_small[d], small_shapes) for d in range(N_DEV)]
    conv_w_full = jnp.concatenate([s[0] for s in small_full], axis=1)
    wa2_f = jnp.concatenate([s[1] for s in small_full], axis=1)
    wa2_b = jnp.concatenate([s[2] for s in small_full], axis=1)
    cw_g, cw_v = conv_w_full[:, :D_FF], conv_w_full[:, D_FF:]
    cb_g, cb_v = conv_b[:, :D_FF], conv_b[:, D_FF:]
    w2 = jnp.zeros((LR_PAD, 2 * GLA_KEY_WIDTH), F32)
    w2 = w2.at[:GLA_RANK, :GLA_KEY_WIDTH].set(wa2_f).at[GLA_RANK:2 * GLA_RANK, GLA_KEY_WIDTH:].set(wa2_b)
    ba2 = jnp.concatenate([gla_ba_fwd, gla_ba_bwd], axis=1)
    rs = _ReduceScatter(core)
    (loss_part, grad_x, _, _, _, _, _, _, dcb_g, dcb_v, _, d_ba2,
     d_norm1, d_norm2, d_qn, d_kn, d_sink8, d_gla_norm) = _local_step(
        xs, target, norm1_g, w_in_k, attn_q_norm_g, attn_k_norm_g, attn_sink, w2, ba2, gla_out_norm_g,
        w_out_full, norm2_g, w_up8, cw_g, cw_v, cb_g, cb_v, w_down_full, rs=rs)

    (part_down,), (part_up,), (part_out,) = rs.result("w_down"), rs.result("w_up"), rs.result("w_out")
    part_in, part_small = rs.result("w_in")
    m_small = _pack([m_conv_w[0], m_gla_wa2_fwd[0], m_gla_wa2_bwd[0]])
    v_small = _pack([v_conv_w[0], v_gla_wa2_fwd[0], v_gla_wa2_bwd[0]])
    upd_in = _adamw(part_in, w_in_t, m_in_t, v_in_t, tile=(IN_TOTAL // N_DEV, 512), name="adamw_w_in")
    upd_in = [jnp.swapaxes(u, 0, 1) for u in upd_in]
    upd_out = _adamw(part_out, w_out[0], m_w_out[0], v_w_out[0], tile=(256, D_MODEL), name="adamw_w_out")
    upd_up = _adamw(part_up, w_up[0], m_w_up[0], v_w_up[0], tile=(256, UP_BLOCK), name="adamw_w_up")
    upd_down = _adamw(part_down, w_down[0], m_w_down[0], v_w_down[0], tile=(64, D_MODEL), name="adamw_w_down")
    upd_small = _adamw(part_small, _pack(sharded_small), m_small, v_small, tile=part_small.shape[1:],
                       name="adamw_small")
    upd_small = [_unpack(u, small_shapes) for u in upd_small]

    rep_names = ["norm1_g", "attn_q_norm_g", "attn_k_norm_g", "attn_sink", "gla_ba_fwd", "gla_ba_bwd",
                 "gla_out_norm_g", "norm2_g", "conv_b"]
    rep_w = [norm1_g, attn_q_norm_g, attn_k_norm_g, attn_sink, gla_ba_fwd, gla_ba_bwd, gla_out_norm_g, norm2_g, conv_b]
    rep_m = [m_norm1_g, m_attn_q_norm_g, m_attn_k_norm_g, m_attn_sink, m_gla_ba_fwd, m_gla_ba_bwd,
             m_gla_out_norm_g, m_norm2_g, m_conv_b]
    rep_v = [v_norm1_g, v_attn_q_norm_g, v_attn_k_norm_g, v_attn_sink, v_gla_ba_fwd, v_gla_ba_bwd,
             v_gla_out_norm_g, v_norm2_g, v_conv_b]
    d_sink = d_sink8[:, :GQA_GROUP, 0].reshape(1, ATTN_HEADS)
    rep_g = [d_norm1, d_qn, d_kn, d_sink, d_ba2[:, :GLA_KEY_WIDTH], d_ba2[:, GLA_KEY_WIDTH:], d_gla_norm, d_norm2,
             jnp.concatenate([dcb_g, dcb_v], axis=1)]
    ride = jnp.ones((1, 128), F32)
    rep_shapes = [w.shape for w in rep_w] + [loss_part.shape]
    (rep_terms,) = _all_gather([_pack(rep_g + [loss_part])], name="gather_small_grads")
    upd_rep = _adamw(rep_terms, _pack(rep_w + [ride]), _pack(rep_m + [ride]), _pack(rep_v + [ride]),
                     tile=rep_terms.shape[1:], name="adamw_replicated")
    upd_rep = [_unpack(u, rep_shapes) for u in upd_rep]
    loss = upd_rep[0][-1][0, 0]

    order = ["norm1_g", "w_in", "attn_q_norm_g", "attn_k_norm_g", "attn_sink", "gla_wa2_fwd", "gla_ba_fwd",
             "gla_wa2_bwd", "gla_ba_bwd", "gla_out_norm_g", "w_out", "norm2_g", "w_up", "conv_w", "conv_b", "w_down"]
    outs = [loss, grad_x.reshape(1, t, D_MODEL)]
    for kind in range(4):
        by_name = {n: upd_rep[kind][i] for i, n in enumerate(rep_names)}
        by_name["w_in"] = upd_in[kind][None]
        by_name["w_out"] = upd_out[kind][None]
        by_name["w_up"] = upd_up[kind][None]
        by_name["w_down"] = upd_down[kind][None]
        by_name["conv_w"] = upd_small[kind][0][None]
        by_name["gla_wa2_fwd"] = upd_small[kind][1][None]
        by_name["gla_wa2_bwd"] = upd_small[kind][2][None]
        outs += [by_name[n] for n in order]
    return tuple(outs)
```

```python
import functools

import jax
import jax.numpy as jnp
from jax import lax
from jax.experimental import pallas as pl
from jax.experimental.pallas import tpu as pltpu
from jax.experimental.pallas import tpu_sc as plsc

F32 = jnp.float32
BF16 = jnp.bfloat16

D_MODEL = 2048
HEAD_DIM = 128
ATTN_WIDTH = 1024
ATTN_HEADS = 8
KV_HEADS = 2
GQA_GROUP = 4
KV_WIDTH = KV_HEADS * HEAD_DIM
ATTN_BLOCK = 128
WINDOW = 128
ROPE_THETA = 10000.0
GLA_HEADS = 4
GLA_DK = 128
GLA_DV = 256
GLA_KEY_WIDTH = 512
GLA_WIDTH = 1024
GLA_RANK = 16
GLA_GATE_NORMALIZER = 16.0
GLA_CHUNK = 64
D_FF = 5632
NORM_EPS = 1e-6
IN_TOTAL = 4640
IN_MAIN = 4608
LR_PAD = 128
N_DEV = 8
N_CHIP = 4

ADAM_LR = 0.001
ADAM_B1 = 0.9
ADAM_B2 = 0.999
ADAM_EPS = 1e-08
ADAM_WD = 0.01
ADAM_STEP = 10

SEGMENTS = {
    "qa": (0, 0, 1024),
    "gate": (1024, 3584, 1024),
    "vg": (2048, 2560, 1024),
    "qg": (3072, 1536, 512),
    "kg": (3584, 2048, 512),
    "ka": (4096, 1024, 256),
    "va": (4352, 1280, 256),
}

VMEM_LIMIT = 56 * 1024 * 1024
MESH = pl.DeviceIdType.MESH


def _params(semantics=None, vmem=None):
    return pltpu.CompilerParams(dimension_semantics=semantics, vmem_limit_bytes=vmem)


_DIMS = {
    "nn": (((1,), (0,)), ((), ())),
    "nt": (((1,), (1,)), ((), ())),
    "tn": (((0,), (0,)), ((), ())),
}


def _mxu(a, b, mode):
    return lax.dot_general(a.astype(BF16), b.astype(BF16), _DIMS[mode], preferred_element_type=F32)


@functools.partial(jax.custom_vjp, nondiff_argnums=(2,))
def bdot(a, b, mode):
    return _mxu(a, b, mode)


def _bdot_fwd(a, b, mode):
    return _mxu(a, b, mode), (a, b)


def _bdot_bwd(mode, res, g):
    a, b = res
    if mode == "nn":
        return _mxu(g, b, "nt"), _mxu(a, g, "tn")
    if mode == "nt":
        return _mxu(g, b, "nn"), _mxu(g, a, "tn")
    return _mxu(b, g, "nt"), _mxu(a, g, "nn")


bdot.defvjp(_bdot_fwd, _bdot_bwd)


def _rms(x, g):
    return x * lax.rsqrt(jnp.mean(x * x, axis=-1, keepdims=True) + NORM_EPS) * g


def _rope(x, cos, sin_signed):
    return x * cos + pltpu.roll(x, HEAD_DIM // 2, 1) * sin_signed


def _rope_transposed(d, cos, sin_signed):
    return d * cos + pltpu.roll(d * sin_signed, HEAD_DIM // 2, 1)


def _silu(x):
    return x * jax.nn.sigmoid(x)


def _log_sigmoid(z):
    return -(jnp.maximum(-z, 0.0) + jnp.log(1.0 + jnp.exp(-jnp.abs(z))))


def _matmul_call(args, in_specs, o_spec, out_shape, grid, mode, nk, acc_shape, *, name, has_res=False,
                 prefetch=None, load_b=lambda ref: ref[...]):
    dims = _DIMS[mode]
    out_dtype = out_shape.dtype
    n_pre = 0 if prefetch is None else 1

    def body(*refs):
        refs = refs[n_pre:]
        if has_res:
            a_ref, b_ref, r_ref, o_ref = refs[:4]
            rest = refs[4:]
        else:
            a_ref, b_ref, o_ref = refs[:3]
            r_ref = None
            rest = refs[3:]
        part = lax.dot_general(a_ref[...], load_b(b_ref), dims, preferred_element_type=F32)

        def finish(acc):
            if r_ref is not None:
                acc = acc + r_ref[...]
            o_ref[...] = acc.astype(out_dtype)

        if nk == 1:
            finish(part)
        else:
            acc_ref = rest[0]
            kk = pl.program_id(2)

            @pl.when(kk == 0)
            def _():
                acc_ref[...] = part

            @pl.when(kk > 0)
            def _():
                acc_ref[...] += part

            @pl.when(kk == nk - 1)
            def _():
                finish(acc_ref[...])

    scratch = [pltpu.VMEM(acc_shape, F32)] if nk > 1 else []
    params = _params(("parallel", "parallel", "arbitrary"), VMEM_LIMIT)
    if prefetch is None:
        return pl.pallas_call(body, name=name, grid=grid, in_specs=in_specs, out_specs=o_spec, out_shape=out_shape,
                              scratch_shapes=scratch, compiler_params=params)(*args)
    return pl.pallas_call(
        body, name=name,
        grid_spec=pltpu.PrefetchScalarGridSpec(num_scalar_prefetch=1, grid=grid, in_specs=in_specs,
                                               out_specs=o_spec, scratch_shapes=scratch),
        out_shape=out_shape, compiler_params=params)(prefetch, *args)


def _matmul(a, b, mode, *, tm, tn, tk, out_dtype=F32, res=None, name, n_out=None):
    if mode == "nn":
        (m, k), (k2, n) = a.shape, b.shape
    elif mode == "nt":
        (m, k), (n, k2) = a.shape, b.shape
    else:
        (k, m), (k2, n) = a.shape, b.shape
    n = n if n_out is None else n_out
    assert k == k2 and m % tm == 0 and n % tn == 0 and k % tk == 0, (name, a.shape, b.shape, tm, tn, tk)
    if mode == "tn":
        a_spec = pl.BlockSpec((tk, tm), lambda i, j, kk: (kk, i))
    else:
        a_spec = pl.BlockSpec((tm, tk), lambda i, j, kk: (i, kk))
    if mode == "nt":
        b_spec = pl.BlockSpec((tn, tk), lambda i, j, kk: (j, kk))
    else:
        b_spec = pl.BlockSpec((tk, tn), lambda i, j, kk: (kk, j))
    o_spec = pl.BlockSpec((tm, tn), lambda i, j, kk: (i, j))
    in_specs, args = [a_spec, b_spec], [a, b]
    if res is not None:
        in_specs.append(o_spec)
        args.append(res)
    return _matmul_call(args, in_specs, o_spec, jax.ShapeDtypeStruct((m, n), out_dtype),
                        (m // tm, n // tn, k // tk), mode, k // tk, (tm, tn), name=name, has_res=res is not None)


def _out_proj(o_attn, o_gla, w_out, x, *, tm, tn):
    t, ka = o_attn.shape
    kg = o_gla.shape[1]

    def body(a_ref, g_ref, w_ref, x_ref, o_ref):
        acc = lax.dot_general(a_ref[...], w_ref[:ka], _DIMS["nn"], preferred_element_type=F32)
        acc = acc + lax.dot_general(g_ref[...], w_ref[ka:], _DIMS["nn"], preferred_element_type=F32)
        o_ref[...] = acc + x_ref[...]

    tile = pl.BlockSpec((tm, tn), lambda i, j: (i, j))
    return pl.pallas_call(
        body, name="out_proj", grid=(t // tm, D_MODEL // tn),
        in_specs=[pl.BlockSpec((tm, ka), lambda i, j: (i, 0)), pl.BlockSpec((tm, kg), lambda i, j: (i, 0)),
                  pl.BlockSpec((ka + kg, tn), lambda i, j: (0, j)), tile],
        out_specs=tile, out_shape=jax.ShapeDtypeStruct((t, D_MODEL), F32),
        compiler_params=_params(("parallel", "parallel"), VMEM_LIMIT),
    )(o_attn, o_gla, w_out, x)


def _out_proj_dw(o_attn, o_gla, dx1, *, tn):
    t, ka = o_attn.shape
    assert o_gla.shape == (t, ka)

    def body(a_ref, g_ref, d_ref, o_ref):
        @pl.when(pl.program_id(0) == 0)
        def _():
            o_ref[...] = lax.dot_general(a_ref[...], d_ref[...], _DIMS["tn"], preferred_element_type=F32)

        @pl.when(pl.program_id(0) == 1)
        def _():
            o_ref[...] = lax.dot_general(g_ref[...], d_ref[...], _DIMS["tn"], preferred_element_type=F32)

    whole = pl.BlockSpec((t, ka), lambda i, j: (0, 0))
    return pl.pallas_call(
        body, name="dw_out", grid=(2, D_MODEL // tn),
        in_specs=[whole, whole, pl.BlockSpec((t, tn), lambda i, j: (0, j))],
        out_specs=pl.BlockSpec((ka, tn), lambda i, j: (i, j)),
        out_shape=jax.ShapeDtypeStruct((2 * ka, D_MODEL), F32),
        compiler_params=_params(("parallel", "parallel"), VMEM_LIMIT),
    )(o_attn, o_gla, dx1)


UP_BLOCK = 2 * D_FF // N_DEV


def _up_proj(h2, w_up8, *, tm):
    t = h2.shape[0]
    return _matmul_call(
        [h2, w_up8],
        [pl.BlockSpec((tm, D_MODEL), lambda i, j, kk: (i, 0)),
         pl.BlockSpec((None, D_MODEL, UP_BLOCK), lambda i, j, kk: (j, 0, 0))],
        pl.BlockSpec((None, tm, UP_BLOCK), lambda i, j, kk: (j // N_CHIP, i, j % N_CHIP)),
        jax.ShapeDtypeStruct((2, t, D_FF), F32), (t // tm, N_DEV, 1), "nn", 1, None, name="up_proj")


def _up_proj_dx(du, w_up8, *, tm, tn):
    t = du.shape[1]
    pair = 2
    return _matmul_call(
        [du, w_up8],
        [pl.BlockSpec((None, tm, pair * UP_BLOCK), lambda i, j, kk: (kk // 2, i, kk % 2)),
         pl.BlockSpec((pair, tn, UP_BLOCK), lambda i, j, kk: (kk, j, 0))],
        pl.BlockSpec((tm, tn), lambda i, j, kk: (i, j)),
        jax.ShapeDtypeStruct((t, D_MODEL), F32), (t // tm, D_MODEL // tn, N_DEV // pair), "nt", N_DEV // pair,
        (tm, tn), name="up_proj_dx", load_b=lambda ref: jnp.concatenate([ref[0], ref[1]], axis=1))


def _up_proj_dw(h2, du, *, tm, tk):
    t = h2.shape[0]
    return _matmul_call(
        [h2, du],
        [pl.BlockSpec((tk, tm), lambda j, i, kk: (kk, i)),
         pl.BlockSpec((None, tk, UP_BLOCK), lambda j, i, kk: (j // N_CHIP, kk, j % N_CHIP))],
        pl.BlockSpec((None, tm, UP_BLOCK), lambda j, i, kk: (j, i, 0)),
        jax.ShapeDtypeStruct((N_DEV, D_MODEL, UP_BLOCK), F32), (N_DEV, D_MODEL // tm, t // tk), "tn", t // tk,
        (tm, UP_BLOCK), name="up_proj_dw")


IN_TILE = 512


def _in_proj_dw(d_proj, h1, *, tn):
    t = h1.shape[0]
    table = []
    for tile in range(IN_MAIN // IN_TILE):
        dst, src, _ = max(s for s in SEGMENTS.values() if s[0] <= tile * IN_TILE)
        assert (src + tile * IN_TILE - dst) % IN_TILE == 0
        table.append((src + tile * IN_TILE - dst) // IN_TILE)
    assert sorted(table) == list(range(IN_MAIN // IN_TILE))
    return _matmul_call(
        [d_proj, h1],
        [pl.BlockSpec((t, IN_TILE), lambda j, i, kk, tab: (0, i)),
         pl.BlockSpec((t, tn), lambda j, i, kk, tab: (0, j))],
        pl.BlockSpec((IN_TILE, tn), lambda j, i, kk, tab: (tab[i], j)),
        jax.ShapeDtypeStruct((IN_TOTAL, D_MODEL), F32), (D_MODEL // tn, IN_MAIN // IN_TILE, 1), "tn", 1, None,
        name="in_proj_dw", prefetch=jnp.asarray(table, jnp.int32))


def _in_proj_dw_lr(dw_t, d_lr, h1):
    t = h1.shape[0]
    n_lr = IN_TOTAL - IN_MAIN
    tn = 512

    def body(dw_ref, dlr_ref, h1_ref, out_ref):
        full = lax.dot_general(dlr_ref[...], h1_ref[...], _DIMS["tn"], preferred_element_type=F32)
        out_ref[...] = full[:n_lr]

    return pl.pallas_call(
        body, name="in_proj_dw_lr", grid=(D_MODEL // tn,),
        in_specs=[pl.BlockSpec(memory_space=pl.ANY),
                  pl.BlockSpec((t, LR_PAD), lambda j: (0, 0)),
                  pl.BlockSpec((t, tn), lambda j: (0, j))],
        out_specs=pl.BlockSpec((n_lr, tn), lambda j: (IN_MAIN // n_lr, j)),
        out_shape=jax.ShapeDtypeStruct(dw_t.shape, F32),
        input_output_aliases={0: 0},
        compiler_params=_params(("parallel",), VMEM_LIMIT),
    )(dw_t, d_lr, h1)


def _rmsnorm_fwd(x, g, *, name, tr=512):
    t, d = x.shape

    def body(x_ref, g_ref, h_ref):
        h_ref[...] = _rms(x_ref[...], g_ref[...]).astype(BF16)

    return pl.pallas_call(
        body, name=name, grid=(t // tr,),
        in_specs=[pl.BlockSpec((tr, d), lambda i: (i, 0)), pl.BlockSpec((1, d), lambda i: (0, 0))],
        out_specs=pl.BlockSpec((tr, d), lambda i: (i, 0)),
        out_shape=jax.ShapeDtypeStruct((t, d), BF16),
        compiler_params=_params(("parallel",), VMEM_LIMIT),
    )(x, g)


def _rmsnorm_bwd(x, g, dh, dres, *, name, tr=256):
    t, d = x.shape

    def body(x_ref, g_ref, dh_ref, dres_ref, dx_ref, dxb_ref, dg_ref):
        _, vjp = jax.vjp(_rms, x_ref[...], g_ref[...])
        dx, dg = vjp(dh_ref[...])
        dx = dx + dres_ref[...]
        dx_ref[...] = dx
        dxb_ref[...] = dx.astype(BF16)

        @pl.when(pl.program_id(0) == 0)
        def _():
            dg_ref[...] = jnp.zeros_like(dg_ref)

        dg_ref[...] += dg

    row = pl.BlockSpec((tr, d), lambda i: (i, 0))
    vec = pl.BlockSpec((1, d), lambda i: (0, 0))
    return pl.pallas_call(
        body, name=name, grid=(t // tr,),
        in_specs=[row, vec, row, row],
        out_specs=[row, row, vec],
        out_shape=[jax.ShapeDtypeStruct((t, d), F32), jax.ShapeDtypeStruct((t, d), BF16),
                   jax.ShapeDtypeStruct((1, d), F32)],
        compiler_params=_params(("arbitrary",), VMEM_LIMIT),
    )(x, g, dh, dres)


def _seg_block(name, width):
    off = SEGMENTS[name][0]
    assert off % width == 0
    return off // width


def _attn_prep_fwd(proj, cos, sin_signed, gq, gk, *, tr=256):
    t = proj.shape[0]

    def body(q_ref, k_ref, v_ref, cos_ref, sin_ref, gq_ref, gk_ref, qo_ref, ko_ref, vo_ref):
        cos_t, sin_t = cos_ref[...], sin_ref[...]
        for h in range(ATTN_HEADS):
            cols = slice(h * HEAD_DIM, (h + 1) * HEAD_DIM)
            qo_ref[:, cols] = _rope(_rms(q_ref[:, cols], gq_ref[...]), cos_t, sin_t).astype(BF16)
        for h in range(KV_HEADS):
            cols = slice(h * HEAD_DIM, (h + 1) * HEAD_DIM)
            ko_ref[:, cols] = _rope(_rms(k_ref[:, cols], gk_ref[...]), cos_t, sin_t).astype(BF16)
        vo_ref[...] = v_ref[...].astype(BF16)

    qb, kb, vb = _seg_block("qa", ATTN_WIDTH), _seg_block("ka", KV_WIDTH), _seg_block("va", KV_WIDTH)
    tab = pl.BlockSpec((tr, HEAD_DIM), lambda i: (i, 0))
    vec = pl.BlockSpec((1, HEAD_DIM), lambda i: (0, 0))
    return pl.pallas_call(
        body, name="attn_prep_fwd", grid=(t // tr,),
        in_specs=[pl.BlockSpec((tr, ATTN_WIDTH), lambda i: (i, qb)),
                  pl.BlockSpec((tr, KV_WIDTH), lambda i: (i, kb)),
                  pl.BlockSpec((tr, KV_WIDTH), lambda i: (i, vb)),
                  tab, tab, vec, vec],
        out_specs=[pl.BlockSpec((tr, ATTN_WIDTH), lambda i: (i, 0)),
                   pl.BlockSpec((tr, KV_WIDTH), lambda i: (i, 0)),
                   pl.BlockSpec((tr, KV_WIDTH), lambda i: (i, 0))],
        out_shape=[jax.ShapeDtypeStruct((t, ATTN_WIDTH), BF16),
                   jax.ShapeDtypeStruct((t, KV_WIDTH), BF16),
                   jax.ShapeDtypeStruct((t, KV_WIDTH), BF16)],
        compiler_params=_params(("parallel",), VMEM_LIMIT),
    )(proj, proj, proj, cos, sin_signed, gq, gk)


def _attn_heads(q, kcat, vcat, sink_col, valid):
    s = bdot(q, kcat, "nt") * (HEAD_DIM ** -0.5)
    s = jnp.where(valid, s, -jnp.inf)
    m = lax.stop_gradient(jnp.maximum(jnp.max(s, axis=-1, keepdims=True), sink_col))
    p = jnp.exp(s - m)
    p = p / (jnp.sum(p, axis=-1, keepdims=True) + jnp.exp(sink_col - m))
    return bdot(p, vcat, "nn")


def _attn_valid(n, t):
    shape = (GQA_GROUP * ATTN_BLOCK, 3 * ATTN_BLOCK)
    qi = lax.broadcasted_iota(jnp.int32, shape, 0) % ATTN_BLOCK
    sj = lax.broadcasted_iota(jnp.int32, shape, 1)
    kpos = n * ATTN_BLOCK - ATTN_BLOCK + sj
    return (jnp.abs(sj - ATTN_BLOCK - qi) <= WINDOW) & (kpos >= 0) & (kpos < t)


def _head_rows(g):
    return slice(g * ATTN_BLOCK, (g + 1) * ATTN_BLOCK)


def _head_cols(g):
    return slice(g * HEAD_DIM, (g + 1) * HEAD_DIM)


def _stack_heads(ref):
    return jnp.concatenate([ref[:, _head_cols(g)] for g in range(GQA_GROUP)], axis=0).astype(F32)


def _sink_column(sink_ref, h):
    return jnp.concatenate([jnp.full((ATTN_BLOCK, 1), sink_ref[h * GQA_GROUP + g], F32)
                            for g in range(GQA_GROUP)], axis=0)


def _attn_specs(nb):
    q_spec = pl.BlockSpec((ATTN_BLOCK, GQA_GROUP * HEAD_DIM), lambda h, n: (n, h))
    kv_specs = [
        pl.BlockSpec((ATTN_BLOCK, HEAD_DIM), lambda h, n: (jnp.maximum(n - 1, 0), h)),
        pl.BlockSpec((ATTN_BLOCK, HEAD_DIM), lambda h, n: (n, h)),
        pl.BlockSpec((ATTN_BLOCK, HEAD_DIM), lambda h, n: (jnp.minimum(n + 1, nb - 1), h)),
    ]
    return q_spec, kv_specs


def _attn_fwd(q, k, v, sink):
    t = q.shape[0]
    nb = t // ATTN_BLOCK

    def body(sink_ref, q_ref, kp_ref, kc_ref, kn_ref, vp_ref, vc_ref, vn_ref, o_ref):
        h, n = pl.program_id(0), pl.program_id(1)
        valid = _attn_valid(n, t)
        kcat = jnp.concatenate([kp_ref[...], kc_ref[...], kn_ref[...]], axis=0).astype(F32)
        vcat = jnp.concatenate([vp_ref[...], vc_ref[...], vn_ref[...]], axis=0).astype(F32)
        o = _attn_heads(_stack_heads(q_ref), kcat, vcat, _sink_column(sink_ref, h), valid).astype(BF16)
        for g in range(GQA_GROUP):
            o_ref[:, _head_cols(g)] = o[_head_rows(g)]

    q_spec, kv_specs = _attn_specs(nb)
    return pl.pallas_call(
        body, name="attn_fwd", grid=(KV_HEADS, nb),
        in_specs=[pl.BlockSpec(memory_space=pltpu.SMEM), q_spec] + kv_specs + kv_specs,
        out_specs=q_spec,
        out_shape=jax.ShapeDtypeStruct((t, ATTN_WIDTH), BF16),
        compiler_params=_params(("parallel", "parallel"), VMEM_LIMIT),
    )(sink, q, k, k, k, v, v, v)


def _attn_bwd(q, k, v, sink, dmix):
    t = q.shape[0]
    nb = t // ATTN_BLOCK

    def body(sink_ref, q_ref, kp_ref, kc_ref, kn_ref, vp_ref, vc_ref, vn_ref, do_ref,
             dq_ref, dk_lo, dk_mid, dk_hi, dv_lo, dv_mid, dv_hi, dsink_ref):
        h, n = pl.program_id(0), pl.program_id(1)
        valid = _attn_valid(n, t)
        kcat = jnp.concatenate([kp_ref[...], kc_ref[...], kn_ref[...]], axis=0).astype(F32)
        vcat = jnp.concatenate([vp_ref[...], vc_ref[...], vn_ref[...]], axis=0).astype(F32)
        _, vjp = jax.vjp(functools.partial(_attn_heads, valid=valid),
                         _stack_heads(q_ref), kcat, vcat, _sink_column(sink_ref, h))
        dq, dk, dv, dsink_col = vjp(_stack_heads(do_ref))
        row = lax.broadcasted_iota(jnp.int32, (8, HEAD_DIM), 0)
        dsink = jnp.zeros((8, HEAD_DIM), F32)
        for g in range(GQA_GROUP):
            dq_ref[:, _head_cols(g)] = dq[_head_rows(g)]
            dsink = dsink + jnp.where(row == g, jnp.sum(dsink_col[_head_rows(g)]), 0.0)
        for i, (dk_ref, dv_ref) in enumerate(((dk_lo, dv_lo), (dk_mid, dv_mid), (dk_hi, dv_hi))):
            rows = slice(i * ATTN_BLOCK, (i + 1) * ATTN_BLOCK)
            dk_ref[...] = dk[rows]
            dv_ref[...] = dv[rows]

        @pl.when(n == 0)
        def _():
            dsink_ref[...] = jnp.zeros_like(dsink_ref)

        dsink_ref[...] += dsink

    q_spec, kv_specs = _attn_specs(nb)
    kv_out = pl.BlockSpec((ATTN_BLOCK, HEAD_DIM), lambda h, n: (n, h))
    kv_shape = jax.ShapeDtypeStruct((t, KV_WIDTH), F32)
    return pl.pallas_call(
        body, name="attn_bwd", grid=(KV_HEADS, nb),
        in_specs=[pl.BlockSpec(memory_space=pltpu.SMEM), q_spec] + kv_specs + kv_specs + [q_spec],
        out_specs=[q_spec] + [kv_out] * 6 + [pl.BlockSpec((None, 8, HEAD_DIM), lambda h, n: (h, 0, 0))],
        out_shape=[jax.ShapeDtypeStruct((t, ATTN_WIDTH), F32)] + [kv_shape] * 6
                  + [jax.ShapeDtypeStruct((KV_HEADS, 8, HEAD_DIM), F32)],
        compiler_params=_params(("parallel", "arbitrary"), VMEM_LIMIT),
    )(sink, q, k, k, k, v, v, v, dmix)


def _attn_prep_bwd(proj, cos, sin_signed, gq, gk, dq, dks, dvs):
    t = proj.shape[0]
    tr = ATTN_BLOCK
    nb = t // tr

    def body(q_ref, k_ref, cos_ref, sin_ref, gq_ref, gk_ref, dq_ref,
             dk_lo, dk_mid, dk_hi, dv_lo, dv_mid, dv_hi,
             dqo_ref, dko_ref, dvo_ref, dgq_ref, dgk_ref):
        n = pl.program_id(0)
        cos_t, sin_t = cos_ref[...], sin_ref[...]
        has_next = (n < nb - 1).astype(F32)
        has_prev = (n > 0).astype(F32)
        dk = dk_lo[...] * has_next + dk_mid[...] + dk_hi[...] * has_prev
        dv = dv_lo[...] * has_next + dv_mid[...] + dv_hi[...] * has_prev
        dvo_ref[...] = dv.astype(BF16)
        dgq = jnp.zeros((1, HEAD_DIM), F32)
        dgk = jnp.zeros((1, HEAD_DIM), F32)
        for h in range(ATTN_HEADS):
            cols = slice(h * HEAD_DIM, (h + 1) * HEAD_DIM)
            _, vjp = jax.vjp(_rms, q_ref[:, cols], gq_ref[...])
            dx, dg = vjp(_rope_transposed(dq_ref[:, cols], cos_t, sin_t))
            dqo_ref[:, cols] = dx.astype(BF16)
            dgq = dgq + dg
        for h in range(KV_HEADS):
            cols = slice(h * HEAD_DIM, (h + 1) * HEAD_DIM)
            _, vjp = jax.vjp(_rms, k_ref[:, cols], gk_ref[...])
            dx, dg = vjp(_rope_transposed(dk[:, cols], cos_t, sin_t))
            dko_ref[:, cols] = dx.astype(BF16)
            dgk = dgk + dg

        @pl.when(n == 0)
        def _():
            dgq_ref[...] = jnp.zeros_like(dgq_ref)
            dgk_ref[...] = jnp.zeros_like(dgk_ref)

        dgq_ref[...] += dgq
        dgk_ref[...] += dgk

    qb, kb = _seg_block("qa", ATTN_WIDTH), _seg_block("ka", KV_WIDTH)
    tab = pl.BlockSpec((tr, HEAD_DIM), lambda i: (i, 0))
    vec = pl.BlockSpec((1, HEAD_DIM), lambda i: (0, 0))
    kv = [pl.BlockSpec((tr, KV_WIDTH), lambda i: (jnp.minimum(i + 1, nb - 1), 0)),
          pl.BlockSpec((tr, KV_WIDTH), lambda i: (i, 0)),
          pl.BlockSpec((tr, KV_WIDTH), lambda i: (jnp.maximum(i - 1, 0), 0))]
    wide = pl.BlockSpec((tr, ATTN_WIDTH), lambda i: (i, 0))
    narrow = pl.BlockSpec((tr, KV_WIDTH), lambda i: (i, 0))
    return pl.pallas_call(
        body, name="attn_prep_bwd", grid=(nb,),
        in_specs=[pl.BlockSpec((tr, ATTN_WIDTH), lambda i: (i, qb)),
                  pl.BlockSpec((tr, KV_WIDTH), lambda i: (i, kb)),
                  tab, tab, vec, vec, wide] + kv + kv,
        out_specs=[wide, narrow, narrow, vec, vec],
        out_shape=[jax.ShapeDtypeStruct((t, ATTN_WIDTH), BF16),
                   jax.ShapeDtypeStruct((t, KV_WIDTH), BF16),
                   jax.ShapeDtypeStruct((t, KV_WIDTH), BF16),
                   jax.ShapeDtypeStruct((1, HEAD_DIM), F32),
                   jax.ShapeDtypeStruct((1, HEAD_DIM), F32)],
        compiler_params=_params(("arbitrary",), VMEM_LIMIT),
    )(proj, proj, cos, sin_signed, gq, gk, dq, *dks, *dvs)


def _decay_fn(lr, w2, ba):
    return _log_sigmoid(bdot(lr, w2, "nn") + ba) / GLA_GATE_NORMALIZER


def _gla_prep_fwd(proj_lr, w2, ba2, *, tr=512):
    t = proj_lr.shape[0]
    width = 2 * GLA_KEY_WIDTH

    def body(lr_ref, w2_ref, ba_ref, g_ref):
        g_ref[...] = _decay_fn(lr_ref[...], w2_ref[...], ba_ref[...])

    return pl.pallas_call(
        body, name="gla_prep_fwd", grid=(t // tr,),
        in_specs=[pl.BlockSpec((tr, LR_PAD), lambda i: (i, 0)),
                  pl.BlockSpec((LR_PAD, width), lambda i: (0, 0)),
                  pl.BlockSpec((1, width), lambda i: (0, 0))],
        out_specs=pl.BlockSpec((tr, width), lambda i: (i, 0)),
        out_shape=jax.ShapeDtypeStruct((t, width), F32),
        compiler_params=_params(("parallel",), VMEM_LIMIT),
    )(proj_lr, w2, ba2)


def _gla_prep_bwd(proj_lr, w2, ba2, dg_f, dg_b, *, tr=512):
    t = proj_lr.shape[0]
    width = 2 * GLA_KEY_WIDTH

    def body(lr_ref, w2_ref, ba_ref, dgf_ref, dgb_ref, dlr_ref, dw2_ref, dba_ref):
        _, vjp = jax.vjp(_decay_fn, lr_ref[...], w2_ref[...], ba_ref[...])
        dlr, dw2, dba = vjp(jnp.concatenate([dgf_ref[...], dgb_ref[...]], axis=1))
        dlr_ref[...] = dlr.astype(BF16)

        @pl.when(pl.program_id(0) == 0)
        def _():
            dw2_ref[...] = jnp.zeros_like(dw2_ref)
            dba_ref[...] = jnp.zeros_like(dba_ref)

        dw2_ref[...] += dw2
        dba_ref[...] += dba

    half = pl.BlockSpec((tr, GLA_KEY_WIDTH), lambda i: (i, 0))
    return pl.pallas_call(
        body, name="gla_prep_bwd", grid=(t // tr,),
        in_specs=[pl.BlockSpec((tr, LR_PAD), lambda i: (i, 0)),
                  pl.BlockSpec((LR_PAD, width), lambda i: (0, 0)),
                  pl.BlockSpec((1, width), lambda i: (0, 0)), half, half],
        out_specs=[pl.BlockSpec((tr, LR_PAD), lambda i: (i, 0)),
                   pl.BlockSpec((LR_PAD, width), lambda i: (0, 0)),
                   pl.BlockSpec((1, width), lambda i: (0, 0))],
        out_shape=[jax.ShapeDtypeStruct((t, LR_PAD), BF16),
                   jax.ShapeDtypeStruct((LR_PAD, width), F32),
                   jax.ShapeDtypeStruct((1, width), F32)],
        compiler_params=_params(("arbitrary",), VMEM_LIMIT),
    )(proj_lr, w2, ba2, dg_f, dg_b)


def _gla_k(h):
    return slice(h * GLA_DK, (h + 1) * GLA_DK)


def _gla_v(h):
    return slice(h * GLA_DV, (h + 1) * GLA_DV)


def _running_sum(x, downward):
    n = x.shape[0]
    row = lax.broadcasted_iota(jnp.int32, x.shape, 0)
    step = 1
    while step < n:
        if downward:
            x = x + jnp.where(row >= step, pltpu.roll(x, step, 0), 0.0)
        else:
            x = x + jnp.where(row < n - step, pltpu.roll(x, n - step, 0), 0.0)
        step *= 2
    return x


@functools.partial(jax.custom_vjp, nondiff_argnums=(1,))
def _cumsum_rows(x, downward):
    return _running_sum(x, downward)


def _cumsum_rows_fwd(x, downward):
    return _running_sum(x, downward), None


def _cumsum_rows_bwd(downward, _, ct):
    return (_running_sum(ct, not downward),)


_cumsum_rows.defvjp(_cumsum_rows_fwd, _cumsum_rows_bwd)


def _gla_chunk(q, k, v, g, state, forward):
    c = GLA_CHUNK
    row = lax.broadcasted_iota(jnp.int32, (c, c), 0)
    col = lax.broadcasted_iota(jnp.int32, (c, c), 1)
    rid = lax.broadcasted_iota(jnp.int32, (c, GLA_DK), 0)
    q = q * (GLA_DK ** -0.5)
    if forward:
        see = row >= col
        upto_ref = rid <= c // 2
    else:
        see = row < col
        upto_ref = rid >= c - 1 - c // 2
    b = _cumsum_rows(g, forward)
    b_last = jnp.sum(g, axis=0, keepdims=True)
    b_ref = jnp.sum(jnp.where(upto_ref, g, 0.0), axis=0, keepdims=True)
    a = bdot(q * jnp.exp(b - b_ref), k * jnp.exp(b_ref - b), "nt")
    a = jnp.where(see, a, 0.0)
    o = bdot(a, v, "nn") + bdot(q * jnp.exp(b), state, "nt")
    new_state = state * jnp.exp(b_last) + bdot(v, k * jnp.exp(b_last - b), "tn")
    return o, new_state


def _gla_fwd(proj, g):
    t = proj.shape[0]
    c = GLA_CHUNK
    nchunk = t // c
    qb, kb, vb = _seg_block("qg", GLA_KEY_WIDTH), _seg_block("kg", GLA_KEY_WIDTH), _seg_block("vg", GLA_WIDTH)

    def body(qf, kf, vf, gf, qr, kr, vr, gr, of_ref, ob_ref, sf_ref, sb_ref, state):
        @pl.when(pl.program_id(0) == 0)
        def _():
            state[...] = jnp.zeros_like(state)

        dirs = ((qf, kf, vf, gf, of_ref, sf_ref), (qr, kr, vr, gr, ob_ref, sb_ref))
        args = [(q_ref[:, _gla_k(h)], k_ref[:, _gla_k(h)], v_ref[:, _gla_v(h)], g_ref[:, _gla_k(h)], state[d, h])
                for d, (q_ref, k_ref, v_ref, g_ref, _, _) in enumerate(dirs) for h in range(GLA_HEADS)]
        results = [_gla_chunk(*a, forward=(i < GLA_HEADS)) for i, a in enumerate(args)]
        for i, (a, (o, s_out)) in enumerate(zip(args, results)):
            d, h = divmod(i, GLA_HEADS)
            dirs[d][5][h] = a[4]
            dirs[d][4][:, _gla_v(h)] = o
            state[d, h] = s_out

    specs, outs = [], []
    for d in range(2):
        ci = (lambda i: i) if d == 0 else (lambda i: nchunk - 1 - i)
        specs += [pl.BlockSpec((c, GLA_KEY_WIDTH), lambda i, ci=ci: (ci(i), qb)),
                  pl.BlockSpec((c, GLA_KEY_WIDTH), lambda i, ci=ci: (ci(i), kb)),
                  pl.BlockSpec((c, GLA_WIDTH), lambda i, ci=ci: (ci(i), vb)),
                  pl.BlockSpec((c, GLA_KEY_WIDTH), lambda i, ci=ci, d=d: (ci(i), d))]
        outs.append(pl.BlockSpec((c, GLA_WIDTH), lambda i, ci=ci: (ci(i), 0)))
    for d in range(2):
        ci = (lambda i: i) if d == 0 else (lambda i: nchunk - 1 - i)
        outs.append(pl.BlockSpec((None, GLA_HEADS, GLA_DV, GLA_DK), lambda i, ci=ci: (ci(i), 0, 0, 0)))
    o_shape = jax.ShapeDtypeStruct((t, GLA_WIDTH), F32)
    s_shape = jax.ShapeDtypeStruct((nchunk, GLA_HEADS, GLA_DV, GLA_DK), F32)
    return pl.pallas_call(
        body, name="gla_fwd", grid=(nchunk,),
        in_specs=specs, out_specs=outs,
        out_shape=[o_shape, o_shape, s_shape, s_shape],
        scratch_shapes=[pltpu.VMEM((2, GLA_HEADS, GLA_DV, GLA_DK), F32)],
        compiler_params=_params(("arbitrary",), VMEM_LIMIT),
    )(proj, proj, proj, g, proj, proj, proj, g)


def _gla_bwd(proj, g, s_f, s_b, do):
    t = proj.shape[0]
    c = GLA_CHUNK
    nchunk = t // c
    qb, kb, vb = _seg_block("qg", GLA_KEY_WIDTH), _seg_block("kg", GLA_KEY_WIDTH), _seg_block("vg", GLA_WIDTH)

    def body(*refs):
        ins, outs, dstate = refs[:12], refs[12:20], refs[20]

        @pl.when(pl.program_id(0) == 0)
        def _():
            dstate[...] = jnp.zeros_like(dstate)

        loaded = []
        for d in range(2):
            q_ref, k_ref, v_ref, g_ref, s_ref, do_ref = ins[6 * d:6 * d + 6]
            for h in range(GLA_HEADS):
                loaded.append(((q_ref[:, _gla_k(h)], k_ref[:, _gla_k(h)], v_ref[:, _gla_v(h)], g_ref[:, _gla_k(h)],
                                s_ref[h]), (do_ref[:, _gla_v(h)], dstate[d, h])))
        grads = []
        for i, (primals, cotangents) in enumerate(loaded):
            _, vjp = jax.vjp(functools.partial(_gla_chunk, forward=(i < GLA_HEADS)), *primals)
            grads.append(vjp(cotangents))
        for i, (dq, dk, dv, dg, ds) in enumerate(grads):
            d, h = divmod(i, GLA_HEADS)
            dq_ref, dk_ref, dv_ref, dg_ref = outs[4 * d:4 * d + 4]
            dq_ref[:, _gla_k(h)] = dq
            dk_ref[:, _gla_k(h)] = dk
            dv_ref[:, _gla_v(h)] = dv
            dg_ref[:, _gla_k(h)] = dg
            dstate[d, h] = ds

    specs, outs, shapes = [], [], []
    for d in range(2):
        ci = (lambda i: nchunk - 1 - i) if d == 0 else (lambda i: i)
        specs += [pl.BlockSpec((c, GLA_KEY_WIDTH), lambda i, ci=ci: (ci(i), qb)),
                  pl.BlockSpec((c, GLA_KEY_WIDTH), lambda i, ci=ci: (ci(i), kb)),
                  pl.BlockSpec((c, GLA_WIDTH), lambda i, ci=ci: (ci(i), vb)),
                  pl.BlockSpec((c, GLA_KEY_WIDTH), lambda i, ci=ci, d=d: (ci(i), d)),
                  pl.BlockSpec((None, GLA_HEADS, GLA_DV, GLA_DK), lambda i, ci=ci: (ci(i), 0, 0, 0)),
                  pl.BlockSpec((c, GLA_WIDTH), lambda i, ci=ci: (ci(i), 0))]
        key = pl.BlockSpec((c, GLA_KEY_WIDTH), lambda i, ci=ci: (ci(i), 0))
        val = pl.BlockSpec((c, GLA_WIDTH), lambda i, ci=ci: (ci(i), 0))
        outs += [key, key, val, key]
        shapes += [jax.ShapeDtypeStruct((t, GLA_KEY_WIDTH), F32), jax.ShapeDtypeStruct((t, GLA_KEY_WIDTH), F32),
                   jax.ShapeDtypeStruct((t, GLA_WIDTH), F32), jax.ShapeDtypeStruct((t, GLA_KEY_WIDTH), F32)]
    return pl.pallas_call(
        body, name="gla_bwd", grid=(nchunk,),
        in_specs=specs, out_specs=outs, out_shape=shapes,
        scratch_shapes=[pltpu.VMEM((2, GLA_HEADS, GLA_DV, GLA_DK), F32)],
        compiler_params=_params(("arbitrary",), VMEM_LIMIT),
    )(proj, proj, proj, g, s_f, do, proj, proj, proj, g, s_b, do)


def _gla_out_head(o_f, o_b, gate, gn):
    return _rms(o_f + o_b, gn) * _silu(gate)


def _gla_out_fwd(o_f, o_b, proj, gn, *, tr=256):
    t = o_f.shape[0]
    gb = _seg_block("gate", GLA_WIDTH)

    def body(of_ref, ob_ref, gate_ref, gn_ref, out_ref):
        for h in range(GLA_HEADS):
            vc = slice(h * GLA_DV, (h + 1) * GLA_DV)
            out_ref[:, vc] = _gla_out_head(of_ref[:, vc], ob_ref[:, vc], gate_ref[:, vc], gn_ref[...]).astype(BF16)

    wide = pl.BlockSpec((tr, GLA_WIDTH), lambda i: (i, 0))
    return pl.pallas_call(
        body, name="gla_out_fwd", grid=(t // tr,),
        in_specs=[wide, wide, pl.BlockSpec((tr, GLA_WIDTH), lambda i: (i, gb)),
                  pl.BlockSpec((1, GLA_DV), lambda i: (0, 0))],
        out_specs=wide,
        out_shape=jax.ShapeDtypeStruct((t, GLA_WIDTH), BF16),
        compiler_params=_params(("parallel",), VMEM_LIMIT),
    )(o_f, o_b, proj, gn)


def _gla_out_bwd(o_f, o_b, proj, gn, dmix, *, tr=256):
    t = o_f.shape[0]
    gb = _seg_block("gate", GLA_WIDTH)

    def body(of_ref, ob_ref, gate_ref, gn_ref, dout_ref, do_ref, dgate_ref, dgn_ref):
        dgn = jnp.zeros((1, GLA_DV), F32)
        for h in range(GLA_HEADS):
            vc = slice(h * GLA_DV, (h + 1) * GLA_DV)
            _, vjp = jax.vjp(_gla_out_head, of_ref[:, vc], ob_ref[:, vc], gate_ref[:, vc], gn_ref[...])
            do, _, dgate, dg = vjp(dout_ref[:, vc])
            do_ref[:, vc] = do
            dgate_ref[:, vc] = dgate.astype(BF16)
            dgn = dgn + dg

        @pl.when(pl.program_id(0) == 0)
        def _():
            dgn_ref[...] = jnp.zeros_like(dgn_ref)

        dgn_ref[...] += dgn

    wide = pl.BlockSpec((tr, GLA_WIDTH), lambda i: (i, 0))
    vec = pl.BlockSpec((1, GLA_DV), lambda i: (0, 0))
    return pl.pallas_call(
        body, name="gla_out_bwd", grid=(t // tr,),
        in_specs=[wide, wide, pl.BlockSpec((tr, GLA_WIDTH), lambda i: (i, gb)), vec,
                  pl.BlockSpec((tr, GLA_WIDTH), lambda i: (i, 1))],
        out_specs=[wide, wide, vec],
        out_shape=[jax.ShapeDtypeStruct((t, GLA_WIDTH), F32), jax.ShapeDtypeStruct((t, GLA_WIDTH), BF16),
                   jax.ShapeDtypeStruct((1, GLA_DV), F32)],
        compiler_params=_params(("arbitrary",), VMEM_LIMIT),
    )(o_f, o_b, proj, gn, dmix)


CONV_TR = 1024
CONV_TC = 512
HALO = 8
HALO16 = 16


def _conv3(u, w, b):
    n = u.shape[0]
    return pltpu.roll(u, 1, 0) * w[0:1] + u * w[1:2] + pltpu.roll(u, n - 1, 0) * w[2:3] + b


def _conv_ext(main_ref, prev_ref, next_ref, r, nr):
    prev = prev_ref[...].astype(F32)[-HALO:] * (r > 0).astype(F32)
    nxt = next_ref[...].astype(F32)[:HALO] * (r < nr - 1).astype(F32)
    return jnp.concatenate([prev, main_ref[...].astype(F32), nxt], axis=0)


def _conv_specs(t, halo, half=None):
    per = CONV_TR // halo
    last = t // halo - 1
    lead = () if half is None else (None,)
    at = (lambda *ix: ix) if half is None else (lambda *ix: (half,) + ix)
    return [pl.BlockSpec(lead + (CONV_TR, CONV_TC), lambda j, r: at(r, j)),
            pl.BlockSpec(lead + (halo, CONV_TC), lambda j, r: at(jnp.maximum(r * per - 1, 0), j)),
            pl.BlockSpec(lead + (halo, CONV_TC), lambda j, r: at(jnp.minimum((r + 1) * per, last), j))]


def _ffn_mid_fwd(u, cw_g, cw_v, cb_g, cb_v):
    _, t, f = u.shape
    nr = t // CONV_TR

    def body(ug, ugp, ugn, uv, uvp, uvn, wg, wv, bg, bv, a_ref):
        r = pl.program_id(1)
        gate = _conv3(_conv_ext(ug, ugp, ugn, r, nr), wg[...], bg[...])[HALO:HALO + CONV_TR]
        val = _conv3(_conv_ext(uv, uvp, uvn, r, nr), wv[...], bv[...])[HALO:HALO + CONV_TR]
        a_ref[...] = (_silu(gate) * val).astype(BF16)

    w_spec = pl.BlockSpec((3, CONV_TC), lambda j, r: (0, j))
    b_spec = pl.BlockSpec((1, CONV_TC), lambda j, r: (0, j))
    return pl.pallas_call(
        body, name="ffn_mid_fwd", grid=(f // CONV_TC, nr),
        in_specs=_conv_specs(t, HALO, 0) + _conv_specs(t, HALO, 1) + [w_spec, w_spec, b_spec, b_spec],
        out_specs=pl.BlockSpec((CONV_TR, CONV_TC), lambda j, r: (r, j)),
        out_shape=jax.ShapeDtypeStruct((t, f), BF16),
        compiler_params=_params(("parallel", "parallel"), VMEM_LIMIT),
    )(u, u, u, u, u, u, cw_g, cw_v, cb_g, cb_v)


def _ffn_mid_bwd(u, cw_g, cw_v, cb_g, cb_v, da):
    _, t, f = u.shape
    nr = t // CONV_TR
    ext = CONV_TR + 2 * HALO

    def body(ug, ugp, ugn, uv, uvp, uvn, dam, dap, dan, wg, wv, bg, bv,
             du_ref, dwg_ref, dwv_ref, dbg_ref, dbv_ref):
        r = pl.program_id(1)
        shifted = []
        for main, prev, nxt in ((ug, ugp, ugn), (uv, uvp, uvn)):
            x = _conv_ext(main, prev, nxt, r, nr)
            shifted.append((pltpu.roll(x, 1, 0), x, pltpu.roll(x, ext - 1, 0)))
        da_x = _conv_ext(dam, dap, dan, r, nr)
        wg_t, wv_t = wg[...], wv[...]
        gate = shifted[0][0] * wg_t[0:1] + shifted[0][1] * wg_t[1:2] + shifted[0][2] * wg_t[2:3] + bg[...]
        val = shifted[1][0] * wv_t[0:1] + shifted[1][1] * wv_t[1:2] + shifted[1][2] * wv_t[2:3] + bv[...]
        sig = jax.nn.sigmoid(gate)
        silu = gate * sig
        d_val = da_x * silu
        d_gate = da_x * val * (sig + silu * (1.0 - sig))
        own = slice(HALO, HALO + CONV_TR)
        for half, (xs3, d, wt, dw_ref, db_ref) in enumerate(((shifted[0], d_gate, wg_t, dwg_ref, dbg_ref),
                                                            (shifted[1], d_val, wv_t, dwv_ref, dbv_ref))):
            du = pltpu.roll(d, ext - 1, 0) * wt[0:1] + d * wt[1:2] + pltpu.roll(d, 1, 0) * wt[2:3]
            du_ref[half] = du[own].astype(BF16)
            d_own = d[own]
            dw = jnp.concatenate([jnp.sum(x[own] * d_own, axis=0, keepdims=True) for x in xs3], axis=0)
            db = jnp.sum(d_own, axis=0, keepdims=True)

            @pl.when(r == 0)
            def _():
                dw_ref[...] = jnp.zeros_like(dw_ref)
                db_ref[...] = jnp.zeros_like(db_ref)

            dw_ref[...] += dw
            db_ref[...] += db

    w_spec = pl.BlockSpec((3, CONV_TC), lambda j, r: (0, j))
    b_spec = pl.BlockSpec((1, CONV_TC), lambda j, r: (0, j))
    return pl.pallas_call(
        body, name="ffn_mid_bwd", grid=(f // CONV_TC, nr),
        in_specs=(_conv_specs(t, HALO, 0) + _conv_specs(t, HALO, 1) + _conv_specs(t, HALO16)
                  + [w_spec, w_spec, b_spec, b_spec]),
        out_specs=[pl.BlockSpec((2, CONV_TR, CONV_TC), lambda j, r: (0, r, j)), w_spec, w_spec, b_spec, b_spec],
        out_shape=[jax.ShapeDtypeStruct((2, t, f), BF16),
                   jax.ShapeDtypeStruct((3, f), F32), jax.ShapeDtypeStruct((3, f), F32),
                   jax.ShapeDtypeStruct((1, f), F32), jax.ShapeDtypeStruct((1, f), F32)],
        compiler_params=_params(("parallel", "arbitrary"), VMEM_LIMIT),
    )(u, u, u, u, u, u, da, da, da, cw_g, cw_v, cb_g, cb_v)


def _down_proj_loss(act, w_down, x1, target, *, tm, tn):
    t, f = act.shape
    d = w_down.shape[1]

    def body(a_ref, w_ref, x_ref, t_ref, loss_ref, dy_ref, dyb_ref):
        y = lax.dot_general(a_ref[...], w_ref[...], _DIMS["nn"], preferred_element_type=F32) + x_ref[...]
        err = y - t_ref[...]
        dy = err * (1.0 / d)
        dy_ref[...] = dy
        dyb_ref[...] = dy.astype(BF16)
        part = 0.5 * jnp.sum(jnp.sum(err * err, axis=-1, keepdims=True) * (1.0 / d), axis=0, keepdims=True)

        @pl.when((pl.program_id(0) == 0) & (pl.program_id(1) == 0))
        def _():
            loss_ref[...] = jnp.zeros_like(loss_ref)

        loss_ref[...] += jnp.broadcast_to(part, loss_ref.shape)

    tile = pl.BlockSpec((tm, tn), lambda i, j: (i, j))
    return pl.pallas_call(
        body, name="down_proj_loss", grid=(t // tm, d // tn),
        in_specs=[pl.BlockSpec((tm, f), lambda i, j: (i, 0)), pl.BlockSpec((f, tn), lambda i, j: (0, j)), tile, tile],
        out_specs=[pl.BlockSpec((1, 128), lambda i, j: (0, 0)), tile, tile],
        out_shape=[jax.ShapeDtypeStruct((1, 128), F32), jax.ShapeDtypeStruct((t, d), F32),
                   jax.ShapeDtypeStruct((t, d), BF16)],
        compiler_params=_params(("arbitrary", "arbitrary"), VMEM_LIMIT),
    )(act, w_down, x1, target)


ANY = pl.BlockSpec(memory_space=pl.ANY)


def _position():
    return lax.axis_index("x"), lax.axis_index("y"), lax.axis_index("c")


def _other_chips(x, y):
    return [(1 - x, y), (x, 1 - y), (1 - x, 1 - y)]


def _handshake(peers):
    barrier = pltpu.get_barrier_semaphore()
    for peer in peers:
        pl.semaphore_signal(barrier, inc=1, device_id=peer, device_id_type=MESH)
    pl.semaphore_wait(barrier, len(peers))


def _exchange(body, operands, out_shapes, sems, *, name, collective_id):
    n_in, n_out = len(operands), len(out_shapes)

    def run(*refs):
        body(refs[:n_in], refs[n_in:n_in + n_out], *refs[n_in + n_out:])

    if collective_id is None:
        return pl.pallas_call(run, name=name, in_specs=[ANY] * n_in, out_specs=[ANY] * n_out,
                              out_shape=out_shapes, scratch_shapes=sems)(*operands)
    return pl.kernel(run, name=name, out_type=out_shapes,
                     mesh=plsc.ScalarSubcoreMesh(axis_name="sequencer", num_cores=1), scratch_types=sems,
                     compiler_params=pltpu.CompilerParams(collective_id=collective_id))(*operands)


def _all_gather(blocks, *, name, collective_id=None):
    na = len(blocks)

    def body(ins, outs, send_sems, recv_sems, local_sems):
        x, y, c = _position()
        sibling = (x, y, 1 - c)
        chips = _other_chips(x, y)
        if collective_id is not None:
            _handshake([sibling] + [(*chip, c) for chip in chips])

        def index(px, py, pc):
            return 4 * px + 2 * py + pc

        def copy(a, k, block, to, src=None):
            dst = outs[a].at[index(*block)]
            return pltpu.make_async_remote_copy(
                src_ref=dst if src is None else src, dst_ref=dst,
                send_sem=send_sems.at[a, k], recv_sem=recv_sems.at[a, k],
                device_id=to, device_id_type=MESH)

        pending = []
        for a in range(na):
            mine = pltpu.make_async_copy(ins[a], outs[a].at[index(x, y, c)], local_sems.at[a])
            mine.start()
            pending.append(mine)
        first = []
        for a in range(na):
            first.append(copy(a, 0, (x, y, c), sibling, src=ins[a]))
            first += [copy(a, 1 + j, (x, y, c), (*chip, c), src=ins[a]) for j, chip in enumerate(chips)]
        for cp in first:
            cp.start()
        passed = []
        for j, chip in enumerate(chips):
            for a in range(na):
                copy(a, 1 + j, (*chip, c), (x, y, c)).wait_recv()
                fwd = copy(a, 4 + j, (*chip, c), sibling)
                fwd.start()
                passed.append(fwd)
        for a in range(na):
            copy(a, 0, sibling, (x, y, c)).wait_recv()
            for j, chip in enumerate(chips):
                copy(a, 4 + j, (*chip, 1 - c), (x, y, c)).wait_recv()
        for cp in first + passed:
            cp.wait_send()
        for cp in pending:
            cp.wait()

    return _exchange(
        body, blocks, [jax.ShapeDtypeStruct((N_DEV,) + b.shape, b.dtype) for b in blocks],
        [pltpu.SemaphoreType.DMA((na, 7)), pltpu.SemaphoreType.DMA((na, 7)), pltpu.SemaphoreType.DMA((na,))],
        name=name, collective_id=collective_id)


def _grad_exchange(grads, parts, *, name, collective_id):
    ng, npart = len(grads), len(parts)

    def body(ins, outs, core_send, core_recv, chip_send, chip_recv, local_sems):
        x, y, c = _position()
        sibling = (x, y, 1 - c)
        chips = _other_chips(x, y)
        _handshake([sibling] + [(px, py, c) for px, py in chips])
        me = 2 * x + y
        copies = []
        for b in range(npart):
            src, dst = ins[ng + b], outs[ng + b]
            own = pltpu.make_async_copy(src.at[me], dst.at[me], local_sems.at[b])
            own.start()
            copies.append(own)
            for j, (px, py) in enumerate(chips):
                cp = pltpu.make_async_remote_copy(
                    src_ref=src.at[2 * px + py], dst_ref=dst.at[me],
                    send_sem=chip_send.at[b, j], recv_sem=chip_recv.at[b, j],
                    device_id=(px, py, c), device_id_type=MESH)
                cp.start()
                copies.append(cp)
        for a in range(ng):
            for k in range(N_CHIP):
                cp = pltpu.make_async_remote_copy(
                    src_ref=ins[a].at[k, 1 - c], dst_ref=outs[a].at[k],
                    send_sem=core_send.at[a, k], recv_sem=core_recv.at[a, k],
                    device_id=sibling, device_id_type=MESH)
                cp.start()
                copies.append(cp)
        for cp in copies:
            cp.wait()

    shapes = ([jax.ShapeDtypeStruct((N_CHIP,) + g.shape[2:], g.dtype) for g in grads]
              + [jax.ShapeDtypeStruct(p.shape, p.dtype) for p in parts])
    sems = [pltpu.SemaphoreType.DMA((max(ng, 1), N_CHIP)), pltpu.SemaphoreType.DMA((max(ng, 1), N_CHIP)),
            pltpu.SemaphoreType.DMA((max(npart, 1), 3)), pltpu.SemaphoreType.DMA((max(npart, 1), 3)),
            pltpu.SemaphoreType.DMA((max(npart, 1),))]
    out = _exchange(body, list(grads) + list(parts), shapes, sems, name=name, collective_id=collective_id)
    return out[:ng], out[ng:]


def _pair_sum(grad, theirs, core, *, tile, name):
    _, _, r, w = grad.shape
    tr, tw = tile
    assert r % tr == 0 and w % tw == 0

    def body(core_ref, mine_ref, theirs_ref, out_ref):
        out_ref[...] = mine_ref[...] + theirs_ref[...]

    return pl.pallas_call(
        body, name=name,
        grid_spec=pltpu.PrefetchScalarGridSpec(
            num_scalar_prefetch=1, grid=(N_CHIP, r // tr, w // tw),
            in_specs=[pl.BlockSpec((None, None, tr, tw), lambda k, i, j, core_ref: (k, core_ref[0], i, j)),
                      pl.BlockSpec((None, tr, tw), lambda k, i, j, core_ref: (k, i, j))],
            out_specs=pl.BlockSpec((None, tr, tw), lambda k, i, j, core_ref: (k, i, j))),
        out_shape=jax.ShapeDtypeStruct((N_CHIP, r, w), F32),
        compiler_params=_params(("parallel", "parallel", "parallel"), VMEM_LIMIT),
    )(core, grad, theirs)


class _ReduceScatter:
    def __init__(self, core):
        self.core = core
        self.pending = None
        self.results = {}
        self.launches = 0

    def push(self, tag, grads, rows, then):
        pair, prev_tag = [], None
        if self.pending is not None:
            prev_tag, prev, theirs, prev_rows = self.pending
            pair = [_pair_sum(g, s, self.core, tile=tile, name=f"pair_sum_{prev_tag}_{i}")
                    for i, (g, s, tile) in enumerate(zip(prev, theirs, prev_rows))]
        grads, pair, then = lax.optimization_barrier((list(grads), pair, then))
        grads = [g.reshape((N_CHIP, 2) + g.shape[1:]) for g in grads]
        self.launches += 1
        theirs, parts = _grad_exchange(grads, pair, name=f"grad_exchange_{self.launches}",
                                       collective_id=1 + self.launches)
        if prev_tag is not None:
            self.results[prev_tag] = parts
        self.pending = (tag, grads, theirs, rows) if tag is not None else None
        return then

    def result(self, tag):
        return self.results[tag]


def _adamw(parts, w, m, v, *, tile, name):
    n, r, cols = parts.shape
    tr, tw = tile
    assert r % tr == 0 and cols % tw == 0 and w.shape == (r, cols)
    c1 = 1.0 - ADAM_B1 ** ADAM_STEP
    c2 = 1.0 - ADAM_B2 ** ADAM_STEP

    def body(p_ref, w_ref, m_ref, v_ref, g_ref, d_ref, nm_ref, nv_ref):
        g = p_ref[0]
        for k in range(1, n):
            g = g + p_ref[k]
        new_m = ADAM_B1 * m_ref[...] + (1.0 - ADAM_B1) * g
        new_v = ADAM_B2 * v_ref[...] + (1.0 - ADAM_B2) * (g * g)
        m_hat = new_m / c1
        v_hat = new_v / c2
        g_ref[...] = g
        d_ref[...] = -ADAM_LR * (m_hat / (jnp.sqrt(v_hat) + ADAM_EPS) + ADAM_WD * w_ref[...])
        nm_ref[...] = new_m
        nv_ref[...] = new_v

    spec = pl.BlockSpec((tr, tw), lambda i, j: (i, j))
    shape = jax.ShapeDtypeStruct((r, cols), F32)
    return pl.pallas_call(
        body, name=name, grid=(r // tr, cols // tw),
        in_specs=[pl.BlockSpec((n, tr, tw), lambda i, j: (0, i, j)), spec, spec, spec],
        out_specs=[spec] * 4, out_shape=[shape] * 4,
        compiler_params=_params(("parallel", "parallel"), VMEM_LIMIT),
    )(parts, w, m, v)


def _rope_tables(t):
    half = HEAD_DIM // 2
    inv = 1.0 / (ROPE_THETA ** (jnp.arange(half, dtype=F32) / half))
    ang = jnp.arange(t, dtype=jnp.int32).astype(F32)[:, None] * inv[None, :]
    cos, sin = jnp.cos(ang), jnp.sin(ang)
    return jnp.concatenate([cos, cos], axis=1), jnp.concatenate([-sin, sin], axis=1)


IN_KERNEL = IN_MAIN + LR_PAD


def _to_kernel_rows(w_t):
    order = sorted(SEGMENTS.values())
    pad = jnp.zeros((LR_PAD - (IN_TOTAL - IN_MAIN), w_t.shape[1]), w_t.dtype)
    return jnp.concatenate([w_t[src:src + width] for _, src, width in order] + [w_t[IN_MAIN:IN_TOTAL], pad], axis=0)


def _pack(pieces):
    flat = []
    for p in pieces:
        p = p.reshape(-1)
        flat.append(jnp.pad(p, (0, (-p.shape[0]) % 128)))
    return jnp.concatenate(flat).reshape(-1, 128)


def _unpack(packed, shapes):
    flat = packed.reshape(-1)
    out, off = [], 0
    for s in shapes:
        size = 1
        for dim in s:
            size *= dim
        out.append(flat[off:off + size].reshape(s))
        off += size + (-size) % 128
    return out


def _local_step(xs, target, norm1_g, w_in_k, gq, gk, attn_sink, w2, ba2, gla_norm_g, w_out_full, norm2_g,
                w_up8, cw_g, cw_v, cb_g, cb_v, w_down_full, rs=None):
    t = xs.shape[0]
    tm = min(1024, t)
    tall = min(2048, t)
    cos, sin_signed = _rope_tables(t)
    sink = attn_sink.reshape(ATTN_HEADS)

    h1 = _rmsnorm_fwd(xs, norm1_g, name="norm1_fwd")
    proj = _matmul(h1, w_in_k, "nt", tm=tall, tn=IN_MAIN // 4, tk=D_MODEL, n_out=IN_MAIN, name="proj_main")
    proj_lr = _matmul(h1, w_in_k[IN_MAIN:], "nt", tm=tm, tn=LR_PAD, tk=D_MODEL, name="proj_lr")
    qa, ka, va = _attn_prep_fwd(proj, cos, sin_signed, gq, gk)
    o_attn = _attn_fwd(qa, ka, va, sink)
    g_dec = _gla_prep_fwd(proj_lr, w2, ba2)
    o_f, o_b, s_f, s_b = _gla_fwd(proj, g_dec)
    o_gla = _gla_out_fwd(o_f, o_b, proj, gla_norm_g)
    x1 = _out_proj(o_attn, o_gla, w_out_full, xs, tm=tall, tn=512)
    h2 = _rmsnorm_fwd(x1, norm2_g, name="norm2_fwd")
    u = _up_proj(h2, w_up8, tm=tm)
    act = _ffn_mid_fwd(u, cw_g, cw_v, cb_g, cb_v)
    loss_part, dy, dy_b = _down_proj_loss(act, w_down_full, x1, target, tm=tm, tn=512)

    d_act = _matmul(dy_b, w_down_full, "nt", tm=tall, tn=D_FF // 4, tk=D_MODEL, out_dtype=BF16, name="d_act")
    dw_down = _matmul(act, dy_b, "tn", tm=D_FF // 4, tn=512, tk=t, name="dw_down")
    if rs is not None:
        d_act = rs.push("w_down", [dw_down.reshape(N_DEV, D_FF // N_DEV, D_MODEL)], [(64, D_MODEL)], d_act)
    du, dcw_g, dcw_v, dcb_g, dcb_v = _ffn_mid_bwd(u, cw_g, cw_v, cb_g, cb_v, d_act)
    dw_up8 = _up_proj_dw(h2, du, tm=512, tk=t)
    if rs is not None:
        du = rs.push("w_up", [dw_up8], [(256, UP_BLOCK)], du)
    dh2 = _up_proj_dx(du, w_up8, tm=tm, tn=1024)
    dx1, dx1_b, d_norm2 = _rmsnorm_bwd(x1, norm2_g, dh2, dy, name="norm2_bwd")
    dmix = _matmul(dx1_b, w_out_full, "nt", tm=tall, tn=1024, tk=D_MODEL, name="d_mix")
    dw_out = _out_proj_dw(o_attn, o_gla, dx1_b, tn=512)
    if rs is not None:
        dmix = rs.push("w_out", [dw_out.reshape(N_DEV, D_MODEL // N_DEV, D_MODEL)], [(256, D_MODEL)], dmix)
    do_gla, d_gate, d_gla_norm = _gla_out_bwd(o_f, o_b, proj, gla_norm_g, dmix)
    (dq_f, dk_f, dv_f, dg_f, dq_b, dk_b, dv_b, dg_b) = _gla_bwd(proj, g_dec, s_f, s_b, do_gla)
    d_lr, d_w2, d_ba2 = _gla_prep_bwd(proj_lr, w2, ba2, dg_f, dg_b)
    dqa, dk_lo, dk_mid, dk_hi, dv_lo, dv_mid, dv_hi, d_sink8 = _attn_bwd(qa, ka, va, sink, dmix)
    d_qa, d_ka, d_va, d_qn, d_kn = _attn_prep_bwd(proj, cos, sin_signed, gq, gk, dqa,
                                                  (dk_lo, dk_mid, dk_hi), (dv_lo, dv_mid, dv_hi))
    d_seg = {"qa": d_qa, "gate": d_gate, "vg": (dv_f + dv_b).astype(BF16), "qg": (dq_f + dq_b).astype(BF16),
             "kg": (dk_f + dk_b).astype(BF16), "ka": d_ka, "va": d_va}
    d_proj = jnp.concatenate([d_seg[k] for k in sorted(SEGMENTS, key=lambda k: SEGMENTS[k][0])] + [d_lr], axis=1)
    dw_in_t = _in_proj_dw_lr(_in_proj_dw(d_proj, h1, tn=1024), d_lr, h1)
    if rs is not None:
        per_in, per_wa = IN_TOTAL // N_DEV, GLA_KEY_WIDTH // N_DEV
        dconv_w = jnp.concatenate([dcw_g, dcw_v], axis=1)
        d_wa2_f = d_w2[:GLA_RANK, :GLA_KEY_WIDTH]
        d_wa2_b = d_w2[GLA_RANK:2 * GLA_RANK, GLA_KEY_WIDTH:]
        small_grad = jnp.stack([
            _pack([dconv_w[:, d * UP_BLOCK:(d + 1) * UP_BLOCK], d_wa2_f[:, d * per_wa:(d + 1) * per_wa],
                   d_wa2_b[:, d * per_wa:(d + 1) * per_wa]]) for d in range(N_DEV)])
        d_proj, d_lr = rs.push("w_in", [dw_in_t.reshape(N_DEV, per_in, D_MODEL), small_grad],
                               [(per_in, 512), small_grad.shape[1:]], (d_proj, d_lr))
    dh1 = _matmul(d_proj, w_in_k, "nn", tm=tm, tn=512, tk=IN_KERNEL, name="dh1")
    if rs is not None:
        dh1 = rs.push(None, [], [], dh1)
    grad_x, _, d_norm1 = _rmsnorm_bwd(xs, norm1_g, dh1, dx1, name="norm1_bwd")
    return (loss_part, grad_x, dw_in_t, dw_out, dw_up8, dw_down, dcw_g, dcw_v, dcb_g, dcb_v,
            d_w2, d_ba2, d_norm1, d_norm2, d_qn, d_kn, d_sink8, d_gla_norm)


def kernel(x, norm1_g, w_in, attn_q_norm_g, attn_k_norm_g, attn_sink, gla_wa2_fwd, gla_ba_fwd, gla_wa2_bwd, gla_ba_bwd, gla_out_norm_g, w_out, norm2_g, w_up, conv_w, conv_b, w_down, loss_target, m_norm1_g, m_w_in, m_attn_q_norm_g, m_attn_k_norm_g, m_attn_sink, m_gla_wa2_fwd, m_gla_ba_fwd, m_gla_wa2_bwd, m_gla_ba_bwd, m_gla_out_norm_g, m_w_out, m_norm2_g, m_w_up, m_conv_w, m_conv_b, m_w_down, v_norm1_g, v_w_in, v_attn_q_norm_g, v_attn_k_norm_g, v_attn_sink, v_gla_wa2_fwd, v_gla_ba_fwd, v_gla_wa2_bwd, v_gla_ba_bwd, v_gla_out_norm_g, v_w_out, v_norm2_g, v_w_up, v_conv_w, v_conv_b, v_w_down):
    t = x.shape[1]
    xs = x.reshape(t, D_MODEL)
    target = loss_target.reshape(t, D_MODEL)
    core = lax.axis_index("c").astype(jnp.int32).reshape(1)

    sharded_small = [conv_w[0], gla_wa2_fwd[0], gla_wa2_bwd[0]]
    small_shapes = [s.shape for s in sharded_small]
    w_in_t, m_in_t, v_in_t = (jnp.swapaxes(a[0], 0, 1) for a in (w_in, m_w_in, v_w_in))
    g_in, g_small = _all_gather([w_in_t.astype(BF16), _pack(sharded_small)], name="gather_w_in")
    g_in, later = lax.optimization_barrier(
        (g_in, [w_out[0].astype(BF16), w_up[0].astype(BF16), w_down[0].astype(BF16)]))
    g_out, w_up8, g_down = _all_gather(later, name="gather_later_weights", collective_id=1)
    w_in_k = _to_kernel_rows(g_in.reshape(IN_TOTAL, D_MODEL))
    w_out_full = g_out.reshape(D_MODEL, D_MODEL)
    w_down_full = g_down.reshape(D_FF, D_MODEL)
    small_full = [_unpack(g_small[d], small_shapes) for d in range(N_DEV)]
    conv_w_full = jnp.concatenate([s[0] for s in small_full], axis=1)
    wa2_f = jnp.concatenate([s[1] for s in small_full], axis=1)
    wa2_b = jnp.concatenate([s[2] for s in small_full], axis=1)
    cw_g, cw_v = conv_w_full[:, :D_FF], conv_w_full[:, D_FF:]
    cb_g, cb_v = conv_b[:, :D_FF], conv_b[:, D_FF:]
    w2 = jnp.zeros((LR_PAD, 2 * GLA_KEY_WIDTH), F32)
    w2 = w2.at[:GLA_RANK, :GLA_KEY_WIDTH].set(wa2_f).at[GLA_RANK:2 * GLA_RANK, GLA_KEY_WIDTH:].set(wa2_b)
    ba2 = jnp.concatenate([gla_ba_fwd, gla_ba_bwd], axis=1)
    rs = _ReduceScatter(core)
    (loss_part, grad_x, _, _, _, _, _, _, dcb_g, dcb_v, _, d_ba2,
     d_norm1, d_norm2, d_qn, d_kn, d_sink8, d_gla_norm) = _local_step(
        xs, target, norm1_g, w_in_k, attn_q_norm_g, attn_k_norm_g, attn_sink, w2, ba2, gla_out_norm_g,
        w_out_full, norm2_g, w_up8, cw_g, cw_v, cb_g, cb_v, w_down_full, rs=rs)

    (part_down,), (part_up,), (part_out,) = rs.result("w_down"), rs.result("w_up"), rs.result("w_out")
    part_in, part_small = rs.result("w_in")
    m_small = _pack([m_conv_w[0], m_gla_wa2_fwd[0], m_gla_wa2_bwd[0]])
    v_small = _pack([v_conv_w[0], v_gla_wa2_fwd[0], v_gla_wa2_bwd[0]])
    upd_in = _adamw(part_in, w_in_t, m_in_t, v_in_t, tile=(IN_TOTAL // N_DEV, 512), name="adamw_w_in")
    upd_in = [jnp.swapaxes(u, 0, 1) for u in upd_in]
    upd_out = _adamw(part_out, w_out[0], m_w_out[0], v_w_out[0], tile=(256, D_MODEL), name="adamw_w_out")
    upd_up = _adamw(part_up, w_up[0], m_w_up[0], v_w_up[0], tile=(256, UP_BLOCK), name="adamw_w_up")
    upd_down = _adamw(part_down, w_down[0], m_w_down[0], v_w_down[0], tile=(64, D_MODEL), name="adamw_w_down")
    upd_small = _adamw(part_small, _pack(sharded_small), m_small, v_small, tile=part_small.shape[1:],
                       name="adamw_small")
    upd_small = [_unpack(u, small_shapes) for u in upd_small]

    rep_names = ["norm1_g", "attn_q_norm_g", "attn_k_norm_g", "attn_sink", "gla_ba_fwd", "gla_ba_bwd",
                 "gla_out_norm_g", "norm2_g", "conv_b"]
    rep_w = [norm1_g, attn_q_norm_g, attn_k_norm_g, attn_sink, gla_ba_fwd, gla_ba_bwd, gla_out_norm_g, norm2_g, conv_b]
    rep_m = [m_norm1_g, m_attn_q_norm_g, m_attn_k_norm_g, m_attn_sink, m_gla_ba_fwd, m_gla_ba_bwd,
             m_gla_out_norm_g, m_norm2_g, m_conv_b]
    rep_v = [v_norm1_g, v_attn_q_norm_g, v_attn_k_norm_g, v_attn_sink, v_gla_ba_fwd, v_gla_ba_bwd,
             v_gla_out_norm_g, v_norm2_g, v_conv_b]
    d_sink = d_sink8[:, :GQA_GROUP, 0].reshape(1, ATTN_HEADS)
    rep_g = [d_norm1, d_qn, d_kn, d_sink, d_ba2[:, :GLA_KEY_WIDTH], d_ba2[:, GLA_KEY_WIDTH:], d_gla_norm, d_norm2,
             jnp.concatenate([dcb_g, dcb_v], axis=1)]
    ride = jnp.ones((1, 128), F32)
    rep_shapes = [w.shape for w in rep_w] + [loss_part.shape]
    (rep_terms,) = _all_gather([_pack(rep_g + [loss_part])], name="gather_small_grads")
    upd_rep = _adamw(rep_terms, _pack(rep_w + [ride]), _pack(rep_m + [ride]), _pack(rep_v + [ride]),
                     tile=rep_terms.shape[1:], name="adamw_replicated")
    upd_rep = [_unpack(u, rep_shapes) for u in upd_rep]
    loss = upd_rep[0][-1][0, 0]

    order = ["norm1_g", "w_in", "attn_q_norm_g", "attn_k_norm_g", "attn_sink", "gla_wa2_fwd", "gla_ba_fwd",
             "gla_wa2_bwd", "gla_ba_bwd", "gla_out_norm_g", "w_out", "norm2_g", "w_up", "conv_w", "conv_b", "w_down"]
    outs = [loss, grad_x.reshape(1, t, D_MODEL)]
    for kind in range(4):
        by_name = {n: upd_rep[kind][i] for i, n in enumerate(rep_names)}
        by_name["w_in"] = upd_in[kind][None]
        by_name["w_out"] = upd_out[kind][None]
        by_name["w_up"] = upd_up[kind][None]
        by_name["w_down"] = upd_down[kind][None]
        by_name["conv_w"] = upd_small[kind][0][None]
        by_name["gla_wa2_fwd"] = upd_small[kind][1][None]
        by_name["gla_wa2_bwd"] = upd_small[kind][2][None]
        outs += [by_name[n] for n in order]
    return tuple(outs)
```

```python
import functools

import jax
import jax.numpy as jnp
from jax import lax
from jax.experimental import pallas as pl
from jax.experimental.pallas import tpu as pltpu
from jax.experimental.pallas import tpu_sc as plsc

F32 = jnp.float32
BF16 = jnp.bfloat16

D_MODEL = 2048
HEAD_DIM = 128
ATTN_WIDTH = 1024
ATTN_HEADS = 8
KV_HEADS = 2
GQA_GROUP = 4
KV_WIDTH = KV_HEADS * HEAD_DIM
ATTN_BLOCK = 128
WINDOW = 128
ROPE_THETA = 10000.0
GLA_HEADS = 4
GLA_DK = 128
GLA_DV = 256
GLA_KEY_WIDTH = 512
GLA_WIDTH = 1024
GLA_RANK = 16
GLA_GATE_NORMALIZER = 16.0
GLA_CHUNK = 64
D_FF = 5632
NORM_EPS = 1e-6
IN_TOTAL = 4640
IN_MAIN = 4608
LR_PAD = 128
N_DEV = 8
N_CHIP = 4

ADAM_LR = 0.001
ADAM_B1 = 0.9
ADAM_B2 = 0.999
ADAM_EPS = 1e-08
ADAM_WD = 0.01
ADAM_STEP = 10

SEGMENTS = {
    "qa": (0, 0, 1024),
    "gate": (1024, 3584, 1024),
    "vg": (2048, 2560, 1024),
    "qg": (3072, 1536, 512),
    "kg": (3584, 2048, 512),
    "ka": (4096, 1024, 256),
    "va": (4352, 1280, 256),
}

VMEM_LIMIT = 56 * 1024 * 1024
MESH = pl.DeviceIdType.MESH


def _params(semantics=None, vmem=None):
    return pltpu.CompilerParams(dimension_semantics=semantics, vmem_limit_bytes=vmem)


_DIMS = {
    "nn": (((1,), (0,)), ((), ())),
    "nt": (((1,), (1,)), ((), ())),
    "tn": (((0,), (0,)), ((), ())),
}


def _mxu(a, b, mode):
    return lax.dot_general(a.astype(BF16), b.astype(BF16), _DIMS[mode], preferred_element_type=F32)


@functools.partial(jax.custom_vjp, nondiff_argnums=(2,))
def bdot(a, b, mode):
    return _mxu(a, b, mode)


def _bdot_fwd(a, b, mode):
    return _mxu(a, b, mode), (a, b)


def _bdot_bwd(mode, res, g):
    a, b = res
    if mode == "nn":
        return _mxu(g, b, "nt"), _mxu(a, g, "tn")
    if mode == "nt":
        return _mxu(g, b, "nn"), _mxu(g, a, "tn")
    return _mxu(b, g, "nt"), _mxu(a, g, "nn")


bdot.defvjp(_bdot_fwd, _bdot_bwd)


def _rms(x, g):
    return x * lax.rsqrt(jnp.mean(x * x, axis=-1, keepdims=True) + NORM_EPS) * g


def _rope(x, cos, sin_signed):
    return x * cos + pltpu.roll(x, HEAD_DIM // 2, 1) * sin_signed


def _rope_transposed(d, cos, sin_signed):
    return d * cos + pltpu.roll(d * sin_signed, HEAD_DIM // 2, 1)


def _silu(x):
    return x * jax.nn.sigmoid(x)


def _log_sigmoid(z):
    return -(jnp.maximum(-z, 0.0) + jnp.log(1.0 + jnp.exp(-jnp.abs(z))))


def _matmul_call(args, in_specs, o_spec, out_shape, grid, mode, nk, acc_shape, *, name, has_res=False,
                 prefetch=None, load_b=lambda ref: ref[...]):
    dims = _DIMS[mode]
    out_dtype = out_shape.dtype
    n_pre = 0 if prefetch is None else 1

    def body(*refs):
        refs = refs[n_pre:]
        if has_res:
            a_ref, b_ref, r_ref, o_ref = refs[:4]
            rest = refs[4:]
        else:
            a_ref, b_ref, o_ref = refs[:3]
            r_ref = None
            rest = refs[3:]
        part = lax.dot_general(a_ref[...], load_b(b_ref), dims, preferred_element_type=F32)

        def finish(acc):
            if r_ref is not None:
                acc = acc + r_ref[...]
            o_ref[...] = acc.astype(out_dtype)

        if nk == 1:
            finish(part)
        else:
            acc_ref = rest[0]
            kk = pl.program_id(2)

            @pl.when(kk == 0)
            def _():
                acc_ref[...] = part

            @pl.when(kk > 0)
            def _():
                acc_ref[...] += part

            @pl.when(kk == nk - 1)
            def _():
                finish(acc_ref[...])

    scratch = [pltpu.VMEM(acc_shape, F32)] if nk > 1 else []
    params = _params(("parallel", "parallel", "arbitrary"), VMEM_LIMIT)
    if prefetch is None:
        return pl.pallas_call(body, name=name, grid=grid, in_specs=in_specs, out_specs=o_spec, out_shape=out_shape,
                              scratch_shapes=scratch, compiler_params=params)(*args)
    return pl.pallas_call(
        body, name=name,
        grid_spec=pltpu.PrefetchScalarGridSpec(num_scalar_prefetch=1, grid=grid, in_specs=in_specs,
                                               out_specs=o_spec, scratch_shapes=scratch),
        out_shape=out_shape, compiler_params=params)(prefetch, *args)


def _matmul(a, b, mode, *, tm, tn, tk, out_dtype=F32, res=None, name, n_out=None):
    if mode == "nn":
        (m, k), (k2, n) = a.shape, b.shape
    elif mode == "nt":
        (m, k), (n, k2) = a.shape, b.shape
    else:
        (k, m), (k2, n) = a.shape, b.shape
    n = n if n_out is None else n_out
    assert k == k2 and m % tm == 0 and n % tn == 0 and k % tk == 0, (name, a.shape, b.shape, tm, tn, tk)
    if mode == "tn":
        a_spec = pl.BlockSpec((tk, tm), lambda i, j, kk: (kk, i))
    else:
        a_spec = pl.BlockSpec((tm, tk), lambda i, j, kk: (i, kk))
    if mode == "nt":
        b_spec = pl.BlockSpec((tn, tk), lambda i, j, kk: (j, kk))
    else:
        b_spec = pl.BlockSpec((tk, tn), lambda i, j, kk: (kk, j))
    o_spec = pl.BlockSpec((tm, tn), lambda i, j, kk: (i, j))
    in_specs, args = [a_spec, b_spec], [a, b]
    if res is not None:
        in_specs.append(o_spec)
        args.append(res)
    return _matmul_call(args, in_specs, o_spec, jax.ShapeDtypeStruct((m, n), out_dtype),
                        (m // tm, n // tn, k // tk), mode, k // tk, (tm, tn), name=name, has_res=res is not None)


def _out_proj(o_attn, o_gla, w_out, x, *, tm, tn):
    t, ka = o_attn.shape
    kg = o_gla.shape[1]

    def body(a_ref, g_ref, w_ref, x_ref, o_ref):
        acc = lax.dot_general(a_ref[...], w_ref[:ka], _DIMS["nn"], preferred_element_type=F32)
        acc = acc + lax.dot_general(g_ref[...], w_ref[ka:], _DIMS["nn"], preferred_element_type=F32)
        o_ref[...] = acc + x_ref[...]

    tile = pl.BlockSpec((tm, tn), lambda i, j: (i, j))
    return pl.pallas_call(
        body, name="out_proj", grid=(t // tm, D_MODEL // tn),
        in_specs=[pl.BlockSpec((tm, ka), lambda i, j: (i, 0)), pl.BlockSpec((tm, kg), lambda i, j: (i, 0)),
                  pl.BlockSpec((ka + kg, tn), lambda i, j: (0, j)), tile],
        out_specs=tile, out_shape=jax.ShapeDtypeStruct((t, D_MODEL), F32),
        compiler_params=_params(("parallel", "parallel"), VMEM_LIMIT),
    )(o_attn, o_gla, w_out, x)


def _out_proj_dw(o_attn, o_gla, dx1, *, tn):
    t, ka = o_attn.shape
    assert o_gla.shape == (t, ka)

    def body(a_ref, g_ref, d_ref, o_ref):
        @pl.when(pl.program_id(0) == 0)
        def _():
            o_ref[...] = lax.dot_general(a_ref[...], d_ref[...], _DIMS["tn"], preferred_element_type=F32)

        @pl.when(pl.program_id(0) == 1)
        def _():
            o_ref[...] = lax.dot_general(g_ref[...], d_ref[...], _DIMS["tn"], preferred_element_type=F32)

    whole = pl.BlockSpec((t, ka), lambda i, j: (0, 0))
    return pl.pallas_call(
        body, name="dw_out", grid=(2, D_MODEL // tn),
        in_specs=[whole, whole, pl.BlockSpec((t, tn), lambda i, j: (0, j))],
        out_specs=pl.BlockSpec((ka, tn), lambda i, j: (i, j)),
        out_shape=jax.ShapeDtypeStruct((2 * ka, D_MODEL), F32),
        compiler_params=_params(("parallel", "parallel"), VMEM_LIMIT),
    )(o_attn, o_gla, dx1)


UP_BLOCK = 2 * D_FF // N_DEV


def _up_proj(h2, w_up8, *, tm):
    t = h2.shape[0]
    return _matmul_call(
        [h2, w_up8],
        [pl.BlockSpec((tm, D_MODEL), lambda i, j, kk: (i, 0)),
         pl.BlockSpec((None, D_MODEL, UP_BLOCK), lambda i, j, kk: (j, 0, 0))],
        pl.BlockSpec((None, tm, UP_BLOCK), lambda i, j, kk: (j // N_CHIP, i, j % N_CHIP)),
        jax.ShapeDtypeStruct((2, t, D_FF), F32), (t // tm, N_DEV, 1), "nn", 1, None, name="up_proj")


def _up_proj_dx(du, w_up8, *, tm, tn):
    t = du.shape[1]
    pair = 2
    return _matmul_call(
        [du, w_up8],
        [pl.BlockSpec((None, tm, pair * UP_BLOCK), lambda i, j, kk: (kk // 2, i, kk % 2)),
         pl.BlockSpec((pair, tn, UP_BLOCK), lambda i, j, kk: (kk, j, 0))],
        pl.BlockSpec((tm, tn), lambda i, j, kk: (i, j)),
        jax.ShapeDtypeStruct((t, D_MODEL), F32), (t // tm, D_MODEL // tn, N_DEV // pair), "nt", N_DEV // pair,
        (tm, tn), name="up_proj_dx", load_b=lambda ref: jnp.concatenate([ref[0], ref[1]], axis=1))


def _up_proj_dw(h2, du, *, tm, tk):
    t = h2.shape[0]
    return _matmul_call(
        [h2, du],
        [pl.BlockSpec((tk, tm), lambda j, i, kk: (kk, i)),
         pl.BlockSpec((None, tk, UP_BLOCK), lambda j, i, kk: (j // N_CHIP, kk, j % N_CHIP))],
        pl.BlockSpec((None, tm, UP_BLOCK), lambda j, i, kk: (j, i, 0)),
        jax.ShapeDtypeStruct((N_DEV, D_MODEL, UP_BLOCK), F32), (N_DEV, D_MODEL // tm, t // tk), "tn", t // tk,
        (tm, UP_BLOCK), name="up_proj_dw")


IN_TILE = 512


def _in_proj_dw(d_proj, h1, *, tn):
    t = h1.shape[0]
    table = []
    for tile in range(IN_MAIN // IN_TILE):
        dst, src, _ = max(s for s in SEGMENTS.values() if s[0] <= tile * IN_TILE)
        assert (src + tile * IN_TILE - dst) % IN_TILE == 0
        table.append((src + tile * IN_TILE - dst) // IN_TILE)
    assert sorted(table) == list(range(IN_MAIN // IN_TILE))
    return _matmul_call(
        [d_proj, h1],
        [pl.BlockSpec((t, IN_TILE), lambda j, i, kk, tab: (0, i)),
         pl.BlockSpec((t, tn), lambda j, i, kk, tab: (0, j))],
        pl.BlockSpec((IN_TILE, tn), lambda j, i, kk, tab: (tab[i], j)),
        jax.ShapeDtypeStruct((IN_TOTAL, D_MODEL), F32), (D_MODEL // tn, IN_MAIN // IN_TILE, 1), "tn", 1, None,
        name="in_proj_dw", prefetch=jnp.asarray(table, jnp.int32))


def _in_proj_dw_lr(dw_t, d_lr, h1):
    t = h1.shape[0]
    n_lr = IN_TOTAL - IN_MAIN
    tn = 512

    def body(dw_ref, dlr_ref, h1_ref, out_ref):
        full = lax.dot_general(dlr_ref[...], h1_ref[...], _DIMS["tn"], preferred_element_type=F32)
        out_ref[...] = full[:n_lr]

    return pl.pallas_call(
        body, name="in_proj_dw_lr", grid=(D_MODEL // tn,),
        in_specs=[pl.BlockSpec(memory_space=pl.ANY),
                  pl.BlockSpec((t, LR_PAD), lambda j: (0, 0)),
                  pl.BlockSpec((t, tn), lambda j: (0, j))],
        out_specs=pl.BlockSpec((n_lr, tn), lambda j: (IN_MAIN // n_lr, j)),
        out_shape=jax.ShapeDtypeStruct(dw_t.shape, F32),
        input_output_aliases={0: 0},
        compiler_params=_params(("parallel",), VMEM_LIMIT),
    )(dw_t, d_lr, h1)


def _rmsnorm_fwd(x, g, *, name, tr=512):
    t, d = x.shape

    def body(x_ref, g_ref, h_ref):
        h_ref[...] = _rms(x_ref[...], g_ref[...]).astype(BF16)

    return pl.pallas_call(
        body, name=name, grid=(t // tr,),
        in_specs=[pl.BlockSpec((tr, d), lambda i: (i, 0)), pl.BlockSpec((1, d), lambda i: (0, 0))],
        out_specs=pl.BlockSpec((tr, d), lambda i: (i, 0)),
        out_shape=jax.ShapeDtypeStruct((t, d), BF16),
        compiler_params=_params(("parallel",), VMEM_LIMIT),
    )(x, g)


def _rmsnorm_bwd(x, g, dh, dres, *, name, tr=256):
    t, d = x.shape

    def body(x_ref, g_ref, dh_ref, dres_ref, dx_ref, dxb_ref, dg_ref):
        _, vjp = jax.vjp(_rms, x_ref[...], g_ref[...])
        dx, dg = vjp(dh_ref[...])
        dx = dx + dres_ref[...]
        dx_ref[...] = dx
        dxb_ref[...] = dx.astype(BF16)

        @pl.when(pl.program_id(0) == 0)
        def _():
            dg_ref[...] = jnp.zeros_like(dg_ref)

        dg_ref[...] += dg

    row = pl.BlockSpec((tr, d), lambda i: (i, 0))
    vec = pl.BlockSpec((1, d), lambda i: (0, 0))
    return pl.pallas_call(
        body, name=name, grid=(t // tr,),
        in_specs=[row, vec, row, row],
        out_specs=[row, row, vec],
        out_shape=[jax.ShapeDtypeStruct((t, d), F32), jax.ShapeDtypeStruct((t, d), BF16),
                   jax.ShapeDtypeStruct((1, d), F32)],
        compiler_params=_params(("arbitrary",), VMEM_LIMIT),
    )(x, g, dh, dres)


def _seg_block(name, width):
    off = SEGMENTS[name][0]
    assert off % width == 0
    return off // width


def _attn_prep_fwd(proj, cos, sin_signed, gq, gk, *, tr=256):
    t = proj.shape[0]

    def body(q_ref, k_ref, v_ref, cos_ref, sin_ref, gq_ref, gk_ref, qo_ref, ko_ref, vo_ref):
        cos_t, sin_t = cos_ref[...], sin_ref[...]
        for h in range(ATTN_HEADS):
            cols = slice(h * HEAD_DIM, (h + 1) * HEAD_DIM)
            qo_ref[:, cols] = _rope(_rms(q_ref[:, cols], gq_ref[...]), cos_t, sin_t).astype(BF16)
        for h in range(KV_HEADS):
            cols = slice(h * HEAD_DIM, (h + 1) * HEAD_DIM)
            ko_ref[:, cols] = _rope(_rms(k_ref[:, cols], gk_ref[...]), cos_t, sin_t).astype(BF16)
        vo_ref[...] = v_ref[...].astype(BF16)

    qb, kb, vb = _seg_block("qa", ATTN_WIDTH), _seg_block("ka", KV_WIDTH), _seg_block("va", KV_WIDTH)
    tab = pl.BlockSpec((tr, HEAD_DIM), lambda i: (i, 0))
    vec = pl.BlockSpec((1, HEAD_DIM), lambda i: (0, 0))
    return pl.pallas_call(
        body, name="attn_prep_fwd", grid=(t // tr,),
        in_specs=[pl.BlockSpec((tr, ATTN_WIDTH), lambda i: (i, qb)),
                  pl.BlockSpec((tr, KV_WIDTH), lambda i: (i, kb)),
                  pl.BlockSpec((tr, KV_WIDTH), lambda i: (i, vb)),
                  tab, tab, vec, vec],
        out_specs=[pl.BlockSpec((tr, ATTN_WIDTH), lambda i: (i, 0)),
                   pl.BlockSpec((tr, KV_WIDTH), lambda i: (i, 0)),
                   pl.BlockSpec((tr, KV_WIDTH), lambda i: (i, 0))],
        out_shape=[jax.ShapeDtypeStruct((t, ATTN_WIDTH), BF16),
                   jax.ShapeDtypeStruct((t, KV_WIDTH), BF16),
                   jax.ShapeDtypeStruct((t, KV_WIDTH), BF16)],
        compiler_params=_params(("parallel",), VMEM_LIMIT),
    )(proj, proj, proj, cos, sin_signed, gq, gk)


def _attn_heads(q, kcat, vcat, sink_col, valid):
    s = bdot(q, kcat, "nt") * (HEAD_DIM ** -0.5)
    s = jnp.where(valid, s, -jnp.inf)
    m = lax.stop_gradient(jnp.maximum(jnp.max(s, axis=-1, keepdims=True), sink_col))
    p = jnp.exp(s - m)
    p = p / (jnp.sum(p, axis=-1, keepdims=True) + jnp.exp(sink_col - m))
    return bdot(p, vcat, "nn")


def _attn_valid(n, t):
    shape = (GQA_GROUP * ATTN_BLOCK, 3 * ATTN_BLOCK)
    qi = lax.broadcasted_iota(jnp.int32, shape, 0) % ATTN_BLOCK
    sj = lax.broadcasted_iota(jnp.int32, shape, 1)
    kpos = n * ATTN_BLOCK - ATTN_BLOCK + sj
    return (jnp.abs(sj - ATTN_BLOCK - qi) <= WINDOW) & (kpos >= 0) & (kpos < t)


def _head_rows(g):
    return slice(g * ATTN_BLOCK, (g + 1) * ATTN_BLOCK)


def _head_cols(g):
    return slice(g * HEAD_DIM, (g + 1) * HEAD_DIM)


def _stack_heads(ref):
    return jnp.concatenate([ref[:, _head_cols(g)] for g in range(GQA_GROUP)], axis=0).astype(F32)


def _sink_column(sink_ref, h):
    return jnp.concatenate([jnp.full((ATTN_BLOCK, 1), sink_ref[h * GQA_GROUP + g], F32)
                            for g in range(GQA_GROUP)], axis=0)


def _attn_specs(nb):
    q_spec = pl.BlockSpec((ATTN_BLOCK, GQA_GROUP * HEAD_DIM), lambda h, n: (n, h))
    kv_specs = [
        pl.BlockSpec((ATTN_BLOCK, HEAD_DIM), lambda h, n: (jnp.maximum(n - 1, 0), h)),
        pl.BlockSpec((ATTN_BLOCK, HEAD_DIM), lambda h, n: (n, h)),
        pl.BlockSpec((ATTN_BLOCK, HEAD_DIM), lambda h, n: (jnp.minimum(n + 1, nb - 1), h)),
    ]
    return q_spec, kv_specs


def _attn_fwd(q, k, v, sink):
    t = q.shape[0]
    nb = t // ATTN_BLOCK

    def body(sink_ref, q_ref, kp_ref, kc_ref, kn_ref, vp_ref, vc_ref, vn_ref, o_ref):
        h, n = pl.program_id(0), pl.program_id(1)
        valid = _attn_valid(n, t)
        kcat = jnp.concatenate([kp_ref[...], kc_ref[...], kn_ref[...]], axis=0).astype(F32)
        vcat = jnp.concatenate([vp_ref[...], vc_ref[...], vn_ref[...]], axis=0).astype(F32)
        o = _attn_heads(_stack_heads(q_ref), kcat, vcat, _sink_column(sink_ref, h), valid).astype(BF16)
        for g in range(GQA_GROUP):
            o_ref[:, _head_cols(g)] = o[_head_rows(g)]

    q_spec, kv_specs = _attn_specs(nb)
    return pl.pallas_call(
        body, name="attn_fwd", grid=(KV_HEADS, nb),
        in_specs=[pl.BlockSpec(memory_space=pltpu.SMEM), q_spec] + kv_specs + kv_specs,
        out_specs=q_spec,
        out_shape=jax.ShapeDtypeStruct((t, ATTN_WIDTH), BF16),
        compiler_params=_params(("parallel", "parallel"), VMEM_LIMIT),
    )(sink, q, k, k, k, v, v, v)


def _attn_bwd(q, k, v, sink, dmix):
    t = q.shape[0]
    nb = t // ATTN_BLOCK

    def body(sink_ref, q_ref, kp_ref, kc_ref, kn_ref, vp_ref, vc_ref, vn_ref, do_ref,
             dq_ref, dk_lo, dk_mid, dk_hi, dv_lo, dv_mid, dv_hi, dsink_ref):
        h, n = pl.program_id(0), pl.program_id(1)
        valid = _attn_valid(n, t)
        kcat = jnp.concatenate([kp_ref[...], kc_ref[...], kn_ref[...]], axis=0).astype(F32)
        vcat = jnp.concatenate([vp_ref[...], vc_ref[...], vn_ref[...]], axis=0).astype(F32)
        _, vjp = jax.vjp(functools.partial(_attn_heads, valid=valid),
                         _stack_heads(q_ref), kcat, vcat, _sink_column(sink_ref, h))
        dq, dk, dv, dsink_col = vjp(_stack_heads(do_ref))
        row = lax.broadcasted_iota(jnp.int32, (8, HEAD_DIM), 0)
        dsink = jnp.zeros((8, HEAD_DIM), F32)
        for g in range(GQA_GROUP):
            dq_ref[:, _head_cols(g)] = dq[_head_rows(g)]
            dsink = dsink + jnp.where(row == g, jnp.sum(dsink_col[_head_rows(g)]), 0.0)
        for i, (dk_ref, dv_ref) in enumerate(((dk_lo, dv_lo), (dk_mid, dv_mid), (dk_hi, dv_hi))):
            rows = slice(i * ATTN_BLOCK, (i + 1) * ATTN_BLOCK)
            dk_ref[...] = dk[rows]
            dv_ref[...] = dv[rows]

        @pl.when(n == 0)
        def _():
            dsink_ref[...] = jnp.zeros_like(dsink_ref)

        dsink_ref[...] += dsink

    q_spec, kv_specs = _attn_specs(nb)
    kv_out = pl.BlockSpec((ATTN_BLOCK, HEAD_DIM), lambda h, n: (n, h))
    kv_shape = jax.ShapeDtypeStruct((t, KV_WIDTH), F32)
    return pl.pallas_call(
        body, name="attn_bwd", grid=(KV_HEADS, nb),
        in_specs=[pl.BlockSpec(memory_space=pltpu.SMEM), q_spec] + kv_specs + kv_specs + [q_spec],
        out_specs=[q_spec] + [kv_out] * 6 + [pl.BlockSpec((None, 8, HEAD_DIM), lambda h, n: (h, 0, 0))],
        out_shape=[jax.ShapeDtypeStruct((t, ATTN_WIDTH), F32)] + [kv_shape] * 6
                  + [jax.ShapeDtypeStruct((KV_HEADS, 8, HEAD_DIM), F32)],
        compiler_params=_params(("parallel", "arbitrary"), VMEM_LIMIT),
    )(sink, q, k, k, k, v, v, v, dmix)


def _attn_prep_bwd(proj, cos, sin_signed, gq, gk, dq, dks, dvs):
    t = proj.shape[0]
    tr = ATTN_BLOCK
    nb = t // tr

    def body(q_ref, k_ref, cos_ref, sin_ref, gq_ref, gk_ref, dq_ref,
             dk_lo, dk_mid, dk_hi, dv_lo, dv_mid, dv_hi,
             dqo_ref, dko_ref, dvo_ref, dgq_ref, dgk_ref):
        n = pl.program_id(0)
        cos_t, sin_t = cos_ref[...], sin_ref[...]
        has_next = (n < nb - 1).astype(F32)
        has_prev = (n > 0).astype(F32)
        dk = dk_lo[...] * has_next + dk_mid[...] + dk_hi[...] * has_prev
        dv = dv_lo[...] * has_next + dv_mid[...] + dv_hi[...] * has_prev
        dvo_ref[...] = dv.astype(BF16)
        dgq = jnp.zeros((1, HEAD_DIM), F32)
        dgk = jnp.zeros((1, HEAD_DIM), F32)
        for h in range(ATTN_HEADS):
            cols = slice(h * HEAD_DIM, (h + 1) * HEAD_DIM)
            _, vjp = jax.vjp(_rms, q_ref[:, cols], gq_ref[...])
            dx, dg = vjp(_rope_transposed(dq_ref[:, cols], cos_t, sin_t))
            dqo_ref[:, cols] = dx.astype(BF16)
            dgq = dgq + dg
        for h in range(KV_HEADS):
            cols = slice(h * HEAD_DIM, (h + 1) * HEAD_DIM)
            _, vjp = jax.vjp(_rms, k_ref[:, cols], gk_ref[...])
            dx, dg = vjp(_rope_transposed(dk[:, cols], cos_t, sin_t))
            dko_ref[:, cols] = dx.astype(BF16)
            dgk = dgk + dg

        @pl.when(n == 0)
        def _():
            dgq_ref[...] = jnp.zeros_like(dgq_ref)
            dgk_ref[...] = jnp.zeros_like(dgk_ref)

        dgq_ref[...] += dgq
        dgk_ref[...] += dgk

    qb, kb = _seg_block("qa", ATTN_WIDTH), _seg_block("ka", KV_WIDTH)
    tab = pl.BlockSpec((tr, HEAD_DIM), lambda i: (i, 0))
    vec = pl.BlockSpec((1, HEAD_DIM), lambda i: (0, 0))
    kv = [pl.BlockSpec((tr, KV_WIDTH), lambda i: (jnp.minimum(i + 1, nb - 1), 0)),
          pl.BlockSpec((tr, KV_WIDTH), lambda i: (i, 0)),
          pl.BlockSpec((tr, KV_WIDTH), lambda i: (jnp.maximum(i - 1, 0), 0))]
    wide = pl.BlockSpec((tr, ATTN_WIDTH), lambda i: (i, 0))
    narrow = pl.BlockSpec((tr, KV_WIDTH), lambda i: (i, 0))
    return pl.pallas_call(
        body, name="attn_prep_bwd", grid=(nb,),
        in_specs=[pl.BlockSpec((tr, ATTN_WIDTH), lambda i: (i, qb)),
                  pl.BlockSpec((tr, KV_WIDTH), lambda i: (i, kb)),
                  tab, tab, vec, vec, wide] + kv + kv,
        out_specs=[wide, narrow, narrow, vec, vec],
        out_shape=[jax.ShapeDtypeStruct((t, ATTN_WIDTH), BF16),
                   jax.ShapeDtypeStruct((t, KV_WIDTH), BF16),
                   jax.ShapeDtypeStruct((t, KV_WIDTH), BF16),
                   jax.ShapeDtypeStruct((1, HEAD_DIM), F32),
                   jax.ShapeDtypeStruct((1, HEAD_DIM), F32)],
        compiler_params=_params(("arbitrary",), VMEM_LIMIT),
    )(proj, proj, cos, sin_signed, gq, gk, dq, *dks, *dvs)


def _decay_fn(lr, w2, ba):
    return _log_sigmoid(bdot(lr, w2, "nn") + ba) / GLA_GATE_NORMALIZER


def _gla_prep_fwd(proj_lr, w2, ba2, *, tr=512):
    t = proj_lr.shape[0]
    width = 2 * GLA_KEY_WIDTH

    def body(lr_ref, w2_ref, ba_ref, g_ref):
        g_ref[...] = _decay_fn(lr_ref[...], w2_ref[...], ba_ref[...])

    return pl.pallas_call(
        body, name="gla_prep_fwd", grid=(t // tr,),
        in_specs=[pl.BlockSpec((tr, LR_PAD), lambda i: (i, 0)),
                  pl.BlockSpec((LR_PAD, width), lambda i: (0, 0)),
                  pl.BlockSpec((1, width), lambda i: (0, 0))],
        out_specs=pl.BlockSpec((tr, width), lambda i: (i, 0)),
        out_shape=jax.ShapeDtypeStruct((t, width), F32),
        compiler_params=_params(("parallel",), VMEM_LIMIT),
    )(proj_lr, w2, ba2)


def _gla_prep_bwd(proj_lr, w2, ba2, dg_f, dg_b, *, tr=512):
    t = proj_lr.shape[0]
    width = 2 * GLA_KEY_WIDTH

    def body(lr_ref, w2_ref, ba_ref, dgf_ref, dgb_ref, dlr_ref, dw2_ref, dba_ref):
        _, vjp = jax.vjp(_decay_fn, lr_ref[...], w2_ref[...], ba_ref[...])
        dlr, dw2, dba = vjp(jnp.concatenate([dgf_ref[...], dgb_ref[...]], axis=1))
        dlr_ref[...] = dlr.astype(BF16)

        @pl.when(pl.program_id(0) == 0)
        def _():
            dw2_ref[...] = jnp.zeros_like(dw2_ref)
            dba_ref[...] = jnp.zeros_like(dba_ref)

        dw2_ref[...] += dw2
        dba_ref[...] += dba

    half = pl.BlockSpec((tr, GLA_KEY_WIDTH), lambda i: (i, 0))
    return pl.pallas_call(
        body, name="gla_prep_bwd", grid=(t // tr,),
        in_specs=[pl.BlockSpec((tr, LR_PAD), lambda i: (i, 0)),
                  pl.BlockSpec((LR_PAD, width), lambda i: (0, 0)),
                  pl.BlockSpec((1, width), lambda i: (0, 0)), half, half],
        out_specs=[pl.BlockSpec((tr, LR_PAD), lambda i: (i, 0)),
                   pl.BlockSpec((LR_PAD, width), lambda i: (0, 0)),
                   pl.BlockSpec((1, width), lambda i: (0, 0))],
        out_shape=[jax.ShapeDtypeStruct((t, LR_PAD), BF16),
                   jax.ShapeDtypeStruct((LR_PAD, width), F32),
                   jax.ShapeDtypeStruct((1, width), F32)],
        compiler_params=_params(("arbitrary",), VMEM_LIMIT),
    )(proj_lr, w2, ba2, dg_f, dg_b)


def _gla_k(h):
    return slice(h * GLA_DK, (h + 1) * GLA_DK)


def _gla_v(h):
    return slice(h * GLA_DV, (h + 1) * GLA_DV)


def _running_sum(x, downward):
    n = x.shape[0]
    row = lax.broadcasted_iota(jnp.int32, x.shape, 0)
    step = 1
    while step < n:
        if downward:
            x = x + jnp.where(row >= step, pltpu.roll(x, step, 0), 0.0)
        else:
            x = x + jnp.where(row < n - step, pltpu.roll(x, n - step, 0), 0.0)
        step *= 2
    return x


@functools.partial(jax.custom_vjp, nondiff_argnums=(1,))
def _cumsum_rows(x, downward):
    return _running_sum(x, downward)


def _cumsum_rows_fwd(x, downward):
    return _running_sum(x, downward), None


def _cumsum_rows_bwd(downward, _, ct):
    return (_running_sum(ct, not downward),)


_cumsum_rows.defvjp(_cumsum_rows_fwd, _cumsum_rows_bwd)


def _gla_chunk(q, k, v, g, state, forward):
    c = GLA_CHUNK
    row = lax.broadcasted_iota(jnp.int32, (c, c), 0)
    col = lax.broadcasted_iota(jnp.int32, (c, c), 1)
    rid = lax.broadcasted_iota(jnp.int32, (c, GLA_DK), 0)
    q = q * (GLA_DK ** -0.5)
    if forward:
        see = row >= col
        upto_ref = rid <= c // 2
    else:
        see = row < col
        upto_ref = rid >= c - 1 - c // 2
    b = _cumsum_rows(g, forward)
    b_last = jnp.sum(g, axis=0, keepdims=True)
    b_ref = jnp.sum(jnp.where(upto_ref, g, 0.0), axis=0, keepdims=True)
    a = bdot(q * jnp.exp(b - b_ref), k * jnp.exp(b_ref - b), "nt")
    a = jnp.where(see, a, 0.0)
    o = bdot(a, v, "nn") + bdot(q * jnp.exp(b), state, "nt")
    new_state = state * jnp.exp(b_last) + bdot(v, k * jnp.exp(b_last - b), "tn")
    return o, new_state


def _gla_fwd(proj, g):
    t = proj.shape[0]
    c = GLA_CHUNK
    nchunk = t // c
    qb, kb, vb = _seg_block("qg", GLA_KEY_WIDTH), _seg_block("kg", GLA_KEY_WIDTH), _seg_block("vg", GLA_WIDTH)

    def body(qf, kf, vf, gf, qr, kr, vr, gr, of_ref, ob_ref, sf_ref, sb_ref, state):
        @pl.when(pl.program_id(0) == 0)
        def _():
            state[...] = jnp.zeros_like(state)

        dirs = ((qf, kf, vf, gf, of_ref, sf_ref), (qr, kr, vr, gr, ob_ref, sb_ref))
        args = [(q_ref[:, _gla_k(h)], k_ref[:, _gla_k(h)], v_ref[:, _gla_v(h)], g_ref[:, _gla_k(h)], state[d, h])
                for d, (q_ref, k_ref, v_ref, g_ref, _, _) in enumerate(dirs) for h in range(GLA_HEADS)]
        results = [_gla_chunk(*a, forward=(i < GLA_HEADS)) for i, a in enumerate(args)]
        for i, (a, (o, s_out)) in enumerate(zip(args, results)):
            d, h = divmod(i, GLA_HEADS)
            dirs[d][5][h] = a[4]
            dirs[d][4][:, _gla_v(h)] = o
            state[d, h] = s_out

    specs, outs = [], []
    for d in range(2):
        ci = (lambda i: i) if d == 0 else (lambda i: nchunk - 1 - i)
        specs += [pl.BlockSpec((c, GLA_KEY_WIDTH), lambda i, ci=ci: (ci(i), qb)),
                  pl.BlockSpec((c, GLA_KEY_WIDTH), lambda i, ci=ci: (ci(i), kb)),
                  pl.BlockSpec((c, GLA_WIDTH), lambda i, ci=ci: (ci(i), vb)),
                  pl.BlockSpec((c, GLA_KEY_WIDTH), lambda i, ci=ci, d=d: (ci(i), d))]
        outs.append(pl.BlockSpec((c, GLA_WIDTH), lambda i, ci=ci: (ci(i), 0)))
    for d in range(2):
        ci = (lambda i: i) if d == 0 else (lambda i: nchunk - 1 - i)
        outs.append(pl.BlockSpec((None, GLA_HEADS, GLA_DV, GLA_DK), lambda i, ci=ci: (ci(i), 0, 0, 0)))
    o_shape = jax.ShapeDtypeStruct((t, GLA_WIDTH), F32)
    s_shape = jax.ShapeDtypeStruct((nchunk, GLA_HEADS, GLA_DV, GLA_DK), F32)
    return pl.pallas_call(
        body, name="gla_fwd", grid=(nchunk,),
        in_specs=specs, out_specs=outs,
        out_shape=[o_shape, o_shape, s_shape, s_shape],
        scratch_shapes=[pltpu.VMEM((2, GLA_HEADS, GLA_DV, GLA_DK), F32)],
        compiler_params=_params(("arbitrary",), VMEM_LIMIT),
    )(proj, proj, proj, g, proj, proj, proj, g)


def _gla_bwd(proj, g, s_f, s_b, do):
    t = proj.shape[0]
    c = GLA_CHUNK
    nchunk = t // c
    qb, kb, vb = _seg_block("qg", GLA_KEY_WIDTH), _seg_block("kg", GLA_KEY_WIDTH), _seg_block("vg", GLA_WIDTH)

    def body(*refs):
        ins, outs, dstate = refs[:12], refs[12:20], refs[20]

        @pl.when(pl.program_id(0) == 0)
        def _():
            dstate[...] = jnp.zeros_like(dstate)

        loaded = []
        for d in range(2):
            q_ref, k_ref, v_ref, g_ref, s_ref, do_ref = ins[6 * d:6 * d + 6]
            for h in range(GLA_HEADS):
                loaded.append(((q_ref[:, _gla_k(h)], k_ref[:, _gla_k(h)], v_ref[:, _gla_v(h)], g_ref[:, _gla_k(h)],
                                s_ref[h]), (do_ref[:, _gla_v(h)], dstate[d, h])))
        grads = []
        for i, (primals, cotangents) in enumerate(loaded):
            _, vjp = jax.vjp(functools.partial(_gla_chunk, forward=(i < GLA_HEADS)), *primals)
            grads.append(vjp(cotangents))
        for i, (dq, dk, dv, dg, ds) in enumerate(grads):
            d, h = divmod(i, GLA_HEADS)
            dq_ref, dk_ref, dv_ref, dg_ref = outs[4 * d:4 * d + 4]
            dq_ref[:, _gla_k(h)] = dq
            dk_ref[:, _gla_k(h)] = dk
            dv_ref[:, _gla_v(h)] = dv
            dg_ref[:, _gla_k(h)] = dg
            dstate[d, h] = ds

    specs, outs, shapes = [], [], []
    for d in range(2):
        ci = (lambda i: nchunk - 1 - i) if d == 0 else (lambda i: i)
        specs += [pl.BlockSpec((c, GLA_KEY_WIDTH), lambda i, ci=ci: (ci(i), qb)),
                  pl.BlockSpec((c, GLA_KEY_WIDTH), lambda i, ci=ci: (ci(i), kb)),
                  pl.BlockSpec((c, GLA_WIDTH), lambda i, ci=ci: (ci(i), vb)),
                  pl.BlockSpec((c, GLA_KEY_WIDTH), lambda i, ci=ci, d=d: (ci(i), d)),
                  pl.BlockSpec((None, GLA_HEADS, GLA_DV, GLA_DK), lambda i, ci=ci: (ci(i), 0, 0, 0)),
                  pl.BlockSpec((c, GLA_WIDTH), lambda i, ci=ci: (ci(i), 0))]
        key = pl.BlockSpec((c, GLA_KEY_WIDTH), lambda i, ci=ci: (ci(i), 0))
        val = pl.BlockSpec((c, GLA_WIDTH), lambda i, ci=ci: (ci(i), 0))
        outs += [key, key, val, key]
        shapes += [jax.ShapeDtypeStruct((t, GLA_KEY_WIDTH), F32), jax.ShapeDtypeStruct((t, GLA_KEY_WIDTH), F32),
                   jax.ShapeDtypeStruct((t, GLA_WIDTH), F32), jax.ShapeDtypeStruct((t, GLA_KEY_WIDTH), F32)]
    return pl.pallas_call(
        body, name="gla_bwd", grid=(nchunk,),
        in_specs=specs, out_specs=outs, out_shape=shapes,
        scratch_shapes=[pltpu.VMEM((2, GLA_HEADS, GLA_DV, GLA_DK), F32)],
        compiler_params=_params(("arbitrary",), VMEM_LIMIT),
    )(proj, proj, proj, g, s_f, do, proj, proj, proj, g, s_b, do)


def _gla_out_head(o_f, o_b, gate, gn):
    return _rms(o_f + o_b, gn) * _silu(gate)


def _gla_out_fwd(o_f, o_b, proj, gn, *, tr=256):
    t = o_f.shape[0]
    gb = _seg_block("gate", GLA_WIDTH)

    def body(of_ref, ob_ref, gate_ref, gn_ref, out_ref):
        for h in range(GLA_HEADS):
            vc = slice(h * GLA_DV, (h + 1) * GLA_DV)
            out_ref[:, vc] = _gla_out_head(of_ref[:, vc], ob_ref[:, vc], gate_ref[:, vc], gn_ref[...]).astype(BF16)

    wide = pl.BlockSpec((tr, GLA_WIDTH), lambda i: (i, 0))
    return pl.pallas_call(
        body, name="gla_out_fwd", grid=(t // tr,),
        in_specs=[wide, wide, pl.BlockSpec((tr, GLA_WIDTH), lambda i: (i, gb)),
                  pl.BlockSpec((1, GLA_DV), lambda i: (0, 0))],
        out_specs=wide,
        out_shape=jax.ShapeDtypeStruct((t, GLA_WIDTH), BF16),
        compiler_params=_params(("parallel",), VMEM_LIMIT),
    )(o_f, o_b, proj, gn)


def _gla_out_bwd(o_f, o_b, proj, gn, dmix, *, tr=256):
    t = o_f.shape[0]
    gb = _seg_block("gate", GLA_WIDTH)

    def body(of_ref, ob_ref, gate_ref, gn_ref, dout_ref, do_ref, dgate_ref, dgn_ref):
        dgn = jnp.zeros((1, GLA_DV), F32)
        for h in range(GLA_HEADS):
            vc = slice(h * GLA_DV, (h + 1) * GLA_DV)
            _, vjp = jax.vjp(_gla_out_head, of_ref[:, vc], ob_ref[:, vc], gate_ref[:, vc], gn_ref[...])
            do, _, dgate, dg = vjp(dout_ref[:, vc])
            do_ref[:, vc] = do
            dgate_ref[:, vc] = dgate.astype(BF16)
            dgn = dgn + dg

        @pl.when(pl.program_id(0) == 0)
        def _():
            dgn_ref[...] = jnp.zeros_like(dgn_ref)

        dgn_ref[...] += dgn

    wide = pl.BlockSpec((tr, GLA_WIDTH), lambda i: (i, 0))
    vec = pl.BlockSpec((1, GLA_DV), lambda i: (0, 0))
    return pl.pallas_call(
        body, name="gla_out_bwd", grid=(t // tr,),
        in_specs=[wide, wide, pl.BlockSpec((tr, GLA_WIDTH), lambda i: (i, gb)), vec,
                  pl.BlockSpec((tr, GLA_WIDTH), lambda i: (i, 1))],
        out_specs=[wide, wide, vec],
        out_shape=[jax.ShapeDtypeStruct((t, GLA_WIDTH), F32), jax.ShapeDtypeStruct((t, GLA_WIDTH), BF16),
                   jax.ShapeDtypeStruct((1, GLA_DV), F32)],
        compiler_params=_params(("arbitrary",), VMEM_LIMIT),
    )(o_f, o_b, proj, gn, dmix)


CONV_TR = 1024
CONV_TC = 512
HALO = 8
HALO16 = 16


def _conv3(u, w, b):
    n = u.shape[0]
    return pltpu.roll(u, 1, 0) * w[0:1] + u * w[1:2] + pltpu.roll(u, n - 1, 0) * w[2:3] + b


def _conv_ext(main_ref, prev_ref, next_ref, r, nr):
    prev = prev_ref[...].astype(F32)[-HALO:] * (r > 0).astype(F32)
    nxt = next_ref[...].astype(F32)[:HALO] * (r < nr - 1).astype(F32)
    return jnp.concatenate([prev, main_ref[...].astype(F32), nxt], axis=0)


def _conv_specs(t, halo, half=None):
    per = CONV_TR // halo
    last = t // halo - 1
    lead = () if half is None else (None,)
    at = (lambda *ix: ix) if half is None else (lambda *ix: (half,) + ix)
    return [pl.BlockSpec(lead + (CONV_TR, CONV_TC), lambda j, r: at(r, j)),
            pl.BlockSpec(lead + (halo, CONV_TC), lambda j, r: at(jnp.maximum(r * per - 1, 0), j)),
            pl.BlockSpec(lead + (halo, CONV_TC), lambda j, r: at(jnp.minimum((r + 1) * per, last), j))]


def _ffn_mid_fwd(u, cw_g, cw_v, cb_g, cb_v):
    _, t, f = u.shape
    nr = t // CONV_TR

    def body(ug, ugp, ugn, uv, uvp, uvn, wg, wv, bg, bv, a_ref):
        r = pl.program_id(1)
        gate = _conv3(_conv_ext(ug, ugp, ugn, r, nr), wg[...], bg[...])[HALO:HALO + CONV_TR]
        val = _conv3(_conv_ext(uv, uvp, uvn, r, nr), wv[...], bv[...])[HALO:HALO + CONV_TR]
        a_ref[...] = (_silu(gate) * val).astype(BF16)

    w_spec = pl.BlockSpec((3, CONV_TC), lambda j, r: (0, j))
    b_spec = pl.BlockSpec((1, CONV_TC), lambda j, r: (0, j))
    return pl.pallas_call(
        body, name="ffn_mid_fwd", grid=(f // CONV_TC, nr),
        in_specs=_conv_specs(t, HALO, 0) + _conv_specs(t, HALO, 1) + [w_spec, w_spec, b_spec, b_spec],
        out_specs=pl.BlockSpec((CONV_TR, CONV_TC), lambda j, r: (r, j)),
        out_shape=jax.ShapeDtypeStruct((t, f), BF16),
        compiler_params=_params(("parallel", "parallel"), VMEM_LIMIT),
    )(u, u, u, u, u, u, cw_g, cw_v, cb_g, cb_v)


def _ffn_mid_bwd(u, cw_g, cw_v, cb_g, cb_v, da):
    _, t, f = u.shape
    nr = t // CONV_TR
    ext = CONV_TR + 2 * HALO

    def body(ug, ugp, ugn, uv, uvp, uvn, dam, dap, dan, wg, wv, bg, bv,
             du_ref, dwg_ref, dwv_ref, dbg_ref, dbv_ref):
        r = pl.program_id(1)
        shifted = []
        for main, prev, nxt in ((ug, ugp, ugn), (uv, uvp, uvn)):
            x = _conv_ext(main, prev, nxt, r, nr)
            shifted.append((pltpu.roll(x, 1, 0), x, pltpu.roll(x, ext - 1, 0)))
        da_x = _conv_ext(dam, dap, dan, r, nr)
        wg_t, wv_t = wg[...], wv[...]
        gate = shifted[0][0] * wg_t[0:1] + shifted[0][1] * wg_t[1:2] + shifted[0][2] * wg_t[2:3] + bg[...]
        val = shifted[1][0] * wv_t[0:1] + shifted[1][1] * wv_t[1:2] + shifted[1][2] * wv_t[2:3] + bv[...]
        sig = jax.nn.sigmoid(gate)
        silu = gate * sig
        d_val = da_x * silu
        d_gate = da_x * val * (sig + silu * (1.0 - sig))
        own = slice(HALO, HALO + CONV_TR)
        for half, (xs3, d, wt, dw_ref, db_ref) in enumerate(((shifted[0], d_gate, wg_t, dwg_ref, dbg_ref),
                                                            (shifted[1], d_val, wv_t, dwv_ref, dbv_ref))):
            du = pltpu.roll(d, ext - 1, 0) * wt[0:1] + d * wt[1:2] + pltpu.roll(d, 1, 0) * wt[2:3]
            du_ref[half] = du[own].astype(BF16)
            d_own = d[own]
            dw = jnp.concatenate([jnp.sum(x[own] * d_own, axis=0, keepdims=True) for x in xs3], axis=0)
            db = jnp.sum(d_own, axis=0, keepdims=True)

            @pl.when(r == 0)
            def _():
                dw_ref[...] = jnp.zeros_like(dw_ref)
                db_ref[...] = jnp.zeros_like(db_ref)

            dw_ref[...] += dw
            db_ref[...] += db

    w_spec = pl.BlockSpec((3, CONV_TC), lambda j, r: (0, j))
    b_spec = pl.BlockSpec((1, CONV_TC), lambda j, r: (0, j))
    return pl.pallas_call(
        body, name="ffn_mid_bwd", grid=(f // CONV_TC, nr),
        in_specs=(_conv_specs(t, HALO, 0) + _conv_specs(t, HALO, 1) + _conv_specs(t, HALO16)
                  + [w_spec, w_spec, b_spec, b_spec]),
        out_specs=[pl.BlockSpec((2, CONV_TR, CONV_TC), lambda j, r: (0, r, j)), w_spec, w_spec, b_spec, b_spec],
        out_shape=[jax.ShapeDtypeStruct((2, t, f), BF16),
                   jax.ShapeDtypeStruct((3, f), F32), jax.ShapeDtypeStruct((3, f), F32),
                   jax.ShapeDtypeStruct((1, f), F32), jax.ShapeDtypeStruct((1, f), F32)],
        compiler_params=_params(("parallel", "arbitrary"), VMEM_LIMIT),
    )(u, u, u, u, u, u, da, da, da, cw_g, cw_v, cb_g, cb_v)


def _down_proj_loss(act, w_down, x1, target, *, tm, tn):
    t, f = act.shape
    d = w_down.shape[1]

    def body(a_ref, w_ref, x_ref, t_ref, loss_ref, dy_ref, dyb_ref):
        y = lax.dot_general(a_ref[...], w_ref[...], _DIMS["nn"], preferred_element_type=F32) + x_ref[...]
        err = y - t_ref[...]
        dy = err * (1.0 / d)
        dy_ref[...] = dy
        dyb_ref[...] = dy.astype(BF16)
        part = 0.5 * jnp.sum(jnp.sum(err * err, axis=-1, keepdims=True) * (1.0 / d), axis=0, keepdims=True)

        @pl.when((pl.program_id(0) == 0) & (pl.program_id(1) == 0))
        def _():
            loss_ref[...] = jnp.zeros_like(loss_ref)

        loss_ref[...] += jnp.broadcast_to(part, loss_ref.shape)

    tile = pl.BlockSpec((tm, tn), lambda i, j: (i, j))
    return pl.pallas_call(
        body, name="down_proj_loss", grid=(t // tm, d // tn),
        in_specs=[pl.BlockSpec((tm, f), lambda i, j: (i, 0)), pl.BlockSpec((f, tn), lambda i, j: (0, j)), tile, tile],
        out_specs=[pl.BlockSpec((1, 128), lambda i, j: (0, 0)), tile, tile],
        out_shape=[jax.ShapeDtypeStruct((1, 128), F32), jax.ShapeDtypeStruct((t, d), F32),
                   jax.ShapeDtypeStruct((t, d), BF16)],
        compiler_params=_params(("arbitrary", "arbitrary"), VMEM_LIMIT),
    )(act, w_down, x1, target)


ANY = pl.BlockSpec(memory_space=pl.ANY)


def _position():
    return lax.axis_index("x"), lax.axis_index("y"), lax.axis_index("c")


def _other_chips(x, y):
    return [(1 - x, y), (x, 1 - y), (1 - x, 1 - y)]


def _handshake(peers):
    barrier = pltpu.get_barrier_semaphore()
    for peer in peers:
        pl.semaphore_signal(barrier, inc=1, device_id=peer, device_id_type=MESH)
    pl.semaphore_wait(barrier, len(peers))


def _exchange(body, operands, out_shapes, sems, *, name, collective_id):
    n_in, n_out = len(operands), len(out_shapes)

    def run(*refs):
        body(refs[:n_in], refs[n_in:n_in + n_out], *refs[n_in + n_out:])

    if collective_id is None:
        return pl.pallas_call(run, name=name, in_specs=[ANY] * n_in, out_specs=[ANY] * n_out,
                              out_shape=out_shapes, scratch_shapes=sems)(*operands)
    return pl.kernel(run, name=name, out_type=out_shapes,
                     mesh=plsc.ScalarSubcoreMesh(axis_name="sequencer", num_cores=1), scratch_types=sems,
                     compiler_params=pltpu.CompilerParams(collective_id=collective_id))(*operands)


def _all_gather(blocks, *, name, collective_id=None):
    na = len(blocks)

    def body(ins, outs, send_sems, recv_sems, local_sems):
        x, y, c = _position()
        me, sibling = (x, y, c), (x, y, 1 - c)
        along_x, along_y, diagonal = (1 - x, y, c), (x, 1 - y, c), (1 - x, 1 - y, c)
        first_c = c == 0
        relay_from = (jnp.where(first_c, x, 1 - x), jnp.where(first_c, 1 - y, y), c)
        relay_to = (jnp.where(first_c, 1 - x, x), jnp.where(first_c, y, 1 - y), c)
        if collective_id is not None:
            _handshake([sibling, along_x, along_y])

        def index(px, py, pc):
            return 4 * px + 2 * py + pc

        def copy(a, k, block, to, src=None):
            dst = outs[a].at[index(*block)]
            return pltpu.make_async_remote_copy(
                src_ref=dst if src is None else src, dst_ref=dst,
                send_sem=send_sems.at[a, k], recv_sem=recv_sems.at[a, k],
                device_id=to, device_id_type=MESH)

        pending = []
        for a in range(na):
            mine = pltpu.make_async_copy(ins[a], outs[a].at[index(*me)], local_sems.at[a])
            mine.start()
            pending.append(mine)
        sent = []
        for a in range(na):
            sent += [copy(a, 0, me, sibling, src=ins[a]), copy(a, 1, me, along_x, src=ins[a]),
                     copy(a, 2, me, along_y, src=ins[a])]
        for cp in sent:
            cp.start()

        def passes_on(k_in, owner, k_out):
            for a in range(na):
                copy(a, k_in, owner, me).wait_recv()
                cp = copy(a, k_out, owner, sibling)
                cp.start()
                sent.append(cp)

        passes_on(1, along_x, 4)
        passes_on(2, along_y, 5)
        for a in range(na):
            cp = copy(a, 3, relay_from, relay_to)
            cp.start()
            sent.append(cp)
        passes_on(3, diagonal, 6)
        for a in range(na):
            copy(a, 0, sibling, me).wait_recv()
            for k, owner in ((4, along_x), (5, along_y), (6, diagonal)):
                copy(a, k, (owner[0], owner[1], 1 - c), me).wait_recv()
        for cp in sent:
            cp.wait_send()
        for cp in pending:
            cp.wait()

    return _exchange(
        body, blocks, [jax.ShapeDtypeStruct((N_DEV,) + b.shape, b.dtype) for b in blocks],
        [pltpu.SemaphoreType.DMA((na, 7)), pltpu.SemaphoreType.DMA((na, 7)), pltpu.SemaphoreType.DMA((na,))],
        name=name, collective_id=collective_id)


def _grad_exchange(grads, parts, *, name, collective_id):
    ng, npart = len(grads), len(parts)

    def body(ins, outs, core_send, core_recv, chip_send, chip_recv, local_sems):
        x, y, c = _position()
        sibling = (x, y, 1 - c)
        chips = _other_chips(x, y)
        _handshake([sibling] + [(px, py, c) for px, py in chips])
        me = 2 * x + y
        copies = []
        for b in range(npart):
            src, dst = ins[ng + b], outs[ng + b]
            own = pltpu.make_async_copy(src.at[me], dst.at[me], local_sems.at[b])
            own.start()
            copies.append(own)
            for j, (px, py) in enumerate(chips):
                cp = pltpu.make_async_remote_copy(
                    src_ref=src.at[2 * px + py], dst_ref=dst.at[me],
                    send_sem=chip_send.at[b, j], recv_sem=chip_recv.at[b, j],
                    device_id=(px, py, c), device_id_type=MESH)
                cp.start()
                copies.append(cp)
        for a in range(ng):
            for k in range(N_CHIP):
                cp = pltpu.make_async_remote_copy(
                    src_ref=ins[a].at[k, 1 - c], dst_ref=outs[a].at[k],
                    send_sem=core_send.at[a, k], recv_sem=core_recv.at[a, k],
                    device_id=sibling, device_id_type=MESH)
                cp.start()
                copies.append(cp)
        for cp in copies:
            cp.wait()

    shapes = ([jax.ShapeDtypeStruct((N_CHIP,) + g.shape[2:], g.dtype) for g in grads]
              + [jax.ShapeDtypeStruct(p.shape, p.dtype) for p in parts])
    sems = [pltpu.SemaphoreType.DMA((max(ng, 1), N_CHIP)), pltpu.SemaphoreType.DMA((max(ng, 1), N_CHIP)),
            pltpu.SemaphoreType.DMA((max(npart, 1), 3)), pltpu.SemaphoreType.DMA((max(npart, 1), 3)),
            pltpu.SemaphoreType.DMA((max(npart, 1),))]
    out = _exchange(body, list(grads) + list(parts), shapes, sems, name=name, collective_id=collective_id)
    return out[:ng], out[ng:]


def _pair_sum(grad, theirs, core, *, tile, name):
    _, _, r, w = grad.shape
    tr, tw = tile
    assert r % tr == 0 and w % tw == 0

    def body(core_ref, mine_ref, theirs_ref, out_ref):
        out_ref[...] = mine_ref[...] + theirs_ref[...]

    return pl.pallas_call(
        body, name=name,
        grid_spec=pltpu.PrefetchScalarGridSpec(
            num_scalar_prefetch=1, grid=(N_CHIP, r // tr, w // tw),
            in_specs=[pl.BlockSpec((None, None, tr, tw), lambda k, i, j, core_ref: (k, core_ref[0], i, j)),
                      pl.BlockSpec((None, tr, tw), lambda k, i, j, core_ref: (k, i, j))],
            out_specs=pl.BlockSpec((None, tr, tw), lambda k, i, j, core_ref: (k, i, j))),
        out_shape=jax.ShapeDtypeStruct((N_CHIP, r, w), F32),
        compiler_params=_params(("parallel", "parallel", "parallel"), VMEM_LIMIT),
    )(core, grad, theirs)


class _ReduceScatter:
    def __init__(self, core):
        self.core = core
        self.pending = None
        self.results = {}
        self.launches = 0

    def push(self, tag, grads, rows, then):
        pair, prev_tag = [], None
        if self.pending is not None:
            prev_tag, prev, theirs, prev_rows = self.pending
            pair = [_pair_sum(g, s, self.core, tile=tile, name=f"pair_sum_{prev_tag}_{i}")
                    for i, (g, s, tile) in enumerate(zip(prev, theirs, prev_rows))]
        grads, pair, then = lax.optimization_barrier((list(grads), pair, then))
        grads = [g.reshape((N_CHIP, 2) + g.shape[1:]) for g in grads]
        self.launches += 1
        theirs, parts = _grad_exchange(grads, pair, name=f"grad_exchange_{self.launches}",
                                       collective_id=1 + self.launches)
        if prev_tag is not None:
            self.results[prev_tag] = parts
        self.pending = (tag, grads, theirs, rows) if tag is not None else None
        return then

    def result(self, tag):
        return self.results[tag]


def _adamw(parts, w, m, v, *, tile, name):
    n, r, cols = parts.shape
    tr, tw = tile
    assert r % tr == 0 and cols % tw == 0 and w.shape == (r, cols)
    c1 = 1.0 - ADAM_B1 ** ADAM_STEP
    c2 = 1.0 - ADAM_B2 ** ADAM_STEP

    def body(p_ref, w_ref, m_ref, v_ref, g_ref, d_ref, nm_ref, nv_ref):
        g = p_ref[0]
        for k in range(1, n):
            g = g + p_ref[k]
        new_m = ADAM_B1 * m_ref[...] + (1.0 - ADAM_B1) * g
        new_v = ADAM_B2 * v_ref[...] + (1.0 - ADAM_B2) * (g * g)
        m_hat = new_m / c1
        v_hat = new_v / c2
        g_ref[...] = g
        d_ref[...] = -ADAM_LR * (m_hat / (jnp.sqrt(v_hat) + ADAM_EPS) + ADAM_WD * w_ref[...])
        nm_ref[...] = new_m
        nv_ref[...] = new_v

    spec = pl.BlockSpec((tr, tw), lambda i, j: (i, j))
    shape = jax.ShapeDtypeStruct((r, cols), F32)
    return pl.pallas_call(
        body, name=name, grid=(r // tr, cols // tw),
        in_specs=[pl.BlockSpec((n, tr, tw), lambda i, j: (0, i, j)), spec, spec, spec],
        out_specs=[spec] * 4, out_shape=[shape] * 4,
        compiler_params=_params(("parallel", "parallel"), VMEM_LIMIT),
    )(parts, w, m, v)


def _rope_tables(t):
    half = HEAD_DIM // 2
    inv = 1.0 / (ROPE_THETA ** (jnp.arange(half, dtype=F32) / half))
    ang = jnp.arange(t, dtype=jnp.int32).astype(F32)[:, None] * inv[None, :]
    cos, sin = jnp.cos(ang), jnp.sin(ang)
    return jnp.concatenate([cos, cos], axis=1), jnp.concatenate([-sin, sin], axis=1)


IN_KERNEL = IN_MAIN + LR_PAD


def _to_kernel_rows(w_t):
    order = sorted(SEGMENTS.values())
    pad = jnp.zeros((LR_PAD - (IN_TOTAL - IN_MAIN), w_t.shape[1]), w_t.dtype)
    return jnp.concatenate([w_t[src:src + width] for _, src, width in order] + [w_t[IN_MAIN:IN_TOTAL], pad], axis=0)


def _pack(pieces):
    flat = []
    for p in pieces:
        p = p.reshape(-1)
        flat.append(jnp.pad(p, (0, (-p.shape[0]) % 128)))
    return jnp.concatenate(flat).reshape(-1, 128)


def _unpack(packed, shapes):
    flat = packed.reshape(-1)
    out, off = [], 0
    for s in shapes:
        size = 1
        for dim in s:
            size *= dim
        out.append(flat[off:off + size].reshape(s))
        off += size + (-size) % 128
    return out


def _local_step(xs, target, norm1_g, w_in_k, gq, gk, attn_sink, w2, ba2, gla_norm_g, w_out_full, norm2_g,
                w_up8, cw_g, cw_v, cb_g, cb_v, w_down_full, rs=None):
    t = xs.shape[0]
    tm = min(1024, t)
    tall = min(2048, t)
    cos, sin_signed = _rope_tables(t)
    sink = attn_sink.reshape(ATTN_HEADS)

    h1 = _rmsnorm_fwd(xs, norm1_g, name="norm1_fwd")
    proj = _matmul(h1, w_in_k, "nt", tm=tall, tn=IN_MAIN // 4, tk=D_MODEL, n_out=IN_MAIN, name="proj_main")
    proj_lr = _matmul(h1, w_in_k[IN_MAIN:], "nt", tm=tm, tn=LR_PAD, tk=D_MODEL, name="proj_lr")
    qa, ka, va = _attn_prep_fwd(proj, cos, sin_signed, gq, gk)
    o_attn = _attn_fwd(qa, ka, va, sink)
    g_dec = _gla_prep_fwd(proj_lr, w2, ba2)
    o_f, o_b, s_f, s_b = _gla_fwd(proj, g_dec)
    o_gla = _gla_out_fwd(o_f, o_b, proj, gla_norm_g)
    x1 = _out_proj(o_attn, o_gla, w_out_full, xs, tm=tall, tn=512)
    h2 = _rmsnorm_fwd(x1, norm2_g, name="norm2_fwd")
    u = _up_proj(h2, w_up8, tm=tm)
    act = _ffn_mid_fwd(u, cw_g, cw_v, cb_g, cb_v)
    loss_part, dy, dy_b = _down_proj_loss(act, w_down_full, x1, target, tm=tm, tn=512)

    d_act = _matmul(dy_b, w_down_full, "nt", tm=tall, tn=D_FF // 4, tk=D_MODEL, out_dtype=BF16, name="d_act")
    dw_down = _matmul(act, dy_b, "tn", tm=D_FF // 4, tn=512, tk=t, name="dw_down")
    if rs is not None:
        d_act = rs.push("w_down", [dw_down.reshape(N_DEV, D_FF // N_DEV, D_MODEL)], [(64, D_MODEL)], d_act)
    du, dcw_g, dcw_v, dcb_g, dcb_v = _ffn_mid_bwd(u, cw_g, cw_v, cb_g, cb_v, d_act)
    dw_up8 = _up_proj_dw(h2, du, tm=512, tk=t)
    if rs is not None:
        du = rs.push("w_up", [dw_up8], [(256, UP_BLOCK)], du)
    dh2 = _up_proj_dx(du, w_up8, tm=tm, tn=1024)
    dx1, dx1_b, d_norm2 = _rmsnorm_bwd(x1, norm2_g, dh2, dy, name="norm2_bwd")
    dmix = _matmul(dx1_b, w_out_full, "nt", tm=tall, tn=1024, tk=D_MODEL, name="d_mix")
    dw_out = _out_proj_dw(o_attn, o_gla, dx1_b, tn=512)
    if rs is not None:
        dmix = rs.push("w_out", [dw_out.reshape(N_DEV, D_MODEL // N_DEV, D_MODEL)], [(256, D_MODEL)], dmix)
    do_gla, d_gate, d_gla_norm = _gla_out_bwd(o_f, o_b, proj, gla_norm_g, dmix)
    (dq_f, dk_f, dv_f, dg_f, dq_b, dk_b, dv_b, dg_b) = _gla_bwd(proj, g_dec, s_f, s_b, do_gla)
    d_lr, d_w2, d_ba2 = _gla_prep_bwd(proj_lr, w2, ba2, dg_f, dg_b)
    dqa, dk_lo, dk_mid, dk_hi, dv_lo, dv_mid, dv_hi, d_sink8 = _attn_bwd(qa, ka, va, sink, dmix)
    d_qa, d_ka, d_va, d_qn, d_kn = _attn_prep_bwd(proj, cos, sin_signed, gq, gk, dqa,
                                                  (dk_lo, dk_mid, dk_hi), (dv_lo, dv_mid, dv_hi))
    d_seg = {"qa": d_qa, "gate": d_gate, "vg": (dv_f + dv_b).astype(BF16), "qg": (dq_f + dq_b).astype(BF16),
             "kg": (dk_f + dk_b).astype(BF16), "ka": d_ka, "va": d_va}
    d_proj = jnp.concatenate([d_seg[k] for k in sorted(SEGMENTS, key=lambda k: SEGMENTS[k][0])] + [d_lr], axis=1)
    dw_in_t = _in_proj_dw_lr(_in_proj_dw(d_proj, h1, tn=1024), d_lr, h1)
    if rs is not None:
        per_in, per_wa = IN_TOTAL // N_DEV, GLA_KEY_WIDTH // N_DEV
        dconv_w = jnp.concatenate([dcw_g, dcw_v], axis=1)
        d_wa2_f = d_w2[:GLA_RANK, :GLA_KEY_WIDTH]
        d_wa2_b = d_w2[GLA_RANK:2 * GLA_RANK, GLA_KEY_WIDTH:]
        small_grad = jnp.stack([
            _pack([dconv_w[:, d * UP_BLOCK:(d + 1) * UP_BLOCK], d_wa2_f[:, d * per_wa:(d + 1) * per_wa],
                   d_wa2_b[:, d * per_wa:(d + 1) * per_wa]]) for d in range(N_DEV)])
        d_proj, d_lr = rs.push("w_in", [dw_in_t.reshape(N_DEV, per_in, D_MODEL), small_grad],
                               [(per_in, 512), small_grad.shape[1:]], (d_proj, d_lr))
    dh1 = _matmul(d_proj, w_in_k, "nn", tm=tm, tn=512, tk=IN_KERNEL, name="dh1")
    if rs is not None:
        dh1 = rs.push(None, [], [], dh1)
    grad_x, _, d_norm1 = _rmsnorm_bwd(xs, norm1_g, dh1, dx1, name="norm1_bwd")
    return (loss_part, grad_x, dw_in_t, dw_out, dw_up8, dw_down, dcw_g, dcw_v, dcb_g, dcb_v,
            d_w2, d_ba2, d_norm1, d_norm2, d_qn, d_kn, d_sink8, d_gla_norm)


def kernel(x, norm1_g, w_in, attn_q_norm_g, attn_k_norm_g, attn_sink, gla_wa2_fwd, gla_ba_fwd, gla_wa2_bwd, gla_ba_bwd, gla_out_norm_g, w_out, norm2_g, w_up, conv_w, conv_b, w_down, loss_target, m_norm1_g, m_w_in, m_attn_q_norm_g, m_attn_k_norm_g, m_attn_sink, m_gla_wa2_fwd, m_gla_ba_fwd, m_gla_wa2_bwd, m_gla_ba_bwd, m_gla_out_norm_g, m_w_out, m_norm2_g, m_w_up, m_conv_w, m_conv_b, m_w_down, v_norm1_g, v_w_in, v_attn_q_norm_g, v_attn_k_norm_g, v_attn_sink, v_gla_wa2_fwd, v_gla_ba_fwd, v_gla_wa2_bwd, v_gla_ba_bwd, v_gla_out_norm_g, v_w_out, v_norm2_g, v_w_up, v_conv_w, v_conv_b, v_w_down):
    t = x.shape[1]
    xs = x.reshape(t, D_MODEL)
    target = loss_target.reshape(t, D_MODEL)
    core = lax.axis_index("c").astype(jnp.int32).reshape(1)

    sharded_small = [conv_w[0], gla_wa2_fwd[0], gla_wa2_bwd[0]]
    small_shapes = [s.shape for s in sharded_small]
    w_in_t, m_in_t, v_in_t = (jnp.swapaxes(a[0], 0, 1) for a in (w_in, m_w_in, v_w_in))
    g_in, g_small = _all_gather([w_in_t.astype(BF16), _pack(sharded_small)], name="gather_w_in")
    g_in, later = lax.optimization_barrier(
        (g_in, [w_out[0].astype(BF16), w_up[0].astype(BF16), w_down[0].astype(BF16)]))
    g_out, w_up8, g_down = _all_gather(later, name="gather_later_weights", collective_id=1)
    w_in_k = _to_kernel_rows(g_in.reshape(IN_TOTAL, D_MODEL))
    w_out_full = g_out.reshape(D_MODEL, D_MODEL)
    w_down_full = g_down.reshape(D_FF, D_MODEL)
    small_full = [_unpack(g_small[d], small_shapes) for d in range(N_DEV)]
    conv_w_full = jnp.concatenate([s[0] for s in small_full], axis=1)
    wa2_f = jnp.concatenate([s[1] for s in small_full], axis=1)
    wa2_b = jnp.concatenate([s[2] for s in small_full], axis=1)
    cw_g, cw_v = conv_w_full[:, :D_FF], conv_w_full[:, D_FF:]
    cb_g, cb_v = conv_b[:, :D_FF], conv_b[:, D_FF:]
    w2 = jnp.zeros((LR_PAD, 2 * GLA_KEY_WIDTH), F32)
    w2 = w2.at[:GLA_RANK, :GLA_KEY_WIDTH].set(wa2_f).at[GLA_RANK:2 * GLA_RANK, GLA_KEY_WIDTH:].set(wa2_b)
    ba2 = jnp.concatenate([gla_ba_fwd, gla_ba_bwd], axis=1)
    rs = _ReduceScatter(core)
    (loss_part, grad_x, _, _, _, _, _, _, dcb_g, dcb_v, _, d_ba2,
     d_norm1, d_norm2, d_qn, d_kn, d_sink8, d_gla_norm) = _local_step(
        xs, target, norm1_g, w_in_k, attn_q_norm_g, attn_k_norm_g, attn_sink, w2, ba2, gla_out_norm_g,
        w_out_full, norm2_g, w_up8, cw_g, cw_v, cb_g, cb_v, w_down_full, rs=rs)

    (part_down,), (part_up,), (part_out,) = rs.result("w_down"), rs.result("w_up"), rs.result("w_out")
    part_in, part_small = rs.result("w_in")
    m_small = _pack([m_conv_w[0], m_gla_wa2_fwd[0], m_gla_wa2_bwd[0]])
    v_small = _pack([v_conv_w[0], v_gla_wa2_fwd[0], v_gla_wa2_bwd[0]])
    upd_in = _adamw(part_in, w_in_t, m_in_t, v_in_t, tile=(IN_TOTAL // N_DEV, 512), name="adamw_w_in")
    upd_in = [jnp.swapaxes(u, 0, 1) for u in upd_in]
    upd_out = _adamw(part_out, w_out[0], m_w_out[0], v_w_out[0], tile=(256, D_MODEL), name="adamw_w_out")
    upd_up = _adamw(part_up, w_up[0], m_w_up[0], v_w_up[0], tile=(256, UP_BLOCK), name="adamw_w_up")
    upd_down = _adamw(part_down, w_down[0], m_w_down[0], v_w_down[0], tile=(64, D_MODEL), name="adamw_w_down")
    upd_small = _adamw(part_small, _pack(sharded_small), m_small, v_small, tile=part_small.shape[1:],
                       name="adamw_small")
    upd_small = [_unpack(u, small_shapes) for u in upd_small]

    rep_names = ["norm1_g", "attn_q_norm_g", "attn_k_norm_g", "attn_sink", "gla_ba_fwd", "gla_ba_bwd",
                 "gla_out_norm_g", "norm2_g", "conv_b"]
    rep_w = [norm1_g, attn_q_norm_g, attn_k_norm_g, attn_sink, gla_ba_fwd, gla_ba_bwd, gla_out_norm_g, norm2_g, conv_b]
    rep_m = [m_norm1_g, m_attn_q_norm_g, m_attn_k_norm_g, m_attn_sink, m_gla_ba_fwd, m_gla_ba_bwd,
             m_gla_out_norm_g, m_norm2_g, m_conv_b]
    rep_v = [v_norm1_g, v_attn_q_norm_g, v_attn_k_norm_g, v_attn_sink, v_gla_ba_fwd, v_gla_ba_bwd,
             v_gla_out_norm_g, v_norm2_g, v_conv_b]
    d_sink = d_sink8[:, :GQA_GROUP, 0].reshape(1, ATTN_HEADS)
    rep_g = [d_norm1, d_qn, d_kn, d_sink, d_ba2[:, :GLA_KEY_WIDTH], d_ba2[:, GLA_KEY_WIDTH:], d_gla_norm, d_norm2,
             jnp.concatenate([dcb_g, dcb_v], axis=1)]
    ride = jnp.ones((1, 128), F32)
    rep_shapes = [w.shape for w in rep_w] + [loss_part.shape]
    (rep_terms,) = _all_gather([_pack(rep_g + [loss_part])], name="gather_small_grads")
    upd_rep = _adamw(rep_terms, _pack(rep_w + [ride]), _pack(rep_m + [ride]), _pack(rep_v + [ride]),
                     tile=rep_terms.shape[1:], name="adamw_replicated")
    upd_rep = [_unpack(u, rep_shapes) for u in upd_rep]
    loss = upd_rep[0][-1][0, 0]

    order = ["norm1_g", "w_in", "attn_q_norm_g", "attn_k_norm_g", "attn_sink", "gla_wa2_fwd", "gla_ba_fwd",
             "gla_wa2_bwd", "gla_ba_bwd", "gla_out_norm_g", "w_out", "norm2_g", "w_up", "conv_w", "conv_b", "w_down"]
    outs = [loss, grad_x.reshape(1, t, D_MODEL)]
    for kind in range(4):
        by_name = {n: upd_rep[kind][i] for i, n in enumerate(rep_names)}
        by_name["w_in"] = upd_in[kind][None]
        by_name["w_out"] = upd_out[kind][None]
        by_name["w_up"] = upd_up[kind][None]
        by_name["w_down"] = upd_down[kind][None]
        by_name["conv_w"] = upd_small[kind][0][None]
        by_name["gla_wa2_fwd"] = upd_small[kind][1][None]
        by_name["gla_wa2_bwd"] = upd_small[kind][2][None]
        outs += [by_name[n] for n in order]
    return tuple(outs)
```

```python
import functools

import jax
import jax.numpy as jnp
from jax import lax
from jax.experimental import pallas as pl
from jax.experimental.pallas import tpu as pltpu
from jax.experimental.pallas import tpu_sc as plsc

F32 = jnp.float32
BF16 = jnp.bfloat16

D_MODEL = 2048
HEAD_DIM = 128
ATTN_WIDTH = 1024
ATTN_HEADS = 8
KV_HEADS = 2
GQA_GROUP = 4
KV_WIDTH = KV_HEADS * HEAD_DIM
ATTN_BLOCK = 128
WINDOW = 128
ROPE_THETA = 10000.0
GLA_HEADS = 4
GLA_DK = 128
GLA_DV = 256
GLA_KEY_WIDTH = 512
GLA_WIDTH = 1024
GLA_RANK = 16
GLA_GATE_NORMALIZER = 16.0
GLA_CHUNK = 64
D_FF = 5632
NORM_EPS = 1e-6
IN_TOTAL = 4640
IN_MAIN = 4608
LR_PAD = 128
N_DEV = 8
N_CHIP = 4

ADAM_LR = 0.001
ADAM_B1 = 0.9
ADAM_B2 = 0.999
ADAM_EPS = 1e-08
ADAM_WD = 0.01
ADAM_STEP = 10

SEGMENTS = {
    "qa": (0, 0, 1024),
    "gate": (1024, 3584, 1024),
    "vg": (2048, 2560, 1024),
    "qg": (3072, 1536, 512),
    "kg": (3584, 2048, 512),
    "ka": (4096, 1024, 256),
    "va": (4352, 1280, 256),
}

VMEM_LIMIT = 56 * 1024 * 1024
MESH = pl.DeviceIdType.MESH


def _params(semantics=None, vmem=None):
    return pltpu.CompilerParams(dimension_semantics=semantics, vmem_limit_bytes=vmem)


_DIMS = {
    "nn": (((1,), (0,)), ((), ())),
    "nt": (((1,), (1,)), ((), ())),
    "tn": (((0,), (0,)), ((), ())),
}


def _mxu(a, b, mode):
    return lax.dot_general(a.astype(BF16), b.astype(BF16), _DIMS[mode], preferred_element_type=F32)


@functools.partial(jax.custom_vjp, nondiff_argnums=(2,))
def bdot(a, b, mode):
    return _mxu(a, b, mode)


def _bdot_fwd(a, b, mode):
    return _mxu(a, b, mode), (a, b)


def _bdot_bwd(mode, res, g):
    a, b = res
    if mode == "nn":
        return _mxu(g, b, "nt"), _mxu(a, g, "tn")
    if mode == "nt":
        return _mxu(g, b, "nn"), _mxu(g, a, "tn")
    return _mxu(b, g, "nt"), _mxu(a, g, "nn")


bdot.defvjp(_bdot_fwd, _bdot_bwd)


def _rms(x, g):
    return x * lax.rsqrt(jnp.mean(x * x, axis=-1, keepdims=True) + NORM_EPS) * g


def _rope(x, cos, sin_signed):
    return x * cos + pltpu.roll(x, HEAD_DIM // 2, 1) * sin_signed


def _rope_transposed(d, cos, sin_signed):
    return d * cos + pltpu.roll(d * sin_signed, HEAD_DIM // 2, 1)


def _silu(x):
    return x * jax.nn.sigmoid(x)


def _log_sigmoid(z):
    return -(jnp.maximum(-z, 0.0) + jnp.log(1.0 + jnp.exp(-jnp.abs(z))))


def _matmul_call(args, in_specs, o_spec, out_shape, grid, mode, nk, acc_shape, *, name, has_res=False,
                 prefetch=None, load_b=lambda ref: ref[...]):
    dims = _DIMS[mode]
    out_dtype = out_shape.dtype
    n_pre = 0 if prefetch is None else 1

    def body(*refs):
        refs = refs[n_pre:]
        if has_res:
            a_ref, b_ref, r_ref, o_ref = refs[:4]
            rest = refs[4:]
        else:
            a_ref, b_ref, o_ref = refs[:3]
            r_ref = None
            rest = refs[3:]
        part = lax.dot_general(a_ref[...], load_b(b_ref), dims, preferred_element_type=F32)

        def finish(acc):
            if r_ref is not None:
                acc = acc + r_ref[...]
            o_ref[...] = acc.astype(out_dtype)

        if nk == 1:
            finish(part)
        else:
            acc_ref = rest[0]
            kk = pl.program_id(2)

            @pl.when(kk == 0)
            def _():
                acc_ref[...] = part

            @pl.when(kk > 0)
            def _():
                acc_ref[...] += part

            @pl.when(kk == nk - 1)
            def _():
                finish(acc_ref[...])

    scratch = [pltpu.VMEM(acc_shape, F32)] if nk > 1 else []
    params = _params(("parallel", "parallel", "arbitrary"), VMEM_LIMIT)
    if prefetch is None:
        return pl.pallas_call(body, name=name, grid=grid, in_specs=in_specs, out_specs=o_spec, out_shape=out_shape,
                              scratch_shapes=scratch, compiler_params=params)(*args)
    return pl.pallas_call(
        body, name=name,
        grid_spec=pltpu.PrefetchScalarGridSpec(num_scalar_prefetch=1, grid=grid, in_specs=in_specs,
                                               out_specs=o_spec, scratch_shapes=scratch),
        out_shape=out_shape, compiler_params=params)(prefetch, *args)


def _matmul(a, b, mode, *, tm, tn, tk, out_dtype=F32, res=None, name, n_out=None):
    if mode == "nn":
        (m, k), (k2, n) = a.shape, b.shape
    elif mode == "nt":
        (m, k), (n, k2) = a.shape, b.shape
    else:
        (k, m), (k2, n) = a.shape, b.shape
    n = n if n_out is None else n_out
    assert k == k2 and m % tm == 0 and n % tn == 0 and k % tk == 0, (name, a.shape, b.shape, tm, tn, tk)
    if mode == "tn":
        a_spec = pl.BlockSpec((tk, tm), lambda i, j, kk: (kk, i))
    else:
        a_spec = pl.BlockSpec((tm, tk), lambda i, j, kk: (i, kk))
    if mode == "nt":
        b_spec = pl.BlockSpec((tn, tk), lambda i, j, kk: (j, kk))
    else:
        b_spec = pl.BlockSpec((tk, tn), lambda i, j, kk: (kk, j))
    o_spec = pl.BlockSpec((tm, tn), lambda i, j, kk: (i, j))
    in_specs, args = [a_spec, b_spec], [a, b]
    if res is not None:
        in_specs.append(o_spec)
        args.append(res)
    return _matmul_call(args, in_specs, o_spec, jax.ShapeDtypeStruct((m, n), out_dtype),
                        (m // tm, n // tn, k // tk), mode, k // tk, (tm, tn), name=name, has_res=res is not None)


def _out_proj(o_attn, o_gla, w_out, x, *, tm, tn):
    t, ka = o_attn.shape
    kg = o_gla.shape[1]

    def body(a_ref, g_ref, w_ref, x_ref, o_ref):
        acc = lax.dot_general(a_ref[...], w_ref[:ka], _DIMS["nn"], preferred_element_type=F32)
        acc = acc + lax.dot_general(g_ref[...], w_ref[ka:], _DIMS["nn"], preferred_element_type=F32)
        o_ref[...] = acc + x_ref[...]

    tile = pl.BlockSpec((tm, tn), lambda i, j: (i, j))
    return pl.pallas_call(
        body, name="out_proj", grid=(t // tm, D_MODEL // tn),
        in_specs=[pl.BlockSpec((tm, ka), lambda i, j: (i, 0)), pl.BlockSpec((tm, kg), lambda i, j: (i, 0)),
                  pl.BlockSpec((ka + kg, tn), lambda i, j: (0, j)), tile],
        out_specs=tile, out_shape=jax.ShapeDtypeStruct((t, D_MODEL), F32),
        compiler_params=_params(("parallel", "parallel"), VMEM_LIMIT),
    )(o_attn, o_gla, w_out, x)


def _out_proj_dw(o_attn, o_gla, dx1, *, tn):
    t, ka = o_attn.shape
    assert o_gla.shape == (t, ka)

    def body(a_ref, g_ref, d_ref, o_ref):
        @pl.when(pl.program_id(0) == 0)
        def _():
            o_ref[...] = lax.dot_general(a_ref[...], d_ref[...], _DIMS["tn"], preferred_element_type=F32)

        @pl.when(pl.program_id(0) == 1)
        def _():
            o_ref[...] = lax.dot_general(g_ref[...], d_ref[...], _DIMS["tn"], preferred_element_type=F32)

    whole = pl.BlockSpec((t, ka), lambda i, j: (0, 0))
    return pl.pallas_call(
        body, name="dw_out", grid=(2, D_MODEL // tn),
        in_specs=[whole, whole, pl.BlockSpec((t, tn), lambda i, j: (0, j))],
        out_specs=pl.BlockSpec((ka, tn), lambda i, j: (i, j)),
        out_shape=jax.ShapeDtypeStruct((2 * ka, D_MODEL), F32),
        compiler_params=_params(("parallel", "parallel"), VMEM_LIMIT),
    )(o_attn, o_gla, dx1)


UP_BLOCK = 2 * D_FF // N_DEV


def _up_proj(h2, w_up8, *, tm):
    t = h2.shape[0]
    return _matmul_call(
        [h2, w_up8],
        [pl.BlockSpec((tm, D_MODEL), lambda i, j, kk: (i, 0)),
         pl.BlockSpec((None, D_MODEL, UP_BLOCK), lambda i, j, kk: (j, 0, 0))],
        pl.BlockSpec((None, tm, UP_BLOCK), lambda i, j, kk: (j // N_CHIP, i, j % N_CHIP)),
        jax.ShapeDtypeStruct((2, t, D_FF), F32), (t // tm, N_DEV, 1), "nn", 1, None, name="up_proj")


def _up_proj_dx(du, w_up8, *, tm, tn):
    t = du.shape[1]
    pair = 2
    return _matmul_call(
        [du, w_up8],
        [pl.BlockSpec((None, tm, pair * UP_BLOCK), lambda i, j, kk: (kk // 2, i, kk % 2)),
         pl.BlockSpec((pair, tn, UP_BLOCK), lambda i, j, kk: (kk, j, 0))],
        pl.BlockSpec((tm, tn), lambda i, j, kk: (i, j)),
        jax.ShapeDtypeStruct((t, D_MODEL), F32), (t // tm, D_MODEL // tn, N_DEV // pair), "nt", N_DEV // pair,
        (tm, tn), name="up_proj_dx", load_b=lambda ref: jnp.concatenate([ref[0], ref[1]], axis=1))


def _up_proj_dw(h2, du, *, tm, tk):
    t = h2.shape[0]
    return _matmul_call(
        [h2, du],
        [pl.BlockSpec((tk, tm), lambda j, i, kk: (kk, i)),
         pl.BlockSpec((None, tk, UP_BLOCK), lambda j, i, kk: (j // N_CHIP, kk, j % N_CHIP))],
        pl.BlockSpec((None, tm, UP_BLOCK), lambda j, i, kk: (j, i, 0)),
        jax.ShapeDtypeStruct((N_DEV, D_MODEL, UP_BLOCK), F32), (N_DEV, D_MODEL // tm, t // tk), "tn", t // tk,
        (tm, UP_BLOCK), name="up_proj_dw")


IN_TILE = 512


def _in_proj_dw(d_proj, h1, *, tn):
    t = h1.shape[0]
    table = []
    for tile in range(IN_MAIN // IN_TILE):
        dst, src, _ = max(s for s in SEGMENTS.values() if s[0] <= tile * IN_TILE)
        assert (src + tile * IN_TILE - dst) % IN_TILE == 0
        table.append((src + tile * IN_TILE - dst) // IN_TILE)
    assert sorted(table) == list(range(IN_MAIN // IN_TILE))
    return _matmul_call(
        [d_proj, h1],
        [pl.BlockSpec((t, IN_TILE), lambda j, i, kk, tab: (0, i)),
         pl.BlockSpec((t, tn), lambda j, i, kk, tab: (0, j))],
        pl.BlockSpec((IN_TILE, tn), lambda j, i, kk, tab: (tab[i], j)),
        jax.ShapeDtypeStruct((IN_TOTAL, D_MODEL), F32), (D_MODEL // tn, IN_MAIN // IN_TILE, 1), "tn", 1, None,
        name="in_proj_dw", prefetch=jnp.asarray(table, jnp.int32))


def _in_proj_dw_lr(dw_t, d_lr, h1):
    t = h1.shape[0]
    n_lr = IN_TOTAL - IN_MAIN
    tn = 512

    def body(dw_ref, dlr_ref, h1_ref, out_ref):
        full = lax.dot_general(dlr_ref[...], h1_ref[...], _DIMS["tn"], preferred_element_type=F32)
        out_ref[...] = full[:n_lr]

    return pl.pallas_call(
        body, name="in_proj_dw_lr", grid=(D_MODEL // tn,),
        in_specs=[pl.BlockSpec(memory_space=pl.ANY),
                  pl.BlockSpec((t, LR_PAD), lambda j: (0, 0)),
                  pl.BlockSpec((t, tn), lambda j: (0, j))],
        out_specs=pl.BlockSpec((n_lr, tn), lambda j: (IN_MAIN // n_lr, j)),
        out_shape=jax.ShapeDtypeStruct(dw_t.shape, F32),
        input_output_aliases={0: 0},
        compiler_params=_params(("parallel",), VMEM_LIMIT),
    )(dw_t, d_lr, h1)


def _rmsnorm_fwd(x, g, *, name, tr=512):
    t, d = x.shape

    def body(x_ref, g_ref, h_ref):
        h_ref[...] = _rms(x_ref[...], g_ref[...]).astype(BF16)

    return pl.pallas_call(
        body, name=name, grid=(t // tr,),
        in_specs=[pl.BlockSpec((tr, d), lambda i: (i, 0)), pl.BlockSpec((1, d), lambda i: (0, 0))],
        out_specs=pl.BlockSpec((tr, d), lambda i: (i, 0)),
        out_shape=jax.ShapeDtypeStruct((t, d), BF16),
        compiler_params=_params(("parallel",), VMEM_LIMIT),
    )(x, g)


def _rmsnorm_bwd(x, g, dh, dres, *, name, tr=256):
    t, d = x.shape

    def body(x_ref, g_ref, dh_ref, dres_ref, dx_ref, dxb_ref, dg_ref):
        _, vjp = jax.vjp(_rms, x_ref[...], g_ref[...])
        dx, dg = vjp(dh_ref[...])
        dx = dx + dres_ref[...]
        dx_ref[...] = dx
        dxb_ref[...] = dx.astype(BF16)

        @pl.when(pl.program_id(0) == 0)
        def _():
            dg_ref[...] = jnp.zeros_like(dg_ref)

        dg_ref[...] += dg

    row = pl.BlockSpec((tr, d), lambda i: (i, 0))
    vec = pl.BlockSpec((1, d), lambda i: (0, 0))
    return pl.pallas_call(
        body, name=name, grid=(t // tr,),
        in_specs=[row, vec, row, row],
        out_specs=[row, row, vec],
        out_shape=[jax.ShapeDtypeStruct((t, d), F32), jax.ShapeDtypeStruct((t, d), BF16),
                   jax.ShapeDtypeStruct((1, d), F32)],
        compiler_params=_params(("arbitrary",), VMEM_LIMIT),
    )(x, g, dh, dres)


def _seg_block(name, width):
    off = SEGMENTS[name][0]
    assert off % width == 0
    return off // width


def _attn_prep_fwd(proj, cos, sin_signed, gq, gk, *, tr=256):
    t = proj.shape[0]

    def body(q_ref, k_ref, v_ref, cos_ref, sin_ref, gq_ref, gk_ref, qo_ref, ko_ref, vo_ref):
        cos_t, sin_t = cos_ref[...], sin_ref[...]
        for h in range(ATTN_HEADS):
            cols = slice(h * HEAD_DIM, (h + 1) * HEAD_DIM)
            qo_ref[:, cols] = _rope(_rms(q_ref[:, cols], gq_ref[...]), cos_t, sin_t).astype(BF16)
        for h in range(KV_HEADS):
            cols = slice(h * HEAD_DIM, (h + 1) * HEAD_DIM)
            ko_ref[:, cols] = _rope(_rms(k_ref[:, cols], gk_ref[...]), cos_t, sin_t).astype(BF16)
        vo_ref[...] = v_ref[...].astype(BF16)

    qb, kb, vb = _seg_block("qa", ATTN_WIDTH), _seg_block("ka", KV_WIDTH), _seg_block("va", KV_WIDTH)
    tab = pl.BlockSpec((tr, HEAD_DIM), lambda i: (i, 0))
    vec = pl.BlockSpec((1, HEAD_DIM), lambda i: (0, 0))
    return pl.pallas_call(
        body, name="attn_prep_fwd", grid=(t // tr,),
        in_specs=[pl.BlockSpec((tr, ATTN_WIDTH), lambda i: (i, qb)),
                  pl.BlockSpec((tr, KV_WIDTH), lambda i: (i, kb)),
                  pl.BlockSpec((tr, KV_WIDTH), lambda i: (i, vb)),
                  tab, tab, vec, vec],
        out_specs=[pl.BlockSpec((tr, ATTN_WIDTH), lambda i: (i, 0)),
                   pl.BlockSpec((tr, KV_WIDTH), lambda i: (i, 0)),
                   pl.BlockSpec((tr, KV_WIDTH), lambda i: (i, 0))],
        out_shape=[jax.ShapeDtypeStruct((t, ATTN_WIDTH), BF16),
                   jax.ShapeDtypeStruct((t, KV_WIDTH), BF16),
                   jax.ShapeDtypeStruct((t, KV_WIDTH), BF16)],
        compiler_params=_params(("parallel",), VMEM_LIMIT),
    )(proj, proj, proj, cos, sin_signed, gq, gk)


def _attn_heads(q, kcat, vcat, sink_col, valid):
    s = bdot(q, kcat, "nt") * (HEAD_DIM ** -0.5)
    s = jnp.where(valid, s, -jnp.inf)
    m = lax.stop_gradient(jnp.maximum(jnp.max(s, axis=-1, keepdims=True), sink_col))
    p = jnp.exp(s - m)
    p = p / (jnp.sum(p, axis=-1, keepdims=True) + jnp.exp(sink_col - m))
    return bdot(p, vcat, "nn")


def _attn_valid(n, t):
    shape = (GQA_GROUP * ATTN_BLOCK, 3 * ATTN_BLOCK)
    qi = lax.broadcasted_iota(jnp.int32, shape, 0) % ATTN_BLOCK
    sj = lax.broadcasted_iota(jnp.int32, shape, 1)
    kpos = n * ATTN_BLOCK - ATTN_BLOCK + sj
    return (jnp.abs(sj - ATTN_BLOCK - qi) <= WINDOW) & (kpos >= 0) & (kpos < t)


def _head_rows(g):
    return slice(g * ATTN_BLOCK, (g + 1) * ATTN_BLOCK)


def _head_cols(g):
    return slice(g * HEAD_DIM, (g + 1) * HEAD_DIM)


def _stack_heads(ref):
    return jnp.concatenate([ref[:, _head_cols(g)] for g in range(GQA_GROUP)], axis=0).astype(F32)


def _sink_column(sink_ref, h):
    return jnp.concatenate([jnp.full((ATTN_BLOCK, 1), sink_ref[h * GQA_GROUP + g], F32)
                            for g in range(GQA_GROUP)], axis=0)


def _attn_specs(nb):
    q_spec = pl.BlockSpec((ATTN_BLOCK, GQA_GROUP * HEAD_DIM), lambda h, n: (n, h))
    kv_specs = [
        pl.BlockSpec((ATTN_BLOCK, HEAD_DIM), lambda h, n: (jnp.maximum(n - 1, 0), h)),
        pl.BlockSpec((ATTN_BLOCK, HEAD_DIM), lambda h, n: (n, h)),
        pl.BlockSpec((ATTN_BLOCK, HEAD_DIM), lambda h, n: (jnp.minimum(n + 1, nb - 1), h)),
    ]
    return q_spec, kv_specs


def _attn_fwd(q, k, v, sink):
    t = q.shape[0]
    nb = t // ATTN_BLOCK

    def body(sink_ref, q_ref, kp_ref, kc_ref, kn_ref, vp_ref, vc_ref, vn_ref, o_ref):
        h, n = pl.program_id(0), pl.program_id(1)
        valid = _attn_valid(n, t)
        kcat = jnp.concatenate([kp_ref[...], kc_ref[...], kn_ref[...]], axis=0).astype(F32)
        vcat = jnp.concatenate([vp_ref[...], vc_ref[...], vn_ref[...]], axis=0).astype(F32)
        o = _attn_heads(_stack_heads(q_ref), kcat, vcat, _sink_column(sink_ref, h), valid).astype(BF16)
        for g in range(GQA_GROUP):
            o_ref[:, _head_cols(g)] = o[_head_rows(g)]

    q_spec, kv_specs = _attn_specs(nb)
    return pl.pallas_call(
        body, name="attn_fwd", grid=(KV_HEADS, nb),
        in_specs=[pl.BlockSpec(memory_space=pltpu.SMEM), q_spec] + kv_specs + kv_specs,
        out_specs=q_spec,
        out_shape=jax.ShapeDtypeStruct((t, ATTN_WIDTH), BF16),
        compiler_params=_params(("parallel", "parallel"), VMEM_LIMIT),
    )(sink, q, k, k, k, v, v, v)


def _attn_bwd(q, k, v, sink, dmix):
    t = q.shape[0]
    nb = t // ATTN_BLOCK

    def body(sink_ref, q_ref, kp_ref, kc_ref, kn_ref, vp_ref, vc_ref, vn_ref, do_ref,
             dq_ref, dk_lo, dk_mid, dk_hi, dv_lo, dv_mid, dv_hi, dsink_ref):
        h, n = pl.program_id(0), pl.program_id(1)
        valid = _attn_valid(n, t)
        kcat = jnp.concatenate([kp_ref[...], kc_ref[...], kn_ref[...]], axis=0).astype(F32)
        vcat = jnp.concatenate([vp_ref[...], vc_ref[...], vn_ref[...]], axis=0).astype(F32)
        _, vjp = jax.vjp(functools.partial(_attn_heads, valid=valid),
                         _stack_heads(q_ref), kcat, vcat, _sink_column(sink_ref, h))
        dq, dk, dv, dsink_col = vjp(_stack_heads(do_ref))
        row = lax.broadcasted_iota(jnp.int32, (8, HEAD_DIM), 0)
        dsink = jnp.zeros((8, HEAD_DIM), F32)
        for g in range(GQA_GROUP):
            dq_ref[:, _head_cols(g)] = dq[_head_rows(g)]
            dsink = dsink + jnp.where(row == g, jnp.sum(dsink_col[_head_rows(g)]), 0.0)
        for i, (dk_ref, dv_ref) in enumerate(((dk_lo, dv_lo), (dk_mid, dv_mid), (dk_hi, dv_hi))):
            rows = slice(i * ATTN_BLOCK, (i + 1) * ATTN_BLOCK)
            dk_ref[...] = dk[rows]
            dv_ref[...] = dv[rows]

        @pl.when(n == 0)
        def _():
            dsink_ref[...] = jnp.zeros_like(dsink_ref)

        dsink_ref[...] += dsink

    q_spec, kv_specs = _attn_specs(nb)
    kv_out = pl.BlockSpec((ATTN_BLOCK, HEAD_DIM), lambda h, n: (n, h))
    kv_shape = jax.ShapeDtypeStruct((t, KV_WIDTH), F32)
    return pl.pallas_call(
        body, name="attn_bwd", grid=(KV_HEADS, nb),
        in_specs=[pl.BlockSpec(memory_space=pltpu.SMEM), q_spec] + kv_specs + kv_specs + [q_spec],
        out_specs=[q_spec] + [kv_out] * 6 + [pl.BlockSpec((None, 8, HEAD_DIM), lambda h, n: (h, 0, 0))],
        out_shape=[jax.ShapeDtypeStruct((t, ATTN_WIDTH), F32)] + [kv_shape] * 6
                  + [jax.ShapeDtypeStruct((KV_HEADS, 8, HEAD_DIM), F32)],
        compiler_params=_params(("parallel", "arbitrary"), VMEM_LIMIT),
    )(sink, q, k, k, k, v, v, v, dmix)


def _attn_prep_bwd(proj, cos, sin_signed, gq, gk, dq, dks, dvs):
    t = proj.shape[0]
    tr = ATTN_BLOCK
    nb = t // tr

    def body(q_ref, k_ref, cos_ref, sin_ref, gq_ref, gk_ref, dq_ref,
             dk_lo, dk_mid, dk_hi, dv_lo, dv_mid, dv_hi,
             dqo_ref, dko_ref, dvo_ref, dgq_ref, dgk_ref):
        n = pl.program_id(0)
        cos_t, sin_t = cos_ref[...], sin_ref[...]
        has_next = (n < nb - 1).astype(F32)
        has_prev = (n > 0).astype(F32)
        dk = dk_lo[...] * has_next + dk_mid[...] + dk_hi[...] * has_prev
        dv = dv_lo[...] * has_next + dv_mid[...] + dv_hi[...] * has_prev
        dvo_ref[...] = dv.astype(BF16)
        dgq = jnp.zeros((1, HEAD_DIM), F32)
        dgk = jnp.zeros((1, HEAD_DIM), F32)
        for h in range(ATTN_HEADS):
            cols = slice(h * HEAD_DIM, (h + 1) * HEAD_DIM)
            _, vjp = jax.vjp(_rms, q_ref[:, cols], gq_ref[...])
            dx, dg = vjp(_rope_transposed(dq_ref[:, cols], cos_t, sin_t))
            dqo_ref[:, cols] = dx.astype(BF16)
            dgq = dgq + dg
        for h in range(KV_HEADS):
            cols = slice(h * HEAD_DIM, (h + 1) * HEAD_DIM)
            _, vjp = jax.vjp(_rms, k_ref[:, cols], gk_ref[...])
            dx, dg = vjp(_rope_transposed(dk[:, cols], cos_t, sin_t))
            dko_ref[:, cols] = dx.astype(BF16)
            dgk = dgk + dg

        @pl.when(n == 0)
        def _():
            dgq_ref[...] = jnp.zeros_like(dgq_ref)
            dgk_ref[...] = jnp.zeros_like(dgk_ref)

        dgq_ref[...] += dgq
        dgk_ref[...] += dgk

    qb, kb = _seg_block("qa", ATTN_WIDTH), _seg_block("ka", KV_WIDTH)
    tab = pl.BlockSpec((tr, HEAD_DIM), lambda i: (i, 0))
    vec = pl.BlockSpec((1, HEAD_DIM), lambda i: (0, 0))
    kv = [pl.BlockSpec((tr, KV_WIDTH), lambda i: (jnp.minimum(i + 1, nb - 1), 0)),
          pl.BlockSpec((tr, KV_WIDTH), lambda i: (i, 0)),
          pl.BlockSpec((tr, KV_WIDTH), lambda i: (jnp.maximum(i - 1, 0), 0))]
    wide = pl.BlockSpec((tr, ATTN_WIDTH), lambda i: (i, 0))
    narrow = pl.BlockSpec((tr, KV_WIDTH), lambda i: (i, 0))
    return pl.pallas_call(
        body, name="attn_prep_bwd", grid=(nb,),
        in_specs=[pl.BlockSpec((tr, ATTN_WIDTH), lambda i: (i, qb)),
                  pl.BlockSpec((tr, KV_WIDTH), lambda i: (i, kb)),
                  tab, tab, vec, vec, wide] + kv + kv,
        out_specs=[wide, narrow, narrow, vec, vec],
        out_shape=[jax.ShapeDtypeStruct((t, ATTN_WIDTH), BF16),
                   jax.ShapeDtypeStruct((t, KV_WIDTH), BF16),
                   jax.ShapeDtypeStruct((t, KV_WIDTH), BF16),
                   jax.ShapeDtypeStruct((1, HEAD_DIM), F32),
                   jax.ShapeDtypeStruct((1, HEAD_DIM), F32)],
        compiler_params=_params(("arbitrary",), VMEM_LIMIT),
    )(proj, proj, cos, sin_signed, gq, gk, dq, *dks, *dvs)


def _decay_fn(lr, w2, ba):
    return _log_sigmoid(bdot(lr, w2, "nn") + ba) / GLA_GATE_NORMALIZER


def _gla_prep_fwd(proj_lr, w2, ba2, *, tr=512):
    t = proj_lr.shape[0]
    width = 2 * GLA_KEY_WIDTH

    def body(lr_ref, w2_ref, ba_ref, g_ref):
        g_ref[...] = _decay_fn(lr_ref[...], w2_ref[...], ba_ref[...])

    return pl.pallas_call(
        body, name="gla_prep_fwd", grid=(t // tr,),
        in_specs=[pl.BlockSpec((tr, LR_PAD), lambda i: (i, 0)),
                  pl.BlockSpec((LR_PAD, width), lambda i: (0, 0)),
                  pl.BlockSpec((1, width), lambda i: (0, 0))],
        out_specs=pl.BlockSpec((tr, width), lambda i: (i, 0)),
        out_shape=jax.ShapeDtypeStruct((t, width), F32),
        compiler_params=_params(("parallel",), VMEM_LIMIT),
    )(proj_lr, w2, ba2)


def _gla_prep_bwd(proj_lr, w2, ba2, dg_f, dg_b, *, tr=512):
    t = proj_lr.shape[0]
    width = 2 * GLA_KEY_WIDTH

    def body(lr_ref, w2_ref, ba_ref, dgf_ref, dgb_ref, dlr_ref, dw2_ref, dba_ref):
        _, vjp = jax.vjp(_decay_fn, lr_ref[...], w2_ref[...], ba_ref[...])
        dlr, dw2, dba = vjp(jnp.concatenate([dgf_ref[...], dgb_ref[...]], axis=1))
        dlr_ref[...] = dlr.astype(BF16)

        @pl.when(pl.program_id(0) == 0)
        def _():
            dw2_ref[...] = jnp.zeros_like(dw2_ref)
            dba_ref[...] = jnp.zeros_like(dba_ref)

        dw2_ref[...] += dw2
        dba_ref[...] += dba

    half = pl.BlockSpec((tr, GLA_KEY_WIDTH), lambda i: (i, 0))
    return pl.pallas_call(
        body, name="gla_prep_bwd", grid=(t // tr,),
        in_specs=[pl.BlockSpec((tr, LR_PAD), lambda i: (i, 0)),
                  pl.BlockSpec((LR_PAD, width), lambda i: (0, 0)),
                  pl.BlockSpec((1, width), lambda i: (0, 0)), half, half],
        out_specs=[pl.BlockSpec((tr, LR_PAD), lambda i: (i, 0)),
                   pl.BlockSpec((LR_PAD, width), lambda i: (0, 0)),
                   pl.BlockSpec((1, width), lambda i: (0, 0))],
        out_shape=[jax.ShapeDtypeStruct((t, LR_PAD), BF16),
                   jax.ShapeDtypeStruct((LR_PAD, width), F32),
                   jax.ShapeDtypeStruct((1, width), F32)],
        compiler_params=_params(("arbitrary",), VMEM_LIMIT),
    )(proj_lr, w2, ba2, dg_f, dg_b)


def _gla_k(h):
    return slice(h * GLA_DK, (h + 1) * GLA_DK)


def _gla_v(h):
    return slice(h * GLA_DV, (h + 1) * GLA_DV)


def _running_sum(x, downward):
    n = x.shape[0]
    row = lax.broadcasted_iota(jnp.int32, x.shape, 0)
    step = 1
    while step < n:
        if downward:
            x = x + jnp.where(row >= step, pltpu.roll(x, step, 0), 0.0)
        else:
            x = x + jnp.where(row < n - step, pltpu.roll(x, n - step, 0), 0.0)
        step *= 2
    return x


@functools.partial(jax.custom_vjp, nondiff_argnums=(1,))
def _cumsum_rows(x, downward):
    return _running_sum(x, downward)


def _cumsum_rows_fwd(x, downward):
    return _running_sum(x, downward), None


def _cumsum_rows_bwd(downward, _, ct):
    return (_running_sum(ct, not downward),)


_cumsum_rows.defvjp(_cumsum_rows_fwd, _cumsum_rows_bwd)


def _gla_chunk(q, k, v, g, state, forward):
    c = GLA_CHUNK
    row = lax.broadcasted_iota(jnp.int32, (c, c), 0)
    col = lax.broadcasted_iota(jnp.int32, (c, c), 1)
    rid = lax.broadcasted_iota(jnp.int32, (c, GLA_DK), 0)
    q = q * (GLA_DK ** -0.5)
    if forward:
        see = row >= col
        upto_ref = rid <= c // 2
    else:
        see = row < col
        upto_ref = rid >= c - 1 - c // 2
    b = _cumsum_rows(g, forward)
    b_last = jnp.sum(g, axis=0, keepdims=True)
    b_ref = jnp.sum(jnp.where(upto_ref, g, 0.0), axis=0, keepdims=True)
    a = bdot(q * jnp.exp(b - b_ref), k * jnp.exp(b_ref - b), "nt")
    a = jnp.where(see, a, 0.0)
    o = bdot(a, v, "nn") + bdot(q * jnp.exp(b), state, "nt")
    new_state = state * jnp.exp(b_last) + bdot(v, k * jnp.exp(b_last - b), "tn")
    return o, new_state


def _gla_fwd(proj, g):
    t = proj.shape[0]
    c = GLA_CHUNK
    nchunk = t // c
    qb, kb, vb = _seg_block("qg", GLA_KEY_WIDTH), _seg_block("kg", GLA_KEY_WIDTH), _seg_block("vg", GLA_WIDTH)

    def body(qf, kf, vf, gf, qr, kr, vr, gr, of_ref, ob_ref, sf_ref, sb_ref, state):
        @pl.when(pl.program_id(0) == 0)
        def _():
            state[...] = jnp.zeros_like(state)

        dirs = ((qf, kf, vf, gf, of_ref, sf_ref), (qr, kr, vr, gr, ob_ref, sb_ref))
        args = [(q_ref[:, _gla_k(h)], k_ref[:, _gla_k(h)], v_ref[:, _gla_v(h)], g_ref[:, _gla_k(h)], state[d, h])
                for d, (q_ref, k_ref, v_ref, g_ref, _, _) in enumerate(dirs) for h in range(GLA_HEADS)]
        results = [_gla_chunk(*a, forward=(i < GLA_HEADS)) for i, a in enumerate(args)]
        for i, (a, (o, s_out)) in enumerate(zip(args, results)):
            d, h = divmod(i, GLA_HEADS)
            dirs[d][5][h] = a[4]
            dirs[d][4][:, _gla_v(h)] = o
            state[d, h] = s_out

    specs, outs = [], []
    for d in range(2):
        ci = (lambda i: i) if d == 0 else (lambda i: nchunk - 1 - i)
        specs += [pl.BlockSpec((c, GLA_KEY_WIDTH), lambda i, ci=ci: (ci(i), qb)),
                  pl.BlockSpec((c, GLA_KEY_WIDTH), lambda i, ci=ci: (ci(i), kb)),
                  pl.BlockSpec((c, GLA_WIDTH), lambda i, ci=ci: (ci(i), vb)),
                  pl.BlockSpec((c, GLA_KEY_WIDTH), lambda i, ci=ci, d=d: (ci(i), d))]
        outs.append(pl.BlockSpec((c, GLA_WIDTH), lambda i, ci=ci: (ci(i), 0)))
    for d in range(2):
        ci = (lambda i: i) if d == 0 else (lambda i: nchunk - 1 - i)
        outs.append(pl.BlockSpec((None, GLA_HEADS, GLA_DV, GLA_DK), lambda i, ci=ci: (ci(i), 0, 0, 0)))
    o_shape = jax.ShapeDtypeStruct((t, GLA_WIDTH), F32)
    s_shape = jax.ShapeDtypeStruct((nchunk, GLA_HEADS, GLA_DV, GLA_DK), F32)
    return pl.pallas_call(
        body, name="gla_fwd", grid=(nchunk,),
        in_specs=specs, out_specs=outs,
        out_shape=[o_shape, o_shape, s_shape, s_shape],
        scratch_shapes=[pltpu.VMEM((2, GLA_HEADS, GLA_DV, GLA_DK), F32)],
        compiler_params=_params(("arbitrary",), VMEM_LIMIT),
    )(proj, proj, proj, g, proj, proj, proj, g)


def _gla_bwd(proj, g, s_f, s_b, do):
    t = proj.shape[0]
    c = GLA_CHUNK
    nchunk = t // c
    qb, kb, vb = _seg_block("qg", GLA_KEY_WIDTH), _seg_block("kg", GLA_KEY_WIDTH), _seg_block("vg", GLA_WIDTH)

    def body(*refs):
        ins, outs, dstate = refs[:12], refs[12:20], refs[20]

        @pl.when(pl.program_id(0) == 0)
        def _():
            dstate[...] = jnp.zeros_like(dstate)

        loaded = []
        for d in range(2):
            q_ref, k_ref, v_ref, g_ref, s_ref, do_ref = ins[6 * d:6 * d + 6]
            for h in range(GLA_HEADS):
                loaded.append(((q_ref[:, _gla_k(h)], k_ref[:, _gla_k(h)], v_ref[:, _gla_v(h)], g_ref[:, _gla_k(h)],
                                s_ref[h]), (do_ref[:, _gla_v(h)], dstate[d, h])))
        grads = []
        for i, (primals, cotangents) in enumerate(loaded):
            _, vjp = jax.vjp(functools.partial(_gla_chunk, forward=(i < GLA_HEADS)), *primals)
            grads.append(vjp(cotangents))
        for i, (dq, dk, dv, dg, ds) in enumerate(grads):
            d, h = divmod(i, GLA_HEADS)
            dq_ref, dk_ref, dv_ref, dg_ref = outs[4 * d:4 * d + 4]
            dq_ref[:, _gla_k(h)] = dq
            dk_ref[:, _gla_k(h)] = dk
            dv_ref[:, _gla_v(h)] = dv
            dg_ref[:, _gla_k(h)] = dg
            dstate[d, h] = ds

    specs, outs, shapes = [], [], []
    for d in range(2):
        ci = (lambda i: nchunk - 1 - i) if d == 0 else (lambda i: i)
        specs += [pl.BlockSpec((c, GLA_KEY_WIDTH), lambda i, ci=ci: (ci(i), qb)),
                  pl.BlockSpec((c, GLA_KEY_WIDTH), lambda i, ci=ci: (ci(i), kb)),
                  pl.BlockSpec((c, GLA_WIDTH), lambda i, ci=ci: (ci(i), vb)),
                  pl.BlockSpec((c, GLA_KEY_WIDTH), lambda i, ci=ci, d=d: (ci(i), d)),
                  pl.BlockSpec((None, GLA_HEADS, GLA_DV, GLA_DK), lambda i, ci=ci: (ci(i), 0, 0, 0)),
                  pl.BlockSpec((c, GLA_WIDTH), lambda i, ci=ci: (ci(i), 0))]
        key = pl.BlockSpec((c, GLA_KEY_WIDTH), lambda i, ci=ci: (ci(i), 0))
        val = pl.BlockSpec((c, GLA_WIDTH), lambda i, ci=ci: (ci(i), 0))
        outs += [key, key, val, key]
        shapes += [jax.ShapeDtypeStruct((t, GLA_KEY_WIDTH), F32), jax.ShapeDtypeStruct((t, GLA_KEY_WIDTH), F32),
                   jax.ShapeDtypeStruct((t, GLA_WIDTH), F32), jax.ShapeDtypeStruct((t, GLA_KEY_WIDTH), F32)]
    return pl.pallas_call(
        body, name="gla_bwd", grid=(nchunk,),
        in_specs=specs, out_specs=outs, out_shape=shapes,
        scratch_shapes=[pltpu.VMEM((2, GLA_HEADS, GLA_DV, GLA_DK), F32)],
        compiler_params=_params(("arbitrary",), VMEM_LIMIT),
    )(proj, proj, proj, g, s_f, do, proj, proj, proj, g, s_b, do)


def _gla_out_head(o_f, o_b, gate, gn):
    return _rms(o_f + o_b, gn) * _silu(gate)


def _gla_out_fwd(o_f, o_b, proj, gn, *, tr=256):
    t = o_f.shape[0]
    gb = _seg_block("gate", GLA_WIDTH)

    def body(of_ref, ob_ref, gate_ref, gn_ref, out_ref):
        for h in range(GLA_HEADS):
            vc = slice(h * GLA_DV, (h + 1) * GLA_DV)
            out_ref[:, vc] = _gla_out_head(of_ref[:, vc], ob_ref[:, vc], gate_ref[:, vc], gn_ref[...]).astype(BF16)

    wide = pl.BlockSpec((tr, GLA_WIDTH), lambda i: (i, 0))
    return pl.pallas_call(
        body, name="gla_out_fwd", grid=(t // tr,),
        in_specs=[wide, wide, pl.BlockSpec((tr, GLA_WIDTH), lambda i: (i, gb)),
                  pl.BlockSpec((1, GLA_DV), lambda i: (0, 0))],
        out_specs=wide,
        out_shape=jax.ShapeDtypeStruct((t, GLA_WIDTH), BF16),
        compiler_params=_params(("parallel",), VMEM_LIMIT),
    )(o_f, o_b, proj, gn)


def _gla_out_bwd(o_f, o_b, proj, gn, dmix, *, tr=256):
    t = o_f.shape[0]
    gb = _seg_block("gate", GLA_WIDTH)

    def body(of_ref, ob_ref, gate_ref, gn_ref, dout_ref, do_ref, dgate_ref, dgn_ref):
        dgn = jnp.zeros((1, GLA_DV), F32)
        for h in range(GLA_HEADS):
            vc = slice(h * GLA_DV, (h + 1) * GLA_DV)
            _, vjp = jax.vjp(_gla_out_head, of_ref[:, vc], ob_ref[:, vc], gate_ref[:, vc], gn_ref[...])
            do, _, dgate, dg = vjp(dout_ref[:, vc])
            do_ref[:, vc] = do
            dgate_ref[:, vc] = dgate.astype(BF16)
            dgn = dgn + dg

        @pl.when(pl.program_id(0) == 0)
        def _():
            dgn_ref[...] = jnp.zeros_like(dgn_ref)

        dgn_ref[...] += dgn

    wide = pl.BlockSpec((tr, GLA_WIDTH), lambda i: (i, 0))
    vec = pl.BlockSpec((1, GLA_DV), lambda i: (0, 0))
    return pl.pallas_call(
        body, name="gla_out_bwd", grid=(t // tr,),
        in_specs=[wide, wide, pl.BlockSpec((tr, GLA_WIDTH), lambda i: (i, gb)), vec,
                  pl.BlockSpec((tr, GLA_WIDTH), lambda i: (i, 1))],
        out_specs=[wide, wide, vec],
        out_shape=[jax.ShapeDtypeStruct((t, GLA_WIDTH), F32), jax.ShapeDtypeStruct((t, GLA_WIDTH), BF16),
                   jax.ShapeDtypeStruct((1, GLA_DV), F32)],
        compiler_params=_params(("arbitrary",), VMEM_LIMIT),
    )(o_f, o_b, proj, gn, dmix)


CONV_TR = 1024
CONV_TC = 512
HALO = 8
HALO16 = 16


def _conv3(u, w, b):
    n = u.shape[0]
    return pltpu.roll(u, 1, 0) * w[0:1] + u * w[1:2] + pltpu.roll(u, n - 1, 0) * w[2:3] + b


def _conv_ext(main_ref, prev_ref, next_ref, r, nr):
    prev = prev_ref[...].astype(F32)[-HALO:] * (r > 0).astype(F32)
    nxt = next_ref[...].astype(F32)[:HALO] * (r < nr - 1).astype(F32)
    return jnp.concatenate([prev, main_ref[...].astype(F32), nxt], axis=0)


def _conv_specs(t, halo, half=None):
    per = CONV_TR // halo
    last = t // halo - 1
    lead = () if half is None else (None,)
    at = (lambda *ix: ix) if half is None else (lambda *ix: (half,) + ix)
    return [pl.BlockSpec(lead + (CONV_TR, CONV_TC), lambda j, r: at(r, j)),
            pl.BlockSpec(lead + (halo, CONV_TC), lambda j, r: at(jnp.maximum(r * per - 1, 0), j)),
            pl.BlockSpec(lead + (halo, CONV_TC), lambda j, r: at(jnp.minimum((r + 1) * per, last), j))]


def _ffn_mid_fwd(u, cw_g, cw_v, cb_g, cb_v):
    _, t, f = u.shape
    nr = t // CONV_TR

    def body(ug, ugp, ugn, uv, uvp, uvn, wg, wv, bg, bv, a_ref):
        r = pl.program_id(1)
        gate = _conv3(_conv_ext(ug, ugp, ugn, r, nr), wg[...], bg[...])[HALO:HALO + CONV_TR]
        val = _conv3(_conv_ext(uv, uvp, uvn, r, nr), wv[...], bv[...])[HALO:HALO + CONV_TR]
        a_ref[...] = (_silu(gate) * val).astype(BF16)

    w_spec = pl.BlockSpec((3, CONV_TC), lambda j, r: (0, j))
    b_spec = pl.BlockSpec((1, CONV_TC), lambda j, r: (0, j))
    return pl.pallas_call(
        body, name="ffn_mid_fwd", grid=(f // CONV_TC, nr),
        in_specs=_conv_specs(t, HALO, 0) + _conv_specs(t, HALO, 1) + [w_spec, w_spec, b_spec, b_spec],
        out_specs=pl.BlockSpec((CONV_TR, CONV_TC), lambda j, r: (r, j)),
        out_shape=jax.ShapeDtypeStruct((t, f), BF16),
        compiler_params=_params(("parallel", "parallel"), VMEM_LIMIT),
    )(u, u, u, u, u, u, cw_g, cw_v, cb_g, cb_v)


def _ffn_mid_bwd(u, cw_g, cw_v, cb_g, cb_v, da):
    _, t, f = u.shape
    nr = t // CONV_TR
    ext = CONV_TR + 2 * HALO

    def body(ug, ugp, ugn, uv, uvp, uvn, dam, dap, dan, wg, wv, bg, bv,
             du_ref, dwg_ref, dwv_ref, dbg_ref, dbv_ref):
        r = pl.program_id(1)
        shifted = []
        for main, prev, nxt in ((ug, ugp, ugn), (uv, uvp, uvn)):
            x = _conv_ext(main, prev, nxt, r, nr)
            shifted.append((pltpu.roll(x, 1, 0), x, pltpu.roll(x, ext - 1, 0)))
        da_x = _conv_ext(dam, dap, dan, r, nr)
        wg_t, wv_t = wg[...], wv[...]
        gate = shifted[0][0] * wg_t[0:1] + shifted[0][1] * wg_t[1:2] + shifted[0][2] * wg_t[2:3] + bg[...]
        val = shifted[1][0] * wv_t[0:1] + shifted[1][1] * wv_t[1:2] + shifted[1][2] * wv_t[2:3] + bv[...]
        sig = jax.nn.sigmoid(gate)
        silu = gate * sig
        d_val = da_x * silu
        d_gate = da_x * val * (sig + silu * (1.0 - sig))
        own = slice(HALO, HALO + CONV_TR)
        for half, (xs3, d, wt, dw_ref, db_ref) in enumerate(((shifted[0], d_gate, wg_t, dwg_ref, dbg_ref),
                                                            (shifted[1], d_val, wv_t, dwv_ref, dbv_ref))):
            du = pltpu.roll(d, ext - 1, 0) * wt[0:1] + d * wt[1:2] + pltpu.roll(d, 1, 0) * wt[2:3]
            du_ref[half] = du[own].astype(BF16)
            d_own = d[own]
            dw = jnp.concatenate([jnp.sum(x[own] * d_own, axis=0, keepdims=True) for x in xs3], axis=0)
            db = jnp.sum(d_own, axis=0, keepdims=True)

            @pl.when(r == 0)
            def _():
                dw_ref[...] = jnp.zeros_like(dw_ref)
                db_ref[...] = jnp.zeros_like(db_ref)

            dw_ref[...] += dw
            db_ref[...] += db

    w_spec = pl.BlockSpec((3, CONV_TC), lambda j, r: (0, j))
    b_spec = pl.BlockSpec((1, CONV_TC), lambda j, r: (0, j))
    return pl.pallas_call(
        body, name="ffn_mid_bwd", grid=(f // CONV_TC, nr),
        in_specs=(_conv_specs(t, HALO, 0) + _conv_specs(t, HALO, 1) + _conv_specs(t, HALO16)
                  + [w_spec, w_spec, b_spec, b_spec]),
        out_specs=[pl.BlockSpec((2, CONV_TR, CONV_TC), lambda j, r: (0, r, j)), w_spec, w_spec, b_spec, b_spec],
        out_shape=[jax.ShapeDtypeStruct((2, t, f), BF16),
                   jax.ShapeDtypeStruct((3, f), F32), jax.ShapeDtypeStruct((3, f), F32),
                   jax.ShapeDtypeStruct((1, f), F32), jax.ShapeDtypeStruct((1, f), F32)],
        compiler_params=_params(("parallel", "arbitrary"), VMEM_LIMIT),
    )(u, u, u, u, u, u, da, da, da, cw_g, cw_v, cb_g, cb_v)


def _down_proj_loss(act, w_down, x1, target, *, tm, tn):
    t, f = act.shape
    d = w_down.shape[1]

    def body(a_ref, w_ref, x_ref, t_ref, loss_ref, dy_ref, dyb_ref):
        y = lax.dot_general(a_ref[...], w_ref[...], _DIMS["nn"], preferred_element_type=F32) + x_ref[...]
        err = y - t_ref[...]
        dy = err * (1.0 / d)
        dy_ref[...] = dy
        dyb_ref[...] = dy.astype(BF16)
        part = 0.5 * jnp.sum(jnp.sum(err * err, axis=-1, keepdims=True) * (1.0 / d), axis=0, keepdims=True)

        @pl.when((pl.program_id(0) == 0) & (pl.program_id(1) == 0))
        def _():
            loss_ref[...] = jnp.zeros_like(loss_ref)

        loss_ref[...] += jnp.broadcast_to(part, loss_ref.shape)

    tile = pl.BlockSpec((tm, tn), lambda i, j: (i, j))
    return pl.pallas_call(
        body, name="down_proj_loss", grid=(t // tm, d // tn),
        in_specs=[pl.BlockSpec((tm, f), lambda i, j: (i, 0)), pl.BlockSpec((f, tn), lambda i, j: (0, j)), tile, tile],
        out_specs=[pl.BlockSpec((1, 128), lambda i, j: (0, 0)), tile, tile],
        out_shape=[jax.ShapeDtypeStruct((1, 128), F32), jax.ShapeDtypeStruct((t, d), F32),
                   jax.ShapeDtypeStruct((t, d), BF16)],
        compiler_params=_params(("arbitrary", "arbitrary"), VMEM_LIMIT),
    )(act, w_down, x1, target)


ANY = pl.BlockSpec(memory_space=pl.ANY)


def _position():
    return lax.axis_index("x"), lax.axis_index("y"), lax.axis_index("c")


def _other_chips(x, y):
    return [(1 - x, y), (x, 1 - y), (1 - x, 1 - y)]


def _handshake(peers):
    barrier = pltpu.get_barrier_semaphore()
    for peer in peers:
        pl.semaphore_signal(barrier, inc=1, device_id=peer, device_id_type=MESH)
    pl.semaphore_wait(barrier, len(peers))


def _exchange(body, operands, out_shapes, sems, *, name, collective_id):
    n_in, n_out = len(operands), len(out_shapes)

    def run(*refs):
        body(refs[:n_in], refs[n_in:n_in + n_out], *refs[n_in + n_out:])

    if collective_id is None:
        return pl.pallas_call(run, name=name, in_specs=[ANY] * n_in, out_specs=[ANY] * n_out,
                              out_shape=out_shapes, scratch_shapes=sems)(*operands)
    return pl.kernel(run, name=name, out_type=out_shapes,
                     mesh=plsc.ScalarSubcoreMesh(axis_name="sequencer", num_cores=1), scratch_types=sems,
                     compiler_params=pltpu.CompilerParams(collective_id=collective_id))(*operands)


def _all_gather(blocks, *, name, collective_id=None):
    na = len(blocks)

    def body(ins, outs, send_sems, recv_sems, local_sems):
        x, y, c = _position()
        me, sibling = (x, y, c), (x, y, 1 - c)
        along_x, along_y, diagonal = (1 - x, y, c), (x, 1 - y, c), (1 - x, 1 - y, c)
        first_c = c == 0
        relay_from = (jnp.where(first_c, x, 1 - x), jnp.where(first_c, 1 - y, y), c)
        relay_to = (jnp.where(first_c, 1 - x, x), jnp.where(first_c, y, 1 - y), c)
        if collective_id is not None:
            _handshake([sibling, along_x, along_y])

        def index(px, py, pc):
            return 4 * px + 2 * py + pc

        def copy(a, k, block, to, src=None):
            dst = outs[a].at[index(*block)]
            return pltpu.make_async_remote_copy(
                src_ref=dst if src is None else src, dst_ref=dst,
                send_sem=send_sems.at[a, k], recv_sem=recv_sems.at[a, k],
                device_id=to, device_id_type=MESH)

        pending = []
        for a in range(na):
            mine = pltpu.make_async_copy(ins[a], outs[a].at[index(*me)], local_sems.at[a])
            mine.start()
            pending.append(mine)
        sent = []
        for a in range(na):
            sent += [copy(a, 0, me, sibling, src=ins[a]), copy(a, 1, me, along_x, src=ins[a]),
                     copy(a, 2, me, along_y, src=ins[a])]
        for cp in sent:
            cp.start()

        def passes_on(k_in, owner, k_out):
            for a in range(na):
                copy(a, k_in, owner, me).wait_recv()
                cp = copy(a, k_out, owner, sibling)
                cp.start()
                sent.append(cp)

        passes_on(1, along_x, 4)
        passes_on(2, along_y, 5)
        for a in range(na):
            cp = copy(a, 3, relay_from, relay_to)
            cp.start()
            sent.append(cp)
        passes_on(3, diagonal, 6)
        for a in range(na):
            copy(a, 0, sibling, me).wait_recv()
            for k, owner in ((4, along_x), (5, along_y), (6, diagonal)):
                copy(a, k, (owner[0], owner[1], 1 - c), me).wait_recv()
        for cp in sent:
            cp.wait_send()
        for cp in pending:
            cp.wait()

    return _exchange(
        body, blocks, [jax.ShapeDtypeStruct((N_DEV,) + b.shape, b.dtype) for b in blocks],
        [pltpu.SemaphoreType.DMA((na, 7)), pltpu.SemaphoreType.DMA((na, 7)), pltpu.SemaphoreType.DMA((na,))],
        name=name, collective_id=collective_id)


def _grad_exchange(grads, parts, *, name, collective_id):
    ng, npart = len(grads), len(parts)

    def body(ins, outs, core_send, core_recv, chip_send, chip_recv, local_sems):
        x, y, c = _position()
        sibling = (x, y, 1 - c)
        chips = _other_chips(x, y)
        _handshake([sibling] + [(px, py, c) for px, py in chips])
        me = 2 * x + y
        copies = []
        for b in range(npart):
            src, dst = ins[ng + b], outs[ng + b]
            own = pltpu.make_async_copy(src.at[me], dst.at[me], local_sems.at[b])
            own.start()
            copies.append(own)
            for j, (px, py) in enumerate(chips):
                cp = pltpu.make_async_remote_copy(
                    src_ref=src.at[2 * px + py], dst_ref=dst.at[me],
                    send_sem=chip_send.at[b, j], recv_sem=chip_recv.at[b, j],
                    device_id=(px, py, c), device_id_type=MESH)
                cp.start()
                copies.append(cp)
        for a in range(ng):
            for k in range(N_CHIP):
                cp = pltpu.make_async_remote_copy(
                    src_ref=ins[a].at[k, 1 - c], dst_ref=outs[a].at[k],
                    send_sem=core_send.at[a, k], recv_sem=core_recv.at[a, k],
                    device_id=sibling, device_id_type=MESH)
                cp.start()
                copies.append(cp)
        for cp in copies:
            cp.wait()

    shapes = ([jax.ShapeDtypeStruct((N_CHIP,) + g.shape[2:], g.dtype) for g in grads]
              + [jax.ShapeDtypeStruct(p.shape, p.dtype) for p in parts])
    sems = [pltpu.SemaphoreType.DMA((max(ng, 1), N_CHIP)), pltpu.SemaphoreType.DMA((max(ng, 1), N_CHIP)),
            pltpu.SemaphoreType.DMA((max(npart, 1), 3)), pltpu.SemaphoreType.DMA((max(npart, 1), 3)),
            pltpu.SemaphoreType.DMA((max(npart, 1),))]
    out = _exchange(body, list(grads) + list(parts), shapes, sems, name=name, collective_id=collective_id)
    return out[:ng], out[ng:]


def _pair_sum(grad, theirs, core, *, tile, name, narrow=False):
    _, _, r, w = grad.shape
    tr, tw = tile
    assert r % tr == 0 and w % tw == 0

    def body(core_ref, mine_ref, theirs_ref, out_ref, *narrow_ref):
        total = mine_ref[...] + theirs_ref[...]
        out_ref[...] = total
        if narrow:
            narrow_ref[0][...] = total.astype(BF16)

    spec = pl.BlockSpec((None, tr, tw), lambda k, i, j, core_ref: (k, i, j))
    shapes = [jax.ShapeDtypeStruct((N_CHIP, r, w), F32)] + [jax.ShapeDtypeStruct((N_CHIP, r, w), BF16)] * narrow
    out = pl.pallas_call(
        body, name=name,
        grid_spec=pltpu.PrefetchScalarGridSpec(
            num_scalar_prefetch=1, grid=(N_CHIP, r // tr, w // tw),
            in_specs=[pl.BlockSpec((None, None, tr, tw), lambda k, i, j, core_ref: (k, core_ref[0], i, j)), spec],
            out_specs=[spec] * len(shapes)),
        out_shape=shapes,
        compiler_params=_params(("parallel", "parallel", "parallel"), VMEM_LIMIT),
    )(core, grad, theirs)
    return tuple(out) if narrow else out[0]


class _ReduceScatter:
    def __init__(self, core):
        self.core = core
        self.pending = None
        self.results = {}
        self.launches = 0

    def push(self, tag, grads, rows, then, narrow=False):
        pair, kept, prev_tag = [], [], None
        if self.pending is not None:
            prev_tag, prev, theirs, prev_rows, prev_narrow = self.pending
            pair = [_pair_sum(g, s, self.core, tile=tile, name=f"pair_sum_{prev_tag}_{i}", narrow=prev_narrow)
                    for i, (g, s, tile) in enumerate(zip(prev, theirs, prev_rows))]
            if prev_narrow:
                kept, pair = [p[0] for p in pair], [p[1] for p in pair]
        grads, pair, then = lax.optimization_barrier((list(grads), pair, then))
        grads = [g.reshape((N_CHIP, 2) + g.shape[1:]) for g in grads]
        self.launches += 1
        theirs, parts = _grad_exchange(grads, pair, name=f"grad_exchange_{self.launches}",
                                       collective_id=1 + self.launches)
        if prev_tag is not None:
            self.results[prev_tag] = (parts, kept)
        self.pending = (tag, grads, theirs, rows, narrow) if tag is not None else None
        return then

    def result(self, tag):
        return self.results[tag]


def _adamw(parts, w, m, v, *, tile, name, own=None, chip=None):
    n, r, cols = parts.shape
    tr, tw = tile
    assert r % tr == 0 and cols % tw == 0 and w.shape == (r, cols)
    c1 = 1.0 - ADAM_B1 ** ADAM_STEP
    c2 = 1.0 - ADAM_B2 ** ADAM_STEP

    def update(g, w_ref, m_ref, v_ref, g_ref, d_ref, nm_ref, nv_ref):
        new_m = ADAM_B1 * m_ref[...] + (1.0 - ADAM_B1) * g
        new_v = ADAM_B2 * v_ref[...] + (1.0 - ADAM_B2) * (g * g)
        m_hat = new_m / c1
        v_hat = new_v / c2
        g_ref[...] = g
        d_ref[...] = -ADAM_LR * (m_hat / (jnp.sqrt(v_hat) + ADAM_EPS) + ADAM_WD * w_ref[...])
        nm_ref[...] = new_m
        nv_ref[...] = new_v

    shape = jax.ShapeDtypeStruct((r, cols), F32)
    if own is None:
        def body(p_ref, *refs):
            g = p_ref[0]
            for k in range(1, n):
                g = g + p_ref[k]
            update(g, *refs)

        spec = pl.BlockSpec((tr, tw), lambda i, j: (i, j))
        return pl.pallas_call(
            body, name=name, grid=(r // tr, cols // tw),
            in_specs=[pl.BlockSpec((n, tr, tw), lambda i, j: (0, i, j)), spec, spec, spec],
            out_specs=[spec] * 4, out_shape=[shape] * 4,
            compiler_params=_params(("parallel", "parallel"), VMEM_LIMIT),
        )(parts, w, m, v)

    def body(chip_ref, p_ref, own_ref, *refs):
        g = None
        for k in range(n):
            term = jnp.where(chip_ref[0] == k, own_ref[...], p_ref[k].astype(F32))
            g = term if g is None else g + term
        update(g, *refs)

    spec = pl.BlockSpec((tr, tw), lambda i, j, chip_ref: (i, j))
    return pl.pallas_call(
        body, name=name,
        grid_spec=pltpu.PrefetchScalarGridSpec(
            num_scalar_prefetch=1, grid=(r // tr, cols // tw),
            in_specs=[pl.BlockSpec((n, tr, tw), lambda i, j, chip_ref: (0, i, j)),
                      pl.BlockSpec((None, tr, tw), lambda i, j, chip_ref: (chip_ref[0], i, j)), spec, spec, spec],
            out_specs=[spec] * 4),
        out_shape=[shape] * 4,
        compiler_params=_params(("parallel", "parallel"), VMEM_LIMIT),
    )(chip, parts, own, w, m, v)


def _rope_tables(t):
    half = HEAD_DIM // 2
    inv = 1.0 / (ROPE_THETA ** (jnp.arange(half, dtype=F32) / half))
    ang = jnp.arange(t, dtype=jnp.int32).astype(F32)[:, None] * inv[None, :]
    cos, sin = jnp.cos(ang), jnp.sin(ang)
    return jnp.concatenate([cos, cos], axis=1), jnp.concatenate([-sin, sin], axis=1)


IN_KERNEL = IN_MAIN + LR_PAD


def _to_kernel_rows(w_t):
    order = sorted(SEGMENTS.values())
    pad = jnp.zeros((LR_PAD - (IN_TOTAL - IN_MAIN), w_t.shape[1]), w_t.dtype)
    return jnp.concatenate([w_t[src:src + width] for _, src, width in order] + [w_t[IN_MAIN:IN_TOTAL], pad], axis=0)


def _pack(pieces):
    flat = []
    for p in pieces:
        p = p.reshape(-1)
        flat.append(jnp.pad(p, (0, (-p.shape[0]) % 128)))
    return jnp.concatenate(flat).reshape(-1, 128)


def _unpack(packed, shapes):
    flat = packed.reshape(-1)
    out, off = [], 0
    for s in shapes:
        size = 1
        for dim in s:
            size *= dim
        out.append(flat[off:off + size].reshape(s))
        off += size + (-size) % 128
    return out


def _local_step(xs, target, norm1_g, w_in_k, gq, gk, attn_sink, w2, ba2, gla_norm_g, w_out_full, norm2_g,
                w_up8, cw_g, cw_v, cb_g, cb_v, w_down_full, rs=None):
    t = xs.shape[0]
    tm = min(1024, t)
    tall = min(2048, t)
    cos, sin_signed = _rope_tables(t)
    sink = attn_sink.reshape(ATTN_HEADS)

    h1 = _rmsnorm_fwd(xs, norm1_g, name="norm1_fwd")
    proj = _matmul(h1, w_in_k, "nt", tm=tall, tn=IN_MAIN // 4, tk=D_MODEL, n_out=IN_MAIN, name="proj_main")
    proj_lr = _matmul(h1, w_in_k[IN_MAIN:], "nt", tm=tm, tn=LR_PAD, tk=D_MODEL, name="proj_lr")
    qa, ka, va = _attn_prep_fwd(proj, cos, sin_signed, gq, gk)
    o_attn = _attn_fwd(qa, ka, va, sink)
    g_dec = _gla_prep_fwd(proj_lr, w2, ba2)
    o_f, o_b, s_f, s_b = _gla_fwd(proj, g_dec)
    o_gla = _gla_out_fwd(o_f, o_b, proj, gla_norm_g)
    x1 = _out_proj(o_attn, o_gla, w_out_full, xs, tm=tall, tn=512)
    h2 = _rmsnorm_fwd(x1, norm2_g, name="norm2_fwd")
    u = _up_proj(h2, w_up8, tm=tm)
    act = _ffn_mid_fwd(u, cw_g, cw_v, cb_g, cb_v)
    loss_part, dy, dy_b = _down_proj_loss(act, w_down_full, x1, target, tm=tm, tn=512)

    d_act = _matmul(dy_b, w_down_full, "nt", tm=tall, tn=D_FF // 4, tk=D_MODEL, out_dtype=BF16, name="d_act")
    dw_down = _matmul(act, dy_b, "tn", tm=D_FF // 4, tn=512, tk=t, name="dw_down")
    if rs is not None:
        d_act = rs.push("w_down", [dw_down.reshape(N_DEV, D_FF // N_DEV, D_MODEL)], [(64, D_MODEL)], d_act)
    du, dcw_g, dcw_v, dcb_g, dcb_v = _ffn_mid_bwd(u, cw_g, cw_v, cb_g, cb_v, d_act)
    dw_up8 = _up_proj_dw(h2, du, tm=512, tk=t)
    if rs is not None:
        du = rs.push("w_up", [dw_up8], [(256, UP_BLOCK)], du)
    dh2 = _up_proj_dx(du, w_up8, tm=tm, tn=1024)
    dx1, dx1_b, d_norm2 = _rmsnorm_bwd(x1, norm2_g, dh2, dy, name="norm2_bwd")
    dmix = _matmul(dx1_b, w_out_full, "nt", tm=tall, tn=1024, tk=D_MODEL, name="d_mix")
    dw_out = _out_proj_dw(o_attn, o_gla, dx1_b, tn=512)
    if rs is not None:
        dmix = rs.push("w_out", [dw_out.reshape(N_DEV, D_MODEL // N_DEV, D_MODEL)], [(256, D_MODEL)], dmix)
    do_gla, d_gate, d_gla_norm = _gla_out_bwd(o_f, o_b, proj, gla_norm_g, dmix)
    (dq_f, dk_f, dv_f, dg_f, dq_b, dk_b, dv_b, dg_b) = _gla_bwd(proj, g_dec, s_f, s_b, do_gla)
    d_lr, d_w2, d_ba2 = _gla_prep_bwd(proj_lr, w2, ba2, dg_f, dg_b)
    dqa, dk_lo, dk_mid, dk_hi, dv_lo, dv_mid, dv_hi, d_sink8 = _attn_bwd(qa, ka, va, sink, dmix)
    d_qa, d_ka, d_va, d_qn, d_kn = _attn_prep_bwd(proj, cos, sin_signed, gq, gk, dqa,
                                                  (dk_lo, dk_mid, dk_hi), (dv_lo, dv_mid, dv_hi))
    d_seg = {"qa": d_qa, "gate": d_gate, "vg": (dv_f + dv_b).astype(BF16), "qg": (dq_f + dq_b).astype(BF16),
             "kg": (dk_f + dk_b).astype(BF16), "ka": d_ka, "va": d_va}
    d_proj = jnp.concatenate([d_seg[k] for k in sorted(SEGMENTS, key=lambda k: SEGMENTS[k][0])] + [d_lr], axis=1)
    dw_in_t = _in_proj_dw_lr(_in_proj_dw(d_proj, h1, tn=1024), d_lr, h1)
    if rs is not None:
        per_in, per_wa = IN_TOTAL // N_DEV, GLA_KEY_WIDTH // N_DEV
        dconv_w = jnp.concatenate([dcw_g, dcw_v], axis=1)
        d_wa2_f = d_w2[:GLA_RANK, :GLA_KEY_WIDTH]
        d_wa2_b = d_w2[GLA_RANK:2 * GLA_RANK, GLA_KEY_WIDTH:]
        small_grad = jnp.stack([
            _pack([dconv_w[:, d * UP_BLOCK:(d + 1) * UP_BLOCK], d_wa2_f[:, d * per_wa:(d + 1) * per_wa],
                   d_wa2_b[:, d * per_wa:(d + 1) * per_wa]]) for d in range(N_DEV)])
        d_proj, d_lr = rs.push("w_in", [dw_in_t.reshape(N_DEV, per_in, D_MODEL), small_grad],
                               [(per_in, 512), small_grad.shape[1:]], (d_proj, d_lr), narrow=True)
    dh1 = _matmul(d_proj, w_in_k, "nn", tm=tm, tn=512, tk=IN_KERNEL, name="dh1")
    if rs is not None:
        dh1 = rs.push(None, [], [], dh1)
    grad_x, _, d_norm1 = _rmsnorm_bwd(xs, norm1_g, dh1, dx1, name="norm1_bwd")
    return (loss_part, grad_x, dw_in_t, dw_out, dw_up8, dw_down, dcw_g, dcw_v, dcb_g, dcb_v,
            d_w2, d_ba2, d_norm1, d_norm2, d_qn, d_kn, d_sink8, d_gla_norm)


def kernel(x, norm1_g, w_in, attn_q_norm_g, attn_k_norm_g, attn_sink, gla_wa2_fwd, gla_ba_fwd, gla_wa2_bwd, gla_ba_bwd, gla_out_norm_g, w_out, norm2_g, w_up, conv_w, conv_b, w_down, loss_target, m_norm1_g, m_w_in, m_attn_q_norm_g, m_attn_k_norm_g, m_attn_sink, m_gla_wa2_fwd, m_gla_ba_fwd, m_gla_wa2_bwd, m_gla_ba_bwd, m_gla_out_norm_g, m_w_out, m_norm2_g, m_w_up, m_conv_w, m_conv_b, m_w_down, v_norm1_g, v_w_in, v_attn_q_norm_g, v_attn_k_norm_g, v_attn_sink, v_gla_wa2_fwd, v_gla_ba_fwd, v_gla_wa2_bwd, v_gla_ba_bwd, v_gla_out_norm_g, v_w_out, v_norm2_g, v_w_up, v_conv_w, v_conv_b, v_w_down):
    t = x.shape[1]
    xs = x.reshape(t, D_MODEL)
    target = loss_target.reshape(t, D_MODEL)
    core = lax.axis_index("c").astype(jnp.int32).reshape(1)

    sharded_small = [conv_w[0], gla_wa2_fwd[0], gla_wa2_bwd[0]]
    small_shapes = [s.shape for s in sharded_small]
    w_in_t, m_in_t, v_in_t = (jnp.swapaxes(a[0], 0, 1) for a in (w_in, m_w_in, v_w_in))
    g_in, g_small = _all_gather([w_in_t.astype(BF16), _pack(sharded_small)], name="gather_w_in")
    g_in, later = lax.optimization_barrier(
        (g_in, [w_out[0].astype(BF16), w_up[0].astype(BF16), w_down[0].astype(BF16)]))
    g_out, w_up8, g_down = _all_gather(later, name="gather_later_weights", collective_id=1)
    w_in_k = _to_kernel_rows(g_in.reshape(IN_TOTAL, D_MODEL))
    w_out_full = g_out.reshape(D_MODEL, D_MODEL)
    w_down_full = g_down.reshape(D_FF, D_MODEL)
    small_full = [_unpack(g_small[d], small_shapes) for d in range(N_DEV)]
    conv_w_full = jnp.concatenate([s[0] for s in small_full], axis=1)
    wa2_f = jnp.concatenate([s[1] for s in small_full], axis=1)
    wa2_b = jnp.concatenate([s[2] for s in small_full], axis=1)
    cw_g, cw_v = conv_w_full[:, :D_FF], conv_w_full[:, D_FF:]
    cb_g, cb_v = conv_b[:, :D_FF], conv_b[:, D_FF:]
    w2 = jnp.zeros((LR_PAD, 2 * GLA_KEY_WIDTH), F32)
    w2 = w2.at[:GLA_RANK, :GLA_KEY_WIDTH].set(wa2_f).at[GLA_RANK:2 * GLA_RANK, GLA_KEY_WIDTH:].set(wa2_b)
    ba2 = jnp.concatenate([gla_ba_fwd, gla_ba_bwd], axis=1)
    rs = _ReduceScatter(core)
    (loss_part, grad_x, _, _, _, _, _, _, dcb_g, dcb_v, _, d_ba2,
     d_norm1, d_norm2, d_qn, d_kn, d_sink8, d_gla_norm) = _local_step(
        xs, target, norm1_g, w_in_k, attn_q_norm_g, attn_k_norm_g, attn_sink, w2, ba2, gla_out_norm_g,
        w_out_full, norm2_g, w_up8, cw_g, cw_v, cb_g, cb_v, w_down_full, rs=rs)

    (part_down,), (part_up,), (part_out,) = rs.result("w_down")[0], rs.result("w_up")[0], rs.result("w_out")[0]
    (part_in, part_small), (own_in, own_small) = rs.result("w_in")
    chip = (2 * lax.axis_index("x") + lax.axis_index("y")).astype(jnp.int32).reshape(1)
    m_small = _pack([m_conv_w[0], m_gla_wa2_fwd[0], m_gla_wa2_bwd[0]])
    v_small = _pack([v_conv_w[0], v_gla_wa2_fwd[0], v_gla_wa2_bwd[0]])
    upd_in = _adamw(part_in, w_in_t, m_in_t, v_in_t, tile=(IN_TOTAL // N_DEV, 512), name="adamw_w_in",
                    own=own_in, chip=chip)
    upd_in = [jnp.swapaxes(u, 0, 1) for u in upd_in]
    upd_out = _adamw(part_out, w_out[0], m_w_out[0], v_w_out[0], tile=(256, D_MODEL), name="adamw_w_out")
    upd_up = _adamw(part_up, w_up[0], m_w_up[0], v_w_up[0], tile=(256, UP_BLOCK), name="adamw_w_up")
    upd_down = _adamw(part_down, w_down[0], m_w_down[0], v_w_down[0], tile=(64, D_MODEL), name="adamw_w_down")
    upd_small = _adamw(part_small, _pack(sharded_small), m_small, v_small, tile=part_small.shape[1:],
                       name="adamw_small", own=own_small, chip=chip)
    upd_small = [_unpack(u, small_shapes) for u in upd_small]

    rep_names = ["norm1_g", "attn_q_norm_g", "attn_k_norm_g", "attn_sink", "gla_ba_fwd", "gla_ba_bwd",
                 "gla_out_norm_g", "norm2_g", "conv_b"]
    rep_w = [norm1_g, attn_q_norm_g, attn_k_norm_g, attn_sink, gla_ba_fwd, gla_ba_bwd, gla_out_norm_g, norm2_g, conv_b]
    rep_m = [m_norm1_g, m_attn_q_norm_g, m_attn_k_norm_g, m_attn_sink, m_gla_ba_fwd, m_gla_ba_bwd,
             m_gla_out_norm_g, m_norm2_g, m_conv_b]
    rep_v = [v_norm1_g, v_attn_q_norm_g, v_attn_k_norm_g, v_attn_sink, v_gla_ba_fwd, v_gla_ba_bwd,
             v_gla_out_norm_g, v_norm2_g, v_conv_b]
    d_sink = d_sink8[:, :GQA_GROUP, 0].reshape(1, ATTN_HEADS)
    rep_g = [d_norm1, d_qn, d_kn, d_sink, d_ba2[:, :GLA_KEY_WIDTH], d_ba2[:, GLA_KEY_WIDTH:], d_gla_norm, d_norm2,
             jnp.concatenate([dcb_g, dcb_v], axis=1)]
    ride = jnp.ones((1, 128), F32)
    rep_shapes = [w.shape for w in rep_w] + [loss_part.shape]
    (rep_terms,) = _all_gather([_pack(rep_g + [loss_part])], name="gather_small_grads", collective_id=7)
    upd_rep = _adamw(rep_terms, _pack(rep_w + [ride]), _pack(rep_m + [ride]), _pack(rep_v + [ride]),
                     tile=rep_terms.shape[1:], name="adamw_replicated")
    upd_rep = [_unpack(u, rep_shapes) for u in upd_rep]
    loss = upd_rep[0][-1][0, 0]

    order = ["norm1_g", "w_in", "attn_q_norm_g", "attn_k_norm_g", "attn_sink", "gla_wa2_fwd", "gla_ba_fwd",
             "gla_wa2_bwd", "gla_ba_bwd", "gla_out_norm_g", "w_out", "norm2_g", "w_up", "conv_w", "conv_b", "w_down"]
    outs = [loss, grad_x.reshape(1, t, D_MODEL)]
    for kind in range(4):
        by_name = {n: upd_rep[kind][i] for i, n in enumerate(rep_names)}
        by_name["w_in"] = upd_in[kind][None]
        by_name["w_out"] = upd_out[kind][None]
        by_name["w_up"] = upd_up[kind][None]
        by_name["w_down"] = upd_down[kind][None]
        by_name["conv_w"] = upd_small[kind][0][None]
        by_name["gla_wa2_fwd"] = upd_small[kind][1][None]
        by_name["gla_wa2_bwd"] = upd_small[kind][2][None]
        outs += [by_name[n] for n in order]
    return tuple(outs)
```

```python
import functools

import jax
import jax.numpy as jnp
from jax import lax
from jax.experimental import pallas as pl
from jax.experimental.pallas import tpu as pltpu
from jax.experimental.pallas import tpu_sc as plsc

F32 = jnp.float32
BF16 = jnp.bfloat16

D_MODEL = 2048
HEAD_DIM = 128
ATTN_WIDTH = 1024
ATTN_HEADS = 8
KV_HEADS = 2
GQA_GROUP = 4
KV_WIDTH = KV_HEADS * HEAD_DIM
ATTN_BLOCK = 128
WINDOW = 128
ROPE_THETA = 10000.0
GLA_HEADS = 4
GLA_DK = 128
GLA_DV = 256
GLA_KEY_WIDTH = 512
GLA_WIDTH = 1024
GLA_RANK = 16
GLA_GATE_NORMALIZER = 16.0
GLA_CHUNK = 64
D_FF = 5632
NORM_EPS = 1e-6
IN_TOTAL = 4640
IN_MAIN = 4608
LR_PAD = 128
N_DEV = 8
N_CHIP = 4

ADAM_LR = 0.001
ADAM_B1 = 0.9
ADAM_B2 = 0.999
ADAM_EPS = 1e-08
ADAM_WD = 0.01
ADAM_STEP = 10

SEGMENTS = {
    "qa": (0, 0, 1024),
    "gate": (1024, 3584, 1024),
    "vg": (2048, 2560, 1024),
    "qg": (3072, 1536, 512),
    "kg": (3584, 2048, 512),
    "ka": (4096, 1024, 256),
    "va": (4352, 1280, 256),
}

VMEM_LIMIT = 56 * 1024 * 1024
MESH = pl.DeviceIdType.MESH


def _params(semantics=None, vmem=None):
    return pltpu.CompilerParams(dimension_semantics=semantics, vmem_limit_bytes=vmem)


_DIMS = {
    "nn": (((1,), (0,)), ((), ())),
    "nt": (((1,), (1,)), ((), ())),
    "tn": (((0,), (0,)), ((), ())),
}


def _mxu(a, b, mode):
    return lax.dot_general(a.astype(BF16), b.astype(BF16), _DIMS[mode], preferred_element_type=F32)


@functools.partial(jax.custom_vjp, nondiff_argnums=(2,))
def bdot(a, b, mode):
    return _mxu(a, b, mode)


def _bdot_fwd(a, b, mode):
    return _mxu(a, b, mode), (a, b)


def _bdot_bwd(mode, res, g):
    a, b = res
    if mode == "nn":
        return _mxu(g, b, "nt"), _mxu(a, g, "tn")
    if mode == "nt":
        return _mxu(g, b, "nn"), _mxu(g, a, "tn")
    return _mxu(b, g, "nt"), _mxu(a, g, "nn")


bdot.defvjp(_bdot_fwd, _bdot_bwd)


def _rms(x, g):
    return x * lax.rsqrt(jnp.mean(x * x, axis=-1, keepdims=True) + NORM_EPS) * g


def _rope(x, cos, sin_signed):
    return x * cos + pltpu.roll(x, HEAD_DIM // 2, 1) * sin_signed


def _rope_transposed(d, cos, sin_signed):
    return d * cos + pltpu.roll(d * sin_signed, HEAD_DIM // 2, 1)


def _silu(x):
    return x * jax.nn.sigmoid(x)


def _log_sigmoid(z):
    return -(jnp.maximum(-z, 0.0) + jnp.log(1.0 + jnp.exp(-jnp.abs(z))))


def _matmul_call(args, in_specs, o_spec, out_shape, grid, mode, nk, acc_shape, *, name, has_res=False,
                 prefetch=None, load_b=lambda ref: ref[...]):
    dims = _DIMS[mode]
    out_dtype = out_shape.dtype
    n_pre = 0 if prefetch is None else 1

    def body(*refs):
        refs = refs[n_pre:]
        if has_res:
            a_ref, b_ref, r_ref, o_ref = refs[:4]
            rest = refs[4:]
        else:
            a_ref, b_ref, o_ref = refs[:3]
            r_ref = None
            rest = refs[3:]
        part = lax.dot_general(a_ref[...], load_b(b_ref), dims, preferred_element_type=F32)

        def finish(acc):
            if r_ref is not None:
                acc = acc + r_ref[...]
            o_ref[...] = acc.astype(out_dtype)

        if nk == 1:
            finish(part)
        else:
            acc_ref = rest[0]
            kk = pl.program_id(2)

            @pl.when(kk == 0)
            def _():
                acc_ref[...] = part

            @pl.when(kk > 0)
            def _():
                acc_ref[...] += part

            @pl.when(kk == nk - 1)
            def _():
                finish(acc_ref[...])

    scratch = [pltpu.VMEM(acc_shape, F32)] if nk > 1 else []
    params = _params(("parallel", "parallel", "arbitrary"), VMEM_LIMIT)
    if prefetch is None:
        return pl.pallas_call(body, name=name, grid=grid, in_specs=in_specs, out_specs=o_spec, out_shape=out_shape,
                              scratch_shapes=scratch, compiler_params=params)(*args)
    return pl.pallas_call(
        body, name=name,
        grid_spec=pltpu.PrefetchScalarGridSpec(num_scalar_prefetch=1, grid=grid, in_specs=in_specs,
                                               out_specs=o_spec, scratch_shapes=scratch),
        out_shape=out_shape, compiler_params=params)(prefetch, *args)


def _matmul(a, b, mode, *, tm, tn, tk, out_dtype=F32, res=None, name, n_out=None):
    if mode == "nn":
        (m, k), (k2, n) = a.shape, b.shape
    elif mode == "nt":
        (m, k), (n, k2) = a.shape, b.shape
    else:
        (k, m), (k2, n) = a.shape, b.shape
    n = n if n_out is None else n_out
    assert k == k2 and m % tm == 0 and n % tn == 0 and k % tk == 0, (name, a.shape, b.shape, tm, tn, tk)
    if mode == "tn":
        a_spec = pl.BlockSpec((tk, tm), lambda i, j, kk: (kk, i))
    else:
        a_spec = pl.BlockSpec((tm, tk), lambda i, j, kk: (i, kk))
    if mode == "nt":
        b_spec = pl.BlockSpec((tn, tk), lambda i, j, kk: (j, kk))
    else:
        b_spec = pl.BlockSpec((tk, tn), lambda i, j, kk: (kk, j))
    o_spec = pl.BlockSpec((tm, tn), lambda i, j, kk: (i, j))
    in_specs, args = [a_spec, b_spec], [a, b]
    if res is not None:
        in_specs.append(o_spec)
        args.append(res)
    return _matmul_call(args, in_specs, o_spec, jax.ShapeDtypeStruct((m, n), out_dtype),
                        (m // tm, n // tn, k // tk), mode, k // tk, (tm, tn), name=name, has_res=res is not None)


def _out_proj(o_attn, o_gla, w_out, x, *, tm, tn):
    t, ka = o_attn.shape
    kg = o_gla.shape[1]

    def body(a_ref, g_ref, w_ref, x_ref, o_ref):
        acc = lax.dot_general(a_ref[...], w_ref[:ka], _DIMS["nn"], preferred_element_type=F32)
        acc = acc + lax.dot_general(g_ref[...], w_ref[ka:], _DIMS["nn"], preferred_element_type=F32)
        o_ref[...] = acc + x_ref[...]

    tile = pl.BlockSpec((tm, tn), lambda i, j: (i, j))
    return pl.pallas_call(
        body, name="out_proj", grid=(t // tm, D_MODEL // tn),
        in_specs=[pl.BlockSpec((tm, ka), lambda i, j: (i, 0)), pl.BlockSpec((tm, kg), lambda i, j: (i, 0)),
                  pl.BlockSpec((ka + kg, tn), lambda i, j: (0, j)), tile],
        out_specs=tile, out_shape=jax.ShapeDtypeStruct((t, D_MODEL), F32),
        compiler_params=_params(("parallel", "parallel"), VMEM_LIMIT),
    )(o_attn, o_gla, w_out, x)


def _out_proj_dw(o_attn, o_gla, dx1, *, tn):
    t, ka = o_attn.shape
    assert o_gla.shape == (t, ka)

    def body(a_ref, g_ref, d_ref, o_ref):
        @pl.when(pl.program_id(0) == 0)
        def _():
            o_ref[...] = lax.dot_general(a_ref[...], d_ref[...], _DIMS["tn"], preferred_element_type=F32)

        @pl.when(pl.program_id(0) == 1)
        def _():
            o_ref[...] = lax.dot_general(g_ref[...], d_ref[...], _DIMS["tn"], preferred_element_type=F32)

    whole = pl.BlockSpec((t, ka), lambda i, j: (0, 0))
    return pl.pallas_call(
        body, name="dw_out", grid=(2, D_MODEL // tn),
        in_specs=[whole, whole, pl.BlockSpec((t, tn), lambda i, j: (0, j))],
        out_specs=pl.BlockSpec((ka, tn), lambda i, j: (i, j)),
        out_shape=jax.ShapeDtypeStruct((2 * ka, D_MODEL), F32),
        compiler_params=_params(("parallel", "parallel"), VMEM_LIMIT),
    )(o_attn, o_gla, dx1)


UP_BLOCK = 2 * D_FF // N_DEV


def _up_proj(h2, w_up8, *, tm):
    t = h2.shape[0]
    return _matmul_call(
        [h2, w_up8],
        [pl.BlockSpec((tm, D_MODEL), lambda i, j, kk: (i, 0)),
         pl.BlockSpec((None, D_MODEL, UP_BLOCK), lambda i, j, kk: (j, 0, 0))],
        pl.BlockSpec((None, tm, UP_BLOCK), lambda i, j, kk: (j // N_CHIP, i, j % N_CHIP)),
        jax.ShapeDtypeStruct((2, t, D_FF), F32), (t // tm, N_DEV, 1), "nn", 1, None, name="up_proj")


def _up_proj_dx(du, w_up8, *, tm, tn):
    t = du.shape[1]
    pair = 2
    return _matmul_call(
        [du, w_up8],
        [pl.BlockSpec((None, tm, pair * UP_BLOCK), lambda i, j, kk: (kk // 2, i, kk % 2)),
         pl.BlockSpec((pair, tn, UP_BLOCK), lambda i, j, kk: (kk, j, 0))],
        pl.BlockSpec((tm, tn), lambda i, j, kk: (i, j)),
        jax.ShapeDtypeStruct((t, D_MODEL), F32), (t // tm, D_MODEL // tn, N_DEV // pair), "nt", N_DEV // pair,
        (tm, tn), name="up_proj_dx", load_b=lambda ref: jnp.concatenate([ref[0], ref[1]], axis=1))


def _up_proj_dw(h2, du, *, tm, tk):
    t = h2.shape[0]
    return _matmul_call(
        [h2, du],
        [pl.BlockSpec((tk, tm), lambda j, i, kk: (kk, i)),
         pl.BlockSpec((None, tk, UP_BLOCK), lambda j, i, kk: (j // N_CHIP, kk, j % N_CHIP))],
        pl.BlockSpec((None, tm, UP_BLOCK), lambda j, i, kk: (j, i, 0)),
        jax.ShapeDtypeStruct((N_DEV, D_MODEL, UP_BLOCK), F32), (N_DEV, D_MODEL // tm, t // tk), "tn", t // tk,
        (tm, UP_BLOCK), name="up_proj_dw")


IN_TILE = 512


def _in_proj_dw(d_proj, h1, *, tn):
    t = h1.shape[0]
    table = []
    for tile in range(IN_MAIN // IN_TILE):
        dst, src, _ = max(s for s in SEGMENTS.values() if s[0] <= tile * IN_TILE)
        assert (src + tile * IN_TILE - dst) % IN_TILE == 0
        table.append((src + tile * IN_TILE - dst) // IN_TILE)
    assert sorted(table) == list(range(IN_MAIN // IN_TILE))
    return _matmul_call(
        [d_proj, h1],
        [pl.BlockSpec((t, IN_TILE), lambda j, i, kk, tab: (0, i)),
         pl.BlockSpec((t, tn), lambda j, i, kk, tab: (0, j))],
        pl.BlockSpec((IN_TILE, tn), lambda j, i, kk, tab: (tab[i], j)),
        jax.ShapeDtypeStruct((IN_TOTAL, D_MODEL), F32), (D_MODEL // tn, IN_MAIN // IN_TILE, 1), "tn", 1, None,
        name="in_proj_dw", prefetch=jnp.asarray(table, jnp.int32))


def _in_proj_dw_lr(dw_t, d_lr, h1):
    t = h1.shape[0]
    n_lr = IN_TOTAL - IN_MAIN
    tn = 512

    def body(dw_ref, dlr_ref, h1_ref, out_ref):
        full = lax.dot_general(dlr_ref[...], h1_ref[...], _DIMS["tn"], preferred_element_type=F32)
        out_ref[...] = full[:n_lr]

    return pl.pallas_call(
        body, name="in_proj_dw_lr", grid=(D_MODEL // tn,),
        in_specs=[pl.BlockSpec(memory_space=pl.ANY),
                  pl.BlockSpec((t, LR_PAD), lambda j: (0, 0)),
                  pl.BlockSpec((t, tn), lambda j: (0, j))],
        out_specs=pl.BlockSpec((n_lr, tn), lambda j: (IN_MAIN // n_lr, j)),
        out_shape=jax.ShapeDtypeStruct(dw_t.shape, F32),
        input_output_aliases={0: 0},
        compiler_params=_params(("parallel",), VMEM_LIMIT),
    )(dw_t, d_lr, h1)


def _rmsnorm_fwd(x, g, *, name, tr=512):
    t, d = x.shape

    def body(x_ref, g_ref, h_ref):
        h_ref[...] = _rms(x_ref[...], g_ref[...]).astype(BF16)

    return pl.pallas_call(
        body, name=name, grid=(t // tr,),
        in_specs=[pl.BlockSpec((tr, d), lambda i: (i, 0)), pl.BlockSpec((1, d), lambda i: (0, 0))],
        out_specs=pl.BlockSpec((tr, d), lambda i: (i, 0)),
        out_shape=jax.ShapeDtypeStruct((t, d), BF16),
        compiler_params=_params(("parallel",), VMEM_LIMIT),
    )(x, g)


def _rmsnorm_bwd(x, g, dh, dres, *, name, tr=256):
    t, d = x.shape

    def body(x_ref, g_ref, dh_ref, dres_ref, dx_ref, dxb_ref, dg_ref):
        _, vjp = jax.vjp(_rms, x_ref[...], g_ref[...])
        dx, dg = vjp(dh_ref[...])
        dx = dx + dres_ref[...]
        dx_ref[...] = dx
        dxb_ref[...] = dx.astype(BF16)

        @pl.when(pl.program_id(0) == 0)
        def _():
            dg_ref[...] = jnp.zeros_like(dg_ref)

        dg_ref[...] += dg

    row = pl.BlockSpec((tr, d), lambda i: (i, 0))
    vec = pl.BlockSpec((1, d), lambda i: (0, 0))
    return pl.pallas_call(
        body, name=name, grid=(t // tr,),
        in_specs=[row, vec, row, row],
        out_specs=[row, row, vec],
        out_shape=[jax.ShapeDtypeStruct((t, d), F32), jax.ShapeDtypeStruct((t, d), BF16),
                   jax.ShapeDtypeStruct((1, d), F32)],
        compiler_params=_params(("arbitrary",), VMEM_LIMIT),
    )(x, g, dh, dres)


def _seg_block(name, width):
    off = SEGMENTS[name][0]
    assert off % width == 0
    return off // width


def _attn_prep_fwd(proj, cos, sin_signed, gq, gk, *, tr=256):
    t = proj.shape[0]

    def body(q_ref, k_ref, v_ref, cos_ref, sin_ref, gq_ref, gk_ref, qo_ref, ko_ref, vo_ref):
        cos_t, sin_t = cos_ref[...], sin_ref[...]
        for h in range(ATTN_HEADS):
            cols = slice(h * HEAD_DIM, (h + 1) * HEAD_DIM)
            qo_ref[:, cols] = _rope(_rms(q_ref[:, cols], gq_ref[...]), cos_t, sin_t).astype(BF16)
        for h in range(KV_HEADS):
            cols = slice(h * HEAD_DIM, (h + 1) * HEAD_DIM)
            ko_ref[:, cols] = _rope(_rms(k_ref[:, cols], gk_ref[...]), cos_t, sin_t).astype(BF16)
        vo_ref[...] = v_ref[...].astype(BF16)

    qb, kb, vb = _seg_block("qa", ATTN_WIDTH), _seg_block("ka", KV_WIDTH), _seg_block("va", KV_WIDTH)
    tab = pl.BlockSpec((tr, HEAD_DIM), lambda i: (i, 0))
    vec = pl.BlockSpec((1, HEAD_DIM), lambda i: (0, 0))
    return pl.pallas_call(
        body, name="attn_prep_fwd", grid=(t // tr,),
        in_specs=[pl.BlockSpec((tr, ATTN_WIDTH), lambda i: (i, qb)),
                  pl.BlockSpec((tr, KV_WIDTH), lambda i: (i, kb)),
                  pl.BlockSpec((tr, KV_WIDTH), lambda i: (i, vb)),
                  tab, tab, vec, vec],
        out_specs=[pl.BlockSpec((tr, ATTN_WIDTH), lambda i: (i, 0)),
                   pl.BlockSpec((tr, KV_WIDTH), lambda i: (i, 0)),
                   pl.BlockSpec((tr, KV_WIDTH), lambda i: (i, 0))],
        out_shape=[jax.ShapeDtypeStruct((t, ATTN_WIDTH), BF16),
                   jax.ShapeDtypeStruct((t, KV_WIDTH), BF16),
                   jax.ShapeDtypeStruct((t, KV_WIDTH), BF16)],
        compiler_params=_params(("parallel",), VMEM_LIMIT),
    )(proj, proj, proj, cos, sin_signed, gq, gk)


def _attn_heads(q, kcat, vcat, sink_col, valid):
    s = bdot(q, kcat, "nt") * (HEAD_DIM ** -0.5)
    s = jnp.where(valid, s, -jnp.inf)
    m = lax.stop_gradient(jnp.maximum(jnp.max(s, axis=-1, keepdims=True), sink_col))
    p = jnp.exp(s - m)
    p = p / (jnp.sum(p, axis=-1, keepdims=True) + jnp.exp(sink_col - m))
    return bdot(p, vcat, "nn")


def _attn_valid(n, t):
    shape = (GQA_GROUP * ATTN_BLOCK, 3 * ATTN_BLOCK)
    qi = lax.broadcasted_iota(jnp.int32, shape, 0) % ATTN_BLOCK
    sj = lax.broadcasted_iota(jnp.int32, shape, 1)
    kpos = n * ATTN_BLOCK - ATTN_BLOCK + sj
    return (jnp.abs(sj - ATTN_BLOCK - qi) <= WINDOW) & (kpos >= 0) & (kpos < t)


def _head_rows(g):
    return slice(g * ATTN_BLOCK, (g + 1) * ATTN_BLOCK)


def _head_cols(g):
    return slice(g * HEAD_DIM, (g + 1) * HEAD_DIM)


def _stack_heads(ref):
    return jnp.concatenate([ref[:, _head_cols(g)] for g in range(GQA_GROUP)], axis=0).astype(F32)


def _sink_column(sink_ref, h):
    return jnp.concatenate([jnp.full((ATTN_BLOCK, 1), sink_ref[h * GQA_GROUP + g], F32)
                            for g in range(GQA_GROUP)], axis=0)


def _attn_specs(nb):
    q_spec = pl.BlockSpec((ATTN_BLOCK, GQA_GROUP * HEAD_DIM), lambda h, n: (n, h))
    kv_specs = [
        pl.BlockSpec((ATTN_BLOCK, HEAD_DIM), lambda h, n: (jnp.maximum(n - 1, 0), h)),
        pl.BlockSpec((ATTN_BLOCK, HEAD_DIM), lambda h, n: (n, h)),
        pl.BlockSpec((ATTN_BLOCK, HEAD_DIM), lambda h, n: (jnp.minimum(n + 1, nb - 1), h)),
    ]
    return q_spec, kv_specs


def _attn_fwd(q, k, v, sink):
    t = q.shape[0]
    nb = t // ATTN_BLOCK

    def body(sink_ref, q_ref, kp_ref, kc_ref, kn_ref, vp_ref, vc_ref, vn_ref, o_ref):
        h, n = pl.program_id(0), pl.program_id(1)
        valid = _attn_valid(n, t)
        kcat = jnp.concatenate([kp_ref[...], kc_ref[...], kn_ref[...]], axis=0).astype(F32)
        vcat = jnp.concatenate([vp_ref[...], vc_ref[...], vn_ref[...]], axis=0).astype(F32)
        o = _attn_heads(_stack_heads(q_ref), kcat, vcat, _sink_column(sink_ref, h), valid).astype(BF16)
        for g in range(GQA_GROUP):
            o_ref[:, _head_cols(g)] = o[_head_rows(g)]

    q_spec, kv_specs = _attn_specs(nb)
    return pl.pallas_call(
        body, name="attn_fwd", grid=(KV_HEADS, nb),
        in_specs=[pl.BlockSpec(memory_space=pltpu.SMEM), q_spec] + kv_specs + kv_specs,
        out_specs=q_spec,
        out_shape=jax.ShapeDtypeStruct((t, ATTN_WIDTH), BF16),
        compiler_params=_params(("parallel", "parallel"), VMEM_LIMIT),
    )(sink, q, k, k, k, v, v, v)


def _attn_bwd(q, k, v, sink, dmix):
    t = q.shape[0]
    nb = t // ATTN_BLOCK

    def body(sink_ref, q_ref, kp_ref, kc_ref, kn_ref, vp_ref, vc_ref, vn_ref, do_ref,
             dq_ref, dk_lo, dk_mid, dk_hi, dv_lo, dv_mid, dv_hi, dsink_ref):
        h, n = pl.program_id(0), pl.program_id(1)
        valid = _attn_valid(n, t)
        kcat = jnp.concatenate([kp_ref[...], kc_ref[...], kn_ref[...]], axis=0).astype(F32)
        vcat = jnp.concatenate([vp_ref[...], vc_ref[...], vn_ref[...]], axis=0).astype(F32)
        _, vjp = jax.vjp(functools.partial(_attn_heads, valid=valid),
                         _stack_heads(q_ref), kcat, vcat, _sink_column(sink_ref, h))
        dq, dk, dv, dsink_col = vjp(_stack_heads(do_ref))
        row = lax.broadcasted_iota(jnp.int32, (8, HEAD_DIM), 0)
        dsink = jnp.zeros((8, HEAD_DIM), F32)
        for g in range(GQA_GROUP):
            dq_ref[:, _head_cols(g)] = dq[_head_rows(g)]
            dsink = dsink + jnp.where(row == g, jnp.sum(dsink_col[_head_rows(g)]), 0.0)
        for i, (dk_ref, dv_ref) in enumerate(((dk_lo, dv_lo), (dk_mid, dv_mid), (dk_hi, dv_hi))):
            rows = slice(i * ATTN_BLOCK, (i + 1) * ATTN_BLOCK)
            dk_ref[...] = dk[rows]
            dv_ref[...] = dv[rows]

        @pl.when(n == 0)
        def _():
            dsink_ref[...] = jnp.zeros_like(dsink_ref)

        dsink_ref[...] += dsink

    q_spec, kv_specs = _attn_specs(nb)
    kv_out = pl.BlockSpec((ATTN_BLOCK, HEAD_DIM), lambda h, n: (n, h))
    kv_shape = jax.ShapeDtypeStruct((t, KV_WIDTH), F32)
    return pl.pallas_call(
        body, name="attn_bwd", grid=(KV_HEADS, nb),
        in_specs=[pl.BlockSpec(memory_space=pltpu.SMEM), q_spec] + kv_specs + kv_specs + [q_spec],
        out_specs=[q_spec] + [kv_out] * 6 + [pl.BlockSpec((None, 8, HEAD_DIM), lambda h, n: (h, 0, 0))],
        out_shape=[jax.ShapeDtypeStruct((t, ATTN_WIDTH), F32)] + [kv_shape] * 6
                  + [jax.ShapeDtypeStruct((KV_HEADS, 8, HEAD_DIM), F32)],
        compiler_params=_params(("parallel", "arbitrary"), VMEM_LIMIT),
    )(sink, q, k, k, k, v, v, v, dmix)


def _attn_prep_bwd(proj, cos, sin_signed, gq, gk, dq, dks, dvs):
    t = proj.shape[0]
    tr = ATTN_BLOCK
    nb = t // tr

    def body(q_ref, k_ref, cos_ref, sin_ref, gq_ref, gk_ref, dq_ref,
             dk_lo, dk_mid, dk_hi, dv_lo, dv_mid, dv_hi,
             dqo_ref, dko_ref, dvo_ref, dgq_ref, dgk_ref):
        n = pl.program_id(0)
        cos_t, sin_t = cos_ref[...], sin_ref[...]
        has_next = (n < nb - 1).astype(F32)
        has_prev = (n > 0).astype(F32)
        dk = dk_lo[...] * has_next + dk_mid[...] + dk_hi[...] * has_prev
        dv = dv_lo[...] * has_next + dv_mid[...] + dv_hi[...] * has_prev
        dvo_ref[...] = dv.astype(BF16)
        dgq = jnp.zeros((1, HEAD_DIM), F32)
        dgk = jnp.zeros((1, HEAD_DIM), F32)
        for h in range(ATTN_HEADS):
            cols = slice(h * HEAD_DIM, (h + 1) * HEAD_DIM)
            _, vjp = jax.vjp(_rms, q_ref[:, cols], gq_ref[...])
            dx, dg = vjp(_rope_transposed(dq_ref[:, cols], cos_t, sin_t))
            dqo_ref[:, cols] = dx.astype(BF16)
            dgq = dgq + dg
        for h in range(KV_HEADS):
            cols = slice(h * HEAD_DIM, (h + 1) * HEAD_DIM)
            _, vjp = jax.vjp(_rms, k_ref[:, cols], gk_ref[...])
            dx, dg = vjp(_rope_transposed(dk[:, cols], cos_t, sin_t))
            dko_ref[:, cols] = dx.astype(BF16)
            dgk = dgk + dg

        @pl.when(n == 0)
        def _():
            dgq_ref[...] = jnp.zeros_like(dgq_ref)
            dgk_ref[...] = jnp.zeros_like(dgk_ref)

        dgq_ref[...] += dgq
        dgk_ref[...] += dgk

    qb, kb = _seg_block("qa", ATTN_WIDTH), _seg_block("ka", KV_WIDTH)
    tab = pl.BlockSpec((tr, HEAD_DIM), lambda i: (i, 0))
    vec = pl.BlockSpec((1, HEAD_DIM), lambda i: (0, 0))
    kv = [pl.BlockSpec((tr, KV_WIDTH), lambda i: (jnp.minimum(i + 1, nb - 1), 0)),
          pl.BlockSpec((tr, KV_WIDTH), lambda i: (i, 0)),
          pl.BlockSpec((tr, KV_WIDTH), lambda i: (jnp.maximum(i - 1, 0), 0))]
    wide = pl.BlockSpec((tr, ATTN_WIDTH), lambda i: (i, 0))
    narrow = pl.BlockSpec((tr, KV_WIDTH), lambda i: (i, 0))
    return pl.pallas_call(
        body, name="attn_prep_bwd", grid=(nb,),
        in_specs=[pl.BlockSpec((tr, ATTN_WIDTH), lambda i: (i, qb)),
                  pl.BlockSpec((tr, KV_WIDTH), lambda i: (i, kb)),
                  tab, tab, vec, vec, wide] + kv + kv,
        out_specs=[wide, narrow, narrow, vec, vec],
        out_shape=[jax.ShapeDtypeStruct((t, ATTN_WIDTH), BF16),
                   jax.ShapeDtypeStruct((t, KV_WIDTH), BF16),
                   jax.ShapeDtypeStruct((t, KV_WIDTH), BF16),
                   jax.ShapeDtypeStruct((1, HEAD_DIM), F32),
                   jax.ShapeDtypeStruct((1, HEAD_DIM), F32)],
        compiler_params=_params(("arbitrary",), VMEM_LIMIT),
    )(proj, proj, cos, sin_signed, gq, gk, dq, *dks, *dvs)


def _decay_fn(lr, w2, ba):
    return _log_sigmoid(bdot(lr, w2, "nn") + ba) / GLA_GATE_NORMALIZER


def _gla_prep_fwd(proj_lr, w2, ba2, *, tr=512):
    t = proj_lr.shape[0]
    width = 2 * GLA_KEY_WIDTH

    def body(lr_ref, w2_ref, ba_ref, g_ref):
        g_ref[...] = _decay_fn(lr_ref[...], w2_ref[...], ba_ref[...])

    return pl.pallas_call(
        body, name="gla_prep_fwd", grid=(t // tr,),
        in_specs=[pl.BlockSpec((tr, LR_PAD), lambda i: (i, 0)),
                  pl.BlockSpec((LR_PAD, width), lambda i: (0, 0)),
                  pl.BlockSpec((1, width), lambda i: (0, 0))],
        out_specs=pl.BlockSpec((tr, width), lambda i: (i, 0)),
        out_shape=jax.ShapeDtypeStruct((t, width), F32),
        compiler_params=_params(("parallel",), VMEM_LIMIT),
    )(proj_lr, w2, ba2)


def _gla_prep_bwd(proj_lr, w2, ba2, dg_f, dg_b, *, tr=512):
    t = proj_lr.shape[0]
    width = 2 * GLA_KEY_WIDTH

    def body(lr_ref, w2_ref, ba_ref, dgf_ref, dgb_ref, dlr_ref, dw2_ref, dba_ref):
        _, vjp = jax.vjp(_decay_fn, lr_ref[...], w2_ref[...], ba_ref[...])
        dlr, dw2, dba = vjp(jnp.concatenate([dgf_ref[...], dgb_ref[...]], axis=1))
        dlr_ref[...] = dlr.astype(BF16)

        @pl.when(pl.program_id(0) == 0)
        def _():
            dw2_ref[...] = jnp.zeros_like(dw2_ref)
            dba_ref[...] = jnp.zeros_like(dba_ref)

        dw2_ref[...] += dw2
        dba_ref[...] += dba

    half = pl.BlockSpec((tr, GLA_KEY_WIDTH), lambda i: (i, 0))
    return pl.pallas_call(
        body, name="gla_prep_bwd", grid=(t // tr,),
        in_specs=[pl.BlockSpec((tr, LR_PAD), lambda i: (i, 0)),
                  pl.BlockSpec((LR_PAD, width), lambda i: (0, 0)),
                  pl.BlockSpec((1, width), lambda i: (0, 0)), half, half],
        out_specs=[pl.BlockSpec((tr, LR_PAD), lambda i: (i, 0)),
                   pl.BlockSpec((LR_PAD, width), lambda i: (0, 0)),
                   pl.BlockSpec((1, width), lambda i: (0, 0))],
        out_shape=[jax.ShapeDtypeStruct((t, LR_PAD), BF16),
                   jax.ShapeDtypeStruct((LR_PAD, width), F32),
                   jax.ShapeDtypeStruct((1, width), F32)],
        compiler_params=_params(("arbitrary",), VMEM_LIMIT),
    )(proj_lr, w2, ba2, dg_f, dg_b)


def _gla_k(h):
    return slice(h * GLA_DK, (h + 1) * GLA_DK)


def _gla_v(h):
    return slice(h * GLA_DV, (h + 1) * GLA_DV)


def _running_sum(x, downward):
    n = x.shape[0]
    row = lax.broadcasted_iota(jnp.int32, x.shape, 0)
    step = 1
    while step < n:
        if downward:
            x = x + jnp.where(row >= step, pltpu.roll(x, step, 0), 0.0)
        else:
            x = x + jnp.where(row < n - step, pltpu.roll(x, n - step, 0), 0.0)
        step *= 2
    return x


@functools.partial(jax.custom_vjp, nondiff_argnums=(1,))
def _cumsum_rows(x, downward):
    return _running_sum(x, downward)


def _cumsum_rows_fwd(x, downward):
    return _running_sum(x, downward), None


def _cumsum_rows_bwd(downward, _, ct):
    return (_running_sum(ct, not downward),)


_cumsum_rows.defvjp(_cumsum_rows_fwd, _cumsum_rows_bwd)


def _gla_chunk(q, k, v, g, state, forward):
    c = GLA_CHUNK
    row = lax.broadcasted_iota(jnp.int32, (c, c), 0)
    col = lax.broadcasted_iota(jnp.int32, (c, c), 1)
    rid = lax.broadcasted_iota(jnp.int32, (c, GLA_DK), 0)
    q = q * (GLA_DK ** -0.5)
    if forward:
        see = row >= col
        upto_ref = rid <= c // 2
    else:
        see = row < col
        upto_ref = rid >= c - 1 - c // 2
    b = _cumsum_rows(g, forward)
    b_last = jnp.sum(g, axis=0, keepdims=True)
    b_ref = jnp.sum(jnp.where(upto_ref, g, 0.0), axis=0, keepdims=True)
    a = bdot(q * jnp.exp(b - b_ref), k * jnp.exp(b_ref - b), "nt")
    a = jnp.where(see, a, 0.0)
    o = bdot(a, v, "nn") + bdot(q * jnp.exp(b), state, "nt")
    new_state = state * jnp.exp(b_last) + bdot(v, k * jnp.exp(b_last - b), "tn")
    return o, new_state


def _gla_fwd(proj, g):
    t = proj.shape[0]
    c = GLA_CHUNK
    nchunk = t // c
    qb, kb, vb = _seg_block("qg", GLA_KEY_WIDTH), _seg_block("kg", GLA_KEY_WIDTH), _seg_block("vg", GLA_WIDTH)

    def body(qf, kf, vf, gf, qr, kr, vr, gr, of_ref, ob_ref, sf_ref, sb_ref, state):
        @pl.when(pl.program_id(0) == 0)
        def _():
            state[...] = jnp.zeros_like(state)

        dirs = ((qf, kf, vf, gf, of_ref, sf_ref), (qr, kr, vr, gr, ob_ref, sb_ref))
        args = [(q_ref[:, _gla_k(h)], k_ref[:, _gla_k(h)], v_ref[:, _gla_v(h)], g_ref[:, _gla_k(h)], state[d, h])
                for d, (q_ref, k_ref, v_ref, g_ref, _, _) in enumerate(dirs) for h in range(GLA_HEADS)]
        results = [_gla_chunk(*a, forward=(i < GLA_HEADS)) for i, a in enumerate(args)]
        for i, (a, (o, s_out)) in enumerate(zip(args, results)):
            d, h = divmod(i, GLA_HEADS)
            dirs[d][5][h] = a[4]
            dirs[d][4][:, _gla_v(h)] = o
            state[d, h] = s_out

    specs, outs = [], []
    for d in range(2):
        ci = (lambda i: i) if d == 0 else (lambda i: nchunk - 1 - i)
        specs += [pl.BlockSpec((c, GLA_KEY_WIDTH), lambda i, ci=ci: (ci(i), qb)),
                  pl.BlockSpec((c, GLA_KEY_WIDTH), lambda i, ci=ci: (ci(i), kb)),
                  pl.BlockSpec((c, GLA_WIDTH), lambda i, ci=ci: (ci(i), vb)),
                  pl.BlockSpec((c, GLA_KEY_WIDTH), lambda i, ci=ci, d=d: (ci(i), d))]
        outs.append(pl.BlockSpec((c, GLA_WIDTH), lambda i, ci=ci: (ci(i), 0)))
    for d in range(2):
        ci = (lambda i: i) if d == 0 else (lambda i: nchunk - 1 - i)
        outs.append(pl.BlockSpec((None, GLA_HEADS, GLA_DV, GLA_DK), lambda i, ci=ci: (ci(i), 0, 0, 0)))
    o_shape = jax.ShapeDtypeStruct((t, GLA_WIDTH), F32)
    s_shape = jax.ShapeDtypeStruct((nchunk, GLA_HEADS, GLA_DV, GLA_DK), F32)
    return pl.pallas_call(
        body, name="gla_fwd", grid=(nchunk,),
        in_specs=specs, out_specs=outs,
        out_shape=[o_shape, o_shape, s_shape, s_shape],
        scratch_shapes=[pltpu.VMEM((2, GLA_HEADS, GLA_DV, GLA_DK), F32)],
        compiler_params=_params(("arbitrary",), VMEM_LIMIT),
    )(proj, proj, proj, g, proj, proj, proj, g)


def _gla_bwd(proj, g, s_f, s_b, do):
    t = proj.shape[0]
    c = GLA_CHUNK
    nchunk = t // c
    qb, kb, vb = _seg_block("qg", GLA_KEY_WIDTH), _seg_block("kg", GLA_KEY_WIDTH), _seg_block("vg", GLA_WIDTH)

    def body(*refs):
        ins, outs, dstate = refs[:12], refs[12:20], refs[20]

        @pl.when(pl.program_id(0) == 0)
        def _():
            dstate[...] = jnp.zeros_like(dstate)

        loaded = []
        for d in range(2):
            q_ref, k_ref, v_ref, g_ref, s_ref, do_ref = ins[6 * d:6 * d + 6]
            for h in range(GLA_HEADS):
                loaded.append(((q_ref[:, _gla_k(h)], k_ref[:, _gla_k(h)], v_ref[:, _gla_v(h)], g_ref[:, _gla_k(h)],
                                s_ref[h]), (do_ref[:, _gla_v(h)], dstate[d, h])))
        grads = []
        for i, (primals, cotangents) in enumerate(loaded):
            _, vjp = jax.vjp(functools.partial(_gla_chunk, forward=(i < GLA_HEADS)), *primals)
            grads.append(vjp(cotangents))
        for i, (dq, dk, dv, dg, ds) in enumerate(grads):
            d, h = divmod(i, GLA_HEADS)
            dq_ref, dk_ref, dv_ref, dg_ref = outs[4 * d:4 * d + 4]
            dq_ref[:, _gla_k(h)] = dq
            dk_ref[:, _gla_k(h)] = dk
            dv_ref[:, _gla_v(h)] = dv
            dg_ref[:, _gla_k(h)] = dg
            dstate[d, h] = ds

    specs, outs, shapes = [], [], []
    for d in range(2):
        ci = (lambda i: nchunk - 1 - i) if d == 0 else (lambda i: i)
        specs += [pl.BlockSpec((c, GLA_KEY_WIDTH), lambda i, ci=ci: (ci(i), qb)),
                  pl.BlockSpec((c, GLA_KEY_WIDTH), lambda i, ci=ci: (ci(i), kb)),
                  pl.BlockSpec((c, GLA_WIDTH), lambda i, ci=ci: (ci(i), vb)),
                  pl.BlockSpec((c, GLA_KEY_WIDTH), lambda i, ci=ci, d=d: (ci(i), d)),
                  pl.BlockSpec((None, GLA_HEADS, GLA_DV, GLA_DK), lambda i, ci=ci: (ci(i), 0, 0, 0)),
                  pl.BlockSpec((c, GLA_WIDTH), lambda i, ci=ci: (ci(i), 0))]
        key = pl.BlockSpec((c, GLA_KEY_WIDTH), lambda i, ci=ci: (ci(i), 0))
        val = pl.BlockSpec((c, GLA_WIDTH), lambda i, ci=ci: (ci(i), 0))
        outs += [key, key, val, key]
        shapes += [jax.ShapeDtypeStruct((t, GLA_KEY_WIDTH), F32), jax.ShapeDtypeStruct((t, GLA_KEY_WIDTH), F32),
                   jax.ShapeDtypeStruct((t, GLA_WIDTH), F32), jax.ShapeDtypeStruct((t, GLA_KEY_WIDTH), F32)]
    return pl.pallas_call(
        body, name="gla_bwd", grid=(nchunk,),
        in_specs=specs, out_specs=outs, out_shape=shapes,
        scratch_shapes=[pltpu.VMEM((2, GLA_HEADS, GLA_DV, GLA_DK), F32)],
        compiler_params=_params(("arbitrary",), VMEM_LIMIT),
    )(proj, proj, proj, g, s_f, do, proj, proj, proj, g, s_b, do)


def _gla_out_head(o_f, o_b, gate, gn):
    return _rms(o_f + o_b, gn) * _silu(gate)


def _gla_out_fwd(o_f, o_b, proj, gn, *, tr=256):
    t = o_f.shape[0]
    gb = _seg_block("gate", GLA_WIDTH)

    def body(of_ref, ob_ref, gate_ref, gn_ref, out_ref):
        for h in range(GLA_HEADS):
            vc = slice(h * GLA_DV, (h + 1) * GLA_DV)
            out_ref[:, vc] = _gla_out_head(of_ref[:, vc], ob_ref[:, vc], gate_ref[:, vc], gn_ref[...]).astype(BF16)

    wide = pl.BlockSpec((tr, GLA_WIDTH), lambda i: (i, 0))
    return pl.pallas_call(
        body, name="gla_out_fwd", grid=(t // tr,),
        in_specs=[wide, wide, pl.BlockSpec((tr, GLA_WIDTH), lambda i: (i, gb)),
                  pl.BlockSpec((1, GLA_DV), lambda i: (0, 0))],
        out_specs=wide,
        out_shape=jax.ShapeDtypeStruct((t, GLA_WIDTH), BF16),
        compiler_params=_params(("parallel",), VMEM_LIMIT),
    )(o_f, o_b, proj, gn)


def _gla_out_bwd(o_f, o_b, proj, gn, dmix, *, tr=256):
    t = o_f.shape[0]
    gb = _seg_block("gate", GLA_WIDTH)

    def body(of_ref, ob_ref, gate_ref, gn_ref, dout_ref, do_ref, dgate_ref, dgn_ref):
        dgn = jnp.zeros((1, GLA_DV), F32)
        for h in range(GLA_HEADS):
            vc = slice(h * GLA_DV, (h + 1) * GLA_DV)
            _, vjp = jax.vjp(_gla_out_head, of_ref[:, vc], ob_ref[:, vc], gate_ref[:, vc], gn_ref[...])
            do, _, dgate, dg = vjp(dout_ref[:, vc])
            do_ref[:, vc] = do
            dgate_ref[:, vc] = dgate.astype(BF16)
            dgn = dgn + dg

        @pl.when(pl.program_id(0) == 0)
        def _():
            dgn_ref[...] = jnp.zeros_like(dgn_ref)

        dgn_ref[...] += dgn

    wide = pl.BlockSpec((tr, GLA_WIDTH), lambda i: (i, 0))
    vec = pl.BlockSpec((1, GLA_DV), lambda i: (0, 0))
    return pl.pallas_call(
        body, name="gla_out_bwd", grid=(t // tr,),
        in_specs=[wide, wide, pl.BlockSpec((tr, GLA_WIDTH), lambda i: (i, gb)), vec,
                  pl.BlockSpec((tr, GLA_WIDTH), lambda i: (i, 1))],
        out_specs=[wide, wide, vec],
        out_shape=[jax.ShapeDtypeStruct((t, GLA_WIDTH), F32), jax.ShapeDtypeStruct((t, GLA_WIDTH), BF16),
                   jax.ShapeDtypeStruct((1, GLA_DV), F32)],
        compiler_params=_params(("arbitrary",), VMEM_LIMIT),
    )(o_f, o_b, proj, gn, dmix)


CONV_TR = 1024
CONV_TC = 512
HALO = 8
HALO16 = 16


def _conv3(u, w, b):
    n = u.shape[0]
    return pltpu.roll(u, 1, 0) * w[0:1] + u * w[1:2] + pltpu.roll(u, n - 1, 0) * w[2:3] + b


def _conv_ext(main_ref, prev_ref, next_ref, r, nr):
    prev = prev_ref[...].astype(F32)[-HALO:] * (r > 0).astype(F32)
    nxt = next_ref[...].astype(F32)[:HALO] * (r < nr - 1).astype(F32)
    return jnp.concatenate([prev, main_ref[...].astype(F32), nxt], axis=0)


def _conv_specs(t, halo, half=None):
    per = CONV_TR // halo
    last = t // halo - 1
    lead = () if half is None else (None,)
    at = (lambda *ix: ix) if half is None else (lambda *ix: (half,) + ix)
    return [pl.BlockSpec(lead + (CONV_TR, CONV_TC), lambda j, r: at(r, j)),
            pl.BlockSpec(lead + (halo, CONV_TC), lambda j, r: at(jnp.maximum(r * per - 1, 0), j)),
            pl.BlockSpec(lead + (halo, CONV_TC), lambda j, r: at(jnp.minimum((r + 1) * per, last), j))]


def _ffn_mid_fwd(u, cw_g, cw_v, cb_g, cb_v):
    _, t, f = u.shape
    nr = t // CONV_TR

    def body(ug, ugp, ugn, uv, uvp, uvn, wg, wv, bg, bv, a_ref):
        r = pl.program_id(1)
        gate = _conv3(_conv_ext(ug, ugp, ugn, r, nr), wg[...], bg[...])[HALO:HALO + CONV_TR]
        val = _conv3(_conv_ext(uv, uvp, uvn, r, nr), wv[...], bv[...])[HALO:HALO + CONV_TR]
        a_ref[...] = (_silu(gate) * val).astype(BF16)

    w_spec = pl.BlockSpec((3, CONV_TC), lambda j, r: (0, j))
    b_spec = pl.BlockSpec((1, CONV_TC), lambda j, r: (0, j))
    return pl.pallas_call(
        body, name="ffn_mid_fwd", grid=(f // CONV_TC, nr),
        in_specs=_conv_specs(t, HALO, 0) + _conv_specs(t, HALO, 1) + [w_spec, w_spec, b_spec, b_spec],
        out_specs=pl.BlockSpec((CONV_TR, CONV_TC), lambda j, r: (r, j)),
        out_shape=jax.ShapeDtypeStruct((t, f), BF16),
        compiler_params=_params(("parallel", "parallel"), VMEM_LIMIT),
    )(u, u, u, u, u, u, cw_g, cw_v, cb_g, cb_v)


def _ffn_mid_bwd(u, cw_g, cw_v, cb_g, cb_v, da):
    _, t, f = u.shape
    nr = t // CONV_TR
    ext = CONV_TR + 2 * HALO

    def body(ug, ugp, ugn, uv, uvp, uvn, dam, dap, dan, wg, wv, bg, bv,
             du_ref, dwg_ref, dwv_ref, dbg_ref, dbv_ref):
        r = pl.program_id(1)
        shifted = []
        for main, prev, nxt in ((ug, ugp, ugn), (uv, uvp, uvn)):
            x = _conv_ext(main, prev, nxt, r, nr)
            shifted.append((pltpu.roll(x, 1, 0), x, pltpu.roll(x, ext - 1, 0)))
        da_x = _conv_ext(dam, dap, dan, r, nr)
        wg_t, wv_t = wg[...], wv[...]
        gate = shifted[0][0] * wg_t[0:1] + shifted[0][1] * wg_t[1:2] + shifted[0][2] * wg_t[2:3] + bg[...]
        val = shifted[1][0] * wv_t[0:1] + shifted[1][1] * wv_t[1:2] + shifted[1][2] * wv_t[2:3] + bv[...]
        sig = jax.nn.sigmoid(gate)
        silu = gate * sig
        d_val = da_x * silu
        d_gate = da_x * val * (sig + silu * (1.0 - sig))
        own = slice(HALO, HALO + CONV_TR)
        for half, (xs3, d, wt, dw_ref, db_ref) in enumerate(((shifted[0], d_gate, wg_t, dwg_ref, dbg_ref),
                                                            (shifted[1], d_val, wv_t, dwv_ref, dbv_ref))):
            du = pltpu.roll(d, ext - 1, 0) * wt[0:1] + d * wt[1:2] + pltpu.roll(d, 1, 0) * wt[2:3]
            du_ref[half] = du[own].astype(BF16)
            d_own = d[own]
            dw = jnp.concatenate([jnp.sum(x[own] * d_own, axis=0, keepdims=True) for x in xs3], axis=0)
            db = jnp.sum(d_own, axis=0, keepdims=True)

            @pl.when(r == 0)
            def _():
                dw_ref[...] = jnp.zeros_like(dw_ref)
                db_ref[...] = jnp.zeros_like(db_ref)

            dw_ref[...] += dw
            db_ref[...] += db

    w_spec = pl.BlockSpec((3, CONV_TC), lambda j, r: (0, j))
    b_spec = pl.BlockSpec((1, CONV_TC), lambda j, r: (0, j))
    return pl.pallas_call(
        body, name="ffn_mid_bwd", grid=(f // CONV_TC, nr),
        in_specs=(_conv_specs(t, HALO, 0) + _conv_specs(t, HALO, 1) + _conv_specs(t, HALO16)
                  + [w_spec, w_spec, b_spec, b_spec]),
        out_specs=[pl.BlockSpec((2, CONV_TR, CONV_TC), lambda j, r: (0, r, j)), w_spec, w_spec, b_spec, b_spec],
        out_shape=[jax.ShapeDtypeStruct((2, t, f), BF16),
                   jax.ShapeDtypeStruct((3, f), F32), jax.ShapeDtypeStruct((3, f), F32),
                   jax.ShapeDtypeStruct((1, f), F32), jax.ShapeDtypeStruct((1, f), F32)],
        compiler_params=_params(("parallel", "arbitrary"), VMEM_LIMIT),
    )(u, u, u, u, u, u, da, da, da, cw_g, cw_v, cb_g, cb_v)


def _down_proj_loss(act, w_down, x1, target, *, tm, tn):
    t, f = act.shape
    d = w_down.shape[1]

    def body(a_ref, w_ref, x_ref, t_ref, loss_ref, dy_ref, dyb_ref):
        y = lax.dot_general(a_ref[...], w_ref[...], _DIMS["nn"], preferred_element_type=F32) + x_ref[...]
        err = y - t_ref[...]
        dy = err * (1.0 / d)
        dy_ref[...] = dy
        dyb_ref[...] = dy.astype(BF16)
        part = 0.5 * jnp.sum(jnp.sum(err * err, axis=-1, keepdims=True) * (1.0 / d), axis=0, keepdims=True)

        @pl.when((pl.program_id(0) == 0) & (pl.program_id(1) == 0))
        def _():
            loss_ref[...] = jnp.zeros_like(loss_ref)

        loss_ref[...] += jnp.broadcast_to(part, loss_ref.shape)

    tile = pl.BlockSpec((tm, tn), lambda i, j: (i, j))
    return pl.pallas_call(
        body, name="down_proj_loss", grid=(t // tm, d // tn),
        in_specs=[pl.BlockSpec((tm, f), lambda i, j: (i, 0)), pl.BlockSpec((f, tn), lambda i, j: (0, j)), tile, tile],
        out_specs=[pl.BlockSpec((1, 128), lambda i, j: (0, 0)), tile, tile],
        out_shape=[jax.ShapeDtypeStruct((1, 128), F32), jax.ShapeDtypeStruct((t, d), F32),
                   jax.ShapeDtypeStruct((t, d), BF16)],
        compiler_params=_params(("arbitrary", "arbitrary"), VMEM_LIMIT),
    )(act, w_down, x1, target)


ANY = pl.BlockSpec(memory_space=pl.ANY)


def _position():
    return lax.axis_index("x"), lax.axis_index("y"), lax.axis_index("c")


def _other_chips(x, y):
    return [(1 - x, y), (x, 1 - y), (1 - x, 1 - y)]


def _handshake(peers):
    barrier = pltpu.get_barrier_semaphore()
    for peer in peers:
        pl.semaphore_signal(barrier, inc=1, device_id=peer, device_id_type=MESH)
    pl.semaphore_wait(barrier, len(peers))


def _exchange(body, operands, out_shapes, sems, *, name, collective_id):
    n_in, n_out = len(operands), len(out_shapes)

    def run(*refs):
        body(refs[:n_in], refs[n_in:n_in + n_out], *refs[n_in + n_out:])

    if collective_id is None:
        return pl.pallas_call(run, name=name, in_specs=[ANY] * n_in, out_specs=[ANY] * n_out,
                              out_shape=out_shapes, scratch_shapes=sems)(*operands)
    return pl.kernel(run, name=name, out_type=out_shapes,
                     mesh=plsc.ScalarSubcoreMesh(axis_name="sequencer", num_cores=1), scratch_types=sems,
                     compiler_params=pltpu.CompilerParams(collective_id=collective_id))(*operands)


def _all_gather(blocks, *, name, collective_id=None):
    na = len(blocks)

    def body(ins, outs, send_sems, recv_sems, local_sems):
        x, y, c = _position()
        me, sibling = (x, y, c), (x, y, 1 - c)
        along_x, along_y, diagonal = (1 - x, y, c), (x, 1 - y, c), (1 - x, 1 - y, c)
        first_c = c == 0
        relay_from = (jnp.where(first_c, x, 1 - x), jnp.where(first_c, 1 - y, y), c)
        relay_to = (jnp.where(first_c, 1 - x, x), jnp.where(first_c, y, 1 - y), c)
        if collective_id is not None:
            _handshake([sibling, along_x, along_y])

        def index(px, py, pc):
            return 4 * px + 2 * py + pc

        def copy(a, k, block, to, src=None):
            dst = outs[a].at[index(*block)]
            return pltpu.make_async_remote_copy(
                src_ref=dst if src is None else src, dst_ref=dst,
                send_sem=send_sems.at[a, k], recv_sem=recv_sems.at[a, k],
                device_id=to, device_id_type=MESH)

        pending = []
        for a in range(na):
            mine = pltpu.make_async_copy(ins[a], outs[a].at[index(*me)], local_sems.at[a])
            mine.start()
            pending.append(mine)
        sent = []
        for a in range(na):
            sent += [copy(a, 0, me, sibling, src=ins[a]), copy(a, 1, me, along_x, src=ins[a]),
                     copy(a, 2, me, along_y, src=ins[a])]
        for cp in sent:
            cp.start()

        def passes_on(k_in, owner, k_out):
            for a in range(na):
                copy(a, k_in, owner, me).wait_recv()
                cp = copy(a, k_out, owner, sibling)
                cp.start()
                sent.append(cp)

        passes_on(1, along_x, 4)
        passes_on(2, along_y, 5)
        for a in range(na):
            cp = copy(a, 3, relay_from, relay_to)
            cp.start()
            sent.append(cp)
        passes_on(3, diagonal, 6)
        for a in range(na):
            copy(a, 0, sibling, me).wait_recv()
            for k, owner in ((4, along_x), (5, along_y), (6, diagonal)):
                copy(a, k, (owner[0], owner[1], 1 - c), me).wait_recv()
        for cp in sent:
            cp.wait_send()
        for cp in pending:
            cp.wait()

    return _exchange(
        body, blocks, [jax.ShapeDtypeStruct((N_DEV,) + b.shape, b.dtype) for b in blocks],
        [pltpu.SemaphoreType.DMA((na, 7)), pltpu.SemaphoreType.DMA((na, 7)), pltpu.SemaphoreType.DMA((na,))],
        name=name, collective_id=collective_id)


def _grad_exchange(grads, parts, *, name, collective_id):
    ng, npart = len(grads), len(parts)

    def body(ins, outs, core_send, core_recv, chip_send, chip_recv, local_sems):
        x, y, c = _position()
        sibling = (x, y, 1 - c)
        chips = _other_chips(x, y)
        _handshake([sibling] + [(px, py, c) for px, py in chips])
        me = 2 * x + y
        copies = []
        for b in range(npart):
            src, dst = ins[ng + b], outs[ng + b]
            own = pltpu.make_async_copy(src.at[me], dst.at[me], local_sems.at[b])
            own.start()
            copies.append(own)
            for j, (px, py) in enumerate(chips):
                cp = pltpu.make_async_remote_copy(
                    src_ref=src.at[2 * px + py], dst_ref=dst.at[me],
                    send_sem=chip_send.at[b, j], recv_sem=chip_recv.at[b, j],
                    device_id=(px, py, c), device_id_type=MESH)
                cp.start()
                copies.append(cp)
        for a in range(ng):
            for k in range(N_CHIP):
                cp = pltpu.make_async_remote_copy(
                    src_ref=ins[a].at[k, 1 - c], dst_ref=outs[a].at[k],
                    send_sem=core_send.at[a, k], recv_sem=core_recv.at[a, k],
                    device_id=sibling, device_id_type=MESH)
                cp.start()
                copies.append(cp)
        for cp in copies:
            cp.wait()

    shapes = ([jax.ShapeDtypeStruct((N_CHIP,) + g.shape[2:], g.dtype) for g in grads]
              + [jax.ShapeDtypeStruct(p.shape, p.dtype) for p in parts])
    sems = [pltpu.SemaphoreType.DMA((max(ng, 1), N_CHIP)), pltpu.SemaphoreType.DMA((max(ng, 1), N_CHIP)),
            pltpu.SemaphoreType.DMA((max(npart, 1), 3)), pltpu.SemaphoreType.DMA((max(npart, 1), 3)),
            pltpu.SemaphoreType.DMA((max(npart, 1),))]
    out = _exchange(body, list(grads) + list(parts), shapes, sems, name=name, collective_id=collective_id)
    return out[:ng], out[ng:]


def _pair_sum(grad, theirs, core, *, tile, name, narrow=False):
    _, _, r, w = grad.shape
    tr, tw = tile
    assert r % tr == 0 and w % tw == 0

    def body(core_ref, mine_ref, theirs_ref, out_ref, *narrow_ref):
        total = mine_ref[...] + theirs_ref[...]
        out_ref[...] = total
        if narrow:
            narrow_ref[0][...] = total.astype(BF16)

    spec = pl.BlockSpec((None, tr, tw), lambda k, i, j, core_ref: (k, i, j))
    shapes = [jax.ShapeDtypeStruct((N_CHIP, r, w), F32)] + [jax.ShapeDtypeStruct((N_CHIP, r, w), BF16)] * narrow
    out = pl.pallas_call(
        body, name=name,
        grid_spec=pltpu.PrefetchScalarGridSpec(
            num_scalar_prefetch=1, grid=(N_CHIP, r // tr, w // tw),
            in_specs=[pl.BlockSpec((None, None, tr, tw), lambda k, i, j, core_ref: (k, core_ref[0], i, j)), spec],
            out_specs=[spec] * len(shapes)),
        out_shape=shapes,
        compiler_params=_params(("parallel", "parallel", "parallel"), VMEM_LIMIT),
    )(core, grad, theirs)
    return tuple(out) if narrow else out[0]


class _ReduceScatter:
    def __init__(self, core):
        self.core = core
        self.pending = None
        self.results = {}
        self.launches = 0

    def push(self, tag, grads, rows, then, narrow=False):
        pair, kept, prev_tag = [], [], None
        if self.pending is not None:
            prev_tag, prev, theirs, prev_rows, prev_narrow = self.pending
            pair = [_pair_sum(g, s, self.core, tile=tile, name=f"pair_sum_{prev_tag}_{i}", narrow=prev_narrow)
                    for i, (g, s, tile) in enumerate(zip(prev, theirs, prev_rows))]
            if prev_narrow:
                kept, pair = [p[0] for p in pair], [p[1] for p in pair]
        grads, pair, then = lax.optimization_barrier((list(grads), pair, then))
        grads = [g.reshape((N_CHIP, 2) + g.shape[1:]) for g in grads]
        self.launches += 1
        theirs, parts = _grad_exchange(grads, pair, name=f"grad_exchange_{self.launches}",
                                       collective_id=1 + self.launches)
        if prev_tag is not None:
            self.results[prev_tag] = (parts, kept)
        self.pending = (tag, grads, theirs, rows, narrow) if tag is not None else None
        return then

    def result(self, tag):
        return self.results[tag]


def _adamw(parts, w, m, v, *, tile, name, own=None, chip=None):
    n, r, cols = parts.shape
    tr, tw = tile
    assert r % tr == 0 and cols % tw == 0 and w.shape == (r, cols)
    c1 = 1.0 - ADAM_B1 ** ADAM_STEP
    c2 = 1.0 - ADAM_B2 ** ADAM_STEP

    def update(g, w_ref, m_ref, v_ref, g_ref, d_ref, nm_ref, nv_ref):
        new_m = ADAM_B1 * m_ref[...] + (1.0 - ADAM_B1) * g
        new_v = ADAM_B2 * v_ref[...] + (1.0 - ADAM_B2) * (g * g)
        m_hat = new_m / c1
        v_hat = new_v / c2
        g_ref[...] = g
        d_ref[...] = -ADAM_LR * (m_hat / (jnp.sqrt(v_hat) + ADAM_EPS) + ADAM_WD * w_ref[...])
        nm_ref[...] = new_m
        nv_ref[...] = new_v

    shape = jax.ShapeDtypeStruct((r, cols), F32)
    if own is None:
        def body(p_ref, *refs):
            g = p_ref[0]
            for k in range(1, n):
                g = g + p_ref[k]
            update(g, *refs)

        spec = pl.BlockSpec((tr, tw), lambda i, j: (i, j))
        return pl.pallas_call(
            body, name=name, grid=(r // tr, cols // tw),
            in_specs=[pl.BlockSpec((n, tr, tw), lambda i, j: (0, i, j)), spec, spec, spec],
            out_specs=[spec] * 4, out_shape=[shape] * 4,
            compiler_params=_params(("parallel", "parallel"), VMEM_LIMIT),
        )(parts, w, m, v)

    def body(chip_ref, p_ref, own_ref, *refs):
        g = None
        for k in range(n):
            term = jnp.where(chip_ref[0] == k, own_ref[...], p_ref[k].astype(F32))
            g = term if g is None else g + term
        update(g, *refs)

    spec = pl.BlockSpec((tr, tw), lambda i, j, chip_ref: (i, j))
    return pl.pallas_call(
        body, name=name,
        grid_spec=pltpu.PrefetchScalarGridSpec(
            num_scalar_prefetch=1, grid=(r // tr, cols // tw),
            in_specs=[pl.BlockSpec((n, tr, tw), lambda i, j, chip_ref: (0, i, j)),
                      pl.BlockSpec((None, tr, tw), lambda i, j, chip_ref: (chip_ref[0], i, j)), spec, spec, spec],
            out_specs=[spec] * 4),
        out_shape=[shape] * 4,
        compiler_params=_params(("parallel", "parallel"), VMEM_LIMIT),
    )(chip, parts, own, w, m, v)


LANES = 128


def _row_offsets(pieces):
    offsets, row = [], 0
    for p in pieces:
        assert p.shape[0] == 1 and p.shape[1] % LANES == 0, p.shape
        offsets.append(row)
        row += p.shape[1] // LANES
    return offsets, row


def _pack_rows(pieces):
    offsets, rows = _row_offsets(pieces)

    def body(*refs):
        out_ref = refs[-1]
        for ref, start in zip(refs[:-1], offsets):
            for j in range(ref.shape[1] // LANES):
                out_ref[start + j:start + j + 1, :] = ref[:, j * LANES:(j + 1) * LANES]

    return pl.pallas_call(body, name="pack_small_grads",
                          out_shape=jax.ShapeDtypeStruct((rows, LANES), F32))(*pieces)


def _adamw_rows(terms, ws, ms, vs):
    n_dev, rows, _ = terms.shape
    offsets, used = _row_offsets(ws)
    assert used + 1 == rows
    c1 = 1.0 - ADAM_B1 ** ADAM_STEP
    c2 = 1.0 - ADAM_B2 ** ADAM_STEP
    nw = len(ws)

    def body(*refs):
        t_ref = refs[0]
        w_refs, m_refs, v_refs = refs[1:1 + nw], refs[1 + nw:1 + 2 * nw], refs[1 + 2 * nw:1 + 3 * nw]
        outs = refs[1 + 3 * nw:]
        total = t_ref[0]
        for k in range(1, n_dev):
            total = total + t_ref[k]
        for i, start in enumerate(offsets):
            for j in range(ws[i].shape[1] // LANES):
                lanes = slice(j * LANES, (j + 1) * LANES)
                g = total[start + j:start + j + 1, :]
                new_m = ADAM_B1 * m_refs[i][:, lanes] + (1.0 - ADAM_B1) * g
                new_v = ADAM_B2 * v_refs[i][:, lanes] + (1.0 - ADAM_B2) * (g * g)
                delta = -ADAM_LR * ((new_m / c1) / (jnp.sqrt(new_v / c2) + ADAM_EPS) + ADAM_WD * w_refs[i][:, lanes])
                for kind, value in enumerate((g, delta, new_m, new_v)):
                    outs[kind * nw + i][:, lanes] = value
        outs[-1][...] = total[used:used + 1, :]

    shapes = [jax.ShapeDtypeStruct(w.shape, F32) for w in ws] * 4 + [jax.ShapeDtypeStruct((1, LANES), F32)]
    out = pl.pallas_call(body, name="adamw_replicated", out_shape=shapes,
                         compiler_params=_params(None, VMEM_LIMIT))(terms, *ws, *ms, *vs)
    return [list(out[kind * nw:(kind + 1) * nw]) for kind in range(4)], out[-1]


def _rope_tables(t):
    half = HEAD_DIM // 2
    inv = 1.0 / (ROPE_THETA ** (jnp.arange(half, dtype=F32) / half))
    ang = jnp.arange(t, dtype=jnp.int32).astype(F32)[:, None] * inv[None, :]
    cos, sin = jnp.cos(ang), jnp.sin(ang)
    return jnp.concatenate([cos, cos], axis=1), jnp.concatenate([-sin, sin], axis=1)


IN_KERNEL = IN_MAIN + LR_PAD


def _to_kernel_rows(w_t):
    order = sorted(SEGMENTS.values())
    pad = jnp.zeros((LR_PAD - (IN_TOTAL - IN_MAIN), w_t.shape[1]), w_t.dtype)
    return jnp.concatenate([w_t[src:src + width] for _, src, width in order] + [w_t[IN_MAIN:IN_TOTAL], pad], axis=0)


CONV_TAPS = 3
WA_BLOCK = GLA_KEY_WIDTH // N_DEV
SMALL_SIZES = (CONV_TAPS * UP_BLOCK, GLA_RANK * WA_BLOCK, GLA_RANK * WA_BLOCK)
SMALL_SHAPES = ((CONV_TAPS, UP_BLOCK), (GLA_RANK, WA_BLOCK), (GLA_RANK, WA_BLOCK))
SMALL_ROWS = sum(SMALL_SIZES) // LANES


def _small_block(conv, wa_f, wa_b):
    return jnp.concatenate([conv.reshape(-1), wa_f.reshape(-1), wa_b.reshape(-1)]).reshape(SMALL_ROWS, LANES)


def _small_unblock(block):
    flat, out, off = block.reshape(-1), [], 0
    for size, shape in zip(SMALL_SIZES, SMALL_SHAPES):
        out.append(flat[off:off + size].reshape(shape))
        off += size
    return out


def _small_blocks(conv_full, wa_f_full, wa_b_full):
    def by_device(a, width):
        return jnp.transpose(a.reshape(a.shape[0], N_DEV, width), (1, 0, 2)).reshape(N_DEV, -1)
    return jnp.concatenate([by_device(conv_full, UP_BLOCK), by_device(wa_f_full, WA_BLOCK),
                            by_device(wa_b_full, WA_BLOCK)], axis=1).reshape(N_DEV, SMALL_ROWS, LANES)


def _small_unblocks(blocks):
    flat, out, off = blocks.reshape(N_DEV, -1), [], 0
    for size, (rows, width) in zip(SMALL_SIZES, SMALL_SHAPES):
        part = flat[:, off:off + size].reshape(N_DEV, rows, width)
        out.append(jnp.transpose(part, (1, 0, 2)).reshape(rows, N_DEV * width))
        off += size
    return out


def _local_step(xs, target, norm1_g, w_in_k, gq, gk, attn_sink, w2, ba2, gla_norm_g, w_out_full, norm2_g,
                w_up8, cw_g, cw_v, cb_g, cb_v, w_down_full, rs=None):
    t = xs.shape[0]
    tm = min(1024, t)
    tall = min(2048, t)
    cos, sin_signed = _rope_tables(t)
    sink = attn_sink.reshape(ATTN_HEADS)

    h1 = _rmsnorm_fwd(xs, norm1_g, name="norm1_fwd")
    proj = _matmul(h1, w_in_k, "nt", tm=tall, tn=IN_MAIN // 4, tk=D_MODEL, n_out=IN_MAIN, name="proj_main")
    proj_lr = _matmul(h1, w_in_k[IN_MAIN:], "nt", tm=tm, tn=LR_PAD, tk=D_MODEL, name="proj_lr")
    qa, ka, va = _attn_prep_fwd(proj, cos, sin_signed, gq, gk)
    o_attn = _attn_fwd(qa, ka, va, sink)
    g_dec = _gla_prep_fwd(proj_lr, w2, ba2)
    o_f, o_b, s_f, s_b = _gla_fwd(proj, g_dec)
    o_gla = _gla_out_fwd(o_f, o_b, proj, gla_norm_g)
    x1 = _out_proj(o_attn, o_gla, w_out_full, xs, tm=tall, tn=512)
    h2 = _rmsnorm_fwd(x1, norm2_g, name="norm2_fwd")
    u = _up_proj(h2, w_up8, tm=tm)
    act = _ffn_mid_fwd(u, cw_g, cw_v, cb_g, cb_v)
    loss_part, dy, dy_b = _down_proj_loss(act, w_down_full, x1, target, tm=tm, tn=512)

    d_act = _matmul(dy_b, w_down_full, "nt", tm=tall, tn=D_FF // 4, tk=D_MODEL, out_dtype=BF16, name="d_act")
    dw_down = _matmul(act, dy_b, "tn", tm=D_FF // 4, tn=512, tk=t, name="dw_down")
    if rs is not None:
        d_act = rs.push("w_down", [dw_down.reshape(N_DEV, D_FF // N_DEV, D_MODEL)], [(64, D_MODEL)], d_act)
    du, dcw_g, dcw_v, dcb_g, dcb_v = _ffn_mid_bwd(u, cw_g, cw_v, cb_g, cb_v, d_act)
    dw_up8 = _up_proj_dw(h2, du, tm=512, tk=t)
    if rs is not None:
        du = rs.push("w_up", [dw_up8], [(256, UP_BLOCK)], du)
    dh2 = _up_proj_dx(du, w_up8, tm=tm, tn=1024)
    dx1, dx1_b, d_norm2 = _rmsnorm_bwd(x1, norm2_g, dh2, dy, name="norm2_bwd")
    dmix = _matmul(dx1_b, w_out_full, "nt", tm=tall, tn=1024, tk=D_MODEL, name="d_mix")
    dw_out = _out_proj_dw(o_attn, o_gla, dx1_b, tn=512)
    if rs is not None:
        dmix = rs.push("w_out", [dw_out.reshape(N_DEV, D_MODEL // N_DEV, D_MODEL)], [(256, D_MODEL)], dmix)
    do_gla, d_gate, d_gla_norm = _gla_out_bwd(o_f, o_b, proj, gla_norm_g, dmix)
    (dq_f, dk_f, dv_f, dg_f, dq_b, dk_b, dv_b, dg_b) = _gla_bwd(proj, g_dec, s_f, s_b, do_gla)
    d_lr, d_w2, d_ba2 = _gla_prep_bwd(proj_lr, w2, ba2, dg_f, dg_b)
    dqa, dk_lo, dk_mid, dk_hi, dv_lo, dv_mid, dv_hi, d_sink8 = _attn_bwd(qa, ka, va, sink, dmix)
    d_qa, d_ka, d_va, d_qn, d_kn = _attn_prep_bwd(proj, cos, sin_signed, gq, gk, dqa,
                                                  (dk_lo, dk_mid, dk_hi), (dv_lo, dv_mid, dv_hi))
    d_seg = {"qa": d_qa, "gate": d_gate, "vg": (dv_f + dv_b).astype(BF16), "qg": (dq_f + dq_b).astype(BF16),
             "kg": (dk_f + dk_b).astype(BF16), "ka": d_ka, "va": d_va}
    d_proj = jnp.concatenate([d_seg[k] for k in sorted(SEGMENTS, key=lambda k: SEGMENTS[k][0])] + [d_lr], axis=1)
    dw_in_t = _in_proj_dw_lr(_in_proj_dw(d_proj, h1, tn=1024), d_lr, h1)
    if rs is not None:
        per_in = IN_TOTAL // N_DEV
        small_grad = _small_blocks(jnp.concatenate([dcw_g, dcw_v], axis=1), d_w2[:GLA_RANK, :GLA_KEY_WIDTH],
                                   d_w2[GLA_RANK:2 * GLA_RANK, GLA_KEY_WIDTH:])
        d_proj, d_lr = rs.push("w_in", [dw_in_t.reshape(N_DEV, per_in, D_MODEL), small_grad],
                               [(per_in, 512), small_grad.shape[1:]], (d_proj, d_lr), narrow=True)
    dh1 = _matmul(d_proj, w_in_k, "nn", tm=tm, tn=512, tk=IN_KERNEL, name="dh1")
    if rs is not None:
        dh1 = rs.push(None, [], [], dh1)
    grad_x, _, d_norm1 = _rmsnorm_bwd(xs, norm1_g, dh1, dx1, name="norm1_bwd")
    return (loss_part, grad_x, dw_in_t, dw_out, dw_up8, dw_down, dcw_g, dcw_v, dcb_g, dcb_v,
            d_w2, d_ba2, d_norm1, d_norm2, d_qn, d_kn, d_sink8, d_gla_norm)


def kernel(x, norm1_g, w_in, attn_q_norm_g, attn_k_norm_g, attn_sink, gla_wa2_fwd, gla_ba_fwd, gla_wa2_bwd, gla_ba_bwd, gla_out_norm_g, w_out, norm2_g, w_up, conv_w, conv_b, w_down, loss_target, m_norm1_g, m_w_in, m_attn_q_norm_g, m_attn_k_norm_g, m_attn_sink, m_gla_wa2_fwd, m_gla_ba_fwd, m_gla_wa2_bwd, m_gla_ba_bwd, m_gla_out_norm_g, m_w_out, m_norm2_g, m_w_up, m_conv_w, m_conv_b, m_w_down, v_norm1_g, v_w_in, v_attn_q_norm_g, v_attn_k_norm_g, v_attn_sink, v_gla_wa2_fwd, v_gla_ba_fwd, v_gla_wa2_bwd, v_gla_ba_bwd, v_gla_out_norm_g, v_w_out, v_norm2_g, v_w_up, v_conv_w, v_conv_b, v_w_down):
    t = x.shape[1]
    xs = x.reshape(t, D_MODEL)
    target = loss_target.reshape(t, D_MODEL)
    core = lax.axis_index("c").astype(jnp.int32).reshape(1)

    w_small = _small_block(conv_w[0], gla_wa2_fwd[0], gla_wa2_bwd[0])
    w_in_t, m_in_t, v_in_t = (jnp.swapaxes(a[0], 0, 1) for a in (w_in, m_w_in, v_w_in))
    g_in, g_small = _all_gather([w_in_t.astype(BF16), w_small], name="gather_w_in")
    g_in, later = lax.optimization_barrier(
        (g_in, [w_out[0].astype(BF16), w_up[0].astype(BF16), w_down[0].astype(BF16)]))
    g_out, w_up8, g_down = _all_gather(later, name="gather_later_weights", collective_id=1)
    w_in_k = _to_kernel_rows(g_in.reshape(IN_TOTAL, D_MODEL))
    w_out_full = g_out.reshape(D_MODEL, D_MODEL)
    w_down_full = g_down.reshape(D_FF, D_MODEL)
    conv_w_full, wa2_f, wa2_b = _small_unblocks(g_small)
    cw_g, cw_v = conv_w_full[:, :D_FF], conv_w_full[:, D_FF:]
    cb_g, cb_v = conv_b[:, :D_FF], conv_b[:, D_FF:]
    w2 = jnp.zeros((LR_PAD, 2 * GLA_KEY_WIDTH), F32)
    w2 = w2.at[:GLA_RANK, :GLA_KEY_WIDTH].set(wa2_f).at[GLA_RANK:2 * GLA_RANK, GLA_KEY_WIDTH:].set(wa2_b)
    ba2 = jnp.concatenate([gla_ba_fwd, gla_ba_bwd], axis=1)
    rs = _ReduceScatter(core)
    (loss_part, grad_x, _, _, _, _, _, _, dcb_g, dcb_v, _, d_ba2,
     d_norm1, d_norm2, d_qn, d_kn, d_sink8, d_gla_norm) = _local_step(
        xs, target, norm1_g, w_in_k, attn_q_norm_g, attn_k_norm_g, attn_sink, w2, ba2, gla_out_norm_g,
        w_out_full, norm2_g, w_up8, cw_g, cw_v, cb_g, cb_v, w_down_full, rs=rs)

    (part_down,), (part_up,), (part_out,) = rs.result("w_down")[0], rs.result("w_up")[0], rs.result("w_out")[0]
    (part_in, part_small), (own_in, own_small) = rs.result("w_in")
    chip = (2 * lax.axis_index("x") + lax.axis_index("y")).astype(jnp.int32).reshape(1)
    m_small = _small_block(m_conv_w[0], m_gla_wa2_fwd[0], m_gla_wa2_bwd[0])
    v_small = _small_block(v_conv_w[0], v_gla_wa2_fwd[0], v_gla_wa2_bwd[0])
    upd_in = _adamw(part_in, w_in_t, m_in_t, v_in_t, tile=(IN_TOTAL // N_DEV, 512), name="adamw_w_in",
                    own=own_in, chip=chip)
    upd_in = [jnp.swapaxes(u, 0, 1) for u in upd_in]
    upd_out = _adamw(part_out, w_out[0], m_w_out[0], v_w_out[0], tile=(256, D_MODEL), name="adamw_w_out")
    upd_up = _adamw(part_up, w_up[0], m_w_up[0], v_w_up[0], tile=(256, UP_BLOCK), name="adamw_w_up")
    upd_down = _adamw(part_down, w_down[0], m_w_down[0], v_w_down[0], tile=(64, D_MODEL), name="adamw_w_down")
    upd_small = _adamw(part_small, w_small, m_small, v_small, tile=part_small.shape[1:],
                       name="adamw_small", own=own_small, chip=chip)
    upd_small = [_small_unblock(u) for u in upd_small]

    rep_names = ["norm1_g", "attn_q_norm_g", "attn_k_norm_g", "attn_sink", "gla_ba_fwd", "gla_ba_bwd",
                 "gla_out_norm_g", "norm2_g", "conv_b"]
    def whole_lanes(sink_like):
        return jnp.pad(sink_like, ((0, 0), (0, LANES - ATTN_HEADS)))

    rep_w = [norm1_g, attn_q_norm_g, attn_k_norm_g, whole_lanes(attn_sink), gla_ba_fwd, gla_ba_bwd, gla_out_norm_g,
             norm2_g, conv_b]
    rep_m = [m_norm1_g, m_attn_q_norm_g, m_attn_k_norm_g, whole_lanes(m_attn_sink), m_gla_ba_fwd, m_gla_ba_bwd,
             m_gla_out_norm_g, m_norm2_g, m_conv_b]
    rep_v = [v_norm1_g, v_attn_q_norm_g, v_attn_k_norm_g, whole_lanes(v_attn_sink), v_gla_ba_fwd, v_gla_ba_bwd,
             v_gla_out_norm_g, v_norm2_g, v_conv_b]
    d_sink = whole_lanes(d_sink8[:, :GQA_GROUP, 0].reshape(1, ATTN_HEADS))
    rep_g = [d_norm1, d_qn, d_kn, d_sink, d_ba2[:, :GLA_KEY_WIDTH], d_ba2[:, GLA_KEY_WIDTH:], d_gla_norm, d_norm2,
             jnp.concatenate([dcb_g, dcb_v], axis=1)]
    (rep_terms,) = _all_gather([_pack_rows(rep_g + [loss_part])], name="gather_small_grads", collective_id=7)
    upd_rep, loss_row = _adamw_rows(rep_terms, rep_w, rep_m, rep_v)
    sink_at = rep_names.index("attn_sink")
    for kind in range(4):
        upd_rep[kind][sink_at] = upd_rep[kind][sink_at][:, :ATTN_HEADS]
    loss = loss_row[0, 0]

    order = ["norm1_g", "w_in", "attn_q_norm_g", "attn_k_norm_g", "attn_sink", "gla_wa2_fwd", "gla_ba_fwd",
             "gla_wa2_bwd", "gla_ba_bwd", "gla_out_norm_g", "w_out", "norm2_g", "w_up", "conv_w", "conv_b", "w_down"]
    outs = [loss, grad_x.reshape(1, t, D_MODEL)]
    for kind in range(4):
        by_name = {n: upd_rep[kind][i] for i, n in enumerate(rep_names)}
        by_name["w_in"] = upd_in[kind][None]
        by_name["w_out"] = upd_out[kind][None]
        by_name["w_up"] = upd_up[kind][None]
        by_name["w_down"] = upd_down[kind][None]
        by_name["conv_w"] = upd_small[kind][0][None]
        by_name["gla_wa2_fwd"] = upd_small[kind][1][None]
        by_name["gla_wa2_bwd"] = upd_small[kind][2][None]
        outs += [by_name[n] for n in order]
    return tuple(outs)
```

```python
import functools

import jax
import jax.numpy as jnp
from jax import lax
from jax.experimental import pallas as pl
from jax.experimental.pallas import tpu as pltpu
from jax.experimental.pallas import tpu_sc as plsc

F32 = jnp.float32
BF16 = jnp.bfloat16

D_MODEL = 2048
HEAD_DIM = 128
ATTN_WIDTH = 1024
ATTN_HEADS = 8
KV_HEADS = 2
GQA_GROUP = 4
KV_WIDTH = KV_HEADS * HEAD_DIM
ATTN_BLOCK = 128
WINDOW = 128
ROPE_THETA = 10000.0
GLA_HEADS = 4
GLA_DK = 128
GLA_DV = 256
GLA_KEY_WIDTH = 512
GLA_WIDTH = 1024
GLA_RANK = 16
GLA_GATE_NORMALIZER = 16.0
GLA_CHUNK = 64
GLA_PER_STEP = 4
D_FF = 5632
NORM_EPS = 1e-6
IN_TOTAL = 4640
IN_MAIN = 4608
LR_PAD = 128
N_DEV = 8
N_CHIP = 4

ADAM_LR = 0.001
ADAM_B1 = 0.9
ADAM_B2 = 0.999
ADAM_EPS = 1e-08
ADAM_WD = 0.01
ADAM_STEP = 10

SEGMENTS = {
    "qa": (0, 0, 1024),
    "gate": (1024, 3584, 1024),
    "vg": (2048, 2560, 1024),
    "qg": (3072, 1536, 512),
    "kg": (3584, 2048, 512),
    "ka": (4096, 1024, 256),
    "va": (4352, 1280, 256),
}

VMEM_LIMIT = 56 * 1024 * 1024
MESH = pl.DeviceIdType.MESH


def _params(semantics=None, vmem=None):
    return pltpu.CompilerParams(dimension_semantics=semantics, vmem_limit_bytes=vmem)


_DIMS = {
    "nn": (((1,), (0,)), ((), ())),
    "nt": (((1,), (1,)), ((), ())),
    "tn": (((0,), (0,)), ((), ())),
}


def _mxu(a, b, mode):
    return lax.dot_general(a.astype(BF16), b.astype(BF16), _DIMS[mode], preferred_element_type=F32)


@functools.partial(jax.custom_vjp, nondiff_argnums=(2,))
def bdot(a, b, mode):
    return _mxu(a, b, mode)


def _bdot_fwd(a, b, mode):
    return _mxu(a, b, mode), (a, b)


def _bdot_bwd(mode, res, g):
    a, b = res
    if mode == "nn":
        return _mxu(g, b, "nt"), _mxu(a, g, "tn")
    if mode == "nt":
        return _mxu(g, b, "nn"), _mxu(g, a, "tn")
    return _mxu(b, g, "nt"), _mxu(a, g, "nn")


bdot.defvjp(_bdot_fwd, _bdot_bwd)


def _rms(x, g):
    return x * lax.rsqrt(jnp.mean(x * x, axis=-1, keepdims=True) + NORM_EPS) * g


def _rope(x, cos, sin_signed):
    return x * cos + pltpu.roll(x, HEAD_DIM // 2, 1) * sin_signed


def _rope_transposed(d, cos, sin_signed):
    return d * cos + pltpu.roll(d * sin_signed, HEAD_DIM // 2, 1)


def _silu(x):
    return x * jax.nn.sigmoid(x)


def _log_sigmoid(z):
    return -(jnp.maximum(-z, 0.0) + jnp.log(1.0 + jnp.exp(-jnp.abs(z))))


def _matmul_call(args, in_specs, o_spec, out_shape, grid, mode, nk, acc_shape, *, name, has_res=False,
                 prefetch=None, load_b=lambda ref: ref[...]):
    dims = _DIMS[mode]
    out_dtype = out_shape.dtype
    n_pre = 0 if prefetch is None else 1

    def body(*refs):
        refs = refs[n_pre:]
        if has_res:
            a_ref, b_ref, r_ref, o_ref = refs[:4]
            rest = refs[4:]
        else:
            a_ref, b_ref, o_ref = refs[:3]
            r_ref = None
            rest = refs[3:]
        part = lax.dot_general(a_ref[...], load_b(b_ref), dims, preferred_element_type=F32)

        def finish(acc):
            if r_ref is not None:
                acc = acc + r_ref[...]
            o_ref[...] = acc.astype(out_dtype)

        if nk == 1:
            finish(part)
        else:
            acc_ref = rest[0]
            kk = pl.program_id(2)

            @pl.when(kk == 0)
            def _():
                acc_ref[...] = part

            @pl.when(kk > 0)
            def _():
                acc_ref[...] += part

            @pl.when(kk == nk - 1)
            def _():
                finish(acc_ref[...])

    scratch = [pltpu.VMEM(acc_shape, F32)] if nk > 1 else []
    params = _params(("parallel", "parallel", "arbitrary"), VMEM_LIMIT)
    if prefetch is None:
        return pl.pallas_call(body, name=name, grid=grid, in_specs=in_specs, out_specs=o_spec, out_shape=out_shape,
                              scratch_shapes=scratch, compiler_params=params)(*args)
    return pl.pallas_call(
        body, name=name,
        grid_spec=pltpu.PrefetchScalarGridSpec(num_scalar_prefetch=1, grid=grid, in_specs=in_specs,
                                               out_specs=o_spec, scratch_shapes=scratch),
        out_shape=out_shape, compiler_params=params)(prefetch, *args)


def _matmul(a, b, mode, *, tm, tn, tk, out_dtype=F32, res=None, name, n_out=None):
    if mode == "nn":
        (m, k), (k2, n) = a.shape, b.shape
    elif mode == "nt":
        (m, k), (n, k2) = a.shape, b.shape
    else:
        (k, m), (k2, n) = a.shape, b.shape
    n = n if n_out is None else n_out
    assert k == k2 and m % tm == 0 and n % tn == 0 and k % tk == 0, (name, a.shape, b.shape, tm, tn, tk)
    if mode == "tn":
        a_spec = pl.BlockSpec((tk, tm), lambda i, j, kk: (kk, i))
    else:
        a_spec = pl.BlockSpec((tm, tk), lambda i, j, kk: (i, kk))
    if mode == "nt":
        b_spec = pl.BlockSpec((tn, tk), lambda i, j, kk: (j, kk))
    else:
        b_spec = pl.BlockSpec((tk, tn), lambda i, j, kk: (kk, j))
    o_spec = pl.BlockSpec((tm, tn), lambda i, j, kk: (i, j))
    in_specs, args = [a_spec, b_spec], [a, b]
    if res is not None:
        in_specs.append(o_spec)
        args.append(res)
    return _matmul_call(args, in_specs, o_spec, jax.ShapeDtypeStruct((m, n), out_dtype),
                        (m // tm, n // tn, k // tk), mode, k // tk, (tm, tn), name=name, has_res=res is not None)


def _out_proj(o_attn, o_gla, w_out, x, *, tm, tn):
    t, ka = o_attn.shape
    kg = o_gla.shape[1]

    def body(a_ref, g_ref, w_ref, x_ref, o_ref):
        acc = lax.dot_general(a_ref[...], w_ref[:ka], _DIMS["nn"], preferred_element_type=F32)
        acc = acc + lax.dot_general(g_ref[...], w_ref[ka:], _DIMS["nn"], preferred_element_type=F32)
        o_ref[...] = acc + x_ref[...]

    tile = pl.BlockSpec((tm, tn), lambda i, j: (i, j))
    return pl.pallas_call(
        body, name="out_proj", grid=(t // tm, D_MODEL // tn),
        in_specs=[pl.BlockSpec((tm, ka), lambda i, j: (i, 0)), pl.BlockSpec((tm, kg), lambda i, j: (i, 0)),
                  pl.BlockSpec((ka + kg, tn), lambda i, j: (0, j)), tile],
        out_specs=tile, out_shape=jax.ShapeDtypeStruct((t, D_MODEL), F32),
        compiler_params=_params(("parallel", "parallel"), VMEM_LIMIT),
    )(o_attn, o_gla, w_out, x)


def _out_proj_dw(o_attn, o_gla, dx1, *, tn):
    t, ka = o_attn.shape
    assert o_gla.shape == (t, ka)

    def body(a_ref, g_ref, d_ref, o_ref):
        @pl.when(pl.program_id(0) == 0)
        def _():
            o_ref[...] = lax.dot_general(a_ref[...], d_ref[...], _DIMS["tn"], preferred_element_type=F32)

        @pl.when(pl.program_id(0) == 1)
        def _():
            o_ref[...] = lax.dot_general(g_ref[...], d_ref[...], _DIMS["tn"], preferred_element_type=F32)

    whole = pl.BlockSpec((t, ka), lambda i, j: (0, 0))
    return pl.pallas_call(
        body, name="dw_out", grid=(2, D_MODEL // tn),
        in_specs=[whole, whole, pl.BlockSpec((t, tn), lambda i, j: (0, j))],
        out_specs=pl.BlockSpec((ka, tn), lambda i, j: (i, j)),
        out_shape=jax.ShapeDtypeStruct((2 * ka, D_MODEL), F32),
        compiler_params=_params(("parallel", "parallel"), VMEM_LIMIT),
    )(o_attn, o_gla, dx1)


UP_BLOCK = 2 * D_FF // N_DEV


def _up_proj(h2, w_up8, *, tm):
    t = h2.shape[0]
    return _matmul_call(
        [h2, w_up8],
        [pl.BlockSpec((tm, D_MODEL), lambda i, j, kk: (i, 0)),
         pl.BlockSpec((None, D_MODEL, UP_BLOCK), lambda i, j, kk: (j, 0, 0))],
        pl.BlockSpec((None, tm, UP_BLOCK), lambda i, j, kk: (j // N_CHIP, i, j % N_CHIP)),
        jax.ShapeDtypeStruct((2, t, D_FF), F32), (t // tm, N_DEV, 1), "nn", 1, None, name="up_proj")


def _up_proj_dx(du, w_up8, *, tm, tn):
    t = du.shape[1]
    pair = 2
    return _matmul_call(
        [du, w_up8],
        [pl.BlockSpec((None, tm, pair * UP_BLOCK), lambda i, j, kk: (kk // 2, i, kk % 2)),
         pl.BlockSpec((pair, tn, UP_BLOCK), lambda i, j, kk: (kk, j, 0))],
        pl.BlockSpec((tm, tn), lambda i, j, kk: (i, j)),
        jax.ShapeDtypeStruct((t, D_MODEL), F32), (t // tm, D_MODEL // tn, N_DEV // pair), "nt", N_DEV // pair,
        (tm, tn), name="up_proj_dx", load_b=lambda ref: jnp.concatenate([ref[0], ref[1]], axis=1))


def _up_proj_dw(h2, du, *, tm, tk):
    t = h2.shape[0]
    return _matmul_call(
        [h2, du],
        [pl.BlockSpec((tk, tm), lambda j, i, kk: (kk, i)),
         pl.BlockSpec((None, tk, UP_BLOCK), lambda j, i, kk: (j // N_CHIP, kk, j % N_CHIP))],
        pl.BlockSpec((None, tm, UP_BLOCK), lambda j, i, kk: (j, i, 0)),
        jax.ShapeDtypeStruct((N_DEV, D_MODEL, UP_BLOCK), F32), (N_DEV, D_MODEL // tm, t // tk), "tn", t // tk,
        (tm, UP_BLOCK), name="up_proj_dw")


IN_TILE = 512


def _in_proj_dw(d_proj, h1, *, tn):
    t = h1.shape[0]
    table = []
    for tile in range(IN_MAIN // IN_TILE):
        dst, src, _ = max(s for s in SEGMENTS.values() if s[0] <= tile * IN_TILE)
        assert (src + tile * IN_TILE - dst) % IN_TILE == 0
        table.append((src + tile * IN_TILE - dst) // IN_TILE)
    assert sorted(table) == list(range(IN_MAIN // IN_TILE))
    return _matmul_call(
        [d_proj, h1],
        [pl.BlockSpec((t, IN_TILE), lambda j, i, kk, tab: (0, i)),
         pl.BlockSpec((t, tn), lambda j, i, kk, tab: (0, j))],
        pl.BlockSpec((IN_TILE, tn), lambda j, i, kk, tab: (tab[i], j)),
        jax.ShapeDtypeStruct((IN_TOTAL, D_MODEL), F32), (D_MODEL // tn, IN_MAIN // IN_TILE, 1), "tn", 1, None,
        name="in_proj_dw", prefetch=jnp.asarray(table, jnp.int32))


def _in_proj_dw_lr(dw_t, d_lr, h1):
    t = h1.shape[0]
    n_lr = IN_TOTAL - IN_MAIN
    tn = 512

    def body(dw_ref, dlr_ref, h1_ref, out_ref):
        full = lax.dot_general(dlr_ref[...], h1_ref[...], _DIMS["tn"], preferred_element_type=F32)
        out_ref[...] = full[:n_lr]

    return pl.pallas_call(
        body, name="in_proj_dw_lr", grid=(D_MODEL // tn,),
        in_specs=[pl.BlockSpec(memory_space=pl.ANY),
                  pl.BlockSpec((t, LR_PAD), lambda j: (0, 0)),
                  pl.BlockSpec((t, tn), lambda j: (0, j))],
        out_specs=pl.BlockSpec((n_lr, tn), lambda j: (IN_MAIN // n_lr, j)),
        out_shape=jax.ShapeDtypeStruct(dw_t.shape, F32),
        input_output_aliases={0: 0},
        compiler_params=_params(("parallel",), VMEM_LIMIT),
    )(dw_t, d_lr, h1)


def _rmsnorm_fwd(x, g, *, name, tr=512):
    t, d = x.shape

    def body(x_ref, g_ref, h_ref):
        h_ref[...] = _rms(x_ref[...], g_ref[...]).astype(BF16)

    return pl.pallas_call(
        body, name=name, grid=(t // tr,),
        in_specs=[pl.BlockSpec((tr, d), lambda i: (i, 0)), pl.BlockSpec((1, d), lambda i: (0, 0))],
        out_specs=pl.BlockSpec((tr, d), lambda i: (i, 0)),
        out_shape=jax.ShapeDtypeStruct((t, d), BF16),
        compiler_params=_params(("parallel",), VMEM_LIMIT),
    )(x, g)


def _rmsnorm_bwd(x, g, dh, dres, *, name, tr=256):
    t, d = x.shape

    def body(x_ref, g_ref, dh_ref, dres_ref, dx_ref, dxb_ref, dg_ref):
        _, vjp = jax.vjp(_rms, x_ref[...], g_ref[...])
        dx, dg = vjp(dh_ref[...])
        dx = dx + dres_ref[...]
        dx_ref[...] = dx
        dxb_ref[...] = dx.astype(BF16)

        @pl.when(pl.program_id(0) == 0)
        def _():
            dg_ref[...] = jnp.zeros_like(dg_ref)

        dg_ref[...] += dg

    row = pl.BlockSpec((tr, d), lambda i: (i, 0))
    vec = pl.BlockSpec((1, d), lambda i: (0, 0))
    return pl.pallas_call(
        body, name=name, grid=(t // tr,),
        in_specs=[row, vec, row, row],
        out_specs=[row, row, vec],
        out_shape=[jax.ShapeDtypeStruct((t, d), F32), jax.ShapeDtypeStruct((t, d), BF16),
                   jax.ShapeDtypeStruct((1, d), F32)],
        compiler_params=_params(("arbitrary",), VMEM_LIMIT),
    )(x, g, dh, dres)


def _seg_block(name, width):
    off = SEGMENTS[name][0]
    assert off % width == 0
    return off // width


def _attn_prep_fwd(proj, cos, sin_signed, gq, gk, *, tr=256):
    t = proj.shape[0]

    def body(q_ref, k_ref, v_ref, cos_ref, sin_ref, gq_ref, gk_ref, qo_ref, ko_ref, vo_ref):
        cos_t, sin_t = cos_ref[...], sin_ref[...]
        for h in range(ATTN_HEADS):
            cols = slice(h * HEAD_DIM, (h + 1) * HEAD_DIM)
            qo_ref[:, cols] = _rope(_rms(q_ref[:, cols], gq_ref[...]), cos_t, sin_t).astype(BF16)
        for h in range(KV_HEADS):
            cols = slice(h * HEAD_DIM, (h + 1) * HEAD_DIM)
            ko_ref[:, cols] = _rope(_rms(k_ref[:, cols], gk_ref[...]), cos_t, sin_t).astype(BF16)
        vo_ref[...] = v_ref[...].astype(BF16)

    qb, kb, vb = _seg_block("qa", ATTN_WIDTH), _seg_block("ka", KV_WIDTH), _seg_block("va", KV_WIDTH)
    tab = pl.BlockSpec((tr, HEAD_DIM), lambda i: (i, 0))
    vec = pl.BlockSpec((1, HEAD_DIM), lambda i: (0, 0))
    return pl.pallas_call(
        body, name="attn_prep_fwd", grid=(t // tr,),
        in_specs=[pl.BlockSpec((tr, ATTN_WIDTH), lambda i: (i, qb)),
                  pl.BlockSpec((tr, KV_WIDTH), lambda i: (i, kb)),
                  pl.BlockSpec((tr, KV_WIDTH), lambda i: (i, vb)),
                  tab, tab, vec, vec],
        out_specs=[pl.BlockSpec((tr, ATTN_WIDTH), lambda i: (i, 0)),
                   pl.BlockSpec((tr, KV_WIDTH), lambda i: (i, 0)),
                   pl.BlockSpec((tr, KV_WIDTH), lambda i: (i, 0))],
        out_shape=[jax.ShapeDtypeStruct((t, ATTN_WIDTH), BF16),
                   jax.ShapeDtypeStruct((t, KV_WIDTH), BF16),
                   jax.ShapeDtypeStruct((t, KV_WIDTH), BF16)],
        compiler_params=_params(("parallel",), VMEM_LIMIT),
    )(proj, proj, proj, cos, sin_signed, gq, gk)


def _attn_heads(q, kcat, vcat, sink_col, valid):
    s = bdot(q, kcat, "nt") * (HEAD_DIM ** -0.5)
    s = jnp.where(valid, s, -jnp.inf)
    m = lax.stop_gradient(jnp.maximum(jnp.max(s, axis=-1, keepdims=True), sink_col))
    p = jnp.exp(s - m)
    p = p / (jnp.sum(p, axis=-1, keepdims=True) + jnp.exp(sink_col - m))
    return bdot(p, vcat, "nn")


def _attn_valid(n, t):
    shape = (GQA_GROUP * ATTN_BLOCK, 3 * ATTN_BLOCK)
    qi = lax.broadcasted_iota(jnp.int32, shape, 0) % ATTN_BLOCK
    sj = lax.broadcasted_iota(jnp.int32, shape, 1)
    kpos = n * ATTN_BLOCK - ATTN_BLOCK + sj
    return (jnp.abs(sj - ATTN_BLOCK - qi) <= WINDOW) & (kpos >= 0) & (kpos < t)


def _head_rows(g):
    return slice(g * ATTN_BLOCK, (g + 1) * ATTN_BLOCK)


def _head_cols(g):
    return slice(g * HEAD_DIM, (g + 1) * HEAD_DIM)


def _stack_heads(ref):
    return jnp.concatenate([ref[:, _head_cols(g)] for g in range(GQA_GROUP)], axis=0).astype(F32)


def _sink_column(sink_ref, h):
    return jnp.concatenate([jnp.full((ATTN_BLOCK, 1), sink_ref[h * GQA_GROUP + g], F32)
                            for g in range(GQA_GROUP)], axis=0)


def _attn_specs(nb):
    q_spec = pl.BlockSpec((ATTN_BLOCK, GQA_GROUP * HEAD_DIM), lambda h, n: (n, h))
    kv_specs = [
        pl.BlockSpec((ATTN_BLOCK, HEAD_DIM), lambda h, n: (jnp.maximum(n - 1, 0), h)),
        pl.BlockSpec((ATTN_BLOCK, HEAD_DIM), lambda h, n: (n, h)),
        pl.BlockSpec((ATTN_BLOCK, HEAD_DIM), lambda h, n: (jnp.minimum(n + 1, nb - 1), h)),
    ]
    return q_spec, kv_specs


def _attn_fwd(q, k, v, sink):
    t = q.shape[0]
    nb = t // ATTN_BLOCK

    def body(sink_ref, q_ref, kp_ref, kc_ref, kn_ref, vp_ref, vc_ref, vn_ref, o_ref):
        h, n = pl.program_id(0), pl.program_id(1)
        valid = _attn_valid(n, t)
        kcat = jnp.concatenate([kp_ref[...], kc_ref[...], kn_ref[...]], axis=0).astype(F32)
        vcat = jnp.concatenate([vp_ref[...], vc_ref[...], vn_ref[...]], axis=0).astype(F32)
        o = _attn_heads(_stack_heads(q_ref), kcat, vcat, _sink_column(sink_ref, h), valid).astype(BF16)
        for g in range(GQA_GROUP):
            o_ref[:, _head_cols(g)] = o[_head_rows(g)]

    q_spec, kv_specs = _attn_specs(nb)
    return pl.pallas_call(
        body, name="attn_fwd", grid=(KV_HEADS, nb),
        in_specs=[pl.BlockSpec(memory_space=pltpu.SMEM), q_spec] + kv_specs + kv_specs,
        out_specs=q_spec,
        out_shape=jax.ShapeDtypeStruct((t, ATTN_WIDTH), BF16),
        compiler_params=_params(("parallel", "parallel"), VMEM_LIMIT),
    )(sink, q, k, k, k, v, v, v)


def _attn_bwd(q, k, v, sink, dmix):
    t = q.shape[0]
    nb = t // ATTN_BLOCK

    def body(sink_ref, q_ref, kp_ref, kc_ref, kn_ref, vp_ref, vc_ref, vn_ref, do_ref,
             dq_ref, dk_lo, dk_mid, dk_hi, dv_lo, dv_mid, dv_hi, dsink_ref):
        h, n = pl.program_id(0), pl.program_id(1)
        valid = _attn_valid(n, t)
        kcat = jnp.concatenate([kp_ref[...], kc_ref[...], kn_ref[...]], axis=0).astype(F32)
        vcat = jnp.concatenate([vp_ref[...], vc_ref[...], vn_ref[...]], axis=0).astype(F32)
        _, vjp = jax.vjp(functools.partial(_attn_heads, valid=valid),
                         _stack_heads(q_ref), kcat, vcat, _sink_column(sink_ref, h))
        dq, dk, dv, dsink_col = vjp(_stack_heads(do_ref))
        row = lax.broadcasted_iota(jnp.int32, (8, HEAD_DIM), 0)
        dsink = jnp.zeros((8, HEAD_DIM), F32)
        for g in range(GQA_GROUP):
            dq_ref[:, _head_cols(g)] = dq[_head_rows(g)]
            dsink = dsink + jnp.where(row == g, jnp.sum(dsink_col[_head_rows(g)]), 0.0)
        for i, (dk_ref, dv_ref) in enumerate(((dk_lo, dv_lo), (dk_mid, dv_mid), (dk_hi, dv_hi))):
            rows = slice(i * ATTN_BLOCK, (i + 1) * ATTN_BLOCK)
            dk_ref[...] = dk[rows]
            dv_ref[...] = dv[rows]

        @pl.when(n == 0)
        def _():
            dsink_ref[...] = jnp.zeros_like(dsink_ref)

        dsink_ref[...] += dsink

    q_spec, kv_specs = _attn_specs(nb)
    kv_out = pl.BlockSpec((ATTN_BLOCK, HEAD_DIM), lambda h, n: (n, h))
    kv_shape = jax.ShapeDtypeStruct((t, KV_WIDTH), F32)
    return pl.pallas_call(
        body, name="attn_bwd", grid=(KV_HEADS, nb),
        in_specs=[pl.BlockSpec(memory_space=pltpu.SMEM), q_spec] + kv_specs + kv_specs + [q_spec],
        out_specs=[q_spec] + [kv_out] * 6 + [pl.BlockSpec((None, 8, HEAD_DIM), lambda h, n: (h, 0, 0))],
        out_shape=[jax.ShapeDtypeStruct((t, ATTN_WIDTH), F32)] + [kv_shape] * 6
                  + [jax.ShapeDtypeStruct((KV_HEADS, 8, HEAD_DIM), F32)],
        compiler_params=_params(("parallel", "arbitrary"), VMEM_LIMIT),
    )(sink, q, k, k, k, v, v, v, dmix)


def _attn_prep_bwd(proj, cos, sin_signed, gq, gk, dq, dks, dvs):
    t = proj.shape[0]
    tr = ATTN_BLOCK
    nb = t // tr

    def body(q_ref, k_ref, cos_ref, sin_ref, gq_ref, gk_ref, dq_ref,
             dk_lo, dk_mid, dk_hi, dv_lo, dv_mid, dv_hi,
             dqo_ref, dko_ref, dvo_ref, dgq_ref, dgk_ref):
        n = pl.program_id(0)
        cos_t, sin_t = cos_ref[...], sin_ref[...]
        has_next = (n < nb - 1).astype(F32)
        has_prev = (n > 0).astype(F32)
        dk = dk_lo[...] * has_next + dk_mid[...] + dk_hi[...] * has_prev
        dv = dv_lo[...] * has_next + dv_mid[...] + dv_hi[...] * has_prev
        dvo_ref[...] = dv.astype(BF16)
        dgq = jnp.zeros((1, HEAD_DIM), F32)
        dgk = jnp.zeros((1, HEAD_DIM), F32)
        for h in range(ATTN_HEADS):
            cols = slice(h * HEAD_DIM, (h + 1) * HEAD_DIM)
            _, vjp = jax.vjp(_rms, q_ref[:, cols], gq_ref[...])
            dx, dg = vjp(_rope_transposed(dq_ref[:, cols], cos_t, sin_t))
            dqo_ref[:, cols] = dx.astype(BF16)
            dgq = dgq + dg
        for h in range(KV_HEADS):
            cols = slice(h * HEAD_DIM, (h + 1) * HEAD_DIM)
            _, vjp = jax.vjp(_rms, k_ref[:, cols], gk_ref[...])
            dx, dg = vjp(_rope_transposed(dk[:, cols], cos_t, sin_t))
            dko_ref[:, cols] = dx.astype(BF16)
            dgk = dgk + dg

        @pl.when(n == 0)
        def _():
            dgq_ref[...] = jnp.zeros_like(dgq_ref)
            dgk_ref[...] = jnp.zeros_like(dgk_ref)

        dgq_ref[...] += dgq
        dgk_ref[...] += dgk

    qb, kb = _seg_block("qa", ATTN_WIDTH), _seg_block("ka", KV_WIDTH)
    tab = pl.BlockSpec((tr, HEAD_DIM), lambda i: (i, 0))
    vec = pl.BlockSpec((1, HEAD_DIM), lambda i: (0, 0))
    kv = [pl.BlockSpec((tr, KV_WIDTH), lambda i: (jnp.minimum(i + 1, nb - 1), 0)),
          pl.BlockSpec((tr, KV_WIDTH), lambda i: (i, 0)),
          pl.BlockSpec((tr, KV_WIDTH), lambda i: (jnp.maximum(i - 1, 0), 0))]
    wide = pl.BlockSpec((tr, ATTN_WIDTH), lambda i: (i, 0))
    narrow = pl.BlockSpec((tr, KV_WIDTH), lambda i: (i, 0))
    return pl.pallas_call(
        body, name="attn_prep_bwd", grid=(nb,),
        in_specs=[pl.BlockSpec((tr, ATTN_WIDTH), lambda i: (i, qb)),
                  pl.BlockSpec((tr, KV_WIDTH), lambda i: (i, kb)),
                  tab, tab, vec, vec, wide] + kv + kv,
        out_specs=[wide, narrow, narrow, vec, vec],
        out_shape=[jax.ShapeDtypeStruct((t, ATTN_WIDTH), BF16),
                   jax.ShapeDtypeStruct((t, KV_WIDTH), BF16),
                   jax.ShapeDtypeStruct((t, KV_WIDTH), BF16),
                   jax.ShapeDtypeStruct((1, HEAD_DIM), F32),
                   jax.ShapeDtypeStruct((1, HEAD_DIM), F32)],
        compiler_params=_params(("arbitrary",), VMEM_LIMIT),
    )(proj, proj, cos, sin_signed, gq, gk, dq, *dks, *dvs)


def _decay_fn(lr, w2, ba):
    return _log_sigmoid(bdot(lr, w2, "nn") + ba) / GLA_GATE_NORMALIZER


def _gla_prep_fwd(proj_lr, w2, ba2, *, tr=512):
    t = proj_lr.shape[0]
    width = 2 * GLA_KEY_WIDTH

    def body(lr_ref, w2_ref, ba_ref, g_ref):
        g_ref[...] = _decay_fn(lr_ref[...], w2_ref[...], ba_ref[...])

    return pl.pallas_call(
        body, name="gla_prep_fwd", grid=(t // tr,),
        in_specs=[pl.BlockSpec((tr, LR_PAD), lambda i: (i, 0)),
                  pl.BlockSpec((LR_PAD, width), lambda i: (0, 0)),
                  pl.BlockSpec((1, width), lambda i: (0, 0))],
        out_specs=pl.BlockSpec((tr, width), lambda i: (i, 0)),
        out_shape=jax.ShapeDtypeStruct((t, width), F32),
        compiler_params=_params(("parallel",), VMEM_LIMIT),
    )(proj_lr, w2, ba2)


def _gla_prep_bwd(proj_lr, w2, ba2, dg_f, dg_b, *, tr=512):
    t = proj_lr.shape[0]
    width = 2 * GLA_KEY_WIDTH

    def body(lr_ref, w2_ref, ba_ref, dgf_ref, dgb_ref, dlr_ref, dw2_ref, dba_ref):
        _, vjp = jax.vjp(_decay_fn, lr_ref[...], w2_ref[...], ba_ref[...])
        dlr, dw2, dba = vjp(jnp.concatenate([dgf_ref[...], dgb_ref[...]], axis=1))
        dlr_ref[...] = dlr.astype(BF16)

        @pl.when(pl.program_id(0) == 0)
        def _():
            dw2_ref[...] = jnp.zeros_like(dw2_ref)
            dba_ref[...] = jnp.zeros_like(dba_ref)

        dw2_ref[...] += dw2
        dba_ref[...] += dba

    half = pl.BlockSpec((tr, GLA_KEY_WIDTH), lambda i: (i, 0))
    return pl.pallas_call(
        body, name="gla_prep_bwd", grid=(t // tr,),
        in_specs=[pl.BlockSpec((tr, LR_PAD), lambda i: (i, 0)),
                  pl.BlockSpec((LR_PAD, width), lambda i: (0, 0)),
                  pl.BlockSpec((1, width), lambda i: (0, 0)), half, half],
        out_specs=[pl.BlockSpec((tr, LR_PAD), lambda i: (i, 0)),
                   pl.BlockSpec((LR_PAD, width), lambda i: (0, 0)),
                   pl.BlockSpec((1, width), lambda i: (0, 0))],
        out_shape=[jax.ShapeDtypeStruct((t, LR_PAD), BF16),
                   jax.ShapeDtypeStruct((LR_PAD, width), F32),
                   jax.ShapeDtypeStruct((1, width), F32)],
        compiler_params=_params(("arbitrary",), VMEM_LIMIT),
    )(proj_lr, w2, ba2, dg_f, dg_b)


def _gla_k(h):
    return slice(h * GLA_DK, (h + 1) * GLA_DK)


def _gla_v(h):
    return slice(h * GLA_DV, (h + 1) * GLA_DV)


def _running_sum(x, downward):
    n = x.shape[0]
    row = lax.broadcasted_iota(jnp.int32, x.shape, 0)
    step = 1
    while step < n:
        if downward:
            x = x + jnp.where(row >= step, pltpu.roll(x, step, 0), 0.0)
        else:
            x = x + jnp.where(row < n - step, pltpu.roll(x, n - step, 0), 0.0)
        step *= 2
    return x


@functools.partial(jax.custom_vjp, nondiff_argnums=(1,))
def _cumsum_rows(x, downward):
    return _running_sum(x, downward)


def _cumsum_rows_fwd(x, downward):
    return _running_sum(x, downward), None


def _cumsum_rows_bwd(downward, _, ct):
    return (_running_sum(ct, not downward),)


_cumsum_rows.defvjp(_cumsum_rows_fwd, _cumsum_rows_bwd)


def _gla_chunk(q, k, v, g, state, forward):
    c = GLA_CHUNK
    row = lax.broadcasted_iota(jnp.int32, (c, c), 0)
    col = lax.broadcasted_iota(jnp.int32, (c, c), 1)
    rid = lax.broadcasted_iota(jnp.int32, (c, GLA_DK), 0)
    q = q * (GLA_DK ** -0.5)
    if forward:
        see = row >= col
        upto_ref = rid <= c // 2
    else:
        see = row < col
        upto_ref = rid >= c - 1 - c // 2
    b = _cumsum_rows(g, forward)
    b_last = jnp.sum(g, axis=0, keepdims=True)
    b_ref = jnp.sum(jnp.where(upto_ref, g, 0.0), axis=0, keepdims=True)
    a = bdot(q * jnp.exp(b - b_ref), k * jnp.exp(b_ref - b), "nt")
    a = jnp.where(see, a, 0.0)
    o = bdot(a, v, "nn") + bdot(q * jnp.exp(b), state, "nt")
    new_state = state * jnp.exp(b_last) + bdot(v, k * jnp.exp(b_last - b), "tn")
    return o, new_state


def _gla_fwd(proj, g):
    t = proj.shape[0]
    c, per = GLA_CHUNK, GLA_PER_STEP
    nchunk = t // c
    nstep = nchunk // per
    qb, kb, vb = _seg_block("qg", GLA_KEY_WIDTH), _seg_block("kg", GLA_KEY_WIDTH), _seg_block("vg", GLA_WIDTH)

    def body(qf, kf, vf, gf, qr, kr, vr, gr, of_ref, ob_ref, sf_ref, sb_ref, state):
        @pl.when(pl.program_id(0) == 0)
        def _():
            state[...] = jnp.zeros_like(state)

        dirs = ((qf, kf, vf, gf, of_ref, sf_ref), (qr, kr, vr, gr, ob_ref, sb_ref))
        states = [[state[d, h] for h in range(GLA_HEADS)] for d in range(2)]
        for turn in range(per):
            chunk = (turn, per - 1 - turn)
            args = []
            for d, (q_ref, k_ref, v_ref, g_ref, _, _) in enumerate(dirs):
                rows = pl.ds(chunk[d] * c, c)
                args += [(q_ref[rows, _gla_k(h)], k_ref[rows, _gla_k(h)], v_ref[rows, _gla_v(h)],
                          g_ref[rows, _gla_k(h)], states[d][h]) for h in range(GLA_HEADS)]
            results = [_gla_chunk(*a, forward=(i < GLA_HEADS)) for i, a in enumerate(args)]
            for i, (a, (o, s_out)) in enumerate(zip(args, results)):
                d, h = divmod(i, GLA_HEADS)
                dirs[d][5][chunk[d], h] = a[4]
                dirs[d][4][pl.ds(chunk[d] * c, c), _gla_v(h)] = o
                states[d][h] = s_out
        for d in range(2):
            for h in range(GLA_HEADS):
                state[d, h] = states[d][h]

    specs, outs = [], []
    for d in range(2):
        ci = (lambda i: i) if d == 0 else (lambda i: nstep - 1 - i)
        specs += [pl.BlockSpec((per * c, GLA_KEY_WIDTH), lambda i, ci=ci: (ci(i), qb)),
                  pl.BlockSpec((per * c, GLA_KEY_WIDTH), lambda i, ci=ci: (ci(i), kb)),
                  pl.BlockSpec((per * c, GLA_WIDTH), lambda i, ci=ci: (ci(i), vb)),
                  pl.BlockSpec((per * c, GLA_KEY_WIDTH), lambda i, ci=ci, d=d: (ci(i), d))]
        outs.append(pl.BlockSpec((per * c, GLA_WIDTH), lambda i, ci=ci: (ci(i), 0)))
    for d in range(2):
        ci = (lambda i: i) if d == 0 else (lambda i: nstep - 1 - i)
        outs.append(pl.BlockSpec((per, GLA_HEADS, GLA_DV, GLA_DK), lambda i, ci=ci: (ci(i), 0, 0, 0)))
    o_shape = jax.ShapeDtypeStruct((t, GLA_WIDTH), F32)
    s_shape = jax.ShapeDtypeStruct((nchunk, GLA_HEADS, GLA_DV, GLA_DK), F32)
    return pl.pallas_call(
        body, name="gla_fwd", grid=(nstep,),
        in_specs=specs, out_specs=outs,
        out_shape=[o_shape, o_shape, s_shape, s_shape],
        scratch_shapes=[pltpu.VMEM((2, GLA_HEADS, GLA_DV, GLA_DK), F32)],
        compiler_params=_params(("arbitrary",), VMEM_LIMIT),
    )(proj, proj, proj, g, proj, proj, proj, g)


def _gla_bwd(proj, g, s_f, s_b, do):
    t = proj.shape[0]
    c, per = GLA_CHUNK, GLA_PER_STEP
    nchunk = t // c
    nstep = nchunk // per
    qb, kb, vb = _seg_block("qg", GLA_KEY_WIDTH), _seg_block("kg", GLA_KEY_WIDTH), _seg_block("vg", GLA_WIDTH)

    def body(*refs):
        ins, outs, dstate = refs[:12], refs[12:20], refs[20]

        @pl.when(pl.program_id(0) == 0)
        def _():
            dstate[...] = jnp.zeros_like(dstate)

        dstates = [[dstate[d, h] for h in range(GLA_HEADS)] for d in range(2)]
        for turn in range(per):
            chunk = (per - 1 - turn, turn)
            loaded = []
            for d in range(2):
                q_ref, k_ref, v_ref, g_ref, s_ref, do_ref = ins[6 * d:6 * d + 6]
                rows = pl.ds(chunk[d] * c, c)
                for h in range(GLA_HEADS):
                    loaded.append(((q_ref[rows, _gla_k(h)], k_ref[rows, _gla_k(h)], v_ref[rows, _gla_v(h)],
                                    g_ref[rows, _gla_k(h)], s_ref[chunk[d], h]),
                                   (do_ref[rows, _gla_v(h)], dstates[d][h])))
            grads = []
            for i, (primals, cotangents) in enumerate(loaded):
                _, vjp = jax.vjp(functools.partial(_gla_chunk, forward=(i < GLA_HEADS)), *primals)
                grads.append(vjp(cotangents))
            for i, (dq, dk, dv, dg, ds) in enumerate(grads):
                d, h = divmod(i, GLA_HEADS)
                rows = pl.ds(chunk[d] * c, c)
                dq_ref, dk_ref, dv_ref, dg_ref = outs[4 * d:4 * d + 4]
                dq_ref[rows, _gla_k(h)] = dq
                dk_ref[rows, _gla_k(h)] = dk
                dv_ref[rows, _gla_v(h)] = dv
                dg_ref[rows, _gla_k(h)] = dg
                dstates[d][h] = ds
        for d in range(2):
            for h in range(GLA_HEADS):
                dstate[d, h] = dstates[d][h]

    specs, outs, shapes = [], [], []
    for d in range(2):
        ci = (lambda i: nstep - 1 - i) if d == 0 else (lambda i: i)
        specs += [pl.BlockSpec((per * c, GLA_KEY_WIDTH), lambda i, ci=ci: (ci(i), qb)),
                  pl.BlockSpec((per * c, GLA_KEY_WIDTH), lambda i, ci=ci: (ci(i), kb)),
                  pl.BlockSpec((per * c, GLA_WIDTH), lambda i, ci=ci: (ci(i), vb)),
                  pl.BlockSpec((per * c, GLA_KEY_WIDTH), lambda i, ci=ci, d=d: (ci(i), d)),
                  pl.BlockSpec((per, GLA_HEADS, GLA_DV, GLA_DK), lambda i, ci=ci: (ci(i), 0, 0, 0)),
                  pl.BlockSpec((per * c, GLA_WIDTH), lambda i, ci=ci: (ci(i), 0))]
        key = pl.BlockSpec((per * c, GLA_KEY_WIDTH), lambda i, ci=ci: (ci(i), 0))
        val = pl.BlockSpec((per * c, GLA_WIDTH), lambda i, ci=ci: (ci(i), 0))
        outs += [key, key, val, key]
        shapes += [jax.ShapeDtypeStruct((t, GLA_KEY_WIDTH), F32), jax.ShapeDtypeStruct((t, GLA_KEY_WIDTH), F32),
                   jax.ShapeDtypeStruct((t, GLA_WIDTH), F32), jax.ShapeDtypeStruct((t, GLA_KEY_WIDTH), F32)]
    return pl.pallas_call(
        body, name="gla_bwd", grid=(nstep,),
        in_specs=specs, out_specs=outs, out_shape=shapes,
        scratch_shapes=[pltpu.VMEM((2, GLA_HEADS, GLA_DV, GLA_DK), F32)],
        compiler_params=_params(("arbitrary",), VMEM_LIMIT),
    )(proj, proj, proj, g, s_f, do, proj, proj, proj, g, s_b, do)


def _gla_out_head(o_f, o_b, gate, gn):
    return _rms(o_f + o_b, gn) * _silu(gate)


def _gla_out_fwd(o_f, o_b, proj, gn, *, tr=256):
    t = o_f.shape[0]
    gb = _seg_block("gate", GLA_WIDTH)

    def body(of_ref, ob_ref, gate_ref, gn_ref, out_ref):
        for h in range(GLA_HEADS):
            vc = slice(h * GLA_DV, (h + 1) * GLA_DV)
            out_ref[:, vc] = _gla_out_head(of_ref[:, vc], ob_ref[:, vc], gate_ref[:, vc], gn_ref[...]).astype(BF16)

    wide = pl.BlockSpec((tr, GLA_WIDTH), lambda i: (i, 0))
    return pl.pallas_call(
        body, name="gla_out_fwd", grid=(t // tr,),
        in_specs=[wide, wide, pl.BlockSpec((tr, GLA_WIDTH), lambda i: (i, gb)),
                  pl.BlockSpec((1, GLA_DV), lambda i: (0, 0))],
        out_specs=wide,
        out_shape=jax.ShapeDtypeStruct((t, GLA_WIDTH), BF16),
        compiler_params=_params(("parallel",), VMEM_LIMIT),
    )(o_f, o_b, proj, gn)


def _gla_out_bwd(o_f, o_b, proj, gn, dmix, *, tr=256):
    t = o_f.shape[0]
    gb = _seg_block("gate", GLA_WIDTH)

    def body(of_ref, ob_ref, gate_ref, gn_ref, dout_ref, do_ref, dgate_ref, dgn_ref):
        dgn = jnp.zeros((1, GLA_DV), F32)
        for h in range(GLA_HEADS):
            vc = slice(h * GLA_DV, (h + 1) * GLA_DV)
            _, vjp = jax.vjp(_gla_out_head, of_ref[:, vc], ob_ref[:, vc], gate_ref[:, vc], gn_ref[...])
            do, _, dgate, dg = vjp(dout_ref[:, vc])
            do_ref[:, vc] = do
            dgate_ref[:, vc] = dgate.astype(BF16)
            dgn = dgn + dg

        @pl.when(pl.program_id(0) == 0)
        def _():
            dgn_ref[...] = jnp.zeros_like(dgn_ref)

        dgn_ref[...] += dgn

    wide = pl.BlockSpec((tr, GLA_WIDTH), lambda i: (i, 0))
    vec = pl.BlockSpec((1, GLA_DV), lambda i: (0, 0))
    return pl.pallas_call(
        body, name="gla_out_bwd", grid=(t // tr,),
        in_specs=[wide, wide, pl.BlockSpec((tr, GLA_WIDTH), lambda i: (i, gb)), vec,
                  pl.BlockSpec((tr, GLA_WIDTH), lambda i: (i, 1))],
        out_specs=[wide, wide, vec],
        out_shape=[jax.ShapeDtypeStruct((t, GLA_WIDTH), F32), jax.ShapeDtypeStruct((t, GLA_WIDTH), BF16),
                   jax.ShapeDtypeStruct((1, GLA_DV), F32)],
        compiler_params=_params(("arbitrary",), VMEM_LIMIT),
    )(o_f, o_b, proj, gn, dmix)


CONV_TR = 1024
CONV_TC = 512
HALO = 8
HALO16 = 16


def _conv3(u, w, b):
    n = u.shape[0]
    return pltpu.roll(u, 1, 0) * w[0:1] + u * w[1:2] + pltpu.roll(u, n - 1, 0) * w[2:3] + b


def _conv_ext(main_ref, prev_ref, next_ref, r, nr):
    prev = prev_ref[...].astype(F32)[-HALO:] * (r > 0).astype(F32)
    nxt = next_ref[...].astype(F32)[:HALO] * (r < nr - 1).astype(F32)
    return jnp.concatenate([prev, main_ref[...].astype(F32), nxt], axis=0)


def _conv_specs(t, halo, half=None):
    per = CONV_TR // halo
    last = t // halo - 1
    lead = () if half is None else (None,)
    at = (lambda *ix: ix) if half is None else (lambda *ix: (half,) + ix)
    return [pl.BlockSpec(lead + (CONV_TR, CONV_TC), lambda j, r: at(r, j)),
            pl.BlockSpec(lead + (halo, CONV_TC), lambda j, r: at(jnp.maximum(r * per - 1, 0), j)),
            pl.BlockSpec(lead + (halo, CONV_TC), lambda j, r: at(jnp.minimum((r + 1) * per, last), j))]


def _ffn_mid_fwd(u, cw_g, cw_v, cb_g, cb_v):
    _, t, f = u.shape
    nr = t // CONV_TR

    def body(ug, ugp, ugn, uv, uvp, uvn, wg, wv, bg, bv, a_ref):
        r = pl.program_id(1)
        gate = _conv3(_conv_ext(ug, ugp, ugn, r, nr), wg[...], bg[...])[HALO:HALO + CONV_TR]
        val = _conv3(_conv_ext(uv, uvp, uvn, r, nr), wv[...], bv[...])[HALO:HALO + CONV_TR]
        a_ref[...] = (_silu(gate) * val).astype(BF16)

    w_spec = pl.BlockSpec((3, CONV_TC), lambda j, r: (0, j))
    b_spec = pl.BlockSpec((1, CONV_TC), lambda j, r: (0, j))
    return pl.pallas_call(
        body, name="ffn_mid_fwd", grid=(f // CONV_TC, nr),
        in_specs=_conv_specs(t, HALO, 0) + _conv_specs(t, HALO, 1) + [w_spec, w_spec, b_spec, b_spec],
        out_specs=pl.BlockSpec((CONV_TR, CONV_TC), lambda j, r: (r, j)),
        out_shape=jax.ShapeDtypeStruct((t, f), BF16),
        compiler_params=_params(("parallel", "parallel"), VMEM_LIMIT),
    )(u, u, u, u, u, u, cw_g, cw_v, cb_g, cb_v)


def _ffn_mid_bwd(u, cw_g, cw_v, cb_g, cb_v, da):
    _, t, f = u.shape
    nr = t // CONV_TR
    ext = CONV_TR + 2 * HALO

    def body(ug, ugp, ugn, uv, uvp, uvn, dam, dap, dan, wg, wv, bg, bv,
             du_ref, dwg_ref, dwv_ref, dbg_ref, dbv_ref):
        r = pl.program_id(1)
        shifted = []
        for main, prev, nxt in ((ug, ugp, ugn), (uv, uvp, uvn)):
            x = _conv_ext(main, prev, nxt, r, nr)
            shifted.append((pltpu.roll(x, 1, 0), x, pltpu.roll(x, ext - 1, 0)))
        da_x = _conv_ext(dam, dap, dan, r, nr)
        wg_t, wv_t = wg[...], wv[...]
        gate = shifted[0][0] * wg_t[0:1] + shifted[0][1] * wg_t[1:2] + shifted[0][2] * wg_t[2:3] + bg[...]
        val = shifted[1][0] * wv_t[0:1] + shifted[1][1] * wv_t[1:2] + shifted[1][2] * wv_t[2:3] + bv[...]
        sig = jax.nn.sigmoid(gate)
        silu = gate * sig
        d_val = da_x * silu
        d_gate = da_x * val * (sig + silu * (1.0 - sig))
        own = slice(HALO, HALO + CONV_TR)
        for half, (xs3, d, wt, dw_ref, db_ref) in enumerate(((shifted[0], d_gate, wg_t, dwg_ref, dbg_ref),
                                                            (shifted[1], d_val, wv_t, dwv_ref, dbv_ref))):
            du = pltpu.roll(d, ext - 1, 0) * wt[0:1] + d * wt[1:2] + pltpu.roll(d, 1, 0) * wt[2:3]
            du_ref[half] = du[own].astype(BF16)
            d_own = d[own]
            dw = jnp.concatenate([jnp.sum(x[own] * d_own, axis=0, keepdims=True) for x in xs3], axis=0)
            db = jnp.sum(d_own, axis=0, keepdims=True)

            @pl.when(r == 0)
            def _():
                dw_ref[...] = jnp.zeros_like(dw_ref)
                db_ref[...] = jnp.zeros_like(db_ref)

            dw_ref[...] += dw
            db_ref[...] += db

    w_spec = pl.BlockSpec((3, CONV_TC), lambda j, r: (0, j))
    b_spec = pl.BlockSpec((1, CONV_TC), lambda j, r: (0, j))
    return pl.pallas_call(
        body, name="ffn_mid_bwd", grid=(f // CONV_TC, nr),
        in_specs=(_conv_specs(t, HALO, 0) + _conv_specs(t, HALO, 1) + _conv_specs(t, HALO16)
                  + [w_spec, w_spec, b_spec, b_spec]),
        out_specs=[pl.BlockSpec((2, CONV_TR, CONV_TC), lambda j, r: (0, r, j)), w_spec, w_spec, b_spec, b_spec],
        out_shape=[jax.ShapeDtypeStruct((2, t, f), BF16),
                   jax.ShapeDtypeStruct((3, f), F32), jax.ShapeDtypeStruct((3, f), F32),
                   jax.ShapeDtypeStruct((1, f), F32), jax.ShapeDtypeStruct((1, f), F32)],
        compiler_params=_params(("parallel", "arbitrary"), VMEM_LIMIT),
    )(u, u, u, u, u, u, da, da, da, cw_g, cw_v, cb_g, cb_v)


def _down_proj_loss(act, w_down, x1, target, *, tm, tn):
    t, f = act.shape
    d = w_down.shape[1]

    def body(a_ref, w_ref, x_ref, t_ref, loss_ref, dy_ref, dyb_ref):
        y = lax.dot_general(a_ref[...], w_ref[...], _DIMS["nn"], preferred_element_type=F32) + x_ref[...]
        err = y - t_ref[...]
        dy = err * (1.0 / d)
        dy_ref[...] = dy
        dyb_ref[...] = dy.astype(BF16)
        part = 0.5 * jnp.sum(jnp.sum(err * err, axis=-1, keepdims=True) * (1.0 / d), axis=0, keepdims=True)

        @pl.when((pl.program_id(0) == 0) & (pl.program_id(1) == 0))
        def _():
            loss_ref[...] = jnp.zeros_like(loss_ref)

        loss_ref[...] += jnp.broadcast_to(part, loss_ref.shape)

    tile = pl.BlockSpec((tm, tn), lambda i, j: (i, j))
    return pl.pallas_call(
        body, name="down_proj_loss", grid=(t // tm, d // tn),
        in_specs=[pl.BlockSpec((tm, f), lambda i, j: (i, 0)), pl.BlockSpec((f, tn), lambda i, j: (0, j)), tile, tile],
        out_specs=[pl.BlockSpec((1, 128), lambda i, j: (0, 0)), tile, tile],
        out_shape=[jax.ShapeDtypeStruct((1, 128), F32), jax.ShapeDtypeStruct((t, d), F32),
                   jax.ShapeDtypeStruct((t, d), BF16)],
        compiler_params=_params(("arbitrary", "arbitrary"), VMEM_LIMIT),
    )(act, w_down, x1, target)


ANY = pl.BlockSpec(memory_space=pl.ANY)


def _position():
    return lax.axis_index("x"), lax.axis_index("y"), lax.axis_index("c")


def _other_chips(x, y):
    return [(1 - x, y), (x, 1 - y), (1 - x, 1 - y)]


def _handshake(peers):
    barrier = pltpu.get_barrier_semaphore()
    for peer in peers:
        pl.semaphore_signal(barrier, inc=1, device_id=peer, device_id_type=MESH)
    pl.semaphore_wait(barrier, len(peers))


def _exchange(body, operands, out_shapes, sems, *, name, collective_id):
    n_in, n_out = len(operands), len(out_shapes)

    def run(*refs):
        body(refs[:n_in], refs[n_in:n_in + n_out], *refs[n_in + n_out:])

    if collective_id is None:
        return pl.pallas_call(run, name=name, in_specs=[ANY] * n_in, out_specs=[ANY] * n_out,
                              out_shape=out_shapes, scratch_shapes=sems)(*operands)
    return pl.kernel(run, name=name, out_type=out_shapes,
                     mesh=plsc.ScalarSubcoreMesh(axis_name="sequencer", num_cores=1), scratch_types=sems,
                     compiler_params=pltpu.CompilerParams(collective_id=collective_id))(*operands)


def _all_gather(blocks, *, name, collective_id=None):
    na = len(blocks)

    def body(ins, outs, send_sems, recv_sems, local_sems):
        x, y, c = _position()
        me, sibling = (x, y, c), (x, y, 1 - c)
        along_x, along_y, diagonal = (1 - x, y, c), (x, 1 - y, c), (1 - x, 1 - y, c)
        first_c = c == 0
        relay_from = (jnp.where(first_c, x, 1 - x), jnp.where(first_c, 1 - y, y), c)
        relay_to = (jnp.where(first_c, 1 - x, x), jnp.where(first_c, y, 1 - y), c)
        if collective_id is not None:
            _handshake([sibling, along_x, along_y])

        def index(px, py, pc):
            return 4 * px + 2 * py + pc

        def copy(a, k, block, to, src=None):
            dst = outs[a].at[index(*block)]
            return pltpu.make_async_remote_copy(
                src_ref=dst if src is None else src, dst_ref=dst,
                send_sem=send_sems.at[a, k], recv_sem=recv_sems.at[a, k],
                device_id=to, device_id_type=MESH)

        pending = []
        for a in range(na):
            mine = pltpu.make_async_copy(ins[a], outs[a].at[index(*me)], local_sems.at[a])
            mine.start()
            pending.append(mine)
        sent = []
        for a in range(na):
            sent += [copy(a, 0, me, sibling, src=ins[a]), copy(a, 1, me, along_x, src=ins[a]),
                     copy(a, 2, me, along_y, src=ins[a])]
        for cp in sent:
            cp.start()

        def passes_on(k_in, owner, k_out):
            for a in range(na):
                copy(a, k_in, owner, me).wait_recv()
                cp = copy(a, k_out, owner, sibling)
                cp.start()
                sent.append(cp)

        passes_on(1, along_x, 4)
        passes_on(2, along_y, 5)
        for a in range(na):
            cp = copy(a, 3, relay_from, relay_to)
            cp.start()
            sent.append(cp)
        passes_on(3, diagonal, 6)
        for a in range(na):
            copy(a, 0, sibling, me).wait_recv()
            for k, owner in ((4, along_x), (5, along_y), (6, diagonal)):
                copy(a, k, (owner[0], owner[1], 1 - c), me).wait_recv()
        for cp in sent:
            cp.wait_send()
        for cp in pending:
            cp.wait()

    return _exchange(
        body, blocks, [jax.ShapeDtypeStruct((N_DEV,) + b.shape, b.dtype) for b in blocks],
        [pltpu.SemaphoreType.DMA((na, 7)), pltpu.SemaphoreType.DMA((na, 7)), pltpu.SemaphoreType.DMA((na,))],
        name=name, collective_id=collective_id)


def _grad_exchange(grads, parts, *, name, collective_id):
    ng, npart = len(grads), len(parts)

    def body(ins, outs, core_send, core_recv, chip_send, chip_recv, local_sems):
        x, y, c = _position()
        sibling = (x, y, 1 - c)
        chips = _other_chips(x, y)
        _handshake([sibling] + [(px, py, c) for px, py in chips])
        me = 2 * x + y
        copies = []
        for b in range(npart):
            src, dst = ins[ng + b], outs[ng + b]
            own = pltpu.make_async_copy(src.at[me], dst.at[me], local_sems.at[b])
            own.start()
            copies.append(own)
            for j, (px, py) in enumerate(chips):
                cp = pltpu.make_async_remote_copy(
                    src_ref=src.at[2 * px + py], dst_ref=dst.at[me],
                    send_sem=chip_send.at[b, j], recv_sem=chip_recv.at[b, j],
                    device_id=(px, py, c), device_id_type=MESH)
                cp.start()
                copies.append(cp)
        for a in range(ng):
            for k in range(N_CHIP):
                cp = pltpu.make_async_remote_copy(
                    src_ref=ins[a].at[k, 1 - c], dst_ref=outs[a].at[k],
                    send_sem=core_send.at[a, k], recv_sem=core_recv.at[a, k],
                    device_id=sibling, device_id_type=MESH)
                cp.start()
                copies.append(cp)
        for cp in copies:
            cp.wait()

    shapes = ([jax.ShapeDtypeStruct((N_CHIP,) + g.shape[2:], g.dtype) for g in grads]
              + [jax.ShapeDtypeStruct(p.shape, p.dtype) for p in parts])
    sems = [pltpu.SemaphoreType.DMA((max(ng, 1), N_CHIP)), pltpu.SemaphoreType.DMA((max(ng, 1), N_CHIP)),
            pltpu.SemaphoreType.DMA((max(npart, 1), 3)), pltpu.SemaphoreType.DMA((max(npart, 1), 3)),
            pltpu.SemaphoreType.DMA((max(npart, 1),))]
    out = _exchange(body, list(grads) + list(parts), shapes, sems, name=name, collective_id=collective_id)
    return out[:ng], out[ng:]


def _pair_sum(grad, theirs, core, *, tile, name, narrow=False):
    _, _, r, w = grad.shape
    tr, tw = tile
    assert r % tr == 0 and w % tw == 0

    def body(core_ref, mine_ref, theirs_ref, out_ref, *narrow_ref):
        total = mine_ref[...] + theirs_ref[...]
        out_ref[...] = total
        if narrow:
            narrow_ref[0][...] = total.astype(BF16)

    spec = pl.BlockSpec((None, tr, tw), lambda k, i, j, core_ref: (k, i, j))
    shapes = [jax.ShapeDtypeStruct((N_CHIP, r, w), F32)] + [jax.ShapeDtypeStruct((N_CHIP, r, w), BF16)] * narrow
    out = pl.pallas_call(
        body, name=name,
        grid_spec=pltpu.PrefetchScalarGridSpec(
            num_scalar_prefetch=1, grid=(N_CHIP, r // tr, w // tw),
            in_specs=[pl.BlockSpec((None, None, tr, tw), lambda k, i, j, core_ref: (k, core_ref[0], i, j)), spec],
            out_specs=[spec] * len(shapes)),
        out_shape=shapes,
        compiler_params=_params(("parallel", "parallel", "parallel"), VMEM_LIMIT),
    )(core, grad, theirs)
    return tuple(out) if narrow else out[0]


class _ReduceScatter:
    def __init__(self, core):
        self.core = core
        self.pending = None
        self.results = {}
        self.launches = 0

    def push(self, tag, grads, rows, then, narrow=False):
        pair, kept, prev_tag = [], [], None
        if self.pending is not None:
            prev_tag, prev, theirs, prev_rows, prev_narrow = self.pending
            pair = [_pair_sum(g, s, self.core, tile=tile, name=f"pair_sum_{prev_tag}_{i}", narrow=prev_narrow)
                    for i, (g, s, tile) in enumerate(zip(prev, theirs, prev_rows))]
            if prev_narrow:
                kept, pair = [p[0] for p in pair], [p[1] for p in pair]
        grads, pair, then = lax.optimization_barrier((list(grads), pair, then))
        grads = [g.reshape((N_CHIP, 2) + g.shape[1:]) for g in grads]
        self.launches += 1
        theirs, parts = _grad_exchange(grads, pair, name=f"grad_exchange_{self.launches}",
                                       collective_id=1 + self.launches)
        if prev_tag is not None:
            self.results[prev_tag] = (parts, kept)
        self.pending = (tag, grads, theirs, rows, narrow) if tag is not None else None
        return then

    def result(self, tag):
        return self.results[tag]


def _adamw(parts, w, m, v, *, tile, name, own=None, chip=None):
    n, r, cols = parts.shape
    tr, tw = tile
    assert r % tr == 0 and cols % tw == 0 and w.shape == (r, cols)
    c1 = 1.0 - ADAM_B1 ** ADAM_STEP
    c2 = 1.0 - ADAM_B2 ** ADAM_STEP

    def update(g, w_ref, m_ref, v_ref, g_ref, d_ref, nm_ref, nv_ref):
        new_m = ADAM_B1 * m_ref[...] + (1.0 - ADAM_B1) * g
        new_v = ADAM_B2 * v_ref[...] + (1.0 - ADAM_B2) * (g * g)
        m_hat = new_m / c1
        v_hat = new_v / c2
        g_ref[...] = g
        d_ref[...] = -ADAM_LR * (m_hat / (jnp.sqrt(v_hat) + ADAM_EPS) + ADAM_WD * w_ref[...])
        nm_ref[...] = new_m
        nv_ref[...] = new_v

    shape = jax.ShapeDtypeStruct((r, cols), F32)
    if own is None:
        def body(p_ref, *refs):
            g = p_ref[0]
            for k in range(1, n):
                g = g + p_ref[k]
            update(g, *refs)

        spec = pl.BlockSpec((tr, tw), lambda i, j: (i, j))
        return pl.pallas_call(
            body, name=name, grid=(r // tr, cols // tw),
            in_specs=[pl.BlockSpec((n, tr, tw), lambda i, j: (0, i, j)), spec, spec, spec],
            out_specs=[spec] * 4, out_shape=[shape] * 4,
            compiler_params=_params(("parallel", "parallel"), VMEM_LIMIT),
        )(parts, w, m, v)

    def body(chip_ref, p_ref, own_ref, *refs):
        g = None
        for k in range(n):
            term = jnp.where(chip_ref[0] == k, own_ref[...], p_ref[k].astype(F32))
            g = term if g is None else g + term
        update(g, *refs)

    spec = pl.BlockSpec((tr, tw), lambda i, j, chip_ref: (i, j))
    return pl.pallas_call(
        body, name=name,
        grid_spec=pltpu.PrefetchScalarGridSpec(
            num_scalar_prefetch=1, grid=(r // tr, cols // tw),
            in_specs=[pl.BlockSpec((n, tr, tw), lambda i, j, chip_ref: (0, i, j)),
                      pl.BlockSpec((None, tr, tw), lambda i, j, chip_ref: (chip_ref[0], i, j)), spec, spec, spec],
            out_specs=[spec] * 4),
        out_shape=[shape] * 4,
        compiler_params=_params(("parallel", "parallel"), VMEM_LIMIT),
    )(chip, parts, own, w, m, v)


LANES = 128


def _row_offsets(pieces):
    offsets, row = [], 0
    for p in pieces:
        assert p.shape[0] == 1 and p.shape[1] % LANES == 0, p.shape
        offsets.append(row)
        row += p.shape[1] // LANES
    return offsets, row


def _pack_rows(pieces):
    offsets, rows = _row_offsets(pieces)

    def body(*refs):
        out_ref = refs[-1]
        for ref, start in zip(refs[:-1], offsets):
            for j in range(ref.shape[1] // LANES):
                out_ref[start + j:start + j + 1, :] = ref[:, j * LANES:(j + 1) * LANES]

    return pl.pallas_call(body, name="pack_small_grads",
                          out_shape=jax.ShapeDtypeStruct((rows, LANES), F32))(*pieces)


def _adamw_rows(terms, ws, ms, vs):
    n_dev, rows, _ = terms.shape
    offsets, used = _row_offsets(ws)
    assert used + 1 == rows
    c1 = 1.0 - ADAM_B1 ** ADAM_STEP
    c2 = 1.0 - ADAM_B2 ** ADAM_STEP
    nw = len(ws)

    def body(*refs):
        t_ref = refs[0]
        w_refs, m_refs, v_refs = refs[1:1 + nw], refs[1 + nw:1 + 2 * nw], refs[1 + 2 * nw:1 + 3 * nw]
        outs = refs[1 + 3 * nw:]
        total = t_ref[0]
        for k in range(1, n_dev):
            total = total + t_ref[k]
        for i, start in enumerate(offsets):
            for j in range(ws[i].shape[1] // LANES):
                lanes = slice(j * LANES, (j + 1) * LANES)
                g = total[start + j:start + j + 1, :]
                new_m = ADAM_B1 * m_refs[i][:, lanes] + (1.0 - ADAM_B1) * g
                new_v = ADAM_B2 * v_refs[i][:, lanes] + (1.0 - ADAM_B2) * (g * g)
                delta = -ADAM_LR * ((new_m / c1) / (jnp.sqrt(new_v / c2) + ADAM_EPS) + ADAM_WD * w_refs[i][:, lanes])
                for kind, value in enumerate((g, delta, new_m, new_v)):
                    outs[kind * nw + i][:, lanes] = value
        outs[-1][...] = total[used:used + 1, :]

    shapes = [jax.ShapeDtypeStruct(w.shape, F32) for w in ws] * 4 + [jax.ShapeDtypeStruct((1, LANES), F32)]
    out = pl.pallas_call(body, name="adamw_replicated", out_shape=shapes,
                         compiler_params=_params(None, VMEM_LIMIT))(terms, *ws, *ms, *vs)
    return [list(out[kind * nw:(kind + 1) * nw]) for kind in range(4)], out[-1]


def _rope_tables(t):
    half = HEAD_DIM // 2
    inv = 1.0 / (ROPE_THETA ** (jnp.arange(half, dtype=F32) / half))
    ang = jnp.arange(t, dtype=jnp.int32).astype(F32)[:, None] * inv[None, :]
    cos, sin = jnp.cos(ang), jnp.sin(ang)
    return jnp.concatenate([cos, cos], axis=1), jnp.concatenate([-sin, sin], axis=1)


IN_KERNEL = IN_MAIN + LR_PAD


def _to_kernel_rows(w_t):
    order = sorted(SEGMENTS.values())
    pad = jnp.zeros((LR_PAD - (IN_TOTAL - IN_MAIN), w_t.shape[1]), w_t.dtype)
    return jnp.concatenate([w_t[src:src + width] for _, src, width in order] + [w_t[IN_MAIN:IN_TOTAL], pad], axis=0)


CONV_TAPS = 3
WA_BLOCK = GLA_KEY_WIDTH // N_DEV
SMALL_SIZES = (CONV_TAPS * UP_BLOCK, GLA_RANK * WA_BLOCK, GLA_RANK * WA_BLOCK)
SMALL_SHAPES = ((CONV_TAPS, UP_BLOCK), (GLA_RANK, WA_BLOCK), (GLA_RANK, WA_BLOCK))
SMALL_ROWS = sum(SMALL_SIZES) // LANES


def _small_block(conv, wa_f, wa_b):
    return jnp.concatenate([conv.reshape(-1), wa_f.reshape(-1), wa_b.reshape(-1)]).reshape(SMALL_ROWS, LANES)


def _small_unblock(block):
    flat, out, off = block.reshape(-1), [], 0
    for size, shape in zip(SMALL_SIZES, SMALL_SHAPES):
        out.append(flat[off:off + size].reshape(shape))
        off += size
    return out


def _small_blocks(conv_full, wa_f_full, wa_b_full):
    def by_device(a, width):
        return jnp.transpose(a.reshape(a.shape[0], N_DEV, width), (1, 0, 2)).reshape(N_DEV, -1)
    return jnp.concatenate([by_device(conv_full, UP_BLOCK), by_device(wa_f_full, WA_BLOCK),
                            by_device(wa_b_full, WA_BLOCK)], axis=1).reshape(N_DEV, SMALL_ROWS, LANES)


def _small_unblocks(blocks):
    flat, out, off = blocks.reshape(N_DEV, -1), [], 0
    for size, (rows, width) in zip(SMALL_SIZES, SMALL_SHAPES):
        part = flat[:, off:off + size].reshape(N_DEV, rows, width)
        out.append(jnp.transpose(part, (1, 0, 2)).reshape(rows, N_DEV * width))
        off += size
    return out


def _local_step(xs, target, norm1_g, w_in_k, gq, gk, attn_sink, w2, ba2, gla_norm_g, w_out_full, norm2_g,
                w_up8, cw_g, cw_v, cb_g, cb_v, w_down_full, rs=None):
    t = xs.shape[0]
    tm = min(1024, t)
    tall = min(2048, t)
    cos, sin_signed = _rope_tables(t)
    sink = attn_sink.reshape(ATTN_HEADS)

    h1 = _rmsnorm_fwd(xs, norm1_g, name="norm1_fwd")
    proj = _matmul(h1, w_in_k, "nt", tm=tall, tn=IN_MAIN // 4, tk=D_MODEL, n_out=IN_MAIN, name="proj_main")
    proj_lr = _matmul(h1, w_in_k[IN_MAIN:], "nt", tm=tm, tn=LR_PAD, tk=D_MODEL, name="proj_lr")
    qa, ka, va = _attn_prep_fwd(proj, cos, sin_signed, gq, gk)
    o_attn = _attn_fwd(qa, ka, va, sink)
    g_dec = _gla_prep_fwd(proj_lr, w2, ba2)
    o_f, o_b, s_f, s_b = _gla_fwd(proj, g_dec)
    o_gla = _gla_out_fwd(o_f, o_b, proj, gla_norm_g)
    x1 = _out_proj(o_attn, o_gla, w_out_full, xs, tm=tall, tn=512)
    h2 = _rmsnorm_fwd(x1, norm2_g, name="norm2_fwd")
    u = _up_proj(h2, w_up8, tm=tm)
    act = _ffn_mid_fwd(u, cw_g, cw_v, cb_g, cb_v)
    loss_part, dy, dy_b = _down_proj_loss(act, w_down_full, x1, target, tm=tm, tn=512)

    d_act = _matmul(dy_b, w_down_full, "nt", tm=tall, tn=D_FF // 4, tk=D_MODEL, out_dtype=BF16, name="d_act")
    dw_down = _matmul(act, dy_b, "tn", tm=D_FF // 4, tn=512, tk=t, name="dw_down")
    if rs is not None:
        d_act = rs.push("w_down", [dw_down.reshape(N_DEV, D_FF // N_DEV, D_MODEL)], [(64, D_MODEL)], d_act)
    du, dcw_g, dcw_v, dcb_g, dcb_v = _ffn_mid_bwd(u, cw_g, cw_v, cb_g, cb_v, d_act)
    dw_up8 = _up_proj_dw(h2, du, tm=512, tk=t)
    if rs is not None:
        du = rs.push("w_up", [dw_up8], [(256, UP_BLOCK)], du)
    dh2 = _up_proj_dx(du, w_up8, tm=tm, tn=1024)
    dx1, dx1_b, d_norm2 = _rmsnorm_bwd(x1, norm2_g, dh2, dy, name="norm2_bwd")
    dmix = _matmul(dx1_b, w_out_full, "nt", tm=tall, tn=1024, tk=D_MODEL, name="d_mix")
    dw_out = _out_proj_dw(o_attn, o_gla, dx1_b, tn=512)
    if rs is not None:
        dmix = rs.push("w_out", [dw_out.reshape(N_DEV, D_MODEL // N_DEV, D_MODEL)], [(256, D_MODEL)], dmix)
    do_gla, d_gate, d_gla_norm = _gla_out_bwd(o_f, o_b, proj, gla_norm_g, dmix)
    (dq_f, dk_f, dv_f, dg_f, dq_b, dk_b, dv_b, dg_b) = _gla_bwd(proj, g_dec, s_f, s_b, do_gla)
    d_lr, d_w2, d_ba2 = _gla_prep_bwd(proj_lr, w2, ba2, dg_f, dg_b)
    dqa, dk_lo, dk_mid, dk_hi, dv_lo, dv_mid, dv_hi, d_sink8 = _attn_bwd(qa, ka, va, sink, dmix)
    d_qa, d_ka, d_va, d_qn, d_kn = _attn_prep_bwd(proj, cos, sin_signed, gq, gk, dqa,
                                                  (dk_lo, dk_mid, dk_hi), (dv_lo, dv_mid, dv_hi))
    d_seg = {"qa": d_qa, "gate": d_gate, "vg": (dv_f + dv_b).astype(BF16), "qg": (dq_f + dq_b).astype(BF16),
             "kg": (dk_f + dk_b).astype(BF16), "ka": d_ka, "va": d_va}
    d_proj = jnp.concatenate([d_seg[k] for k in sorted(SEGMENTS, key=lambda k: SEGMENTS[k][0])] + [d_lr], axis=1)
    dw_in_t = _in_proj_dw_lr(_in_proj_dw(d_proj, h1, tn=1024), d_lr, h1)
    if rs is not None:
        per_in = IN_TOTAL // N_DEV
        small_grad = _small_blocks(jnp.concatenate([dcw_g, dcw_v], axis=1), d_w2[:GLA_RANK, :GLA_KEY_WIDTH],
                                   d_w2[GLA_RANK:2 * GLA_RANK, GLA_KEY_WIDTH:])
        d_proj, d_lr = rs.push("w_in", [dw_in_t.reshape(N_DEV, per_in, D_MODEL), small_grad],
                               [(per_in, 512), small_grad.shape[1:]], (d_proj, d_lr), narrow=True)
    dh1 = _matmul(d_proj, w_in_k, "nn", tm=tm, tn=512, tk=IN_KERNEL, name="dh1")
    if rs is not None:
        dh1 = rs.push(None, [], [], dh1)
    grad_x, _, d_norm1 = _rmsnorm_bwd(xs, norm1_g, dh1, dx1, name="norm1_bwd")
    return (loss_part, grad_x, dw_in_t, dw_out, dw_up8, dw_down, dcw_g, dcw_v, dcb_g, dcb_v,
            d_w2, d_ba2, d_norm1, d_norm2, d_qn, d_kn, d_sink8, d_gla_norm)


def kernel(x, norm1_g, w_in, attn_q_norm_g, attn_k_norm_g, attn_sink, gla_wa2_fwd, gla_ba_fwd, gla_wa2_bwd, gla_ba_bwd, gla_out_norm_g, w_out, norm2_g, w_up, conv_w, conv_b, w_down, loss_target, m_norm1_g, m_w_in, m_attn_q_norm_g, m_attn_k_norm_g, m_attn_sink, m_gla_wa2_fwd, m_gla_ba_fwd, m_gla_wa2_bwd, m_gla_ba_bwd, m_gla_out_norm_g, m_w_out, m_norm2_g, m_w_up, m_conv_w, m_conv_b, m_w_down, v_norm1_g, v_w_in, v_attn_q_norm_g, v_attn_k_norm_g, v_attn_sink, v_gla_wa2_fwd, v_gla_ba_fwd, v_gla_wa2_bwd, v_gla_ba_bwd, v_gla_out_norm_g, v_w_out, v_norm2_g, v_w_up, v_conv_w, v_conv_b, v_w_down):
    t = x.shape[1]
    xs = x.reshape(t, D_MODEL)
    target = loss_target.reshape(t, D_MODEL)
    core = lax.axis_index("c").astype(jnp.int32).reshape(1)

    w_small = _small_block(conv_w[0], gla_wa2_fwd[0], gla_wa2_bwd[0])
    w_in_t, m_in_t, v_in_t = (jnp.swapaxes(a[0], 0, 1) for a in (w_in, m_w_in, v_w_in))
    g_in, g_small = _all_gather([w_in_t.astype(BF16), w_small], name="gather_w_in")
    g_in, later = lax.optimization_barrier(
        (g_in, [w_out[0].astype(BF16), w_up[0].astype(BF16), w_down[0].astype(BF16)]))
    g_out, w_up8, g_down = _all_gather(later, name="gather_later_weights", collective_id=1)
    w_in_k = _to_kernel_rows(g_in.reshape(IN_TOTAL, D_MODEL))
    w_out_full = g_out.reshape(D_MODEL, D_MODEL)
    w_down_full = g_down.reshape(D_FF, D_MODEL)
    conv_w_full, wa2_f, wa2_b = _small_unblocks(g_small)
    cw_g, cw_v = conv_w_full[:, :D_FF], conv_w_full[:, D_FF:]
    cb_g, cb_v = conv_b[:, :D_FF], conv_b[:, D_FF:]
    w2 = jnp.zeros((LR_PAD, 2 * GLA_KEY_WIDTH), F32)
    w2 = w2.at[:GLA_RANK, :GLA_KEY_WIDTH].set(wa2_f).at[GLA_RANK:2 * GLA_RANK, GLA_KEY_WIDTH:].set(wa2_b)
    ba2 = jnp.concatenate([gla_ba_fwd, gla_ba_bwd], axis=1)
    rs = _ReduceScatter(core)
    (loss_part, grad_x, _, _, _, _, _, _, dcb_g, dcb_v, _, d_ba2,
     d_norm1, d_norm2, d_qn, d_kn, d_sink8, d_gla_norm) = _local_step(
        xs, target, norm1_g, w_in_k, attn_q_norm_g, attn_k_norm_g, attn_sink, w2, ba2, gla_out_norm_g,
        w_out_full, norm2_g, w_up8, cw_g, cw_v, cb_g, cb_v, w_down_full, rs=rs)

    (part_down,), (part_up,), (part_out,) = rs.result("w_down")[0], rs.result("w_up")[0], rs.result("w_out")[0]
    (part_in, part_small), (own_in, own_small) = rs.result("w_in")
    chip = (2 * lax.axis_index("x") + lax.axis_index("y")).astype(jnp.int32).reshape(1)
    m_small = _small_block(m_conv_w[0], m_gla_wa2_fwd[0], m_gla_wa2_bwd[0])
    v_small = _small_block(v_conv_w[0], v_gla_wa2_fwd[0], v_gla_wa2_bwd[0])
    upd_in = _adamw(part_in, w_in_t, m_in_t, v_in_t, tile=(IN_TOTAL // N_DEV, 512), name="adamw_w_in",
                    own=own_in, chip=chip)
    upd_in = [jnp.swapaxes(u, 0, 1) for u in upd_in]
    upd_out = _adamw(part_out, w_out[0], m_w_out[0], v_w_out[0], tile=(256, D_MODEL), name="adamw_w_out")
    upd_up = _adamw(part_up, w_up[0], m_w_up[0], v_w_up[0], tile=(256, UP_BLOCK), name="adamw_w_up")
    upd_down = _adamw(part_down, w_down[0], m_w_down[0], v_w_down[0], tile=(64, D_MODEL), name="adamw_w_down")
    upd_small = _adamw(part_small, w_small, m_small, v_small, tile=part_small.shape[1:],
                       name="adamw_small", own=own_small, chip=chip)
    upd_small = [_small_unblock(u) for u in upd_small]

    rep_names = ["norm1_g", "attn_q_norm_g", "attn_k_norm_g", "attn_sink", "gla_ba_fwd", "gla_ba_bwd",
                 "gla_out_norm_g", "norm2_g", "conv_b"]
    def whole_lanes(sink_like):
        return jnp.pad(sink_like, ((0, 0), (0, LANES - ATTN_HEADS)))

    rep_w = [norm1_g, attn_q_norm_g, attn_k_norm_g, whole_lanes(attn_sink), gla_ba_fwd, gla_ba_bwd, gla_out_norm_g,
             norm2_g, conv_b]
    rep_m = [m_norm1_g, m_attn_q_norm_g, m_attn_k_norm_g, whole_lanes(m_attn_sink), m_gla_ba_fwd, m_gla_ba_bwd,
             m_gla_out_norm_g, m_norm2_g, m_conv_b]
    rep_v = [v_norm1_g, v_attn_q_norm_g, v_attn_k_norm_g, whole_lanes(v_attn_sink), v_gla_ba_fwd, v_gla_ba_bwd,
             v_gla_out_norm_g, v_norm2_g, v_conv_b]
    d_sink = whole_lanes(d_sink8[:, :GQA_GROUP, 0].reshape(1, ATTN_HEADS))
    rep_g = [d_norm1, d_qn, d_kn, d_sink, d_ba2[:, :GLA_KEY_WIDTH], d_ba2[:, GLA_KEY_WIDTH:], d_gla_norm, d_norm2,
             jnp.concatenate([dcb_g, dcb_v], axis=1)]
    (rep_terms,) = _all_gather([_pack_rows(rep_g + [loss_part])], name="gather_small_grads", collective_id=7)
    upd_rep, loss_row = _adamw_rows(rep_terms, rep_w, rep_m, rep_v)
    sink_at = rep_names.index("attn_sink")
    for kind in range(4):
        upd_rep[kind][sink_at] = upd_rep[kind][sink_at][:, :ATTN_HEADS]
    loss = loss_row[0, 0]

    order = ["norm1_g", "w_in", "attn_q_norm_g", "attn_k_norm_g", "attn_sink", "gla_wa2_fwd", "gla_ba_fwd",
             "gla_wa2_bwd", "gla_ba_bwd", "gla_out_norm_g", "w_out", "norm2_g", "w_up", "conv_w", "conv_b", "w_down"]
    outs = [loss, grad_x.reshape(1, t, D_MODEL)]
    for kind in range(4):
        by_name = {n: upd_rep[kind][i] for i, n in enumerate(rep_names)}
        by_name["w_in"] = upd_in[kind][None]
        by_name["w_out"] = upd_out[kind][None]
        by_name["w_up"] = upd_up[kind][None]
        by_name["w_down"] = upd_down[kind][None]
        by_name["conv_w"] = upd_small[kind][0][None]
        by_name["gla_wa2_fwd"] = upd_small[kind][1][None]
        by_name["gla_wa2_bwd"] = upd_small[kind][2][None]
        outs += [by_name[n] for n in order]
    return tuple(outs)
```

```python
import functools

import jax
import jax.numpy as jnp
from jax import lax
from jax.experimental import pallas as pl
from jax.experimental.pallas import tpu as pltpu
from jax.experimental.pallas import tpu_sc as plsc

F32 = jnp.float32
BF16 = jnp.bfloat16

D_MODEL = 2048
HEAD_DIM = 128
ATTN_WIDTH = 1024
ATTN_HEADS = 8
KV_HEADS = 2
GQA_GROUP = 4
KV_WIDTH = KV_HEADS * HEAD_DIM
ATTN_BLOCK = 128
WINDOW = 128
ROPE_THETA = 10000.0
GLA_HEADS = 4
GLA_DK = 128
GLA_DV = 256
GLA_KEY_WIDTH = 512
GLA_WIDTH = 1024
GLA_RANK = 16
GLA_GATE_NORMALIZER = 16.0
GLA_CHUNK = 64
GLA_PER_STEP = 4
D_FF = 5632
NORM_EPS = 1e-6
IN_TOTAL = 4640
IN_MAIN = 4608
LR_PAD = 128
N_DEV = 8
N_CHIP = 4

ADAM_LR = 0.001
ADAM_B1 = 0.9
ADAM_B2 = 0.999
ADAM_EPS = 1e-08
ADAM_WD = 0.01
ADAM_STEP = 10

SEGMENTS = {
    "qa": (0, 0, 1024),
    "gate": (1024, 3584, 1024),
    "vg": (2048, 2560, 1024),
    "qg": (3072, 1536, 512),
    "kg": (3584, 2048, 512),
    "ka": (4096, 1024, 256),
    "va": (4352, 1280, 256),
}

VMEM_LIMIT = 56 * 1024 * 1024
MESH = pl.DeviceIdType.MESH


def _params(semantics=None, vmem=None):
    return pltpu.CompilerParams(dimension_semantics=semantics, vmem_limit_bytes=vmem)


_DIMS = {
    "nn": (((1,), (0,)), ((), ())),
    "nt": (((1,), (1,)), ((), ())),
    "tn": (((0,), (0,)), ((), ())),
}


def _mxu(a, b, mode):
    return lax.dot_general(a.astype(BF16), b.astype(BF16), _DIMS[mode], preferred_element_type=F32)


@functools.partial(jax.custom_vjp, nondiff_argnums=(2,))
def bdot(a, b, mode):
    return _mxu(a, b, mode)


def _bdot_fwd(a, b, mode):
    return _mxu(a, b, mode), (a, b)


def _bdot_bwd(mode, res, g):
    a, b = res
    if mode == "nn":
        return _mxu(g, b, "nt"), _mxu(a, g, "tn")
    if mode == "nt":
        return _mxu(g, b, "nn"), _mxu(g, a, "tn")
    return _mxu(b, g, "nt"), _mxu(a, g, "nn")


bdot.defvjp(_bdot_fwd, _bdot_bwd)


def _rms(x, g):
    return x * lax.rsqrt(jnp.mean(x * x, axis=-1, keepdims=True) + NORM_EPS) * g


def _rope(x, cos, sin_signed):
    return x * cos + pltpu.roll(x, HEAD_DIM // 2, 1) * sin_signed


def _rope_transposed(d, cos, sin_signed):
    return d * cos + pltpu.roll(d * sin_signed, HEAD_DIM // 2, 1)


def _silu(x):
    return x * jax.nn.sigmoid(x)


def _log_sigmoid(z):
    return -(jnp.maximum(-z, 0.0) + jnp.log(1.0 + jnp.exp(-jnp.abs(z))))


def _matmul_call(args, in_specs, o_spec, out_shape, grid, mode, nk, acc_shape, *, name, has_res=False,
                 prefetch=None, load_b=lambda ref: ref[...]):
    dims = _DIMS[mode]
    out_dtype = out_shape.dtype
    n_pre = 0 if prefetch is None else 1

    def body(*refs):
        refs = refs[n_pre:]
        if has_res:
            a_ref, b_ref, r_ref, o_ref = refs[:4]
            rest = refs[4:]
        else:
            a_ref, b_ref, o_ref = refs[:3]
            r_ref = None
            rest = refs[3:]
        part = lax.dot_general(a_ref[...], load_b(b_ref), dims, preferred_element_type=F32)

        def finish(acc):
            if r_ref is not None:
                acc = acc + r_ref[...]
            o_ref[...] = acc.astype(out_dtype)

        if nk == 1:
            finish(part)
        else:
            acc_ref = rest[0]
            kk = pl.program_id(2)

            @pl.when(kk == 0)
            def _():
                acc_ref[...] = part

            @pl.when(kk > 0)
            def _():
                acc_ref[...] += part

            @pl.when(kk == nk - 1)
            def _():
                finish(acc_ref[...])

    scratch = [pltpu.VMEM(acc_shape, F32)] if nk > 1 else []
    params = _params(("parallel", "parallel", "arbitrary"), VMEM_LIMIT)
    if prefetch is None:
        return pl.pallas_call(body, name=name, grid=grid, in_specs=in_specs, out_specs=o_spec, out_shape=out_shape,
                              scratch_shapes=scratch, compiler_params=params)(*args)
    return pl.pallas_call(
        body, name=name,
        grid_spec=pltpu.PrefetchScalarGridSpec(num_scalar_prefetch=1, grid=grid, in_specs=in_specs,
                                               out_specs=o_spec, scratch_shapes=scratch),
        out_shape=out_shape, compiler_params=params)(prefetch, *args)


def _matmul(a, b, mode, *, tm, tn, tk, out_dtype=F32, res=None, name, n_out=None):
    if mode == "nn":
        (m, k), (k2, n) = a.shape, b.shape
    elif mode == "nt":
        (m, k), (n, k2) = a.shape, b.shape
    else:
        (k, m), (k2, n) = a.shape, b.shape
    n = n if n_out is None else n_out
    assert k == k2 and m % tm == 0 and n % tn == 0 and k % tk == 0, (name, a.shape, b.shape, tm, tn, tk)
    if mode == "tn":
        a_spec = pl.BlockSpec((tk, tm), lambda i, j, kk: (kk, i))
    else:
        a_spec = pl.BlockSpec((tm, tk), lambda i, j, kk: (i, kk))
    if mode == "nt":
        b_spec = pl.BlockSpec((tn, tk), lambda i, j, kk: (j, kk))
    else:
        b_spec = pl.BlockSpec((tk, tn), lambda i, j, kk: (kk, j))
    o_spec = pl.BlockSpec((tm, tn), lambda i, j, kk: (i, j))
    in_specs, args = [a_spec, b_spec], [a, b]
    if res is not None:
        in_specs.append(o_spec)
        args.append(res)
    return _matmul_call(args, in_specs, o_spec, jax.ShapeDtypeStruct((m, n), out_dtype),
                        (m // tm, n // tn, k // tk), mode, k // tk, (tm, tn), name=name, has_res=res is not None)


def _out_proj(o_attn, o_gla, w_out, x, *, tm, tn):
    t, ka = o_attn.shape
    kg = o_gla.shape[1]

    def body(a_ref, g_ref, w_ref, x_ref, o_ref):
        acc = lax.dot_general(a_ref[...], w_ref[:ka], _DIMS["nn"], preferred_element_type=F32)
        acc = acc + lax.dot_general(g_ref[...], w_ref[ka:], _DIMS["nn"], preferred_element_type=F32)
        o_ref[...] = acc + x_ref[...]

    tile = pl.BlockSpec((tm, tn), lambda i, j: (i, j))
    return pl.pallas_call(
        body, name="out_proj", grid=(t // tm, D_MODEL // tn),
        in_specs=[pl.BlockSpec((tm, ka), lambda i, j: (i, 0)), pl.BlockSpec((tm, kg), lambda i, j: (i, 0)),
                  pl.BlockSpec((ka + kg, tn), lambda i, j: (0, j)), tile],
        out_specs=tile, out_shape=jax.ShapeDtypeStruct((t, D_MODEL), F32),
        compiler_params=_params(("parallel", "parallel"), VMEM_LIMIT),
    )(o_attn, o_gla, w_out, x)


def _out_proj_dw(o_attn, o_gla, dx1, *, tn):
    t, ka = o_attn.shape
    assert o_gla.shape == (t, ka)

    def body(a_ref, g_ref, d_ref, o_ref):
        @pl.when(pl.program_id(0) == 0)
        def _():
            o_ref[...] = lax.dot_general(a_ref[...], d_ref[...], _DIMS["tn"], preferred_element_type=F32)

        @pl.when(pl.program_id(0) == 1)
        def _():
            o_ref[...] = lax.dot_general(g_ref[...], d_ref[...], _DIMS["tn"], preferred_element_type=F32)

    whole = pl.BlockSpec((t, ka), lambda i, j: (0, 0))
    return pl.pallas_call(
        body, name="dw_out", grid=(2, D_MODEL // tn),
        in_specs=[whole, whole, pl.BlockSpec((t, tn), lambda i, j: (0, j))],
        out_specs=pl.BlockSpec((ka, tn), lambda i, j: (i, j)),
        out_shape=jax.ShapeDtypeStruct((2 * ka, D_MODEL), F32),
        compiler_params=_params(("parallel", "parallel"), VMEM_LIMIT),
    )(o_attn, o_gla, dx1)


UP_BLOCK = 2 * D_FF // N_DEV


def _up_proj(h2, w_up8, *, tm):
    t = h2.shape[0]
    return _matmul_call(
        [h2, w_up8],
        [pl.BlockSpec((tm, D_MODEL), lambda i, j, kk: (i, 0)),
         pl.BlockSpec((None, D_MODEL, UP_BLOCK), lambda i, j, kk: (j, 0, 0))],
        pl.BlockSpec((None, tm, UP_BLOCK), lambda i, j, kk: (j // N_CHIP, i, j % N_CHIP)),
        jax.ShapeDtypeStruct((2, t, D_FF), F32), (t // tm, N_DEV, 1), "nn", 1, None, name="up_proj")


def _up_proj_dx(du, w_up8, *, tm, tn):
    t = du.shape[1]
    pair = 2
    return _matmul_call(
        [du, w_up8],
        [pl.BlockSpec((None, tm, pair * UP_BLOCK), lambda i, j, kk: (kk // 2, i, kk % 2)),
         pl.BlockSpec((pair, tn, UP_BLOCK), lambda i, j, kk: (kk, j, 0))],
        pl.BlockSpec((tm, tn), lambda i, j, kk: (i, j)),
        jax.ShapeDtypeStruct((t, D_MODEL), F32), (t // tm, D_MODEL // tn, N_DEV // pair), "nt", N_DEV // pair,
        (tm, tn), name="up_proj_dx", load_b=lambda ref: jnp.concatenate([ref[0], ref[1]], axis=1))


def _up_proj_dw(h2, du, *, tm, tk):
    t = h2.shape[0]
    return _matmul_call(
        [h2, du],
        [pl.BlockSpec((tk, tm), lambda j, i, kk: (kk, i)),
         pl.BlockSpec((None, tk, UP_BLOCK), lambda j, i, kk: (j // N_CHIP, kk, j % N_CHIP))],
        pl.BlockSpec((None, tm, UP_BLOCK), lambda j, i, kk: (j, i, 0)),
        jax.ShapeDtypeStruct((N_DEV, D_MODEL, UP_BLOCK), F32), (N_DEV, D_MODEL // tm, t // tk), "tn", t // tk,
        (tm, UP_BLOCK), name="up_proj_dw")


IN_TILE = 512


def _in_proj_dw(d_proj, h1, *, tn):
    t = h1.shape[0]
    table = []
    for tile in range(IN_MAIN // IN_TILE):
        dst, src, _ = max(s for s in SEGMENTS.values() if s[0] <= tile * IN_TILE)
        assert (src + tile * IN_TILE - dst) % IN_TILE == 0
        table.append((src + tile * IN_TILE - dst) // IN_TILE)
    assert sorted(table) == list(range(IN_MAIN // IN_TILE))
    return _matmul_call(
        [d_proj, h1],
        [pl.BlockSpec((t, IN_TILE), lambda j, i, kk, tab: (0, i)),
         pl.BlockSpec((t, tn), lambda j, i, kk, tab: (0, j))],
        pl.BlockSpec((IN_TILE, tn), lambda j, i, kk, tab: (tab[i], j)),
        jax.ShapeDtypeStruct((IN_TOTAL, D_MODEL), F32), (D_MODEL // tn, IN_MAIN // IN_TILE, 1), "tn", 1, None,
        name="in_proj_dw", prefetch=jnp.asarray(table, jnp.int32))


def _in_proj_dw_lr(dw_t, d_lr, h1):
    t = h1.shape[0]
    n_lr = IN_TOTAL - IN_MAIN
    tn = 512

    def body(dw_ref, dlr_ref, h1_ref, out_ref):
        full = lax.dot_general(dlr_ref[...], h1_ref[...], _DIMS["tn"], preferred_element_type=F32)
        out_ref[...] = full[:n_lr]

    return pl.pallas_call(
        body, name="in_proj_dw_lr", grid=(D_MODEL // tn,),
        in_specs=[pl.BlockSpec(memory_space=pl.ANY),
                  pl.BlockSpec((t, LR_PAD), lambda j: (0, 0)),
                  pl.BlockSpec((t, tn), lambda j: (0, j))],
        out_specs=pl.BlockSpec((n_lr, tn), lambda j: (IN_MAIN // n_lr, j)),
        out_shape=jax.ShapeDtypeStruct(dw_t.shape, F32),
        input_output_aliases={0: 0},
        compiler_params=_params(("parallel",), VMEM_LIMIT),
    )(dw_t, d_lr, h1)


def _rmsnorm_fwd(x, g, *, name, tr=512):
    t, d = x.shape

    def body(x_ref, g_ref, h_ref):
        h_ref[...] = _rms(x_ref[...], g_ref[...]).astype(BF16)

    return pl.pallas_call(
        body, name=name, grid=(t // tr,),
        in_specs=[pl.BlockSpec((tr, d), lambda i: (i, 0)), pl.BlockSpec((1, d), lambda i: (0, 0))],
        out_specs=pl.BlockSpec((tr, d), lambda i: (i, 0)),
        out_shape=jax.ShapeDtypeStruct((t, d), BF16),
        compiler_params=_params(("parallel",), VMEM_LIMIT),
    )(x, g)


def _rmsnorm_bwd(x, g, dh, dres, *, name, tr=512):
    t, d = x.shape

    def body(x_ref, g_ref, dh_ref, dres_ref, dx_ref, dxb_ref, dg_ref):
        _, vjp = jax.vjp(_rms, x_ref[...], g_ref[...])
        dx, dg = vjp(dh_ref[...])
        dx = dx + dres_ref[...]
        dx_ref[...] = dx
        dxb_ref[...] = dx.astype(BF16)

        @pl.when(pl.program_id(0) == 0)
        def _():
            dg_ref[...] = jnp.zeros_like(dg_ref)

        dg_ref[...] += dg

    row = pl.BlockSpec((tr, d), lambda i: (i, 0))
    vec = pl.BlockSpec((1, d), lambda i: (0, 0))
    return pl.pallas_call(
        body, name=name, grid=(t // tr,),
        in_specs=[row, vec, row, row],
        out_specs=[row, row, vec],
        out_shape=[jax.ShapeDtypeStruct((t, d), F32), jax.ShapeDtypeStruct((t, d), BF16),
                   jax.ShapeDtypeStruct((1, d), F32)],
        compiler_params=_params(("arbitrary",), VMEM_LIMIT),
    )(x, g, dh, dres)


def _seg_block(name, width):
    off = SEGMENTS[name][0]
    assert off % width == 0
    return off // width


def _attn_prep_fwd(proj, cos, sin_signed, gq, gk, *, tr=512):
    t = proj.shape[0]

    def body(q_ref, k_ref, v_ref, cos_ref, sin_ref, gq_ref, gk_ref, qo_ref, ko_ref, vo_ref):
        cos_t, sin_t = cos_ref[...], sin_ref[...]
        for h in range(ATTN_HEADS):
            cols = slice(h * HEAD_DIM, (h + 1) * HEAD_DIM)
            qo_ref[:, cols] = _rope(_rms(q_ref[:, cols], gq_ref[...]), cos_t, sin_t).astype(BF16)
        for h in range(KV_HEADS):
            cols = slice(h * HEAD_DIM, (h + 1) * HEAD_DIM)
            ko_ref[:, cols] = _rope(_rms(k_ref[:, cols], gk_ref[...]), cos_t, sin_t).astype(BF16)
        vo_ref[...] = v_ref[...].astype(BF16)

    qb, kb, vb = _seg_block("qa", ATTN_WIDTH), _seg_block("ka", KV_WIDTH), _seg_block("va", KV_WIDTH)
    tab = pl.BlockSpec((tr, HEAD_DIM), lambda i: (i, 0))
    vec = pl.BlockSpec((1, HEAD_DIM), lambda i: (0, 0))
    return pl.pallas_call(
        body, name="attn_prep_fwd", grid=(t // tr,),
        in_specs=[pl.BlockSpec((tr, ATTN_WIDTH), lambda i: (i, qb)),
                  pl.BlockSpec((tr, KV_WIDTH), lambda i: (i, kb)),
                  pl.BlockSpec((tr, KV_WIDTH), lambda i: (i, vb)),
                  tab, tab, vec, vec],
        out_specs=[pl.BlockSpec((tr, ATTN_WIDTH), lambda i: (i, 0)),
                   pl.BlockSpec((tr, KV_WIDTH), lambda i: (i, 0)),
                   pl.BlockSpec((tr, KV_WIDTH), lambda i: (i, 0))],
        out_shape=[jax.ShapeDtypeStruct((t, ATTN_WIDTH), BF16),
                   jax.ShapeDtypeStruct((t, KV_WIDTH), BF16),
                   jax.ShapeDtypeStruct((t, KV_WIDTH), BF16)],
        compiler_params=_params(("parallel",), VMEM_LIMIT),
    )(proj, proj, proj, cos, sin_signed, gq, gk)


def _attn_heads(q, kcat, vcat, sink_col, valid):
    s = bdot(q, kcat, "nt") * (HEAD_DIM ** -0.5)
    s = jnp.where(valid, s, -jnp.inf)
    m = lax.stop_gradient(jnp.maximum(jnp.max(s, axis=-1, keepdims=True), sink_col))
    p = jnp.exp(s - m)
    p = p / (jnp.sum(p, axis=-1, keepdims=True) + jnp.exp(sink_col - m))
    return bdot(p, vcat, "nn")


def _attn_valid(n, t):
    shape = (GQA_GROUP * ATTN_BLOCK, 3 * ATTN_BLOCK)
    qi = lax.broadcasted_iota(jnp.int32, shape, 0) % ATTN_BLOCK
    sj = lax.broadcasted_iota(jnp.int32, shape, 1)
    kpos = n * ATTN_BLOCK - ATTN_BLOCK + sj
    return (jnp.abs(sj - ATTN_BLOCK - qi) <= WINDOW) & (kpos >= 0) & (kpos < t)


def _head_rows(g):
    return slice(g * ATTN_BLOCK, (g + 1) * ATTN_BLOCK)


def _head_cols(g):
    return slice(g * HEAD_DIM, (g + 1) * HEAD_DIM)


def _stack_heads(ref):
    return jnp.concatenate([ref[:, _head_cols(g)] for g in range(GQA_GROUP)], axis=0).astype(F32)


def _sink_column(sink_ref, h):
    return jnp.concatenate([jnp.full((ATTN_BLOCK, 1), sink_ref[h * GQA_GROUP + g], F32)
                            for g in range(GQA_GROUP)], axis=0)


def _attn_specs(nb):
    q_spec = pl.BlockSpec((ATTN_BLOCK, GQA_GROUP * HEAD_DIM), lambda h, n: (n, h))
    kv_specs = [
        pl.BlockSpec((ATTN_BLOCK, HEAD_DIM), lambda h, n: (jnp.maximum(n - 1, 0), h)),
        pl.BlockSpec((ATTN_BLOCK, HEAD_DIM), lambda h, n: (n, h)),
        pl.BlockSpec((ATTN_BLOCK, HEAD_DIM), lambda h, n: (jnp.minimum(n + 1, nb - 1), h)),
    ]
    return q_spec, kv_specs


def _attn_fwd(q, k, v, sink):
    t = q.shape[0]
    nb = t // ATTN_BLOCK

    def body(sink_ref, q_ref, kp_ref, kc_ref, kn_ref, vp_ref, vc_ref, vn_ref, o_ref):
        valid = _attn_valid(pl.program_id(0), t)
        args = []
        for h in range(KV_HEADS):
            q = jnp.concatenate([q_ref[:, _head_cols(h * GQA_GROUP + g)] for g in range(GQA_GROUP)], axis=0)
            kcat = jnp.concatenate([r[:, _head_cols(h)] for r in (kp_ref, kc_ref, kn_ref)], axis=0)
            vcat = jnp.concatenate([r[:, _head_cols(h)] for r in (vp_ref, vc_ref, vn_ref)], axis=0)
            args.append((q.astype(F32), kcat.astype(F32), vcat.astype(F32), _sink_column(sink_ref, h)))
        outs = [_attn_heads(*a, valid=valid).astype(BF16) for a in args]
        for h, o in enumerate(outs):
            for g in range(GQA_GROUP):
                o_ref[:, _head_cols(h * GQA_GROUP + g)] = o[_head_rows(g)]

    q_spec = pl.BlockSpec((ATTN_BLOCK, ATTN_WIDTH), lambda n: (n, 0))
    kv_specs = [pl.BlockSpec((ATTN_BLOCK, KV_WIDTH), lambda n: (jnp.maximum(n - 1, 0), 0)),
                pl.BlockSpec((ATTN_BLOCK, KV_WIDTH), lambda n: (n, 0)),
                pl.BlockSpec((ATTN_BLOCK, KV_WIDTH), lambda n: (jnp.minimum(n + 1, nb - 1), 0))]
    return pl.pallas_call(
        body, name="attn_fwd", grid=(nb,),
        in_specs=[pl.BlockSpec(memory_space=pltpu.SMEM), q_spec] + kv_specs + kv_specs,
        out_specs=q_spec,
        out_shape=jax.ShapeDtypeStruct((t, ATTN_WIDTH), BF16),
        compiler_params=_params(("parallel",), VMEM_LIMIT),
    )(sink, q, k, k, k, v, v, v)


def _attn_bwd(q, k, v, sink, dmix):
    t = q.shape[0]
    nb = t // ATTN_BLOCK

    def body(sink_ref, q_ref, kp_ref, kc_ref, kn_ref, vp_ref, vc_ref, vn_ref, do_ref,
             dq_ref, dk_lo, dk_mid, dk_hi, dv_lo, dv_mid, dv_hi, dsink_ref):
        h, n = pl.program_id(0), pl.program_id(1)
        valid = _attn_valid(n, t)
        kcat = jnp.concatenate([kp_ref[...], kc_ref[...], kn_ref[...]], axis=0).astype(F32)
        vcat = jnp.concatenate([vp_ref[...], vc_ref[...], vn_ref[...]], axis=0).astype(F32)
        _, vjp = jax.vjp(functools.partial(_attn_heads, valid=valid),
                         _stack_heads(q_ref), kcat, vcat, _sink_column(sink_ref, h))
        dq, dk, dv, dsink_col = vjp(_stack_heads(do_ref))
        row = lax.broadcasted_iota(jnp.int32, (8, HEAD_DIM), 0)
        dsink = jnp.zeros((8, HEAD_DIM), F32)
        for g in range(GQA_GROUP):
            dq_ref[:, _head_cols(g)] = dq[_head_rows(g)]
            dsink = dsink + jnp.where(row == g, jnp.sum(dsink_col[_head_rows(g)]), 0.0)
        for i, (dk_ref, dv_ref) in enumerate(((dk_lo, dv_lo), (dk_mid, dv_mid), (dk_hi, dv_hi))):
            rows = slice(i * ATTN_BLOCK, (i + 1) * ATTN_BLOCK)
            dk_ref[...] = dk[rows]
            dv_ref[...] = dv[rows]

        @pl.when(n == 0)
        def _():
            dsink_ref[...] = jnp.zeros_like(dsink_ref)

        dsink_ref[...] += dsink

    q_spec, kv_specs = _attn_specs(nb)
    kv_out = pl.BlockSpec((ATTN_BLOCK, HEAD_DIM), lambda h, n: (n, h))
    kv_shape = jax.ShapeDtypeStruct((t, KV_WIDTH), F32)
    return pl.pallas_call(
        body, name="attn_bwd", grid=(KV_HEADS, nb),
        in_specs=[pl.BlockSpec(memory_space=pltpu.SMEM), q_spec] + kv_specs + kv_specs + [q_spec],
        out_specs=[q_spec] + [kv_out] * 6 + [pl.BlockSpec((None, 8, HEAD_DIM), lambda h, n: (h, 0, 0))],
        out_shape=[jax.ShapeDtypeStruct((t, ATTN_WIDTH), F32)] + [kv_shape] * 6
                  + [jax.ShapeDtypeStruct((KV_HEADS, 8, HEAD_DIM), F32)],
        compiler_params=_params(("parallel", "arbitrary"), VMEM_LIMIT),
    )(sink, q, k, k, k, v, v, v, dmix)


def _attn_prep_bwd(proj, cos, sin_signed, gq, gk, dq, dks, dvs):
    t = proj.shape[0]
    tr = ATTN_BLOCK
    nb = t // tr

    def body(q_ref, k_ref, cos_ref, sin_ref, gq_ref, gk_ref, dq_ref,
             dk_lo, dk_mid, dk_hi, dv_lo, dv_mid, dv_hi,
             dqo_ref, dko_ref, dvo_ref, dgq_ref, dgk_ref):
        n = pl.program_id(0)
        cos_t, sin_t = cos_ref[...], sin_ref[...]
        has_next = (n < nb - 1).astype(F32)
        has_prev = (n > 0).astype(F32)
        dk = dk_lo[...] * has_next + dk_mid[...] + dk_hi[...] * has_prev
        dv = dv_lo[...] * has_next + dv_mid[...] + dv_hi[...] * has_prev
        dvo_ref[...] = dv.astype(BF16)
        dgq = jnp.zeros((1, HEAD_DIM), F32)
        dgk = jnp.zeros((1, HEAD_DIM), F32)
        for h in range(ATTN_HEADS):
            cols = slice(h * HEAD_DIM, (h + 1) * HEAD_DIM)
            _, vjp = jax.vjp(_rms, q_ref[:, cols], gq_ref[...])
            dx, dg = vjp(_rope_transposed(dq_ref[:, cols], cos_t, sin_t))
            dqo_ref[:, cols] = dx.astype(BF16)
            dgq = dgq + dg
        for h in range(KV_HEADS):
            cols = slice(h * HEAD_DIM, (h + 1) * HEAD_DIM)
            _, vjp = jax.vjp(_rms, k_ref[:, cols], gk_ref[...])
            dx, dg = vjp(_rope_transposed(dk[:, cols], cos_t, sin_t))
            dko_ref[:, cols] = dx.astype(BF16)
            dgk = dgk + dg

        @pl.when(n == 0)
        def _():
            dgq_ref[...] = jnp.zeros_like(dgq_ref)
            dgk_ref[...] = jnp.zeros_like(dgk_ref)

        dgq_ref[...] += dgq
        dgk_ref[...] += dgk

    qb, kb = _seg_block("qa", ATTN_WIDTH), _seg_block("ka", KV_WIDTH)
    tab = pl.BlockSpec((tr, HEAD_DIM), lambda i: (i, 0))
    vec = pl.BlockSpec((1, HEAD_DIM), lambda i: (0, 0))
    kv = [pl.BlockSpec((tr, KV_WIDTH), lambda i: (jnp.minimum(i + 1, nb - 1), 0)),
          pl.BlockSpec((tr, KV_WIDTH), lambda i: (i, 0)),
          pl.BlockSpec((tr, KV_WIDTH), lambda i: (jnp.maximum(i - 1, 0), 0))]
    wide = pl.BlockSpec((tr, ATTN_WIDTH), lambda i: (i, 0))
    narrow = pl.BlockSpec((tr, KV_WIDTH), lambda i: (i, 0))
    return pl.pallas_call(
        body, name="attn_prep_bwd", grid=(nb,),
        in_specs=[pl.BlockSpec((tr, ATTN_WIDTH), lambda i: (i, qb)),
                  pl.BlockSpec((tr, KV_WIDTH), lambda i: (i, kb)),
                  tab, tab, vec, vec, wide] + kv + kv,
        out_specs=[wide, narrow, narrow, vec, vec],
        out_shape=[jax.ShapeDtypeStruct((t, ATTN_WIDTH), BF16),
                   jax.ShapeDtypeStruct((t, KV_WIDTH), BF16),
                   jax.ShapeDtypeStruct((t, KV_WIDTH), BF16),
                   jax.ShapeDtypeStruct((1, HEAD_DIM), F32),
                   jax.ShapeDtypeStruct((1, HEAD_DIM), F32)],
        compiler_params=_params(("arbitrary",), VMEM_LIMIT),
    )(proj, proj, cos, sin_signed, gq, gk, dq, *dks, *dvs)


def _decay_fn(lr, w2, ba):
    return _log_sigmoid(bdot(lr, w2, "nn") + ba) / GLA_GATE_NORMALIZER


def _gla_prep_fwd(proj_lr, w2, ba2, *, tr=512):
    t = proj_lr.shape[0]
    width = 2 * GLA_KEY_WIDTH

    def body(lr_ref, w2_ref, ba_ref, g_ref):
        g_ref[...] = _decay_fn(lr_ref[...], w2_ref[...], ba_ref[...])

    return pl.pallas_call(
        body, name="gla_prep_fwd", grid=(t // tr,),
        in_specs=[pl.BlockSpec((tr, LR_PAD), lambda i: (i, 0)),
                  pl.BlockSpec((LR_PAD, width), lambda i: (0, 0)),
                  pl.BlockSpec((1, width), lambda i: (0, 0))],
        out_specs=pl.BlockSpec((tr, width), lambda i: (i, 0)),
        out_shape=jax.ShapeDtypeStruct((t, width), F32),
        compiler_params=_params(("parallel",), VMEM_LIMIT),
    )(proj_lr, w2, ba2)


def _gla_prep_bwd(proj_lr, w2, ba2, dg_f, dg_b, *, tr=512):
    t = proj_lr.shape[0]
    width = 2 * GLA_KEY_WIDTH

    def body(lr_ref, w2_ref, ba_ref, dgf_ref, dgb_ref, dlr_ref, dw2_ref, dba_ref):
        _, vjp = jax.vjp(_decay_fn, lr_ref[...], w2_ref[...], ba_ref[...])
        dlr, dw2, dba = vjp(jnp.concatenate([dgf_ref[...], dgb_ref[...]], axis=1))
        dlr_ref[...] = dlr.astype(BF16)

        @pl.when(pl.program_id(0) == 0)
        def _():
            dw2_ref[...] = jnp.zeros_like(dw2_ref)
            dba_ref[...] = jnp.zeros_like(dba_ref)

        dw2_ref[...] += dw2
        dba_ref[...] += dba

    half = pl.BlockSpec((tr, GLA_KEY_WIDTH), lambda i: (i, 0))
    return pl.pallas_call(
        body, name="gla_prep_bwd", grid=(t // tr,),
        in_specs=[pl.BlockSpec((tr, LR_PAD), lambda i: (i, 0)),
                  pl.BlockSpec((LR_PAD, width), lambda i: (0, 0)),
                  pl.BlockSpec((1, width), lambda i: (0, 0)), half, half],
        out_specs=[pl.BlockSpec((tr, LR_PAD), lambda i: (i, 0)),
                   pl.BlockSpec((LR_PAD, width), lambda i: (0, 0)),
                   pl.BlockSpec((1, width), lambda i: (0, 0))],
        out_shape=[jax.ShapeDtypeStruct((t, LR_PAD), BF16),
                   jax.ShapeDtypeStruct((LR_PAD, width), F32),
                   jax.ShapeDtypeStruct((1, width), F32)],
        compiler_params=_params(("arbitrary",), VMEM_LIMIT),
    )(proj_lr, w2, ba2, dg_f, dg_b)


def _gla_k(h):
    return slice(h * GLA_DK, (h + 1) * GLA_DK)


def _gla_v(h):
    return slice(h * GLA_DV, (h + 1) * GLA_DV)


def _running_sum(x, downward):
    n = x.shape[0]
    row = lax.broadcasted_iota(jnp.int32, x.shape, 0)
    step = 1
    while step < n:
        if downward:
            x = x + jnp.where(row >= step, pltpu.roll(x, step, 0), 0.0)
        else:
            x = x + jnp.where(row < n - step, pltpu.roll(x, n - step, 0), 0.0)
        step *= 2
    return x


@functools.partial(jax.custom_vjp, nondiff_argnums=(1,))
def _cumsum_rows(x, downward):
    return _running_sum(x, downward)


def _cumsum_rows_fwd(x, downward):
    return _running_sum(x, downward), None


def _cumsum_rows_bwd(downward, _, ct):
    return (_running_sum(ct, not downward),)


_cumsum_rows.defvjp(_cumsum_rows_fwd, _cumsum_rows_bwd)


def _gla_chunk(q, k, v, g, state, forward):
    c = GLA_CHUNK
    row = lax.broadcasted_iota(jnp.int32, (c, c), 0)
    col = lax.broadcasted_iota(jnp.int32, (c, c), 1)
    rid = lax.broadcasted_iota(jnp.int32, (c, GLA_DK), 0)
    q = q * (GLA_DK ** -0.5)
    if forward:
        see = row >= col
        upto_ref = rid <= c // 2
    else:
        see = row < col
        upto_ref = rid >= c - 1 - c // 2
    b = _cumsum_rows(g, forward)
    b_last = jnp.sum(g, axis=0, keepdims=True)
    b_ref = jnp.sum(jnp.where(upto_ref, g, 0.0), axis=0, keepdims=True)
    a = bdot(q * jnp.exp(b - b_ref), k * jnp.exp(b_ref - b), "nt")
    a = jnp.where(see, a, 0.0)
    o = bdot(a, v, "nn") + bdot(q * jnp.exp(b), state, "nt")
    new_state = state * jnp.exp(b_last) + bdot(v, k * jnp.exp(b_last - b), "tn")
    return o, new_state


def _gla_fwd(proj, g):
    t = proj.shape[0]
    c, per = GLA_CHUNK, GLA_PER_STEP
    nchunk = t // c
    nstep = nchunk // per
    qb, kb, vb = _seg_block("qg", GLA_KEY_WIDTH), _seg_block("kg", GLA_KEY_WIDTH), _seg_block("vg", GLA_WIDTH)

    def body(qf, kf, vf, gf, qr, kr, vr, gr, of_ref, ob_ref, sf_ref, sb_ref, state):
        @pl.when(pl.program_id(0) == 0)
        def _():
            state[...] = jnp.zeros_like(state)

        dirs = ((qf, kf, vf, gf, of_ref, sf_ref), (qr, kr, vr, gr, ob_ref, sb_ref))
        states = [[state[d, h] for h in range(GLA_HEADS)] for d in range(2)]
        for turn in range(per):
            chunk = (turn, per - 1 - turn)
            args = []
            for d, (q_ref, k_ref, v_ref, g_ref, _, _) in enumerate(dirs):
                rows = pl.ds(chunk[d] * c, c)
                args += [(q_ref[rows, _gla_k(h)], k_ref[rows, _gla_k(h)], v_ref[rows, _gla_v(h)],
                          g_ref[rows, _gla_k(h)], states[d][h]) for h in range(GLA_HEADS)]
            results = [_gla_chunk(*a, forward=(i < GLA_HEADS)) for i, a in enumerate(args)]
            for i, (a, (o, s_out)) in enumerate(zip(args, results)):
                d, h = divmod(i, GLA_HEADS)
                dirs[d][5][chunk[d], h] = a[4]
                dirs[d][4][pl.ds(chunk[d] * c, c), _gla_v(h)] = o
                states[d][h] = s_out
        for d in range(2):
            for h in range(GLA_HEADS):
                state[d, h] = states[d][h]

    specs, outs = [], []
    for d in range(2):
        ci = (lambda i: i) if d == 0 else (lambda i: nstep - 1 - i)
        specs += [pl.BlockSpec((per * c, GLA_KEY_WIDTH), lambda i, ci=ci: (ci(i), qb)),
                  pl.BlockSpec((per * c, GLA_KEY_WIDTH), lambda i, ci=ci: (ci(i), kb)),
                  pl.BlockSpec((per * c, GLA_WIDTH), lambda i, ci=ci: (ci(i), vb)),
                  pl.BlockSpec((per * c, GLA_KEY_WIDTH), lambda i, ci=ci, d=d: (ci(i), d))]
        outs.append(pl.BlockSpec((per * c, GLA_WIDTH), lambda i, ci=ci: (ci(i), 0)))
    for d in range(2):
        ci = (lambda i: i) if d == 0 else (lambda i: nstep - 1 - i)
        outs.append(pl.BlockSpec((per, GLA_HEADS, GLA_DV, GLA_DK), lambda i, ci=ci: (ci(i), 0, 0, 0)))
    o_shape = jax.ShapeDtypeStruct((t, GLA_WIDTH), F32)
    s_shape = jax.ShapeDtypeStruct((nchunk, GLA_HEADS, GLA_DV, GLA_DK), F32)
    return pl.pallas_call(
        body, name="gla_fwd", grid=(nstep,),
        in_specs=specs, out_specs=outs,
        out_shape=[o_shape, o_shape, s_shape, s_shape],
        scratch_shapes=[pltpu.VMEM((2, GLA_HEADS, GLA_DV, GLA_DK), F32)],
        compiler_params=_params(("arbitrary",), VMEM_LIMIT),
    )(proj, proj, proj, g, proj, proj, proj, g)


def _gla_bwd(proj, g, s_f, s_b, do):
    t = proj.shape[0]
    c, per = GLA_CHUNK, GLA_PER_STEP
    nchunk = t // c
    nstep = nchunk // per
    qb, kb, vb = _seg_block("qg", GLA_KEY_WIDTH), _seg_block("kg", GLA_KEY_WIDTH), _seg_block("vg", GLA_WIDTH)

    def body(*refs):
        ins, outs, dstate = refs[:12], refs[12:20], refs[20]

        @pl.when(pl.program_id(0) == 0)
        def _():
            dstate[...] = jnp.zeros_like(dstate)

        dstates = [[dstate[d, h] for h in range(GLA_HEADS)] for d in range(2)]
        for turn in range(per):
            chunk = (per - 1 - turn, turn)
            loaded = []
            for d in range(2):
                q_ref, k_ref, v_ref, g_ref, s_ref, do_ref = ins[6 * d:6 * d + 6]
                rows = pl.ds(chunk[d] * c, c)
                for h in range(GLA_HEADS):
                    loaded.append(((q_ref[rows, _gla_k(h)], k_ref[rows, _gla_k(h)], v_ref[rows, _gla_v(h)],
                                    g_ref[rows, _gla_k(h)], s_ref[chunk[d], h]),
                                   (do_ref[rows, _gla_v(h)], dstates[d][h])))
            grads = []
            for i, (primals, cotangents) in enumerate(loaded):
                _, vjp = jax.vjp(functools.partial(_gla_chunk, forward=(i < GLA_HEADS)), *primals)
                grads.append(vjp(cotangents))
            for i, (dq, dk, dv, dg, ds) in enumerate(grads):
                d, h = divmod(i, GLA_HEADS)
                rows = pl.ds(chunk[d] * c, c)
                dq_ref, dk_ref, dv_ref, dg_ref = outs[4 * d:4 * d + 4]
                dq_ref[rows, _gla_k(h)] = dq
                dk_ref[rows, _gla_k(h)] = dk
                dv_ref[rows, _gla_v(h)] = dv
                dg_ref[rows, _gla_k(h)] = dg
                dstates[d][h] = ds
        for d in range(2):
            for h in range(GLA_HEADS):
                dstate[d, h] = dstates[d][h]

    specs, outs, shapes = [], [], []
    for d in range(2):
        ci = (lambda i: nstep - 1 - i) if d == 0 else (lambda i: i)
        specs += [pl.BlockSpec((per * c, GLA_KEY_WIDTH), lambda i, ci=ci: (ci(i), qb)),
                  pl.BlockSpec((per * c, GLA_KEY_WIDTH), lambda i, ci=ci: (ci(i), kb)),
                  pl.BlockSpec((per * c, GLA_WIDTH), lambda i, ci=ci: (ci(i), vb)),
                  pl.BlockSpec((per * c, GLA_KEY_WIDTH), lambda i, ci=ci, d=d: (ci(i), d)),
                  pl.BlockSpec((per, GLA_HEADS, GLA_DV, GLA_DK), lambda i, ci=ci: (ci(i), 0, 0, 0)),
                  pl.BlockSpec((per * c, GLA_WIDTH), lambda i, ci=ci: (ci(i), 0))]
        key = pl.BlockSpec((per * c, GLA_KEY_WIDTH), lambda i, ci=ci: (ci(i), 0))
        val = pl.BlockSpec((per * c, GLA_WIDTH), lambda i, ci=ci: (ci(i), 0))
        outs += [key, key, val, key]
        shapes += [jax.ShapeDtypeStruct((t, GLA_KEY_WIDTH), F32), jax.ShapeDtypeStruct((t, GLA_KEY_WIDTH), F32),
                   jax.ShapeDtypeStruct((t, GLA_WIDTH), F32), jax.ShapeDtypeStruct((t, GLA_KEY_WIDTH), F32)]
    return pl.pallas_call(
        body, name="gla_bwd", grid=(nstep,),
        in_specs=specs, out_specs=outs, out_shape=shapes,
        scratch_shapes=[pltpu.VMEM((2, GLA_HEADS, GLA_DV, GLA_DK), F32)],
        compiler_params=_params(("arbitrary",), VMEM_LIMIT),
    )(proj, proj, proj, g, s_f, do, proj, proj, proj, g, s_b, do)


def _gla_out_head(o_f, o_b, gate, gn):
    return _rms(o_f + o_b, gn) * _silu(gate)


def _gla_out_fwd(o_f, o_b, proj, gn, *, tr=512):
    t = o_f.shape[0]
    gb = _seg_block("gate", GLA_WIDTH)

    def body(of_ref, ob_ref, gate_ref, gn_ref, out_ref):
        for h in range(GLA_HEADS):
            vc = slice(h * GLA_DV, (h + 1) * GLA_DV)
            out_ref[:, vc] = _gla_out_head(of_ref[:, vc], ob_ref[:, vc], gate_ref[:, vc], gn_ref[...]).astype(BF16)

    wide = pl.BlockSpec((tr, GLA_WIDTH), lambda i: (i, 0))
    return pl.pallas_call(
        body, name="gla_out_fwd", grid=(t // tr,),
        in_specs=[wide, wide, pl.BlockSpec((tr, GLA_WIDTH), lambda i: (i, gb)),
                  pl.BlockSpec((1, GLA_DV), lambda i: (0, 0))],
        out_specs=wide,
        out_shape=jax.ShapeDtypeStruct((t, GLA_WIDTH), BF16),
        compiler_params=_params(("parallel",), VMEM_LIMIT),
    )(o_f, o_b, proj, gn)


def _gla_out_bwd(o_f, o_b, proj, gn, dmix, *, tr=512):
    t = o_f.shape[0]
    gb = _seg_block("gate", GLA_WIDTH)

    def body(of_ref, ob_ref, gate_ref, gn_ref, dout_ref, do_ref, dgate_ref, dgn_ref):
        dgn = jnp.zeros((1, GLA_DV), F32)
        for h in range(GLA_HEADS):
            vc = slice(h * GLA_DV, (h + 1) * GLA_DV)
            _, vjp = jax.vjp(_gla_out_head, of_ref[:, vc], ob_ref[:, vc], gate_ref[:, vc], gn_ref[...])
            do, _, dgate, dg = vjp(dout_ref[:, vc])
            do_ref[:, vc] = do
            dgate_ref[:, vc] = dgate.astype(BF16)
            dgn = dgn + dg

        @pl.when(pl.program_id(0) == 0)
        def _():
            dgn_ref[...] = jnp.zeros_like(dgn_ref)

        dgn_ref[...] += dgn

    wide = pl.BlockSpec((tr, GLA_WIDTH), lambda i: (i, 0))
    vec = pl.BlockSpec((1, GLA_DV), lambda i: (0, 0))
    return pl.pallas_call(
        body, name="gla_out_bwd", grid=(t // tr,),
        in_specs=[wide, wide, pl.BlockSpec((tr, GLA_WIDTH), lambda i: (i, gb)), vec,
                  pl.BlockSpec((tr, GLA_WIDTH), lambda i: (i, 1))],
        out_specs=[wide, wide, vec],
        out_shape=[jax.ShapeDtypeStruct((t, GLA_WIDTH), F32), jax.ShapeDtypeStruct((t, GLA_WIDTH), BF16),
                   jax.ShapeDtypeStruct((1, GLA_DV), F32)],
        compiler_params=_params(("arbitrary",), VMEM_LIMIT),
    )(o_f, o_b, proj, gn, dmix)


CONV_TR = 1024
CONV_TC = 512
HALO = 8
HALO16 = 16


def _conv3(u, w, b):
    n = u.shape[0]
    return pltpu.roll(u, 1, 0) * w[0:1] + u * w[1:2] + pltpu.roll(u, n - 1, 0) * w[2:3] + b


def _conv_ext(main_ref, prev_ref, next_ref, r, nr):
    prev = prev_ref[...].astype(F32)[-HALO:] * (r > 0).astype(F32)
    nxt = next_ref[...].astype(F32)[:HALO] * (r < nr - 1).astype(F32)
    return jnp.concatenate([prev, main_ref[...].astype(F32), nxt], axis=0)


def _conv_specs(t, halo, half=None):
    per = CONV_TR // halo
    last = t // halo - 1
    lead = () if half is None else (None,)
    at = (lambda *ix: ix) if half is None else (lambda *ix: (half,) + ix)
    return [pl.BlockSpec(lead + (CONV_TR, CONV_TC), lambda j, r: at(r, j)),
            pl.BlockSpec(lead + (halo, CONV_TC), lambda j, r: at(jnp.maximum(r * per - 1, 0), j)),
            pl.BlockSpec(lead + (halo, CONV_TC), lambda j, r: at(jnp.minimum((r + 1) * per, last), j))]


def _ffn_mid_fwd(u, cw_g, cw_v, cb_g, cb_v):
    _, t, f = u.shape
    nr = t // CONV_TR

    def body(ug, ugp, ugn, uv, uvp, uvn, wg, wv, bg, bv, a_ref):
        r = pl.program_id(1)
        gate = _conv3(_conv_ext(ug, ugp, ugn, r, nr), wg[...], bg[...])[HALO:HALO + CONV_TR]
        val = _conv3(_conv_ext(uv, uvp, uvn, r, nr), wv[...], bv[...])[HALO:HALO + CONV_TR]
        a_ref[...] = (_silu(gate) * val).astype(BF16)

    w_spec = pl.BlockSpec((3, CONV_TC), lambda j, r: (0, j))
    b_spec = pl.BlockSpec((1, CONV_TC), lambda j, r: (0, j))
    return pl.pallas_call(
        body, name="ffn_mid_fwd", grid=(f // CONV_TC, nr),
        in_specs=_conv_specs(t, HALO, 0) + _conv_specs(t, HALO, 1) + [w_spec, w_spec, b_spec, b_spec],
        out_specs=pl.BlockSpec((CONV_TR, CONV_TC), lambda j, r: (r, j)),
        out_shape=jax.ShapeDtypeStruct((t, f), BF16),
        compiler_params=_params(("parallel", "parallel"), VMEM_LIMIT),
    )(u, u, u, u, u, u, cw_g, cw_v, cb_g, cb_v)


def _ffn_mid_bwd(u, cw_g, cw_v, cb_g, cb_v, da):
    _, t, f = u.shape
    nr = t // CONV_TR
    ext = CONV_TR + 2 * HALO

    def body(ug, ugp, ugn, uv, uvp, uvn, dam, dap, dan, wg, wv, bg, bv,
             du_ref, dwg_ref, dwv_ref, dbg_ref, dbv_ref):
        r = pl.program_id(1)
        shifted = []
        for main, prev, nxt in ((ug, ugp, ugn), (uv, uvp, uvn)):
            x = _conv_ext(main, prev, nxt, r, nr)
            shifted.append((pltpu.roll(x, 1, 0), x, pltpu.roll(x, ext - 1, 0)))
        da_x = _conv_ext(dam, dap, dan, r, nr)
        wg_t, wv_t = wg[...], wv[...]
        gate = shifted[0][0] * wg_t[0:1] + shifted[0][1] * wg_t[1:2] + shifted[0][2] * wg_t[2:3] + bg[...]
        val = shifted[1][0] * wv_t[0:1] + shifted[1][1] * wv_t[1:2] + shifted[1][2] * wv_t[2:3] + bv[...]
        sig = jax.nn.sigmoid(gate)
        silu = gate * sig
        d_val = da_x * silu
        d_gate = da_x * val * (sig + silu * (1.0 - sig))
        own = slice(HALO, HALO + CONV_TR)
        for half, (xs3, d, wt, dw_ref, db_ref) in enumerate(((shifted[0], d_gate, wg_t, dwg_ref, dbg_ref),
                                                            (shifted[1], d_val, wv_t, dwv_ref, dbv_ref))):
            du = pltpu.roll(d, ext - 1, 0) * wt[0:1] + d * wt[1:2] + pltpu.roll(d, 1, 0) * wt[2:3]
            du_ref[half] = du[own].astype(BF16)
            d_own = d[own]
            dw = jnp.concatenate([jnp.sum(x[own] * d_own, axis=0, keepdims=True) for x in xs3], axis=0)
            db = jnp.sum(d_own, axis=0, keepdims=True)

            @pl.when(r == 0)
            def _():
                dw_ref[...] = jnp.zeros_like(dw_ref)
                db_ref[...] = jnp.zeros_like(db_ref)

            dw_ref[...] += dw
            db_ref[...] += db

    w_spec = pl.BlockSpec((3, CONV_TC), lambda j, r: (0, j))
    b_spec = pl.BlockSpec((1, CONV_TC), lambda j, r: (0, j))
    return pl.pallas_call(
        body, name="ffn_mid_bwd", grid=(f // CONV_TC, nr),
        in_specs=(_conv_specs(t, HALO, 0) + _conv_specs(t, HALO, 1) + _conv_specs(t, HALO16)
                  + [w_spec, w_spec, b_spec, b_spec]),
        out_specs=[pl.BlockSpec((2, CONV_TR, CONV_TC), lambda j, r: (0, r, j)), w_spec, w_spec, b_spec, b_spec],
        out_shape=[jax.ShapeDtypeStruct((2, t, f), BF16),
                   jax.ShapeDtypeStruct((3, f), F32), jax.ShapeDtypeStruct((3, f), F32),
                   jax.ShapeDtypeStruct((1, f), F32), jax.ShapeDtypeStruct((1, f), F32)],
        compiler_params=_params(("parallel", "arbitrary"), VMEM_LIMIT),
    )(u, u, u, u, u, u, da, da, da, cw_g, cw_v, cb_g, cb_v)


def _down_proj_loss(act, w_down, x1, target, *, tm, tn):
    t, f = act.shape
    d = w_down.shape[1]

    def body(a_ref, w_ref, x_ref, t_ref, loss_ref, dy_ref, dyb_ref):
        y = lax.dot_general(a_ref[...], w_ref[...], _DIMS["nn"], preferred_element_type=F32) + x_ref[...]
        err = y - t_ref[...]
        dy = err * (1.0 / d)
        dy_ref[...] = dy
        dyb_ref[...] = dy.astype(BF16)
        part = 0.5 * jnp.sum(jnp.sum(err * err, axis=-1, keepdims=True) * (1.0 / d), axis=0, keepdims=True)

        @pl.when((pl.program_id(0) == 0) & (pl.program_id(1) == 0))
        def _():
            loss_ref[...] = jnp.zeros_like(loss_ref)

        loss_ref[...] += jnp.broadcast_to(part, loss_ref.shape)

    tile = pl.BlockSpec((tm, tn), lambda i, j: (i, j))
    return pl.pallas_call(
        body, name="down_proj_loss", grid=(t // tm, d // tn),
        in_specs=[pl.BlockSpec((tm, f), lambda i, j: (i, 0)), pl.BlockSpec((f, tn), lambda i, j: (0, j)), tile, tile],
        out_specs=[pl.BlockSpec((1, 128), lambda i, j: (0, 0)), tile, tile],
        out_shape=[jax.ShapeDtypeStruct((1, 128), F32), jax.ShapeDtypeStruct((t, d), F32),
                   jax.ShapeDtypeStruct((t, d), BF16)],
        compiler_params=_params(("arbitrary", "arbitrary"), VMEM_LIMIT),
    )(act, w_down, x1, target)


ANY = pl.BlockSpec(memory_space=pl.ANY)


def _position():
    return lax.axis_index("x"), lax.axis_index("y"), lax.axis_index("c")


def _other_chips(x, y):
    return [(1 - x, y), (x, 1 - y), (1 - x, 1 - y)]


def _handshake(peers):
    barrier = pltpu.get_barrier_semaphore()
    for peer in peers:
        pl.semaphore_signal(barrier, inc=1, device_id=peer, device_id_type=MESH)
    pl.semaphore_wait(barrier, len(peers))


def _exchange(body, operands, out_shapes, sems, *, name, collective_id):
    n_in, n_out = len(operands), len(out_shapes)

    def run(*refs):
        body(refs[:n_in], refs[n_in:n_in + n_out], *refs[n_in + n_out:])

    if collective_id is None:
        return pl.pallas_call(run, name=name, in_specs=[ANY] * n_in, out_specs=[ANY] * n_out,
                              out_shape=out_shapes, scratch_shapes=sems)(*operands)
    return pl.kernel(run, name=name, out_type=out_shapes,
                     mesh=plsc.ScalarSubcoreMesh(axis_name="sequencer", num_cores=1), scratch_types=sems,
                     compiler_params=pltpu.CompilerParams(collective_id=collective_id))(*operands)


def _all_gather(blocks, *, name, collective_id=None):
    na = len(blocks)

    def body(ins, outs, send_sems, recv_sems, local_sems):
        x, y, c = _position()
        me, sibling = (x, y, c), (x, y, 1 - c)
        along_x, along_y, diagonal = (1 - x, y, c), (x, 1 - y, c), (1 - x, 1 - y, c)
        first_c = c == 0
        relay_from = (jnp.where(first_c, x, 1 - x), jnp.where(first_c, 1 - y, y), c)
        relay_to = (jnp.where(first_c, 1 - x, x), jnp.where(first_c, y, 1 - y), c)
        if collective_id is not None:
            _handshake([sibling, along_x, along_y])

        def index(px, py, pc):
            return 4 * px + 2 * py + pc

        def copy(a, k, block, to, src=None):
            dst = outs[a].at[index(*block)]
            return pltpu.make_async_remote_copy(
                src_ref=dst if src is None else src, dst_ref=dst,
                send_sem=send_sems.at[a, k], recv_sem=recv_sems.at[a, k],
                device_id=to, device_id_type=MESH)

        pending = []
        for a in range(na):
            mine = pltpu.make_async_copy(ins[a], outs[a].at[index(*me)], local_sems.at[a])
            mine.start()
            pending.append(mine)
        sent = []
        for a in range(na):
            sent += [copy(a, 0, me, sibling, src=ins[a]), copy(a, 1, me, along_x, src=ins[a]),
                     copy(a, 2, me, along_y, src=ins[a])]
        for cp in sent:
            cp.start()

        def passes_on(k_in, owner, k_out):
            for a in range(na):
                copy(a, k_in, owner, me).wait_recv()
                cp = copy(a, k_out, owner, sibling)
                cp.start()
                sent.append(cp)

        passes_on(1, along_x, 4)
        passes_on(2, along_y, 5)
        for a in range(na):
            cp = copy(a, 3, relay_from, relay_to)
            cp.start()
            sent.append(cp)
        passes_on(3, diagonal, 6)
        for a in range(na):
            copy(a, 0, sibling, me).wait_recv()
            for k, owner in ((4, along_x), (5, along_y), (6, diagonal)):
                copy(a, k, (owner[0], owner[1], 1 - c), me).wait_recv()
        for cp in sent:
            cp.wait_send()
        for cp in pending:
            cp.wait()

    return _exchange(
        body, blocks, [jax.ShapeDtypeStruct((N_DEV,) + b.shape, b.dtype) for b in blocks],
        [pltpu.SemaphoreType.DMA((na, 7)), pltpu.SemaphoreType.DMA((na, 7)), pltpu.SemaphoreType.DMA((na,))],
        name=name, collective_id=collective_id)


def _grad_exchange(grads, parts, *, name, collective_id):
    ng, npart = len(grads), len(parts)

    def body(ins, outs, core_send, core_recv, chip_send, chip_recv, local_sems):
        x, y, c = _position()
        sibling = (x, y, 1 - c)
        chips = _other_chips(x, y)
        _handshake([sibling] + [(px, py, c) for px, py in chips])
        me = 2 * x + y
        copies = []
        for b in range(npart):
            src, dst = ins[ng + b], outs[ng + b]
            own = pltpu.make_async_copy(src.at[me], dst.at[me], local_sems.at[b])
            own.start()
            copies.append(own)
            for j, (px, py) in enumerate(chips):
                cp = pltpu.make_async_remote_copy(
                    src_ref=src.at[2 * px + py], dst_ref=dst.at[me],
                    send_sem=chip_send.at[b, j], recv_sem=chip_recv.at[b, j],
                    device_id=(px, py, c), device_id_type=MESH)
                cp.start()
                copies.append(cp)
        for a in range(ng):
            for k in range(N_CHIP):
                cp = pltpu.make_async_remote_copy(
                    src_ref=ins[a].at[k, 1 - c], dst_ref=outs[a].at[k],
                    send_sem=core_send.at[a, k], recv_sem=core_recv.at[a, k],
                    device_id=sibling, device_id_type=MESH)
                cp.start()
                copies.append(cp)
        for cp in copies:
            cp.wait()

    shapes = ([jax.ShapeDtypeStruct((N_CHIP,) + g.shape[2:], g.dtype) for g in grads]
              + [jax.ShapeDtypeStruct(p.shape, p.dtype) for p in parts])
    sems = [pltpu.SemaphoreType.DMA((max(ng, 1), N_CHIP)), pltpu.SemaphoreType.DMA((max(ng, 1), N_CHIP)),
            pltpu.SemaphoreType.DMA((max(npart, 1), 3)), pltpu.SemaphoreType.DMA((max(npart, 1), 3)),
            pltpu.SemaphoreType.DMA((max(npart, 1),))]
    out = _exchange(body, list(grads) + list(parts), shapes, sems, name=name, collective_id=collective_id)
    return out[:ng], out[ng:]


def _pair_sum(grad, theirs, core, *, tile, name, narrow=False):
    _, _, r, w = grad.shape
    tr, tw = tile
    assert r % tr == 0 and w % tw == 0

    def body(core_ref, mine_ref, theirs_ref, out_ref, *narrow_ref):
        total = mine_ref[...] + theirs_ref[...]
        out_ref[...] = total
        if narrow:
            narrow_ref[0][...] = total.astype(BF16)

    spec = pl.BlockSpec((None, tr, tw), lambda k, i, j, core_ref: (k, i, j))
    shapes = [jax.ShapeDtypeStruct((N_CHIP, r, w), F32)] + [jax.ShapeDtypeStruct((N_CHIP, r, w), BF16)] * narrow
    out = pl.pallas_call(
        body, name=name,
        grid_spec=pltpu.PrefetchScalarGridSpec(
            num_scalar_prefetch=1, grid=(N_CHIP, r // tr, w // tw),
            in_specs=[pl.BlockSpec((None, None, tr, tw), lambda k, i, j, core_ref: (k, core_ref[0], i, j)), spec],
            out_specs=[spec] * len(shapes)),
        out_shape=shapes,
        compiler_params=_params(("parallel", "parallel", "parallel"), VMEM_LIMIT),
    )(core, grad, theirs)
    return tuple(out) if narrow else out[0]


class _ReduceScatter:
    def __init__(self, core):
        self.core = core
        self.pending = None
        self.results = {}
        self.launches = 0

    def push(self, tag, grads, rows, then, narrow=False):
        pair, kept, prev_tag = [], [], None
        if self.pending is not None:
            prev_tag, prev, theirs, prev_rows, prev_narrow = self.pending
            pair = [_pair_sum(g, s, self.core, tile=tile, name=f"pair_sum_{prev_tag}_{i}", narrow=prev_narrow)
                    for i, (g, s, tile) in enumerate(zip(prev, theirs, prev_rows))]
            if prev_narrow:
                kept, pair = [p[0] for p in pair], [p[1] for p in pair]
        grads, pair, then = lax.optimization_barrier((list(grads), pair, then))
        grads = [g.reshape((N_CHIP, 2) + g.shape[1:]) for g in grads]
        self.launches += 1
        theirs, parts = _grad_exchange(grads, pair, name=f"grad_exchange_{self.launches}",
                                       collective_id=1 + self.launches)
        if prev_tag is not None:
            self.results[prev_tag] = (parts, kept)
        self.pending = (tag, grads, theirs, rows, narrow) if tag is not None else None
        return then

    def result(self, tag):
        return self.results[tag]


def _adamw(parts, w, m, v, *, tile, name, own=None, chip=None):
    n, r, cols = parts.shape
    tr, tw = tile
    assert r % tr == 0 and cols % tw == 0 and w.shape == (r, cols)
    c1 = 1.0 - ADAM_B1 ** ADAM_STEP
    c2 = 1.0 - ADAM_B2 ** ADAM_STEP

    def update(g, w_ref, m_ref, v_ref, g_ref, d_ref, nm_ref, nv_ref):
        new_m = ADAM_B1 * m_ref[...] + (1.0 - ADAM_B1) * g
        new_v = ADAM_B2 * v_ref[...] + (1.0 - ADAM_B2) * (g * g)
        m_hat = new_m / c1
        v_hat = new_v / c2
        g_ref[...] = g
        d_ref[...] = -ADAM_LR * (m_hat / (jnp.sqrt(v_hat) + ADAM_EPS) + ADAM_WD * w_ref[...])
        nm_ref[...] = new_m
        nv_ref[...] = new_v

    shape = jax.ShapeDtypeStruct((r, cols), F32)
    if own is None:
        def body(p_ref, *refs):
            g = p_ref[0]
            for k in range(1, n):
                g = g + p_ref[k]
            update(g, *refs)

        spec = pl.BlockSpec((tr, tw), lambda i, j: (i, j))
        return pl.pallas_call(
            body, name=name, grid=(r // tr, cols // tw),
            in_specs=[pl.BlockSpec((n, tr, tw), lambda i, j: (0, i, j)), spec, spec, spec],
            out_specs=[spec] * 4, out_shape=[shape] * 4,
            compiler_params=_params(("parallel", "parallel"), VMEM_LIMIT),
        )(parts, w, m, v)

    def body(chip_ref, p_ref, own_ref, *refs):
        g = None
        for k in range(n):
            term = jnp.where(chip_ref[0] == k, own_ref[...], p_ref[k].astype(F32))
            g = term if g is None else g + term
        update(g, *refs)

    spec = pl.BlockSpec((tr, tw), lambda i, j, chip_ref: (i, j))
    return pl.pallas_call(
        body, name=name,
        grid_spec=pltpu.PrefetchScalarGridSpec(
            num_scalar_prefetch=1, grid=(r // tr, cols // tw),
            in_specs=[pl.BlockSpec((n, tr, tw), lambda i, j, chip_ref: (0, i, j)),
                      pl.BlockSpec((None, tr, tw), lambda i, j, chip_ref: (chip_ref[0], i, j)), spec, spec, spec],
            out_specs=[spec] * 4),
        out_shape=[shape] * 4,
        compiler_params=_params(("parallel", "parallel"), VMEM_LIMIT),
    )(chip, parts, own, w, m, v)


LANES = 128


def _row_offsets(pieces):
    offsets, row = [], 0
    for p in pieces:
        assert p.shape[0] == 1 and p.shape[1] % LANES == 0, p.shape
        offsets.append(row)
        row += p.shape[1] // LANES
    return offsets, row


def _pack_rows(pieces):
    offsets, rows = _row_offsets(pieces)

    def body(*refs):
        out_ref = refs[-1]
        for ref, start in zip(refs[:-1], offsets):
            for j in range(ref.shape[1] // LANES):
                out_ref[start + j:start + j + 1, :] = ref[:, j * LANES:(j + 1) * LANES]

    return pl.pallas_call(body, name="pack_small_grads",
                          out_shape=jax.ShapeDtypeStruct((rows, LANES), F32))(*pieces)


def _adamw_rows(terms, ws, ms, vs):
    n_dev, rows, _ = terms.shape
    offsets, used = _row_offsets(ws)
    assert used + 1 == rows
    c1 = 1.0 - ADAM_B1 ** ADAM_STEP
    c2 = 1.0 - ADAM_B2 ** ADAM_STEP
    nw = len(ws)

    def body(*refs):
        t_ref = refs[0]
        w_refs, m_refs, v_refs = refs[1:1 + nw], refs[1 + nw:1 + 2 * nw], refs[1 + 2 * nw:1 + 3 * nw]
        outs = refs[1 + 3 * nw:]
        total = t_ref[0]
        for k in range(1, n_dev):
            total = total + t_ref[k]
        for i, start in enumerate(offsets):
            for j in range(ws[i].shape[1] // LANES):
                lanes = slice(j * LANES, (j + 1) * LANES)
                g = total[start + j:start + j + 1, :]
                new_m = ADAM_B1 * m_refs[i][:, lanes] + (1.0 - ADAM_B1) * g
                new_v = ADAM_B2 * v_refs[i][:, lanes] + (1.0 - ADAM_B2) * (g * g)
                delta = -ADAM_LR * ((new_m / c1) / (jnp.sqrt(new_v / c2) + ADAM_EPS) + ADAM_WD * w_refs[i][:, lanes])
                for kind, value in enumerate((g, delta, new_m, new_v)):
                    outs[kind * nw + i][:, lanes] = value
        outs[-1][...] = total[used:used + 1, :]

    shapes = [jax.ShapeDtypeStruct(w.shape, F32) for w in ws] * 4 + [jax.ShapeDtypeStruct((1, LANES), F32)]
    out = pl.pallas_call(body, name="adamw_replicated", out_shape=shapes,
                         compiler_params=_params(None, VMEM_LIMIT))(terms, *ws, *ms, *vs)
    return [list(out[kind * nw:(kind + 1) * nw]) for kind in range(4)], out[-1]


def _rope_tables(t):
    half = HEAD_DIM // 2
    inv = 1.0 / (ROPE_THETA ** (jnp.arange(half, dtype=F32) / half))
    ang = jnp.arange(t, dtype=jnp.int32).astype(F32)[:, None] * inv[None, :]
    cos, sin = jnp.cos(ang), jnp.sin(ang)
    return jnp.concatenate([cos, cos], axis=1), jnp.concatenate([-sin, sin], axis=1)


IN_KERNEL = IN_MAIN + LR_PAD


def _to_kernel_rows(w_t):
    order = sorted(SEGMENTS.values())
    pad = jnp.zeros((LR_PAD - (IN_TOTAL - IN_MAIN), w_t.shape[1]), w_t.dtype)
    return jnp.concatenate([w_t[src:src + width] for _, src, width in order] + [w_t[IN_MAIN:IN_TOTAL], pad], axis=0)


CONV_TAPS = 3
WA_BLOCK = GLA_KEY_WIDTH // N_DEV
SMALL_SIZES = (CONV_TAPS * UP_BLOCK, GLA_RANK * WA_BLOCK, GLA_RANK * WA_BLOCK)
SMALL_SHAPES = ((CONV_TAPS, UP_BLOCK), (GLA_RANK, WA_BLOCK), (GLA_RANK, WA_BLOCK))
SMALL_ROWS = sum(SMALL_SIZES) // LANES


def _small_block(conv, wa_f, wa_b):
    return jnp.concatenate([conv.reshape(-1), wa_f.reshape(-1), wa_b.reshape(-1)]).reshape(SMALL_ROWS, LANES)


def _small_unblock(block):
    flat, out, off = block.reshape(-1), [], 0
    for size, shape in zip(SMALL_SIZES, SMALL_SHAPES):
        out.append(flat[off:off + size].reshape(shape))
        off += size
    return out


def _small_blocks(conv_full, wa_f_full, wa_b_full):
    def by_device(a, width):
        return jnp.transpose(a.reshape(a.shape[0], N_DEV, width), (1, 0, 2)).reshape(N_DEV, -1)
    return jnp.concatenate([by_device(conv_full, UP_BLOCK), by_device(wa_f_full, WA_BLOCK),
                            by_device(wa_b_full, WA_BLOCK)], axis=1).reshape(N_DEV, SMALL_ROWS, LANES)


def _small_unblocks(blocks):
    flat, out, off = blocks.reshape(N_DEV, -1), [], 0
    for size, (rows, width) in zip(SMALL_SIZES, SMALL_SHAPES):
        part = flat[:, off:off + size].reshape(N_DEV, rows, width)
        out.append(jnp.transpose(part, (1, 0, 2)).reshape(rows, N_DEV * width))
        off += size
    return out


def _local_step(xs, target, norm1_g, w_in_k, gq, gk, attn_sink, w2, ba2, gla_norm_g, w_out_full, norm2_g,
                w_up8, cw_g, cw_v, cb_g, cb_v, w_down_full, rs=None):
    t = xs.shape[0]
    tm = min(1024, t)
    tall = min(2048, t)
    cos, sin_signed = _rope_tables(t)
    sink = attn_sink.reshape(ATTN_HEADS)

    h1 = _rmsnorm_fwd(xs, norm1_g, name="norm1_fwd")
    proj = _matmul(h1, w_in_k, "nt", tm=tall, tn=IN_MAIN // 4, tk=D_MODEL, n_out=IN_MAIN, name="proj_main")
    proj_lr = _matmul(h1, w_in_k[IN_MAIN:], "nt", tm=tm, tn=LR_PAD, tk=D_MODEL, name="proj_lr")
    qa, ka, va = _attn_prep_fwd(proj, cos, sin_signed, gq, gk)
    o_attn = _attn_fwd(qa, ka, va, sink)
    g_dec = _gla_prep_fwd(proj_lr, w2, ba2)
    o_f, o_b, s_f, s_b = _gla_fwd(proj, g_dec)
    o_gla = _gla_out_fwd(o_f, o_b, proj, gla_norm_g)
    x1 = _out_proj(o_attn, o_gla, w_out_full, xs, tm=tall, tn=512)
    h2 = _rmsnorm_fwd(x1, norm2_g, name="norm2_fwd")
    u = _up_proj(h2, w_up8, tm=tm)
    act = _ffn_mid_fwd(u, cw_g, cw_v, cb_g, cb_v)
    loss_part, dy, dy_b = _down_proj_loss(act, w_down_full, x1, target, tm=tm, tn=512)

    d_act = _matmul(dy_b, w_down_full, "nt", tm=tall, tn=D_FF // 4, tk=D_MODEL, out_dtype=BF16, name="d_act")
    dw_down = _matmul(act, dy_b, "tn", tm=D_FF // 4, tn=512, tk=t, name="dw_down")
    if rs is not None:
        d_act = rs.push("w_down", [dw_down.reshape(N_DEV, D_FF // N_DEV, D_MODEL)], [(64, D_MODEL)], d_act)
    du, dcw_g, dcw_v, dcb_g, dcb_v = _ffn_mid_bwd(u, cw_g, cw_v, cb_g, cb_v, d_act)
    dw_up8 = _up_proj_dw(h2, du, tm=512, tk=t)
    if rs is not None:
        du = rs.push("w_up", [dw_up8], [(256, UP_BLOCK)], du)
    dh2 = _up_proj_dx(du, w_up8, tm=tm, tn=1024)
    dx1, dx1_b, d_norm2 = _rmsnorm_bwd(x1, norm2_g, dh2, dy, name="norm2_bwd")
    dmix = _matmul(dx1_b, w_out_full, "nt", tm=tall, tn=1024, tk=D_MODEL, name="d_mix")
    dw_out = _out_proj_dw(o_attn, o_gla, dx1_b, tn=512)
    if rs is not None:
        dmix = rs.push("w_out", [dw_out.reshape(N_DEV, D_MODEL // N_DEV, D_MODEL)], [(256, D_MODEL)], dmix)
    do_gla, d_gate, d_gla_norm = _gla_out_bwd(o_f, o_b, proj, gla_norm_g, dmix)
    (dq_f, dk_f, dv_f, dg_f, dq_b, dk_b, dv_b, dg_b) = _gla_bwd(proj, g_dec, s_f, s_b, do_gla)
    d_lr, d_w2, d_ba2 = _gla_prep_bwd(proj_lr, w2, ba2, dg_f, dg_b)
    dqa, dk_lo, dk_mid, dk_hi, dv_lo, dv_mid, dv_hi, d_sink8 = _attn_bwd(qa, ka, va, sink, dmix)
    d_qa, d_ka, d_va, d_qn, d_kn = _attn_prep_bwd(proj, cos, sin_signed, gq, gk, dqa,
                                                  (dk_lo, dk_mid, dk_hi), (dv_lo, dv_mid, dv_hi))
    d_seg = {"qa": d_qa, "gate": d_gate, "vg": (dv_f + dv_b).astype(BF16), "qg": (dq_f + dq_b).astype(BF16),
             "kg": (dk_f + dk_b).astype(BF16), "ka": d_ka, "va": d_va}
    d_proj = jnp.concatenate([d_seg[k] for k in sorted(SEGMENTS, key=lambda k: SEGMENTS[k][0])] + [d_lr], axis=1)
    dw_in_t = _in_proj_dw_lr(_in_proj_dw(d_proj, h1, tn=1024), d_lr, h1)
    if rs is not None:
        per_in = IN_TOTAL // N_DEV
        small_grad = _small_blocks(jnp.concatenate([dcw_g, dcw_v], axis=1), d_w2[:GLA_RANK, :GLA_KEY_WIDTH],
                                   d_w2[GLA_RANK:2 * GLA_RANK, GLA_KEY_WIDTH:])
        d_proj, d_lr = rs.push("w_in", [dw_in_t.reshape(N_DEV, per_in, D_MODEL), small_grad],
                               [(per_in, 512), small_grad.shape[1:]], (d_proj, d_lr), narrow=True)
    dh1 = _matmul(d_proj, w_in_k, "nn", tm=tm, tn=512, tk=IN_KERNEL, name="dh1")
    if rs is not None:
        dh1 = rs.push(None, [], [], dh1)
    grad_x, _, d_norm1 = _rmsnorm_bwd(xs, norm1_g, dh1, dx1, name="norm1_bwd")
    return (loss_part, grad_x, dw_in_t, dw_out, dw_up8, dw_down, dcw_g, dcw_v, dcb_g, dcb_v,
            d_w2, d_ba2, d_norm1, d_norm2, d_qn, d_kn, d_sink8, d_gla_norm)


def kernel(x, norm1_g, w_in, attn_q_norm_g, attn_k_norm_g, attn_sink, gla_wa2_fwd, gla_ba_fwd, gla_wa2_bwd, gla_ba_bwd, gla_out_norm_g, w_out, norm2_g, w_up, conv_w, conv_b, w_down, loss_target, m_norm1_g, m_w_in, m_attn_q_norm_g, m_attn_k_norm_g, m_attn_sink, m_gla_wa2_fwd, m_gla_ba_fwd, m_gla_wa2_bwd, m_gla_ba_bwd, m_gla_out_norm_g, m_w_out, m_norm2_g, m_w_up, m_conv_w, m_conv_b, m_w_down, v_norm1_g, v_w_in, v_attn_q_norm_g, v_attn_k_norm_g, v_attn_sink, v_gla_wa2_fwd, v_gla_ba_fwd, v_gla_wa2_bwd, v_gla_ba_bwd, v_gla_out_norm_g, v_w_out, v_norm2_g, v_w_up, v_conv_w, v_conv_b, v_w_down):
    t = x.shape[1]
    xs = x.reshape(t, D_MODEL)
    target = loss_target.reshape(t, D_MODEL)
    core = lax.axis_index("c").astype(jnp.int32).reshape(1)

    w_small = _small_block(conv_w[0], gla_wa2_fwd[0], gla_wa2_bwd[0])
    w_in_t, m_in_t, v_in_t = (jnp.swapaxes(a[0], 0, 1) for a in (w_in, m_w_in, v_w_in))
    g_in, g_small = _all_gather([w_in_t.astype(BF16), w_small], name="gather_w_in")
    g_in, later = lax.optimization_barrier(
        (g_in, [w_out[0].astype(BF16), w_up[0].astype(BF16), w_down[0].astype(BF16)]))
    g_out, w_up8, g_down = _all_gather(later, name="gather_later_weights", collective_id=1)
    w_in_k = _to_kernel_rows(g_in.reshape(IN_TOTAL, D_MODEL))
    w_out_full = g_out.reshape(D_MODEL, D_MODEL)
    w_down_full = g_down.reshape(D_FF, D_MODEL)
    conv_w_full, wa2_f, wa2_b = _small_unblocks(g_small)
    cw_g, cw_v = conv_w_full[:, :D_FF], conv_w_full[:, D_FF:]
    cb_g, cb_v = conv_b[:, :D_FF], conv_b[:, D_FF:]
    w2 = jnp.zeros((LR_PAD, 2 * GLA_KEY_WIDTH), F32)
    w2 = w2.at[:GLA_RANK, :GLA_KEY_WIDTH].set(wa2_f).at[GLA_RANK:2 * GLA_RANK, GLA_KEY_WIDTH:].set(wa2_b)
    ba2 = jnp.concatenate([gla_ba_fwd, gla_ba_bwd], axis=1)
    rs = _ReduceScatter(core)
    (loss_part, grad_x, _, _, _, _, _, _, dcb_g, dcb_v, _, d_ba2,
     d_norm1, d_norm2, d_qn, d_kn, d_sink8, d_gla_norm) = _local_step(
        xs, target, norm1_g, w_in_k, attn_q_norm_g, attn_k_norm_g, attn_sink, w2, ba2, gla_out_norm_g,
        w_out_full, norm2_g, w_up8, cw_g, cw_v, cb_g, cb_v, w_down_full, rs=rs)

    (part_down,), (part_up,), (part_out,) = rs.result("w_down")[0], rs.result("w_up")[0], rs.result("w_out")[0]
    (part_in, part_small), (own_in, own_small) = rs.result("w_in")
    chip = (2 * lax.axis_index("x") + lax.axis_index("y")).astype(jnp.int32).reshape(1)
    m_small = _small_block(m_conv_w[0], m_gla_wa2_fwd[0], m_gla_wa2_bwd[0])
    v_small = _small_block(v_conv_w[0], v_gla_wa2_fwd[0], v_gla_wa2_bwd[0])
    upd_in = _adamw(part_in, w_in_t, m_in_t, v_in_t, tile=(IN_TOTAL // N_DEV, 512), name="adamw_w_in",
                    own=own_in, chip=chip)
    upd_in = [jnp.swapaxes(u, 0, 1) for u in upd_in]
    upd_out = _adamw(part_out, w_out[0], m_w_out[0], v_w_out[0], tile=(256, D_MODEL), name="adamw_w_out")
    upd_up = _adamw(part_up, w_up[0], m_w_up[0], v_w_up[0], tile=(256, UP_BLOCK), name="adamw_w_up")
    upd_down = _adamw(part_down, w_down[0], m_w_down[0], v_w_down[0], tile=(64, D_MODEL), name="adamw_w_down")
    upd_small = _adamw(part_small, w_small, m_small, v_small, tile=part_small.shape[1:],
                       name="adamw_small", own=own_small, chip=chip)
    upd_small = [_small_unblock(u) for u in upd_small]

    rep_names = ["norm1_g", "attn_q_norm_g", "attn_k_norm_g", "attn_sink", "gla_ba_fwd", "gla_ba_bwd",
                 "gla_out_norm_g", "norm2_g", "conv_b"]
    def whole_lanes(sink_like):
        return jnp.pad(sink_like, ((0, 0), (0, LANES - ATTN_HEADS)))

    rep_w = [norm1_g, attn_q_norm_g, attn_k_norm_g, whole_lanes(attn_sink), gla_ba_fwd, gla_ba_bwd, gla_out_norm_g,
             norm2_g, conv_b]
    rep_m = [m_norm1_g, m_attn_q_norm_g, m_attn_k_norm_g, whole_lanes(m_attn_sink), m_gla_ba_fwd, m_gla_ba_bwd,
             m_gla_out_norm_g, m_norm2_g, m_conv_b]
    rep_v = [v_norm1_g, v_attn_q_norm_g, v_attn_k_norm_g, whole_lanes(v_attn_sink), v_gla_ba_fwd, v_gla_ba_bwd,
             v_gla_out_norm_g, v_norm2_g, v_conv_b]
    d_sink = whole_lanes(d_sink8[:, :GQA_GROUP, 0].reshape(1, ATTN_HEADS))
    rep_g = [d_norm1, d_qn, d_kn, d_sink, d_ba2[:, :GLA_KEY_WIDTH], d_ba2[:, GLA_KEY_WIDTH:], d_gla_norm, d_norm2,
             jnp.concatenate([dcb_g, dcb_v], axis=1)]
    (rep_terms,) = _all_gather([_pack_rows(rep_g + [loss_part])], name="gather_small_grads", collective_id=7)
    upd_rep, loss_row = _adamw_rows(rep_terms, rep_w, rep_m, rep_v)
    sink_at = rep_names.index("attn_sink")
    for kind in range(4):
        upd_rep[kind][sink_at] = upd_rep[kind][sink_at][:, :ATTN_HEADS]
    loss = loss_row[0, 0]

    order = ["norm1_g", "w_in", "attn_q_norm_g", "attn_k_norm_g", "attn_sink", "gla_wa2_fwd", "gla_ba_fwd",
             "gla_wa2_bwd", "gla_ba_bwd", "gla_out_norm_g", "w_out", "norm2_g", "w_up", "conv_w", "conv_b", "w_down"]
    outs = [loss, grad_x.reshape(1, t, D_MODEL)]
    for kind in range(4):
        by_name = {n: upd_rep[kind][i] for i, n in enumerate(rep_names)}
        by_name["w_in"] = upd_in[kind][None]
        by_name["w_out"] = upd_out[kind][None]
        by_name["w_up"] = upd_up[kind][None]
        by_name["w_down"] = upd_down[kind][None]
        by_name["conv_w"] = upd_small[kind][0][None]
        by_name["gla_wa2_fwd"] = upd_small[kind][1][None]
        by_name["gla_wa2_bwd"] = upd_small[kind][2][None]
        outs += [by_name[n] for n in order]
    return tuple(outs)
```

```python
import functools

import jax
import jax.numpy as jnp
from jax import lax
from jax.experimental import pallas as pl
from jax.experimental.pallas import tpu as pltpu
from jax.experimental.pallas import tpu_sc as plsc

F32 = jnp.float32
BF16 = jnp.bfloat16

D_MODEL = 2048
HEAD_DIM = 128
ATTN_WIDTH = 1024
ATTN_HEADS = 8
KV_HEADS = 2
GQA_GROUP = 4
KV_WIDTH = KV_HEADS * HEAD_DIM
ATTN_BLOCK = 128
WINDOW = 128
ROPE_THETA = 10000.0
GLA_HEADS = 4
GLA_DK = 128
GLA_DV = 256
GLA_KEY_WIDTH = 512
GLA_WIDTH = 1024
GLA_RANK = 16
GLA_GATE_NORMALIZER = 16.0
GLA_CHUNK = 64
GLA_PER_STEP = 4
D_FF = 5632
NORM_EPS = 1e-6
IN_TOTAL = 4640
IN_MAIN = 4608
LR_PAD = 128
N_DEV = 8
N_CHIP = 4

ADAM_LR = 0.001
ADAM_B1 = 0.9
ADAM_B2 = 0.999
ADAM_EPS = 1e-08
ADAM_WD = 0.01
ADAM_STEP = 10

SEGMENTS = {
    "qa": (0, 0, 1024),
    "gate": (1024, 3584, 1024),
    "vg": (2048, 2560, 1024),
    "qg": (3072, 1536, 512),
    "kg": (3584, 2048, 512),
    "ka": (4096, 1024, 256),
    "va": (4352, 1280, 256),
}

VMEM_LIMIT = 56 * 1024 * 1024
MESH = pl.DeviceIdType.MESH


def _params(semantics=None, vmem=None):
    return pltpu.CompilerParams(dimension_semantics=semantics, vmem_limit_bytes=vmem)


_DIMS = {
    "nn": (((1,), (0,)), ((), ())),
    "nt": (((1,), (1,)), ((), ())),
    "tn": (((0,), (0,)), ((), ())),
}


def _mxu(a, b, mode):
    return lax.dot_general(a.astype(BF16), b.astype(BF16), _DIMS[mode], preferred_element_type=F32)


@functools.partial(jax.custom_vjp, nondiff_argnums=(2,))
def bdot(a, b, mode):
    return _mxu(a, b, mode)


def _bdot_fwd(a, b, mode):
    return _mxu(a, b, mode), (a, b)


def _bdot_bwd(mode, res, g):
    a, b = res
    if mode == "nn":
        return _mxu(g, b, "nt"), _mxu(a, g, "tn")
    if mode == "nt":
        return _mxu(g, b, "nn"), _mxu(g, a, "tn")
    return _mxu(b, g, "nt"), _mxu(a, g, "nn")


bdot.defvjp(_bdot_fwd, _bdot_bwd)


def _rms(x, g):
    return x * lax.rsqrt(jnp.mean(x * x, axis=-1, keepdims=True) + NORM_EPS) * g


def _rope(x, cos, sin_signed):
    return x * cos + pltpu.roll(x, HEAD_DIM // 2, 1) * sin_signed


def _rope_transposed(d, cos, sin_signed):
    return d * cos + pltpu.roll(d * sin_signed, HEAD_DIM // 2, 1)


def _silu(x):
    return x * jax.nn.sigmoid(x)


def _log_sigmoid(z):
    return -(jnp.maximum(-z, 0.0) + jnp.log(1.0 + jnp.exp(-jnp.abs(z))))


def _matmul_call(args, in_specs, o_spec, out_shape, grid, mode, nk, acc_shape, *, name, has_res=False,
                 prefetch=None, load_b=lambda ref: ref[...]):
    dims = _DIMS[mode]
    out_dtype = out_shape.dtype
    n_pre = 0 if prefetch is None else 1

    def body(*refs):
        refs = refs[n_pre:]
        if has_res:
            a_ref, b_ref, r_ref, o_ref = refs[:4]
            rest = refs[4:]
        else:
            a_ref, b_ref, o_ref = refs[:3]
            r_ref = None
            rest = refs[3:]
        part = lax.dot_general(a_ref[...], load_b(b_ref), dims, preferred_element_type=F32)

        def finish(acc):
            if r_ref is not None:
                acc = acc + r_ref[...]
            o_ref[...] = acc.astype(out_dtype)

        if nk == 1:
            finish(part)
        else:
            acc_ref = rest[0]
            kk = pl.program_id(2)

            @pl.when(kk == 0)
            def _():
                acc_ref[...] = part

            @pl.when(kk > 0)
            def _():
                acc_ref[...] += part

            @pl.when(kk == nk - 1)
            def _():
                finish(acc_ref[...])

    scratch = [pltpu.VMEM(acc_shape, F32)] if nk > 1 else []
    params = _params(("parallel", "parallel", "arbitrary"), VMEM_LIMIT)
    if prefetch is None:
        return pl.pallas_call(body, name=name, grid=grid, in_specs=in_specs, out_specs=o_spec, out_shape=out_shape,
                              scratch_shapes=scratch, compiler_params=params)(*args)
    return pl.pallas_call(
        body, name=name,
        grid_spec=pltpu.PrefetchScalarGridSpec(num_scalar_prefetch=1, grid=grid, in_specs=in_specs,
                                               out_specs=o_spec, scratch_shapes=scratch),
        out_shape=out_shape, compiler_params=params)(prefetch, *args)


def _matmul(a, b, mode, *, tm, tn, tk, out_dtype=F32, res=None, name, n_out=None):
    if mode == "nn":
        (m, k), (k2, n) = a.shape, b.shape
    elif mode == "nt":
        (m, k), (n, k2) = a.shape, b.shape
    else:
        (k, m), (k2, n) = a.shape, b.shape
    n = n if n_out is None else n_out
    assert k == k2 and m % tm == 0 and n % tn == 0 and k % tk == 0, (name, a.shape, b.shape, tm, tn, tk)
    if mode == "tn":
        a_spec = pl.BlockSpec((tk, tm), lambda i, j, kk: (kk, i))
    else:
        a_spec = pl.BlockSpec((tm, tk), lambda i, j, kk: (i, kk))
    if mode == "nt":
        b_spec = pl.BlockSpec((tn, tk), lambda i, j, kk: (j, kk))
    else:
        b_spec = pl.BlockSpec((tk, tn), lambda i, j, kk: (kk, j))
    o_spec = pl.BlockSpec((tm, tn), lambda i, j, kk: (i, j))
    in_specs, args = [a_spec, b_spec], [a, b]
    if res is not None:
        in_specs.append(o_spec)
        args.append(res)
    return _matmul_call(args, in_specs, o_spec, jax.ShapeDtypeStruct((m, n), out_dtype),
                        (m // tm, n // tn, k // tk), mode, k // tk, (tm, tn), name=name, has_res=res is not None)


def _out_proj(o_attn, o_gla, w_out, x, *, tm, tn):
    t, ka = o_attn.shape
    kg = o_gla.shape[1]

    def body(a_ref, g_ref, w_ref, x_ref, o_ref):
        acc = lax.dot_general(a_ref[...], w_ref[:ka], _DIMS["nn"], preferred_element_type=F32)
        acc = acc + lax.dot_general(g_ref[...], w_ref[ka:], _DIMS["nn"], preferred_element_type=F32)
        o_ref[...] = acc + x_ref[...]

    tile = pl.BlockSpec((tm, tn), lambda i, j: (i, j))
    return pl.pallas_call(
        body, name="out_proj", grid=(t // tm, D_MODEL // tn),
        in_specs=[pl.BlockSpec((tm, ka), lambda i, j: (i, 0)), pl.BlockSpec((tm, kg), lambda i, j: (i, 0)),
                  pl.BlockSpec((ka + kg, tn), lambda i, j: (0, j)), tile],
        out_specs=tile, out_shape=jax.ShapeDtypeStruct((t, D_MODEL), F32),
        compiler_params=_params(("parallel", "parallel"), VMEM_LIMIT),
    )(o_attn, o_gla, w_out, x)


def _out_proj_dw(o_attn, o_gla, dx1, *, tn):
    t, ka = o_attn.shape
    assert o_gla.shape == (t, ka)

    def body(a_ref, g_ref, d_ref, o_ref):
        @pl.when(pl.program_id(0) == 0)
        def _():
            o_ref[...] = lax.dot_general(a_ref[...], d_ref[...], _DIMS["tn"], preferred_element_type=F32)

        @pl.when(pl.program_id(0) == 1)
        def _():
            o_ref[...] = lax.dot_general(g_ref[...], d_ref[...], _DIMS["tn"], preferred_element_type=F32)

    whole = pl.BlockSpec((t, ka), lambda i, j: (0, 0))
    return pl.pallas_call(
        body, name="dw_out", grid=(2, D_MODEL // tn),
        in_specs=[whole, whole, pl.BlockSpec((t, tn), lambda i, j: (0, j))],
        out_specs=pl.BlockSpec((ka, tn), lambda i, j: (i, j)),
        out_shape=jax.ShapeDtypeStruct((2 * ka, D_MODEL), F32),
        compiler_params=_params(("parallel", "parallel"), VMEM_LIMIT),
    )(o_attn, o_gla, dx1)


UP_BLOCK = 2 * D_FF // N_DEV


def _up_proj(h2, w_up8, *, tm):
    t = h2.shape[0]
    return _matmul_call(
        [h2, w_up8],
        [pl.BlockSpec((tm, D_MODEL), lambda i, j, kk: (i, 0)),
         pl.BlockSpec((None, D_MODEL, UP_BLOCK), lambda i, j, kk: (j, 0, 0))],
        pl.BlockSpec((None, tm, UP_BLOCK), lambda i, j, kk: (j // N_CHIP, i, j % N_CHIP)),
        jax.ShapeDtypeStruct((2, t, D_FF), F32), (t // tm, N_DEV, 1), "nn", 1, None, name="up_proj")


def _up_proj_dx(du, w_up8, *, tm, tn):
    t = du.shape[1]
    pair = 2
    return _matmul_call(
        [du, w_up8],
        [pl.BlockSpec((None, tm, pair * UP_BLOCK), lambda i, j, kk: (kk // 2, i, kk % 2)),
         pl.BlockSpec((pair, tn, UP_BLOCK), lambda i, j, kk: (kk, j, 0))],
        pl.BlockSpec((tm, tn), lambda i, j, kk: (i, j)),
        jax.ShapeDtypeStruct((t, D_MODEL), F32), (t // tm, D_MODEL // tn, N_DEV // pair), "nt", N_DEV // pair,
        (tm, tn), name="up_proj_dx", load_b=lambda ref: jnp.concatenate([ref[0], ref[1]], axis=1))


def _up_proj_dw(h2, du, *, tm, tk):
    t = h2.shape[0]
    return _matmul_call(
        [h2, du],
        [pl.BlockSpec((tk, tm), lambda j, i, kk: (kk, i)),
         pl.BlockSpec((None, tk, UP_BLOCK), lambda j, i, kk: (j // N_CHIP, kk, j % N_CHIP))],
        pl.BlockSpec((None, tm, UP_BLOCK), lambda j, i, kk: (j, i, 0)),
        jax.ShapeDtypeStruct((N_DEV, D_MODEL, UP_BLOCK), F32), (N_DEV, D_MODEL // tm, t // tk), "tn", t // tk,
        (tm, UP_BLOCK), name="up_proj_dw")


IN_TILE = 512


def _in_proj_dw(d_proj, h1, *, tn):
    t = h1.shape[0]
    table = []
    for tile in range(IN_MAIN // IN_TILE):
        dst, src, _ = max(s for s in SEGMENTS.values() if s[0] <= tile * IN_TILE)
        assert (src + tile * IN_TILE - dst) % IN_TILE == 0
        table.append((src + tile * IN_TILE - dst) // IN_TILE)
    assert sorted(table) == list(range(IN_MAIN // IN_TILE))
    return _matmul_call(
        [d_proj, h1],
        [pl.BlockSpec((t, IN_TILE), lambda j, i, kk, tab: (0, i)),
         pl.BlockSpec((t, tn), lambda j, i, kk, tab: (0, j))],
        pl.BlockSpec((IN_TILE, tn), lambda j, i, kk, tab: (tab[i], j)),
        jax.ShapeDtypeStruct((IN_TOTAL, D_MODEL), F32), (D_MODEL // tn, IN_MAIN // IN_TILE, 1), "tn", 1, None,
        name="in_proj_dw", prefetch=jnp.asarray(table, jnp.int32))


def _in_proj_dw_lr(dw_t, d_lr, h1):
    t = h1.shape[0]
    n_lr = IN_TOTAL - IN_MAIN
    tn = 512

    def body(dw_ref, dlr_ref, h1_ref, out_ref):
        full = lax.dot_general(dlr_ref[...], h1_ref[...], _DIMS["tn"], preferred_element_type=F32)
        out_ref[...] = full[:n_lr]

    return pl.pallas_call(
        body, name="in_proj_dw_lr", grid=(D_MODEL // tn,),
        in_specs=[pl.BlockSpec(memory_space=pl.ANY),
                  pl.BlockSpec((t, LR_PAD), lambda j: (0, 0)),
                  pl.BlockSpec((t, tn), lambda j: (0, j))],
        out_specs=pl.BlockSpec((n_lr, tn), lambda j: (IN_MAIN // n_lr, j)),
        out_shape=jax.ShapeDtypeStruct(dw_t.shape, F32),
        input_output_aliases={0: 0},
        compiler_params=_params(("parallel",), VMEM_LIMIT),
    )(dw_t, d_lr, h1)


def _rmsnorm_fwd(x, g, *, name, tr=512):
    t, d = x.shape

    def body(x_ref, g_ref, h_ref):
        h_ref[...] = _rms(x_ref[...], g_ref[...]).astype(BF16)

    return pl.pallas_call(
        body, name=name, grid=(t // tr,),
        in_specs=[pl.BlockSpec((tr, d), lambda i: (i, 0)), pl.BlockSpec((1, d), lambda i: (0, 0))],
        out_specs=pl.BlockSpec((tr, d), lambda i: (i, 0)),
        out_shape=jax.ShapeDtypeStruct((t, d), BF16),
        compiler_params=_params(("parallel",), VMEM_LIMIT),
    )(x, g)


def _rmsnorm_bwd(x, g, dh, dres, *, name, also_bf16, tr=512):
    t, d = x.shape

    def body(x_ref, g_ref, dh_ref, dres_ref, dx_ref, *rest):
        dg_ref = rest[-1]
        _, vjp = jax.vjp(_rms, x_ref[...], g_ref[...])
        dx, dg = vjp(dh_ref[...])
        dx = dx + dres_ref[...]
        dx_ref[...] = dx
        if also_bf16:
            rest[0][...] = dx.astype(BF16)

        @pl.when(pl.program_id(0) == 0)
        def _():
            dg_ref[...] = jnp.zeros_like(dg_ref)

        dg_ref[...] += dg

    row = pl.BlockSpec((tr, d), lambda i: (i, 0))
    vec = pl.BlockSpec((1, d), lambda i: (0, 0))
    return pl.pallas_call(
        body, name=name, grid=(t // tr,),
        in_specs=[row, vec, row, row],
        out_specs=[row] + [row] * also_bf16 + [vec],
        out_shape=[jax.ShapeDtypeStruct((t, d), F32)] + [jax.ShapeDtypeStruct((t, d), BF16)] * also_bf16
                  + [jax.ShapeDtypeStruct((1, d), F32)],
        compiler_params=_params(("arbitrary",), VMEM_LIMIT),
    )(x, g, dh, dres)


def _seg_block(name, width):
    off = SEGMENTS[name][0]
    assert off % width == 0
    return off // width


def _attn_prep_fwd(proj, cos, sin_signed, gq, gk, *, tr=512):
    t = proj.shape[0]

    def body(q_ref, k_ref, v_ref, cos_ref, sin_ref, gq_ref, gk_ref, qo_ref, ko_ref, vo_ref):
        cos_t, sin_t = cos_ref[...], sin_ref[...]
        for h in range(ATTN_HEADS):
            cols = slice(h * HEAD_DIM, (h + 1) * HEAD_DIM)
            qo_ref[:, cols] = _rope(_rms(q_ref[:, cols], gq_ref[...]), cos_t, sin_t).astype(BF16)
        for h in range(KV_HEADS):
            cols = slice(h * HEAD_DIM, (h + 1) * HEAD_DIM)
            ko_ref[:, cols] = _rope(_rms(k_ref[:, cols], gk_ref[...]), cos_t, sin_t).astype(BF16)
        vo_ref[...] = v_ref[...].astype(BF16)

    qb, kb, vb = _seg_block("qa", ATTN_WIDTH), _seg_block("ka", KV_WIDTH), _seg_block("va", KV_WIDTH)
    tab = pl.BlockSpec((tr, HEAD_DIM), lambda i: (i, 0))
    vec = pl.BlockSpec((1, HEAD_DIM), lambda i: (0, 0))
    return pl.pallas_call(
        body, name="attn_prep_fwd", grid=(t // tr,),
        in_specs=[pl.BlockSpec((tr, ATTN_WIDTH), lambda i: (i, qb)),
                  pl.BlockSpec((tr, KV_WIDTH), lambda i: (i, kb)),
                  pl.BlockSpec((tr, KV_WIDTH), lambda i: (i, vb)),
                  tab, tab, vec, vec],
        out_specs=[pl.BlockSpec((tr, ATTN_WIDTH), lambda i: (i, 0)),
                   pl.BlockSpec((tr, KV_WIDTH), lambda i: (i, 0)),
                   pl.BlockSpec((tr, KV_WIDTH), lambda i: (i, 0))],
        out_shape=[jax.ShapeDtypeStruct((t, ATTN_WIDTH), BF16),
                   jax.ShapeDtypeStruct((t, KV_WIDTH), BF16),
                   jax.ShapeDtypeStruct((t, KV_WIDTH), BF16)],
        compiler_params=_params(("parallel",), VMEM_LIMIT),
    )(proj, proj, proj, cos, sin_signed, gq, gk)


def _attn_heads(q, kcat, vcat, sink_col, valid):
    s = bdot(q, kcat, "nt") * (HEAD_DIM ** -0.5)
    s = jnp.where(valid, s, -jnp.inf)
    m = lax.stop_gradient(jnp.maximum(jnp.max(s, axis=-1, keepdims=True), sink_col))
    p = jnp.exp(s - m)
    p = p / (jnp.sum(p, axis=-1, keepdims=True) + jnp.exp(sink_col - m))
    return bdot(p, vcat, "nn")


def _attn_valid(n, t):
    shape = (GQA_GROUP * ATTN_BLOCK, 3 * ATTN_BLOCK)
    qi = lax.broadcasted_iota(jnp.int32, shape, 0) % ATTN_BLOCK
    sj = lax.broadcasted_iota(jnp.int32, shape, 1)
    kpos = n * ATTN_BLOCK - ATTN_BLOCK + sj
    return (jnp.abs(sj - ATTN_BLOCK - qi) <= WINDOW) & (kpos >= 0) & (kpos < t)


def _head_rows(g):
    return slice(g * ATTN_BLOCK, (g + 1) * ATTN_BLOCK)


def _head_cols(g):
    return slice(g * HEAD_DIM, (g + 1) * HEAD_DIM)


def _stack_heads(ref):
    return jnp.concatenate([ref[:, _head_cols(g)] for g in range(GQA_GROUP)], axis=0).astype(F32)


def _sink_column(sink_ref, h):
    return jnp.concatenate([jnp.full((ATTN_BLOCK, 1), sink_ref[h * GQA_GROUP + g], F32)
                            for g in range(GQA_GROUP)], axis=0)


def _attn_specs(nb):
    q_spec = pl.BlockSpec((ATTN_BLOCK, GQA_GROUP * HEAD_DIM), lambda h, n: (n, h))
    kv_specs = [
        pl.BlockSpec((ATTN_BLOCK, HEAD_DIM), lambda h, n: (jnp.maximum(n - 1, 0), h)),
        pl.BlockSpec((ATTN_BLOCK, HEAD_DIM), lambda h, n: (n, h)),
        pl.BlockSpec((ATTN_BLOCK, HEAD_DIM), lambda h, n: (jnp.minimum(n + 1, nb - 1), h)),
    ]
    return q_spec, kv_specs


def _attn_fwd(q, k, v, sink):
    t = q.shape[0]
    nb = t // ATTN_BLOCK

    def body(sink_ref, q_ref, kp_ref, kc_ref, kn_ref, vp_ref, vc_ref, vn_ref, o_ref):
        valid = _attn_valid(pl.program_id(0), t)
        args = []
        for h in range(KV_HEADS):
            q = jnp.concatenate([q_ref[:, _head_cols(h * GQA_GROUP + g)] for g in range(GQA_GROUP)], axis=0)
            kcat = jnp.concatenate([r[:, _head_cols(h)] for r in (kp_ref, kc_ref, kn_ref)], axis=0)
            vcat = jnp.concatenate([r[:, _head_cols(h)] for r in (vp_ref, vc_ref, vn_ref)], axis=0)
            args.append((q.astype(F32), kcat.astype(F32), vcat.astype(F32), _sink_column(sink_ref, h)))
        outs = [_attn_heads(*a, valid=valid).astype(BF16) for a in args]
        for h, o in enumerate(outs):
            for g in range(GQA_GROUP):
                o_ref[:, _head_cols(h * GQA_GROUP + g)] = o[_head_rows(g)]

    q_spec = pl.BlockSpec((ATTN_BLOCK, ATTN_WIDTH), lambda n: (n, 0))
    kv_specs = [pl.BlockSpec((ATTN_BLOCK, KV_WIDTH), lambda n: (jnp.maximum(n - 1, 0), 0)),
                pl.BlockSpec((ATTN_BLOCK, KV_WIDTH), lambda n: (n, 0)),
                pl.BlockSpec((ATTN_BLOCK, KV_WIDTH), lambda n: (jnp.minimum(n + 1, nb - 1), 0))]
    return pl.pallas_call(
        body, name="attn_fwd", grid=(nb,),
        in_specs=[pl.BlockSpec(memory_space=pltpu.SMEM), q_spec] + kv_specs + kv_specs,
        out_specs=q_spec,
        out_shape=jax.ShapeDtypeStruct((t, ATTN_WIDTH), BF16),
        compiler_params=_params(("parallel",), VMEM_LIMIT),
    )(sink, q, k, k, k, v, v, v)


def _attn_bwd(q, k, v, sink, dmix):
    t = q.shape[0]
    nb = t // ATTN_BLOCK

    def body(sink_ref, q_ref, kp_ref, kc_ref, kn_ref, vp_ref, vc_ref, vn_ref, do_ref,
             dq_ref, dk_lo, dk_mid, dk_hi, dv_lo, dv_mid, dv_hi, dsink_ref):
        h, n = pl.program_id(0), pl.program_id(1)
        valid = _attn_valid(n, t)
        kcat = jnp.concatenate([kp_ref[...], kc_ref[...], kn_ref[...]], axis=0).astype(F32)
        vcat = jnp.concatenate([vp_ref[...], vc_ref[...], vn_ref[...]], axis=0).astype(F32)
        _, vjp = jax.vjp(functools.partial(_attn_heads, valid=valid),
                         _stack_heads(q_ref), kcat, vcat, _sink_column(sink_ref, h))
        dq, dk, dv, dsink_col = vjp(_stack_heads(do_ref))
        row = lax.broadcasted_iota(jnp.int32, (8, HEAD_DIM), 0)
        dsink = jnp.zeros((8, HEAD_DIM), F32)
        for g in range(GQA_GROUP):
            dq_ref[:, _head_cols(g)] = dq[_head_rows(g)]
            dsink = dsink + jnp.where(row == g, jnp.sum(dsink_col[_head_rows(g)]), 0.0)
        for i, (dk_ref, dv_ref) in enumerate(((dk_lo, dv_lo), (dk_mid, dv_mid), (dk_hi, dv_hi))):
            rows = slice(i * ATTN_BLOCK, (i + 1) * ATTN_BLOCK)
            dk_ref[...] = dk[rows]
            dv_ref[...] = dv[rows]

        @pl.when(n == 0)
        def _():
            dsink_ref[...] = jnp.zeros_like(dsink_ref)

        dsink_ref[...] += dsink

    q_spec, kv_specs = _attn_specs(nb)
    kv_out = pl.BlockSpec((ATTN_BLOCK, HEAD_DIM), lambda h, n: (n, h))
    kv_shape = jax.ShapeDtypeStruct((t, KV_WIDTH), F32)
    return pl.pallas_call(
        body, name="attn_bwd", grid=(KV_HEADS, nb),
        in_specs=[pl.BlockSpec(memory_space=pltpu.SMEM), q_spec] + kv_specs + kv_specs + [q_spec],
        out_specs=[q_spec] + [kv_out] * 6 + [pl.BlockSpec((None, 8, HEAD_DIM), lambda h, n: (h, 0, 0))],
        out_shape=[jax.ShapeDtypeStruct((t, ATTN_WIDTH), F32)] + [kv_shape] * 6
                  + [jax.ShapeDtypeStruct((KV_HEADS, 8, HEAD_DIM), F32)],
        compiler_params=_params(("parallel", "arbitrary"), VMEM_LIMIT),
    )(sink, q, k, k, k, v, v, v, dmix)


def _attn_prep_bwd(proj, cos, sin_signed, gq, gk, dq, dks, dvs):
    t = proj.shape[0]
    tr = ATTN_BLOCK
    nb = t // tr

    def body(q_ref, k_ref, cos_ref, sin_ref, gq_ref, gk_ref, dq_ref,
             dk_lo, dk_mid, dk_hi, dv_lo, dv_mid, dv_hi,
             dqo_ref, dko_ref, dvo_ref, dgq_ref, dgk_ref):
        n = pl.program_id(0)
        cos_t, sin_t = cos_ref[...], sin_ref[...]
        has_next = (n < nb - 1).astype(F32)
        has_prev = (n > 0).astype(F32)
        dk = dk_lo[...] * has_next + dk_mid[...] + dk_hi[...] * has_prev
        dv = dv_lo[...] * has_next + dv_mid[...] + dv_hi[...] * has_prev
        dvo_ref[...] = dv.astype(BF16)
        dgq = jnp.zeros((1, HEAD_DIM), F32)
        dgk = jnp.zeros((1, HEAD_DIM), F32)
        for h in range(ATTN_HEADS):
            cols = slice(h * HEAD_DIM, (h + 1) * HEAD_DIM)
            _, vjp = jax.vjp(_rms, q_ref[:, cols], gq_ref[...])
            dx, dg = vjp(_rope_transposed(dq_ref[:, cols], cos_t, sin_t))
            dqo_ref[:, cols] = dx.astype(BF16)
            dgq = dgq + dg
        for h in range(KV_HEADS):
            cols = slice(h * HEAD_DIM, (h + 1) * HEAD_DIM)
            _, vjp = jax.vjp(_rms, k_ref[:, cols], gk_ref[...])
            dx, dg = vjp(_rope_transposed(dk[:, cols], cos_t, sin_t))
            dko_ref[:, cols] = dx.astype(BF16)
            dgk = dgk + dg

        @pl.when(n == 0)
        def _():
            dgq_ref[...] = jnp.zeros_like(dgq_ref)
            dgk_ref[...] = jnp.zeros_like(dgk_ref)

        dgq_ref[...] += dgq
        dgk_ref[...] += dgk

    qb, kb = _seg_block("qa", ATTN_WIDTH), _seg_block("ka", KV_WIDTH)
    tab = pl.BlockSpec((tr, HEAD_DIM), lambda i: (i, 0))
    vec = pl.BlockSpec((1, HEAD_DIM), lambda i: (0, 0))
    kv = [pl.BlockSpec((tr, KV_WIDTH), lambda i: (jnp.minimum(i + 1, nb - 1), 0)),
          pl.BlockSpec((tr, KV_WIDTH), lambda i: (i, 0)),
          pl.BlockSpec((tr, KV_WIDTH), lambda i: (jnp.maximum(i - 1, 0), 0))]
    wide = pl.BlockSpec((tr, ATTN_WIDTH), lambda i: (i, 0))
    narrow = pl.BlockSpec((tr, KV_WIDTH), lambda i: (i, 0))
    return pl.pallas_call(
        body, name="attn_prep_bwd", grid=(nb,),
        in_specs=[pl.BlockSpec((tr, ATTN_WIDTH), lambda i: (i, qb)),
                  pl.BlockSpec((tr, KV_WIDTH), lambda i: (i, kb)),
                  tab, tab, vec, vec, wide] + kv + kv,
        out_specs=[wide, narrow, narrow, vec, vec],
        out_shape=[jax.ShapeDtypeStruct((t, ATTN_WIDTH), BF16),
                   jax.ShapeDtypeStruct((t, KV_WIDTH), BF16),
                   jax.ShapeDtypeStruct((t, KV_WIDTH), BF16),
                   jax.ShapeDtypeStruct((1, HEAD_DIM), F32),
                   jax.ShapeDtypeStruct((1, HEAD_DIM), F32)],
        compiler_params=_params(("arbitrary",), VMEM_LIMIT),
    )(proj, proj, cos, sin_signed, gq, gk, dq, *dks, *dvs)


def _decay_fn(lr, w2, ba):
    return _log_sigmoid(bdot(lr, w2, "nn") + ba) / GLA_GATE_NORMALIZER


def _gla_prep_fwd(proj_lr, w2, ba2, *, tr=512):
    t = proj_lr.shape[0]
    width = 2 * GLA_KEY_WIDTH

    def body(lr_ref, w2_ref, ba_ref, g_ref):
        g_ref[...] = _decay_fn(lr_ref[...], w2_ref[...], ba_ref[...])

    return pl.pallas_call(
        body, name="gla_prep_fwd", grid=(t // tr,),
        in_specs=[pl.BlockSpec((tr, LR_PAD), lambda i: (i, 0)),
                  pl.BlockSpec((LR_PAD, width), lambda i: (0, 0)),
                  pl.BlockSpec((1, width), lambda i: (0, 0))],
        out_specs=pl.BlockSpec((tr, width), lambda i: (i, 0)),
        out_shape=jax.ShapeDtypeStruct((t, width), F32),
        compiler_params=_params(("parallel",), VMEM_LIMIT),
    )(proj_lr, w2, ba2)


def _gla_prep_bwd(proj_lr, w2, ba2, dg_f, dg_b, *, tr=512):
    t = proj_lr.shape[0]
    width = 2 * GLA_KEY_WIDTH

    def body(lr_ref, w2_ref, ba_ref, dgf_ref, dgb_ref, dlr_ref, dw2_ref, dba_ref):
        _, vjp = jax.vjp(_decay_fn, lr_ref[...], w2_ref[...], ba_ref[...])
        dlr, dw2, dba = vjp(jnp.concatenate([dgf_ref[...], dgb_ref[...]], axis=1))
        dlr_ref[...] = dlr.astype(BF16)

        @pl.when(pl.program_id(0) == 0)
        def _():
            dw2_ref[...] = jnp.zeros_like(dw2_ref)
            dba_ref[...] = jnp.zeros_like(dba_ref)

        dw2_ref[...] += dw2
        dba_ref[...] += dba

    half = pl.BlockSpec((tr, GLA_KEY_WIDTH), lambda i: (i, 0))
    return pl.pallas_call(
        body, name="gla_prep_bwd", grid=(t // tr,),
        in_specs=[pl.BlockSpec((tr, LR_PAD), lambda i: (i, 0)),
                  pl.BlockSpec((LR_PAD, width), lambda i: (0, 0)),
                  pl.BlockSpec((1, width), lambda i: (0, 0)), half, half],
        out_specs=[pl.BlockSpec((tr, LR_PAD), lambda i: (i, 0)),
                   pl.BlockSpec((LR_PAD, width), lambda i: (0, 0)),
                   pl.BlockSpec((1, width), lambda i: (0, 0))],
        out_shape=[jax.ShapeDtypeStruct((t, LR_PAD), BF16),
                   jax.ShapeDtypeStruct((LR_PAD, width), F32),
                   jax.ShapeDtypeStruct((1, width), F32)],
        compiler_params=_params(("arbitrary",), VMEM_LIMIT),
    )(proj_lr, w2, ba2, dg_f, dg_b)


def _gla_k(h):
    return slice(h * GLA_DK, (h + 1) * GLA_DK)


def _gla_v(h):
    return slice(h * GLA_DV, (h + 1) * GLA_DV)


def _running_sum(x, downward):
    n = x.shape[0]
    row = lax.broadcasted_iota(jnp.int32, x.shape, 0)
    step = 1
    while step < n:
        if downward:
            x = x + jnp.where(row >= step, pltpu.roll(x, step, 0), 0.0)
        else:
            x = x + jnp.where(row < n - step, pltpu.roll(x, n - step, 0), 0.0)
        step *= 2
    return x


@functools.partial(jax.custom_vjp, nondiff_argnums=(1,))
def _cumsum_rows(x, downward):
    return _running_sum(x, downward)


def _cumsum_rows_fwd(x, downward):
    return _running_sum(x, downward), None


def _cumsum_rows_bwd(downward, _, ct):
    return (_running_sum(ct, not downward),)


_cumsum_rows.defvjp(_cumsum_rows_fwd, _cumsum_rows_bwd)


def _gla_chunk(q, k, v, g, state, forward):
    c = GLA_CHUNK
    row = lax.broadcasted_iota(jnp.int32, (c, c), 0)
    col = lax.broadcasted_iota(jnp.int32, (c, c), 1)
    rid = lax.broadcasted_iota(jnp.int32, (c, GLA_DK), 0)
    q = q * (GLA_DK ** -0.5)
    if forward:
        see = row >= col
        upto_ref = rid <= c // 2
    else:
        see = row < col
        upto_ref = rid >= c - 1 - c // 2
    b = _cumsum_rows(g, forward)
    b_last = jnp.sum(g, axis=0, keepdims=True)
    b_ref = jnp.sum(jnp.where(upto_ref, g, 0.0), axis=0, keepdims=True)
    a = bdot(q * jnp.exp(b - b_ref), k * jnp.exp(b_ref - b), "nt")
    a = jnp.where(see, a, 0.0)
    o = bdot(a, v, "nn") + bdot(q * jnp.exp(b), state, "nt")
    new_state = state * jnp.exp(b_last) + bdot(v, k * jnp.exp(b_last - b), "tn")
    return o, new_state


def _gla_fwd(proj, g):
    t = proj.shape[0]
    c, per = GLA_CHUNK, GLA_PER_STEP
    nchunk = t // c
    nstep = nchunk // per
    qb, kb, vb = _seg_block("qg", GLA_KEY_WIDTH), _seg_block("kg", GLA_KEY_WIDTH), _seg_block("vg", GLA_WIDTH)

    def body(qf, kf, vf, gf, qr, kr, vr, gr, of_ref, ob_ref, sf_ref, sb_ref, state):
        @pl.when(pl.program_id(0) == 0)
        def _():
            state[...] = jnp.zeros_like(state)

        dirs = ((qf, kf, vf, gf, of_ref, sf_ref), (qr, kr, vr, gr, ob_ref, sb_ref))
        states = [[state[d, h] for h in range(GLA_HEADS)] for d in range(2)]
        for turn in range(per):
            chunk = (turn, per - 1 - turn)
            args = []
            for d, (q_ref, k_ref, v_ref, g_ref, _, _) in enumerate(dirs):
                rows = pl.ds(chunk[d] * c, c)
                args += [(q_ref[rows, _gla_k(h)], k_ref[rows, _gla_k(h)], v_ref[rows, _gla_v(h)],
                          g_ref[rows, _gla_k(h)], states[d][h]) for h in range(GLA_HEADS)]
            results = [_gla_chunk(*a, forward=(i < GLA_HEADS)) for i, a in enumerate(args)]
            for i, (a, (o, s_out)) in enumerate(zip(args, results)):
                d, h = divmod(i, GLA_HEADS)
                dirs[d][5][chunk[d], h] = a[4]
                dirs[d][4][pl.ds(chunk[d] * c, c), _gla_v(h)] = o
                states[d][h] = s_out
        for d in range(2):
            for h in range(GLA_HEADS):
                state[d, h] = states[d][h]

    specs, outs = [], []
    for d in range(2):
        ci = (lambda i: i) if d == 0 else (lambda i: nstep - 1 - i)
        specs += [pl.BlockSpec((per * c, GLA_KEY_WIDTH), lambda i, ci=ci: (ci(i), qb)),
                  pl.BlockSpec((per * c, GLA_KEY_WIDTH), lambda i, ci=ci: (ci(i), kb)),
                  pl.BlockSpec((per * c, GLA_WIDTH), lambda i, ci=ci: (ci(i), vb)),
                  pl.BlockSpec((per * c, GLA_KEY_WIDTH), lambda i, ci=ci, d=d: (ci(i), d))]
        outs.append(pl.BlockSpec((per * c, GLA_WIDTH), lambda i, ci=ci: (ci(i), 0)))
    for d in range(2):
        ci = (lambda i: i) if d == 0 else (lambda i: nstep - 1 - i)
        outs.append(pl.BlockSpec((per, GLA_HEADS, GLA_DV, GLA_DK), lambda i, ci=ci: (ci(i), 0, 0, 0)))
    o_shape = jax.ShapeDtypeStruct((t, GLA_WIDTH), F32)
    s_shape = jax.ShapeDtypeStruct((nchunk, GLA_HEADS, GLA_DV, GLA_DK), F32)
    return pl.pallas_call(
        body, name="gla_fwd", grid=(nstep,),
        in_specs=specs, out_specs=outs,
        out_shape=[o_shape, o_shape, s_shape, s_shape],
        scratch_shapes=[pltpu.VMEM((2, GLA_HEADS, GLA_DV, GLA_DK), F32)],
        compiler_params=_params(("arbitrary",), VMEM_LIMIT),
    )(proj, proj, proj, g, proj, proj, proj, g)


def _gla_bwd(proj, g, s_f, s_b, do):
    t = proj.shape[0]
    c, per = GLA_CHUNK, GLA_PER_STEP
    nchunk = t // c
    nstep = nchunk // per
    qb, kb, vb = _seg_block("qg", GLA_KEY_WIDTH), _seg_block("kg", GLA_KEY_WIDTH), _seg_block("vg", GLA_WIDTH)

    def body(*refs):
        ins, outs, dstate = refs[:12], refs[12:20], refs[20]

        @pl.when(pl.program_id(0) == 0)
        def _():
            dstate[...] = jnp.zeros_like(dstate)

        dstates = [[dstate[d, h] for h in range(GLA_HEADS)] for d in range(2)]
        for turn in range(per):
            chunk = (per - 1 - turn, turn)
            loaded = []
            for d in range(2):
                q_ref, k_ref, v_ref, g_ref, s_ref, do_ref = ins[6 * d:6 * d + 6]
                rows = pl.ds(chunk[d] * c, c)
                for h in range(GLA_HEADS):
                    loaded.append(((q_ref[rows, _gla_k(h)], k_ref[rows, _gla_k(h)], v_ref[rows, _gla_v(h)],
                                    g_ref[rows, _gla_k(h)], s_ref[chunk[d], h]),
                                   (do_ref[rows, _gla_v(h)], dstates[d][h])))
            grads = []
            for i, (primals, cotangents) in enumerate(loaded):
                _, vjp = jax.vjp(functools.partial(_gla_chunk, forward=(i < GLA_HEADS)), *primals)
                grads.append(vjp(cotangents))
            for i, (dq, dk, dv, dg, ds) in enumerate(grads):
                d, h = divmod(i, GLA_HEADS)
                rows = pl.ds(chunk[d] * c, c)
                dq_ref, dk_ref, dv_ref, dg_ref = outs[4 * d:4 * d + 4]
                dq_ref[rows, _gla_k(h)] = dq
                dk_ref[rows, _gla_k(h)] = dk
                dv_ref[rows, _gla_v(h)] = dv
                dg_ref[rows, _gla_k(h)] = dg
                dstates[d][h] = ds
        for d in range(2):
            for h in range(GLA_HEADS):
                dstate[d, h] = dstates[d][h]

    specs, outs, shapes = [], [], []
    for d in range(2):
        ci = (lambda i: nstep - 1 - i) if d == 0 else (lambda i: i)
        specs += [pl.BlockSpec((per * c, GLA_KEY_WIDTH), lambda i, ci=ci: (ci(i), qb)),
                  pl.BlockSpec((per * c, GLA_KEY_WIDTH), lambda i, ci=ci: (ci(i), kb)),
                  pl.BlockSpec((per * c, GLA_WIDTH), lambda i, ci=ci: (ci(i), vb)),
                  pl.BlockSpec((per * c, GLA_KEY_WIDTH), lambda i, ci=ci, d=d: (ci(i), d)),
                  pl.BlockSpec((per, GLA_HEADS, GLA_DV, GLA_DK), lambda i, ci=ci: (ci(i), 0, 0, 0)),
                  pl.BlockSpec((per * c, GLA_WIDTH), lambda i, ci=ci: (ci(i), 0))]
        key = pl.BlockSpec((per * c, GLA_KEY_WIDTH), lambda i, ci=ci: (ci(i), 0))
        val = pl.BlockSpec((per * c, GLA_WIDTH), lambda i, ci=ci: (ci(i), 0))
        outs += [key, key, val, key]
        shapes += [jax.ShapeDtypeStruct((t, GLA_KEY_WIDTH), F32), jax.ShapeDtypeStruct((t, GLA_KEY_WIDTH), F32),
                   jax.ShapeDtypeStruct((t, GLA_WIDTH), F32), jax.ShapeDtypeStruct((t, GLA_KEY_WIDTH), F32)]
    return pl.pallas_call(
        body, name="gla_bwd", grid=(nstep,),
        in_specs=specs, out_specs=outs, out_shape=shapes,
        scratch_shapes=[pltpu.VMEM((2, GLA_HEADS, GLA_DV, GLA_DK), F32)],
        compiler_params=_params(("arbitrary",), VMEM_LIMIT),
    )(proj, proj, proj, g, s_f, do, proj, proj, proj, g, s_b, do)


def _gla_out_head(o_f, o_b, gate, gn):
    return _rms(o_f + o_b, gn) * _silu(gate)


def _gla_out_fwd(o_f, o_b, proj, gn, *, tr=512):
    t = o_f.shape[0]
    gb = _seg_block("gate", GLA_WIDTH)

    def body(of_ref, ob_ref, gate_ref, gn_ref, out_ref):
        for h in range(GLA_HEADS):
            vc = slice(h * GLA_DV, (h + 1) * GLA_DV)
            out_ref[:, vc] = _gla_out_head(of_ref[:, vc], ob_ref[:, vc], gate_ref[:, vc], gn_ref[...]).astype(BF16)

    wide = pl.BlockSpec((tr, GLA_WIDTH), lambda i: (i, 0))
    return pl.pallas_call(
        body, name="gla_out_fwd", grid=(t // tr,),
        in_specs=[wide, wide, pl.BlockSpec((tr, GLA_WIDTH), lambda i: (i, gb)),
                  pl.BlockSpec((1, GLA_DV), lambda i: (0, 0))],
        out_specs=wide,
        out_shape=jax.ShapeDtypeStruct((t, GLA_WIDTH), BF16),
        compiler_params=_params(("parallel",), VMEM_LIMIT),
    )(o_f, o_b, proj, gn)


def _gla_out_bwd(o_f, o_b, proj, gn, dmix, *, tr=512):
    t = o_f.shape[0]
    gb = _seg_block("gate", GLA_WIDTH)

    def body(of_ref, ob_ref, gate_ref, gn_ref, dout_ref, do_ref, dgate_ref, dgn_ref):
        dgn = jnp.zeros((1, GLA_DV), F32)
        for h in range(GLA_HEADS):
            vc = slice(h * GLA_DV, (h + 1) * GLA_DV)
            _, vjp = jax.vjp(_gla_out_head, of_ref[:, vc], ob_ref[:, vc], gate_ref[:, vc], gn_ref[...])
            do, _, dgate, dg = vjp(dout_ref[:, vc])
            do_ref[:, vc] = do
            dgate_ref[:, vc] = dgate.astype(BF16)
            dgn = dgn + dg

        @pl.when(pl.program_id(0) == 0)
        def _():
            dgn_ref[...] = jnp.zeros_like(dgn_ref)

        dgn_ref[...] += dgn

    wide = pl.BlockSpec((tr, GLA_WIDTH), lambda i: (i, 0))
    vec = pl.BlockSpec((1, GLA_DV), lambda i: (0, 0))
    return pl.pallas_call(
        body, name="gla_out_bwd", grid=(t // tr,),
        in_specs=[wide, wide, pl.BlockSpec((tr, GLA_WIDTH), lambda i: (i, gb)), vec,
                  pl.BlockSpec((tr, GLA_WIDTH), lambda i: (i, 1))],
        out_specs=[wide, wide, vec],
        out_shape=[jax.ShapeDtypeStruct((t, GLA_WIDTH), F32), jax.ShapeDtypeStruct((t, GLA_WIDTH), BF16),
                   jax.ShapeDtypeStruct((1, GLA_DV), F32)],
        compiler_params=_params(("arbitrary",), VMEM_LIMIT),
    )(o_f, o_b, proj, gn, dmix)


CONV_TR = 1024
CONV_TC = 512
HALO = 8
HALO16 = 16


def _conv3(u, w, b):
    n = u.shape[0]
    return pltpu.roll(u, 1, 0) * w[0:1] + u * w[1:2] + pltpu.roll(u, n - 1, 0) * w[2:3] + b


def _conv_ext(main_ref, prev_ref, next_ref, r, nr):
    prev = prev_ref[...].astype(F32)[-HALO:] * (r > 0).astype(F32)
    nxt = next_ref[...].astype(F32)[:HALO] * (r < nr - 1).astype(F32)
    return jnp.concatenate([prev, main_ref[...].astype(F32), nxt], axis=0)


def _conv_specs(t, halo, half=None):
    per = CONV_TR // halo
    last = t // halo - 1
    lead = () if half is None else (None,)
    at = (lambda *ix: ix) if half is None else (lambda *ix: (half,) + ix)
    return [pl.BlockSpec(lead + (CONV_TR, CONV_TC), lambda j, r: at(r, j)),
            pl.BlockSpec(lead + (halo, CONV_TC), lambda j, r: at(jnp.maximum(r * per - 1, 0), j)),
            pl.BlockSpec(lead + (halo, CONV_TC), lambda j, r: at(jnp.minimum((r + 1) * per, last), j))]


def _ffn_mid_fwd(u, cw_g, cw_v, cb_g, cb_v):
    _, t, f = u.shape
    nr = t // CONV_TR

    def body(ug, ugp, ugn, uv, uvp, uvn, wg, wv, bg, bv, a_ref):
        r = pl.program_id(1)
        gate = _conv3(_conv_ext(ug, ugp, ugn, r, nr), wg[...], bg[...])[HALO:HALO + CONV_TR]
        val = _conv3(_conv_ext(uv, uvp, uvn, r, nr), wv[...], bv[...])[HALO:HALO + CONV_TR]
        a_ref[...] = (_silu(gate) * val).astype(BF16)

    w_spec = pl.BlockSpec((3, CONV_TC), lambda j, r: (0, j))
    b_spec = pl.BlockSpec((1, CONV_TC), lambda j, r: (0, j))
    return pl.pallas_call(
        body, name="ffn_mid_fwd", grid=(f // CONV_TC, nr),
        in_specs=_conv_specs(t, HALO, 0) + _conv_specs(t, HALO, 1) + [w_spec, w_spec, b_spec, b_spec],
        out_specs=pl.BlockSpec((CONV_TR, CONV_TC), lambda j, r: (r, j)),
        out_shape=jax.ShapeDtypeStruct((t, f), BF16),
        compiler_params=_params(("parallel", "parallel"), VMEM_LIMIT),
    )(u, u, u, u, u, u, cw_g, cw_v, cb_g, cb_v)


def _ffn_mid_bwd(u, cw_g, cw_v, cb_g, cb_v, da):
    _, t, f = u.shape
    nr = t // CONV_TR
    ext = CONV_TR + 2 * HALO

    def body(ug, ugp, ugn, uv, uvp, uvn, dam, dap, dan, wg, wv, bg, bv,
             du_ref, dwg_ref, dwv_ref, dbg_ref, dbv_ref):
        r = pl.program_id(1)
        shifted = []
        for main, prev, nxt in ((ug, ugp, ugn), (uv, uvp, uvn)):
            x = _conv_ext(main, prev, nxt, r, nr)
            shifted.append((pltpu.roll(x, 1, 0), x, pltpu.roll(x, ext - 1, 0)))
        da_x = _conv_ext(dam, dap, dan, r, nr)
        wg_t, wv_t = wg[...], wv[...]
        gate = shifted[0][0] * wg_t[0:1] + shifted[0][1] * wg_t[1:2] + shifted[0][2] * wg_t[2:3] + bg[...]
        val = shifted[1][0] * wv_t[0:1] + shifted[1][1] * wv_t[1:2] + shifted[1][2] * wv_t[2:3] + bv[...]
        sig = jax.nn.sigmoid(gate)
        silu = gate * sig
        d_val = da_x * silu
        d_gate = da_x * val * (sig + silu * (1.0 - sig))
        own = slice(HALO, HALO + CONV_TR)
        for half, (xs3, d, wt, dw_ref, db_ref) in enumerate(((shifted[0], d_gate, wg_t, dwg_ref, dbg_ref),
                                                            (shifted[1], d_val, wv_t, dwv_ref, dbv_ref))):
            du = pltpu.roll(d, ext - 1, 0) * wt[0:1] + d * wt[1:2] + pltpu.roll(d, 1, 0) * wt[2:3]
            du_ref[half] = du[own].astype(BF16)
            d_own = d[own]
            dw = jnp.concatenate([jnp.sum(x[own] * d_own, axis=0, keepdims=True) for x in xs3], axis=0)
            db = jnp.sum(d_own, axis=0, keepdims=True)

            @pl.when(r == 0)
            def _():
                dw_ref[...] = jnp.zeros_like(dw_ref)
                db_ref[...] = jnp.zeros_like(db_ref)

            dw_ref[...] += dw
            db_ref[...] += db

    w_spec = pl.BlockSpec((3, CONV_TC), lambda j, r: (0, j))
    b_spec = pl.BlockSpec((1, CONV_TC), lambda j, r: (0, j))
    return pl.pallas_call(
        body, name="ffn_mid_bwd", grid=(f // CONV_TC, nr),
        in_specs=(_conv_specs(t, HALO, 0) + _conv_specs(t, HALO, 1) + _conv_specs(t, HALO16)
                  + [w_spec, w_spec, b_spec, b_spec]),
        out_specs=[pl.BlockSpec((2, CONV_TR, CONV_TC), lambda j, r: (0, r, j)), w_spec, w_spec, b_spec, b_spec],
        out_shape=[jax.ShapeDtypeStruct((2, t, f), BF16),
                   jax.ShapeDtypeStruct((3, f), F32), jax.ShapeDtypeStruct((3, f), F32),
                   jax.ShapeDtypeStruct((1, f), F32), jax.ShapeDtypeStruct((1, f), F32)],
        compiler_params=_params(("parallel", "arbitrary"), VMEM_LIMIT),
    )(u, u, u, u, u, u, da, da, da, cw_g, cw_v, cb_g, cb_v)


def _down_proj_loss(act, w_down, x1, target, *, tm, tn):
    t, f = act.shape
    d = w_down.shape[1]

    def body(a_ref, w_ref, x_ref, t_ref, loss_ref, dy_ref, dyb_ref):
        y = lax.dot_general(a_ref[...], w_ref[...], _DIMS["nn"], preferred_element_type=F32) + x_ref[...]
        err = y - t_ref[...]
        dy = err * (1.0 / d)
        dy_ref[...] = dy
        dyb_ref[...] = dy.astype(BF16)
        part = 0.5 * jnp.sum(jnp.sum(err * err, axis=-1, keepdims=True) * (1.0 / d), axis=0, keepdims=True)

        @pl.when((pl.program_id(0) == 0) & (pl.program_id(1) == 0))
        def _():
            loss_ref[...] = jnp.zeros_like(loss_ref)

        loss_ref[...] += jnp.broadcast_to(part, loss_ref.shape)

    tile = pl.BlockSpec((tm, tn), lambda i, j: (i, j))
    return pl.pallas_call(
        body, name="down_proj_loss", grid=(t // tm, d // tn),
        in_specs=[pl.BlockSpec((tm, f), lambda i, j: (i, 0)), pl.BlockSpec((f, tn), lambda i, j: (0, j)), tile, tile],
        out_specs=[pl.BlockSpec((1, 128), lambda i, j: (0, 0)), tile, tile],
        out_shape=[jax.ShapeDtypeStruct((1, 128), F32), jax.ShapeDtypeStruct((t, d), F32),
                   jax.ShapeDtypeStruct((t, d), BF16)],
        compiler_params=_params(("arbitrary", "arbitrary"), VMEM_LIMIT),
    )(act, w_down, x1, target)


ANY = pl.BlockSpec(memory_space=pl.ANY)


def _position():
    return lax.axis_index("x"), lax.axis_index("y"), lax.axis_index("c")


def _other_chips(x, y):
    return [(1 - x, y), (x, 1 - y), (1 - x, 1 - y)]


def _handshake(peers):
    barrier = pltpu.get_barrier_semaphore()
    for peer in peers:
        pl.semaphore_signal(barrier, inc=1, device_id=peer, device_id_type=MESH)
    pl.semaphore_wait(barrier, len(peers))


def _exchange(body, operands, out_shapes, sems, *, name, collective_id):
    n_in, n_out = len(operands), len(out_shapes)

    def run(*refs):
        body(refs[:n_in], refs[n_in:n_in + n_out], *refs[n_in + n_out:])

    if collective_id is None:
        return pl.pallas_call(run, name=name, in_specs=[ANY] * n_in, out_specs=[ANY] * n_out,
                              out_shape=out_shapes, scratch_shapes=sems)(*operands)
    return pl.kernel(run, name=name, out_type=out_shapes,
                     mesh=plsc.ScalarSubcoreMesh(axis_name="sequencer", num_cores=1), scratch_types=sems,
                     compiler_params=pltpu.CompilerParams(collective_id=collective_id))(*operands)


def _all_gather(blocks, *, name, collective_id=None):
    na = len(blocks)

    def body(ins, outs, send_sems, recv_sems, local_sems):
        x, y, c = _position()
        me, sibling = (x, y, c), (x, y, 1 - c)
        along_x, along_y, diagonal = (1 - x, y, c), (x, 1 - y, c), (1 - x, 1 - y, c)
        first_c = c == 0
        relay_from = (jnp.where(first_c, x, 1 - x), jnp.where(first_c, 1 - y, y), c)
        relay_to = (jnp.where(first_c, 1 - x, x), jnp.where(first_c, y, 1 - y), c)
        if collective_id is not None:
            _handshake([sibling, along_x, along_y])

        def index(px, py, pc):
            return 4 * px + 2 * py + pc

        def copy(a, k, block, to, src=None):
            dst = outs[a].at[index(*block)]
            return pltpu.make_async_remote_copy(
                src_ref=dst if src is None else src, dst_ref=dst,
                send_sem=send_sems.at[a, k], recv_sem=recv_sems.at[a, k],
                device_id=to, device_id_type=MESH)

        pending = []
        for a in range(na):
            mine = pltpu.make_async_copy(ins[a], outs[a].at[index(*me)], local_sems.at[a])
            mine.start()
            pending.append(mine)
        sent = []
        for a in range(na):
            sent += [copy(a, 0, me, sibling, src=ins[a]), copy(a, 1, me, along_x, src=ins[a]),
                     copy(a, 2, me, along_y, src=ins[a])]
        for cp in sent:
            cp.start()

        def passes_on(k_in, owner, k_out):
            for a in range(na):
                copy(a, k_in, owner, me).wait_recv()
                cp = copy(a, k_out, owner, sibling)
                cp.start()
                sent.append(cp)

        passes_on(1, along_x, 4)
        passes_on(2, along_y, 5)
        for a in range(na):
            cp = copy(a, 3, relay_from, relay_to)
            cp.start()
            sent.append(cp)
        passes_on(3, diagonal, 6)
        for a in range(na):
            copy(a, 0, sibling, me).wait_recv()
            for k, owner in ((4, along_x), (5, along_y), (6, diagonal)):
                copy(a, k, (owner[0], owner[1], 1 - c), me).wait_recv()
        for cp in sent:
            cp.wait_send()
        for cp in pending:
            cp.wait()

    return _exchange(
        body, blocks, [jax.ShapeDtypeStruct((N_DEV,) + b.shape, b.dtype) for b in blocks],
        [pltpu.SemaphoreType.DMA((na, 7)), pltpu.SemaphoreType.DMA((na, 7)), pltpu.SemaphoreType.DMA((na,))],
        name=name, collective_id=collective_id)


def _grad_exchange(grads, parts, *, name, collective_id):
    ng, npart = len(grads), len(parts)

    def body(ins, outs, core_send, core_recv, chip_send, chip_recv, local_sems):
        x, y, c = _position()
        sibling = (x, y, 1 - c)
        chips = _other_chips(x, y)
        _handshake([sibling] + [(px, py, c) for px, py in chips])
        me = 2 * x + y
        copies = []
        for b in range(npart):
            src, dst = ins[ng + b], outs[ng + b]
            own = pltpu.make_async_copy(src.at[me], dst.at[me], local_sems.at[b])
            own.start()
            copies.append(own)
            for j, (px, py) in enumerate(chips):
                cp = pltpu.make_async_remote_copy(
                    src_ref=src.at[2 * px + py], dst_ref=dst.at[me],
                    send_sem=chip_send.at[b, j], recv_sem=chip_recv.at[b, j],
                    device_id=(px, py, c), device_id_type=MESH)
                cp.start()
                copies.append(cp)
        for a in range(ng):
            for k in range(N_CHIP):
                cp = pltpu.make_async_remote_copy(
                    src_ref=ins[a].at[k, 1 - c], dst_ref=outs[a].at[k],
                    send_sem=core_send.at[a, k], recv_sem=core_recv.at[a, k],
                    device_id=sibling, device_id_type=MESH)
                cp.start()
                copies.append(cp)
        for cp in copies:
            cp.wait()

    shapes = ([jax.ShapeDtypeStruct((N_CHIP,) + g.shape[2:], g.dtype) for g in grads]
              + [jax.ShapeDtypeStruct(p.shape, p.dtype) for p in parts])
    sems = [pltpu.SemaphoreType.DMA((max(ng, 1), N_CHIP)), pltpu.SemaphoreType.DMA((max(ng, 1), N_CHIP)),
            pltpu.SemaphoreType.DMA((max(npart, 1), 3)), pltpu.SemaphoreType.DMA((max(npart, 1), 3)),
            pltpu.SemaphoreType.DMA((max(npart, 1),))]
    out = _exchange(body, list(grads) + list(parts), shapes, sems, name=name, collective_id=collective_id)
    return out[:ng], out[ng:]


def _pair_sum(grad, theirs, core, *, tile, name, narrow=False):
    _, _, r, w = grad.shape
    tr, tw = tile
    assert r % tr == 0 and w % tw == 0

    def body(core_ref, mine_ref, theirs_ref, out_ref, *narrow_ref):
        total = mine_ref[...] + theirs_ref[...]
        out_ref[...] = total
        if narrow:
            narrow_ref[0][...] = total.astype(BF16)

    spec = pl.BlockSpec((None, tr, tw), lambda k, i, j, core_ref: (k, i, j))
    shapes = [jax.ShapeDtypeStruct((N_CHIP, r, w), F32)] + [jax.ShapeDtypeStruct((N_CHIP, r, w), BF16)] * narrow
    out = pl.pallas_call(
        body, name=name,
        grid_spec=pltpu.PrefetchScalarGridSpec(
            num_scalar_prefetch=1, grid=(N_CHIP, r // tr, w // tw),
            in_specs=[pl.BlockSpec((None, None, tr, tw), lambda k, i, j, core_ref: (k, core_ref[0], i, j)), spec],
            out_specs=[spec] * len(shapes)),
        out_shape=shapes,
        compiler_params=_params(("parallel", "parallel", "parallel"), VMEM_LIMIT),
    )(core, grad, theirs)
    return tuple(out) if narrow else out[0]


class _ReduceScatter:
    def __init__(self, core):
        self.core = core
        self.pending = None
        self.results = {}
        self.launches = 0

    def push(self, tag, grads, rows, then, narrow=False):
        pair, kept, prev_tag = [], [], None
        if self.pending is not None:
            prev_tag, prev, theirs, prev_rows, prev_narrow = self.pending
            pair = [_pair_sum(g, s, self.core, tile=tile, name=f"pair_sum_{prev_tag}_{i}", narrow=prev_narrow)
                    for i, (g, s, tile) in enumerate(zip(prev, theirs, prev_rows))]
            if prev_narrow:
                kept, pair = [p[0] for p in pair], [p[1] for p in pair]
        grads, pair, then = lax.optimization_barrier((list(grads), pair, then))
        grads = [g.reshape((N_CHIP, 2) + g.shape[1:]) for g in grads]
        self.launches += 1
        theirs, parts = _grad_exchange(grads, pair, name=f"grad_exchange_{self.launches}",
                                       collective_id=1 + self.launches)
        if prev_tag is not None:
            self.results[prev_tag] = (parts, kept)
        self.pending = (tag, grads, theirs, rows, narrow) if tag is not None else None
        return then

    def result(self, tag):
        return self.results[tag]


def _adamw(parts, w, m, v, *, tile, name, own=None, chip=None):
    n, r, cols = parts.shape
    tr, tw = tile
    assert r % tr == 0 and cols % tw == 0 and w.shape == (r, cols)
    c1 = 1.0 - ADAM_B1 ** ADAM_STEP
    c2 = 1.0 - ADAM_B2 ** ADAM_STEP

    def update(g, w_ref, m_ref, v_ref, g_ref, d_ref, nm_ref, nv_ref):
        new_m = ADAM_B1 * m_ref[...] + (1.0 - ADAM_B1) * g
        new_v = ADAM_B2 * v_ref[...] + (1.0 - ADAM_B2) * (g * g)
        m_hat = new_m / c1
        v_hat = new_v / c2
        g_ref[...] = g
        d_ref[...] = -ADAM_LR * (m_hat / (jnp.sqrt(v_hat) + ADAM_EPS) + ADAM_WD * w_ref[...])
        nm_ref[...] = new_m
        nv_ref[...] = new_v

    shape = jax.ShapeDtypeStruct((r, cols), F32)
    if own is None:
        def body(p_ref, *refs):
            g = p_ref[0]
            for k in range(1, n):
                g = g + p_ref[k]
            update(g, *refs)

        spec = pl.BlockSpec((tr, tw), lambda i, j: (i, j))
        return pl.pallas_call(
            body, name=name, grid=(r // tr, cols // tw),
            in_specs=[pl.BlockSpec((n, tr, tw), lambda i, j: (0, i, j)), spec, spec, spec],
            out_specs=[spec] * 4, out_shape=[shape] * 4,
            compiler_params=_params(("parallel", "parallel"), VMEM_LIMIT),
        )(parts, w, m, v)

    def body(chip_ref, p_ref, own_ref, *refs):
        g = None
        for k in range(n):
            term = jnp.where(chip_ref[0] == k, own_ref[...], p_ref[k].astype(F32))
            g = term if g is None else g + term
        update(g, *refs)

    spec = pl.BlockSpec((tr, tw), lambda i, j, chip_ref: (i, j))
    return pl.pallas_call(
        body, name=name,
        grid_spec=pltpu.PrefetchScalarGridSpec(
            num_scalar_prefetch=1, grid=(r // tr, cols // tw),
            in_specs=[pl.BlockSpec((n, tr, tw), lambda i, j, chip_ref: (0, i, j)),
                      pl.BlockSpec((None, tr, tw), lambda i, j, chip_ref: (chip_ref[0], i, j)), spec, spec, spec],
            out_specs=[spec] * 4),
        out_shape=[shape] * 4,
        compiler_params=_params(("parallel", "parallel"), VMEM_LIMIT),
    )(chip, parts, own, w, m, v)


LANES = 128


def _row_offsets(pieces):
    offsets, row = [], 0
    for p in pieces:
        assert p.shape[0] == 1 and p.shape[1] % LANES == 0, p.shape
        offsets.append(row)
        row += p.shape[1] // LANES
    return offsets, row


def _pack_rows(pieces):
    offsets, rows = _row_offsets(pieces)

    def body(*refs):
        out_ref = refs[-1]
        for ref, start in zip(refs[:-1], offsets):
            for j in range(ref.shape[1] // LANES):
                out_ref[start + j:start + j + 1, :] = ref[:, j * LANES:(j + 1) * LANES]

    return pl.pallas_call(body, name="pack_small_grads",
                          out_shape=jax.ShapeDtypeStruct((rows, LANES), F32))(*pieces)


def _adamw_rows(terms, ws, ms, vs):
    n_dev, rows, _ = terms.shape
    offsets, used = _row_offsets(ws)
    assert used + 1 == rows
    c1 = 1.0 - ADAM_B1 ** ADAM_STEP
    c2 = 1.0 - ADAM_B2 ** ADAM_STEP
    nw = len(ws)

    def body(*refs):
        t_ref = refs[0]
        w_refs, m_refs, v_refs = refs[1:1 + nw], refs[1 + nw:1 + 2 * nw], refs[1 + 2 * nw:1 + 3 * nw]
        outs = refs[1 + 3 * nw:]
        total = t_ref[0]
        for k in range(1, n_dev):
            total = total + t_ref[k]
        for i, start in enumerate(offsets):
            for j in range(ws[i].shape[1] // LANES):
                lanes = slice(j * LANES, (j + 1) * LANES)
                g = total[start + j:start + j + 1, :]
                new_m = ADAM_B1 * m_refs[i][:, lanes] + (1.0 - ADAM_B1) * g
                new_v = ADAM_B2 * v_refs[i][:, lanes] + (1.0 - ADAM_B2) * (g * g)
                delta = -ADAM_LR * ((new_m / c1) / (jnp.sqrt(new_v / c2) + ADAM_EPS) + ADAM_WD * w_refs[i][:, lanes])
                for kind, value in enumerate((g, delta, new_m, new_v)):
                    outs[kind * nw + i][:, lanes] = value
        outs[-1][...] = total[used:used + 1, :]

    shapes = [jax.ShapeDtypeStruct(w.shape, F32) for w in ws] * 4 + [jax.ShapeDtypeStruct((1, LANES), F32)]
    out = pl.pallas_call(body, name="adamw_replicated", out_shape=shapes,
                         compiler_params=_params(None, VMEM_LIMIT))(terms, *ws, *ms, *vs)
    return [list(out[kind * nw:(kind + 1) * nw]) for kind in range(4)], out[-1]


def _rope_tables(t):
    half = HEAD_DIM // 2
    inv = 1.0 / (ROPE_THETA ** (jnp.arange(half, dtype=F32) / half))
    ang = jnp.arange(t, dtype=jnp.int32).astype(F32)[:, None] * inv[None, :]
    cos, sin = jnp.cos(ang), jnp.sin(ang)
    return jnp.concatenate([cos, cos], axis=1), jnp.concatenate([-sin, sin], axis=1)


IN_KERNEL = IN_MAIN + LR_PAD


def _to_kernel_rows(w_t):
    order = sorted(SEGMENTS.values())
    pad = jnp.zeros((LR_PAD - (IN_TOTAL - IN_MAIN), w_t.shape[1]), w_t.dtype)
    return jnp.concatenate([w_t[src:src + width] for _, src, width in order] + [w_t[IN_MAIN:IN_TOTAL], pad], axis=0)


CONV_TAPS = 3
WA_BLOCK = GLA_KEY_WIDTH // N_DEV
SMALL_SIZES = (CONV_TAPS * UP_BLOCK, GLA_RANK * WA_BLOCK, GLA_RANK * WA_BLOCK)
SMALL_SHAPES = ((CONV_TAPS, UP_BLOCK), (GLA_RANK, WA_BLOCK), (GLA_RANK, WA_BLOCK))
SMALL_ROWS = sum(SMALL_SIZES) // LANES


def _small_block(conv, wa_f, wa_b):
    return jnp.concatenate([conv.reshape(-1), wa_f.reshape(-1), wa_b.reshape(-1)]).reshape(SMALL_ROWS, LANES)


def _small_unblock(block):
    flat, out, off = block.reshape(-1), [], 0
    for size, shape in zip(SMALL_SIZES, SMALL_SHAPES):
        out.append(flat[off:off + size].reshape(shape))
        off += size
    return out


def _small_blocks(conv_full, wa_f_full, wa_b_full):
    def by_device(a, width):
        return jnp.transpose(a.reshape(a.shape[0], N_DEV, width), (1, 0, 2)).reshape(N_DEV, -1)
    return jnp.concatenate([by_device(conv_full, UP_BLOCK), by_device(wa_f_full, WA_BLOCK),
                            by_device(wa_b_full, WA_BLOCK)], axis=1).reshape(N_DEV, SMALL_ROWS, LANES)


def _small_unblocks(blocks):
    flat, out, off = blocks.reshape(N_DEV, -1), [], 0
    for size, (rows, width) in zip(SMALL_SIZES, SMALL_SHAPES):
        part = flat[:, off:off + size].reshape(N_DEV, rows, width)
        out.append(jnp.transpose(part, (1, 0, 2)).reshape(rows, N_DEV * width))
        off += size
    return out


def _local_step(xs, target, norm1_g, w_in_k, gq, gk, attn_sink, w2, ba2, gla_norm_g, w_out_full, norm2_g,
                w_up8, cw_g, cw_v, cb_g, cb_v, w_down_full, rs=None):
    t = xs.shape[0]
    tm = min(1024, t)
    tall = min(2048, t)
    cos, sin_signed = _rope_tables(t)
    sink = attn_sink.reshape(ATTN_HEADS)

    h1 = _rmsnorm_fwd(xs, norm1_g, name="norm1_fwd")
    proj = _matmul(h1, w_in_k, "nt", tm=tall, tn=IN_MAIN // 4, tk=D_MODEL, n_out=IN_MAIN, name="proj_main")
    proj_lr = _matmul(h1, w_in_k[IN_MAIN:], "nt", tm=tm, tn=LR_PAD, tk=D_MODEL, name="proj_lr")
    qa, ka, va = _attn_prep_fwd(proj, cos, sin_signed, gq, gk)
    o_attn = _attn_fwd(qa, ka, va, sink)
    g_dec = _gla_prep_fwd(proj_lr, w2, ba2)
    o_f, o_b, s_f, s_b = _gla_fwd(proj, g_dec)
    o_gla = _gla_out_fwd(o_f, o_b, proj, gla_norm_g)
    x1 = _out_proj(o_attn, o_gla, w_out_full, xs, tm=tall, tn=512)
    h2 = _rmsnorm_fwd(x1, norm2_g, name="norm2_fwd")
    u = _up_proj(h2, w_up8, tm=tm)
    act = _ffn_mid_fwd(u, cw_g, cw_v, cb_g, cb_v)
    loss_part, dy, dy_b = _down_proj_loss(act, w_down_full, x1, target, tm=tm, tn=512)

    d_act = _matmul(dy_b, w_down_full, "nt", tm=tall, tn=D_FF // 4, tk=D_MODEL, out_dtype=BF16, name="d_act")
    dw_down = _matmul(act, dy_b, "tn", tm=D_FF // 4, tn=512, tk=t, name="dw_down")
    if rs is not None:
        d_act = rs.push("w_down", [dw_down.reshape(N_DEV, D_FF // N_DEV, D_MODEL)], [(64, D_MODEL)], d_act)
    du, dcw_g, dcw_v, dcb_g, dcb_v = _ffn_mid_bwd(u, cw_g, cw_v, cb_g, cb_v, d_act)
    dw_up8 = _up_proj_dw(h2, du, tm=512, tk=t)
    if rs is not None:
        du = rs.push("w_up", [dw_up8], [(256, UP_BLOCK)], du)
    dh2 = _up_proj_dx(du, w_up8, tm=tm, tn=1024)
    dx1, dx1_b, d_norm2 = _rmsnorm_bwd(x1, norm2_g, dh2, dy, name="norm2_bwd", also_bf16=True)
    dmix = _matmul(dx1_b, w_out_full, "nt", tm=tall, tn=1024, tk=D_MODEL, name="d_mix")
    dw_out = _out_proj_dw(o_attn, o_gla, dx1_b, tn=512)
    if rs is not None:
        dmix = rs.push("w_out", [dw_out.reshape(N_DEV, D_MODEL // N_DEV, D_MODEL)], [(256, D_MODEL)], dmix)
    do_gla, d_gate, d_gla_norm = _gla_out_bwd(o_f, o_b, proj, gla_norm_g, dmix)
    (dq_f, dk_f, dv_f, dg_f, dq_b, dk_b, dv_b, dg_b) = _gla_bwd(proj, g_dec, s_f, s_b, do_gla)
    d_lr, d_w2, d_ba2 = _gla_prep_bwd(proj_lr, w2, ba2, dg_f, dg_b)
    dqa, dk_lo, dk_mid, dk_hi, dv_lo, dv_mid, dv_hi, d_sink8 = _attn_bwd(qa, ka, va, sink, dmix)
    d_qa, d_ka, d_va, d_qn, d_kn = _attn_prep_bwd(proj, cos, sin_signed, gq, gk, dqa,
                                                  (dk_lo, dk_mid, dk_hi), (dv_lo, dv_mid, dv_hi))
    d_seg = {"qa": d_qa, "gate": d_gate, "vg": (dv_f + dv_b).astype(BF16), "qg": (dq_f + dq_b).astype(BF16),
             "kg": (dk_f + dk_b).astype(BF16), "ka": d_ka, "va": d_va}
    d_proj = jnp.concatenate([d_seg[k] for k in sorted(SEGMENTS, key=lambda k: SEGMENTS[k][0])] + [d_lr], axis=1)
    dw_in_t = _in_proj_dw_lr(_in_proj_dw(d_proj, h1, tn=1024), d_lr, h1)
    if rs is not None:
        per_in = IN_TOTAL // N_DEV
        small_grad = _small_blocks(jnp.concatenate([dcw_g, dcw_v], axis=1), d_w2[:GLA_RANK, :GLA_KEY_WIDTH],
                                   d_w2[GLA_RANK:2 * GLA_RANK, GLA_KEY_WIDTH:])
        d_proj, d_lr = rs.push("w_in", [dw_in_t.reshape(N_DEV, per_in, D_MODEL), small_grad],
                               [(per_in, 512), small_grad.shape[1:]], (d_proj, d_lr), narrow=True)
    dh1 = _matmul(d_proj, w_in_k, "nn", tm=tm, tn=512, tk=IN_KERNEL, name="dh1")
    if rs is not None:
        dh1 = rs.push(None, [], [], dh1)
    grad_x, d_norm1 = _rmsnorm_bwd(xs, norm1_g, dh1, dx1, name="norm1_bwd", also_bf16=False)
    return (loss_part, grad_x, dw_in_t, dw_out, dw_up8, dw_down, dcw_g, dcw_v, dcb_g, dcb_v,
            d_w2, d_ba2, d_norm1, d_norm2, d_qn, d_kn, d_sink8, d_gla_norm)


def kernel(x, norm1_g, w_in, attn_q_norm_g, attn_k_norm_g, attn_sink, gla_wa2_fwd, gla_ba_fwd, gla_wa2_bwd, gla_ba_bwd, gla_out_norm_g, w_out, norm2_g, w_up, conv_w, conv_b, w_down, loss_target, m_norm1_g, m_w_in, m_attn_q_norm_g, m_attn_k_norm_g, m_attn_sink, m_gla_wa2_fwd, m_gla_ba_fwd, m_gla_wa2_bwd, m_gla_ba_bwd, m_gla_out_norm_g, m_w_out, m_norm2_g, m_w_up, m_conv_w, m_conv_b, m_w_down, v_norm1_g, v_w_in, v_attn_q_norm_g, v_attn_k_norm_g, v_attn_sink, v_gla_wa2_fwd, v_gla_ba_fwd, v_gla_wa2_bwd, v_gla_ba_bwd, v_gla_out_norm_g, v_w_out, v_norm2_g, v_w_up, v_conv_w, v_conv_b, v_w_down):
    t = x.shape[1]
    xs = x.reshape(t, D_MODEL)
    target = loss_target.reshape(t, D_MODEL)
    core = lax.axis_index("c").astype(jnp.int32).reshape(1)

    w_small = _small_block(conv_w[0], gla_wa2_fwd[0], gla_wa2_bwd[0])
    w_in_t, m_in_t, v_in_t = (jnp.swapaxes(a[0], 0, 1) for a in (w_in, m_w_in, v_w_in))
    g_in, g_small = _all_gather([w_in_t.astype(BF16), w_small], name="gather_w_in", collective_id=8)
    g_in, later = lax.optimization_barrier(
        (g_in, [w_out[0].astype(BF16), w_up[0].astype(BF16), w_down[0].astype(BF16)]))
    g_out, w_up8, g_down = _all_gather(later, name="gather_later_weights", collective_id=1)
    w_in_k = _to_kernel_rows(g_in.reshape(IN_TOTAL, D_MODEL))
    w_out_full = g_out.reshape(D_MODEL, D_MODEL)
    w_down_full = g_down.reshape(D_FF, D_MODEL)
    conv_w_full, wa2_f, wa2_b = _small_unblocks(g_small)
    cw_g, cw_v = conv_w_full[:, :D_FF], conv_w_full[:, D_FF:]
    cb_g, cb_v = conv_b[:, :D_FF], conv_b[:, D_FF:]
    w2 = jnp.zeros((LR_PAD, 2 * GLA_KEY_WIDTH), F32)
    w2 = w2.at[:GLA_RANK, :GLA_KEY_WIDTH].set(wa2_f).at[GLA_RANK:2 * GLA_RANK, GLA_KEY_WIDTH:].set(wa2_b)
    ba2 = jnp.concatenate([gla_ba_fwd, gla_ba_bwd], axis=1)
    rs = _ReduceScatter(core)
    (loss_part, grad_x, _, _, _, _, _, _, dcb_g, dcb_v, _, d_ba2,
     d_norm1, d_norm2, d_qn, d_kn, d_sink8, d_gla_norm) = _local_step(
        xs, target, norm1_g, w_in_k, attn_q_norm_g, attn_k_norm_g, attn_sink, w2, ba2, gla_out_norm_g,
        w_out_full, norm2_g, w_up8, cw_g, cw_v, cb_g, cb_v, w_down_full, rs=rs)

    (part_down,), (part_up,), (part_out,) = rs.result("w_down")[0], rs.result("w_up")[0], rs.result("w_out")[0]
    (part_in, part_small), (own_in, own_small) = rs.result("w_in")
    chip = (2 * lax.axis_index("x") + lax.axis_index("y")).astype(jnp.int32).reshape(1)
    m_small = _small_block(m_conv_w[0], m_gla_wa2_fwd[0], m_gla_wa2_bwd[0])
    v_small = _small_block(v_conv_w[0], v_gla_wa2_fwd[0], v_gla_wa2_bwd[0])
    upd_in = _adamw(part_in, w_in_t, m_in_t, v_in_t, tile=(IN_TOTAL // N_DEV, 512), name="adamw_w_in",
                    own=own_in, chip=chip)
    upd_in = [jnp.swapaxes(u, 0, 1) for u in upd_in]
    upd_out = _adamw(part_out, w_out[0], m_w_out[0], v_w_out[0], tile=(256, D_MODEL), name="adamw_w_out")
    upd_up = _adamw(part_up, w_up[0], m_w_up[0], v_w_up[0], tile=(256, UP_BLOCK), name="adamw_w_up")
    upd_down = _adamw(part_down, w_down[0], m_w_down[0], v_w_down[0], tile=(64, D_MODEL), name="adamw_w_down")
    upd_small = _adamw(part_small, w_small, m_small, v_small, tile=part_small.shape[1:],
                       name="adamw_small", own=own_small, chip=chip)
    upd_small = [_small_unblock(u) for u in upd_small]

    rep_names = ["norm1_g", "attn_q_norm_g", "attn_k_norm_g", "attn_sink", "gla_ba_fwd", "gla_ba_bwd",
                 "gla_out_norm_g", "norm2_g", "conv_b"]
    def whole_lanes(sink_like):
        return jnp.pad(sink_like, ((0, 0), (0, LANES - ATTN_HEADS)))

    rep_w = [norm1_g, attn_q_norm_g, attn_k_norm_g, whole_lanes(attn_sink), gla_ba_fwd, gla_ba_bwd, gla_out_norm_g,
             norm2_g, conv_b]
    rep_m = [m_norm1_g, m_attn_q_norm_g, m_attn_k_norm_g, whole_lanes(m_attn_sink), m_gla_ba_fwd, m_gla_ba_bwd,
             m_gla_out_norm_g, m_norm2_g, m_conv_b]
    rep_v = [v_norm1_g, v_attn_q_norm_g, v_attn_k_norm_g, whole_lanes(v_attn_sink), v_gla_ba_fwd, v_gla_ba_bwd,
             v_gla_out_norm_g, v_norm2_g, v_conv_b]
    d_sink = whole_lanes(d_sink8[:, :GQA_GROUP, 0].reshape(1, ATTN_HEADS))
    rep_g = [d_norm1, d_qn, d_kn, d_sink, d_ba2[:, :GLA_KEY_WIDTH], d_ba2[:, GLA_KEY_WIDTH:], d_gla_norm, d_norm2,
             jnp.concatenate([dcb_g, dcb_v], axis=1)]
    (rep_terms,) = _all_gather([_pack_rows(rep_g + [loss_part])], name="gather_small_grads", collective_id=7)
    upd_rep, loss_row = _adamw_rows(rep_terms, rep_w, rep_m, rep_v)
    sink_at = rep_names.index("attn_sink")
    for kind in range(4):
        upd_rep[kind][sink_at] = upd_rep[kind][sink_at][:, :ATTN_HEADS]
    loss = loss_row[0, 0]

    order = ["norm1_g", "w_in", "attn_q_norm_g", "attn_k_norm_g", "attn_sink", "gla_wa2_fwd", "gla_ba_fwd",
             "gla_wa2_bwd", "gla_ba_bwd", "gla_out_norm_g", "w_out", "norm2_g", "w_up", "conv_w", "conv_b", "w_down"]
    outs = [loss, grad_x.reshape(1, t, D_MODEL)]
    for kind in range(4):
        by_name = {n: upd_rep[kind][i] for i, n in enumerate(rep_names)}
        by_name["w_in"] = upd_in[kind][None]
        by_name["w_out"] = upd_out[kind][None]
        by_name["w_up"] = upd_up[kind][None]
        by_name["w_down"] = upd_down[kind][None]
        by_name["conv_w"] = upd_small[kind][0][None]
        by_name["gla_wa2_fwd"] = upd_small[kind][1][None]
        by_name["gla_wa2_bwd"] = upd_small[kind][2][None]
        outs += [by_name[n] for n in order]
    return tuple(outs)
```

```python
import functools

import jax
import jax.numpy as jnp
from jax import lax
from jax.experimental import pallas as pl
from jax.experimental.pallas import tpu as pltpu
from jax.experimental.pallas import tpu_sc as plsc

F32 = jnp.float32
BF16 = jnp.bfloat16

D_MODEL = 2048
HEAD_DIM = 128
ATTN_WIDTH = 1024
ATTN_HEADS = 8
KV_HEADS = 2
GQA_GROUP = 4
KV_WIDTH = KV_HEADS * HEAD_DIM
ATTN_BLOCK = 128
WINDOW = 128
ROPE_THETA = 10000.0
GLA_HEADS = 4
GLA_DK = 128
GLA_DV = 256
GLA_KEY_WIDTH = 512
GLA_WIDTH = 1024
GLA_RANK = 16
GLA_GATE_NORMALIZER = 16.0
GLA_CHUNK = 64
GLA_PER_STEP = 4
D_FF = 5632
NORM_EPS = 1e-6
IN_TOTAL = 4640
IN_MAIN = 4608
LR_PAD = 128
N_DEV = 8
N_CHIP = 4

ADAM_LR = 0.001
ADAM_B1 = 0.9
ADAM_B2 = 0.999
ADAM_EPS = 1e-08
ADAM_WD = 0.01
ADAM_STEP = 10

SEGMENTS = {
    "qa": (0, 0, 1024),
    "gate": (1024, 3584, 1024),
    "vg": (2048, 2560, 1024),
    "qg": (3072, 1536, 512),
    "kg": (3584, 2048, 512),
    "ka": (4096, 1024, 256),
    "va": (4352, 1280, 256),
}

VMEM_LIMIT = 56 * 1024 * 1024
MESH = pl.DeviceIdType.MESH


def _params(semantics=None, vmem=None):
    return pltpu.CompilerParams(dimension_semantics=semantics, vmem_limit_bytes=vmem)


_DIMS = {
    "nn": (((1,), (0,)), ((), ())),
    "nt": (((1,), (1,)), ((), ())),
    "tn": (((0,), (0,)), ((), ())),
}


def _mxu(a, b, mode):
    return lax.dot_general(a.astype(BF16), b.astype(BF16), _DIMS[mode], preferred_element_type=F32)


@functools.partial(jax.custom_vjp, nondiff_argnums=(2,))
def bdot(a, b, mode):
    return _mxu(a, b, mode)


def _bdot_fwd(a, b, mode):
    return _mxu(a, b, mode), (a, b)


def _bdot_bwd(mode, res, g):
    a, b = res
    if mode == "nn":
        return _mxu(g, b, "nt"), _mxu(a, g, "tn")
    if mode == "nt":
        return _mxu(g, b, "nn"), _mxu(g, a, "tn")
    return _mxu(b, g, "nt"), _mxu(a, g, "nn")


bdot.defvjp(_bdot_fwd, _bdot_bwd)


def _rms(x, g):
    return x * lax.rsqrt(jnp.mean(x * x, axis=-1, keepdims=True) + NORM_EPS) * g


def _rope(x, cos, sin_signed):
    return x * cos + pltpu.roll(x, HEAD_DIM // 2, 1) * sin_signed


def _rope_transposed(d, cos, sin_signed):
    return d * cos + pltpu.roll(d * sin_signed, HEAD_DIM // 2, 1)


def _silu(x):
    return x * jax.nn.sigmoid(x)


def _log_sigmoid(z):
    return -(jnp.maximum(-z, 0.0) + jnp.log(1.0 + jnp.exp(-jnp.abs(z))))


def _matmul_call(args, in_specs, o_spec, out_shape, grid, mode, nk, acc_shape, *, name, has_res=False,
                 prefetch=None, load_b=lambda ref: ref[...]):
    dims = _DIMS[mode]
    out_dtype = out_shape.dtype
    n_pre = 0 if prefetch is None else 1

    def body(*refs):
        refs = refs[n_pre:]
        if has_res:
            a_ref, b_ref, r_ref, o_ref = refs[:4]
            rest = refs[4:]
        else:
            a_ref, b_ref, o_ref = refs[:3]
            r_ref = None
            rest = refs[3:]
        part = lax.dot_general(a_ref[...], load_b(b_ref), dims, preferred_element_type=F32)

        def finish(acc):
            if r_ref is not None:
                acc = acc + r_ref[...]
            o_ref[...] = acc.astype(out_dtype)

        if nk == 1:
            finish(part)
        else:
            acc_ref = rest[0]
            kk = pl.program_id(2)

            @pl.when(kk == 0)
            def _():
                acc_ref[...] = part

            @pl.when(kk > 0)
            def _():
                acc_ref[...] += part

            @pl.when(kk == nk - 1)
            def _():
                finish(acc_ref[...])

    scratch = [pltpu.VMEM(acc_shape, F32)] if nk > 1 else []
    params = _params(("parallel", "parallel", "arbitrary"), VMEM_LIMIT)
    if prefetch is None:
        return pl.pallas_call(body, name=name, grid=grid, in_specs=in_specs, out_specs=o_spec, out_shape=out_shape,
                              scratch_shapes=scratch, compiler_params=params)(*args)
    return pl.pallas_call(
        body, name=name,
        grid_spec=pltpu.PrefetchScalarGridSpec(num_scalar_prefetch=1, grid=grid, in_specs=in_specs,
                                               out_specs=o_spec, scratch_shapes=scratch),
        out_shape=out_shape, compiler_params=params)(prefetch, *args)


def _matmul(a, b, mode, *, tm, tn, tk, out_dtype=F32, res=None, name, n_out=None):
    if mode == "nn":
        (m, k), (k2, n) = a.shape, b.shape
    elif mode == "nt":
        (m, k), (n, k2) = a.shape, b.shape
    else:
        (k, m), (k2, n) = a.shape, b.shape
    n = n if n_out is None else n_out
    assert k == k2 and m % tm == 0 and n % tn == 0 and k % tk == 0, (name, a.shape, b.shape, tm, tn, tk)
    if mode == "tn":
        a_spec = pl.BlockSpec((tk, tm), lambda i, j, kk: (kk, i))
    else:
        a_spec = pl.BlockSpec((tm, tk), lambda i, j, kk: (i, kk))
    if mode == "nt":
        b_spec = pl.BlockSpec((tn, tk), lambda i, j, kk: (j, kk))
    else:
        b_spec = pl.BlockSpec((tk, tn), lambda i, j, kk: (kk, j))
    o_spec = pl.BlockSpec((tm, tn), lambda i, j, kk: (i, j))
    in_specs, args = [a_spec, b_spec], [a, b]
    if res is not None:
        in_specs.append(o_spec)
        args.append(res)
    return _matmul_call(args, in_specs, o_spec, jax.ShapeDtypeStruct((m, n), out_dtype),
                        (m // tm, n // tn, k // tk), mode, k // tk, (tm, tn), name=name, has_res=res is not None)


def _out_proj(o_attn, o_gla, w_out, x, *, tm, tn):
    t, ka = o_attn.shape
    kg = o_gla.shape[1]

    def body(a_ref, g_ref, w_ref, x_ref, o_ref):
        acc = lax.dot_general(a_ref[...], w_ref[:ka], _DIMS["nn"], preferred_element_type=F32)
        acc = acc + lax.dot_general(g_ref[...], w_ref[ka:], _DIMS["nn"], preferred_element_type=F32)
        o_ref[...] = acc + x_ref[...]

    tile = pl.BlockSpec((tm, tn), lambda i, j: (i, j))
    return pl.pallas_call(
        body, name="out_proj", grid=(t // tm, D_MODEL // tn),
        in_specs=[pl.BlockSpec((tm, ka), lambda i, j: (i, 0)), pl.BlockSpec((tm, kg), lambda i, j: (i, 0)),
                  pl.BlockSpec((ka + kg, tn), lambda i, j: (0, j)), tile],
        out_specs=tile, out_shape=jax.ShapeDtypeStruct((t, D_MODEL), F32),
        compiler_params=_params(("parallel", "parallel"), VMEM_LIMIT),
    )(o_attn, o_gla, w_out, x)


def _out_proj_dw(o_attn, o_gla, dx1, *, tn):
    t, ka = o_attn.shape
    assert o_gla.shape == (t, ka)

    def body(a_ref, g_ref, d_ref, o_ref):
        @pl.when(pl.program_id(0) == 0)
        def _():
            o_ref[...] = lax.dot_general(a_ref[...], d_ref[...], _DIMS["tn"], preferred_element_type=F32)

        @pl.when(pl.program_id(0) == 1)
        def _():
            o_ref[...] = lax.dot_general(g_ref[...], d_ref[...], _DIMS["tn"], preferred_element_type=F32)

    whole = pl.BlockSpec((t, ka), lambda i, j: (0, 0))
    return pl.pallas_call(
        body, name="dw_out", grid=(2, D_MODEL // tn),
        in_specs=[whole, whole, pl.BlockSpec((t, tn), lambda i, j: (0, j))],
        out_specs=pl.BlockSpec((ka, tn), lambda i, j: (i, j)),
        out_shape=jax.ShapeDtypeStruct((2 * ka, D_MODEL), F32),
        compiler_params=_params(("parallel", "parallel"), VMEM_LIMIT),
    )(o_attn, o_gla, dx1)


UP_BLOCK = 2 * D_FF // N_DEV


def _up_proj(h2, w_up8, *, tm):
    t = h2.shape[0]
    return _matmul_call(
        [h2, w_up8],
        [pl.BlockSpec((tm, D_MODEL), lambda i, j, kk: (i, 0)),
         pl.BlockSpec((None, D_MODEL, UP_BLOCK), lambda i, j, kk: (j, 0, 0))],
        pl.BlockSpec((None, tm, UP_BLOCK), lambda i, j, kk: (j // N_CHIP, i, j % N_CHIP)),
        jax.ShapeDtypeStruct((2, t, D_FF), F32), (t // tm, N_DEV, 1), "nn", 1, None, name="up_proj")


def _up_proj_dx(du, w_up8, *, tm, tn):
    t = du.shape[1]
    pair = 2
    return _matmul_call(
        [du, w_up8],
        [pl.BlockSpec((None, tm, pair * UP_BLOCK), lambda i, j, kk: (kk // 2, i, kk % 2)),
         pl.BlockSpec((pair, tn, UP_BLOCK), lambda i, j, kk: (kk, j, 0))],
        pl.BlockSpec((tm, tn), lambda i, j, kk: (i, j)),
        jax.ShapeDtypeStruct((t, D_MODEL), F32), (t // tm, D_MODEL // tn, N_DEV // pair), "nt", N_DEV // pair,
        (tm, tn), name="up_proj_dx", load_b=lambda ref: jnp.concatenate([ref[0], ref[1]], axis=1))


def _up_proj_dw(h2, du, *, tm, tk):
    t = h2.shape[0]
    return _matmul_call(
        [h2, du],
        [pl.BlockSpec((tk, tm), lambda j, i, kk: (kk, i)),
         pl.BlockSpec((None, tk, UP_BLOCK), lambda j, i, kk: (j // N_CHIP, kk, j % N_CHIP))],
        pl.BlockSpec((None, tm, UP_BLOCK), lambda j, i, kk: (j, i, 0)),
        jax.ShapeDtypeStruct((N_DEV, D_MODEL, UP_BLOCK), F32), (N_DEV, D_MODEL // tm, t // tk), "tn", t // tk,
        (tm, UP_BLOCK), name="up_proj_dw")


IN_TILE = 512


def _in_proj_dw(d_proj, h1, *, tn):
    t = h1.shape[0]
    table = []
    for tile in range(IN_MAIN // IN_TILE):
        dst, src, _ = max(s for s in SEGMENTS.values() if s[0] <= tile * IN_TILE)
        assert (src + tile * IN_TILE - dst) % IN_TILE == 0
        table.append((src + tile * IN_TILE - dst) // IN_TILE)
    assert sorted(table) == list(range(IN_MAIN // IN_TILE))
    return _matmul_call(
        [d_proj, h1],
        [pl.BlockSpec((t, IN_TILE), lambda j, i, kk, tab: (0, i)),
         pl.BlockSpec((t, tn), lambda j, i, kk, tab: (0, j))],
        pl.BlockSpec((IN_TILE, tn), lambda j, i, kk, tab: (tab[i], j)),
        jax.ShapeDtypeStruct((IN_TOTAL, D_MODEL), F32), (D_MODEL // tn, IN_MAIN // IN_TILE, 1), "tn", 1, None,
        name="in_proj_dw", prefetch=jnp.asarray(table, jnp.int32))


def _in_proj_dw_lr(dw_t, d_lr, h1):
    t = h1.shape[0]
    n_lr = IN_TOTAL - IN_MAIN
    tn = 512

    def body(dw_ref, dlr_ref, h1_ref, out_ref):
        full = lax.dot_general(dlr_ref[...], h1_ref[...], _DIMS["tn"], preferred_element_type=F32)
        out_ref[...] = full[:n_lr]

    return pl.pallas_call(
        body, name="in_proj_dw_lr", grid=(D_MODEL // tn,),
        in_specs=[pl.BlockSpec(memory_space=pl.ANY),
                  pl.BlockSpec((t, LR_PAD), lambda j: (0, 0)),
                  pl.BlockSpec((t, tn), lambda j: (0, j))],
        out_specs=pl.BlockSpec((n_lr, tn), lambda j: (IN_MAIN // n_lr, j)),
        out_shape=jax.ShapeDtypeStruct(dw_t.shape, F32),
        input_output_aliases={0: 0},
        compiler_params=_params(("parallel",), VMEM_LIMIT),
    )(dw_t, d_lr, h1)


def _rmsnorm_fwd(x, g, *, name, tr=512):
    t, d = x.shape

    def body(x_ref, g_ref, h_ref):
        h_ref[...] = _rms(x_ref[...], g_ref[...]).astype(BF16)

    return pl.pallas_call(
        body, name=name, grid=(t // tr,),
        in_specs=[pl.BlockSpec((tr, d), lambda i: (i, 0)), pl.BlockSpec((1, d), lambda i: (0, 0))],
        out_specs=pl.BlockSpec((tr, d), lambda i: (i, 0)),
        out_shape=jax.ShapeDtypeStruct((t, d), BF16),
        compiler_params=_params(("parallel",), VMEM_LIMIT),
    )(x, g)


def _rmsnorm_bwd(x, g, dh, dres, *, name, also_bf16, tr=512):
    t, d = x.shape

    def body(x_ref, g_ref, dh_ref, dres_ref, dx_ref, *rest):
        dg_ref = rest[-1]
        _, vjp = jax.vjp(_rms, x_ref[...], g_ref[...])
        dx, dg = vjp(dh_ref[...])
        dx = dx + dres_ref[...]
        dx_ref[...] = dx
        if also_bf16:
            rest[0][...] = dx.astype(BF16)

        @pl.when(pl.program_id(0) == 0)
        def _():
            dg_ref[...] = jnp.zeros_like(dg_ref)

        dg_ref[...] += dg

    row = pl.BlockSpec((tr, d), lambda i: (i, 0))
    vec = pl.BlockSpec((1, d), lambda i: (0, 0))
    return pl.pallas_call(
        body, name=name, grid=(t // tr,),
        in_specs=[row, vec, row, row],
        out_specs=[row] + [row] * also_bf16 + [vec],
        out_shape=[jax.ShapeDtypeStruct((t, d), F32)] + [jax.ShapeDtypeStruct((t, d), BF16)] * also_bf16
                  + [jax.ShapeDtypeStruct((1, d), F32)],
        compiler_params=_params(("arbitrary",), VMEM_LIMIT),
    )(x, g, dh, dres)


def _seg_block(name, width):
    off = SEGMENTS[name][0]
    assert off % width == 0
    return off // width


def _attn_prep_fwd(proj, cos, sin_signed, gq, gk, *, tr=512):
    t = proj.shape[0]

    def body(q_ref, k_ref, v_ref, cos_ref, sin_ref, gq_ref, gk_ref, qo_ref, ko_ref, vo_ref):
        cos_t, sin_t = cos_ref[...], sin_ref[...]
        for h in range(ATTN_HEADS):
            cols = slice(h * HEAD_DIM, (h + 1) * HEAD_DIM)
            qo_ref[:, cols] = _rope(_rms(q_ref[:, cols], gq_ref[...]), cos_t, sin_t).astype(BF16)
        for h in range(KV_HEADS):
            cols = slice(h * HEAD_DIM, (h + 1) * HEAD_DIM)
            ko_ref[:, cols] = _rope(_rms(k_ref[:, cols], gk_ref[...]), cos_t, sin_t).astype(BF16)
        vo_ref[...] = v_ref[...].astype(BF16)

    qb, kb, vb = _seg_block("qa", ATTN_WIDTH), _seg_block("ka", KV_WIDTH), _seg_block("va", KV_WIDTH)
    tab = pl.BlockSpec((tr, HEAD_DIM), lambda i: (i, 0))
    vec = pl.BlockSpec((1, HEAD_DIM), lambda i: (0, 0))
    return pl.pallas_call(
        body, name="attn_prep_fwd", grid=(t // tr,),
        in_specs=[pl.BlockSpec((tr, ATTN_WIDTH), lambda i: (i, qb)),
                  pl.BlockSpec((tr, KV_WIDTH), lambda i: (i, kb)),
                  pl.BlockSpec((tr, KV_WIDTH), lambda i: (i, vb)),
                  tab, tab, vec, vec],
        out_specs=[pl.BlockSpec((tr, ATTN_WIDTH), lambda i: (i, 0)),
                   pl.BlockSpec((tr, KV_WIDTH), lambda i: (i, 0)),
                   pl.BlockSpec((tr, KV_WIDTH), lambda i: (i, 0))],
        out_shape=[jax.ShapeDtypeStruct((t, ATTN_WIDTH), BF16),
                   jax.ShapeDtypeStruct((t, KV_WIDTH), BF16),
                   jax.ShapeDtypeStruct((t, KV_WIDTH), BF16)],
        compiler_params=_params(("parallel",), VMEM_LIMIT),
    )(proj, proj, proj, cos, sin_signed, gq, gk)


def _attn_heads(q, kcat, vcat, sink_col, valid):
    s = bdot(q, kcat, "nt") * (HEAD_DIM ** -0.5)
    s = jnp.where(valid, s, -jnp.inf)
    m = lax.stop_gradient(jnp.maximum(jnp.max(s, axis=-1, keepdims=True), sink_col))
    p = jnp.exp(s - m)
    p = p / (jnp.sum(p, axis=-1, keepdims=True) + jnp.exp(sink_col - m))
    return bdot(p, vcat, "nn")


def _attn_valid(n, t):
    shape = (GQA_GROUP * ATTN_BLOCK, 3 * ATTN_BLOCK)
    qi = lax.broadcasted_iota(jnp.int32, shape, 0) % ATTN_BLOCK
    sj = lax.broadcasted_iota(jnp.int32, shape, 1)
    kpos = n * ATTN_BLOCK - ATTN_BLOCK + sj
    return (jnp.abs(sj - ATTN_BLOCK - qi) <= WINDOW) & (kpos >= 0) & (kpos < t)


def _head_rows(g):
    return slice(g * ATTN_BLOCK, (g + 1) * ATTN_BLOCK)


def _head_cols(g):
    return slice(g * HEAD_DIM, (g + 1) * HEAD_DIM)


def _stack_heads(ref):
    return jnp.concatenate([ref[:, _head_cols(g)] for g in range(GQA_GROUP)], axis=0).astype(F32)


def _sink_column(sink_ref, h):
    return jnp.concatenate([jnp.full((ATTN_BLOCK, 1), sink_ref[h * GQA_GROUP + g], F32)
                            for g in range(GQA_GROUP)], axis=0)


def _attn_specs(nb):
    q_spec = pl.BlockSpec((ATTN_BLOCK, GQA_GROUP * HEAD_DIM), lambda h, n: (n, h))
    kv_specs = [
        pl.BlockSpec((ATTN_BLOCK, HEAD_DIM), lambda h, n: (jnp.maximum(n - 1, 0), h)),
        pl.BlockSpec((ATTN_BLOCK, HEAD_DIM), lambda h, n: (n, h)),
        pl.BlockSpec((ATTN_BLOCK, HEAD_DIM), lambda h, n: (jnp.minimum(n + 1, nb - 1), h)),
    ]
    return q_spec, kv_specs


def _attn_fwd(q, k, v, sink):
    t = q.shape[0]
    nb = t // ATTN_BLOCK

    def body(sink_ref, q_ref, kp_ref, kc_ref, kn_ref, vp_ref, vc_ref, vn_ref, o_ref):
        valid = _attn_valid(pl.program_id(0), t)
        args = []
        for h in range(KV_HEADS):
            q = jnp.concatenate([q_ref[:, _head_cols(h * GQA_GROUP + g)] for g in range(GQA_GROUP)], axis=0)
            kcat = jnp.concatenate([r[:, _head_cols(h)] for r in (kp_ref, kc_ref, kn_ref)], axis=0)
            vcat = jnp.concatenate([r[:, _head_cols(h)] for r in (vp_ref, vc_ref, vn_ref)], axis=0)
            args.append((q.astype(F32), kcat.astype(F32), vcat.astype(F32), _sink_column(sink_ref, h)))
        outs = [_attn_heads(*a, valid=valid).astype(BF16) for a in args]
        for h, o in enumerate(outs):
            for g in range(GQA_GROUP):
                o_ref[:, _head_cols(h * GQA_GROUP + g)] = o[_head_rows(g)]

    q_spec = pl.BlockSpec((ATTN_BLOCK, ATTN_WIDTH), lambda n: (n, 0))
    kv_specs = [pl.BlockSpec((ATTN_BLOCK, KV_WIDTH), lambda n: (jnp.maximum(n - 1, 0), 0)),
                pl.BlockSpec((ATTN_BLOCK, KV_WIDTH), lambda n: (n, 0)),
                pl.BlockSpec((ATTN_BLOCK, KV_WIDTH), lambda n: (jnp.minimum(n + 1, nb - 1), 0))]
    return pl.pallas_call(
        body, name="attn_fwd", grid=(nb,),
        in_specs=[pl.BlockSpec(memory_space=pltpu.SMEM), q_spec] + kv_specs + kv_specs,
        out_specs=q_spec,
        out_shape=jax.ShapeDtypeStruct((t, ATTN_WIDTH), BF16),
        compiler_params=_params(("parallel",), VMEM_LIMIT),
    )(sink, q, k, k, k, v, v, v)


def _attn_bwd(q, k, v, sink, dmix):
    t = q.shape[0]
    nb = t // ATTN_BLOCK

    def body(sink_ref, q_ref, kp_ref, kc_ref, kn_ref, vp_ref, vc_ref, vn_ref, do_ref,
             dq_ref, dk_lo, dk_mid, dk_hi, dv_lo, dv_mid, dv_hi, dsink_ref):
        h, n = pl.program_id(0), pl.program_id(1)
        valid = _attn_valid(n, t)
        kcat = jnp.concatenate([kp_ref[...], kc_ref[...], kn_ref[...]], axis=0).astype(F32)
        vcat = jnp.concatenate([vp_ref[...], vc_ref[...], vn_ref[...]], axis=0).astype(F32)
        _, vjp = jax.vjp(functools.partial(_attn_heads, valid=valid),
                         _stack_heads(q_ref), kcat, vcat, _sink_column(sink_ref, h))
        dq, dk, dv, dsink_col = vjp(_stack_heads(do_ref))
        row = lax.broadcasted_iota(jnp.int32, (8, HEAD_DIM), 0)
        dsink = jnp.zeros((8, HEAD_DIM), F32)
        for g in range(GQA_GROUP):
            dq_ref[:, _head_cols(g)] = dq[_head_rows(g)]
            dsink = dsink + jnp.where(row == g, jnp.sum(dsink_col[_head_rows(g)]), 0.0)
        for i, (dk_ref, dv_ref) in enumerate(((dk_lo, dv_lo), (dk_mid, dv_mid), (dk_hi, dv_hi))):
            rows = slice(i * ATTN_BLOCK, (i + 1) * ATTN_BLOCK)
            dk_ref[...] = dk[rows]
            dv_ref[...] = dv[rows]

        @pl.when(n == 0)
        def _():
            dsink_ref[...] = jnp.zeros_like(dsink_ref)

        dsink_ref[...] += dsink

    q_spec, kv_specs = _attn_specs(nb)
    kv_out = pl.BlockSpec((ATTN_BLOCK, HEAD_DIM), lambda h, n: (n, h))
    kv_shape = jax.ShapeDtypeStruct((t, KV_WIDTH), F32)
    return pl.pallas_call(
        body, name="attn_bwd", grid=(KV_HEADS, nb),
        in_specs=[pl.BlockSpec(memory_space=pltpu.SMEM), q_spec] + kv_specs + kv_specs + [q_spec],
        out_specs=[q_spec] + [kv_out] * 6 + [pl.BlockSpec((None, 8, HEAD_DIM), lambda h, n: (h, 0, 0))],
        out_shape=[jax.ShapeDtypeStruct((t, ATTN_WIDTH), F32)] + [kv_shape] * 6
                  + [jax.ShapeDtypeStruct((KV_HEADS, 8, HEAD_DIM), F32)],
        compiler_params=_params(("parallel", "arbitrary"), VMEM_LIMIT),
    )(sink, q, k, k, k, v, v, v, dmix)


def _attn_prep_bwd(proj, cos, sin_signed, gq, gk, dq, dks, dvs):
    t = proj.shape[0]
    tr = ATTN_BLOCK
    nb = t // tr

    def body(q_ref, k_ref, cos_ref, sin_ref, gq_ref, gk_ref, dq_ref,
             dk_lo, dk_mid, dk_hi, dv_lo, dv_mid, dv_hi,
             dqo_ref, dko_ref, dvo_ref, dgq_ref, dgk_ref):
        n = pl.program_id(0)
        cos_t, sin_t = cos_ref[...], sin_ref[...]
        has_next = (n < nb - 1).astype(F32)
        has_prev = (n > 0).astype(F32)
        dk = dk_lo[...] * has_next + dk_mid[...] + dk_hi[...] * has_prev
        dv = dv_lo[...] * has_next + dv_mid[...] + dv_hi[...] * has_prev
        dvo_ref[...] = dv.astype(BF16)
        dgq = jnp.zeros((1, HEAD_DIM), F32)
        dgk = jnp.zeros((1, HEAD_DIM), F32)
        for h in range(ATTN_HEADS):
            cols = slice(h * HEAD_DIM, (h + 1) * HEAD_DIM)
            _, vjp = jax.vjp(_rms, q_ref[:, cols], gq_ref[...])
            dx, dg = vjp(_rope_transposed(dq_ref[:, cols], cos_t, sin_t))
            dqo_ref[:, cols] = dx.astype(BF16)
            dgq = dgq + dg
        for h in range(KV_HEADS):
            cols = slice(h * HEAD_DIM, (h + 1) * HEAD_DIM)
            _, vjp = jax.vjp(_rms, k_ref[:, cols], gk_ref[...])
            dx, dg = vjp(_rope_transposed(dk[:, cols], cos_t, sin_t))
            dko_ref[:, cols] = dx.astype(BF16)
            dgk = dgk + dg

        @pl.when(n == 0)
        def _():
            dgq_ref[...] = jnp.zeros_like(dgq_ref)
            dgk_ref[...] = jnp.zeros_like(dgk_ref)

        dgq_ref[...] += dgq
        dgk_ref[...] += dgk

    qb, kb = _seg_block("qa", ATTN_WIDTH), _seg_block("ka", KV_WIDTH)
    tab = pl.BlockSpec((tr, HEAD_DIM), lambda i: (i, 0))
    vec = pl.BlockSpec((1, HEAD_DIM), lambda i: (0, 0))
    kv = [pl.BlockSpec((tr, KV_WIDTH), lambda i: (jnp.minimum(i + 1, nb - 1), 0)),
          pl.BlockSpec((tr, KV_WIDTH), lambda i: (i, 0)),
          pl.BlockSpec((tr, KV_WIDTH), lambda i: (jnp.maximum(i - 1, 0), 0))]
    wide = pl.BlockSpec((tr, ATTN_WIDTH), lambda i: (i, 0))
    narrow = pl.BlockSpec((tr, KV_WIDTH), lambda i: (i, 0))
    return pl.pallas_call(
        body, name="attn_prep_bwd", grid=(nb,),
        in_specs=[pl.BlockSpec((tr, ATTN_WIDTH), lambda i: (i, qb)),
                  pl.BlockSpec((tr, KV_WIDTH), lambda i: (i, kb)),
                  tab, tab, vec, vec, wide] + kv + kv,
        out_specs=[wide, narrow, narrow, vec, vec],
        out_shape=[jax.ShapeDtypeStruct((t, ATTN_WIDTH), BF16),
                   jax.ShapeDtypeStruct((t, KV_WIDTH), BF16),
                   jax.ShapeDtypeStruct((t, KV_WIDTH), BF16),
                   jax.ShapeDtypeStruct((1, HEAD_DIM), F32),
                   jax.ShapeDtypeStruct((1, HEAD_DIM), F32)],
        compiler_params=_params(("arbitrary",), VMEM_LIMIT),
    )(proj, proj, cos, sin_signed, gq, gk, dq, *dks, *dvs)


def _decay_fn(lr, w2, ba):
    return _log_sigmoid(bdot(lr, w2, "nn") + ba) / GLA_GATE_NORMALIZER


def _gla_prep_fwd(proj_lr, w2, ba2, *, tr=512):
    t = proj_lr.shape[0]
    width = 2 * GLA_KEY_WIDTH

    def body(lr_ref, w2_ref, ba_ref, g_ref):
        g_ref[...] = _decay_fn(lr_ref[...], w2_ref[...], ba_ref[...])

    return pl.pallas_call(
        body, name="gla_prep_fwd", grid=(t // tr,),
        in_specs=[pl.BlockSpec((tr, LR_PAD), lambda i: (i, 0)),
                  pl.BlockSpec((LR_PAD, width), lambda i: (0, 0)),
                  pl.BlockSpec((1, width), lambda i: (0, 0))],
        out_specs=pl.BlockSpec((tr, width), lambda i: (i, 0)),
        out_shape=jax.ShapeDtypeStruct((t, width), F32),
        compiler_params=_params(("parallel",), VMEM_LIMIT),
    )(proj_lr, w2, ba2)


def _gla_prep_bwd(proj_lr, w2, ba2, dg_f, dg_b, *, tr=512):
    t = proj_lr.shape[0]
    width = 2 * GLA_KEY_WIDTH

    def body(lr_ref, w2_ref, ba_ref, dgf_ref, dgb_ref, dlr_ref, dw2_ref, dba_ref):
        _, vjp = jax.vjp(_decay_fn, lr_ref[...], w2_ref[...], ba_ref[...])
        dlr, dw2, dba = vjp(jnp.concatenate([dgf_ref[...], dgb_ref[...]], axis=1))
        dlr_ref[...] = dlr.astype(BF16)

        @pl.when(pl.program_id(0) == 0)
        def _():
            dw2_ref[...] = jnp.zeros_like(dw2_ref)
            dba_ref[...] = jnp.zeros_like(dba_ref)

        dw2_ref[...] += dw2
        dba_ref[...] += dba

    half = pl.BlockSpec((tr, GLA_KEY_WIDTH), lambda i: (i, 0))
    return pl.pallas_call(
        body, name="gla_prep_bwd", grid=(t // tr,),
        in_specs=[pl.BlockSpec((tr, LR_PAD), lambda i: (i, 0)),
                  pl.BlockSpec((LR_PAD, width), lambda i: (0, 0)),
                  pl.BlockSpec((1, width), lambda i: (0, 0)), half, half],
        out_specs=[pl.BlockSpec((tr, LR_PAD), lambda i: (i, 0)),
                   pl.BlockSpec((LR_PAD, width), lambda i: (0, 0)),
                   pl.BlockSpec((1, width), lambda i: (0, 0))],
        out_shape=[jax.ShapeDtypeStruct((t, LR_PAD), BF16),
                   jax.ShapeDtypeStruct((LR_PAD, width), F32),
                   jax.ShapeDtypeStruct((1, width), F32)],
        compiler_params=_params(("arbitrary",), VMEM_LIMIT),
    )(proj_lr, w2, ba2, dg_f, dg_b)


def _gla_k(h):
    return slice(h * GLA_DK, (h + 1) * GLA_DK)


def _gla_v(h):
    return slice(h * GLA_DV, (h + 1) * GLA_DV)


def _running_sum(x, downward):
    n = x.shape[0]
    row = lax.broadcasted_iota(jnp.int32, x.shape, 0)
    step = 1
    while step < n:
        if downward:
            x = x + jnp.where(row >= step, pltpu.roll(x, step, 0), 0.0)
        else:
            x = x + jnp.where(row < n - step, pltpu.roll(x, n - step, 0), 0.0)
        step *= 2
    return x


@functools.partial(jax.custom_vjp, nondiff_argnums=(1,))
def _cumsum_rows(x, downward):
    return _running_sum(x, downward)


def _cumsum_rows_fwd(x, downward):
    return _running_sum(x, downward), None


def _cumsum_rows_bwd(downward, _, ct):
    return (_running_sum(ct, not downward),)


_cumsum_rows.defvjp(_cumsum_rows_fwd, _cumsum_rows_bwd)


def _gla_chunk(q, k, v, g, state, forward):
    c = GLA_CHUNK
    row = lax.broadcasted_iota(jnp.int32, (c, c), 0)
    col = lax.broadcasted_iota(jnp.int32, (c, c), 1)
    rid = lax.broadcasted_iota(jnp.int32, (c, GLA_DK), 0)
    q = q * (GLA_DK ** -0.5)
    if forward:
        see = row >= col
        upto_ref = rid <= c // 2
    else:
        see = row < col
        upto_ref = rid >= c - 1 - c // 2
    b = _cumsum_rows(g, forward)
    b_last = jnp.sum(g, axis=0, keepdims=True)
    b_ref = jnp.sum(jnp.where(upto_ref, g, 0.0), axis=0, keepdims=True)
    a = bdot(q * jnp.exp(b - b_ref), k * jnp.exp(b_ref - b), "nt")
    a = jnp.where(see, a, 0.0)
    o = bdot(a, v, "nn") + bdot(q * jnp.exp(b), state, "nt")
    new_state = state * jnp.exp(b_last) + bdot(v, k * jnp.exp(b_last - b), "tn")
    return o, new_state


def _gla_fwd(proj, g):
    t = proj.shape[0]
    c, per = GLA_CHUNK, GLA_PER_STEP
    nchunk = t // c
    nstep = nchunk // per
    qb, kb, vb = _seg_block("qg", GLA_KEY_WIDTH), _seg_block("kg", GLA_KEY_WIDTH), _seg_block("vg", GLA_WIDTH)

    def body(qf, kf, vf, gf, qr, kr, vr, gr, of_ref, ob_ref, sf_ref, sb_ref, state):
        @pl.when(pl.program_id(0) == 0)
        def _():
            state[...] = jnp.zeros_like(state)

        dirs = ((qf, kf, vf, gf, of_ref, sf_ref), (qr, kr, vr, gr, ob_ref, sb_ref))
        states = [[state[d, h] for h in range(GLA_HEADS)] for d in range(2)]
        for turn in range(per):
            chunk = (turn, per - 1 - turn)
            args = []
            for d, (q_ref, k_ref, v_ref, g_ref, _, _) in enumerate(dirs):
                rows = pl.ds(chunk[d] * c, c)
                args += [(q_ref[rows, _gla_k(h)], k_ref[rows, _gla_k(h)], v_ref[rows, _gla_v(h)],
                          g_ref[rows, _gla_k(h)], states[d][h]) for h in range(GLA_HEADS)]
            results = [_gla_chunk(*a, forward=(i < GLA_HEADS)) for i, a in enumerate(args)]
            for i, (a, (o, s_out)) in enumerate(zip(args, results)):
                d, h = divmod(i, GLA_HEADS)
                dirs[d][5][chunk[d], h] = a[4]
                dirs[d][4][pl.ds(chunk[d] * c, c), _gla_v(h)] = o
                states[d][h] = s_out
        for d in range(2):
            for h in range(GLA_HEADS):
                state[d, h] = states[d][h]

    specs, outs = [], []
    for d in range(2):
        ci = (lambda i: i) if d == 0 else (lambda i: nstep - 1 - i)
        specs += [pl.BlockSpec((per * c, GLA_KEY_WIDTH), lambda i, ci=ci: (ci(i), qb)),
                  pl.BlockSpec((per * c, GLA_KEY_WIDTH), lambda i, ci=ci: (ci(i), kb)),
                  pl.BlockSpec((per * c, GLA_WIDTH), lambda i, ci=ci: (ci(i), vb)),
                  pl.BlockSpec((per * c, GLA_KEY_WIDTH), lambda i, ci=ci, d=d: (ci(i), d))]
        outs.append(pl.BlockSpec((per * c, GLA_WIDTH), lambda i, ci=ci: (ci(i), 0)))
    for d in range(2):
        ci = (lambda i: i) if d == 0 else (lambda i: nstep - 1 - i)
        outs.append(pl.BlockSpec((per, GLA_HEADS, GLA_DV, GLA_DK), lambda i, ci=ci: (ci(i), 0, 0, 0)))
    o_shape = jax.ShapeDtypeStruct((t, GLA_WIDTH), F32)
    s_shape = jax.ShapeDtypeStruct((nchunk, GLA_HEADS, GLA_DV, GLA_DK), F32)
    return pl.pallas_call(
        body, name="gla_fwd", grid=(nstep,),
        in_specs=specs, out_specs=outs,
        out_shape=[o_shape, o_shape, s_shape, s_shape],
        scratch_shapes=[pltpu.VMEM((2, GLA_HEADS, GLA_DV, GLA_DK), F32)],
        compiler_params=_params(("arbitrary",), VMEM_LIMIT),
    )(proj, proj, proj, g, proj, proj, proj, g)


def _gla_bwd(proj, g, s_f, s_b, do):
    t = proj.shape[0]
    c, per = GLA_CHUNK, GLA_PER_STEP
    nchunk = t // c
    nstep = nchunk // per
    qb, kb, vb = _seg_block("qg", GLA_KEY_WIDTH), _seg_block("kg", GLA_KEY_WIDTH), _seg_block("vg", GLA_WIDTH)

    def body(*refs):
        ins, outs, dstate = refs[:12], refs[12:20], refs[20]

        @pl.when(pl.program_id(0) == 0)
        def _():
            dstate[...] = jnp.zeros_like(dstate)

        dstates = [[dstate[d, h] for h in range(GLA_HEADS)] for d in range(2)]
        for turn in range(per):
            chunk = (per - 1 - turn, turn)
            loaded = []
            for d in range(2):
                q_ref, k_ref, v_ref, g_ref, s_ref, do_ref = ins[6 * d:6 * d + 6]
                rows = pl.ds(chunk[d] * c, c)
                for h in range(GLA_HEADS):
                    loaded.append(((q_ref[rows, _gla_k(h)], k_ref[rows, _gla_k(h)], v_ref[rows, _gla_v(h)],
                                    g_ref[rows, _gla_k(h)], s_ref[chunk[d], h]),
                                   (do_ref[rows, _gla_v(h)], dstates[d][h])))
            grads = []
            for i, (primals, cotangents) in enumerate(loaded):
                _, vjp = jax.vjp(functools.partial(_gla_chunk, forward=(i < GLA_HEADS)), *primals)
                grads.append(vjp(cotangents))
            for i, (dq, dk, dv, dg, ds) in enumerate(grads):
                d, h = divmod(i, GLA_HEADS)
                rows = pl.ds(chunk[d] * c, c)
                dq_ref, dk_ref, dv_ref, dg_ref = outs[4 * d:4 * d + 4]
                dq_ref[rows, _gla_k(h)] = dq
                dk_ref[rows, _gla_k(h)] = dk
                dv_ref[rows, _gla_v(h)] = dv
                dg_ref[rows, _gla_k(h)] = dg
                dstates[d][h] = ds
        for d in range(2):
            for h in range(GLA_HEADS):
                dstate[d, h] = dstates[d][h]

    specs, outs, shapes = [], [], []
    for d in range(2):
        ci = (lambda i: nstep - 1 - i) if d == 0 else (lambda i: i)
        specs += [pl.BlockSpec((per * c, GLA_KEY_WIDTH), lambda i, ci=ci: (ci(i), qb)),
                  pl.BlockSpec((per * c, GLA_KEY_WIDTH), lambda i, ci=ci: (ci(i), kb)),
                  pl.BlockSpec((per * c, GLA_WIDTH), lambda i, ci=ci: (ci(i), vb)),
                  pl.BlockSpec((per * c, GLA_KEY_WIDTH), lambda i, ci=ci, d=d: (ci(i), d)),
                  pl.BlockSpec((per, GLA_HEADS, GLA_DV, GLA_DK), lambda i, ci=ci: (ci(i), 0, 0, 0)),
                  pl.BlockSpec((per * c, GLA_WIDTH), lambda i, ci=ci: (ci(i), 0))]
        key = pl.BlockSpec((per * c, GLA_KEY_WIDTH), lambda i, ci=ci: (ci(i), 0))
        val = pl.BlockSpec((per * c, GLA_WIDTH), lambda i, ci=ci: (ci(i), 0))
        outs += [key, key, val, key]
        shapes += [jax.ShapeDtypeStruct((t, GLA_KEY_WIDTH), F32), jax.ShapeDtypeStruct((t, GLA_KEY_WIDTH), F32),
                   jax.ShapeDtypeStruct((t, GLA_WIDTH), F32), jax.ShapeDtypeStruct((t, GLA_KEY_WIDTH), F32)]
    return pl.pallas_call(
        body, name="gla_bwd", grid=(nstep,),
        in_specs=specs, out_specs=outs, out_shape=shapes,
        scratch_shapes=[pltpu.VMEM((2, GLA_HEADS, GLA_DV, GLA_DK), F32)],
        compiler_params=_params(("arbitrary",), VMEM_LIMIT),
    )(proj, proj, proj, g, s_f, do, proj, proj, proj, g, s_b, do)


def _gla_out_head(o_f, o_b, gate, gn):
    return _rms(o_f + o_b, gn) * _silu(gate)


def _gla_out_fwd(o_f, o_b, proj, gn, *, tr=512):
    t = o_f.shape[0]
    gb = _seg_block("gate", GLA_WIDTH)

    def body(of_ref, ob_ref, gate_ref, gn_ref, out_ref):
        for h in range(GLA_HEADS):
            vc = slice(h * GLA_DV, (h + 1) * GLA_DV)
            out_ref[:, vc] = _gla_out_head(of_ref[:, vc], ob_ref[:, vc], gate_ref[:, vc], gn_ref[...]).astype(BF16)

    wide = pl.BlockSpec((tr, GLA_WIDTH), lambda i: (i, 0))
    return pl.pallas_call(
        body, name="gla_out_fwd", grid=(t // tr,),
        in_specs=[wide, wide, pl.BlockSpec((tr, GLA_WIDTH), lambda i: (i, gb)),
                  pl.BlockSpec((1, GLA_DV), lambda i: (0, 0))],
        out_specs=wide,
        out_shape=jax.ShapeDtypeStruct((t, GLA_WIDTH), BF16),
        compiler_params=_params(("parallel",), VMEM_LIMIT),
    )(o_f, o_b, proj, gn)


def _gla_out_bwd(o_f, o_b, proj, gn, dmix, *, tr=512):
    t = o_f.shape[0]
    gb = _seg_block("gate", GLA_WIDTH)

    def body(of_ref, ob_ref, gate_ref, gn_ref, dout_ref, do_ref, dgate_ref, dgn_ref):
        dgn = jnp.zeros((1, GLA_DV), F32)
        for h in range(GLA_HEADS):
            vc = slice(h * GLA_DV, (h + 1) * GLA_DV)
            _, vjp = jax.vjp(_gla_out_head, of_ref[:, vc], ob_ref[:, vc], gate_ref[:, vc], gn_ref[...])
            do, _, dgate, dg = vjp(dout_ref[:, vc])
            do_ref[:, vc] = do
            dgate_ref[:, vc] = dgate.astype(BF16)
            dgn = dgn + dg

        @pl.when(pl.program_id(0) == 0)
        def _():
            dgn_ref[...] = jnp.zeros_like(dgn_ref)

        dgn_ref[...] += dgn

    wide = pl.BlockSpec((tr, GLA_WIDTH), lambda i: (i, 0))
    vec = pl.BlockSpec((1, GLA_DV), lambda i: (0, 0))
    return pl.pallas_call(
        body, name="gla_out_bwd", grid=(t // tr,),
        in_specs=[wide, wide, pl.BlockSpec((tr, GLA_WIDTH), lambda i: (i, gb)), vec,
                  pl.BlockSpec((tr, GLA_WIDTH), lambda i: (i, 1))],
        out_specs=[wide, wide, vec],
        out_shape=[jax.ShapeDtypeStruct((t, GLA_WIDTH), F32), jax.ShapeDtypeStruct((t, GLA_WIDTH), BF16),
                   jax.ShapeDtypeStruct((1, GLA_DV), F32)],
        compiler_params=_params(("arbitrary",), VMEM_LIMIT),
    )(o_f, o_b, proj, gn, dmix)


CONV_TR = 1024
CONV_TC = 512
HALO = 8
HALO16 = 16


def _conv3(u, w, b):
    n = u.shape[0]
    return pltpu.roll(u, 1, 0) * w[0:1] + u * w[1:2] + pltpu.roll(u, n - 1, 0) * w[2:3] + b


def _conv_ext(main_ref, prev_ref, next_ref, r, nr):
    prev = prev_ref[...].astype(F32)[-HALO:] * (r > 0).astype(F32)
    nxt = next_ref[...].astype(F32)[:HALO] * (r < nr - 1).astype(F32)
    return jnp.concatenate([prev, main_ref[...].astype(F32), nxt], axis=0)


def _conv_specs(t, halo, half=None):
    per = CONV_TR // halo
    last = t // halo - 1
    lead = () if half is None else (None,)
    at = (lambda *ix: ix) if half is None else (lambda *ix: (half,) + ix)
    return [pl.BlockSpec(lead + (CONV_TR, CONV_TC), lambda j, r: at(r, j)),
            pl.BlockSpec(lead + (halo, CONV_TC), lambda j, r: at(jnp.maximum(r * per - 1, 0), j)),
            pl.BlockSpec(lead + (halo, CONV_TC), lambda j, r: at(jnp.minimum((r + 1) * per, last), j))]


def _ffn_mid_fwd(u, cw_g, cw_v, cb_g, cb_v):
    _, t, f = u.shape
    nr = t // CONV_TR

    def body(ug, ugp, ugn, uv, uvp, uvn, wg, wv, bg, bv, a_ref):
        r = pl.program_id(1)
        gate = _conv3(_conv_ext(ug, ugp, ugn, r, nr), wg[...], bg[...])[HALO:HALO + CONV_TR]
        val = _conv3(_conv_ext(uv, uvp, uvn, r, nr), wv[...], bv[...])[HALO:HALO + CONV_TR]
        a_ref[...] = (_silu(gate) * val).astype(BF16)

    w_spec = pl.BlockSpec((3, CONV_TC), lambda j, r: (0, j))
    b_spec = pl.BlockSpec((1, CONV_TC), lambda j, r: (0, j))
    return pl.pallas_call(
        body, name="ffn_mid_fwd", grid=(f // CONV_TC, nr),
        in_specs=_conv_specs(t, HALO, 0) + _conv_specs(t, HALO, 1) + [w_spec, w_spec, b_spec, b_spec],
        out_specs=pl.BlockSpec((CONV_TR, CONV_TC), lambda j, r: (r, j)),
        out_shape=jax.ShapeDtypeStruct((t, f), BF16),
        compiler_params=_params(("parallel", "parallel"), VMEM_LIMIT),
    )(u, u, u, u, u, u, cw_g, cw_v, cb_g, cb_v)


def _ffn_mid_bwd(u, cw_g, cw_v, cb_g, cb_v, da):
    _, t, f = u.shape
    nr = t // CONV_TR
    ext = CONV_TR + 2 * HALO

    def body(ug, ugp, ugn, uv, uvp, uvn, dam, dap, dan, wg, wv, bg, bv,
             du_ref, dwg_ref, dwv_ref, dbg_ref, dbv_ref):
        r = pl.program_id(1)
        shifted = []
        for main, prev, nxt in ((ug, ugp, ugn), (uv, uvp, uvn)):
            x = _conv_ext(main, prev, nxt, r, nr)
            shifted.append((pltpu.roll(x, 1, 0), x, pltpu.roll(x, ext - 1, 0)))
        da_x = _conv_ext(dam, dap, dan, r, nr)
        wg_t, wv_t = wg[...], wv[...]
        gate = shifted[0][0] * wg_t[0:1] + shifted[0][1] * wg_t[1:2] + shifted[0][2] * wg_t[2:3] + bg[...]
        val = shifted[1][0] * wv_t[0:1] + shifted[1][1] * wv_t[1:2] + shifted[1][2] * wv_t[2:3] + bv[...]
        sig = jax.nn.sigmoid(gate)
        silu = gate * sig
        d_val = da_x * silu
        d_gate = da_x * val * (sig + silu * (1.0 - sig))
        own = slice(HALO, HALO + CONV_TR)
        for half, (xs3, d, wt, dw_ref, db_ref) in enumerate(((shifted[0], d_gate, wg_t, dwg_ref, dbg_ref),
                                                            (shifted[1], d_val, wv_t, dwv_ref, dbv_ref))):
            du = pltpu.roll(d, ext - 1, 0) * wt[0:1] + d * wt[1:2] + pltpu.roll(d, 1, 0) * wt[2:3]
            du_ref[half] = du[own].astype(BF16)
            d_own = d[own]
            dw = jnp.concatenate([jnp.sum(x[own] * d_own, axis=0, keepdims=True) for x in xs3], axis=0)
            db = jnp.sum(d_own, axis=0, keepdims=True)

            @pl.when(r == 0)
            def _():
                dw_ref[...] = jnp.zeros_like(dw_ref)
                db_ref[...] = jnp.zeros_like(db_ref)

            dw_ref[...] += dw
            db_ref[...] += db

    w_spec = pl.BlockSpec((3, CONV_TC), lambda j, r: (0, j))
    b_spec = pl.BlockSpec((1, CONV_TC), lambda j, r: (0, j))
    return pl.pallas_call(
        body, name="ffn_mid_bwd", grid=(f // CONV_TC, nr),
        in_specs=(_conv_specs(t, HALO, 0) + _conv_specs(t, HALO, 1) + _conv_specs(t, HALO16)
                  + [w_spec, w_spec, b_spec, b_spec]),
        out_specs=[pl.BlockSpec((2, CONV_TR, CONV_TC), lambda j, r: (0, r, j)), w_spec, w_spec, b_spec, b_spec],
        out_shape=[jax.ShapeDtypeStruct((2, t, f), BF16),
                   jax.ShapeDtypeStruct((3, f), F32), jax.ShapeDtypeStruct((3, f), F32),
                   jax.ShapeDtypeStruct((1, f), F32), jax.ShapeDtypeStruct((1, f), F32)],
        compiler_params=_params(("parallel", "arbitrary"), VMEM_LIMIT),
    )(u, u, u, u, u, u, da, da, da, cw_g, cw_v, cb_g, cb_v)


def _down_proj_loss(act, w_down, x1, target, *, tm, tn):
    t, f = act.shape
    d = w_down.shape[1]

    def body(a_ref, w_ref, x_ref, t_ref, loss_ref, dy_ref, dyb_ref):
        y = lax.dot_general(a_ref[...], w_ref[...], _DIMS["nn"], preferred_element_type=F32) + x_ref[...]
        err = y - t_ref[...]
        dy = err * (1.0 / d)
        dy_ref[...] = dy
        dyb_ref[...] = dy.astype(BF16)
        part = 0.5 * jnp.sum(jnp.sum(err * err, axis=-1, keepdims=True) * (1.0 / d), axis=0, keepdims=True)

        @pl.when((pl.program_id(0) == 0) & (pl.program_id(1) == 0))
        def _():
            loss_ref[...] = jnp.zeros_like(loss_ref)

        loss_ref[...] += jnp.broadcast_to(part, loss_ref.shape)

    tile = pl.BlockSpec((tm, tn), lambda i, j: (i, j))
    return pl.pallas_call(
        body, name="down_proj_loss", grid=(t // tm, d // tn),
        in_specs=[pl.BlockSpec((tm, f), lambda i, j: (i, 0)), pl.BlockSpec((f, tn), lambda i, j: (0, j)), tile, tile],
        out_specs=[pl.BlockSpec((1, 128), lambda i, j: (0, 0)), tile, tile],
        out_shape=[jax.ShapeDtypeStruct((1, 128), F32), jax.ShapeDtypeStruct((t, d), F32),
                   jax.ShapeDtypeStruct((t, d), BF16)],
        compiler_params=_params(("arbitrary", "arbitrary"), VMEM_LIMIT),
    )(act, w_down, x1, target)


ANY = pl.BlockSpec(memory_space=pl.ANY)


def _position():
    return lax.axis_index("x"), lax.axis_index("y"), lax.axis_index("c")


def _other_chips(x, y):
    return [(1 - x, y), (x, 1 - y), (1 - x, 1 - y)]


def _handshake(peers):
    barrier = pltpu.get_barrier_semaphore()
    for peer in peers:
        pl.semaphore_signal(barrier, inc=1, device_id=peer, device_id_type=MESH)
    pl.semaphore_wait(barrier, len(peers))


def _exchange(body, operands, out_shapes, sems, *, name, collective_id):
    n_in, n_out = len(operands), len(out_shapes)

    def run(*refs):
        body(refs[:n_in], refs[n_in:n_in + n_out], *refs[n_in + n_out:])

    if collective_id is None:
        return pl.pallas_call(run, name=name, in_specs=[ANY] * n_in, out_specs=[ANY] * n_out,
                              out_shape=out_shapes, scratch_shapes=sems)(*operands)
    return pl.kernel(run, name=name, out_type=out_shapes,
                     mesh=plsc.ScalarSubcoreMesh(axis_name="sequencer", num_cores=1), scratch_types=sems,
                     compiler_params=pltpu.CompilerParams(collective_id=collective_id))(*operands)


def _all_gather(blocks, *, name, collective_id=None):
    na = len(blocks)

    def body(ins, outs, send_sems, recv_sems, local_sems):
        x, y, c = _position()
        me, sibling = (x, y, c), (x, y, 1 - c)
        along_x, along_y, diagonal = (1 - x, y, c), (x, 1 - y, c), (1 - x, 1 - y, c)
        first_c = c == 0
        relay_from = (jnp.where(first_c, x, 1 - x), jnp.where(first_c, 1 - y, y), c)
        relay_to = (jnp.where(first_c, 1 - x, x), jnp.where(first_c, y, 1 - y), c)
        if collective_id is not None:
            _handshake([sibling, along_x, along_y])

        def index(px, py, pc):
            return 4 * px + 2 * py + pc

        def copy(a, k, block, to, src=None):
            dst = outs[a].at[index(*block)]
            return pltpu.make_async_remote_copy(
                src_ref=dst if src is None else src, dst_ref=dst,
                send_sem=send_sems.at[a, k], recv_sem=recv_sems.at[a, k],
                device_id=to, device_id_type=MESH)

        pending = []
        for a in range(na):
            mine = pltpu.make_async_copy(ins[a], outs[a].at[index(*me)], local_sems.at[a])
            mine.start()
            pending.append(mine)
        sent = []
        for a in range(na):
            sent += [copy(a, 0, me, sibling, src=ins[a]), copy(a, 1, me, along_x, src=ins[a]),
                     copy(a, 2, me, along_y, src=ins[a])]
        for cp in sent:
            cp.start()

        def passes_on(k_in, owner, k_out):
            for a in range(na):
                copy(a, k_in, owner, me).wait_recv()
                cp = copy(a, k_out, owner, sibling)
                cp.start()
                sent.append(cp)

        passes_on(1, along_x, 4)
        passes_on(2, along_y, 5)
        for a in range(na):
            cp = copy(a, 3, relay_from, relay_to)
            cp.start()
            sent.append(cp)
        passes_on(3, diagonal, 6)
        for a in range(na):
            copy(a, 0, sibling, me).wait_recv()
            for k, owner in ((4, along_x), (5, along_y), (6, diagonal)):
                copy(a, k, (owner[0], owner[1], 1 - c), me).wait_recv()
        for cp in sent:
            cp.wait_send()
        for cp in pending:
            cp.wait()

    return _exchange(
        body, blocks, [jax.ShapeDtypeStruct((N_DEV,) + b.shape, b.dtype) for b in blocks],
        [pltpu.SemaphoreType.DMA((na, 7)), pltpu.SemaphoreType.DMA((na, 7)), pltpu.SemaphoreType.DMA((na,))],
        name=name, collective_id=collective_id)


def _grad_exchange(grads, parts, *, name, collective_id):
    ng, npart = len(grads), len(parts)

    def body(ins, outs, core_send, core_recv, chip_send, chip_recv, local_sems):
        x, y, c = _position()
        sibling = (x, y, 1 - c)
        chips = _other_chips(x, y)
        _handshake([sibling] + [(px, py, c) for px, py in chips])
        me = 2 * x + y
        copies = []
        for b in range(npart):
            src, dst = ins[ng + b], outs[ng + b]
            own = pltpu.make_async_copy(src.at[me], dst.at[me], local_sems.at[b])
            own.start()
            copies.append(own)
            for j, (px, py) in enumerate(chips):
                cp = pltpu.make_async_remote_copy(
                    src_ref=src.at[2 * px + py], dst_ref=dst.at[me],
                    send_sem=chip_send.at[b, j], recv_sem=chip_recv.at[b, j],
                    device_id=(px, py, c), device_id_type=MESH)
                cp.start()
                copies.append(cp)
        for a in range(ng):
            for k in range(N_CHIP):
                cp = pltpu.make_async_remote_copy(
                    src_ref=ins[a].at[k, 1 - c], dst_ref=outs[a].at[k],
                    send_sem=core_send.at[a, k], recv_sem=core_recv.at[a, k],
                    device_id=sibling, device_id_type=MESH)
                cp.start()
                copies.append(cp)
        for cp in copies:
            cp.wait()

    shapes = ([jax.ShapeDtypeStruct((N_CHIP,) + g.shape[2:], g.dtype) for g in grads]
              + [jax.ShapeDtypeStruct(p.shape, p.dtype) for p in parts])
    sems = [pltpu.SemaphoreType.DMA((max(ng, 1), N_CHIP)), pltpu.SemaphoreType.DMA((max(ng, 1), N_CHIP)),
            pltpu.SemaphoreType.DMA((max(npart, 1), 3)), pltpu.SemaphoreType.DMA((max(npart, 1), 3)),
            pltpu.SemaphoreType.DMA((max(npart, 1),))]
    out = _exchange(body, list(grads) + list(parts), shapes, sems, name=name, collective_id=collective_id)
    return out[:ng], out[ng:]


def _pair_sum(grad, theirs, core, *, tile, name, narrow=False):
    _, _, r, w = grad.shape
    tr, tw = tile
    assert r % tr == 0 and w % tw == 0

    def body(core_ref, mine_ref, theirs_ref, out_ref, *narrow_ref):
        total = mine_ref[...] + theirs_ref[...]
        out_ref[...] = total
        if narrow:
            narrow_ref[0][...] = total.astype(BF16)

    spec = pl.BlockSpec((None, tr, tw), lambda k, i, j, core_ref: (k, i, j))
    shapes = [jax.ShapeDtypeStruct((N_CHIP, r, w), F32)] + [jax.ShapeDtypeStruct((N_CHIP, r, w), BF16)] * narrow
    out = pl.pallas_call(
        body, name=name,
        grid_spec=pltpu.PrefetchScalarGridSpec(
            num_scalar_prefetch=1, grid=(N_CHIP, r // tr, w // tw),
            in_specs=[pl.BlockSpec((None, None, tr, tw), lambda k, i, j, core_ref: (k, core_ref[0], i, j)), spec],
            out_specs=[spec] * len(shapes)),
        out_shape=shapes,
        compiler_params=_params(("parallel", "parallel", "parallel"), VMEM_LIMIT),
    )(core, grad, theirs)
    return tuple(out) if narrow else out[0]


class _ReduceScatter:
    def __init__(self, core):
        self.core = core
        self.pending = None
        self.results = {}
        self.launches = 0

    def push(self, tag, grads, rows, then, narrow=False):
        pair, kept, prev_tag = [], [], None
        if self.pending is not None:
            prev_tag, prev, theirs, prev_rows, prev_narrow = self.pending
            pair = [_pair_sum(g, s, self.core, tile=tile, name=f"pair_sum_{prev_tag}_{i}", narrow=prev_narrow)
                    for i, (g, s, tile) in enumerate(zip(prev, theirs, prev_rows))]
            if prev_narrow:
                kept, pair = [p[0] for p in pair], [p[1] for p in pair]
        grads, pair, then = lax.optimization_barrier((list(grads), pair, then))
        grads = [g.reshape((N_CHIP, 2) + g.shape[1:]) for g in grads]
        self.launches += 1
        theirs, parts = _grad_exchange(grads, pair, name=f"grad_exchange_{self.launches}",
                                       collective_id=1 + self.launches)
        if prev_tag is not None:
            self.results[prev_tag] = (parts, kept)
        self.pending = (tag, grads, theirs, rows, narrow) if tag is not None else None
        return then

    def result(self, tag):
        return self.results[tag]


def _adamw(parts, w, m, v, *, tile, name, own=None, chip=None):
    n, r, cols = parts.shape
    tr, tw = tile
    assert r % tr == 0 and cols % tw == 0 and w.shape == (r, cols)
    c1 = 1.0 - ADAM_B1 ** ADAM_STEP
    c2 = 1.0 - ADAM_B2 ** ADAM_STEP

    def update(g, w_ref, m_ref, v_ref, g_ref, d_ref, nm_ref, nv_ref):
        new_m = ADAM_B1 * m_ref[...] + (1.0 - ADAM_B1) * g
        new_v = ADAM_B2 * v_ref[...] + (1.0 - ADAM_B2) * (g * g)
        m_hat = new_m / c1
        v_hat = new_v / c2
        g_ref[...] = g
        d_ref[...] = -ADAM_LR * (m_hat / (jnp.sqrt(v_hat) + ADAM_EPS) + ADAM_WD * w_ref[...])
        nm_ref[...] = new_m
        nv_ref[...] = new_v

    shape = jax.ShapeDtypeStruct((r, cols), F32)
    if own is None:
        def body(p_ref, *refs):
            g = p_ref[0]
            for k in range(1, n):
                g = g + p_ref[k]
            update(g, *refs)

        spec = pl.BlockSpec((tr, tw), lambda i, j: (i, j))
        return pl.pallas_call(
            body, name=name, grid=(r // tr, cols // tw),
            in_specs=[pl.BlockSpec((n, tr, tw), lambda i, j: (0, i, j)), spec, spec, spec],
            out_specs=[spec] * 4, out_shape=[shape] * 4,
            compiler_params=_params(("parallel", "parallel"), VMEM_LIMIT),
        )(parts, w, m, v)

    def body(chip_ref, p_ref, own_ref, *refs):
        g = None
        for k in range(n):
            term = jnp.where(chip_ref[0] == k, own_ref[...], p_ref[k].astype(F32))
            g = term if g is None else g + term
        update(g, *refs)

    spec = pl.BlockSpec((tr, tw), lambda i, j, chip_ref: (i, j))
    return pl.pallas_call(
        body, name=name,
        grid_spec=pltpu.PrefetchScalarGridSpec(
            num_scalar_prefetch=1, grid=(r // tr, cols // tw),
            in_specs=[pl.BlockSpec((n, tr, tw), lambda i, j, chip_ref: (0, i, j)),
                      pl.BlockSpec((None, tr, tw), lambda i, j, chip_ref: (chip_ref[0], i, j)), spec, spec, spec],
            out_specs=[spec] * 4),
        out_shape=[shape] * 4,
        compiler_params=_params(("parallel", "parallel"), VMEM_LIMIT),
    )(chip, parts, own, w, m, v)


LANES = 128


def _row_offsets(pieces):
    offsets, row = [], 0
    for p in pieces:
        assert p.shape[0] == 1 and p.shape[1] % LANES == 0, p.shape
        offsets.append(row)
        row += p.shape[1] // LANES
    return offsets, row


def _pack_rows(pieces):
    offsets, rows = _row_offsets(pieces)

    def body(*refs):
        out_ref = refs[-1]
        for ref, start in zip(refs[:-1], offsets):
            for j in range(ref.shape[1] // LANES):
                out_ref[start + j:start + j + 1, :] = ref[:, j * LANES:(j + 1) * LANES]

    return pl.pallas_call(body, name="pack_small_grads",
                          out_shape=jax.ShapeDtypeStruct((rows, LANES), F32))(*pieces)


def _adamw_rows(terms, ws, ms, vs):
    n_dev, rows, _ = terms.shape
    offsets, used = _row_offsets(ws)
    assert used + 1 == rows
    c1 = 1.0 - ADAM_B1 ** ADAM_STEP
    c2 = 1.0 - ADAM_B2 ** ADAM_STEP
    nw = len(ws)

    def body(*refs):
        t_ref = refs[0]
        w_refs, m_refs, v_refs = refs[1:1 + nw], refs[1 + nw:1 + 2 * nw], refs[1 + 2 * nw:1 + 3 * nw]
        outs = refs[1 + 3 * nw:]
        total = t_ref[0]
        for k in range(1, n_dev):
            total = total + t_ref[k]
        for i, start in enumerate(offsets):
            for j in range(ws[i].shape[1] // LANES):
                lanes = slice(j * LANES, (j + 1) * LANES)
                g = total[start + j:start + j + 1, :]
                new_m = ADAM_B1 * m_refs[i][:, lanes] + (1.0 - ADAM_B1) * g
                new_v = ADAM_B2 * v_refs[i][:, lanes] + (1.0 - ADAM_B2) * (g * g)
                delta = -ADAM_LR * ((new_m / c1) / (jnp.sqrt(new_v / c2) + ADAM_EPS) + ADAM_WD * w_refs[i][:, lanes])
                for kind, value in enumerate((g, delta, new_m, new_v)):
                    outs[kind * nw + i][:, lanes] = value
        outs[-1][...] = total[used:used + 1, :]

    shapes = [jax.ShapeDtypeStruct(w.shape, F32) for w in ws] * 4 + [jax.ShapeDtypeStruct((1, LANES), F32)]
    out = pl.pallas_call(body, name="adamw_replicated", out_shape=shapes,
                         compiler_params=_params(None, VMEM_LIMIT))(terms, *ws, *ms, *vs)
    return [list(out[kind * nw:(kind + 1) * nw]) for kind in range(4)], out[-1]


def _rope_tables(t):
    half = HEAD_DIM // 2
    inv = 1.0 / (ROPE_THETA ** (jnp.arange(half, dtype=F32) / half))
    ang = jnp.arange(t, dtype=jnp.int32).astype(F32)[:, None] * inv[None, :]
    cos, sin = jnp.cos(ang), jnp.sin(ang)
    return jnp.concatenate([cos, cos], axis=1), jnp.concatenate([-sin, sin], axis=1)


IN_KERNEL = IN_MAIN + LR_PAD


def _to_kernel_rows(w_t):
    order = sorted(SEGMENTS.values())
    pad = jnp.zeros((LR_PAD - (IN_TOTAL - IN_MAIN), w_t.shape[1]), w_t.dtype)
    return jnp.concatenate([w_t[src:src + width] for _, src, width in order] + [w_t[IN_MAIN:IN_TOTAL], pad], axis=0)


CONV_TAPS = 3
WA_BLOCK = GLA_KEY_WIDTH // N_DEV
SMALL_SIZES = (CONV_TAPS * UP_BLOCK, GLA_RANK * WA_BLOCK, GLA_RANK * WA_BLOCK)
SMALL_SHAPES = ((CONV_TAPS, UP_BLOCK), (GLA_RANK, WA_BLOCK), (GLA_RANK, WA_BLOCK))
SMALL_ROWS = sum(SMALL_SIZES) // LANES


def _small_block(conv, wa_f, wa_b):
    return jnp.concatenate([conv.reshape(-1), wa_f.reshape(-1), wa_b.reshape(-1)]).reshape(SMALL_ROWS, LANES)


def _small_unblock(block):
    flat, out, off = block.reshape(-1), [], 0
    for size, shape in zip(SMALL_SIZES, SMALL_SHAPES):
        out.append(flat[off:off + size].reshape(shape))
        off += size
    return out


def _small_blocks(conv_full, wa_f_full, wa_b_full):
    def by_device(a, width):
        return jnp.transpose(a.reshape(a.shape[0], N_DEV, width), (1, 0, 2)).reshape(N_DEV, -1)
    return jnp.concatenate([by_device(conv_full, UP_BLOCK), by_device(wa_f_full, WA_BLOCK),
                            by_device(wa_b_full, WA_BLOCK)], axis=1).reshape(N_DEV, SMALL_ROWS, LANES)


def _small_unblocks(blocks):
    flat, out, off = blocks.reshape(N_DEV, -1), [], 0
    for size, (rows, width) in zip(SMALL_SIZES, SMALL_SHAPES):
        part = flat[:, off:off + size].reshape(N_DEV, rows, width)
        out.append(jnp.transpose(part, (1, 0, 2)).reshape(rows, N_DEV * width))
        off += size
    return out


def _local_step(xs, target, norm1_g, w_in8, gq, gk, attn_sink, w2, ba2, gla_norm_g, w_out_full, norm2_g,
                w_up8, cw_g, cw_v, cb_g, cb_v, w_down_full, rs=None):
    t = xs.shape[0]
    tm = min(1024, t)
    tall = min(2048, t)
    cos, sin_signed = _rope_tables(t)
    sink = attn_sink.reshape(ATTN_HEADS)

    h1 = _rmsnorm_fwd(xs, norm1_g, name="norm1_fwd")
    w_in8, h1, cos, sin_signed = lax.optimization_barrier((w_in8, h1, cos, sin_signed))
    w_in_k = _to_kernel_rows(w_in8.reshape(IN_TOTAL, D_MODEL))
    proj = _matmul(h1, w_in_k, "nt", tm=tall, tn=IN_MAIN // 4, tk=D_MODEL, n_out=IN_MAIN, name="proj_main")
    proj_lr = _matmul(h1, w_in_k[IN_MAIN:], "nt", tm=tm, tn=LR_PAD, tk=D_MODEL, name="proj_lr")
    qa, ka, va = _attn_prep_fwd(proj, cos, sin_signed, gq, gk)
    o_attn = _attn_fwd(qa, ka, va, sink)
    g_dec = _gla_prep_fwd(proj_lr, w2, ba2)
    o_f, o_b, s_f, s_b = _gla_fwd(proj, g_dec)
    o_gla = _gla_out_fwd(o_f, o_b, proj, gla_norm_g)
    x1 = _out_proj(o_attn, o_gla, w_out_full, xs, tm=tall, tn=512)
    h2 = _rmsnorm_fwd(x1, norm2_g, name="norm2_fwd")
    u = _up_proj(h2, w_up8, tm=tm)
    act = _ffn_mid_fwd(u, cw_g, cw_v, cb_g, cb_v)
    loss_part, dy, dy_b = _down_proj_loss(act, w_down_full, x1, target, tm=tm, tn=512)

    d_act = _matmul(dy_b, w_down_full, "nt", tm=tall, tn=D_FF // 4, tk=D_MODEL, out_dtype=BF16, name="d_act")
    dw_down = _matmul(act, dy_b, "tn", tm=D_FF // 4, tn=512, tk=t, name="dw_down")
    if rs is not None:
        d_act = rs.push("w_down", [dw_down.reshape(N_DEV, D_FF // N_DEV, D_MODEL)], [(64, D_MODEL)], d_act)
    du, dcw_g, dcw_v, dcb_g, dcb_v = _ffn_mid_bwd(u, cw_g, cw_v, cb_g, cb_v, d_act)
    dw_up8 = _up_proj_dw(h2, du, tm=512, tk=t)
    if rs is not None:
        du = rs.push("w_up", [dw_up8], [(256, UP_BLOCK)], du)
    dh2 = _up_proj_dx(du, w_up8, tm=tm, tn=1024)
    dx1, dx1_b, d_norm2 = _rmsnorm_bwd(x1, norm2_g, dh2, dy, name="norm2_bwd", also_bf16=True)
    dmix = _matmul(dx1_b, w_out_full, "nt", tm=tall, tn=1024, tk=D_MODEL, name="d_mix")
    dw_out = _out_proj_dw(o_attn, o_gla, dx1_b, tn=512)
    if rs is not None:
        dmix = rs.push("w_out", [dw_out.reshape(N_DEV, D_MODEL // N_DEV, D_MODEL)], [(256, D_MODEL)], dmix)
    do_gla, d_gate, d_gla_norm = _gla_out_bwd(o_f, o_b, proj, gla_norm_g, dmix)
    (dq_f, dk_f, dv_f, dg_f, dq_b, dk_b, dv_b, dg_b) = _gla_bwd(proj, g_dec, s_f, s_b, do_gla)
    d_lr, d_w2, d_ba2 = _gla_prep_bwd(proj_lr, w2, ba2, dg_f, dg_b)
    dqa, dk_lo, dk_mid, dk_hi, dv_lo, dv_mid, dv_hi, d_sink8 = _attn_bwd(qa, ka, va, sink, dmix)
    d_qa, d_ka, d_va, d_qn, d_kn = _attn_prep_bwd(proj, cos, sin_signed, gq, gk, dqa,
                                                  (dk_lo, dk_mid, dk_hi), (dv_lo, dv_mid, dv_hi))
    d_seg = {"qa": d_qa, "gate": d_gate, "vg": (dv_f + dv_b).astype(BF16), "qg": (dq_f + dq_b).astype(BF16),
             "kg": (dk_f + dk_b).astype(BF16), "ka": d_ka, "va": d_va}
    d_proj = jnp.concatenate([d_seg[k] for k in sorted(SEGMENTS, key=lambda k: SEGMENTS[k][0])] + [d_lr], axis=1)
    dw_in_t = _in_proj_dw_lr(_in_proj_dw(d_proj, h1, tn=1024), d_lr, h1)
    if rs is not None:
        per_in = IN_TOTAL // N_DEV
        small_grad = _small_blocks(jnp.concatenate([dcw_g, dcw_v], axis=1), d_w2[:GLA_RANK, :GLA_KEY_WIDTH],
                                   d_w2[GLA_RANK:2 * GLA_RANK, GLA_KEY_WIDTH:])
        d_proj, d_lr = rs.push("w_in", [dw_in_t.reshape(N_DEV, per_in, D_MODEL), small_grad],
                               [(per_in, 512), small_grad.shape[1:]], (d_proj, d_lr), narrow=True)
    dh1 = _matmul(d_proj, w_in_k, "nn", tm=tm, tn=512, tk=IN_KERNEL, name="dh1")
    if rs is not None:
        dh1 = rs.push(None, [], [], dh1)
    grad_x, d_norm1 = _rmsnorm_bwd(xs, norm1_g, dh1, dx1, name="norm1_bwd", also_bf16=False)
    return (loss_part, grad_x, dw_in_t, dw_out, dw_up8, dw_down, dcw_g, dcw_v, dcb_g, dcb_v,
            d_w2, d_ba2, d_norm1, d_norm2, d_qn, d_kn, d_sink8, d_gla_norm)


def kernel(x, norm1_g, w_in, attn_q_norm_g, attn_k_norm_g, attn_sink, gla_wa2_fwd, gla_ba_fwd, gla_wa2_bwd, gla_ba_bwd, gla_out_norm_g, w_out, norm2_g, w_up, conv_w, conv_b, w_down, loss_target, m_norm1_g, m_w_in, m_attn_q_norm_g, m_attn_k_norm_g, m_attn_sink, m_gla_wa2_fwd, m_gla_ba_fwd, m_gla_wa2_bwd, m_gla_ba_bwd, m_gla_out_norm_g, m_w_out, m_norm2_g, m_w_up, m_conv_w, m_conv_b, m_w_down, v_norm1_g, v_w_in, v_attn_q_norm_g, v_attn_k_norm_g, v_attn_sink, v_gla_wa2_fwd, v_gla_ba_fwd, v_gla_wa2_bwd, v_gla_ba_bwd, v_gla_out_norm_g, v_w_out, v_norm2_g, v_w_up, v_conv_w, v_conv_b, v_w_down):
    t = x.shape[1]
    xs = x.reshape(t, D_MODEL)
    target = loss_target.reshape(t, D_MODEL)
    core = lax.axis_index("c").astype(jnp.int32).reshape(1)

    w_small = _small_block(conv_w[0], gla_wa2_fwd[0], gla_wa2_bwd[0])
    w_in_t, m_in_t, v_in_t = (jnp.swapaxes(a[0], 0, 1) for a in (w_in, m_w_in, v_w_in))
    g_in, g_small = _all_gather([w_in_t.astype(BF16), w_small], name="gather_w_in", collective_id=8)
    g_out, w_up8, g_down = _all_gather([w_out[0].astype(BF16), w_up[0].astype(BF16), w_down[0].astype(BF16)],
                                       name="gather_later_weights", collective_id=1)
    w_out_full = g_out.reshape(D_MODEL, D_MODEL)
    w_down_full = g_down.reshape(D_FF, D_MODEL)
    conv_w_full, wa2_f, wa2_b = _small_unblocks(g_small)
    cw_g, cw_v = conv_w_full[:, :D_FF], conv_w_full[:, D_FF:]
    cb_g, cb_v = conv_b[:, :D_FF], conv_b[:, D_FF:]
    w2 = jnp.zeros((LR_PAD, 2 * GLA_KEY_WIDTH), F32)
    w2 = w2.at[:GLA_RANK, :GLA_KEY_WIDTH].set(wa2_f).at[GLA_RANK:2 * GLA_RANK, GLA_KEY_WIDTH:].set(wa2_b)
    ba2 = jnp.concatenate([gla_ba_fwd, gla_ba_bwd], axis=1)
    rs = _ReduceScatter(core)
    (loss_part, grad_x, _, _, _, _, _, _, dcb_g, dcb_v, _, d_ba2,
     d_norm1, d_norm2, d_qn, d_kn, d_sink8, d_gla_norm) = _local_step(
        xs, target, norm1_g, g_in, attn_q_norm_g, attn_k_norm_g, attn_sink, w2, ba2, gla_out_norm_g,
        w_out_full, norm2_g, w_up8, cw_g, cw_v, cb_g, cb_v, w_down_full, rs=rs)

    (part_down,), (part_up,), (part_out,) = rs.result("w_down")[0], rs.result("w_up")[0], rs.result("w_out")[0]
    (part_in, part_small), (own_in, own_small) = rs.result("w_in")
    chip = (2 * lax.axis_index("x") + lax.axis_index("y")).astype(jnp.int32).reshape(1)
    m_small = _small_block(m_conv_w[0], m_gla_wa2_fwd[0], m_gla_wa2_bwd[0])
    v_small = _small_block(v_conv_w[0], v_gla_wa2_fwd[0], v_gla_wa2_bwd[0])
    upd_in = _adamw(part_in, w_in_t, m_in_t, v_in_t, tile=(IN_TOTAL // N_DEV, 512), name="adamw_w_in",
                    own=own_in, chip=chip)
    upd_in = [jnp.swapaxes(u, 0, 1) for u in upd_in]
    upd_out = _adamw(part_out, w_out[0], m_w_out[0], v_w_out[0], tile=(256, D_MODEL), name="adamw_w_out")
    upd_up = _adamw(part_up, w_up[0], m_w_up[0], v_w_up[0], tile=(256, UP_BLOCK), name="adamw_w_up")
    upd_down = _adamw(part_down, w_down[0], m_w_down[0], v_w_down[0], tile=(64, D_MODEL), name="adamw_w_down")
    upd_small = _adamw(part_small, w_small, m_small, v_small, tile=part_small.shape[1:],
                       name="adamw_small", own=own_small, chip=chip)
    upd_small = [_small_unblock(u) for u in upd_small]

    rep_names = ["norm1_g", "attn_q_norm_g", "attn_k_norm_g", "attn_sink", "gla_ba_fwd", "gla_ba_bwd",
                 "gla_out_norm_g", "norm2_g", "conv_b"]
    def whole_lanes(sink_like):
        return jnp.pad(sink_like, ((0, 0), (0, LANES - ATTN_HEADS)))

    rep_w = [norm1_g, attn_q_norm_g, attn_k_norm_g, whole_lanes(attn_sink), gla_ba_fwd, gla_ba_bwd, gla_out_norm_g,
             norm2_g, conv_b]
    rep_m = [m_norm1_g, m_attn_q_norm_g, m_attn_k_norm_g, whole_lanes(m_attn_sink), m_gla_ba_fwd, m_gla_ba_bwd,
             m_gla_out_norm_g, m_norm2_g, m_conv_b]
    rep_v = [v_norm1_g, v_attn_q_norm_g, v_attn_k_norm_g, whole_lanes(v_attn_sink), v_gla_ba_fwd, v_gla_ba_bwd,
             v_gla_out_norm_g, v_norm2_g, v_conv_b]
    d_sink = whole_lanes(d_sink8[:, :GQA_GROUP, 0].reshape(1, ATTN_HEADS))
    rep_g = [d_norm1, d_qn, d_kn, d_sink, d_ba2[:, :GLA_KEY_WIDTH], d_ba2[:, GLA_KEY_WIDTH:], d_gla_norm, d_norm2,
             jnp.concatenate([dcb_g, dcb_v], axis=1)]
    (rep_terms,) = _all_gather([_pack_rows(rep_g + [loss_part])], name="gather_small_grads", collective_id=7)
    upd_rep, loss_row = _adamw_rows(rep_terms, rep_w, rep_m, rep_v)
    sink_at = rep_names.index("attn_sink")
    for kind in range(4):
        upd_rep[kind][sink_at] = upd_rep[kind][sink_at][:, :ATTN_HEADS]
    loss = loss_row[0, 0]

    order = ["norm1_g", "w_in", "attn_q_norm_g", "attn_k_norm_g", "attn_sink", "gla_wa2_fwd", "gla_ba_fwd",
             "gla_wa2_bwd", "gla_ba_bwd", "gla_out_norm_g", "w_out", "norm2_g", "w_up", "conv_w", "conv_b", "w_down"]
    outs = [loss, grad_x.reshape(1, t, D_MODEL)]
    for kind in range(4):
        by_name = {n: upd_rep[kind][i] for i, n in enumerate(rep_names)}
        by_name["w_in"] = upd_in[kind][None]
        by_name["w_out"] = upd_out[kind][None]
        by_name["w_up"] = upd_up[kind][None]
        by_name["w_down"] = upd_down[kind][None]
        by_name["conv_w"] = upd_small[kind][0][None]
        by_name["gla_wa2_fwd"] = upd_small[kind][1][None]
        by_name["gla_wa2_bwd"] = upd_small[kind][2][None]
        outs += [by_name[n] for n in order]
    return tuple(outs)
```

```python
import functools

import jax
import jax.numpy as jnp
from jax import lax
from jax.experimental import pallas as pl
from jax.experimental.pallas import tpu as pltpu
from jax.experimental.pallas import tpu_sc as plsc

F32 = jnp.float32
BF16 = jnp.bfloat16

D_MODEL = 2048
HEAD_DIM = 128
ATTN_WIDTH = 1024
ATTN_HEADS = 8
KV_HEADS = 2
GQA_GROUP = 4
KV_WIDTH = KV_HEADS * HEAD_DIM
ATTN_BLOCK = 128
WINDOW = 128
ROPE_THETA = 10000.0
GLA_HEADS = 4
GLA_DK = 128
GLA_DV = 256
GLA_KEY_WIDTH = 512
GLA_WIDTH = 1024
GLA_RANK = 16
GLA_GATE_NORMALIZER = 16.0
GLA_CHUNK = 64
GLA_PER_STEP = 4
D_FF = 5632
NORM_EPS = 1e-6
IN_TOTAL = 4640
IN_MAIN = 4608
LR_PAD = 128
N_DEV = 8
N_CHIP = 4

ADAM_LR = 0.001
ADAM_B1 = 0.9
ADAM_B2 = 0.999
ADAM_EPS = 1e-08
ADAM_WD = 0.01
ADAM_STEP = 10

SEGMENTS = {
    "qa": (0, 0, 1024),
    "gate": (1024, 3584, 1024),
    "vg": (2048, 2560, 1024),
    "qg": (3072, 1536, 512),
    "kg": (3584, 2048, 512),
    "ka": (4096, 1024, 256),
    "va": (4352, 1280, 256),
}

VMEM_LIMIT = 56 * 1024 * 1024
MESH = pl.DeviceIdType.MESH


def _params(semantics=None, vmem=None):
    return pltpu.CompilerParams(dimension_semantics=semantics, vmem_limit_bytes=vmem)


_DIMS = {
    "nn": (((1,), (0,)), ((), ())),
    "nt": (((1,), (1,)), ((), ())),
    "tn": (((0,), (0,)), ((), ())),
}


def _mxu(a, b, mode):
    return lax.dot_general(a.astype(BF16), b.astype(BF16), _DIMS[mode], preferred_element_type=F32)


@functools.partial(jax.custom_vjp, nondiff_argnums=(2,))
def bdot(a, b, mode):
    return _mxu(a, b, mode)


def _bdot_fwd(a, b, mode):
    return _mxu(a, b, mode), (a, b)


def _bdot_bwd(mode, res, g):
    a, b = res
    if mode == "nn":
        return _mxu(g, b, "nt"), _mxu(a, g, "tn")
    if mode == "nt":
        return _mxu(g, b, "nn"), _mxu(g, a, "tn")
    return _mxu(b, g, "nt"), _mxu(a, g, "nn")


bdot.defvjp(_bdot_fwd, _bdot_bwd)


def _rms(x, g):
    return x * lax.rsqrt(jnp.mean(x * x, axis=-1, keepdims=True) + NORM_EPS) * g


def _rope(x, cos, sin_signed):
    return x * cos + pltpu.roll(x, HEAD_DIM // 2, 1) * sin_signed


def _rope_transposed(d, cos, sin_signed):
    return d * cos + pltpu.roll(d * sin_signed, HEAD_DIM // 2, 1)


def _silu(x):
    return x * jax.nn.sigmoid(x)


def _log_sigmoid(z):
    return -(jnp.maximum(-z, 0.0) + jnp.log(1.0 + jnp.exp(-jnp.abs(z))))


def _matmul_call(args, in_specs, o_spec, out_shape, grid, mode, nk, acc_shape, *, name, has_res=False,
                 prefetch=None, load_b=lambda ref: ref[...]):
    dims = _DIMS[mode]
    out_dtype = out_shape.dtype
    n_pre = 0 if prefetch is None else 1

    def body(*refs):
        refs = refs[n_pre:]
        if has_res:
            a_ref, b_ref, r_ref, o_ref = refs[:4]
            rest = refs[4:]
        else:
            a_ref, b_ref, o_ref = refs[:3]
            r_ref = None
            rest = refs[3:]
        part = lax.dot_general(a_ref[...], load_b(b_ref), dims, preferred_element_type=F32)

        def finish(acc):
            if r_ref is not None:
                acc = acc + r_ref[...]
            o_ref[...] = acc.astype(out_dtype)

        if nk == 1:
            finish(part)
        else:
            acc_ref = rest[0]
            kk = pl.program_id(2)

            @pl.when(kk == 0)
            def _():
                acc_ref[...] = part

            @pl.when(kk > 0)
            def _():
                acc_ref[...] += part

            @pl.when(kk == nk - 1)
            def _():
                finish(acc_ref[...])

    scratch = [pltpu.VMEM(acc_shape, F32)] if nk > 1 else []
    params = _params(("parallel", "parallel", "arbitrary"), VMEM_LIMIT)
    if prefetch is None:
        return pl.pallas_call(body, name=name, grid=grid, in_specs=in_specs, out_specs=o_spec, out_shape=out_shape,
                              scratch_shapes=scratch, compiler_params=params)(*args)
    return pl.pallas_call(
        body, name=name,
        grid_spec=pltpu.PrefetchScalarGridSpec(num_scalar_prefetch=1, grid=grid, in_specs=in_specs,
                                               out_specs=o_spec, scratch_shapes=scratch),
        out_shape=out_shape, compiler_params=params)(prefetch, *args)


def _matmul(a, b, mode, *, tm, tn, tk, out_dtype=F32, res=None, name, n_out=None):
    if mode == "nn":
        (m, k), (k2, n) = a.shape, b.shape
    elif mode == "nt":
        (m, k), (n, k2) = a.shape, b.shape
    else:
        (k, m), (k2, n) = a.shape, b.shape
    n = n if n_out is None else n_out
    assert k == k2 and m % tm == 0 and n % tn == 0 and k % tk == 0, (name, a.shape, b.shape, tm, tn, tk)
    if mode == "tn":
        a_spec = pl.BlockSpec((tk, tm), lambda i, j, kk: (kk, i))
    else:
        a_spec = pl.BlockSpec((tm, tk), lambda i, j, kk: (i, kk))
    if mode == "nt":
        b_spec = pl.BlockSpec((tn, tk), lambda i, j, kk: (j, kk))
    else:
        b_spec = pl.BlockSpec((tk, tn), lambda i, j, kk: (kk, j))
    o_spec = pl.BlockSpec((tm, tn), lambda i, j, kk: (i, j))
    in_specs, args = [a_spec, b_spec], [a, b]
    if res is not None:
        in_specs.append(o_spec)
        args.append(res)
    return _matmul_call(args, in_specs, o_spec, jax.ShapeDtypeStruct((m, n), out_dtype),
                        (m // tm, n // tn, k // tk), mode, k // tk, (tm, tn), name=name, has_res=res is not None)


def _out_proj(o_attn, o_gla, w_out, x, *, tm, tn):
    t, ka = o_attn.shape
    kg = o_gla.shape[1]

    def body(a_ref, g_ref, w_ref, x_ref, o_ref):
        acc = lax.dot_general(a_ref[...], w_ref[:ka], _DIMS["nn"], preferred_element_type=F32)
        acc = acc + lax.dot_general(g_ref[...], w_ref[ka:], _DIMS["nn"], preferred_element_type=F32)
        o_ref[...] = acc + x_ref[...]

    tile = pl.BlockSpec((tm, tn), lambda i, j: (i, j))
    return pl.pallas_call(
        body, name="out_proj", grid=(t // tm, D_MODEL // tn),
        in_specs=[pl.BlockSpec((tm, ka), lambda i, j: (i, 0)), pl.BlockSpec((tm, kg), lambda i, j: (i, 0)),
                  pl.BlockSpec((ka + kg, tn), lambda i, j: (0, j)), tile],
        out_specs=tile, out_shape=jax.ShapeDtypeStruct((t, D_MODEL), F32),
        compiler_params=_params(("parallel", "parallel"), VMEM_LIMIT),
    )(o_attn, o_gla, w_out, x)


def _out_proj_dw(o_attn, o_gla, dx1, *, tn):
    t, ka = o_attn.shape
    assert o_gla.shape == (t, ka)

    def body(a_ref, g_ref, d_ref, o_ref):
        @pl.when(pl.program_id(0) == 0)
        def _():
            o_ref[...] = lax.dot_general(a_ref[...], d_ref[...], _DIMS["tn"], preferred_element_type=F32)

        @pl.when(pl.program_id(0) == 1)
        def _():
            o_ref[...] = lax.dot_general(g_ref[...], d_ref[...], _DIMS["tn"], preferred_element_type=F32)

    whole = pl.BlockSpec((t, ka), lambda i, j: (0, 0))
    return pl.pallas_call(
        body, name="dw_out", grid=(2, D_MODEL // tn),
        in_specs=[whole, whole, pl.BlockSpec((t, tn), lambda i, j: (0, j))],
        out_specs=pl.BlockSpec((ka, tn), lambda i, j: (i, j)),
        out_shape=jax.ShapeDtypeStruct((2 * ka, D_MODEL), F32),
        compiler_params=_params(("parallel", "parallel"), VMEM_LIMIT),
    )(o_attn, o_gla, dx1)


UP_BLOCK = 2 * D_FF // N_DEV


def _up_proj(h2, w_up8, *, tm):
    t = h2.shape[0]
    return _matmul_call(
        [h2, w_up8],
        [pl.BlockSpec((tm, D_MODEL), lambda i, j, kk: (i, 0)),
         pl.BlockSpec((None, D_MODEL, UP_BLOCK), lambda i, j, kk: (j, 0, 0))],
        pl.BlockSpec((None, tm, UP_BLOCK), lambda i, j, kk: (j // N_CHIP, i, j % N_CHIP)),
        jax.ShapeDtypeStruct((2, t, D_FF), F32), (t // tm, N_DEV, 1), "nn", 1, None, name="up_proj")


def _up_proj_dx(du, w_up8, *, tm, tn):
    t = du.shape[1]
    pair = 2
    return _matmul_call(
        [du, w_up8],
        [pl.BlockSpec((None, tm, pair * UP_BLOCK), lambda i, j, kk: (kk // 2, i, kk % 2)),
         pl.BlockSpec((pair, tn, UP_BLOCK), lambda i, j, kk: (kk, j, 0))],
        pl.BlockSpec((tm, tn), lambda i, j, kk: (i, j)),
        jax.ShapeDtypeStruct((t, D_MODEL), F32), (t // tm, D_MODEL // tn, N_DEV // pair), "nt", N_DEV // pair,
        (tm, tn), name="up_proj_dx", load_b=lambda ref: jnp.concatenate([ref[0], ref[1]], axis=1))


def _up_proj_dw(h2, du, *, tm, tk):
    t = h2.shape[0]
    return _matmul_call(
        [h2, du],
        [pl.BlockSpec((tk, tm), lambda j, i, kk: (kk, i)),
         pl.BlockSpec((None, tk, UP_BLOCK), lambda j, i, kk: (j // N_CHIP, kk, j % N_CHIP))],
        pl.BlockSpec((None, tm, UP_BLOCK), lambda j, i, kk: (j, i, 0)),
        jax.ShapeDtypeStruct((N_DEV, D_MODEL, UP_BLOCK), F32), (N_DEV, D_MODEL // tm, t // tk), "tn", t // tk,
        (tm, UP_BLOCK), name="up_proj_dw")


def _up_proj_dw_core(h2, du, core, *, tm, tk, res=None, name):
    t = h2.shape[0]

    def block(j, core_ref):
        return 2 * j + core_ref[0]

    out_spec = pl.BlockSpec((None, tm, UP_BLOCK), lambda j, i, kk, core_ref: (j, i, 0))
    return _matmul_call(
        [h2, du] + ([] if res is None else [res]),
        [pl.BlockSpec((tk, tm), lambda j, i, kk, core_ref: (kk, i)),
         pl.BlockSpec((None, tk, UP_BLOCK),
                      lambda j, i, kk, core_ref: (block(j, core_ref) // N_CHIP, kk, block(j, core_ref) % N_CHIP))]
        + ([] if res is None else [out_spec]),
        out_spec, jax.ShapeDtypeStruct((N_CHIP, D_MODEL, UP_BLOCK), F32), (N_CHIP, D_MODEL // tm, t // tk), "tn",
        t // tk, (tm, UP_BLOCK), name=name, has_res=res is not None, prefetch=core)


IN_TILE = 512


def _in_proj_dw(d_proj, h1, *, tn):
    t = h1.shape[0]
    table = []
    for tile in range(IN_MAIN // IN_TILE):
        dst, src, _ = max(s for s in SEGMENTS.values() if s[0] <= tile * IN_TILE)
        assert (src + tile * IN_TILE - dst) % IN_TILE == 0
        table.append((src + tile * IN_TILE - dst) // IN_TILE)
    assert sorted(table) == list(range(IN_MAIN // IN_TILE))
    return _matmul_call(
        [d_proj, h1],
        [pl.BlockSpec((t, IN_TILE), lambda j, i, kk, tab: (0, i)),
         pl.BlockSpec((t, tn), lambda j, i, kk, tab: (0, j))],
        pl.BlockSpec((IN_TILE, tn), lambda j, i, kk, tab: (tab[i], j)),
        jax.ShapeDtypeStruct((IN_TOTAL, D_MODEL), F32), (D_MODEL // tn, IN_MAIN // IN_TILE, 1), "tn", 1, None,
        name="in_proj_dw", prefetch=jnp.asarray(table, jnp.int32))


def _in_proj_dw_lr(dw_t, d_lr, h1):
    t = h1.shape[0]
    n_lr = IN_TOTAL - IN_MAIN
    tn = 512

    def body(dw_ref, dlr_ref, h1_ref, out_ref):
        full = lax.dot_general(dlr_ref[...], h1_ref[...], _DIMS["tn"], preferred_element_type=F32)
        out_ref[...] = full[:n_lr]

    return pl.pallas_call(
        body, name="in_proj_dw_lr", grid=(D_MODEL // tn,),
        in_specs=[pl.BlockSpec(memory_space=pl.ANY),
                  pl.BlockSpec((t, LR_PAD), lambda j: (0, 0)),
                  pl.BlockSpec((t, tn), lambda j: (0, j))],
        out_specs=pl.BlockSpec((n_lr, tn), lambda j: (IN_MAIN // n_lr, j)),
        out_shape=jax.ShapeDtypeStruct(dw_t.shape, F32),
        input_output_aliases={0: 0},
        compiler_params=_params(("parallel",), VMEM_LIMIT),
    )(dw_t, d_lr, h1)


def _rmsnorm_fwd(x, g, *, name, tr=512):
    t, d = x.shape

    def body(x_ref, g_ref, h_ref):
        h_ref[...] = _rms(x_ref[...], g_ref[...]).astype(BF16)

    return pl.pallas_call(
        body, name=name, grid=(t // tr,),
        in_specs=[pl.BlockSpec((tr, d), lambda i: (i, 0)), pl.BlockSpec((1, d), lambda i: (0, 0))],
        out_specs=pl.BlockSpec((tr, d), lambda i: (i, 0)),
        out_shape=jax.ShapeDtypeStruct((t, d), BF16),
        compiler_params=_params(("parallel",), VMEM_LIMIT),
    )(x, g)


def _rmsnorm_bwd(x, g, dh, dres, *, name, also_bf16, tr=512):
    t, d = x.shape

    def body(x_ref, g_ref, dh_ref, dres_ref, dx_ref, *rest):
        dg_ref = rest[-1]
        _, vjp = jax.vjp(_rms, x_ref[...], g_ref[...])
        dx, dg = vjp(dh_ref[...])
        dx = dx + dres_ref[...]
        dx_ref[...] = dx
        if also_bf16:
            rest[0][...] = dx.astype(BF16)

        @pl.when(pl.program_id(0) == 0)
        def _():
            dg_ref[...] = jnp.zeros_like(dg_ref)

        dg_ref[...] += dg

    row = pl.BlockSpec((tr, d), lambda i: (i, 0))
    vec = pl.BlockSpec((1, d), lambda i: (0, 0))
    return pl.pallas_call(
        body, name=name, grid=(t // tr,),
        in_specs=[row, vec, row, row],
        out_specs=[row] + [row] * also_bf16 + [vec],
        out_shape=[jax.ShapeDtypeStruct((t, d), F32)] + [jax.ShapeDtypeStruct((t, d), BF16)] * also_bf16
                  + [jax.ShapeDtypeStruct((1, d), F32)],
        compiler_params=_params(("arbitrary",), VMEM_LIMIT),
    )(x, g, dh, dres)


def _seg_block(name, width):
    off = SEGMENTS[name][0]
    assert off % width == 0
    return off // width


def _attn_prep_fwd(proj, cos, sin_signed, gq, gk, *, tr=512):
    t = proj.shape[0]

    def body(q_ref, k_ref, v_ref, cos_ref, sin_ref, gq_ref, gk_ref, qo_ref, ko_ref, vo_ref):
        cos_t, sin_t = cos_ref[...], sin_ref[...]
        for h in range(ATTN_HEADS):
            cols = slice(h * HEAD_DIM, (h + 1) * HEAD_DIM)
            qo_ref[:, cols] = _rope(_rms(q_ref[:, cols], gq_ref[...]), cos_t, sin_t).astype(BF16)
        for h in range(KV_HEADS):
            cols = slice(h * HEAD_DIM, (h + 1) * HEAD_DIM)
            ko_ref[:, cols] = _rope(_rms(k_ref[:, cols], gk_ref[...]), cos_t, sin_t).astype(BF16)
        vo_ref[...] = v_ref[...].astype(BF16)

    qb, kb, vb = _seg_block("qa", ATTN_WIDTH), _seg_block("ka", KV_WIDTH), _seg_block("va", KV_WIDTH)
    tab = pl.BlockSpec((tr, HEAD_DIM), lambda i: (i, 0))
    vec = pl.BlockSpec((1, HEAD_DIM), lambda i: (0, 0))
    return pl.pallas_call(
        body, name="attn_prep_fwd", grid=(t // tr,),
        in_specs=[pl.BlockSpec((tr, ATTN_WIDTH), lambda i: (i, qb)),
                  pl.BlockSpec((tr, KV_WIDTH), lambda i: (i, kb)),
                  pl.BlockSpec((tr, KV_WIDTH), lambda i: (i, vb)),
                  tab, tab, vec, vec],
        out_specs=[pl.BlockSpec((tr, ATTN_WIDTH), lambda i: (i, 0)),
                   pl.BlockSpec((tr, KV_WIDTH), lambda i: (i, 0)),
                   pl.BlockSpec((tr, KV_WIDTH), lambda i: (i, 0))],
        out_shape=[jax.ShapeDtypeStruct((t, ATTN_WIDTH), BF16),
                   jax.ShapeDtypeStruct((t, KV_WIDTH), BF16),
                   jax.ShapeDtypeStruct((t, KV_WIDTH), BF16)],
        compiler_params=_params(("parallel",), VMEM_LIMIT),
    )(proj, proj, proj, cos, sin_signed, gq, gk)


def _attn_heads(q, kcat, vcat, sink_col, valid):
    s = bdot(q, kcat, "nt") * (HEAD_DIM ** -0.5)
    s = jnp.where(valid, s, -jnp.inf)
    m = lax.stop_gradient(jnp.maximum(jnp.max(s, axis=-1, keepdims=True), sink_col))
    p = jnp.exp(s - m)
    p = p / (jnp.sum(p, axis=-1, keepdims=True) + jnp.exp(sink_col - m))
    return bdot(p, vcat, "nn")


def _attn_valid(n, t):
    shape = (GQA_GROUP * ATTN_BLOCK, 3 * ATTN_BLOCK)
    qi = lax.broadcasted_iota(jnp.int32, shape, 0) % ATTN_BLOCK
    sj = lax.broadcasted_iota(jnp.int32, shape, 1)
    kpos = n * ATTN_BLOCK - ATTN_BLOCK + sj
    return (jnp.abs(sj - ATTN_BLOCK - qi) <= WINDOW) & (kpos >= 0) & (kpos < t)


def _head_rows(g):
    return slice(g * ATTN_BLOCK, (g + 1) * ATTN_BLOCK)


def _head_cols(g):
    return slice(g * HEAD_DIM, (g + 1) * HEAD_DIM)


def _stack_heads(ref):
    return jnp.concatenate([ref[:, _head_cols(g)] for g in range(GQA_GROUP)], axis=0).astype(F32)


def _sink_column(sink_ref, h):
    return jnp.concatenate([jnp.full((ATTN_BLOCK, 1), sink_ref[h * GQA_GROUP + g], F32)
                            for g in range(GQA_GROUP)], axis=0)


def _attn_specs(nb):
    q_spec = pl.BlockSpec((ATTN_BLOCK, GQA_GROUP * HEAD_DIM), lambda h, n: (n, h))
    kv_specs = [
        pl.BlockSpec((ATTN_BLOCK, HEAD_DIM), lambda h, n: (jnp.maximum(n - 1, 0), h)),
        pl.BlockSpec((ATTN_BLOCK, HEAD_DIM), lambda h, n: (n, h)),
        pl.BlockSpec((ATTN_BLOCK, HEAD_DIM), lambda h, n: (jnp.minimum(n + 1, nb - 1), h)),
    ]
    return q_spec, kv_specs


def _attn_fwd(q, k, v, sink):
    t = q.shape[0]
    nb = t // ATTN_BLOCK

    def body(sink_ref, q_ref, kp_ref, kc_ref, kn_ref, vp_ref, vc_ref, vn_ref, o_ref):
        valid = _attn_valid(pl.program_id(0), t)
        args = []
        for h in range(KV_HEADS):
            q = jnp.concatenate([q_ref[:, _head_cols(h * GQA_GROUP + g)] for g in range(GQA_GROUP)], axis=0)
            kcat = jnp.concatenate([r[:, _head_cols(h)] for r in (kp_ref, kc_ref, kn_ref)], axis=0)
            vcat = jnp.concatenate([r[:, _head_cols(h)] for r in (vp_ref, vc_ref, vn_ref)], axis=0)
            args.append((q.astype(F32), kcat.astype(F32), vcat.astype(F32), _sink_column(sink_ref, h)))
        outs = [_attn_heads(*a, valid=valid).astype(BF16) for a in args]
        for h, o in enumerate(outs):
            for g in range(GQA_GROUP):
                o_ref[:, _head_cols(h * GQA_GROUP + g)] = o[_head_rows(g)]

    q_spec = pl.BlockSpec((ATTN_BLOCK, ATTN_WIDTH), lambda n: (n, 0))
    kv_specs = [pl.BlockSpec((ATTN_BLOCK, KV_WIDTH), lambda n: (jnp.maximum(n - 1, 0), 0)),
                pl.BlockSpec((ATTN_BLOCK, KV_WIDTH), lambda n: (n, 0)),
                pl.BlockSpec((ATTN_BLOCK, KV_WIDTH), lambda n: (jnp.minimum(n + 1, nb - 1), 0))]
    return pl.pallas_call(
        body, name="attn_fwd", grid=(nb,),
        in_specs=[pl.BlockSpec(memory_space=pltpu.SMEM), q_spec] + kv_specs + kv_specs,
        out_specs=q_spec,
        out_shape=jax.ShapeDtypeStruct((t, ATTN_WIDTH), BF16),
        compiler_params=_params(("parallel",), VMEM_LIMIT),
    )(sink, q, k, k, k, v, v, v)


def _attn_bwd(q, k, v, sink, dmix):
    t = q.shape[0]
    nb = t // ATTN_BLOCK

    def body(sink_ref, q_ref, kp_ref, kc_ref, kn_ref, vp_ref, vc_ref, vn_ref, do_ref,
             dq_ref, dk_lo, dk_mid, dk_hi, dv_lo, dv_mid, dv_hi, dsink_ref):
        h, n = pl.program_id(0), pl.program_id(1)
        valid = _attn_valid(n, t)
        kcat = jnp.concatenate([kp_ref[...], kc_ref[...], kn_ref[...]], axis=0).astype(F32)
        vcat = jnp.concatenate([vp_ref[...], vc_ref[...], vn_ref[...]], axis=0).astype(F32)
        _, vjp = jax.vjp(functools.partial(_attn_heads, valid=valid),
                         _stack_heads(q_ref), kcat, vcat, _sink_column(sink_ref, h))
        dq, dk, dv, dsink_col = vjp(_stack_heads(do_ref))
        row = lax.broadcasted_iota(jnp.int32, (8, HEAD_DIM), 0)
        dsink = jnp.zeros((8, HEAD_DIM), F32)
        for g in range(GQA_GROUP):
            dq_ref[:, _head_cols(g)] = dq[_head_rows(g)]
            dsink = dsink + jnp.where(row == g, jnp.sum(dsink_col[_head_rows(g)]), 0.0)
        for i, (dk_ref, dv_ref) in enumerate(((dk_lo, dv_lo), (dk_mid, dv_mid), (dk_hi, dv_hi))):
            rows = slice(i * ATTN_BLOCK, (i + 1) * ATTN_BLOCK)
            dk_ref[...] = dk[rows]
            dv_ref[...] = dv[rows]

        @pl.when(n == 0)
        def _():
            dsink_ref[...] = jnp.zeros_like(dsink_ref)

        dsink_ref[...] += dsink

    q_spec, kv_specs = _attn_specs(nb)
    kv_out = pl.BlockSpec((ATTN_BLOCK, HEAD_DIM), lambda h, n: (n, h))
    kv_shape = jax.ShapeDtypeStruct((t, KV_WIDTH), F32)
    return pl.pallas_call(
        body, name="attn_bwd", grid=(KV_HEADS, nb),
        in_specs=[pl.BlockSpec(memory_space=pltpu.SMEM), q_spec] + kv_specs + kv_specs + [q_spec],
        out_specs=[q_spec] + [kv_out] * 6 + [pl.BlockSpec((None, 8, HEAD_DIM), lambda h, n: (h, 0, 0))],
        out_shape=[jax.ShapeDtypeStruct((t, ATTN_WIDTH), F32)] + [kv_shape] * 6
                  + [jax.ShapeDtypeStruct((KV_HEADS, 8, HEAD_DIM), F32)],
        compiler_params=_params(("parallel", "arbitrary"), VMEM_LIMIT),
    )(sink, q, k, k, k, v, v, v, dmix)


def _attn_prep_bwd(proj, cos, sin_signed, gq, gk, dq, dks, dvs):
    t = proj.shape[0]
    tr = ATTN_BLOCK
    nb = t // tr

    def body(q_ref, k_ref, cos_ref, sin_ref, gq_ref, gk_ref, dq_ref,
             dk_lo, dk_mid, dk_hi, dv_lo, dv_mid, dv_hi,
             dqo_ref, dko_ref, dvo_ref, dgq_ref, dgk_ref):
        n = pl.program_id(0)
        cos_t, sin_t = cos_ref[...], sin_ref[...]
        has_next = (n < nb - 1).astype(F32)
        has_prev = (n > 0).astype(F32)
        dk = dk_lo[...] * has_next + dk_mid[...] + dk_hi[...] * has_prev
        dv = dv_lo[...] * has_next + dv_mid[...] + dv_hi[...] * has_prev
        dvo_ref[...] = dv.astype(BF16)
        dgq = jnp.zeros((1, HEAD_DIM), F32)
        dgk = jnp.zeros((1, HEAD_DIM), F32)
        for h in range(ATTN_HEADS):
            cols = slice(h * HEAD_DIM, (h + 1) * HEAD_DIM)
            _, vjp = jax.vjp(_rms, q_ref[:, cols], gq_ref[...])
            dx, dg = vjp(_rope_transposed(dq_ref[:, cols], cos_t, sin_t))
            dqo_ref[:, cols] = dx.astype(BF16)
            dgq = dgq + dg
        for h in range(KV_HEADS):
            cols = slice(h * HEAD_DIM, (h + 1) * HEAD_DIM)
            _, vjp = jax.vjp(_rms, k_ref[:, cols], gk_ref[...])
            dx, dg = vjp(_rope_transposed(dk[:, cols], cos_t, sin_t))
            dko_ref[:, cols] = dx.astype(BF16)
            dgk = dgk + dg

        @pl.when(n == 0)
        def _():
            dgq_ref[...] = jnp.zeros_like(dgq_ref)
            dgk_ref[...] = jnp.zeros_like(dgk_ref)

        dgq_ref[...] += dgq
        dgk_ref[...] += dgk

    qb, kb = _seg_block("qa", ATTN_WIDTH), _seg_block("ka", KV_WIDTH)
    tab = pl.BlockSpec((tr, HEAD_DIM), lambda i: (i, 0))
    vec = pl.BlockSpec((1, HEAD_DIM), lambda i: (0, 0))
    kv = [pl.BlockSpec((tr, KV_WIDTH), lambda i: (jnp.minimum(i + 1, nb - 1), 0)),
          pl.BlockSpec((tr, KV_WIDTH), lambda i: (i, 0)),
          pl.BlockSpec((tr, KV_WIDTH), lambda i: (jnp.maximum(i - 1, 0), 0))]
    wide = pl.BlockSpec((tr, ATTN_WIDTH), lambda i: (i, 0))
    narrow = pl.BlockSpec((tr, KV_WIDTH), lambda i: (i, 0))
    return pl.pallas_call(
        body, name="attn_prep_bwd", grid=(nb,),
        in_specs=[pl.BlockSpec((tr, ATTN_WIDTH), lambda i: (i, qb)),
                  pl.BlockSpec((tr, KV_WIDTH), lambda i: (i, kb)),
                  tab, tab, vec, vec, wide] + kv + kv,
        out_specs=[wide, narrow, narrow, vec, vec],
        out_shape=[jax.ShapeDtypeStruct((t, ATTN_WIDTH), BF16),
                   jax.ShapeDtypeStruct((t, KV_WIDTH), BF16),
                   jax.ShapeDtypeStruct((t, KV_WIDTH), BF16),
                   jax.ShapeDtypeStruct((1, HEAD_DIM), F32),
                   jax.ShapeDtypeStruct((1, HEAD_DIM), F32)],
        compiler_params=_params(("arbitrary",), VMEM_LIMIT),
    )(proj, proj, cos, sin_signed, gq, gk, dq, *dks, *dvs)


def _decay_fn(lr, w2, ba):
    return _log_sigmoid(bdot(lr, w2, "nn") + ba) / GLA_GATE_NORMALIZER


def _gla_prep_fwd(proj_lr, w2, ba2, *, tr=512):
    t = proj_lr.shape[0]
    width = 2 * GLA_KEY_WIDTH

    def body(lr_ref, w2_ref, ba_ref, g_ref):
        g_ref[...] = _decay_fn(lr_ref[...], w2_ref[...], ba_ref[...])

    return pl.pallas_call(
        body, name="gla_prep_fwd", grid=(t // tr,),
        in_specs=[pl.BlockSpec((tr, LR_PAD), lambda i: (i, 0)),
                  pl.BlockSpec((LR_PAD, width), lambda i: (0, 0)),
                  pl.BlockSpec((1, width), lambda i: (0, 0))],
        out_specs=pl.BlockSpec((tr, width), lambda i: (i, 0)),
        out_shape=jax.ShapeDtypeStruct((t, width), F32),
        compiler_params=_params(("parallel",), VMEM_LIMIT),
    )(proj_lr, w2, ba2)


def _gla_prep_bwd(proj_lr, w2, ba2, dg_f, dg_b, *, tr=512):
    t = proj_lr.shape[0]
    width = 2 * GLA_KEY_WIDTH

    def body(lr_ref, w2_ref, ba_ref, dgf_ref, dgb_ref, dlr_ref, dw2_ref, dba_ref):
        _, vjp = jax.vjp(_decay_fn, lr_ref[...], w2_ref[...], ba_ref[...])
        dlr, dw2, dba = vjp(jnp.concatenate([dgf_ref[...], dgb_ref[...]], axis=1))
        dlr_ref[...] = dlr.astype(BF16)

        @pl.when(pl.program_id(0) == 0)
        def _():
            dw2_ref[...] = jnp.zeros_like(dw2_ref)
            dba_ref[...] = jnp.zeros_like(dba_ref)

        dw2_ref[...] += dw2
        dba_ref[...] += dba

    half = pl.BlockSpec((tr, GLA_KEY_WIDTH), lambda i: (i, 0))
    return pl.pallas_call(
        body, name="gla_prep_bwd", grid=(t // tr,),
        in_specs=[pl.BlockSpec((tr, LR_PAD), lambda i: (i, 0)),
                  pl.BlockSpec((LR_PAD, width), lambda i: (0, 0)),
                  pl.BlockSpec((1, width), lambda i: (0, 0)), half, half],
        out_specs=[pl.BlockSpec((tr, LR_PAD), lambda i: (i, 0)),
                   pl.BlockSpec((LR_PAD, width), lambda i: (0, 0)),
                   pl.BlockSpec((1, width), lambda i: (0, 0))],
        out_shape=[jax.ShapeDtypeStruct((t, LR_PAD), BF16),
                   jax.ShapeDtypeStruct((LR_PAD, width), F32),
                   jax.ShapeDtypeStruct((1, width), F32)],
        compiler_params=_params(("arbitrary",), VMEM_LIMIT),
    )(proj_lr, w2, ba2, dg_f, dg_b)


def _gla_k(h):
    return slice(h * GLA_DK, (h + 1) * GLA_DK)


def _gla_v(h):
    return slice(h * GLA_DV, (h + 1) * GLA_DV)


def _running_sum(x, downward):
    n = x.shape[0]
    row = lax.broadcasted_iota(jnp.int32, x.shape, 0)
    step = 1
    while step < n:
        if downward:
            x = x + jnp.where(row >= step, pltpu.roll(x, step, 0), 0.0)
        else:
            x = x + jnp.where(row < n - step, pltpu.roll(x, n - step, 0), 0.0)
        step *= 2
    return x


@functools.partial(jax.custom_vjp, nondiff_argnums=(1,))
def _cumsum_rows(x, downward):
    return _running_sum(x, downward)


def _cumsum_rows_fwd(x, downward):
    return _running_sum(x, downward), None


def _cumsum_rows_bwd(downward, _, ct):
    return (_running_sum(ct, not downward),)


_cumsum_rows.defvjp(_cumsum_rows_fwd, _cumsum_rows_bwd)


def _gla_chunk(q, k, v, g, state, forward):
    c = GLA_CHUNK
    row = lax.broadcasted_iota(jnp.int32, (c, c), 0)
    col = lax.broadcasted_iota(jnp.int32, (c, c), 1)
    rid = lax.broadcasted_iota(jnp.int32, (c, GLA_DK), 0)
    q = q * (GLA_DK ** -0.5)
    if forward:
        see = row >= col
        upto_ref = rid <= c // 2
    else:
        see = row < col
        upto_ref = rid >= c - 1 - c // 2
    b = _cumsum_rows(g, forward)
    b_last = jnp.sum(g, axis=0, keepdims=True)
    b_ref = jnp.sum(jnp.where(upto_ref, g, 0.0), axis=0, keepdims=True)
    a = bdot(q * jnp.exp(b - b_ref), k * jnp.exp(b_ref - b), "nt")
    a = jnp.where(see, a, 0.0)
    o = bdot(a, v, "nn") + bdot(q * jnp.exp(b), state, "nt")
    new_state = state * jnp.exp(b_last) + bdot(v, k * jnp.exp(b_last - b), "tn")
    return o, new_state


def _gla_fwd(proj, g):
    t = proj.shape[0]
    c, per = GLA_CHUNK, GLA_PER_STEP
    nchunk = t // c
    nstep = nchunk // per
    qb, kb, vb = _seg_block("qg", GLA_KEY_WIDTH), _seg_block("kg", GLA_KEY_WIDTH), _seg_block("vg", GLA_WIDTH)

    def body(qf, kf, vf, gf, qr, kr, vr, gr, of_ref, ob_ref, sf_ref, sb_ref, state):
        @pl.when(pl.program_id(0) == 0)
        def _():
            state[...] = jnp.zeros_like(state)

        dirs = ((qf, kf, vf, gf, of_ref, sf_ref), (qr, kr, vr, gr, ob_ref, sb_ref))
        states = [[state[d, h] for h in range(GLA_HEADS)] for d in range(2)]
        for turn in range(per):
            chunk = (turn, per - 1 - turn)
            args = []
            for d, (q_ref, k_ref, v_ref, g_ref, _, _) in enumerate(dirs):
                rows = pl.ds(chunk[d] * c, c)
                args += [(q_ref[rows, _gla_k(h)], k_ref[rows, _gla_k(h)], v_ref[rows, _gla_v(h)],
                          g_ref[rows, _gla_k(h)], states[d][h]) for h in range(GLA_HEADS)]
            results = [_gla_chunk(*a, forward=(i < GLA_HEADS)) for i, a in enumerate(args)]
            for i, (a, (o, s_out)) in enumerate(zip(args, results)):
                d, h = divmod(i, GLA_HEADS)
                dirs[d][5][chunk[d], h] = a[4]
                dirs[d][4][pl.ds(chunk[d] * c, c), _gla_v(h)] = o
                states[d][h] = s_out
        for d in range(2):
            for h in range(GLA_HEADS):
                state[d, h] = states[d][h]

    specs, outs = [], []
    for d in range(2):
        ci = (lambda i: i) if d == 0 else (lambda i: nstep - 1 - i)
        specs += [pl.BlockSpec((per * c, GLA_KEY_WIDTH), lambda i, ci=ci: (ci(i), qb)),
                  pl.BlockSpec((per * c, GLA_KEY_WIDTH), lambda i, ci=ci: (ci(i), kb)),
                  pl.BlockSpec((per * c, GLA_WIDTH), lambda i, ci=ci: (ci(i), vb)),
                  pl.BlockSpec((per * c, GLA_KEY_WIDTH), lambda i, ci=ci, d=d: (ci(i), d))]
        outs.append(pl.BlockSpec((per * c, GLA_WIDTH), lambda i, ci=ci: (ci(i), 0)))
    for d in range(2):
        ci = (lambda i: i) if d == 0 else (lambda i: nstep - 1 - i)
        outs.append(pl.BlockSpec((per, GLA_HEADS, GLA_DV, GLA_DK), lambda i, ci=ci: (ci(i), 0, 0, 0)))
    o_shape = jax.ShapeDtypeStruct((t, GLA_WIDTH), F32)
    s_shape = jax.ShapeDtypeStruct((nchunk, GLA_HEADS, GLA_DV, GLA_DK), F32)
    return pl.pallas_call(
        body, name="gla_fwd", grid=(nstep,),
        in_specs=specs, out_specs=outs,
        out_shape=[o_shape, o_shape, s_shape, s_shape],
        scratch_shapes=[pltpu.VMEM((2, GLA_HEADS, GLA_DV, GLA_DK), F32)],
        compiler_params=_params(("arbitrary",), VMEM_LIMIT),
    )(proj, proj, proj, g, proj, proj, proj, g)


def _gla_bwd(proj, g, s_f, s_b, do):
    t = proj.shape[0]
    c, per = GLA_CHUNK, GLA_PER_STEP
    nchunk = t // c
    nstep = nchunk // per
    qb, kb, vb = _seg_block("qg", GLA_KEY_WIDTH), _seg_block("kg", GLA_KEY_WIDTH), _seg_block("vg", GLA_WIDTH)

    def body(*refs):
        ins, outs, dstate = refs[:12], refs[12:20], refs[20]

        @pl.when(pl.program_id(0) == 0)
        def _():
            dstate[...] = jnp.zeros_like(dstate)

        dstates = [[dstate[d, h] for h in range(GLA_HEADS)] for d in range(2)]
        for turn in range(per):
            chunk = (per - 1 - turn, turn)
            loaded = []
            for d in range(2):
                q_ref, k_ref, v_ref, g_ref, s_ref, do_ref = ins[6 * d:6 * d + 6]
                rows = pl.ds(chunk[d] * c, c)
                for h in range(GLA_HEADS):
                    loaded.append(((q_ref[rows, _gla_k(h)], k_ref[rows, _gla_k(h)], v_ref[rows, _gla_v(h)],
                                    g_ref[rows, _gla_k(h)], s_ref[chunk[d], h]),
                                   (do_ref[rows, _gla_v(h)], dstates[d][h])))
            grads = []
            for i, (primals, cotangents) in enumerate(loaded):
                _, vjp = jax.vjp(functools.partial(_gla_chunk, forward=(i < GLA_HEADS)), *primals)
                grads.append(vjp(cotangents))
            for i, (dq, dk, dv, dg, ds) in enumerate(grads):
                d, h = divmod(i, GLA_HEADS)
                rows = pl.ds(chunk[d] * c, c)
                dq_ref, dk_ref, dv_ref, dg_ref = outs[4 * d:4 * d + 4]
                dq_ref[rows, _gla_k(h)] = dq
                dk_ref[rows, _gla_k(h)] = dk
                dv_ref[rows, _gla_v(h)] = dv
                dg_ref[rows, _gla_k(h)] = dg
                dstates[d][h] = ds
        for d in range(2):
            for h in range(GLA_HEADS):
                dstate[d, h] = dstates[d][h]

    specs, outs, shapes = [], [], []
    for d in range(2):
        ci = (lambda i: nstep - 1 - i) if d == 0 else (lambda i: i)
        specs += [pl.BlockSpec((per * c, GLA_KEY_WIDTH), lambda i, ci=ci: (ci(i), qb)),
                  pl.BlockSpec((per * c, GLA_KEY_WIDTH), lambda i, ci=ci: (ci(i), kb)),
                  pl.BlockSpec((per * c, GLA_WIDTH), lambda i, ci=ci: (ci(i), vb)),
                  pl.BlockSpec((per * c, GLA_KEY_WIDTH), lambda i, ci=ci, d=d: (ci(i), d)),
                  pl.BlockSpec((per, GLA_HEADS, GLA_DV, GLA_DK), lambda i, ci=ci: (ci(i), 0, 0, 0)),
                  pl.BlockSpec((per * c, GLA_WIDTH), lambda i, ci=ci: (ci(i), 0))]
        key = pl.BlockSpec((per * c, GLA_KEY_WIDTH), lambda i, ci=ci: (ci(i), 0))
        val = pl.BlockSpec((per * c, GLA_WIDTH), lambda i, ci=ci: (ci(i), 0))
        outs += [key, key, val, key]
        shapes += [jax.ShapeDtypeStruct((t, GLA_KEY_WIDTH), F32), jax.ShapeDtypeStruct((t, GLA_KEY_WIDTH), F32),
                   jax.ShapeDtypeStruct((t, GLA_WIDTH), F32), jax.ShapeDtypeStruct((t, GLA_KEY_WIDTH), F32)]
    return pl.pallas_call(
        body, name="gla_bwd", grid=(nstep,),
        in_specs=specs, out_specs=outs, out_shape=shapes,
        scratch_shapes=[pltpu.VMEM((2, GLA_HEADS, GLA_DV, GLA_DK), F32)],
        compiler_params=_params(("arbitrary",), VMEM_LIMIT),
    )(proj, proj, proj, g, s_f, do, proj, proj, proj, g, s_b, do)


def _gla_out_head(o_f, o_b, gate, gn):
    return _rms(o_f + o_b, gn) * _silu(gate)


def _gla_out_fwd(o_f, o_b, proj, gn, *, tr=512):
    t = o_f.shape[0]
    gb = _seg_block("gate", GLA_WIDTH)

    def body(of_ref, ob_ref, gate_ref, gn_ref, out_ref):
        for h in range(GLA_HEADS):
            vc = slice(h * GLA_DV, (h + 1) * GLA_DV)
            out_ref[:, vc] = _gla_out_head(of_ref[:, vc], ob_ref[:, vc], gate_ref[:, vc], gn_ref[...]).astype(BF16)

    wide = pl.BlockSpec((tr, GLA_WIDTH), lambda i: (i, 0))
    return pl.pallas_call(
        body, name="gla_out_fwd", grid=(t // tr,),
        in_specs=[wide, wide, pl.BlockSpec((tr, GLA_WIDTH), lambda i: (i, gb)),
                  pl.BlockSpec((1, GLA_DV), lambda i: (0, 0))],
        out_specs=wide,
        out_shape=jax.ShapeDtypeStruct((t, GLA_WIDTH), BF16),
        compiler_params=_params(("parallel",), VMEM_LIMIT),
    )(o_f, o_b, proj, gn)


def _gla_out_bwd(o_f, o_b, proj, gn, dmix, *, tr=512):
    t = o_f.shape[0]
    gb = _seg_block("gate", GLA_WIDTH)

    def body(of_ref, ob_ref, gate_ref, gn_ref, dout_ref, do_ref, dgate_ref, dgn_ref):
        dgn = jnp.zeros((1, GLA_DV), F32)
        for h in range(GLA_HEADS):
            vc = slice(h * GLA_DV, (h + 1) * GLA_DV)
            _, vjp = jax.vjp(_gla_out_head, of_ref[:, vc], ob_ref[:, vc], gate_ref[:, vc], gn_ref[...])
            do, _, dgate, dg = vjp(dout_ref[:, vc])
            do_ref[:, vc] = do
            dgate_ref[:, vc] = dgate.astype(BF16)
            dgn = dgn + dg

        @pl.when(pl.program_id(0) == 0)
        def _():
            dgn_ref[...] = jnp.zeros_like(dgn_ref)

        dgn_ref[...] += dgn

    wide = pl.BlockSpec((tr, GLA_WIDTH), lambda i: (i, 0))
    vec = pl.BlockSpec((1, GLA_DV), lambda i: (0, 0))
    return pl.pallas_call(
        body, name="gla_out_bwd", grid=(t // tr,),
        in_specs=[wide, wide, pl.BlockSpec((tr, GLA_WIDTH), lambda i: (i, gb)), vec,
                  pl.BlockSpec((tr, GLA_WIDTH), lambda i: (i, 1))],
        out_specs=[wide, wide, vec],
        out_shape=[jax.ShapeDtypeStruct((t, GLA_WIDTH), F32), jax.ShapeDtypeStruct((t, GLA_WIDTH), BF16),
                   jax.ShapeDtypeStruct((1, GLA_DV), F32)],
        compiler_params=_params(("arbitrary",), VMEM_LIMIT),
    )(o_f, o_b, proj, gn, dmix)


CONV_TR = 1024
CONV_TC = 512
HALO = 8
HALO16 = 16


def _conv3(u, w, b):
    n = u.shape[0]
    return pltpu.roll(u, 1, 0) * w[0:1] + u * w[1:2] + pltpu.roll(u, n - 1, 0) * w[2:3] + b


def _conv_ext(main_ref, prev_ref, next_ref, r, nr):
    prev = prev_ref[...].astype(F32)[-HALO:] * (r > 0).astype(F32)
    nxt = next_ref[...].astype(F32)[:HALO] * (r < nr - 1).astype(F32)
    return jnp.concatenate([prev, main_ref[...].astype(F32), nxt], axis=0)


def _conv_specs(t, halo, half=None):
    per = CONV_TR // halo
    last = t // halo - 1
    lead = () if half is None else (None,)
    at = (lambda *ix: ix) if half is None else (lambda *ix: (half,) + ix)
    return [pl.BlockSpec(lead + (CONV_TR, CONV_TC), lambda j, r: at(r, j)),
            pl.BlockSpec(lead + (halo, CONV_TC), lambda j, r: at(jnp.maximum(r * per - 1, 0), j)),
            pl.BlockSpec(lead + (halo, CONV_TC), lambda j, r: at(jnp.minimum((r + 1) * per, last), j))]


def _ffn_mid_fwd(u, cw_g, cw_v, cb_g, cb_v):
    _, t, f = u.shape
    nr = t // CONV_TR

    def body(ug, ugp, ugn, uv, uvp, uvn, wg, wv, bg, bv, a_ref):
        r = pl.program_id(1)
        gate = _conv3(_conv_ext(ug, ugp, ugn, r, nr), wg[...], bg[...])[HALO:HALO + CONV_TR]
        val = _conv3(_conv_ext(uv, uvp, uvn, r, nr), wv[...], bv[...])[HALO:HALO + CONV_TR]
        a_ref[...] = (_silu(gate) * val).astype(BF16)

    w_spec = pl.BlockSpec((3, CONV_TC), lambda j, r: (0, j))
    b_spec = pl.BlockSpec((1, CONV_TC), lambda j, r: (0, j))
    return pl.pallas_call(
        body, name="ffn_mid_fwd", grid=(f // CONV_TC, nr),
        in_specs=_conv_specs(t, HALO, 0) + _conv_specs(t, HALO, 1) + [w_spec, w_spec, b_spec, b_spec],
        out_specs=pl.BlockSpec((CONV_TR, CONV_TC), lambda j, r: (r, j)),
        out_shape=jax.ShapeDtypeStruct((t, f), BF16),
        compiler_params=_params(("parallel", "parallel"), VMEM_LIMIT),
    )(u, u, u, u, u, u, cw_g, cw_v, cb_g, cb_v)


def _ffn_mid_bwd(u, cw_g, cw_v, cb_g, cb_v, da):
    _, t, f = u.shape
    nr = t // CONV_TR
    ext = CONV_TR + 2 * HALO

    def body(ug, ugp, ugn, uv, uvp, uvn, dam, dap, dan, wg, wv, bg, bv,
             du_ref, dwg_ref, dwv_ref, dbg_ref, dbv_ref):
        r = pl.program_id(1)
        shifted = []
        for main, prev, nxt in ((ug, ugp, ugn), (uv, uvp, uvn)):
            x = _conv_ext(main, prev, nxt, r, nr)
            shifted.append((pltpu.roll(x, 1, 0), x, pltpu.roll(x, ext - 1, 0)))
        da_x = _conv_ext(dam, dap, dan, r, nr)
        wg_t, wv_t = wg[...], wv[...]
        gate = shifted[0][0] * wg_t[0:1] + shifted[0][1] * wg_t[1:2] + shifted[0][2] * wg_t[2:3] + bg[...]
        val = shifted[1][0] * wv_t[0:1] + shifted[1][1] * wv_t[1:2] + shifted[1][2] * wv_t[2:3] + bv[...]
        sig = jax.nn.sigmoid(gate)
        silu = gate * sig
        d_val = da_x * silu
        d_gate = da_x * val * (sig + silu * (1.0 - sig))
        own = slice(HALO, HALO + CONV_TR)
        for half, (xs3, d, wt, dw_ref, db_ref) in enumerate(((shifted[0], d_gate, wg_t, dwg_ref, dbg_ref),
                                                            (shifted[1], d_val, wv_t, dwv_ref, dbv_ref))):
            du = pltpu.roll(d, ext - 1, 0) * wt[0:1] + d * wt[1:2] + pltpu.roll(d, 1, 0) * wt[2:3]
            du_ref[half] = du[own].astype(BF16)
            d_own = d[own]
            dw = jnp.concatenate([jnp.sum(x[own] * d_own, axis=0, keepdims=True) for x in xs3], axis=0)
            db = jnp.sum(d_own, axis=0, keepdims=True)

            @pl.when(r == 0)
            def _():
                dw_ref[...] = jnp.zeros_like(dw_ref)
                db_ref[...] = jnp.zeros_like(db_ref)

            dw_ref[...] += dw
            db_ref[...] += db

    w_spec = pl.BlockSpec((3, CONV_TC), lambda j, r: (0, j))
    b_spec = pl.BlockSpec((1, CONV_TC), lambda j, r: (0, j))
    return pl.pallas_call(
        body, name="ffn_mid_bwd", grid=(f // CONV_TC, nr),
        in_specs=(_conv_specs(t, HALO, 0) + _conv_specs(t, HALO, 1) + _conv_specs(t, HALO16)
                  + [w_spec, w_spec, b_spec, b_spec]),
        out_specs=[pl.BlockSpec((2, CONV_TR, CONV_TC), lambda j, r: (0, r, j)), w_spec, w_spec, b_spec, b_spec],
        out_shape=[jax.ShapeDtypeStruct((2, t, f), BF16),
                   jax.ShapeDtypeStruct((3, f), F32), jax.ShapeDtypeStruct((3, f), F32),
                   jax.ShapeDtypeStruct((1, f), F32), jax.ShapeDtypeStruct((1, f), F32)],
        compiler_params=_params(("parallel", "arbitrary"), VMEM_LIMIT),
    )(u, u, u, u, u, u, da, da, da, cw_g, cw_v, cb_g, cb_v)


def _down_proj_loss(act, w_down, x1, target, *, tm, tn):
    t, f = act.shape
    d = w_down.shape[1]

    def body(a_ref, w_ref, x_ref, t_ref, loss_ref, dy_ref, dyb_ref):
        y = lax.dot_general(a_ref[...], w_ref[...], _DIMS["nn"], preferred_element_type=F32) + x_ref[...]
        err = y - t_ref[...]
        dy = err * (1.0 / d)
        dy_ref[...] = dy
        dyb_ref[...] = dy.astype(BF16)
        part = 0.5 * jnp.sum(jnp.sum(err * err, axis=-1, keepdims=True) * (1.0 / d), axis=0, keepdims=True)

        @pl.when((pl.program_id(0) == 0) & (pl.program_id(1) == 0))
        def _():
            loss_ref[...] = jnp.zeros_like(loss_ref)

        loss_ref[...] += jnp.broadcast_to(part, loss_ref.shape)

    tile = pl.BlockSpec((tm, tn), lambda i, j: (i, j))
    return pl.pallas_call(
        body, name="down_proj_loss", grid=(t // tm, d // tn),
        in_specs=[pl.BlockSpec((tm, f), lambda i, j: (i, 0)), pl.BlockSpec((f, tn), lambda i, j: (0, j)), tile, tile],
        out_specs=[pl.BlockSpec((1, 128), lambda i, j: (0, 0)), tile, tile],
        out_shape=[jax.ShapeDtypeStruct((1, 128), F32), jax.ShapeDtypeStruct((t, d), F32),
                   jax.ShapeDtypeStruct((t, d), BF16)],
        compiler_params=_params(("arbitrary", "arbitrary"), VMEM_LIMIT),
    )(act, w_down, x1, target)


ANY = pl.BlockSpec(memory_space=pl.ANY)


def _position():
    return lax.axis_index("x"), lax.axis_index("y"), lax.axis_index("c")


def _other_chips(x, y):
    return [(1 - x, y), (x, 1 - y), (1 - x, 1 - y)]


def _handshake(peers):
    barrier = pltpu.get_barrier_semaphore()
    for peer in peers:
        pl.semaphore_signal(barrier, inc=1, device_id=peer, device_id_type=MESH)
    pl.semaphore_wait(barrier, len(peers))


def _exchange(body, operands, out_shapes, sems, *, name, collective_id):
    n_in, n_out = len(operands), len(out_shapes)

    def run(*refs):
        body(refs[:n_in], refs[n_in:n_in + n_out], *refs[n_in + n_out:])

    if collective_id is None:
        return pl.pallas_call(run, name=name, in_specs=[ANY] * n_in, out_specs=[ANY] * n_out,
                              out_shape=out_shapes, scratch_shapes=sems)(*operands)
    return pl.kernel(run, name=name, out_type=out_shapes,
                     mesh=plsc.ScalarSubcoreMesh(axis_name="sequencer", num_cores=1), scratch_types=sems,
                     compiler_params=pltpu.CompilerParams(collective_id=collective_id))(*operands)


def _all_gather(blocks, *, name, collective_id=None):
    na = len(blocks)

    def body(ins, outs, send_sems, recv_sems, local_sems):
        x, y, c = _position()
        me, sibling = (x, y, c), (x, y, 1 - c)
        along_x, along_y, diagonal = (1 - x, y, c), (x, 1 - y, c), (1 - x, 1 - y, c)
        first_c = c == 0
        relay_from = (jnp.where(first_c, x, 1 - x), jnp.where(first_c, 1 - y, y), c)
        relay_to = (jnp.where(first_c, 1 - x, x), jnp.where(first_c, y, 1 - y), c)
        if collective_id is not None:
            _handshake([sibling, along_x, along_y])

        def index(px, py, pc):
            return 4 * px + 2 * py + pc

        def copy(a, k, block, to, src=None):
            dst = outs[a].at[index(*block)]
            return pltpu.make_async_remote_copy(
                src_ref=dst if src is None else src, dst_ref=dst,
                send_sem=send_sems.at[a, k], recv_sem=recv_sems.at[a, k],
                device_id=to, device_id_type=MESH)

        pending = []
        for a in range(na):
            mine = pltpu.make_async_copy(ins[a], outs[a].at[index(*me)], local_sems.at[a])
            mine.start()
            pending.append(mine)
        sent = []
        for a in range(na):
            sent += [copy(a, 0, me, sibling, src=ins[a]), copy(a, 1, me, along_x, src=ins[a]),
                     copy(a, 2, me, along_y, src=ins[a])]
        for cp in sent:
            cp.start()

        def passes_on(k_in, owner, k_out):
            for a in range(na):
                copy(a, k_in, owner, me).wait_recv()
                cp = copy(a, k_out, owner, sibling)
                cp.start()
                sent.append(cp)

        passes_on(1, along_x, 4)
        passes_on(2, along_y, 5)
        for a in range(na):
            cp = copy(a, 3, relay_from, relay_to)
            cp.start()
            sent.append(cp)
        passes_on(3, diagonal, 6)
        for a in range(na):
            copy(a, 0, sibling, me).wait_recv()
            for k, owner in ((4, along_x), (5, along_y), (6, diagonal)):
                copy(a, k, (owner[0], owner[1], 1 - c), me).wait_recv()
        for cp in sent:
            cp.wait_send()
        for cp in pending:
            cp.wait()

    return _exchange(
        body, blocks, [jax.ShapeDtypeStruct((N_DEV,) + b.shape, b.dtype) for b in blocks],
        [pltpu.SemaphoreType.DMA((na, 7)), pltpu.SemaphoreType.DMA((na, 7)), pltpu.SemaphoreType.DMA((na,))],
        name=name, collective_id=collective_id)


def _grad_exchange(grads, parts, *, name, collective_id):
    ng, npart = len(grads), len(parts)

    def body(ins, outs, core_send, core_recv, chip_send, chip_recv, local_sems):
        x, y, c = _position()
        sibling = (x, y, 1 - c)
        chips = _other_chips(x, y)
        _handshake([sibling] + [(px, py, c) for px, py in chips])
        me = 2 * x + y
        copies = []
        for b in range(npart):
            src, dst = ins[ng + b], outs[ng + b]
            own = pltpu.make_async_copy(src.at[me], dst.at[me], local_sems.at[b])
            own.start()
            copies.append(own)
            for j, (px, py) in enumerate(chips):
                cp = pltpu.make_async_remote_copy(
                    src_ref=src.at[2 * px + py], dst_ref=dst.at[me],
                    send_sem=chip_send.at[b, j], recv_sem=chip_recv.at[b, j],
                    device_id=(px, py, c), device_id_type=MESH)
                cp.start()
                copies.append(cp)
        for a in range(ng):
            for k in range(N_CHIP):
                src = ins[a].at[k, 1 - c] if len(grads[a].shape) == 4 else ins[a].at[k]
                cp = pltpu.make_async_remote_copy(
                    src_ref=src, dst_ref=outs[a].at[k],
                    send_sem=core_send.at[a, k], recv_sem=core_recv.at[a, k],
                    device_id=sibling, device_id_type=MESH)
                cp.start()
                copies.append(cp)
        for cp in copies:
            cp.wait()

    shapes = ([jax.ShapeDtypeStruct((N_CHIP,) + g.shape[-2:], g.dtype) for g in grads]
              + [jax.ShapeDtypeStruct(p.shape, p.dtype) for p in parts])
    sems = [pltpu.SemaphoreType.DMA((max(ng, 1), N_CHIP)), pltpu.SemaphoreType.DMA((max(ng, 1), N_CHIP)),
            pltpu.SemaphoreType.DMA((max(npart, 1), 3)), pltpu.SemaphoreType.DMA((max(npart, 1), 3)),
            pltpu.SemaphoreType.DMA((max(npart, 1),))]
    out = _exchange(body, list(grads) + list(parts), shapes, sems, name=name, collective_id=collective_id)
    return out[:ng], out[ng:]


def _pair_sum(grad, theirs, core, *, tile, name, narrow=False):
    _, _, r, w = grad.shape
    tr, tw = tile
    assert r % tr == 0 and w % tw == 0

    def body(core_ref, mine_ref, theirs_ref, out_ref, *narrow_ref):
        total = mine_ref[...] + theirs_ref[...]
        out_ref[...] = total
        if narrow:
            narrow_ref[0][...] = total.astype(BF16)

    spec = pl.BlockSpec((None, tr, tw), lambda k, i, j, core_ref: (k, i, j))
    shapes = [jax.ShapeDtypeStruct((N_CHIP, r, w), F32)] + [jax.ShapeDtypeStruct((N_CHIP, r, w), BF16)] * narrow
    out = pl.pallas_call(
        body, name=name,
        grid_spec=pltpu.PrefetchScalarGridSpec(
            num_scalar_prefetch=1, grid=(N_CHIP, r // tr, w // tw),
            in_specs=[pl.BlockSpec((None, None, tr, tw), lambda k, i, j, core_ref: (k, core_ref[0], i, j)), spec],
            out_specs=[spec] * len(shapes)),
        out_shape=shapes,
        compiler_params=_params(("parallel", "parallel", "parallel"), VMEM_LIMIT),
    )(core, grad, theirs)
    return tuple(out) if narrow else out[0]


class _ReduceScatter:
    def __init__(self, core):
        self.core = core
        self.pending = None
        self.results = {}
        self.launches = 0

    def push(self, tag, grads, rows, then, narrow=False, finish=None):
        pair, kept, prev_tag = [], [], None
        if self.pending is not None:
            prev_tag, prev, theirs, prev_rows, prev_narrow, prev_finish = self.pending
            if prev_finish is not None:
                pair = prev_finish(theirs)
            else:
                pair = [_pair_sum(g, s, self.core, tile=tile, name=f"pair_sum_{prev_tag}_{i}", narrow=prev_narrow)
                        for i, (g, s, tile) in enumerate(zip(prev, theirs, prev_rows))]
            if prev_narrow:
                kept, pair = [p[0] for p in pair], [p[1] for p in pair]
        grads, pair, then = lax.optimization_barrier((list(grads), pair, then))
        if finish is None:
            grads = [g.reshape((N_CHIP, 2) + g.shape[1:]) for g in grads]
        self.launches += 1
        theirs, parts = _grad_exchange(grads, pair, name=f"grad_exchange_{self.launches}",
                                       collective_id=1 + self.launches)
        if prev_tag is not None:
            self.results[prev_tag] = (parts, kept)
        self.pending = (tag, grads, theirs, rows, narrow, finish) if tag is not None else None
        return then

    def result(self, tag):
        return self.results[tag]


def _adamw(parts, w, m, v, *, tile, name, own=None, chip=None):
    n, r, cols = parts.shape
    tr, tw = tile
    assert r % tr == 0 and cols % tw == 0 and w.shape == (r, cols)
    c1 = 1.0 - ADAM_B1 ** ADAM_STEP
    c2 = 1.0 - ADAM_B2 ** ADAM_STEP

    def update(g, w_ref, m_ref, v_ref, g_ref, d_ref, nm_ref, nv_ref):
        new_m = ADAM_B1 * m_ref[...] + (1.0 - ADAM_B1) * g
        new_v = ADAM_B2 * v_ref[...] + (1.0 - ADAM_B2) * (g * g)
        m_hat = new_m / c1
        v_hat = new_v / c2
        g_ref[...] = g
        d_ref[...] = -ADAM_LR * (m_hat / (jnp.sqrt(v_hat) + ADAM_EPS) + ADAM_WD * w_ref[...])
        nm_ref[...] = new_m
        nv_ref[...] = new_v

    shape = jax.ShapeDtypeStruct((r, cols), F32)
    if own is None:
        def body(p_ref, *refs):
            g = p_ref[0]
            for k in range(1, n):
                g = g + p_ref[k]
            update(g, *refs)

        spec = pl.BlockSpec((tr, tw), lambda i, j: (i, j))
        return pl.pallas_call(
            body, name=name, grid=(r // tr, cols // tw),
            in_specs=[pl.BlockSpec((n, tr, tw), lambda i, j: (0, i, j)), spec, spec, spec],
            out_specs=[spec] * 4, out_shape=[shape] * 4,
            compiler_params=_params(("parallel", "parallel"), VMEM_LIMIT),
        )(parts, w, m, v)

    def body(chip_ref, p_ref, own_ref, *refs):
        g = None
        for k in range(n):
            term = jnp.where(chip_ref[0] == k, own_ref[...], p_ref[k].astype(F32))
            g = term if g is None else g + term
        update(g, *refs)

    spec = pl.BlockSpec((tr, tw), lambda i, j, chip_ref: (i, j))
    return pl.pallas_call(
        body, name=name,
        grid_spec=pltpu.PrefetchScalarGridSpec(
            num_scalar_prefetch=1, grid=(r // tr, cols // tw),
            in_specs=[pl.BlockSpec((n, tr, tw), lambda i, j, chip_ref: (0, i, j)),
                      pl.BlockSpec((None, tr, tw), lambda i, j, chip_ref: (chip_ref[0], i, j)), spec, spec, spec],
            out_specs=[spec] * 4),
        out_shape=[shape] * 4,
        compiler_params=_params(("parallel", "parallel"), VMEM_LIMIT),
    )(chip, parts, own, w, m, v)


LANES = 128


def _row_offsets(pieces):
    offsets, row = [], 0
    for p in pieces:
        assert p.shape[0] == 1 and p.shape[1] % LANES == 0, p.shape
        offsets.append(row)
        row += p.shape[1] // LANES
    return offsets, row


def _pack_rows(pieces):
    offsets, rows = _row_offsets(pieces)

    def body(*refs):
        out_ref = refs[-1]
        for ref, start in zip(refs[:-1], offsets):
            for j in range(ref.shape[1] // LANES):
                out_ref[start + j:start + j + 1, :] = ref[:, j * LANES:(j + 1) * LANES]

    return pl.pallas_call(body, name="pack_small_grads",
                          out_shape=jax.ShapeDtypeStruct((rows, LANES), F32))(*pieces)


def _adamw_rows(terms, ws, ms, vs):
    n_dev, rows, _ = terms.shape
    offsets, used = _row_offsets(ws)
    assert used + 1 == rows
    c1 = 1.0 - ADAM_B1 ** ADAM_STEP
    c2 = 1.0 - ADAM_B2 ** ADAM_STEP
    nw = len(ws)

    def body(*refs):
        t_ref = refs[0]
        w_refs, m_refs, v_refs = refs[1:1 + nw], refs[1 + nw:1 + 2 * nw], refs[1 + 2 * nw:1 + 3 * nw]
        outs = refs[1 + 3 * nw:]
        total = t_ref[0]
        for k in range(1, n_dev):
            total = total + t_ref[k]
        for i, start in enumerate(offsets):
            for j in range(ws[i].shape[1] // LANES):
                lanes = slice(j * LANES, (j + 1) * LANES)
                g = total[start + j:start + j + 1, :]
                new_m = ADAM_B1 * m_refs[i][:, lanes] + (1.0 - ADAM_B1) * g
                new_v = ADAM_B2 * v_refs[i][:, lanes] + (1.0 - ADAM_B2) * (g * g)
                delta = -ADAM_LR * ((new_m / c1) / (jnp.sqrt(new_v / c2) + ADAM_EPS) + ADAM_WD * w_refs[i][:, lanes])
                for kind, value in enumerate((g, delta, new_m, new_v)):
                    outs[kind * nw + i][:, lanes] = value
        outs[-1][...] = total[used:used + 1, :]

    shapes = [jax.ShapeDtypeStruct(w.shape, F32) for w in ws] * 4 + [jax.ShapeDtypeStruct((1, LANES), F32)]
    out = pl.pallas_call(body, name="adamw_replicated", out_shape=shapes,
                         compiler_params=_params(None, VMEM_LIMIT))(terms, *ws, *ms, *vs)
    return [list(out[kind * nw:(kind + 1) * nw]) for kind in range(4)], out[-1]


def _rope_tables(t):
    half = HEAD_DIM // 2
    inv = 1.0 / (ROPE_THETA ** (jnp.arange(half, dtype=F32) / half))
    ang = jnp.arange(t, dtype=jnp.int32).astype(F32)[:, None] * inv[None, :]
    cos, sin = jnp.cos(ang), jnp.sin(ang)
    return jnp.concatenate([cos, cos], axis=1), jnp.concatenate([-sin, sin], axis=1)


IN_KERNEL = IN_MAIN + LR_PAD


def _to_kernel_rows(w_t):
    order = sorted(SEGMENTS.values())
    pad = jnp.zeros((LR_PAD - (IN_TOTAL - IN_MAIN), w_t.shape[1]), w_t.dtype)
    return jnp.concatenate([w_t[src:src + width] for _, src, width in order] + [w_t[IN_MAIN:IN_TOTAL], pad], axis=0)


CONV_TAPS = 3
WA_BLOCK = GLA_KEY_WIDTH // N_DEV
SMALL_SIZES = (CONV_TAPS * UP_BLOCK, GLA_RANK * WA_BLOCK, GLA_RANK * WA_BLOCK)
SMALL_SHAPES = ((CONV_TAPS, UP_BLOCK), (GLA_RANK, WA_BLOCK), (GLA_RANK, WA_BLOCK))
SMALL_ROWS = sum(SMALL_SIZES) // LANES


def _small_block(conv, wa_f, wa_b):
    return jnp.concatenate([conv.reshape(-1), wa_f.reshape(-1), wa_b.reshape(-1)]).reshape(SMALL_ROWS, LANES)


def _small_unblock(block):
    flat, out, off = block.reshape(-1), [], 0
    for size, shape in zip(SMALL_SIZES, SMALL_SHAPES):
        out.append(flat[off:off + size].reshape(shape))
        off += size
    return out


def _small_blocks(conv_full, wa_f_full, wa_b_full):
    def by_device(a, width):
        return jnp.transpose(a.reshape(a.shape[0], N_DEV, width), (1, 0, 2)).reshape(N_DEV, -1)
    return jnp.concatenate([by_device(conv_full, UP_BLOCK), by_device(wa_f_full, WA_BLOCK),
                            by_device(wa_b_full, WA_BLOCK)], axis=1).reshape(N_DEV, SMALL_ROWS, LANES)


def _small_unblocks(blocks):
    flat, out, off = blocks.reshape(N_DEV, -1), [], 0
    for size, (rows, width) in zip(SMALL_SIZES, SMALL_SHAPES):
        part = flat[:, off:off + size].reshape(N_DEV, rows, width)
        out.append(jnp.transpose(part, (1, 0, 2)).reshape(rows, N_DEV * width))
        off += size
    return out


def _local_step(xs, target, norm1_g, w_in8, gq, gk, attn_sink, w2, ba2, gla_norm_g, w_out_full, norm2_g,
                w_up8, cw_g, cw_v, cb_g, cb_v, w_down_full, rs=None):
    t = xs.shape[0]
    tm = min(1024, t)
    tall = min(2048, t)
    cos, sin_signed = _rope_tables(t)
    sink = attn_sink.reshape(ATTN_HEADS)

    h1 = _rmsnorm_fwd(xs, norm1_g, name="norm1_fwd")
    w_in8, h1, cos, sin_signed = lax.optimization_barrier((w_in8, h1, cos, sin_signed))
    w_in_k = _to_kernel_rows(w_in8.reshape(IN_TOTAL, D_MODEL))
    proj = _matmul(h1, w_in_k, "nt", tm=tall, tn=IN_MAIN // 4, tk=D_MODEL, n_out=IN_MAIN, name="proj_main")
    proj_lr = _matmul(h1, w_in_k[IN_MAIN:], "nt", tm=tm, tn=LR_PAD, tk=D_MODEL, name="proj_lr")
    qa, ka, va = _attn_prep_fwd(proj, cos, sin_signed, gq, gk)
    o_attn = _attn_fwd(qa, ka, va, sink)
    g_dec = _gla_prep_fwd(proj_lr, w2, ba2)
    o_f, o_b, s_f, s_b = _gla_fwd(proj, g_dec)
    o_gla = _gla_out_fwd(o_f, o_b, proj, gla_norm_g)
    x1 = _out_proj(o_attn, o_gla, w_out_full, xs, tm=tall, tn=512)
    h2 = _rmsnorm_fwd(x1, norm2_g, name="norm2_fwd")
    u = _up_proj(h2, w_up8, tm=tm)
    act = _ffn_mid_fwd(u, cw_g, cw_v, cb_g, cb_v)
    loss_part, dy, dy_b = _down_proj_loss(act, w_down_full, x1, target, tm=tm, tn=512)

    d_act = _matmul(dy_b, w_down_full, "nt", tm=tall, tn=D_FF // 4, tk=D_MODEL, out_dtype=BF16, name="d_act")
    dw_down = _matmul(act, dy_b, "tn", tm=D_FF // 4, tn=512, tk=t, name="dw_down")
    if rs is not None:
        d_act = rs.push("w_down", [dw_down.reshape(N_DEV, D_FF // N_DEV, D_MODEL)], [(64, D_MODEL)], d_act)
    du, dcw_g, dcw_v, dcb_g, dcb_v = _ffn_mid_bwd(u, cw_g, cw_v, cb_g, cb_v, d_act)
    if rs is None:
        dw_up8 = _up_proj_dw(h2, du, tm=512, tk=t)
    else:
        dw_up8 = None
        theirs_first = _up_proj_dw_core(h2, du, 1 - rs.core, tm=512, tk=t, name="up_proj_dw_sibling")
        du = rs.push("w_up", [theirs_first], None, du, finish=lambda got, h2=h2, du=du: [
            _up_proj_dw_core(h2, du, rs.core, tm=512, tk=t, res=got[0], name="up_proj_dw_own")])
    dh2 = _up_proj_dx(du, w_up8, tm=tm, tn=1024)
    dx1, dx1_b, d_norm2 = _rmsnorm_bwd(x1, norm2_g, dh2, dy, name="norm2_bwd", also_bf16=True)
    dmix = _matmul(dx1_b, w_out_full, "nt", tm=tall, tn=1024, tk=D_MODEL, name="d_mix")
    dw_out = _out_proj_dw(o_attn, o_gla, dx1_b, tn=512)
    if rs is not None:
        dmix = rs.push("w_out", [dw_out.reshape(N_DEV, D_MODEL // N_DEV, D_MODEL)], [(256, D_MODEL)], dmix)
    do_gla, d_gate, d_gla_norm = _gla_out_bwd(o_f, o_b, proj, gla_norm_g, dmix)
    (dq_f, dk_f, dv_f, dg_f, dq_b, dk_b, dv_b, dg_b) = _gla_bwd(proj, g_dec, s_f, s_b, do_gla)
    d_lr, d_w2, d_ba2 = _gla_prep_bwd(proj_lr, w2, ba2, dg_f, dg_b)
    dqa, dk_lo, dk_mid, dk_hi, dv_lo, dv_mid, dv_hi, d_sink8 = _attn_bwd(qa, ka, va, sink, dmix)
    d_qa, d_ka, d_va, d_qn, d_kn = _attn_prep_bwd(proj, cos, sin_signed, gq, gk, dqa,
                                                  (dk_lo, dk_mid, dk_hi), (dv_lo, dv_mid, dv_hi))
    d_seg = {"qa": d_qa, "gate": d_gate, "vg": (dv_f + dv_b).astype(BF16), "qg": (dq_f + dq_b).astype(BF16),
             "kg": (dk_f + dk_b).astype(BF16), "ka": d_ka, "va": d_va}
    d_proj = jnp.concatenate([d_seg[k] for k in sorted(SEGMENTS, key=lambda k: SEGMENTS[k][0])] + [d_lr], axis=1)
    dw_in_t = _in_proj_dw_lr(_in_proj_dw(d_proj, h1, tn=1024), d_lr, h1)
    if rs is not None:
        per_in = IN_TOTAL // N_DEV
        small_grad = _small_blocks(jnp.concatenate([dcw_g, dcw_v], axis=1), d_w2[:GLA_RANK, :GLA_KEY_WIDTH],
                                   d_w2[GLA_RANK:2 * GLA_RANK, GLA_KEY_WIDTH:])
        d_proj, d_lr = rs.push("w_in", [dw_in_t.reshape(N_DEV, per_in, D_MODEL), small_grad],
                               [(per_in, 512), small_grad.shape[1:]], (d_proj, d_lr), narrow=True)
    dh1 = _matmul(d_proj, w_in_k, "nn", tm=tm, tn=512, tk=IN_KERNEL, name="dh1")
    if rs is not None:
        dh1 = rs.push(None, [], [], dh1)
    grad_x, d_norm1 = _rmsnorm_bwd(xs, norm1_g, dh1, dx1, name="norm1_bwd", also_bf16=False)
    return (loss_part, grad_x, dw_in_t, dw_out, dw_up8, dw_down, dcw_g, dcw_v, dcb_g, dcb_v,
            d_w2, d_ba2, d_norm1, d_norm2, d_qn, d_kn, d_sink8, d_gla_norm)


def kernel(x, norm1_g, w_in, attn_q_norm_g, attn_k_norm_g, attn_sink, gla_wa2_fwd, gla_ba_fwd, gla_wa2_bwd, gla_ba_bwd, gla_out_norm_g, w_out, norm2_g, w_up, conv_w, conv_b, w_down, loss_target, m_norm1_g, m_w_in, m_attn_q_norm_g, m_attn_k_norm_g, m_attn_sink, m_gla_wa2_fwd, m_gla_ba_fwd, m_gla_wa2_bwd, m_gla_ba_bwd, m_gla_out_norm_g, m_w_out, m_norm2_g, m_w_up, m_conv_w, m_conv_b, m_w_down, v_norm1_g, v_w_in, v_attn_q_norm_g, v_attn_k_norm_g, v_attn_sink, v_gla_wa2_fwd, v_gla_ba_fwd, v_gla_wa2_bwd, v_gla_ba_bwd, v_gla_out_norm_g, v_w_out, v_norm2_g, v_w_up, v_conv_w, v_conv_b, v_w_down):
    t = x.shape[1]
    xs = x.reshape(t, D_MODEL)
    target = loss_target.reshape(t, D_MODEL)
    core = lax.axis_index("c").astype(jnp.int32).reshape(1)

    w_small = _small_block(conv_w[0], gla_wa2_fwd[0], gla_wa2_bwd[0])
    w_in_t, m_in_t, v_in_t = (jnp.swapaxes(a[0], 0, 1) for a in (w_in, m_w_in, v_w_in))
    g_in, g_small = _all_gather([w_in_t.astype(BF16), w_small], name="gather_w_in", collective_id=8)
    g_out, w_up8, g_down = _all_gather([w_out[0].astype(BF16), w_up[0].astype(BF16), w_down[0].astype(BF16)],
                                       name="gather_later_weights", collective_id=1)
    w_out_full = g_out.reshape(D_MODEL, D_MODEL)
    w_down_full = g_down.reshape(D_FF, D_MODEL)
    conv_w_full, wa2_f, wa2_b = _small_unblocks(g_small)
    cw_g, cw_v = conv_w_full[:, :D_FF], conv_w_full[:, D_FF:]
    cb_g, cb_v = conv_b[:, :D_FF], conv_b[:, D_FF:]
    w2 = jnp.zeros((LR_PAD, 2 * GLA_KEY_WIDTH), F32)
    w2 = w2.at[:GLA_RANK, :GLA_KEY_WIDTH].set(wa2_f).at[GLA_RANK:2 * GLA_RANK, GLA_KEY_WIDTH:].set(wa2_b)
    ba2 = jnp.concatenate([gla_ba_fwd, gla_ba_bwd], axis=1)
    rs = _ReduceScatter(core)
    (loss_part, grad_x, _, _, _, _, _, _, dcb_g, dcb_v, _, d_ba2,
     d_norm1, d_norm2, d_qn, d_kn, d_sink8, d_gla_norm) = _local_step(
        xs, target, norm1_g, g_in, attn_q_norm_g, attn_k_norm_g, attn_sink, w2, ba2, gla_out_norm_g,
        w_out_full, norm2_g, w_up8, cw_g, cw_v, cb_g, cb_v, w_down_full, rs=rs)

    (part_down,), (part_up,), (part_out,) = rs.result("w_down")[0], rs.result("w_up")[0], rs.result("w_out")[0]
    (part_in, part_small), (own_in, own_small) = rs.result("w_in")
    chip = (2 * lax.axis_index("x") + lax.axis_index("y")).astype(jnp.int32).reshape(1)
    m_small = _small_block(m_conv_w[0], m_gla_wa2_fwd[0], m_gla_wa2_bwd[0])
    v_small = _small_block(v_conv_w[0], v_gla_wa2_fwd[0], v_gla_wa2_bwd[0])
    upd_in = _adamw(part_in, w_in_t, m_in_t, v_in_t, tile=(IN_TOTAL // N_DEV, 512), name="adamw_w_in",
                    own=own_in, chip=chip)
    upd_in = [jnp.swapaxes(u, 0, 1) for u in upd_in]
    upd_out = _adamw(part_out, w_out[0], m_w_out[0], v_w_out[0], tile=(256, D_MODEL), name="adamw_w_out")
    upd_up = _adamw(part_up, w_up[0], m_w_up[0], v_w_up[0], tile=(256, UP_BLOCK), name="adamw_w_up")
    upd_down = _adamw(part_down, w_down[0], m_w_down[0], v_w_down[0], tile=(64, D_MODEL), name="adamw_w_down")
    upd_small = _adamw(part_small, w_small, m_small, v_small, tile=part_small.shape[1:],
                       name="adamw_small", own=own_small, chip=chip)
    upd_small = [_small_unblock(u) for u in upd_small]

    rep_names = ["norm1_g", "attn_q_norm_g", "attn_k_norm_g", "attn_sink", "gla_ba_fwd", "gla_ba_bwd",
                 "gla_out_norm_g", "norm2_g", "conv_b"]
    def whole_lanes(sink_like):
        return jnp.pad(sink_like, ((0, 0), (0, LANES - ATTN_HEADS)))

    rep_w = [norm1_g, attn_q_norm_g, attn_k_norm_g, whole_lanes(attn_sink), gla_ba_fwd, gla_ba_bwd, gla_out_norm_g,
             norm2_g, conv_b]
    rep_m = [m_norm1_g, m_attn_q_norm_g, m_attn_k_norm_g, whole_lanes(m_attn_sink), m_gla_ba_fwd, m_gla_ba_bwd,
             m_gla_out_norm_g, m_norm2_g, m_conv_b]
    rep_v = [v_norm1_g, v_attn_q_norm_g, v_attn_k_norm_g, whole_lanes(v_attn_sink), v_gla_ba_fwd, v_gla_ba_bwd,
             v_gla_out_norm_g, v_norm2_g, v_conv_b]
    d_sink = whole_lanes(d_sink8[:, :GQA_GROUP, 0].reshape(1, ATTN_HEADS))
    rep_g = [d_norm1, d_qn, d_kn, d_sink, d_ba2[:, :GLA_KEY_WIDTH], d_ba2[:, GLA_KEY_WIDTH:], d_gla_norm, d_norm2,
             jnp.concatenate([dcb_g, dcb_v], axis=1)]
    (rep_terms,) = _all_gather([_pack_rows(rep_g + [loss_part])], name="gather_small_grads", collective_id=7)
    upd_rep, loss_row = _adamw_rows(rep_terms, rep_w, rep_m, rep_v)
    sink_at = rep_names.index("attn_sink")
    for kind in range(4):
        upd_rep[kind][sink_at] = upd_rep[kind][sink_at][:, :ATTN_HEADS]
    loss = loss_row[0, 0]

    order = ["norm1_g", "w_in", "attn_q_norm_g", "attn_k_norm_g", "attn_sink", "gla_wa2_fwd", "gla_ba_fwd",
             "gla_wa2_bwd", "gla_ba_bwd", "gla_out_norm_g", "w_out", "norm2_g", "w_up", "conv_w", "conv_b", "w_down"]
    outs = [loss, grad_x.reshape(1, t, D_MODEL)]
    for kind in range(4):
        by_name = {n: upd_rep[kind][i] for i, n in enumerate(rep_names)}
        by_name["w_in"] = upd_in[kind][None]
        by_name["w_out"] = upd_out[kind][None]
        by_name["w_up"] = upd_up[kind][None]
        by_name["w_down"] = upd_down[kind][None]
        by_name["conv_w"] = upd_small[kind][0][None]
        by_name["gla_wa2_fwd"] = upd_small[kind][1][None]
        by_name["gla_wa2_bwd"] = upd_small[kind][2][None]
        outs += [by_name[n] for n in order]
    return tuple(outs)
```

```python
import functools

import jax
import jax.numpy as jnp
from jax import lax
from jax.experimental import pallas as pl
from jax.experimental.pallas import tpu as pltpu
from jax.experimental.pallas import tpu_sc as plsc

F32 = jnp.float32
BF16 = jnp.bfloat16

D_MODEL = 2048
HEAD_DIM = 128
ATTN_WIDTH = 1024
ATTN_HEADS = 8
KV_HEADS = 2
GQA_GROUP = 4
KV_WIDTH = KV_HEADS * HEAD_DIM
ATTN_BLOCK = 128
WINDOW = 128
ROPE_THETA = 10000.0
GLA_HEADS = 4
GLA_DK = 128
GLA_DV = 256
GLA_KEY_WIDTH = 512
GLA_WIDTH = 1024
GLA_RANK = 16
GLA_GATE_NORMALIZER = 16.0
GLA_CHUNK = 64
GLA_PER_STEP = 4
D_FF = 5632
NORM_EPS = 1e-6
IN_TOTAL = 4640
IN_MAIN = 4608
LR_PAD = 128
N_DEV = 8
N_CHIP = 4

ADAM_LR = 0.001
ADAM_B1 = 0.9
ADAM_B2 = 0.999
ADAM_EPS = 1e-08
ADAM_WD = 0.01
ADAM_STEP = 10

SEGMENTS = {
    "qa": (0, 0, 1024),
    "gate": (1024, 3584, 1024),
    "vg": (2048, 2560, 1024),
    "qg": (3072, 1536, 512),
    "kg": (3584, 2048, 512),
    "ka": (4096, 1024, 256),
    "va": (4352, 1280, 256),
}

VMEM_LIMIT = 56 * 1024 * 1024
MESH = pl.DeviceIdType.MESH


def _params(semantics=None, vmem=None):
    return pltpu.CompilerParams(dimension_semantics=semantics, vmem_limit_bytes=vmem)


_DIMS = {
    "nn": (((1,), (0,)), ((), ())),
    "nt": (((1,), (1,)), ((), ())),
    "tn": (((0,), (0,)), ((), ())),
}


def _mxu(a, b, mode):
    return lax.dot_general(a.astype(BF16), b.astype(BF16), _DIMS[mode], preferred_element_type=F32)


@functools.partial(jax.custom_vjp, nondiff_argnums=(2,))
def bdot(a, b, mode):
    return _mxu(a, b, mode)


def _bdot_fwd(a, b, mode):
    return _mxu(a, b, mode), (a, b)


def _bdot_bwd(mode, res, g):
    a, b = res
    if mode == "nn":
        return _mxu(g, b, "nt"), _mxu(a, g, "tn")
    if mode == "nt":
        return _mxu(g, b, "nn"), _mxu(g, a, "tn")
    return _mxu(b, g, "nt"), _mxu(a, g, "nn")


bdot.defvjp(_bdot_fwd, _bdot_bwd)


def _rms(x, g):
    return x * lax.rsqrt(jnp.mean(x * x, axis=-1, keepdims=True) + NORM_EPS) * g


def _rope(x, cos, sin_signed):
    return x * cos + pltpu.roll(x, HEAD_DIM // 2, 1) * sin_signed


def _rope_transposed(d, cos, sin_signed):
    return d * cos + pltpu.roll(d * sin_signed, HEAD_DIM // 2, 1)


def _silu(x):
    return x * jax.nn.sigmoid(x)


def _log_sigmoid(z):
    return -(jnp.maximum(-z, 0.0) + jnp.log(1.0 + jnp.exp(-jnp.abs(z))))


def _matmul_call(args, in_specs, o_spec, out_shape, grid, mode, nk, acc_shape, *, name, has_res=False,
                 prefetch=None, load_b=lambda ref: ref[...]):
    dims = _DIMS[mode]
    out_dtype = out_shape.dtype
    n_pre = 0 if prefetch is None else 1

    def body(*refs):
        refs = refs[n_pre:]
        if has_res:
            a_ref, b_ref, r_ref, o_ref = refs[:4]
            rest = refs[4:]
        else:
            a_ref, b_ref, o_ref = refs[:3]
            r_ref = None
            rest = refs[3:]
        part = lax.dot_general(a_ref[...], load_b(b_ref), dims, preferred_element_type=F32)

        def finish(acc):
            if r_ref is not None:
                acc = acc + r_ref[...]
            o_ref[...] = acc.astype(out_dtype)

        if nk == 1:
            finish(part)
        else:
            acc_ref = rest[0]
            kk = pl.program_id(2)

            @pl.when(kk == 0)
            def _():
                acc_ref[...] = part

            @pl.when(kk > 0)
            def _():
                acc_ref[...] += part

            @pl.when(kk == nk - 1)
            def _():
                finish(acc_ref[...])

    scratch = [pltpu.VMEM(acc_shape, F32)] if nk > 1 else []
    params = _params(("parallel", "parallel", "arbitrary"), VMEM_LIMIT)
    if prefetch is None:
        return pl.pallas_call(body, name=name, grid=grid, in_specs=in_specs, out_specs=o_spec, out_shape=out_shape,
                              scratch_shapes=scratch, compiler_params=params)(*args)
    return pl.pallas_call(
        body, name=name,
        grid_spec=pltpu.PrefetchScalarGridSpec(num_scalar_prefetch=1, grid=grid, in_specs=in_specs,
                                               out_specs=o_spec, scratch_shapes=scratch),
        out_shape=out_shape, compiler_params=params)(prefetch, *args)


def _matmul(a, b, mode, *, tm, tn, tk, out_dtype=F32, res=None, name, n_out=None):
    if mode == "nn":
        (m, k), (k2, n) = a.shape, b.shape
    elif mode == "nt":
        (m, k), (n, k2) = a.shape, b.shape
    else:
        (k, m), (k2, n) = a.shape, b.shape
    n = n if n_out is None else n_out
    assert k == k2 and m % tm == 0 and n % tn == 0 and k % tk == 0, (name, a.shape, b.shape, tm, tn, tk)
    if mode == "tn":
        a_spec = pl.BlockSpec((tk, tm), lambda i, j, kk: (kk, i))
    else:
        a_spec = pl.BlockSpec((tm, tk), lambda i, j, kk: (i, kk))
    if mode == "nt":
        b_spec = pl.BlockSpec((tn, tk), lambda i, j, kk: (j, kk))
    else:
        b_spec = pl.BlockSpec((tk, tn), lambda i, j, kk: (kk, j))
    o_spec = pl.BlockSpec((tm, tn), lambda i, j, kk: (i, j))
    in_specs, args = [a_spec, b_spec], [a, b]
    if res is not None:
        in_specs.append(o_spec)
        args.append(res)
    return _matmul_call(args, in_specs, o_spec, jax.ShapeDtypeStruct((m, n), out_dtype),
                        (m // tm, n // tn, k // tk), mode, k // tk, (tm, tn), name=name, has_res=res is not None)


def _out_proj(o_attn, o_gla, w_out, x, *, tm, tn):
    t, ka = o_attn.shape
    kg = o_gla.shape[1]

    def body(a_ref, g_ref, w_ref, x_ref, o_ref):
        acc = lax.dot_general(a_ref[...], w_ref[:ka], _DIMS["nn"], preferred_element_type=F32)
        acc = acc + lax.dot_general(g_ref[...], w_ref[ka:], _DIMS["nn"], preferred_element_type=F32)
        o_ref[...] = acc + x_ref[...]

    tile = pl.BlockSpec((tm, tn), lambda i, j: (i, j))
    return pl.pallas_call(
        body, name="out_proj", grid=(t // tm, D_MODEL // tn),
        in_specs=[pl.BlockSpec((tm, ka), lambda i, j: (i, 0)), pl.BlockSpec((tm, kg), lambda i, j: (i, 0)),
                  pl.BlockSpec((ka + kg, tn), lambda i, j: (0, j)), tile],
        out_specs=tile, out_shape=jax.ShapeDtypeStruct((t, D_MODEL), F32),
        compiler_params=_params(("parallel", "parallel"), VMEM_LIMIT),
    )(o_attn, o_gla, w_out, x)


def _out_proj_dw(o_attn, o_gla, dx1, *, tn):
    t, ka = o_attn.shape
    assert o_gla.shape == (t, ka)

    def body(a_ref, g_ref, d_ref, o_ref):
        @pl.when(pl.program_id(0) == 0)
        def _():
            o_ref[...] = lax.dot_general(a_ref[...], d_ref[...], _DIMS["tn"], preferred_element_type=F32)

        @pl.when(pl.program_id(0) == 1)
        def _():
            o_ref[...] = lax.dot_general(g_ref[...], d_ref[...], _DIMS["tn"], preferred_element_type=F32)

    whole = pl.BlockSpec((t, ka), lambda i, j: (0, 0))
    return pl.pallas_call(
        body, name="dw_out", grid=(2, D_MODEL // tn),
        in_specs=[whole, whole, pl.BlockSpec((t, tn), lambda i, j: (0, j))],
        out_specs=pl.BlockSpec((ka, tn), lambda i, j: (i, j)),
        out_shape=jax.ShapeDtypeStruct((2 * ka, D_MODEL), F32),
        compiler_params=_params(("parallel", "parallel"), VMEM_LIMIT),
    )(o_attn, o_gla, dx1)


UP_BLOCK = 2 * D_FF // N_DEV


def _up_proj(h2, w_up8, *, tm):
    t = h2.shape[0]
    return _matmul_call(
        [h2, w_up8],
        [pl.BlockSpec((tm, D_MODEL), lambda i, j, kk: (i, 0)),
         pl.BlockSpec((None, D_MODEL, UP_BLOCK), lambda i, j, kk: (j, 0, 0))],
        pl.BlockSpec((None, tm, UP_BLOCK), lambda i, j, kk: (j // N_CHIP, i, j % N_CHIP)),
        jax.ShapeDtypeStruct((2, t, D_FF), F32), (t // tm, N_DEV, 1), "nn", 1, None, name="up_proj")


def _up_proj_dx(du, w_up8, *, tm, tn):
    t = du.shape[1]
    pair = 2
    return _matmul_call(
        [du, w_up8],
        [pl.BlockSpec((None, tm, pair * UP_BLOCK), lambda i, j, kk: (kk // 2, i, kk % 2)),
         pl.BlockSpec((pair, tn, UP_BLOCK), lambda i, j, kk: (kk, j, 0))],
        pl.BlockSpec((tm, tn), lambda i, j, kk: (i, j)),
        jax.ShapeDtypeStruct((t, D_MODEL), F32), (t // tm, D_MODEL // tn, N_DEV // pair), "nt", N_DEV // pair,
        (tm, tn), name="up_proj_dx", load_b=lambda ref: jnp.concatenate([ref[0], ref[1]], axis=1))


def _up_proj_dw(h2, du, *, tm, tk):
    t = h2.shape[0]
    return _matmul_call(
        [h2, du],
        [pl.BlockSpec((tk, tm), lambda j, i, kk: (kk, i)),
         pl.BlockSpec((None, tk, UP_BLOCK), lambda j, i, kk: (j // N_CHIP, kk, j % N_CHIP))],
        pl.BlockSpec((None, tm, UP_BLOCK), lambda j, i, kk: (j, i, 0)),
        jax.ShapeDtypeStruct((N_DEV, D_MODEL, UP_BLOCK), F32), (N_DEV, D_MODEL // tm, t // tk), "tn", t // tk,
        (tm, UP_BLOCK), name="up_proj_dw")


def _up_proj_dw_core(h2, du, core, *, tm, tk, res=None, name):
    t = h2.shape[0]

    def block(j, core_ref):
        return 2 * j + core_ref[0]

    out_spec = pl.BlockSpec((None, tm, UP_BLOCK), lambda j, i, kk, core_ref: (j, i, 0))
    return _matmul_call(
        [h2, du] + ([] if res is None else [res]),
        [pl.BlockSpec((tk, tm), lambda j, i, kk, core_ref: (kk, i)),
         pl.BlockSpec((None, tk, UP_BLOCK),
                      lambda j, i, kk, core_ref: (block(j, core_ref) // N_CHIP, kk, block(j, core_ref) % N_CHIP))]
        + ([] if res is None else [out_spec]),
        out_spec, jax.ShapeDtypeStruct((N_CHIP, D_MODEL, UP_BLOCK), F32), (N_CHIP, D_MODEL // tm, t // tk), "tn",
        t // tk, (tm, UP_BLOCK), name=name, has_res=res is not None, prefetch=core)


IN_TILE = 512


def _in_proj_dw(d_proj, h1, *, tn):
    t = h1.shape[0]
    table = []
    for tile in range(IN_MAIN // IN_TILE):
        dst, src, _ = max(s for s in SEGMENTS.values() if s[0] <= tile * IN_TILE)
        assert (src + tile * IN_TILE - dst) % IN_TILE == 0
        table.append((src + tile * IN_TILE - dst) // IN_TILE)
    assert sorted(table) == list(range(IN_MAIN // IN_TILE))
    return _matmul_call(
        [d_proj, h1],
        [pl.BlockSpec((t, IN_TILE), lambda j, i, kk, tab: (0, i)),
         pl.BlockSpec((t, tn), lambda j, i, kk, tab: (0, j))],
        pl.BlockSpec((IN_TILE, tn), lambda j, i, kk, tab: (tab[i], j)),
        jax.ShapeDtypeStruct((IN_TOTAL, D_MODEL), F32), (D_MODEL // tn, IN_MAIN // IN_TILE, 1), "tn", 1, None,
        name="in_proj_dw", prefetch=jnp.asarray(table, jnp.int32))


def _in_proj_dw_lr(dw_t, d_lr, h1):
    t = h1.shape[0]
    n_lr = IN_TOTAL - IN_MAIN
    tn = 512

    def body(dw_ref, dlr_ref, h1_ref, out_ref):
        full = lax.dot_general(dlr_ref[...], h1_ref[...], _DIMS["tn"], preferred_element_type=F32)
        out_ref[...] = full[:n_lr]

    return pl.pallas_call(
        body, name="in_proj_dw_lr", grid=(D_MODEL // tn,),
        in_specs=[pl.BlockSpec(memory_space=pl.ANY),
                  pl.BlockSpec((t, LR_PAD), lambda j: (0, 0)),
                  pl.BlockSpec((t, tn), lambda j: (0, j))],
        out_specs=pl.BlockSpec((n_lr, tn), lambda j: (IN_MAIN // n_lr, j)),
        out_shape=jax.ShapeDtypeStruct(dw_t.shape, F32),
        input_output_aliases={0: 0},
        compiler_params=_params(("parallel",), VMEM_LIMIT),
    )(dw_t, d_lr, h1)


def _rmsnorm_fwd(x, g, *, name, tr=512):
    t, d = x.shape

    def body(x_ref, g_ref, h_ref):
        h_ref[...] = _rms(x_ref[...], g_ref[...]).astype(BF16)

    return pl.pallas_call(
        body, name=name, grid=(t // tr,),
        in_specs=[pl.BlockSpec((tr, d), lambda i: (i, 0)), pl.BlockSpec((1, d), lambda i: (0, 0))],
        out_specs=pl.BlockSpec((tr, d), lambda i: (i, 0)),
        out_shape=jax.ShapeDtypeStruct((t, d), BF16),
        compiler_params=_params(("parallel",), VMEM_LIMIT),
    )(x, g)


def _rmsnorm_bwd(x, g, dh, dres, *, name, also_bf16, tr=512):
    t, d = x.shape

    def body(x_ref, g_ref, dh_ref, dres_ref, dx_ref, *rest):
        dg_ref = rest[-1]
        _, vjp = jax.vjp(_rms, x_ref[...], g_ref[...])
        dx, dg = vjp(dh_ref[...])
        dx = dx + dres_ref[...]
        dx_ref[...] = dx
        if also_bf16:
            rest[0][...] = dx.astype(BF16)

        @pl.when(pl.program_id(0) == 0)
        def _():
            dg_ref[...] = jnp.zeros_like(dg_ref)

        dg_ref[...] += dg

    row = pl.BlockSpec((tr, d), lambda i: (i, 0))
    vec = pl.BlockSpec((1, d), lambda i: (0, 0))
    return pl.pallas_call(
        body, name=name, grid=(t // tr,),
        in_specs=[row, vec, row, row],
        out_specs=[row] + [row] * also_bf16 + [vec],
        out_shape=[jax.ShapeDtypeStruct((t, d), F32)] + [jax.ShapeDtypeStruct((t, d), BF16)] * also_bf16
                  + [jax.ShapeDtypeStruct((1, d), F32)],
        compiler_params=_params(("arbitrary",), VMEM_LIMIT),
    )(x, g, dh, dres)


def _seg_block(name, width):
    off = SEGMENTS[name][0]
    assert off % width == 0
    return off // width


def _attn_prep_fwd(proj, cos, sin_signed, gq, gk, *, tr=512):
    t = proj.shape[0]

    def body(q_ref, k_ref, v_ref, cos_ref, sin_ref, gq_ref, gk_ref, qo_ref, ko_ref, vo_ref):
        cos_t, sin_t = cos_ref[...], sin_ref[...]
        for h in range(ATTN_HEADS):
            cols = slice(h * HEAD_DIM, (h + 1) * HEAD_DIM)
            qo_ref[:, cols] = _rope(_rms(q_ref[:, cols], gq_ref[...]), cos_t, sin_t).astype(BF16)
        for h in range(KV_HEADS):
            cols = slice(h * HEAD_DIM, (h + 1) * HEAD_DIM)
            ko_ref[:, cols] = _rope(_rms(k_ref[:, cols], gk_ref[...]), cos_t, sin_t).astype(BF16)
        vo_ref[...] = v_ref[...].astype(BF16)

    qb, kb, vb = _seg_block("qa", ATTN_WIDTH), _seg_block("ka", KV_WIDTH), _seg_block("va", KV_WIDTH)
    tab = pl.BlockSpec((tr, HEAD_DIM), lambda i: (i, 0))
    vec = pl.BlockSpec((1, HEAD_DIM), lambda i: (0, 0))
    return pl.pallas_call(
        body, name="attn_prep_fwd", grid=(t // tr,),
        in_specs=[pl.BlockSpec((tr, ATTN_WIDTH), lambda i: (i, qb)),
                  pl.BlockSpec((tr, KV_WIDTH), lambda i: (i, kb)),
                  pl.BlockSpec((tr, KV_WIDTH), lambda i: (i, vb)),
                  tab, tab, vec, vec],
        out_specs=[pl.BlockSpec((tr, ATTN_WIDTH), lambda i: (i, 0)),
                   pl.BlockSpec((tr, KV_WIDTH), lambda i: (i, 0)),
                   pl.BlockSpec((tr, KV_WIDTH), lambda i: (i, 0))],
        out_shape=[jax.ShapeDtypeStruct((t, ATTN_WIDTH), BF16),
                   jax.ShapeDtypeStruct((t, KV_WIDTH), BF16),
                   jax.ShapeDtypeStruct((t, KV_WIDTH), BF16)],
        compiler_params=_params(("parallel",), VMEM_LIMIT),
    )(proj, proj, proj, cos, sin_signed, gq, gk)


def _attn_heads(q, kcat, vcat, sink_col, bias):
    s = bdot(q, kcat, "nt") * (HEAD_DIM ** -0.5) + bias
    m = lax.stop_gradient(jnp.maximum(jnp.max(s, axis=-1, keepdims=True), sink_col))
    p = jnp.exp(s - m)
    p = p / (jnp.sum(p, axis=-1, keepdims=True) + jnp.exp(sink_col - m))
    return bdot(p, vcat, "nn")


def _attn_bias():
    rows, cols = GQA_GROUP * ATTN_BLOCK, 3 * ATTN_BLOCK
    qi = lax.broadcasted_iota(jnp.int32, (3, rows, cols), 1) % ATTN_BLOCK
    sj = lax.broadcasted_iota(jnp.int32, (3, rows, cols), 2)
    kind = lax.broadcasted_iota(jnp.int32, (3, rows, cols), 0)
    inside = jnp.where(kind == 0, sj >= ATTN_BLOCK, jnp.where(kind == 2, sj < 2 * ATTN_BLOCK, True))
    valid = (jnp.abs(sj - ATTN_BLOCK - qi) <= WINDOW) & inside
    return jnp.where(valid, 0.0, -jnp.inf).astype(F32)


def _attn_bias_spec(nb, block_of):
    def index(*ix):
        n = block_of(*ix)
        return (jnp.where(n == 0, 0, jnp.where(n == nb - 1, 2, 1)), 0, 0)
    return pl.BlockSpec((None, GQA_GROUP * ATTN_BLOCK, 3 * ATTN_BLOCK), index)


def _head_rows(g):
    return slice(g * ATTN_BLOCK, (g + 1) * ATTN_BLOCK)


def _head_cols(g):
    return slice(g * HEAD_DIM, (g + 1) * HEAD_DIM)


def _stack_heads(ref):
    return jnp.concatenate([ref[:, _head_cols(g)] for g in range(GQA_GROUP)], axis=0).astype(F32)


def _sink_column(sink_ref, h):
    return jnp.concatenate([jnp.full((ATTN_BLOCK, 1), sink_ref[h * GQA_GROUP + g], F32)
                            for g in range(GQA_GROUP)], axis=0)


def _attn_specs(nb):
    q_spec = pl.BlockSpec((ATTN_BLOCK, GQA_GROUP * HEAD_DIM), lambda h, n: (n, h))
    kv_specs = [
        pl.BlockSpec((ATTN_BLOCK, HEAD_DIM), lambda h, n: (jnp.maximum(n - 1, 0), h)),
        pl.BlockSpec((ATTN_BLOCK, HEAD_DIM), lambda h, n: (n, h)),
        pl.BlockSpec((ATTN_BLOCK, HEAD_DIM), lambda h, n: (jnp.minimum(n + 1, nb - 1), h)),
    ]
    return q_spec, kv_specs


def _attn_fwd(q, k, v, sink, bias):
    t = q.shape[0]
    nb = t // ATTN_BLOCK

    def body(sink_ref, bias_ref, q_ref, kp_ref, kc_ref, kn_ref, vp_ref, vc_ref, vn_ref, o_ref):
        bias = bias_ref[...]
        args = []
        for h in range(KV_HEADS):
            q = jnp.concatenate([q_ref[:, _head_cols(h * GQA_GROUP + g)] for g in range(GQA_GROUP)], axis=0)
            kcat = jnp.concatenate([r[:, _head_cols(h)] for r in (kp_ref, kc_ref, kn_ref)], axis=0)
            vcat = jnp.concatenate([r[:, _head_cols(h)] for r in (vp_ref, vc_ref, vn_ref)], axis=0)
            args.append((q.astype(F32), kcat.astype(F32), vcat.astype(F32), _sink_column(sink_ref, h)))
        outs = [_attn_heads(*a, bias=bias).astype(BF16) for a in args]
        for h, o in enumerate(outs):
            for g in range(GQA_GROUP):
                o_ref[:, _head_cols(h * GQA_GROUP + g)] = o[_head_rows(g)]

    q_spec = pl.BlockSpec((ATTN_BLOCK, ATTN_WIDTH), lambda n: (n, 0))
    kv_specs = [pl.BlockSpec((ATTN_BLOCK, KV_WIDTH), lambda n: (jnp.maximum(n - 1, 0), 0)),
                pl.BlockSpec((ATTN_BLOCK, KV_WIDTH), lambda n: (n, 0)),
                pl.BlockSpec((ATTN_BLOCK, KV_WIDTH), lambda n: (jnp.minimum(n + 1, nb - 1), 0))]
    return pl.pallas_call(
        body, name="attn_fwd", grid=(nb,),
        in_specs=[pl.BlockSpec(memory_space=pltpu.SMEM), _attn_bias_spec(nb, lambda n: n), q_spec] + kv_specs + kv_specs,
        out_specs=q_spec,
        out_shape=jax.ShapeDtypeStruct((t, ATTN_WIDTH), BF16),
        compiler_params=_params(("parallel",), VMEM_LIMIT),
    )(sink, bias, q, k, k, k, v, v, v)


def _attn_bwd(q, k, v, sink, bias, dmix):
    t = q.shape[0]
    nb = t // ATTN_BLOCK

    def body(sink_ref, bias_ref, q_ref, kp_ref, kc_ref, kn_ref, vp_ref, vc_ref, vn_ref, do_ref,
             dq_ref, dk_lo, dk_mid, dk_hi, dv_lo, dv_mid, dv_hi, dsink_ref):
        h, n = pl.program_id(0), pl.program_id(1)
        kcat = jnp.concatenate([kp_ref[...], kc_ref[...], kn_ref[...]], axis=0).astype(F32)
        vcat = jnp.concatenate([vp_ref[...], vc_ref[...], vn_ref[...]], axis=0).astype(F32)
        _, vjp = jax.vjp(functools.partial(_attn_heads, bias=bias_ref[...]),
                         _stack_heads(q_ref), kcat, vcat, _sink_column(sink_ref, h))
        dq, dk, dv, dsink_col = vjp(_stack_heads(do_ref))
        row = lax.broadcasted_iota(jnp.int32, (8, HEAD_DIM), 0)
        dsink = jnp.zeros((8, HEAD_DIM), F32)
        for g in range(GQA_GROUP):
            dq_ref[:, _head_cols(g)] = dq[_head_rows(g)]
            dsink = dsink + jnp.where(row == g, jnp.sum(dsink_col[_head_rows(g)]), 0.0)
        for i, (dk_ref, dv_ref) in enumerate(((dk_lo, dv_lo), (dk_mid, dv_mid), (dk_hi, dv_hi))):
            rows = slice(i * ATTN_BLOCK, (i + 1) * ATTN_BLOCK)
            dk_ref[...] = dk[rows]
            dv_ref[...] = dv[rows]

        @pl.when(n == 0)
        def _():
            dsink_ref[...] = jnp.zeros_like(dsink_ref)

        dsink_ref[...] += dsink

    q_spec, kv_specs = _attn_specs(nb)
    kv_out = pl.BlockSpec((ATTN_BLOCK, HEAD_DIM), lambda h, n: (n, h))
    kv_shape = jax.ShapeDtypeStruct((t, KV_WIDTH), F32)
    return pl.pallas_call(
        body, name="attn_bwd", grid=(KV_HEADS, nb),
        in_specs=[pl.BlockSpec(memory_space=pltpu.SMEM), _attn_bias_spec(nb, lambda h, n: n), q_spec]
                 + kv_specs + kv_specs + [q_spec],
        out_specs=[q_spec] + [kv_out] * 6 + [pl.BlockSpec((None, 8, HEAD_DIM), lambda h, n: (h, 0, 0))],
        out_shape=[jax.ShapeDtypeStruct((t, ATTN_WIDTH), F32)] + [kv_shape] * 6
                  + [jax.ShapeDtypeStruct((KV_HEADS, 8, HEAD_DIM), F32)],
        compiler_params=_params(("parallel", "arbitrary"), VMEM_LIMIT),
    )(sink, bias, q, k, k, k, v, v, v, dmix)


def _attn_prep_bwd(proj, cos, sin_signed, gq, gk, dq, dks, dvs):
    t = proj.shape[0]
    tr = ATTN_BLOCK
    nb = t // tr

    def body(q_ref, k_ref, cos_ref, sin_ref, gq_ref, gk_ref, dq_ref,
             dk_lo, dk_mid, dk_hi, dv_lo, dv_mid, dv_hi,
             dqo_ref, dko_ref, dvo_ref, dgq_ref, dgk_ref):
        n = pl.program_id(0)
        cos_t, sin_t = cos_ref[...], sin_ref[...]
        has_next = (n < nb - 1).astype(F32)
        has_prev = (n > 0).astype(F32)
        dk = dk_lo[...] * has_next + dk_mid[...] + dk_hi[...] * has_prev
        dv = dv_lo[...] * has_next + dv_mid[...] + dv_hi[...] * has_prev
        dvo_ref[...] = dv.astype(BF16)
        dgq = jnp.zeros((1, HEAD_DIM), F32)
        dgk = jnp.zeros((1, HEAD_DIM), F32)
        for h in range(ATTN_HEADS):
            cols = slice(h * HEAD_DIM, (h + 1) * HEAD_DIM)
            _, vjp = jax.vjp(_rms, q_ref[:, cols], gq_ref[...])
            dx, dg = vjp(_rope_transposed(dq_ref[:, cols], cos_t, sin_t))
            dqo_ref[:, cols] = dx.astype(BF16)
            dgq = dgq + dg
        for h in range(KV_HEADS):
            cols = slice(h * HEAD_DIM, (h + 1) * HEAD_DIM)
            _, vjp = jax.vjp(_rms, k_ref[:, cols], gk_ref[...])
            dx, dg = vjp(_rope_transposed(dk[:, cols], cos_t, sin_t))
            dko_ref[:, cols] = dx.astype(BF16)
            dgk = dgk + dg

        @pl.when(n == 0)
        def _():
            dgq_ref[...] = jnp.zeros_like(dgq_ref)
            dgk_ref[...] = jnp.zeros_like(dgk_ref)

        dgq_ref[...] += dgq
        dgk_ref[...] += dgk

    qb, kb = _seg_block("qa", ATTN_WIDTH), _seg_block("ka", KV_WIDTH)
    tab = pl.BlockSpec((tr, HEAD_DIM), lambda i: (i, 0))
    vec = pl.BlockSpec((1, HEAD_DIM), lambda i: (0, 0))
    kv = [pl.BlockSpec((tr, KV_WIDTH), lambda i: (jnp.minimum(i + 1, nb - 1), 0)),
          pl.BlockSpec((tr, KV_WIDTH), lambda i: (i, 0)),
          pl.BlockSpec((tr, KV_WIDTH), lambda i: (jnp.maximum(i - 1, 0), 0))]
    wide = pl.BlockSpec((tr, ATTN_WIDTH), lambda i: (i, 0))
    narrow = pl.BlockSpec((tr, KV_WIDTH), lambda i: (i, 0))
    return pl.pallas_call(
        body, name="attn_prep_bwd", grid=(nb,),
        in_specs=[pl.BlockSpec((tr, ATTN_WIDTH), lambda i: (i, qb)),
                  pl.BlockSpec((tr, KV_WIDTH), lambda i: (i, kb)),
                  tab, tab, vec, vec, wide] + kv + kv,
        out_specs=[wide, narrow, narrow, vec, vec],
        out_shape=[jax.ShapeDtypeStruct((t, ATTN_WIDTH), BF16),
                   jax.ShapeDtypeStruct((t, KV_WIDTH), BF16),
                   jax.ShapeDtypeStruct((t, KV_WIDTH), BF16),
                   jax.ShapeDtypeStruct((1, HEAD_DIM), F32),
                   jax.ShapeDtypeStruct((1, HEAD_DIM), F32)],
        compiler_params=_params(("arbitrary",), VMEM_LIMIT),
    )(proj, proj, cos, sin_signed, gq, gk, dq, *dks, *dvs)


def _decay_fn(lr, w2, ba):
    return _log_sigmoid(bdot(lr, w2, "nn") + ba) / GLA_GATE_NORMALIZER


def _gla_prep_fwd(proj_lr, w2, ba2, *, tr=512):
    t = proj_lr.shape[0]
    width = 2 * GLA_KEY_WIDTH

    def body(lr_ref, w2_ref, ba_ref, g_ref):
        g_ref[...] = _decay_fn(lr_ref[...], w2_ref[...], ba_ref[...])

    return pl.pallas_call(
        body, name="gla_prep_fwd", grid=(t // tr,),
        in_specs=[pl.BlockSpec((tr, LR_PAD), lambda i: (i, 0)),
                  pl.BlockSpec((LR_PAD, width), lambda i: (0, 0)),
                  pl.BlockSpec((1, width), lambda i: (0, 0))],
        out_specs=pl.BlockSpec((tr, width), lambda i: (i, 0)),
        out_shape=jax.ShapeDtypeStruct((t, width), F32),
        compiler_params=_params(("parallel",), VMEM_LIMIT),
    )(proj_lr, w2, ba2)


def _gla_prep_bwd(proj_lr, w2, ba2, dg_f, dg_b, *, tr=512):
    t = proj_lr.shape[0]
    width = 2 * GLA_KEY_WIDTH

    def body(lr_ref, w2_ref, ba_ref, dgf_ref, dgb_ref, dlr_ref, dw2_ref, dba_ref):
        _, vjp = jax.vjp(_decay_fn, lr_ref[...], w2_ref[...], ba_ref[...])
        dlr, dw2, dba = vjp(jnp.concatenate([dgf_ref[...], dgb_ref[...]], axis=1))
        dlr_ref[...] = dlr.astype(BF16)

        @pl.when(pl.program_id(0) == 0)
        def _():
            dw2_ref[...] = jnp.zeros_like(dw2_ref)
            dba_ref[...] = jnp.zeros_like(dba_ref)

        dw2_ref[...] += dw2
        dba_ref[...] += dba

    half = pl.BlockSpec((tr, GLA_KEY_WIDTH), lambda i: (i, 0))
    return pl.pallas_call(
        body, name="gla_prep_bwd", grid=(t // tr,),
        in_specs=[pl.BlockSpec((tr, LR_PAD), lambda i: (i, 0)),
                  pl.BlockSpec((LR_PAD, width), lambda i: (0, 0)),
                  pl.BlockSpec((1, width), lambda i: (0, 0)), half, half],
        out_specs=[pl.BlockSpec((tr, LR_PAD), lambda i: (i, 0)),
                   pl.BlockSpec((LR_PAD, width), lambda i: (0, 0)),
                   pl.BlockSpec((1, width), lambda i: (0, 0))],
        out_shape=[jax.ShapeDtypeStruct((t, LR_PAD), BF16),
                   jax.ShapeDtypeStruct((LR_PAD, width), F32),
                   jax.ShapeDtypeStruct((1, width), F32)],
        compiler_params=_params(("arbitrary",), VMEM_LIMIT),
    )(proj_lr, w2, ba2, dg_f, dg_b)


def _gla_k(h):
    return slice(h * GLA_DK, (h + 1) * GLA_DK)


def _gla_v(h):
    return slice(h * GLA_DV, (h + 1) * GLA_DV)


def _running_sum(x, downward):
    n = x.shape[0]
    row = lax.broadcasted_iota(jnp.int32, x.shape, 0)
    step = 1
    while step < n:
        if downward:
            x = x + jnp.where(row >= step, pltpu.roll(x, step, 0), 0.0)
        else:
            x = x + jnp.where(row < n - step, pltpu.roll(x, n - step, 0), 0.0)
        step *= 2
    return x


@functools.partial(jax.custom_vjp, nondiff_argnums=(1,))
def _cumsum_rows(x, downward):
    return _running_sum(x, downward)


def _cumsum_rows_fwd(x, downward):
    return _running_sum(x, downward), None


def _cumsum_rows_bwd(downward, _, ct):
    return (_running_sum(ct, not downward),)


_cumsum_rows.defvjp(_cumsum_rows_fwd, _cumsum_rows_bwd)


def _gla_chunk(q, k, v, g, state, forward):
    c = GLA_CHUNK
    row = lax.broadcasted_iota(jnp.int32, (c, c), 0)
    col = lax.broadcasted_iota(jnp.int32, (c, c), 1)
    rid = lax.broadcasted_iota(jnp.int32, (c, GLA_DK), 0)
    q = q * (GLA_DK ** -0.5)
    if forward:
        see = row >= col
        upto_ref = rid <= c // 2
    else:
        see = row < col
        upto_ref = rid >= c - 1 - c // 2
    b = _cumsum_rows(g, forward)
    b_last = jnp.sum(g, axis=0, keepdims=True)
    b_ref = jnp.sum(jnp.where(upto_ref, g, 0.0), axis=0, keepdims=True)
    a = bdot(q * jnp.exp(b - b_ref), k * jnp.exp(b_ref - b), "nt")
    a = jnp.where(see, a, 0.0)
    o = bdot(a, v, "nn") + bdot(q * jnp.exp(b), state, "nt")
    new_state = state * jnp.exp(b_last) + bdot(v, k * jnp.exp(b_last - b), "tn")
    return o, new_state


def _gla_fwd(proj, g):
    t = proj.shape[0]
    c, per = GLA_CHUNK, GLA_PER_STEP
    nchunk = t // c
    nstep = nchunk // per
    qb, kb, vb = _seg_block("qg", GLA_KEY_WIDTH), _seg_block("kg", GLA_KEY_WIDTH), _seg_block("vg", GLA_WIDTH)

    def body(qf, kf, vf, gf, qr, kr, vr, gr, of_ref, ob_ref, sf_ref, sb_ref, state):
        @pl.when(pl.program_id(0) == 0)
        def _():
            state[...] = jnp.zeros_like(state)

        dirs = ((qf, kf, vf, gf, of_ref, sf_ref), (qr, kr, vr, gr, ob_ref, sb_ref))
        states = [[state[d, h] for h in range(GLA_HEADS)] for d in range(2)]
        for turn in range(per):
            chunk = (turn, per - 1 - turn)
            args = []
            for d, (q_ref, k_ref, v_ref, g_ref, _, _) in enumerate(dirs):
                rows = pl.ds(chunk[d] * c, c)
                args += [(q_ref[rows, _gla_k(h)], k_ref[rows, _gla_k(h)], v_ref[rows, _gla_v(h)],
                          g_ref[rows, _gla_k(h)], states[d][h]) for h in range(GLA_HEADS)]
            results = [_gla_chunk(*a, forward=(i < GLA_HEADS)) for i, a in enumerate(args)]
            for i, (a, (o, s_out)) in enumerate(zip(args, results)):
                d, h = divmod(i, GLA_HEADS)
                dirs[d][5][chunk[d], h] = a[4]
                dirs[d][4][pl.ds(chunk[d] * c, c), _gla_v(h)] = o
                states[d][h] = s_out
        for d in range(2):
            for h in range(GLA_HEADS):
                state[d, h] = states[d][h]

    specs, outs = [], []
    for d in range(2):
        ci = (lambda i: i) if d == 0 else (lambda i: nstep - 1 - i)
        specs += [pl.BlockSpec((per * c, GLA_KEY_WIDTH), lambda i, ci=ci: (ci(i), qb)),
                  pl.BlockSpec((per * c, GLA_KEY_WIDTH), lambda i, ci=ci: (ci(i), kb)),
                  pl.BlockSpec((per * c, GLA_WIDTH), lambda i, ci=ci: (ci(i), vb)),
                  pl.BlockSpec((per * c, GLA_KEY_WIDTH), lambda i, ci=ci, d=d: (ci(i), d))]
        outs.append(pl.BlockSpec((per * c, GLA_WIDTH), lambda i, ci=ci: (ci(i), 0)))
    for d in range(2):
        ci = (lambda i: i) if d == 0 else (lambda i: nstep - 1 - i)
        outs.append(pl.BlockSpec((per, GLA_HEADS, GLA_DV, GLA_DK), lambda i, ci=ci: (ci(i), 0, 0, 0)))
    o_shape = jax.ShapeDtypeStruct((t, GLA_WIDTH), F32)
    s_shape = jax.ShapeDtypeStruct((nchunk, GLA_HEADS, GLA_DV, GLA_DK), F32)
    return pl.pallas_call(
        body, name="gla_fwd", grid=(nstep,),
        in_specs=specs, out_specs=outs,
        out_shape=[o_shape, o_shape, s_shape, s_shape],
        scratch_shapes=[pltpu.VMEM((2, GLA_HEADS, GLA_DV, GLA_DK), F32)],
        compiler_params=_params(("arbitrary",), VMEM_LIMIT),
    )(proj, proj, proj, g, proj, proj, proj, g)


def _gla_bwd(proj, g, s_f, s_b, do):
    t = proj.shape[0]
    c, per = GLA_CHUNK, GLA_PER_STEP
    nchunk = t // c
    nstep = nchunk // per
    qb, kb, vb = _seg_block("qg", GLA_KEY_WIDTH), _seg_block("kg", GLA_KEY_WIDTH), _seg_block("vg", GLA_WIDTH)

    def body(*refs):
        ins, outs, dstate = refs[:12], refs[12:20], refs[20]

        @pl.when(pl.program_id(0) == 0)
        def _():
            dstate[...] = jnp.zeros_like(dstate)

        dstates = [[dstate[d, h] for h in range(GLA_HEADS)] for d in range(2)]
        for turn in range(per):
            chunk = (per - 1 - turn, turn)
            loaded = []
            for d in range(2):
                q_ref, k_ref, v_ref, g_ref, s_ref, do_ref = ins[6 * d:6 * d + 6]
                rows = pl.ds(chunk[d] * c, c)
                for h in range(GLA_HEADS):
                    loaded.append(((q_ref[rows, _gla_k(h)], k_ref[rows, _gla_k(h)], v_ref[rows, _gla_v(h)],
                                    g_ref[rows, _gla_k(h)], s_ref[chunk[d], h]),
                                   (do_ref[rows, _gla_v(h)], dstates[d][h])))
            grads = []
            for i, (primals, cotangents) in enumerate(loaded):
                _, vjp = jax.vjp(functools.partial(_gla_chunk, forward=(i < GLA_HEADS)), *primals)
                grads.append(vjp(cotangents))
            for i, (dq, dk, dv, dg, ds) in enumerate(grads):
                d, h = divmod(i, GLA_HEADS)
                rows = pl.ds(chunk[d] * c, c)
                dq_ref, dk_ref, dv_ref, dg_ref = outs[4 * d:4 * d + 4]
                dq_ref[rows, _gla_k(h)] = dq
                dk_ref[rows, _gla_k(h)] = dk
                dv_ref[rows, _gla_v(h)] = dv
                dg_ref[rows, _gla_k(h)] = dg
                dstates[d][h] = ds
        for d in range(2):
            for h in range(GLA_HEADS):
                dstate[d, h] = dstates[d][h]

    specs, outs, shapes = [], [], []
    for d in range(2):
        ci = (lambda i: nstep - 1 - i) if d == 0 else (lambda i: i)
        specs += [pl.BlockSpec((per * c, GLA_KEY_WIDTH), lambda i, ci=ci: (ci(i), qb)),
                  pl.BlockSpec((per * c, GLA_KEY_WIDTH), lambda i, ci=ci: (ci(i), kb)),
                  pl.BlockSpec((per * c, GLA_WIDTH), lambda i, ci=ci: (ci(i), vb)),
                  pl.BlockSpec((per * c, GLA_KEY_WIDTH), lambda i, ci=ci, d=d: (ci(i), d)),
                  pl.BlockSpec((per, GLA_HEADS, GLA_DV, GLA_DK), lambda i, ci=ci: (ci(i), 0, 0, 0)),
                  pl.BlockSpec((per * c, GLA_WIDTH), lambda i, ci=ci: (ci(i), 0))]
        key = pl.BlockSpec((per * c, GLA_KEY_WIDTH), lambda i, ci=ci: (ci(i), 0))
        val = pl.BlockSpec((per * c, GLA_WIDTH), lambda i, ci=ci: (ci(i), 0))
        outs += [key, key, val, key]
        shapes += [jax.ShapeDtypeStruct((t, GLA_KEY_WIDTH), F32), jax.ShapeDtypeStruct((t, GLA_KEY_WIDTH), F32),
                   jax.ShapeDtypeStruct((t, GLA_WIDTH), F32), jax.ShapeDtypeStruct((t, GLA_KEY_WIDTH), F32)]
    return pl.pallas_call(
        body, name="gla_bwd", grid=(nstep,),
        in_specs=specs, out_specs=outs, out_shape=shapes,
        scratch_shapes=[pltpu.VMEM((2, GLA_HEADS, GLA_DV, GLA_DK), F32)],
        compiler_params=_params(("arbitrary",), VMEM_LIMIT),
    )(proj, proj, proj, g, s_f, do, proj, proj, proj, g, s_b, do)


def _gla_out_head(o_f, o_b, gate, gn):
    return _rms(o_f + o_b, gn) * _silu(gate)


def _gla_out_fwd(o_f, o_b, proj, gn, *, tr=512):
    t = o_f.shape[0]
    gb = _seg_block("gate", GLA_WIDTH)

    def body(of_ref, ob_ref, gate_ref, gn_ref, out_ref):
        for h in range(GLA_HEADS):
            vc = slice(h * GLA_DV, (h + 1) * GLA_DV)
            out_ref[:, vc] = _gla_out_head(of_ref[:, vc], ob_ref[:, vc], gate_ref[:, vc], gn_ref[...]).astype(BF16)

    wide = pl.BlockSpec((tr, GLA_WIDTH), lambda i: (i, 0))
    return pl.pallas_call(
        body, name="gla_out_fwd", grid=(t // tr,),
        in_specs=[wide, wide, pl.BlockSpec((tr, GLA_WIDTH), lambda i: (i, gb)),
                  pl.BlockSpec((1, GLA_DV), lambda i: (0, 0))],
        out_specs=wide,
        out_shape=jax.ShapeDtypeStruct((t, GLA_WIDTH), BF16),
        compiler_params=_params(("parallel",), VMEM_LIMIT),
    )(o_f, o_b, proj, gn)


def _gla_out_bwd(o_f, o_b, proj, gn, dmix, *, tr=512):
    t = o_f.shape[0]
    gb = _seg_block("gate", GLA_WIDTH)

    def body(of_ref, ob_ref, gate_ref, gn_ref, dout_ref, do_ref, dgate_ref, dgn_ref):
        dgn = jnp.zeros((1, GLA_DV), F32)
        for h in range(GLA_HEADS):
            vc = slice(h * GLA_DV, (h + 1) * GLA_DV)
            _, vjp = jax.vjp(_gla_out_head, of_ref[:, vc], ob_ref[:, vc], gate_ref[:, vc], gn_ref[...])
            do, _, dgate, dg = vjp(dout_ref[:, vc])
            do_ref[:, vc] = do
            dgate_ref[:, vc] = dgate.astype(BF16)
            dgn = dgn + dg

        @pl.when(pl.program_id(0) == 0)
        def _():
            dgn_ref[...] = jnp.zeros_like(dgn_ref)

        dgn_ref[...] += dgn

    wide = pl.BlockSpec((tr, GLA_WIDTH), lambda i: (i, 0))
    vec = pl.BlockSpec((1, GLA_DV), lambda i: (0, 0))
    return pl.pallas_call(
        body, name="gla_out_bwd", grid=(t // tr,),
        in_specs=[wide, wide, pl.BlockSpec((tr, GLA_WIDTH), lambda i: (i, gb)), vec,
                  pl.BlockSpec((tr, GLA_WIDTH), lambda i: (i, 1))],
        out_specs=[wide, wide, vec],
        out_shape=[jax.ShapeDtypeStruct((t, GLA_WIDTH), F32), jax.ShapeDtypeStruct((t, GLA_WIDTH), BF16),
                   jax.ShapeDtypeStruct((1, GLA_DV), F32)],
        compiler_params=_params(("arbitrary",), VMEM_LIMIT),
    )(o_f, o_b, proj, gn, dmix)


CONV_TR = 1024
CONV_TC = 512
HALO = 8
HALO16 = 16


def _conv3(u, w, b):
    n = u.shape[0]
    return pltpu.roll(u, 1, 0) * w[0:1] + u * w[1:2] + pltpu.roll(u, n - 1, 0) * w[2:3] + b


def _conv_ext(main_ref, prev_ref, next_ref, r, nr):
    prev = prev_ref[...].astype(F32)[-HALO:] * (r > 0).astype(F32)
    nxt = next_ref[...].astype(F32)[:HALO] * (r < nr - 1).astype(F32)
    return jnp.concatenate([prev, main_ref[...].astype(F32), nxt], axis=0)


def _conv_specs(t, halo, half=None):
    per = CONV_TR // halo
    last = t // halo - 1
    lead = () if half is None else (None,)
    at = (lambda *ix: ix) if half is None else (lambda *ix: (half,) + ix)
    return [pl.BlockSpec(lead + (CONV_TR, CONV_TC), lambda j, r: at(r, j)),
            pl.BlockSpec(lead + (halo, CONV_TC), lambda j, r: at(jnp.maximum(r * per - 1, 0), j)),
            pl.BlockSpec(lead + (halo, CONV_TC), lambda j, r: at(jnp.minimum((r + 1) * per, last), j))]


def _ffn_mid_fwd(u, cw_g, cw_v, cb_g, cb_v):
    _, t, f = u.shape
    nr = t // CONV_TR

    def body(ug, ugp, ugn, uv, uvp, uvn, wg, wv, bg, bv, a_ref):
        r = pl.program_id(1)
        gate = _conv3(_conv_ext(ug, ugp, ugn, r, nr), wg[...], bg[...])[HALO:HALO + CONV_TR]
        val = _conv3(_conv_ext(uv, uvp, uvn, r, nr), wv[...], bv[...])[HALO:HALO + CONV_TR]
        a_ref[...] = (_silu(gate) * val).astype(BF16)

    w_spec = pl.BlockSpec((3, CONV_TC), lambda j, r: (0, j))
    b_spec = pl.BlockSpec((1, CONV_TC), lambda j, r: (0, j))
    return pl.pallas_call(
        body, name="ffn_mid_fwd", grid=(f // CONV_TC, nr),
        in_specs=_conv_specs(t, HALO, 0) + _conv_specs(t, HALO, 1) + [w_spec, w_spec, b_spec, b_spec],
        out_specs=pl.BlockSpec((CONV_TR, CONV_TC), lambda j, r: (r, j)),
        out_shape=jax.ShapeDtypeStruct((t, f), BF16),
        compiler_params=_params(("parallel", "parallel"), VMEM_LIMIT),
    )(u, u, u, u, u, u, cw_g, cw_v, cb_g, cb_v)


def _ffn_mid_bwd(u, cw_g, cw_v, cb_g, cb_v, da):
    _, t, f = u.shape
    nr = t // CONV_TR
    ext = CONV_TR + 2 * HALO

    def body(ug, ugp, ugn, uv, uvp, uvn, dam, dap, dan, wg, wv, bg, bv,
             du_ref, dwg_ref, dwv_ref, dbg_ref, dbv_ref):
        r = pl.program_id(1)
        shifted = []
        for main, prev, nxt in ((ug, ugp, ugn), (uv, uvp, uvn)):
            x = _conv_ext(main, prev, nxt, r, nr)
            shifted.append((pltpu.roll(x, 1, 0), x, pltpu.roll(x, ext - 1, 0)))
        da_x = _conv_ext(dam, dap, dan, r, nr)
        wg_t, wv_t = wg[...], wv[...]
        gate = shifted[0][0] * wg_t[0:1] + shifted[0][1] * wg_t[1:2] + shifted[0][2] * wg_t[2:3] + bg[...]
        val = shifted[1][0] * wv_t[0:1] + shifted[1][1] * wv_t[1:2] + shifted[1][2] * wv_t[2:3] + bv[...]
        sig = jax.nn.sigmoid(gate)
        silu = gate * sig
        d_val = da_x * silu
        d_gate = da_x * val * (sig + silu * (1.0 - sig))
        own = slice(HALO, HALO + CONV_TR)
        for half, (xs3, d, wt, dw_ref, db_ref) in enumerate(((shifted[0], d_gate, wg_t, dwg_ref, dbg_ref),
                                                            (shifted[1], d_val, wv_t, dwv_ref, dbv_ref))):
            du = pltpu.roll(d, ext - 1, 0) * wt[0:1] + d * wt[1:2] + pltpu.roll(d, 1, 0) * wt[2:3]
            du_ref[half] = du[own].astype(BF16)
            d_own = d[own]
            dw = jnp.concatenate([jnp.sum(x[own] * d_own, axis=0, keepdims=True) for x in xs3], axis=0)
            db = jnp.sum(d_own, axis=0, keepdims=True)

            @pl.when(r == 0)
            def _():
                dw_ref[...] = jnp.zeros_like(dw_ref)
                db_ref[...] = jnp.zeros_like(db_ref)

            dw_ref[...] += dw
            db_ref[...] += db

    w_spec = pl.BlockSpec((3, CONV_TC), lambda j, r: (0, j))
    b_spec = pl.BlockSpec((1, CONV_TC), lambda j, r: (0, j))
    return pl.pallas_call(
        body, name="ffn_mid_bwd", grid=(f // CONV_TC, nr),
        in_specs=(_conv_specs(t, HALO, 0) + _conv_specs(t, HALO, 1) + _conv_specs(t, HALO16)
                  + [w_spec, w_spec, b_spec, b_spec]),
        out_specs=[pl.BlockSpec((2, CONV_TR, CONV_TC), lambda j, r: (0, r, j)), w_spec, w_spec, b_spec, b_spec],
        out_shape=[jax.ShapeDtypeStruct((2, t, f), BF16),
                   jax.ShapeDtypeStruct((3, f), F32), jax.ShapeDtypeStruct((3, f), F32),
                   jax.ShapeDtypeStruct((1, f), F32), jax.ShapeDtypeStruct((1, f), F32)],
        compiler_params=_params(("parallel", "arbitrary"), VMEM_LIMIT),
    )(u, u, u, u, u, u, da, da, da, cw_g, cw_v, cb_g, cb_v)


def _down_proj_loss(act, w_down, x1, target, *, tm, tn):
    t, f = act.shape
    d = w_down.shape[1]

    def body(a_ref, w_ref, x_ref, t_ref, loss_ref, dy_ref, dyb_ref):
        y = lax.dot_general(a_ref[...], w_ref[...], _DIMS["nn"], preferred_element_type=F32) + x_ref[...]
        err = y - t_ref[...]
        dy = err * (1.0 / d)
        dy_ref[...] = dy
        dyb_ref[...] = dy.astype(BF16)
        part = 0.5 * jnp.sum(jnp.sum(err * err, axis=-1, keepdims=True) * (1.0 / d), axis=0, keepdims=True)

        @pl.when((pl.program_id(0) == 0) & (pl.program_id(1) == 0))
        def _():
            loss_ref[...] = jnp.zeros_like(loss_ref)

        loss_ref[...] += jnp.broadcast_to(part, loss_ref.shape)

    tile = pl.BlockSpec((tm, tn), lambda i, j: (i, j))
    return pl.pallas_call(
        body, name="down_proj_loss", grid=(t // tm, d // tn),
        in_specs=[pl.BlockSpec((tm, f), lambda i, j: (i, 0)), pl.BlockSpec((f, tn), lambda i, j: (0, j)), tile, tile],
        out_specs=[pl.BlockSpec((1, 128), lambda i, j: (0, 0)), tile, tile],
        out_shape=[jax.ShapeDtypeStruct((1, 128), F32), jax.ShapeDtypeStruct((t, d), F32),
                   jax.ShapeDtypeStruct((t, d), BF16)],
        compiler_params=_params(("arbitrary", "arbitrary"), VMEM_LIMIT),
    )(act, w_down, x1, target)


ANY = pl.BlockSpec(memory_space=pl.ANY)


def _position():
    return lax.axis_index("x"), lax.axis_index("y"), lax.axis_index("c")


def _other_chips(x, y):
    return [(1 - x, y), (x, 1 - y), (1 - x, 1 - y)]


def _handshake(peers):
    barrier = pltpu.get_barrier_semaphore()
    for peer in peers:
        pl.semaphore_signal(barrier, inc=1, device_id=peer, device_id_type=MESH)
    pl.semaphore_wait(barrier, len(peers))


def _exchange(body, operands, out_shapes, sems, *, name, collective_id):
    n_in, n_out = len(operands), len(out_shapes)

    def run(*refs):
        body(refs[:n_in], refs[n_in:n_in + n_out], *refs[n_in + n_out:])

    if collective_id is None:
        return pl.pallas_call(run, name=name, in_specs=[ANY] * n_in, out_specs=[ANY] * n_out,
                              out_shape=out_shapes, scratch_shapes=sems)(*operands)
    return pl.kernel(run, name=name, out_type=out_shapes,
                     mesh=plsc.ScalarSubcoreMesh(axis_name="sequencer", num_cores=1), scratch_types=sems,
                     compiler_params=pltpu.CompilerParams(collective_id=collective_id))(*operands)


def _all_gather(blocks, *, name, collective_id=None):
    na = len(blocks)

    def body(ins, outs, send_sems, recv_sems, local_sems):
        x, y, c = _position()
        me, sibling = (x, y, c), (x, y, 1 - c)
        along_x, along_y, diagonal = (1 - x, y, c), (x, 1 - y, c), (1 - x, 1 - y, c)
        first_c = c == 0
        relay_from = (jnp.where(first_c, x, 1 - x), jnp.where(first_c, 1 - y, y), c)
        relay_to = (jnp.where(first_c, 1 - x, x), jnp.where(first_c, y, 1 - y), c)
        if collective_id is not None:
            _handshake([sibling, along_x, along_y])

        def index(px, py, pc):
            return 4 * px + 2 * py + pc

        def copy(a, k, block, to, src=None):
            dst = outs[a].at[index(*block)]
            return pltpu.make_async_remote_copy(
                src_ref=dst if src is None else src, dst_ref=dst,
                send_sem=send_sems.at[a, k], recv_sem=recv_sems.at[a, k],
                device_id=to, device_id_type=MESH)

        pending = []
        for a in range(na):
            mine = pltpu.make_async_copy(ins[a], outs[a].at[index(*me)], local_sems.at[a])
            mine.start()
            pending.append(mine)
        sent = []
        for a in range(na):
            sent += [copy(a, 0, me, sibling, src=ins[a]), copy(a, 1, me, along_x, src=ins[a]),
                     copy(a, 2, me, along_y, src=ins[a])]
        for cp in sent:
            cp.start()

        def passes_on(k_in, owner, k_out):
            for a in range(na):
                copy(a, k_in, owner, me).wait_recv()
                cp = copy(a, k_out, owner, sibling)
                cp.start()
                sent.append(cp)

        passes_on(1, along_x, 4)
        passes_on(2, along_y, 5)
        for a in range(na):
            cp = copy(a, 3, relay_from, relay_to)
            cp.start()
            sent.append(cp)
        passes_on(3, diagonal, 6)
        for a in range(na):
            copy(a, 0, sibling, me).wait_recv()
            for k, owner in ((4, along_x), (5, along_y), (6, diagonal)):
                copy(a, k, (owner[0], owner[1], 1 - c), me).wait_recv()
        for cp in sent:
            cp.wait_send()
        for cp in pending:
            cp.wait()

    return _exchange(
        body, blocks, [jax.ShapeDtypeStruct((N_DEV,) + b.shape, b.dtype) for b in blocks],
        [pltpu.SemaphoreType.DMA((na, 7)), pltpu.SemaphoreType.DMA((na, 7)), pltpu.SemaphoreType.DMA((na,))],
        name=name, collective_id=collective_id)


def _grad_exchange(grads, parts, *, name, collective_id):
    ng, npart = len(grads), len(parts)

    def body(ins, outs, core_send, core_recv, chip_send, chip_recv, local_sems):
        x, y, c = _position()
        sibling = (x, y, 1 - c)
        chips = _other_chips(x, y)
        _handshake([sibling] + [(px, py, c) for px, py in chips])
        me = 2 * x + y
        copies = []
        for b in range(npart):
            src, dst = ins[ng + b], outs[ng + b]
            own = pltpu.make_async_copy(src.at[me], dst.at[me], local_sems.at[b])
            own.start()
            copies.append(own)
            for j, (px, py) in enumerate(chips):
                cp = pltpu.make_async_remote_copy(
                    src_ref=src.at[2 * px + py], dst_ref=dst.at[me],
                    send_sem=chip_send.at[b, j], recv_sem=chip_recv.at[b, j],
                    device_id=(px, py, c), device_id_type=MESH)
                cp.start()
                copies.append(cp)
        for a in range(ng):
            for k in range(N_CHIP):
                src = ins[a].at[k, 1 - c] if len(grads[a].shape) == 4 else ins[a].at[k]
                cp = pltpu.make_async_remote_copy(
                    src_ref=src, dst_ref=outs[a].at[k],
                    send_sem=core_send.at[a, k], recv_sem=core_recv.at[a, k],
                    device_id=sibling, device_id_type=MESH)
                cp.start()
                copies.append(cp)
        for cp in copies:
            cp.wait()

    shapes = ([jax.ShapeDtypeStruct((N_CHIP,) + g.shape[-2:], g.dtype) for g in grads]
              + [jax.ShapeDtypeStruct(p.shape, p.dtype) for p in parts])
    sems = [pltpu.SemaphoreType.DMA((max(ng, 1), N_CHIP)), pltpu.SemaphoreType.DMA((max(ng, 1), N_CHIP)),
            pltpu.SemaphoreType.DMA((max(npart, 1), 3)), pltpu.SemaphoreType.DMA((max(npart, 1), 3)),
            pltpu.SemaphoreType.DMA((max(npart, 1),))]
    out = _exchange(body, list(grads) + list(parts), shapes, sems, name=name, collective_id=collective_id)
    return out[:ng], out[ng:]


def _pair_sum(grad, theirs, core, *, tile, name, narrow=False):
    _, _, r, w = grad.shape
    tr, tw = tile
    assert r % tr == 0 and w % tw == 0

    def body(core_ref, mine_ref, theirs_ref, out_ref, *narrow_ref):
        total = mine_ref[...] + theirs_ref[...]
        out_ref[...] = total
        if narrow:
            narrow_ref[0][...] = total.astype(BF16)

    spec = pl.BlockSpec((None, tr, tw), lambda k, i, j, core_ref: (k, i, j))
    shapes = [jax.ShapeDtypeStruct((N_CHIP, r, w), F32)] + [jax.ShapeDtypeStruct((N_CHIP, r, w), BF16)] * narrow
    out = pl.pallas_call(
        body, name=name,
        grid_spec=pltpu.PrefetchScalarGridSpec(
            num_scalar_prefetch=1, grid=(N_CHIP, r // tr, w // tw),
            in_specs=[pl.BlockSpec((None, None, tr, tw), lambda k, i, j, core_ref: (k, core_ref[0], i, j)), spec],
            out_specs=[spec] * len(shapes)),
        out_shape=shapes,
        compiler_params=_params(("parallel", "parallel", "parallel"), VMEM_LIMIT),
    )(core, grad, theirs)
    return tuple(out) if narrow else out[0]


class _ReduceScatter:
    def __init__(self, core):
        self.core = core
        self.pending = None
        self.results = {}
        self.launches = 0

    def push(self, tag, grads, rows, then, narrow=False, finish=None):
        pair, kept, prev_tag = [], [], None
        if self.pending is not None:
            prev_tag, prev, theirs, prev_rows, prev_narrow, prev_finish = self.pending
            if prev_finish is not None:
                pair = prev_finish(theirs)
            else:
                pair = [_pair_sum(g, s, self.core, tile=tile, name=f"pair_sum_{prev_tag}_{i}", narrow=prev_narrow)
                        for i, (g, s, tile) in enumerate(zip(prev, theirs, prev_rows))]
            if prev_narrow:
                kept, pair = [p[0] for p in pair], [p[1] for p in pair]
        grads, pair, then = lax.optimization_barrier((list(grads), pair, then))
        if finish is None:
            grads = [g.reshape((N_CHIP, 2) + g.shape[1:]) for g in grads]
        self.launches += 1
        theirs, parts = _grad_exchange(grads, pair, name=f"grad_exchange_{self.launches}",
                                       collective_id=1 + self.launches)
        if prev_tag is not None:
            self.results[prev_tag] = (parts, kept)
        self.pending = (tag, grads, theirs, rows, narrow, finish) if tag is not None else None
        return then

    def result(self, tag):
        return self.results[tag]


def _adamw(parts, w, m, v, *, tile, name, own=None, chip=None):
    n, r, cols = parts.shape
    tr, tw = tile
    assert r % tr == 0 and cols % tw == 0 and w.shape == (r, cols)
    c1 = 1.0 - ADAM_B1 ** ADAM_STEP
    c2 = 1.0 - ADAM_B2 ** ADAM_STEP

    def update(g, w_ref, m_ref, v_ref, g_ref, d_ref, nm_ref, nv_ref):
        new_m = ADAM_B1 * m_ref[...] + (1.0 - ADAM_B1) * g
        new_v = ADAM_B2 * v_ref[...] + (1.0 - ADAM_B2) * (g * g)
        m_hat = new_m / c1
        v_hat = new_v / c2
        g_ref[...] = g
        d_ref[...] = -ADAM_LR * (m_hat / (jnp.sqrt(v_hat) + ADAM_EPS) + ADAM_WD * w_ref[...])
        nm_ref[...] = new_m
        nv_ref[...] = new_v

    shape = jax.ShapeDtypeStruct((r, cols), F32)
    if own is None:
        def body(p_ref, *refs):
            g = p_ref[0]
            for k in range(1, n):
                g = g + p_ref[k]
            update(g, *refs)

        spec = pl.BlockSpec((tr, tw), lambda i, j: (i, j))
        return pl.pallas_call(
            body, name=name, grid=(r // tr, cols // tw),
            in_specs=[pl.BlockSpec((n, tr, tw), lambda i, j: (0, i, j)), spec, spec, spec],
            out_specs=[spec] * 4, out_shape=[shape] * 4,
            compiler_params=_params(("parallel", "parallel"), VMEM_LIMIT),
        )(parts, w, m, v)

    def body(chip_ref, p_ref, own_ref, *refs):
        g = None
        for k in range(n):
            term = jnp.where(chip_ref[0] == k, own_ref[...], p_ref[k].astype(F32))
            g = term if g is None else g + term
        update(g, *refs)

    spec = pl.BlockSpec((tr, tw), lambda i, j, chip_ref: (i, j))
    return pl.pallas_call(
        body, name=name,
        grid_spec=pltpu.PrefetchScalarGridSpec(
            num_scalar_prefetch=1, grid=(r // tr, cols // tw),
            in_specs=[pl.BlockSpec((n, tr, tw), lambda i, j, chip_ref: (0, i, j)),
                      pl.BlockSpec((None, tr, tw), lambda i, j, chip_ref: (chip_ref[0], i, j)), spec, spec, spec],
            out_specs=[spec] * 4),
        out_shape=[shape] * 4,
        compiler_params=_params(("parallel", "parallel"), VMEM_LIMIT),
    )(chip, parts, own, w, m, v)


LANES = 128


def _row_offsets(pieces):
    offsets, row = [], 0
    for p in pieces:
        assert p.shape[0] == 1 and p.shape[1] % LANES == 0, p.shape
        offsets.append(row)
        row += p.shape[1] // LANES
    return offsets, row


def _pack_rows(pieces):
    offsets, rows = _row_offsets(pieces)

    def body(*refs):
        out_ref = refs[-1]
        for ref, start in zip(refs[:-1], offsets):
            for j in range(ref.shape[1] // LANES):
                out_ref[start + j:start + j + 1, :] = ref[:, j * LANES:(j + 1) * LANES]

    return pl.pallas_call(body, name="pack_small_grads",
                          out_shape=jax.ShapeDtypeStruct((rows, LANES), F32))(*pieces)


def _adamw_rows(terms, ws, ms, vs):
    n_dev, rows, _ = terms.shape
    offsets, used = _row_offsets(ws)
    assert used + 1 == rows
    c1 = 1.0 - ADAM_B1 ** ADAM_STEP
    c2 = 1.0 - ADAM_B2 ** ADAM_STEP
    nw = len(ws)

    def body(*refs):
        t_ref = refs[0]
        w_refs, m_refs, v_refs = refs[1:1 + nw], refs[1 + nw:1 + 2 * nw], refs[1 + 2 * nw:1 + 3 * nw]
        outs = refs[1 + 3 * nw:]
        total = t_ref[0]
        for k in range(1, n_dev):
            total = total + t_ref[k]
        for i, start in enumerate(offsets):
            for j in range(ws[i].shape[1] // LANES):
                lanes = slice(j * LANES, (j + 1) * LANES)
                g = total[start + j:start + j + 1, :]
                new_m = ADAM_B1 * m_refs[i][:, lanes] + (1.0 - ADAM_B1) * g
                new_v = ADAM_B2 * v_refs[i][:, lanes] + (1.0 - ADAM_B2) * (g * g)
                delta = -ADAM_LR * ((new_m / c1) / (jnp.sqrt(new_v / c2) + ADAM_EPS) + ADAM_WD * w_refs[i][:, lanes])
                for kind, value in enumerate((g, delta, new_m, new_v)):
                    outs[kind * nw + i][:, lanes] = value
        outs[-1][...] = total[used:used + 1, :]

    shapes = [jax.ShapeDtypeStruct(w.shape, F32) for w in ws] * 4 + [jax.ShapeDtypeStruct((1, LANES), F32)]
    out = pl.pallas_call(body, name="adamw_replicated", out_shape=shapes,
                         compiler_params=_params(None, VMEM_LIMIT))(terms, *ws, *ms, *vs)
    return [list(out[kind * nw:(kind + 1) * nw]) for kind in range(4)], out[-1]


def _rope_tables(t):
    half = HEAD_DIM // 2
    inv = 1.0 / (ROPE_THETA ** (jnp.arange(half, dtype=F32) / half))
    ang = jnp.arange(t, dtype=jnp.int32).astype(F32)[:, None] * inv[None, :]
    cos, sin = jnp.cos(ang), jnp.sin(ang)
    return jnp.concatenate([cos, cos], axis=1), jnp.concatenate([-sin, sin], axis=1)


IN_KERNEL = IN_MAIN + LR_PAD


def _to_kernel_rows(w_t):
    order = sorted(SEGMENTS.values())
    pad = jnp.zeros((LR_PAD - (IN_TOTAL - IN_MAIN), w_t.shape[1]), w_t.dtype)
    return jnp.concatenate([w_t[src:src + width] for _, src, width in order] + [w_t[IN_MAIN:IN_TOTAL], pad], axis=0)


CONV_TAPS = 3
WA_BLOCK = GLA_KEY_WIDTH // N_DEV
SMALL_SIZES = (CONV_TAPS * UP_BLOCK, GLA_RANK * WA_BLOCK, GLA_RANK * WA_BLOCK)
SMALL_SHAPES = ((CONV_TAPS, UP_BLOCK), (GLA_RANK, WA_BLOCK), (GLA_RANK, WA_BLOCK))
SMALL_ROWS = sum(SMALL_SIZES) // LANES


def _small_block(conv, wa_f, wa_b):
    return jnp.concatenate([conv.reshape(-1), wa_f.reshape(-1), wa_b.reshape(-1)]).reshape(SMALL_ROWS, LANES)


def _small_unblock(block):
    flat, out, off = block.reshape(-1), [], 0
    for size, shape in zip(SMALL_SIZES, SMALL_SHAPES):
        out.append(flat[off:off + size].reshape(shape))
        off += size
    return out


def _small_blocks(conv_full, wa_f_full, wa_b_full):
    def by_device(a, width):
        return jnp.transpose(a.reshape(a.shape[0], N_DEV, width), (1, 0, 2)).reshape(N_DEV, -1)
    return jnp.concatenate([by_device(conv_full, UP_BLOCK), by_device(wa_f_full, WA_BLOCK),
                            by_device(wa_b_full, WA_BLOCK)], axis=1).reshape(N_DEV, SMALL_ROWS, LANES)


def _small_unblocks(blocks):
    flat, out, off = blocks.reshape(N_DEV, -1), [], 0
    for size, (rows, width) in zip(SMALL_SIZES, SMALL_SHAPES):
        part = flat[:, off:off + size].reshape(N_DEV, rows, width)
        out.append(jnp.transpose(part, (1, 0, 2)).reshape(rows, N_DEV * width))
        off += size
    return out


def _local_step(xs, target, norm1_g, w_in8, gq, gk, attn_sink, w2, ba2, gla_norm_g, w_out_full, norm2_g,
                w_up8, cw_g, cw_v, cb_g, cb_v, w_down_full, rs=None):
    t = xs.shape[0]
    tm = min(1024, t)
    tall = min(2048, t)
    cos, sin_signed = _rope_tables(t)
    sink = attn_sink.reshape(ATTN_HEADS)

    h1 = _rmsnorm_fwd(xs, norm1_g, name="norm1_fwd")
    w_in8, h1, cos, sin_signed = lax.optimization_barrier((w_in8, h1, cos, sin_signed))
    w_in_k = _to_kernel_rows(w_in8.reshape(IN_TOTAL, D_MODEL))
    proj = _matmul(h1, w_in_k, "nt", tm=tall, tn=IN_MAIN // 4, tk=D_MODEL, n_out=IN_MAIN, name="proj_main")
    proj_lr = _matmul(h1, w_in_k[IN_MAIN:], "nt", tm=tm, tn=LR_PAD, tk=D_MODEL, name="proj_lr")
    qa, ka, va = _attn_prep_fwd(proj, cos, sin_signed, gq, gk)
    assert t >= 2 * ATTN_BLOCK
    attn_bias = _attn_bias()
    o_attn = _attn_fwd(qa, ka, va, sink, attn_bias)
    g_dec = _gla_prep_fwd(proj_lr, w2, ba2)
    o_f, o_b, s_f, s_b = _gla_fwd(proj, g_dec)
    o_gla = _gla_out_fwd(o_f, o_b, proj, gla_norm_g)
    x1 = _out_proj(o_attn, o_gla, w_out_full, xs, tm=tall, tn=512)
    h2 = _rmsnorm_fwd(x1, norm2_g, name="norm2_fwd")
    u = _up_proj(h2, w_up8, tm=tm)
    act = _ffn_mid_fwd(u, cw_g, cw_v, cb_g, cb_v)
    loss_part, dy, dy_b = _down_proj_loss(act, w_down_full, x1, target, tm=tm, tn=512)

    d_act = _matmul(dy_b, w_down_full, "nt", tm=tall, tn=D_FF // 4, tk=D_MODEL, out_dtype=BF16, name="d_act")
    dw_down = _matmul(act, dy_b, "tn", tm=D_FF // 4, tn=512, tk=t, name="dw_down")
    if rs is not None:
        d_act = rs.push("w_down", [dw_down.reshape(N_DEV, D_FF // N_DEV, D_MODEL)], [(64, D_MODEL)], d_act)
    du, dcw_g, dcw_v, dcb_g, dcb_v = _ffn_mid_bwd(u, cw_g, cw_v, cb_g, cb_v, d_act)
    if rs is None:
        dw_up8 = _up_proj_dw(h2, du, tm=512, tk=t)
    else:
        dw_up8 = None
        theirs_first = _up_proj_dw_core(h2, du, 1 - rs.core, tm=512, tk=t, name="up_proj_dw_sibling")
        du = rs.push("w_up", [theirs_first], None, du, finish=lambda got, h2=h2, du=du: [
            _up_proj_dw_core(h2, du, rs.core, tm=512, tk=t, res=got[0], name="up_proj_dw_own")])
    dh2 = _up_proj_dx(du, w_up8, tm=tm, tn=1024)
    dx1, dx1_b, d_norm2 = _rmsnorm_bwd(x1, norm2_g, dh2, dy, name="norm2_bwd", also_bf16=True)
    dmix = _matmul(dx1_b, w_out_full, "nt", tm=tall, tn=1024, tk=D_MODEL, name="d_mix")
    dw_out = _out_proj_dw(o_attn, o_gla, dx1_b, tn=512)
    if rs is not None:
        dmix = rs.push("w_out", [dw_out.reshape(N_DEV, D_MODEL // N_DEV, D_MODEL)], [(256, D_MODEL)], dmix)
    do_gla, d_gate, d_gla_norm = _gla_out_bwd(o_f, o_b, proj, gla_norm_g, dmix)
    (dq_f, dk_f, dv_f, dg_f, dq_b, dk_b, dv_b, dg_b) = _gla_bwd(proj, g_dec, s_f, s_b, do_gla)
    d_lr, d_w2, d_ba2 = _gla_prep_bwd(proj_lr, w2, ba2, dg_f, dg_b)
    dqa, dk_lo, dk_mid, dk_hi, dv_lo, dv_mid, dv_hi, d_sink8 = _attn_bwd(qa, ka, va, sink, attn_bias, dmix)
    d_qa, d_ka, d_va, d_qn, d_kn = _attn_prep_bwd(proj, cos, sin_signed, gq, gk, dqa,
                                                  (dk_lo, dk_mid, dk_hi), (dv_lo, dv_mid, dv_hi))
    d_seg = {"qa": d_qa, "gate": d_gate, "vg": (dv_f + dv_b).astype(BF16), "qg": (dq_f + dq_b).astype(BF16),
             "kg": (dk_f + dk_b).astype(BF16), "ka": d_ka, "va": d_va}
    d_proj = jnp.concatenate([d_seg[k] for k in sorted(SEGMENTS, key=lambda k: SEGMENTS[k][0])] + [d_lr], axis=1)
    dw_in_t = _in_proj_dw_lr(_in_proj_dw(d_proj, h1, tn=1024), d_lr, h1)
    if rs is not None:
        per_in = IN_TOTAL // N_DEV
        small_grad = _small_blocks(jnp.concatenate([dcw_g, dcw_v], axis=1), d_w2[:GLA_RANK, :GLA_KEY_WIDTH],
                                   d_w2[GLA_RANK:2 * GLA_RANK, GLA_KEY_WIDTH:])
        d_proj, d_lr = rs.push("w_in", [dw_in_t.reshape(N_DEV, per_in, D_MODEL), small_grad],
                               [(per_in, 512), small_grad.shape[1:]], (d_proj, d_lr), narrow=True)
    dh1 = _matmul(d_proj, w_in_k, "nn", tm=tm, tn=512, tk=IN_KERNEL, name="dh1")
    if rs is not None:
        dh1 = rs.push(None, [], [], dh1)
    grad_x, d_norm1 = _rmsnorm_bwd(xs, norm1_g, dh1, dx1, name="norm1_bwd", also_bf16=False)
    return (loss_part, grad_x, dw_in_t, dw_out, dw_up8, dw_down, dcw_g, dcw_v, dcb_g, dcb_v,
            d_w2, d_ba2, d_norm1, d_norm2, d_qn, d_kn, d_sink8, d_gla_norm)


def kernel(x, norm1_g, w_in, attn_q_norm_g, attn_k_norm_g, attn_sink, gla_wa2_fwd, gla_ba_fwd, gla_wa2_bwd, gla_ba_bwd, gla_out_norm_g, w_out, norm2_g, w_up, conv_w, conv_b, w_down, loss_target, m_norm1_g, m_w_in, m_attn_q_norm_g, m_attn_k_norm_g, m_attn_sink, m_gla_wa2_fwd, m_gla_ba_fwd, m_gla_wa2_bwd, m_gla_ba_bwd, m_gla_out_norm_g, m_w_out, m_norm2_g, m_w_up, m_conv_w, m_conv_b, m_w_down, v_norm1_g, v_w_in, v_attn_q_norm_g, v_attn_k_norm_g, v_attn_sink, v_gla_wa2_fwd, v_gla_ba_fwd, v_gla_wa2_bwd, v_gla_ba_bwd, v_gla_out_norm_g, v_w_out, v_norm2_g, v_w_up, v_conv_w, v_conv_b, v_w_down):
    t = x.shape[1]
    xs = x.reshape(t, D_MODEL)
    target = loss_target.reshape(t, D_MODEL)
    core = lax.axis_index("c").astype(jnp.int32).reshape(1)

    w_small = _small_block(conv_w[0], gla_wa2_fwd[0], gla_wa2_bwd[0])
    w_in_t, m_in_t, v_in_t = (jnp.swapaxes(a[0], 0, 1) for a in (w_in, m_w_in, v_w_in))
    g_in, g_small = _all_gather([w_in_t.astype(BF16), w_small], name="gather_w_in", collective_id=8)
    g_out, w_up8, g_down = _all_gather([w_out[0].astype(BF16), w_up[0].astype(BF16), w_down[0].astype(BF16)],
                                       name="gather_later_weights", collective_id=1)
    w_out_full = g_out.reshape(D_MODEL, D_MODEL)
    w_down_full = g_down.reshape(D_FF, D_MODEL)
    conv_w_full, wa2_f, wa2_b = _small_unblocks(g_small)
    cw_g, cw_v = conv_w_full[:, :D_FF], conv_w_full[:, D_FF:]
    cb_g, cb_v = conv_b[:, :D_FF], conv_b[:, D_FF:]
    w2 = jnp.zeros((LR_PAD, 2 * GLA_KEY_WIDTH), F32)
    w2 = w2.at[:GLA_RANK, :GLA_KEY_WIDTH].set(wa2_f).at[GLA_RANK:2 * GLA_RANK, GLA_KEY_WIDTH:].set(wa2_b)
    ba2 = jnp.concatenate([gla_ba_fwd, gla_ba_bwd], axis=1)
    rs = _ReduceScatter(core)
    (loss_part, grad_x, _, _, _, _, _, _, dcb_g, dcb_v, _, d_ba2,
     d_norm1, d_norm2, d_qn, d_kn, d_sink8, d_gla_norm) = _local_step(
        xs, target, norm1_g, g_in, attn_q_norm_g, attn_k_norm_g, attn_sink, w2, ba2, gla_out_norm_g,
        w_out_full, norm2_g, w_up8, cw_g, cw_v, cb_g, cb_v, w_down_full, rs=rs)

    (part_down,), (part_up,), (part_out,) = rs.result("w_down")[0], rs.result("w_up")[0], rs.result("w_out")[0]
    (part_in, part_small), (own_in, own_small) = rs.result("w_in")
    chip = (2 * lax.axis_index("x") + lax.axis_index("y")).astype(jnp.int32).reshape(1)
    m_small = _small_block(m_conv_w[0], m_gla_wa2_fwd[0], m_gla_wa2_bwd[0])
    v_small = _small_block(v_conv_w[0], v_gla_wa2_fwd[0], v_gla_wa2_bwd[0])
    upd_in = _adamw(part_in, w_in_t, m_in_t, v_in_t, tile=(IN_TOTAL // N_DEV, 512), name="adamw_w_in",
                    own=own_in, chip=chip)
    upd_in = [jnp.swapaxes(u, 0, 1) for u in upd_in]
    upd_out = _adamw(part_out, w_out[0], m_w_out[0], v_w_out[0], tile=(256, D_MODEL), name="adamw_w_out")
    upd_up = _adamw(part_up, w_up[0], m_w_up[0], v_w_up[0], tile=(256, UP_BLOCK), name="adamw_w_up")
    upd_down = _adamw(part_down, w_down[0], m_w_down[0], v_w_down[0], tile=(64, D_MODEL), name="adamw_w_down")
    upd_small = _adamw(part_small, w_small, m_small, v_small, tile=part_small.shape[1:],
                       name="adamw_small", own=own_small, chip=chip)
    upd_small = [_small_unblock(u) for u in upd_small]

    rep_names = ["norm1_g", "attn_q_norm_g", "attn_k_norm_g", "attn_sink", "gla_ba_fwd", "gla_ba_bwd",
                 "gla_out_norm_g", "norm2_g", "conv_b"]
    def whole_lanes(sink_like):
        return jnp.pad(sink_like, ((0, 0), (0, LANES - ATTN_HEADS)))

    rep_w = [norm1_g, attn_q_norm_g, attn_k_norm_g, whole_lanes(attn_sink), gla_ba_fwd, gla_ba_bwd, gla_out_norm_g,
             norm2_g, conv_b]
    rep_m = [m_norm1_g, m_attn_q_norm_g, m_attn_k_norm_g, whole_lanes(m_attn_sink), m_gla_ba_fwd, m_gla_ba_bwd,
             m_gla_out_norm_g, m_norm2_g, m_conv_b]
    rep_v = [v_norm1_g, v_attn_q_norm_g, v_attn_k_norm_g, whole_lanes(v_attn_sink), v_gla_ba_fwd, v_gla_ba_bwd,
             v_gla_out_norm_g, v_norm2_g, v_conv_b]
    d_sink = whole_lanes(d_sink8[:, :GQA_GROUP, 0].reshape(1, ATTN_HEADS))
    rep_g = [d_norm1, d_qn, d_kn, d_sink, d_ba2[:, :GLA_KEY_WIDTH], d_ba2[:, GLA_KEY_WIDTH:], d_gla_norm, d_norm2,
             jnp.concatenate([dcb_g, dcb_v], axis=1)]
    (rep_terms,) = _all_gather([_pack_rows(rep_g + [loss_part])], name="gather_small_grads", collective_id=7)
    upd_rep, loss_row = _adamw_rows(rep_terms, rep_w, rep_m, rep_v)
    sink_at = rep_names.index("attn_sink")
    for kind in range(4):
        upd_rep[kind][sink_at] = upd_rep[kind][sink_at][:, :ATTN_HEADS]
    loss = loss_row[0, 0]

    order = ["norm1_g", "w_in", "attn_q_norm_g", "attn_k_norm_g", "attn_sink", "gla_wa2_fwd", "gla_ba_fwd",
             "gla_wa2_bwd", "gla_ba_bwd", "gla_out_norm_g", "w_out", "norm2_g", "w_up", "conv_w", "conv_b", "w_down"]
    outs = [loss, grad_x.reshape(1, t, D_MODEL)]
    for kind in range(4):
        by_name = {n: upd_rep[kind][i] for i, n in enumerate(rep_names)}
        by_name["w_in"] = upd_in[kind][None]
        by_name["w_out"] = upd_out[kind][None]
        by_name["w_up"] = upd_up[kind][None]
        by_name["w_down"] = upd_down[kind][None]
        by_name["conv_w"] = upd_small[kind][0][None]
        by_name["gla_wa2_fwd"] = upd_small[kind][1][None]
        by_name["gla_wa2_bwd"] = upd_small[kind][2][None]
        outs += [by_name[n] for n in order]
    return tuple(outs)
```

```python
import functools

import jax
import jax.numpy as jnp
from jax import lax
from jax.experimental import pallas as pl
from jax.experimental.pallas import tpu as pltpu
from jax.experimental.pallas import tpu_sc as plsc

F32 = jnp.float32
BF16 = jnp.bfloat16

D_MODEL = 2048
HEAD_DIM = 128
ATTN_WIDTH = 1024
ATTN_HEADS = 8
KV_HEADS = 2
GQA_GROUP = 4
KV_WIDTH = KV_HEADS * HEAD_DIM
ATTN_BLOCK = 128
WINDOW = 128
ROPE_THETA = 10000.0
GLA_HEADS = 4
GLA_DK = 128
GLA_DV = 256
GLA_KEY_WIDTH = 512
GLA_WIDTH = 1024
GLA_RANK = 16
GLA_GATE_NORMALIZER = 16.0
GLA_CHUNK = 64
GLA_PER_STEP = 4
D_FF = 5632
NORM_EPS = 1e-6
IN_TOTAL = 4640
IN_MAIN = 4608
LR_PAD = 128
N_DEV = 8
N_CHIP = 4

ADAM_LR = 0.001
ADAM_B1 = 0.9
ADAM_B2 = 0.999
ADAM_EPS = 1e-08
ADAM_WD = 0.01
ADAM_STEP = 10

SEGMENTS = {
    "qa": (0, 0, 1024),
    "gate": (1024, 3584, 1024),
    "vg": (2048, 2560, 1024),
    "qg": (3072, 1536, 512),
    "kg": (3584, 2048, 512),
    "ka": (4096, 1024, 256),
    "va": (4352, 1280, 256),
}

VMEM_LIMIT = 56 * 1024 * 1024
MESH = pl.DeviceIdType.MESH


def _params(semantics=None, vmem=None):
    return pltpu.CompilerParams(dimension_semantics=semantics, vmem_limit_bytes=vmem)


_DIMS = {
    "nn": (((1,), (0,)), ((), ())),
    "nt": (((1,), (1,)), ((), ())),
    "tn": (((0,), (0,)), ((), ())),
}


def _mxu(a, b, mode):
    return lax.dot_general(a.astype(BF16), b.astype(BF16), _DIMS[mode], preferred_element_type=F32)


@functools.partial(jax.custom_vjp, nondiff_argnums=(2,))
def bdot(a, b, mode):
    return _mxu(a, b, mode)


def _bdot_fwd(a, b, mode):
    return _mxu(a, b, mode), (a, b)


def _bdot_bwd(mode, res, g):
    a, b = res
    if mode == "nn":
        return _mxu(g, b, "nt"), _mxu(a, g, "tn")
    if mode == "nt":
        return _mxu(g, b, "nn"), _mxu(g, a, "tn")
    return _mxu(b, g, "nt"), _mxu(a, g, "nn")


bdot.defvjp(_bdot_fwd, _bdot_bwd)


def _rms(x, g):
    return x * lax.rsqrt(jnp.mean(x * x, axis=-1, keepdims=True) + NORM_EPS) * g


def _rope(x, cos, sin_signed):
    return x * cos + pltpu.roll(x, HEAD_DIM // 2, 1) * sin_signed


def _rope_transposed(d, cos, sin_signed):
    return d * cos + pltpu.roll(d * sin_signed, HEAD_DIM // 2, 1)


def _silu(x):
    return x * jax.nn.sigmoid(x)


def _log_sigmoid(z):
    return -(jnp.maximum(-z, 0.0) + jnp.log(1.0 + jnp.exp(-jnp.abs(z))))


def _matmul_call(args, in_specs, o_spec, out_shape, grid, mode, nk, acc_shape, *, name, has_res=False,
                 prefetch=None, load_b=lambda ref: ref[...]):
    dims = _DIMS[mode]
    out_dtype = out_shape.dtype
    n_pre = 0 if prefetch is None else 1

    def body(*refs):
        refs = refs[n_pre:]
        if has_res:
            a_ref, b_ref, r_ref, o_ref = refs[:4]
            rest = refs[4:]
        else:
            a_ref, b_ref, o_ref = refs[:3]
            r_ref = None
            rest = refs[3:]
        part = lax.dot_general(a_ref[...], load_b(b_ref), dims, preferred_element_type=F32)

        def finish(acc):
            if r_ref is not None:
                acc = acc + r_ref[...]
            o_ref[...] = acc.astype(out_dtype)

        if nk == 1:
            finish(part)
        else:
            acc_ref = rest[0]
            kk = pl.program_id(2)

            @pl.when(kk == 0)
            def _():
                acc_ref[...] = part

            @pl.when(kk > 0)
            def _():
                acc_ref[...] += part

            @pl.when(kk == nk - 1)
            def _():
                finish(acc_ref[...])

    scratch = [pltpu.VMEM(acc_shape, F32)] if nk > 1 else []
    params = _params(("parallel", "parallel", "arbitrary"), VMEM_LIMIT)
    if prefetch is None:
        return pl.pallas_call(body, name=name, grid=grid, in_specs=in_specs, out_specs=o_spec, out_shape=out_shape,
                              scratch_shapes=scratch, compiler_params=params)(*args)
    return pl.pallas_call(
        body, name=name,
        grid_spec=pltpu.PrefetchScalarGridSpec(num_scalar_prefetch=1, grid=grid, in_specs=in_specs,
                                               out_specs=o_spec, scratch_shapes=scratch),
        out_shape=out_shape, compiler_params=params)(prefetch, *args)


def _matmul(a, b, mode, *, tm, tn, tk, out_dtype=F32, res=None, name, n_out=None):
    if mode == "nn":
        (m, k), (k2, n) = a.shape, b.shape
    elif mode == "nt":
        (m, k), (n, k2) = a.shape, b.shape
    else:
        (k, m), (k2, n) = a.shape, b.shape
    n = n if n_out is None else n_out
    assert k == k2 and m % tm == 0 and n % tn == 0 and k % tk == 0, (name, a.shape, b.shape, tm, tn, tk)
    if mode == "tn":
        a_spec = pl.BlockSpec((tk, tm), lambda i, j, kk: (kk, i))
    else:
        a_spec = pl.BlockSpec((tm, tk), lambda i, j, kk: (i, kk))
    if mode == "nt":
        b_spec = pl.BlockSpec((tn, tk), lambda i, j, kk: (j, kk))
    else:
        b_spec = pl.BlockSpec((tk, tn), lambda i, j, kk: (kk, j))
    o_spec = pl.BlockSpec((tm, tn), lambda i, j, kk: (i, j))
    in_specs, args = [a_spec, b_spec], [a, b]
    if res is not None:
        in_specs.append(o_spec)
        args.append(res)
    return _matmul_call(args, in_specs, o_spec, jax.ShapeDtypeStruct((m, n), out_dtype),
                        (m // tm, n // tn, k // tk), mode, k // tk, (tm, tn), name=name, has_res=res is not None)


def _out_proj(o_attn, o_gla, w_out, x, *, tm, tn):
    t, ka = o_attn.shape
    kg = o_gla.shape[1]

    def body(a_ref, g_ref, w_ref, x_ref, o_ref):
        acc = lax.dot_general(a_ref[...], w_ref[:ka], _DIMS["nn"], preferred_element_type=F32)
        acc = acc + lax.dot_general(g_ref[...], w_ref[ka:], _DIMS["nn"], preferred_element_type=F32)
        o_ref[...] = acc + x_ref[...]

    tile = pl.BlockSpec((tm, tn), lambda i, j: (i, j))
    return pl.pallas_call(
        body, name="out_proj", grid=(t // tm, D_MODEL // tn),
        in_specs=[pl.BlockSpec((tm, ka), lambda i, j: (i, 0)), pl.BlockSpec((tm, kg), lambda i, j: (i, 0)),
                  pl.BlockSpec((ka + kg, tn), lambda i, j: (0, j)), tile],
        out_specs=tile, out_shape=jax.ShapeDtypeStruct((t, D_MODEL), F32),
        compiler_params=_params(("parallel", "parallel"), VMEM_LIMIT),
    )(o_attn, o_gla, w_out, x)


def _out_proj_dw(o_attn, o_gla, dx1, *, tn):
    t, ka = o_attn.shape
    assert o_gla.shape == (t, ka)

    def body(a_ref, g_ref, d_ref, o_ref):
        @pl.when(pl.program_id(0) == 0)
        def _():
            o_ref[...] = lax.dot_general(a_ref[...], d_ref[...], _DIMS["tn"], preferred_element_type=F32)

        @pl.when(pl.program_id(0) == 1)
        def _():
            o_ref[...] = lax.dot_general(g_ref[...], d_ref[...], _DIMS["tn"], preferred_element_type=F32)

    whole = pl.BlockSpec((t, ka), lambda i, j: (0, 0))
    return pl.pallas_call(
        body, name="dw_out", grid=(2, D_MODEL // tn),
        in_specs=[whole, whole, pl.BlockSpec((t, tn), lambda i, j: (0, j))],
        out_specs=pl.BlockSpec((ka, tn), lambda i, j: (i, j)),
        out_shape=jax.ShapeDtypeStruct((2 * ka, D_MODEL), F32),
        compiler_params=_params(("parallel", "parallel"), VMEM_LIMIT),
    )(o_attn, o_gla, dx1)


UP_BLOCK = 2 * D_FF // N_DEV


def _up_proj(h2, w_up8, *, tm):
    t = h2.shape[0]
    return _matmul_call(
        [h2, w_up8],
        [pl.BlockSpec((tm, D_MODEL), lambda i, j, kk: (i, 0)),
         pl.BlockSpec((None, D_MODEL, UP_BLOCK), lambda i, j, kk: (j, 0, 0))],
        pl.BlockSpec((None, tm, UP_BLOCK), lambda i, j, kk: (j // N_CHIP, i, j % N_CHIP)),
        jax.ShapeDtypeStruct((2, t, D_FF), F32), (t // tm, N_DEV, 1), "nn", 1, None, name="up_proj")


def _up_proj_dx(du, w_up8, *, tm, tn):
    t = du.shape[1]
    pair = 2
    return _matmul_call(
        [du, w_up8],
        [pl.BlockSpec((None, tm, pair * UP_BLOCK), lambda i, j, kk: (kk // 2, i, kk % 2)),
         pl.BlockSpec((pair, tn, UP_BLOCK), lambda i, j, kk: (kk, j, 0))],
        pl.BlockSpec((tm, tn), lambda i, j, kk: (i, j)),
        jax.ShapeDtypeStruct((t, D_MODEL), F32), (t // tm, D_MODEL // tn, N_DEV // pair), "nt", N_DEV // pair,
        (tm, tn), name="up_proj_dx", load_b=lambda ref: jnp.concatenate([ref[0], ref[1]], axis=1))


def _up_proj_dw(h2, du, *, tm, tk):
    t = h2.shape[0]
    return _matmul_call(
        [h2, du],
        [pl.BlockSpec((tk, tm), lambda j, i, kk: (kk, i)),
         pl.BlockSpec((None, tk, UP_BLOCK), lambda j, i, kk: (j // N_CHIP, kk, j % N_CHIP))],
        pl.BlockSpec((None, tm, UP_BLOCK), lambda j, i, kk: (j, i, 0)),
        jax.ShapeDtypeStruct((N_DEV, D_MODEL, UP_BLOCK), F32), (N_DEV, D_MODEL // tm, t // tk), "tn", t // tk,
        (tm, UP_BLOCK), name="up_proj_dw")


def _up_proj_dw_core(h2, du, core, *, tm, tk, res=None, name):
    t = h2.shape[0]

    def block(j, core_ref):
        return 2 * j + core_ref[0]

    out_spec = pl.BlockSpec((None, tm, UP_BLOCK), lambda j, i, kk, core_ref: (j, i, 0))
    return _matmul_call(
        [h2, du] + ([] if res is None else [res]),
        [pl.BlockSpec((tk, tm), lambda j, i, kk, core_ref: (kk, i)),
         pl.BlockSpec((None, tk, UP_BLOCK),
                      lambda j, i, kk, core_ref: (block(j, core_ref) // N_CHIP, kk, block(j, core_ref) % N_CHIP))]
        + ([] if res is None else [out_spec]),
        out_spec, jax.ShapeDtypeStruct((N_CHIP, D_MODEL, UP_BLOCK), F32), (N_CHIP, D_MODEL // tm, t // tk), "tn",
        t // tk, (tm, UP_BLOCK), name=name, has_res=res is not None, prefetch=core)


IN_TILE = 512


def _in_proj_dw(d_proj, h1, *, tn):
    t = h1.shape[0]
    table = []
    for tile in range(IN_MAIN // IN_TILE):
        dst, src, _ = max(s for s in SEGMENTS.values() if s[0] <= tile * IN_TILE)
        assert (src + tile * IN_TILE - dst) % IN_TILE == 0
        table.append((src + tile * IN_TILE - dst) // IN_TILE)
    assert sorted(table) == list(range(IN_MAIN // IN_TILE))
    return _matmul_call(
        [d_proj, h1],
        [pl.BlockSpec((t, IN_TILE), lambda j, i, kk, tab: (0, i)),
         pl.BlockSpec((t, tn), lambda j, i, kk, tab: (0, j))],
        pl.BlockSpec((IN_TILE, tn), lambda j, i, kk, tab: (tab[i], j)),
        jax.ShapeDtypeStruct((IN_TOTAL, D_MODEL), F32), (D_MODEL // tn, IN_MAIN // IN_TILE, 1), "tn", 1, None,
        name="in_proj_dw", prefetch=jnp.asarray(table, jnp.int32))


def _in_proj_dw_lr(dw_t, d_lr, h1):
    t = h1.shape[0]
    n_lr = IN_TOTAL - IN_MAIN
    tn = 512

    def body(dw_ref, dlr_ref, h1_ref, out_ref):
        full = lax.dot_general(dlr_ref[...], h1_ref[...], _DIMS["tn"], preferred_element_type=F32)
        out_ref[...] = full[:n_lr]

    return pl.pallas_call(
        body, name="in_proj_dw_lr", grid=(D_MODEL // tn,),
        in_specs=[pl.BlockSpec(memory_space=pl.ANY),
                  pl.BlockSpec((t, LR_PAD), lambda j: (0, 0)),
                  pl.BlockSpec((t, tn), lambda j: (0, j))],
        out_specs=pl.BlockSpec((n_lr, tn), lambda j: (IN_MAIN // n_lr, j)),
        out_shape=jax.ShapeDtypeStruct(dw_t.shape, F32),
        input_output_aliases={0: 0},
        compiler_params=_params(("parallel",), VMEM_LIMIT),
    )(dw_t, d_lr, h1)


def _rmsnorm_fwd(x, g, *, name, tr=512):
    t, d = x.shape

    def body(x_ref, g_ref, h_ref):
        h_ref[...] = _rms(x_ref[...], g_ref[...]).astype(BF16)

    return pl.pallas_call(
        body, name=name, grid=(t // tr,),
        in_specs=[pl.BlockSpec((tr, d), lambda i: (i, 0)), pl.BlockSpec((1, d), lambda i: (0, 0))],
        out_specs=pl.BlockSpec((tr, d), lambda i: (i, 0)),
        out_shape=jax.ShapeDtypeStruct((t, d), BF16),
        compiler_params=_params(("parallel",), VMEM_LIMIT),
    )(x, g)


def _rmsnorm_bwd(x, g, dh, dres, *, name, also_bf16, tr=512):
    t, d = x.shape

    def body(x_ref, g_ref, dh_ref, dres_ref, dx_ref, *rest):
        dg_ref = rest[-1]
        _, vjp = jax.vjp(_rms, x_ref[...], g_ref[...])
        dx, dg = vjp(dh_ref[...])
        dx = dx + dres_ref[...]
        dx_ref[...] = dx
        if also_bf16:
            rest[0][...] = dx.astype(BF16)

        @pl.when(pl.program_id(0) == 0)
        def _():
            dg_ref[...] = jnp.zeros_like(dg_ref)

        dg_ref[...] += dg

    row = pl.BlockSpec((tr, d), lambda i: (i, 0))
    vec = pl.BlockSpec((1, d), lambda i: (0, 0))
    return pl.pallas_call(
        body, name=name, grid=(t // tr,),
        in_specs=[row, vec, row, row],
        out_specs=[row] + [row] * also_bf16 + [vec],
        out_shape=[jax.ShapeDtypeStruct((t, d), F32)] + [jax.ShapeDtypeStruct((t, d), BF16)] * also_bf16
                  + [jax.ShapeDtypeStruct((1, d), F32)],
        compiler_params=_params(("arbitrary",), VMEM_LIMIT),
    )(x, g, dh, dres)


def _seg_block(name, width):
    off = SEGMENTS[name][0]
    assert off % width == 0
    return off // width


def _attn_prep_fwd(proj, cos, sin_signed, gq, gk, *, tr=512):
    t = proj.shape[0]

    def body(q_ref, k_ref, v_ref, cos_ref, sin_ref, gq_ref, gk_ref, qo_ref, ko_ref, vo_ref):
        cos_t, sin_t = cos_ref[...], sin_ref[...]
        for h in range(ATTN_HEADS):
            cols = slice(h * HEAD_DIM, (h + 1) * HEAD_DIM)
            qo_ref[:, cols] = _rope(_rms(q_ref[:, cols], gq_ref[...]), cos_t, sin_t).astype(BF16)
        for h in range(KV_HEADS):
            cols = slice(h * HEAD_DIM, (h + 1) * HEAD_DIM)
            ko_ref[:, cols] = _rope(_rms(k_ref[:, cols], gk_ref[...]), cos_t, sin_t).astype(BF16)
        vo_ref[...] = v_ref[...].astype(BF16)

    qb, kb, vb = _seg_block("qa", ATTN_WIDTH), _seg_block("ka", KV_WIDTH), _seg_block("va", KV_WIDTH)
    tab = pl.BlockSpec((tr, HEAD_DIM), lambda i: (i, 0))
    vec = pl.BlockSpec((1, HEAD_DIM), lambda i: (0, 0))
    return pl.pallas_call(
        body, name="attn_prep_fwd", grid=(t // tr,),
        in_specs=[pl.BlockSpec((tr, ATTN_WIDTH), lambda i: (i, qb)),
                  pl.BlockSpec((tr, KV_WIDTH), lambda i: (i, kb)),
                  pl.BlockSpec((tr, KV_WIDTH), lambda i: (i, vb)),
                  tab, tab, vec, vec],
        out_specs=[pl.BlockSpec((tr, ATTN_WIDTH), lambda i: (i, 0)),
                   pl.BlockSpec((tr, KV_WIDTH), lambda i: (i, 0)),
                   pl.BlockSpec((tr, KV_WIDTH), lambda i: (i, 0))],
        out_shape=[jax.ShapeDtypeStruct((t, ATTN_WIDTH), BF16),
                   jax.ShapeDtypeStruct((t, KV_WIDTH), BF16),
                   jax.ShapeDtypeStruct((t, KV_WIDTH), BF16)],
        compiler_params=_params(("parallel",), VMEM_LIMIT),
    )(proj, proj, proj, cos, sin_signed, gq, gk)


def _attn_heads(q, kcat, vcat, sink_col, bias):
    s = bdot(q, kcat, "nt") * (HEAD_DIM ** -0.5) + bias
    m = lax.stop_gradient(jnp.maximum(jnp.max(s, axis=-1, keepdims=True), sink_col))
    p = jnp.exp(s - m)
    p = p / (jnp.sum(p, axis=-1, keepdims=True) + jnp.exp(sink_col - m))
    return bdot(p, vcat, "nn")


def _attn_bias():
    rows, cols = GQA_GROUP * ATTN_BLOCK, 3 * ATTN_BLOCK
    qi = lax.broadcasted_iota(jnp.int32, (3, rows, cols), 1) % ATTN_BLOCK
    sj = lax.broadcasted_iota(jnp.int32, (3, rows, cols), 2)
    kind = lax.broadcasted_iota(jnp.int32, (3, rows, cols), 0)
    inside = jnp.where(kind == 0, sj >= ATTN_BLOCK, jnp.where(kind == 2, sj < 2 * ATTN_BLOCK, True))
    valid = (jnp.abs(sj - ATTN_BLOCK - qi) <= WINDOW) & inside
    return jnp.where(valid, 0.0, -jnp.inf).astype(F32)


def _attn_bias_spec(nb, block_of):
    def index(*ix):
        n = block_of(*ix)
        return (jnp.where(n == 0, 0, jnp.where(n == nb - 1, 2, 1)), 0, 0)
    return pl.BlockSpec((None, GQA_GROUP * ATTN_BLOCK, 3 * ATTN_BLOCK), index)


def _head_rows(g):
    return slice(g * ATTN_BLOCK, (g + 1) * ATTN_BLOCK)


def _head_cols(g):
    return slice(g * HEAD_DIM, (g + 1) * HEAD_DIM)


def _stack_heads(ref):
    return jnp.concatenate([ref[:, _head_cols(g)] for g in range(GQA_GROUP)], axis=0).astype(F32)


def _sink_column(sink_ref, h):
    return jnp.concatenate([jnp.full((ATTN_BLOCK, 1), sink_ref[h * GQA_GROUP + g], F32)
                            for g in range(GQA_GROUP)], axis=0)


def _attn_specs(nb):
    q_spec = pl.BlockSpec((ATTN_BLOCK, GQA_GROUP * HEAD_DIM), lambda h, n: (n, h))
    kv_specs = [
        pl.BlockSpec((ATTN_BLOCK, HEAD_DIM), lambda h, n: (jnp.maximum(n - 1, 0), h)),
        pl.BlockSpec((ATTN_BLOCK, HEAD_DIM), lambda h, n: (n, h)),
        pl.BlockSpec((ATTN_BLOCK, HEAD_DIM), lambda h, n: (jnp.minimum(n + 1, nb - 1), h)),
    ]
    return q_spec, kv_specs


def _attn_fwd(q, k, v, sink, bias):
    t = q.shape[0]
    nb = t // ATTN_BLOCK

    def body(sink_ref, bias_ref, q_ref, kp_ref, kc_ref, kn_ref, vp_ref, vc_ref, vn_ref, o_ref):
        bias = bias_ref[...]
        args = []
        for h in range(KV_HEADS):
            q = jnp.concatenate([q_ref[:, _head_cols(h * GQA_GROUP + g)] for g in range(GQA_GROUP)], axis=0)
            kcat = jnp.concatenate([r[:, _head_cols(h)] for r in (kp_ref, kc_ref, kn_ref)], axis=0)
            vcat = jnp.concatenate([r[:, _head_cols(h)] for r in (vp_ref, vc_ref, vn_ref)], axis=0)
            args.append((q.astype(F32), kcat.astype(F32), vcat.astype(F32), _sink_column(sink_ref, h)))
        outs = [_attn_heads(*a, bias=bias).astype(BF16) for a in args]
        for h, o in enumerate(outs):
            for g in range(GQA_GROUP):
                o_ref[:, _head_cols(h * GQA_GROUP + g)] = o[_head_rows(g)]

    q_spec = pl.BlockSpec((ATTN_BLOCK, ATTN_WIDTH), lambda n: (n, 0))
    kv_specs = [pl.BlockSpec((ATTN_BLOCK, KV_WIDTH), lambda n: (jnp.maximum(n - 1, 0), 0)),
                pl.BlockSpec((ATTN_BLOCK, KV_WIDTH), lambda n: (n, 0)),
                pl.BlockSpec((ATTN_BLOCK, KV_WIDTH), lambda n: (jnp.minimum(n + 1, nb - 1), 0))]
    return pl.pallas_call(
        body, name="attn_fwd", grid=(nb,),
        in_specs=[pl.BlockSpec(memory_space=pltpu.SMEM), _attn_bias_spec(nb, lambda n: n), q_spec] + kv_specs + kv_specs,
        out_specs=q_spec,
        out_shape=jax.ShapeDtypeStruct((t, ATTN_WIDTH), BF16),
        compiler_params=_params(("parallel",), VMEM_LIMIT),
    )(sink, bias, q, k, k, k, v, v, v)


def _attn_bwd(q, k, v, sink, bias, dmix):
    t = q.shape[0]
    nb = t // ATTN_BLOCK

    def body(sink_ref, bias_ref, q_ref, kp_ref, kc_ref, kn_ref, vp_ref, vc_ref, vn_ref, do_ref,
             dq_ref, dk_lo, dk_mid, dk_hi, dv_lo, dv_mid, dv_hi, dsink_ref):
        h, n = pl.program_id(0), pl.program_id(1)
        kcat = jnp.concatenate([kp_ref[...], kc_ref[...], kn_ref[...]], axis=0).astype(F32)
        vcat = jnp.concatenate([vp_ref[...], vc_ref[...], vn_ref[...]], axis=0).astype(F32)
        _, vjp = jax.vjp(functools.partial(_attn_heads, bias=bias_ref[...]),
                         _stack_heads(q_ref), kcat, vcat, _sink_column(sink_ref, h))
        dq, dk, dv, dsink_col = vjp(_stack_heads(do_ref))
        row = lax.broadcasted_iota(jnp.int32, (8, HEAD_DIM), 0)
        dsink = jnp.zeros((8, HEAD_DIM), F32)
        for g in range(GQA_GROUP):
            dq_ref[:, _head_cols(g)] = dq[_head_rows(g)]
            dsink = dsink + jnp.where(row == g, jnp.sum(dsink_col[_head_rows(g)]), 0.0)
        for i, (dk_ref, dv_ref) in enumerate(((dk_lo, dv_lo), (dk_mid, dv_mid), (dk_hi, dv_hi))):
            rows = slice(i * ATTN_BLOCK, (i + 1) * ATTN_BLOCK)
            dk_ref[...] = dk[rows]
            dv_ref[...] = dv[rows]

        @pl.when(n == 0)
        def _():
            dsink_ref[...] = jnp.zeros_like(dsink_ref)

        dsink_ref[...] += dsink

    q_spec, kv_specs = _attn_specs(nb)
    kv_out = pl.BlockSpec((ATTN_BLOCK, HEAD_DIM), lambda h, n: (n, h))
    kv_shape = jax.ShapeDtypeStruct((t, KV_WIDTH), F32)
    return pl.pallas_call(
        body, name="attn_bwd", grid=(KV_HEADS, nb),
        in_specs=[pl.BlockSpec(memory_space=pltpu.SMEM), _attn_bias_spec(nb, lambda h, n: n), q_spec]
                 + kv_specs + kv_specs + [q_spec],
        out_specs=[q_spec] + [kv_out] * 6 + [pl.BlockSpec((None, 8, HEAD_DIM), lambda h, n: (h, 0, 0))],
        out_shape=[jax.ShapeDtypeStruct((t, ATTN_WIDTH), F32)] + [kv_shape] * 6
                  + [jax.ShapeDtypeStruct((KV_HEADS, 8, HEAD_DIM), F32)],
        compiler_params=_params(("parallel", "arbitrary"), VMEM_LIMIT),
    )(sink, bias, q, k, k, k, v, v, v, dmix)


def _attn_prep_bwd(proj, cos, sin_signed, gq, gk, dq, dks, dvs):
    t = proj.shape[0]
    tr = ATTN_BLOCK
    nb = t // tr

    def body(q_ref, k_ref, cos_ref, sin_ref, gq_ref, gk_ref, dq_ref,
             dk_lo, dk_mid, dk_hi, dv_lo, dv_mid, dv_hi,
             dqo_ref, dko_ref, dvo_ref, dgq_ref, dgk_ref):
        n = pl.program_id(0)
        cos_t, sin_t = cos_ref[...], sin_ref[...]
        has_next = (n < nb - 1).astype(F32)
        has_prev = (n > 0).astype(F32)
        dk = dk_lo[...] * has_next + dk_mid[...] + dk_hi[...] * has_prev
        dv = dv_lo[...] * has_next + dv_mid[...] + dv_hi[...] * has_prev
        dvo_ref[...] = dv.astype(BF16)
        dgq = jnp.zeros((1, HEAD_DIM), F32)
        dgk = jnp.zeros((1, HEAD_DIM), F32)
        for h in range(ATTN_HEADS):
            cols = slice(h * HEAD_DIM, (h + 1) * HEAD_DIM)
            _, vjp = jax.vjp(_rms, q_ref[:, cols], gq_ref[...])
            dx, dg = vjp(_rope_transposed(dq_ref[:, cols], cos_t, sin_t))
            dqo_ref[:, cols] = dx.astype(BF16)
            dgq = dgq + dg
        for h in range(KV_HEADS):
            cols = slice(h * HEAD_DIM, (h + 1) * HEAD_DIM)
            _, vjp = jax.vjp(_rms, k_ref[:, cols], gk_ref[...])
            dx, dg = vjp(_rope_transposed(dk[:, cols], cos_t, sin_t))
            dko_ref[:, cols] = dx.astype(BF16)
            dgk = dgk + dg

        @pl.when(n == 0)
        def _():
            dgq_ref[...] = jnp.zeros_like(dgq_ref)
            dgk_ref[...] = jnp.zeros_like(dgk_ref)

        dgq_ref[...] += dgq
        dgk_ref[...] += dgk

    qb, kb = _seg_block("qa", ATTN_WIDTH), _seg_block("ka", KV_WIDTH)
    tab = pl.BlockSpec((tr, HEAD_DIM), lambda i: (i, 0))
    vec = pl.BlockSpec((1, HEAD_DIM), lambda i: (0, 0))
    kv = [pl.BlockSpec((tr, KV_WIDTH), lambda i: (jnp.minimum(i + 1, nb - 1), 0)),
          pl.BlockSpec((tr, KV_WIDTH), lambda i: (i, 0)),
          pl.BlockSpec((tr, KV_WIDTH), lambda i: (jnp.maximum(i - 1, 0), 0))]
    wide = pl.BlockSpec((tr, ATTN_WIDTH), lambda i: (i, 0))
    narrow = pl.BlockSpec((tr, KV_WIDTH), lambda i: (i, 0))
    return pl.pallas_call(
        body, name="attn_prep_bwd", grid=(nb,),
        in_specs=[pl.BlockSpec((tr, ATTN_WIDTH), lambda i: (i, qb)),
                  pl.BlockSpec((tr, KV_WIDTH), lambda i: (i, kb)),
                  tab, tab, vec, vec, wide] + kv + kv,
        out_specs=[wide, narrow, narrow, vec, vec],
        out_shape=[jax.ShapeDtypeStruct((t, ATTN_WIDTH), BF16),
                   jax.ShapeDtypeStruct((t, KV_WIDTH), BF16),
                   jax.ShapeDtypeStruct((t, KV_WIDTH), BF16),
                   jax.ShapeDtypeStruct((1, HEAD_DIM), F32),
                   jax.ShapeDtypeStruct((1, HEAD_DIM), F32)],
        compiler_params=_params(("arbitrary",), VMEM_LIMIT),
    )(proj, proj, cos, sin_signed, gq, gk, dq, *dks, *dvs)


def _decay_fn(lr, w2, ba):
    return _log_sigmoid(bdot(lr, w2, "nn") + ba) / GLA_GATE_NORMALIZER


def _gla_prep_fwd(proj_lr, w2, ba2, *, tr=512):
    t = proj_lr.shape[0]
    width = 2 * GLA_KEY_WIDTH

    def body(lr_ref, w2_ref, ba_ref, g_ref):
        g_ref[...] = _decay_fn(lr_ref[...], w2_ref[...], ba_ref[...])

    return pl.pallas_call(
        body, name="gla_prep_fwd", grid=(t // tr,),
        in_specs=[pl.BlockSpec((tr, LR_PAD), lambda i: (i, 0)),
                  pl.BlockSpec((LR_PAD, width), lambda i: (0, 0)),
                  pl.BlockSpec((1, width), lambda i: (0, 0))],
        out_specs=pl.BlockSpec((tr, width), lambda i: (i, 0)),
        out_shape=jax.ShapeDtypeStruct((t, width), F32),
        compiler_params=_params(("parallel",), VMEM_LIMIT),
    )(proj_lr, w2, ba2)


def _gla_prep_bwd(proj_lr, w2, ba2, dg_f, dg_b, *, tr=512):
    t = proj_lr.shape[0]
    width = 2 * GLA_KEY_WIDTH

    def body(lr_ref, w2_ref, ba_ref, dgf_ref, dgb_ref, dlr_ref, dw2_ref, dba_ref):
        _, vjp = jax.vjp(_decay_fn, lr_ref[...], w2_ref[...], ba_ref[...])
        dlr, dw2, dba = vjp(jnp.concatenate([dgf_ref[...], dgb_ref[...]], axis=1))
        dlr_ref[...] = dlr.astype(BF16)

        @pl.when(pl.program_id(0) == 0)
        def _():
            dw2_ref[...] = jnp.zeros_like(dw2_ref)
            dba_ref[...] = jnp.zeros_like(dba_ref)

        dw2_ref[...] += dw2
        dba_ref[...] += dba

    half = pl.BlockSpec((tr, GLA_KEY_WIDTH), lambda i: (i, 0))
    return pl.pallas_call(
        body, name="gla_prep_bwd", grid=(t // tr,),
        in_specs=[pl.BlockSpec((tr, LR_PAD), lambda i: (i, 0)),
                  pl.BlockSpec((LR_PAD, width), lambda i: (0, 0)),
                  pl.BlockSpec((1, width), lambda i: (0, 0)), half, half],
        out_specs=[pl.BlockSpec((tr, LR_PAD), lambda i: (i, 0)),
                   pl.BlockSpec((LR_PAD, width), lambda i: (0, 0)),
                   pl.BlockSpec((1, width), lambda i: (0, 0))],
        out_shape=[jax.ShapeDtypeStruct((t, LR_PAD), BF16),
                   jax.ShapeDtypeStruct((LR_PAD, width), F32),
                   jax.ShapeDtypeStruct((1, width), F32)],
        compiler_params=_params(("arbitrary",), VMEM_LIMIT),
    )(proj_lr, w2, ba2, dg_f, dg_b)


def _gla_k(h):
    return slice(h * GLA_DK, (h + 1) * GLA_DK)


def _gla_v(h):
    return slice(h * GLA_DV, (h + 1) * GLA_DV)


def _running_sum(x, downward):
    n = x.shape[0]
    row = lax.broadcasted_iota(jnp.int32, x.shape, 0)
    step = 1
    while step < n:
        if downward:
            x = x + jnp.where(row >= step, pltpu.roll(x, step, 0), 0.0)
        else:
            x = x + jnp.where(row < n - step, pltpu.roll(x, n - step, 0), 0.0)
        step *= 2
    return x


@functools.partial(jax.custom_vjp, nondiff_argnums=(1,))
def _cumsum_rows(x, downward):
    return _running_sum(x, downward)


def _cumsum_rows_fwd(x, downward):
    return _running_sum(x, downward), None


def _cumsum_rows_bwd(downward, _, ct):
    return (_running_sum(ct, not downward),)


_cumsum_rows.defvjp(_cumsum_rows_fwd, _cumsum_rows_bwd)


def _gla_chunk(q, k, v, g, state, forward):
    c = GLA_CHUNK
    row = lax.broadcasted_iota(jnp.int32, (c, c), 0)
    col = lax.broadcasted_iota(jnp.int32, (c, c), 1)
    rid = lax.broadcasted_iota(jnp.int32, (c, GLA_DK), 0)
    q = q * (GLA_DK ** -0.5)
    if forward:
        see = row >= col
        upto_ref = rid <= c // 2
    else:
        see = row < col
        upto_ref = rid >= c - 1 - c // 2
    b = _cumsum_rows(g, forward)
    b_last = jnp.sum(g, axis=0, keepdims=True)
    b_ref = jnp.sum(jnp.where(upto_ref, g, 0.0), axis=0, keepdims=True)
    a = bdot(q * jnp.exp(b - b_ref), k * jnp.exp(b_ref - b), "nt")
    a = jnp.where(see, a, 0.0)
    o = bdot(a, v, "nn") + bdot(q * jnp.exp(b), state, "nt")
    new_state = state * jnp.exp(b_last) + bdot(v, k * jnp.exp(b_last - b), "tn")
    return o, new_state


def _gla_fwd(proj, g):
    t = proj.shape[0]
    c, per = GLA_CHUNK, GLA_PER_STEP
    nchunk = t // c
    nstep = nchunk // per
    qb, kb, vb = _seg_block("qg", GLA_KEY_WIDTH), _seg_block("kg", GLA_KEY_WIDTH), _seg_block("vg", GLA_WIDTH)

    def body(qf, kf, vf, gf, qr, kr, vr, gr, of_ref, ob_ref, sf_ref, sb_ref, state):
        @pl.when(pl.program_id(0) == 0)
        def _():
            state[...] = jnp.zeros_like(state)

        dirs = ((qf, kf, vf, gf, of_ref, sf_ref), (qr, kr, vr, gr, ob_ref, sb_ref))
        states = [[state[d, h] for h in range(GLA_HEADS)] for d in range(2)]
        for turn in range(per):
            chunk = (turn, per - 1 - turn)
            args = []
            for d, (q_ref, k_ref, v_ref, g_ref, _, _) in enumerate(dirs):
                rows = pl.ds(chunk[d] * c, c)
                args += [(q_ref[rows, _gla_k(h)], k_ref[rows, _gla_k(h)], v_ref[rows, _gla_v(h)],
                          g_ref[rows, _gla_k(h)], states[d][h]) for h in range(GLA_HEADS)]
            results = [_gla_chunk(*a, forward=(i < GLA_HEADS)) for i, a in enumerate(args)]
            for i, (a, (o, s_out)) in enumerate(zip(args, results)):
                d, h = divmod(i, GLA_HEADS)
                dirs[d][5][chunk[d], h] = a[4]
                dirs[d][4][pl.ds(chunk[d] * c, c), _gla_v(h)] = o
                states[d][h] = s_out
        for d in range(2):
            for h in range(GLA_HEADS):
                state[d, h] = states[d][h]

    specs, outs = [], []
    for d in range(2):
        ci = (lambda i: i) if d == 0 else (lambda i: nstep - 1 - i)
        specs += [pl.BlockSpec((per * c, GLA_KEY_WIDTH), lambda i, ci=ci: (ci(i), qb)),
                  pl.BlockSpec((per * c, GLA_KEY_WIDTH), lambda i, ci=ci: (ci(i), kb)),
                  pl.BlockSpec((per * c, GLA_WIDTH), lambda i, ci=ci: (ci(i), vb)),
                  pl.BlockSpec((per * c, GLA_KEY_WIDTH), lambda i, ci=ci, d=d: (ci(i), d))]
        outs.append(pl.BlockSpec((per * c, GLA_WIDTH), lambda i, ci=ci: (ci(i), 0)))
    for d in range(2):
        ci = (lambda i: i) if d == 0 else (lambda i: nstep - 1 - i)
        outs.append(pl.BlockSpec((per, GLA_HEADS, GLA_DV, GLA_DK), lambda i, ci=ci: (ci(i), 0, 0, 0)))
    o_shape = jax.ShapeDtypeStruct((t, GLA_WIDTH), F32)
    s_shape = jax.ShapeDtypeStruct((nchunk, GLA_HEADS, GLA_DV, GLA_DK), F32)
    return pl.pallas_call(
        body, name="gla_fwd", grid=(nstep,),
        in_specs=specs, out_specs=outs,
        out_shape=[o_shape, o_shape, s_shape, s_shape],
        scratch_shapes=[pltpu.VMEM((2, GLA_HEADS, GLA_DV, GLA_DK), F32)],
        compiler_params=_params(("arbitrary",), VMEM_LIMIT),
    )(proj, proj, proj, g, proj, proj, proj, g)


def _gla_bwd(proj, g, s_f, s_b, do):
    t = proj.shape[0]
    c, per = GLA_CHUNK, GLA_PER_STEP
    nchunk = t // c
    nstep = nchunk // per
    qb, kb, vb = _seg_block("qg", GLA_KEY_WIDTH), _seg_block("kg", GLA_KEY_WIDTH), _seg_block("vg", GLA_WIDTH)

    def body(*refs):
        ins, outs, dstate = refs[:12], refs[12:20], refs[20]

        @pl.when(pl.program_id(0) == 0)
        def _():
            dstate[...] = jnp.zeros_like(dstate)

        dstates = [[dstate[d, h] for h in range(GLA_HEADS)] for d in range(2)]
        for turn in range(per):
            chunk = (per - 1 - turn, turn)
            loaded = []
            for d in range(2):
                q_ref, k_ref, v_ref, g_ref, s_ref, do_ref = ins[6 * d:6 * d + 6]
                rows = pl.ds(chunk[d] * c, c)
                for h in range(GLA_HEADS):
                    loaded.append(((q_ref[rows, _gla_k(h)], k_ref[rows, _gla_k(h)], v_ref[rows, _gla_v(h)],
                                    g_ref[rows, _gla_k(h)], s_ref[chunk[d], h]),
                                   (do_ref[rows, _gla_v(h)], dstates[d][h])))
            grads = []
            for i, (primals, cotangents) in enumerate(loaded):
                _, vjp = jax.vjp(functools.partial(_gla_chunk, forward=(i < GLA_HEADS)), *primals)
                grads.append(vjp(cotangents))
            for i, (dq, dk, dv, dg, ds) in enumerate(grads):
                d, h = divmod(i, GLA_HEADS)
                rows = pl.ds(chunk[d] * c, c)
                dq_ref, dk_ref, dv_ref, dg_ref = outs[4 * d:4 * d + 4]
                dq_ref[rows, _gla_k(h)] = dq
                dk_ref[rows, _gla_k(h)] = dk
                dv_ref[rows, _gla_v(h)] = dv
                dg_ref[rows, _gla_k(h)] = dg
                dstates[d][h] = ds
        for d in range(2):
            for h in range(GLA_HEADS):
                dstate[d, h] = dstates[d][h]

    specs, outs, shapes = [], [], []
    for d in range(2):
        ci = (lambda i: nstep - 1 - i) if d == 0 else (lambda i: i)
        specs += [pl.BlockSpec((per * c, GLA_KEY_WIDTH), lambda i, ci=ci: (ci(i), qb)),
                  pl.BlockSpec((per * c, GLA_KEY_WIDTH), lambda i, ci=ci: (ci(i), kb)),
                  pl.BlockSpec((per * c, GLA_WIDTH), lambda i, ci=ci: (ci(i), vb)),
                  pl.BlockSpec((per * c, GLA_KEY_WIDTH), lambda i, ci=ci, d=d: (ci(i), d)),
                  pl.BlockSpec((per, GLA_HEADS, GLA_DV, GLA_DK), lambda i, ci=ci: (ci(i), 0, 0, 0)),
                  pl.BlockSpec((per * c, GLA_WIDTH), lambda i, ci=ci: (ci(i), 0))]
        key = pl.BlockSpec((per * c, GLA_KEY_WIDTH), lambda i, ci=ci: (ci(i), 0))
        val = pl.BlockSpec((per * c, GLA_WIDTH), lambda i, ci=ci: (ci(i), 0))
        outs += [key, key, val, key]
        shapes += [jax.ShapeDtypeStruct((t, GLA_KEY_WIDTH), F32), jax.ShapeDtypeStruct((t, GLA_KEY_WIDTH), F32),
                   jax.ShapeDtypeStruct((t, GLA_WIDTH), F32), jax.ShapeDtypeStruct((t, GLA_KEY_WIDTH), F32)]
    return pl.pallas_call(
        body, name="gla_bwd", grid=(nstep,),
        in_specs=specs, out_specs=outs, out_shape=shapes,
        scratch_shapes=[pltpu.VMEM((2, GLA_HEADS, GLA_DV, GLA_DK), F32)],
        compiler_params=_params(("arbitrary",), VMEM_LIMIT),
    )(proj, proj, proj, g, s_f, do, proj, proj, proj, g, s_b, do)


def _gla_out_head(o_f, o_b, gate, gn):
    return _rms(o_f + o_b, gn) * _silu(gate)


def _gla_out_fwd(o_f, o_b, proj, gn, *, tr=512):
    t = o_f.shape[0]
    gb = _seg_block("gate", GLA_WIDTH)

    def body(of_ref, ob_ref, gate_ref, gn_ref, out_ref):
        for h in range(GLA_HEADS):
            vc = slice(h * GLA_DV, (h + 1) * GLA_DV)
            out_ref[:, vc] = _gla_out_head(of_ref[:, vc], ob_ref[:, vc], gate_ref[:, vc], gn_ref[...]).astype(BF16)

    wide = pl.BlockSpec((tr, GLA_WIDTH), lambda i: (i, 0))
    return pl.pallas_call(
        body, name="gla_out_fwd", grid=(t // tr,),
        in_specs=[wide, wide, pl.BlockSpec((tr, GLA_WIDTH), lambda i: (i, gb)),
                  pl.BlockSpec((1, GLA_DV), lambda i: (0, 0))],
        out_specs=wide,
        out_shape=jax.ShapeDtypeStruct((t, GLA_WIDTH), BF16),
        compiler_params=_params(("parallel",), VMEM_LIMIT),
    )(o_f, o_b, proj, gn)


def _gla_out_bwd(o_f, o_b, proj, gn, dmix, *, tr=512):
    t = o_f.shape[0]
    gb = _seg_block("gate", GLA_WIDTH)

    def body(of_ref, ob_ref, gate_ref, gn_ref, dout_ref, do_ref, dgate_ref, dgn_ref):
        dgn = jnp.zeros((1, GLA_DV), F32)
        for h in range(GLA_HEADS):
            vc = slice(h * GLA_DV, (h + 1) * GLA_DV)
            _, vjp = jax.vjp(_gla_out_head, of_ref[:, vc], ob_ref[:, vc], gate_ref[:, vc], gn_ref[...])
            do, _, dgate, dg = vjp(dout_ref[:, vc])
            do_ref[:, vc] = do
            dgate_ref[:, vc] = dgate.astype(BF16)
            dgn = dgn + dg

        @pl.when(pl.program_id(0) == 0)
        def _():
            dgn_ref[...] = jnp.zeros_like(dgn_ref)

        dgn_ref[...] += dgn

    wide = pl.BlockSpec((tr, GLA_WIDTH), lambda i: (i, 0))
    vec = pl.BlockSpec((1, GLA_DV), lambda i: (0, 0))
    return pl.pallas_call(
        body, name="gla_out_bwd", grid=(t // tr,),
        in_specs=[wide, wide, pl.BlockSpec((tr, GLA_WIDTH), lambda i: (i, gb)), vec,
                  pl.BlockSpec((tr, GLA_WIDTH), lambda i: (i, 1))],
        out_specs=[wide, wide, vec],
        out_shape=[jax.ShapeDtypeStruct((t, GLA_WIDTH), F32), jax.ShapeDtypeStruct((t, GLA_WIDTH), BF16),
                   jax.ShapeDtypeStruct((1, GLA_DV), F32)],
        compiler_params=_params(("arbitrary",), VMEM_LIMIT),
    )(o_f, o_b, proj, gn, dmix)


CONV_TR = 1024
CONV_TC = 512
HALO = 8
HALO16 = 16


def _conv3(u, w, b):
    n = u.shape[0]
    return pltpu.roll(u, 1, 0) * w[0:1] + u * w[1:2] + pltpu.roll(u, n - 1, 0) * w[2:3] + b


def _conv_ext(main_ref, prev_ref, next_ref, r, nr):
    prev = prev_ref[...].astype(F32)[-HALO:] * (r > 0).astype(F32)
    nxt = next_ref[...].astype(F32)[:HALO] * (r < nr - 1).astype(F32)
    return jnp.concatenate([prev, main_ref[...].astype(F32), nxt], axis=0)


def _conv_specs(t, halo, half=None):
    per = CONV_TR // halo
    last = t // halo - 1
    lead = () if half is None else (None,)
    at = (lambda *ix: ix) if half is None else (lambda *ix: (half,) + ix)
    return [pl.BlockSpec(lead + (CONV_TR, CONV_TC), lambda j, r: at(r, j)),
            pl.BlockSpec(lead + (halo, CONV_TC), lambda j, r: at(jnp.maximum(r * per - 1, 0), j)),
            pl.BlockSpec(lead + (halo, CONV_TC), lambda j, r: at(jnp.minimum((r + 1) * per, last), j))]


def _ffn_mid_fwd(u, cw_g, cw_v, cb_g, cb_v):
    _, t, f = u.shape
    nr = t // CONV_TR

    def body(ug, ugp, ugn, uv, uvp, uvn, wg, wv, bg, bv, a_ref):
        r = pl.program_id(1)
        gate = _conv3(_conv_ext(ug, ugp, ugn, r, nr), wg[...], bg[...])[HALO:HALO + CONV_TR]
        val = _conv3(_conv_ext(uv, uvp, uvn, r, nr), wv[...], bv[...])[HALO:HALO + CONV_TR]
        a_ref[...] = (_silu(gate) * val).astype(BF16)

    w_spec = pl.BlockSpec((3, CONV_TC), lambda j, r: (0, j))
    b_spec = pl.BlockSpec((1, CONV_TC), lambda j, r: (0, j))
    return pl.pallas_call(
        body, name="ffn_mid_fwd", grid=(f // CONV_TC, nr),
        in_specs=_conv_specs(t, HALO, 0) + _conv_specs(t, HALO, 1) + [w_spec, w_spec, b_spec, b_spec],
        out_specs=pl.BlockSpec((CONV_TR, CONV_TC), lambda j, r: (r, j)),
        out_shape=jax.ShapeDtypeStruct((t, f), BF16),
        compiler_params=_params(("parallel", "parallel"), VMEM_LIMIT),
    )(u, u, u, u, u, u, cw_g, cw_v, cb_g, cb_v)


def _ffn_mid_bwd(u, cw_g, cw_v, cb_g, cb_v, da):
    _, t, f = u.shape
    nr = t // CONV_TR
    ext = CONV_TR + 2 * HALO

    def body(ug, ugp, ugn, uv, uvp, uvn, dam, dap, dan, wg, wv, bg, bv,
             du_ref, dwg_ref, dwv_ref, dbg_ref, dbv_ref):
        r = pl.program_id(1)
        shifted = []
        for main, prev, nxt in ((ug, ugp, ugn), (uv, uvp, uvn)):
            x = _conv_ext(main, prev, nxt, r, nr)
            shifted.append((pltpu.roll(x, 1, 0), x, pltpu.roll(x, ext - 1, 0)))
        da_x = _conv_ext(dam, dap, dan, r, nr)
        wg_t, wv_t = wg[...], wv[...]
        gate = shifted[0][0] * wg_t[0:1] + shifted[0][1] * wg_t[1:2] + shifted[0][2] * wg_t[2:3] + bg[...]
        val = shifted[1][0] * wv_t[0:1] + shifted[1][1] * wv_t[1:2] + shifted[1][2] * wv_t[2:3] + bv[...]
        sig = jax.nn.sigmoid(gate)
        silu = gate * sig
        d_val = da_x * silu
        d_gate = da_x * val * (sig + silu * (1.0 - sig))
        own = slice(HALO, HALO + CONV_TR)
        for half, (xs3, d, wt, dw_ref, db_ref) in enumerate(((shifted[0], d_gate, wg_t, dwg_ref, dbg_ref),
                                                            (shifted[1], d_val, wv_t, dwv_ref, dbv_ref))):
            du = pltpu.roll(d, ext - 1, 0) * wt[0:1] + d * wt[1:2] + pltpu.roll(d, 1, 0) * wt[2:3]
            du_ref[half] = du[own].astype(BF16)
            d_own = d[own]
            dw = jnp.concatenate([jnp.sum(x[own] * d_own, axis=0, keepdims=True) for x in xs3], axis=0)
            db = jnp.sum(d_own, axis=0, keepdims=True)

            @pl.when(r == 0)
            def _():
                dw_ref[...] = jnp.zeros_like(dw_ref)
                db_ref[...] = jnp.zeros_like(db_ref)

            dw_ref[...] += dw
            db_ref[...] += db

    w_spec = pl.BlockSpec((3, CONV_TC), lambda j, r: (0, j))
    b_spec = pl.BlockSpec((1, CONV_TC), lambda j, r: (0, j))
    return pl.pallas_call(
        body, name="ffn_mid_bwd", grid=(f // CONV_TC, nr),
        in_specs=(_conv_specs(t, HALO, 0) + _conv_specs(t, HALO, 1) + _conv_specs(t, HALO16)
                  + [w_spec, w_spec, b_spec, b_spec]),
        out_specs=[pl.BlockSpec((2, CONV_TR, CONV_TC), lambda j, r: (0, r, j)), w_spec, w_spec, b_spec, b_spec],
        out_shape=[jax.ShapeDtypeStruct((2, t, f), BF16),
                   jax.ShapeDtypeStruct((3, f), F32), jax.ShapeDtypeStruct((3, f), F32),
                   jax.ShapeDtypeStruct((1, f), F32), jax.ShapeDtypeStruct((1, f), F32)],
        compiler_params=_params(("parallel", "arbitrary"), VMEM_LIMIT),
    )(u, u, u, u, u, u, da, da, da, cw_g, cw_v, cb_g, cb_v)


def _down_proj_loss(act, w_down, x1, target, *, tm, tn):
    t, f = act.shape
    d = w_down.shape[1]

    def body(a_ref, w_ref, x_ref, t_ref, loss_ref, dy_ref, dyb_ref):
        y = lax.dot_general(a_ref[...], w_ref[...], _DIMS["nn"], preferred_element_type=F32) + x_ref[...]
        err = y - t_ref[...]
        dy = err * (1.0 / d)
        dy_ref[...] = dy
        dyb_ref[...] = dy.astype(BF16)
        part = 0.5 * jnp.sum(jnp.sum(err * err, axis=-1, keepdims=True) * (1.0 / d), axis=0, keepdims=True)

        @pl.when((pl.program_id(0) == 0) & (pl.program_id(1) == 0))
        def _():
            loss_ref[...] = jnp.zeros_like(loss_ref)

        loss_ref[...] += jnp.broadcast_to(part, loss_ref.shape)

    tile = pl.BlockSpec((tm, tn), lambda i, j: (i, j))
    return pl.pallas_call(
        body, name="down_proj_loss", grid=(t // tm, d // tn),
        in_specs=[pl.BlockSpec((tm, f), lambda i, j: (i, 0)), pl.BlockSpec((f, tn), lambda i, j: (0, j)), tile, tile],
        out_specs=[pl.BlockSpec((1, 128), lambda i, j: (0, 0)), tile, tile],
        out_shape=[jax.ShapeDtypeStruct((1, 128), F32), jax.ShapeDtypeStruct((t, d), F32),
                   jax.ShapeDtypeStruct((t, d), BF16)],
        compiler_params=_params(("arbitrary", "arbitrary"), VMEM_LIMIT),
    )(act, w_down, x1, target)


ANY = pl.BlockSpec(memory_space=pl.ANY)


def _position():
    return lax.axis_index("x"), lax.axis_index("y"), lax.axis_index("c")


def _other_chips(x, y):
    return [(1 - x, y), (x, 1 - y), (1 - x, 1 - y)]


def _handshake(peers):
    barrier = pltpu.get_barrier_semaphore()
    for peer in peers:
        pl.semaphore_signal(barrier, inc=1, device_id=peer, device_id_type=MESH)
    pl.semaphore_wait(barrier, len(peers))


def _exchange(body, operands, out_shapes, sems, *, name, collective_id):
    n_in, n_out = len(operands), len(out_shapes)

    def run(*refs):
        body(refs[:n_in], refs[n_in:n_in + n_out], *refs[n_in + n_out:])

    if collective_id is None:
        return pl.pallas_call(run, name=name, in_specs=[ANY] * n_in, out_specs=[ANY] * n_out,
                              out_shape=out_shapes, scratch_shapes=sems)(*operands)
    return pl.kernel(run, name=name, out_type=out_shapes,
                     mesh=plsc.ScalarSubcoreMesh(axis_name="sequencer", num_cores=1), scratch_types=sems,
                     compiler_params=pltpu.CompilerParams(collective_id=collective_id))(*operands)


def _all_gather(blocks, *, name, collective_id=None):
    na = len(blocks)

    def body(ins, outs, send_sems, recv_sems, local_sems):
        x, y, c = _position()
        me, sibling = (x, y, c), (x, y, 1 - c)
        along_x, along_y, diagonal = (1 - x, y, c), (x, 1 - y, c), (1 - x, 1 - y, c)
        first_c = c == 0
        relay_from = (jnp.where(first_c, x, 1 - x), jnp.where(first_c, 1 - y, y), c)
        relay_to = (jnp.where(first_c, 1 - x, x), jnp.where(first_c, y, 1 - y), c)
        if collective_id is not None:
            _handshake([sibling, along_x, along_y])

        def index(px, py, pc):
            return 4 * px + 2 * py + pc

        def copy(a, k, block, to, src=None):
            dst = outs[a].at[index(*block)]
            return pltpu.make_async_remote_copy(
                src_ref=dst if src is None else src, dst_ref=dst,
                send_sem=send_sems.at[a, k], recv_sem=recv_sems.at[a, k],
                device_id=to, device_id_type=MESH)

        pending = []
        for a in range(na):
            mine = pltpu.make_async_copy(ins[a], outs[a].at[index(*me)], local_sems.at[a])
            mine.start()
            pending.append(mine)
        sent = []
        for a in range(na):
            sent += [copy(a, 0, me, sibling, src=ins[a]), copy(a, 1, me, along_x, src=ins[a]),
                     copy(a, 2, me, along_y, src=ins[a])]
        for cp in sent:
            cp.start()

        def passes_on(k_in, owner, k_out):
            for a in range(na):
                copy(a, k_in, owner, me).wait_recv()
                cp = copy(a, k_out, owner, sibling)
                cp.start()
                sent.append(cp)

        passes_on(1, along_x, 4)
        passes_on(2, along_y, 5)
        for a in range(na):
            cp = copy(a, 3, relay_from, relay_to)
            cp.start()
            sent.append(cp)
        passes_on(3, diagonal, 6)
        for a in range(na):
            copy(a, 0, sibling, me).wait_recv()
            for k, owner in ((4, along_x), (5, along_y), (6, diagonal)):
                copy(a, k, (owner[0], owner[1], 1 - c), me).wait_recv()
        for cp in sent:
            cp.wait_send()
        for cp in pending:
            cp.wait()

    return _exchange(
        body, blocks, [jax.ShapeDtypeStruct((N_DEV,) + b.shape, b.dtype) for b in blocks],
        [pltpu.SemaphoreType.DMA((na, 7)), pltpu.SemaphoreType.DMA((na, 7)), pltpu.SemaphoreType.DMA((na,))],
        name=name, collective_id=collective_id)


def _grad_exchange(grads, parts, *, name, collective_id):
    ng, npart = len(grads), len(parts)

    def body(ins, outs, core_send, core_recv, chip_send, chip_recv, local_sems):
        x, y, c = _position()
        sibling = (x, y, 1 - c)
        chips = _other_chips(x, y)
        _handshake([sibling] + [(px, py, c) for px, py in chips])
        me = 2 * x + y
        copies = []
        for b in range(npart):
            src, dst = ins[ng + b], outs[ng + b]
            own = pltpu.make_async_copy(src.at[me], dst.at[me], local_sems.at[b])
            own.start()
            copies.append(own)
            for j, (px, py) in enumerate(chips):
                cp = pltpu.make_async_remote_copy(
                    src_ref=src.at[2 * px + py], dst_ref=dst.at[me],
                    send_sem=chip_send.at[b, j], recv_sem=chip_recv.at[b, j],
                    device_id=(px, py, c), device_id_type=MESH)
                cp.start()
                copies.append(cp)
        for a in range(ng):
            for k in range(N_CHIP):
                src = ins[a].at[k, 1 - c] if len(grads[a].shape) == 4 else ins[a].at[k]
                cp = pltpu.make_async_remote_copy(
                    src_ref=src, dst_ref=outs[a].at[k],
                    send_sem=core_send.at[a, k], recv_sem=core_recv.at[a, k],
                    device_id=sibling, device_id_type=MESH)
                cp.start()
                copies.append(cp)
        for cp in copies:
            cp.wait()

    shapes = ([jax.ShapeDtypeStruct((N_CHIP,) + g.shape[-2:], g.dtype) for g in grads]
              + [jax.ShapeDtypeStruct(p.shape, p.dtype) for p in parts])
    sems = [pltpu.SemaphoreType.DMA((max(ng, 1), N_CHIP)), pltpu.SemaphoreType.DMA((max(ng, 1), N_CHIP)),
            pltpu.SemaphoreType.DMA((max(npart, 1), 3)), pltpu.SemaphoreType.DMA((max(npart, 1), 3)),
            pltpu.SemaphoreType.DMA((max(npart, 1),))]
    out = _exchange(body, list(grads) + list(parts), shapes, sems, name=name, collective_id=collective_id)
    return out[:ng], out[ng:]


def _pair_sum(grad, theirs, core, *, tile, name, narrow=False):
    _, _, r, w = grad.shape
    tr, tw = tile
    assert r % tr == 0 and w % tw == 0

    def body(core_ref, mine_ref, theirs_ref, out_ref, *narrow_ref):
        total = mine_ref[...] + theirs_ref[...]
        out_ref[...] = total
        if narrow:
            narrow_ref[0][...] = total.astype(BF16)

    spec = pl.BlockSpec((None, tr, tw), lambda k, i, j, core_ref: (k, i, j))
    shapes = [jax.ShapeDtypeStruct((N_CHIP, r, w), F32)] + [jax.ShapeDtypeStruct((N_CHIP, r, w), BF16)] * narrow
    out = pl.pallas_call(
        body, name=name,
        grid_spec=pltpu.PrefetchScalarGridSpec(
            num_scalar_prefetch=1, grid=(N_CHIP, r // tr, w // tw),
            in_specs=[pl.BlockSpec((None, None, tr, tw), lambda k, i, j, core_ref: (k, core_ref[0], i, j)), spec],
            out_specs=[spec] * len(shapes)),
        out_shape=shapes,
        compiler_params=_params(("parallel", "parallel", "parallel"), VMEM_LIMIT),
    )(core, grad, theirs)
    return tuple(out) if narrow else out[0]


class _ReduceScatter:
    def __init__(self, core):
        self.core = core
        self.pending = None
        self.results = {}
        self.launches = 0

    def push(self, tag, grads, rows, then, narrow=False, finish=None):
        pair, kept, prev_tag = [], [], None
        if self.pending is not None:
            prev_tag, prev, theirs, prev_rows, prev_narrow, prev_finish = self.pending
            if prev_finish is not None:
                pair = prev_finish(theirs)
            else:
                pair = [_pair_sum(g, s, self.core, tile=tile, name=f"pair_sum_{prev_tag}_{i}", narrow=prev_narrow)
                        for i, (g, s, tile) in enumerate(zip(prev, theirs, prev_rows))]
            if prev_narrow:
                kept, pair = [p[0] for p in pair], [p[1] for p in pair]
        grads, pair, then = lax.optimization_barrier((list(grads), pair, then))
        if finish is None:
            grads = [g.reshape((N_CHIP, 2) + g.shape[1:]) for g in grads]
        self.launches += 1
        theirs, parts = _grad_exchange(grads, pair, name=f"grad_exchange_{self.launches}",
                                       collective_id=1 + self.launches)
        if prev_tag is not None:
            self.results[prev_tag] = (parts, kept)
        self.pending = (tag, grads, theirs, rows, narrow, finish) if tag is not None else None
        return then

    def result(self, tag):
        return self.results[tag]


def _adamw(parts, w, m, v, *, tile, name, own=None, chip=None):
    n, r, cols = parts.shape
    tr, tw = tile
    assert r % tr == 0 and cols % tw == 0 and w.shape == (r, cols)
    c1 = 1.0 - ADAM_B1 ** ADAM_STEP
    c2 = 1.0 - ADAM_B2 ** ADAM_STEP

    def update(g, w_ref, m_ref, v_ref, g_ref, d_ref, nm_ref, nv_ref):
        new_m = ADAM_B1 * m_ref[...] + (1.0 - ADAM_B1) * g
        new_v = ADAM_B2 * v_ref[...] + (1.0 - ADAM_B2) * (g * g)
        m_hat = new_m / c1
        v_hat = new_v / c2
        g_ref[...] = g
        d_ref[...] = -ADAM_LR * (m_hat / (jnp.sqrt(v_hat) + ADAM_EPS) + ADAM_WD * w_ref[...])
        nm_ref[...] = new_m
        nv_ref[...] = new_v

    shape = jax.ShapeDtypeStruct((r, cols), F32)
    if own is None:
        def body(p_ref, *refs):
            g = p_ref[0]
            for k in range(1, n):
                g = g + p_ref[k]
            update(g, *refs)

        spec = pl.BlockSpec((tr, tw), lambda i, j: (i, j))
        return pl.pallas_call(
            body, name=name, grid=(r // tr, cols // tw),
            in_specs=[pl.BlockSpec((n, tr, tw), lambda i, j: (0, i, j)), spec, spec, spec],
            out_specs=[spec] * 4, out_shape=[shape] * 4,
            compiler_params=_params(("parallel", "parallel"), VMEM_LIMIT),
        )(parts, w, m, v)

    def body(chip_ref, p_ref, own_ref, *refs):
        g = None
        for k in range(n):
            term = jnp.where(chip_ref[0] == k, own_ref[...], p_ref[k].astype(F32))
            g = term if g is None else g + term
        update(g, *refs)

    spec = pl.BlockSpec((tr, tw), lambda i, j, chip_ref: (i, j))
    return pl.pallas_call(
        body, name=name,
        grid_spec=pltpu.PrefetchScalarGridSpec(
            num_scalar_prefetch=1, grid=(r // tr, cols // tw),
            in_specs=[pl.BlockSpec((n, tr, tw), lambda i, j, chip_ref: (0, i, j)),
                      pl.BlockSpec((None, tr, tw), lambda i, j, chip_ref: (chip_ref[0], i, j)), spec, spec, spec],
            out_specs=[spec] * 4),
        out_shape=[shape] * 4,
        compiler_params=_params(("parallel", "parallel"), VMEM_LIMIT),
    )(chip, parts, own, w, m, v)


LANES = 128


def _row_offsets(pieces):
    offsets, row = [], 0
    for p in pieces:
        assert p.shape[0] == 1 and p.shape[1] % LANES == 0, p.shape
        offsets.append(row)
        row += p.shape[1] // LANES
    return offsets, row


def _pack_rows(pieces):
    offsets, rows = _row_offsets(pieces)

    def body(*refs):
        out_ref = refs[-1]
        for ref, start in zip(refs[:-1], offsets):
            for j in range(ref.shape[1] // LANES):
                out_ref[start + j:start + j + 1, :] = ref[:, j * LANES:(j + 1) * LANES]

    return pl.pallas_call(body, name="pack_small_grads",
                          out_shape=jax.ShapeDtypeStruct((rows, LANES), F32))(*pieces)


def _adamw_rows(terms, ws, ms, vs):
    n_dev, rows, _ = terms.shape
    offsets, used = _row_offsets(ws)
    assert used + 1 == rows
    c1 = 1.0 - ADAM_B1 ** ADAM_STEP
    c2 = 1.0 - ADAM_B2 ** ADAM_STEP
    nw = len(ws)

    def body(*refs):
        t_ref = refs[0]
        w_refs, m_refs, v_refs = refs[1:1 + nw], refs[1 + nw:1 + 2 * nw], refs[1 + 2 * nw:1 + 3 * nw]
        outs = refs[1 + 3 * nw:]
        total = t_ref[0]
        for k in range(1, n_dev):
            total = total + t_ref[k]
        for i, start in enumerate(offsets):
            for j in range(ws[i].shape[1] // LANES):
                lanes = slice(j * LANES, (j + 1) * LANES)
                g = total[start + j:start + j + 1, :]
                new_m = ADAM_B1 * m_refs[i][:, lanes] + (1.0 - ADAM_B1) * g
                new_v = ADAM_B2 * v_refs[i][:, lanes] + (1.0 - ADAM_B2) * (g * g)
                delta = -ADAM_LR * ((new_m / c1) / (jnp.sqrt(new_v / c2) + ADAM_EPS) + ADAM_WD * w_refs[i][:, lanes])
                for kind, value in enumerate((g, delta, new_m, new_v)):
                    outs[kind * nw + i][:, lanes] = value
        outs[-1][...] = total[used:used + 1, :]

    shapes = [jax.ShapeDtypeStruct(w.shape, F32) for w in ws] * 4 + [jax.ShapeDtypeStruct((1, LANES), F32)]
    out = pl.pallas_call(body, name="adamw_replicated", out_shape=shapes,
                         compiler_params=_params(None, VMEM_LIMIT))(terms, *ws, *ms, *vs)
    return [list(out[kind * nw:(kind + 1) * nw]) for kind in range(4)], out[-1]


def _rope_tables(t):
    half = HEAD_DIM // 2
    inv = 1.0 / (ROPE_THETA ** (jnp.arange(half, dtype=F32) / half))
    ang = jnp.arange(t, dtype=jnp.int32).astype(F32)[:, None] * inv[None, :]
    cos, sin = jnp.cos(ang), jnp.sin(ang)
    return jnp.concatenate([cos, cos], axis=1), jnp.concatenate([-sin, sin], axis=1)


IN_KERNEL = IN_MAIN + LR_PAD


def _to_kernel_rows(w_t):
    order = sorted(SEGMENTS.values())
    pad = jnp.zeros((LR_PAD - (IN_TOTAL - IN_MAIN), w_t.shape[1]), w_t.dtype)
    return jnp.concatenate([w_t[src:src + width] for _, src, width in order] + [w_t[IN_MAIN:IN_TOTAL], pad], axis=0)


CONV_TAPS = 3
WA_BLOCK = GLA_KEY_WIDTH // N_DEV
SMALL_SIZES = (CONV_TAPS * UP_BLOCK, GLA_RANK * WA_BLOCK, GLA_RANK * WA_BLOCK)
SMALL_SHAPES = ((CONV_TAPS, UP_BLOCK), (GLA_RANK, WA_BLOCK), (GLA_RANK, WA_BLOCK))
SMALL_ROWS = sum(SMALL_SIZES) // LANES


def _small_block(conv, wa_f, wa_b):
    return jnp.concatenate([conv.reshape(-1), wa_f.reshape(-1), wa_b.reshape(-1)]).reshape(SMALL_ROWS, LANES)


def _small_unblock(block):
    flat, out, off = block.reshape(-1), [], 0
    for size, shape in zip(SMALL_SIZES, SMALL_SHAPES):
        out.append(flat[off:off + size].reshape(shape))
        off += size
    return out


def _small_blocks(conv_full, wa_f_full, wa_b_full):
    def by_device(a, width):
        return jnp.transpose(a.reshape(a.shape[0], N_DEV, width), (1, 0, 2)).reshape(N_DEV, -1)
    return jnp.concatenate([by_device(conv_full, UP_BLOCK), by_device(wa_f_full, WA_BLOCK),
                            by_device(wa_b_full, WA_BLOCK)], axis=1).reshape(N_DEV, SMALL_ROWS, LANES)


def _small_unblocks(blocks):
    flat, out, off = blocks.reshape(N_DEV, -1), [], 0
    for size, (rows, width) in zip(SMALL_SIZES, SMALL_SHAPES):
        part = flat[:, off:off + size].reshape(N_DEV, rows, width)
        out.append(jnp.transpose(part, (1, 0, 2)).reshape(rows, N_DEV * width))
        off += size
    return out


def _local_step(xs, target, norm1_g, w_in8, gq, gk, attn_sink, w2, ba2, gla_norm_g, w_out_full, norm2_g,
                w_up8, cw_g, cw_v, cb_g, cb_v, w_down_full, rs=None):
    t = xs.shape[0]
    tm = min(1024, t)
    tall = min(2048, t)
    cos, sin_signed = _rope_tables(t)
    sink = attn_sink.reshape(ATTN_HEADS)

    h1 = _rmsnorm_fwd(xs, norm1_g, name="norm1_fwd")
    w_in8, h1, cos, sin_signed = lax.optimization_barrier((w_in8, h1, cos, sin_signed))
    w_in_k = _to_kernel_rows(w_in8.reshape(IN_TOTAL, D_MODEL))
    proj = _matmul(h1, w_in_k, "nt", tm=tall, tn=IN_MAIN // 4, tk=D_MODEL, n_out=IN_MAIN, name="proj_main")
    proj_lr = _matmul(h1, w_in_k[IN_MAIN:], "nt", tm=tm, tn=LR_PAD, tk=D_MODEL, name="proj_lr")
    qa, ka, va = _attn_prep_fwd(proj, cos, sin_signed, gq, gk)
    assert t >= 2 * ATTN_BLOCK
    attn_bias = _attn_bias()
    o_attn = _attn_fwd(qa, ka, va, sink, attn_bias)
    g_dec = _gla_prep_fwd(proj_lr, w2, ba2)
    o_f, o_b, s_f, s_b = _gla_fwd(proj, g_dec)
    o_gla = _gla_out_fwd(o_f, o_b, proj, gla_norm_g)
    x1 = _out_proj(o_attn, o_gla, w_out_full, xs, tm=tall, tn=512)
    h2 = _rmsnorm_fwd(x1, norm2_g, name="norm2_fwd")
    u = _up_proj(h2, w_up8, tm=tm)
    act = _ffn_mid_fwd(u, cw_g, cw_v, cb_g, cb_v)
    loss_part, dy, dy_b = _down_proj_loss(act, w_down_full, x1, target, tm=tm, tn=512)

    d_act = _matmul(dy_b, w_down_full, "nt", tm=tall, tn=D_FF // 4, tk=D_MODEL, out_dtype=BF16, name="d_act")
    dw_down = _matmul(act, dy_b, "tn", tm=D_FF // 4, tn=512, tk=t, name="dw_down")
    if rs is not None:
        d_act = rs.push("w_down", [dw_down.reshape(N_DEV, D_FF // N_DEV, D_MODEL)],
                        [(D_FF // N_DEV // 2, D_MODEL)], d_act)
    du, dcw_g, dcw_v, dcb_g, dcb_v = _ffn_mid_bwd(u, cw_g, cw_v, cb_g, cb_v, d_act)
    if rs is None:
        dw_up8 = _up_proj_dw(h2, du, tm=512, tk=t)
    else:
        dw_up8 = None
        theirs_first = _up_proj_dw_core(h2, du, 1 - rs.core, tm=512, tk=t, name="up_proj_dw_sibling")
        du = rs.push("w_up", [theirs_first], None, du, finish=lambda got, h2=h2, du=du: [
            _up_proj_dw_core(h2, du, rs.core, tm=512, tk=t, res=got[0], name="up_proj_dw_own")])
    dh2 = _up_proj_dx(du, w_up8, tm=tm, tn=1024)
    dx1, dx1_b, d_norm2 = _rmsnorm_bwd(x1, norm2_g, dh2, dy, name="norm2_bwd", also_bf16=True)
    dmix = _matmul(dx1_b, w_out_full, "nt", tm=tall, tn=1024, tk=D_MODEL, name="d_mix")
    dw_out = _out_proj_dw(o_attn, o_gla, dx1_b, tn=512)
    if rs is not None:
        dmix = rs.push("w_out", [dw_out.reshape(N_DEV, D_MODEL // N_DEV, D_MODEL)], [(256, D_MODEL)], dmix)
    do_gla, d_gate, d_gla_norm = _gla_out_bwd(o_f, o_b, proj, gla_norm_g, dmix)
    (dq_f, dk_f, dv_f, dg_f, dq_b, dk_b, dv_b, dg_b) = _gla_bwd(proj, g_dec, s_f, s_b, do_gla)
    d_lr, d_w2, d_ba2 = _gla_prep_bwd(proj_lr, w2, ba2, dg_f, dg_b)
    dqa, dk_lo, dk_mid, dk_hi, dv_lo, dv_mid, dv_hi, d_sink8 = _attn_bwd(qa, ka, va, sink, attn_bias, dmix)
    d_qa, d_ka, d_va, d_qn, d_kn = _attn_prep_bwd(proj, cos, sin_signed, gq, gk, dqa,
                                                  (dk_lo, dk_mid, dk_hi), (dv_lo, dv_mid, dv_hi))
    d_seg = {"qa": d_qa, "gate": d_gate, "vg": (dv_f + dv_b).astype(BF16), "qg": (dq_f + dq_b).astype(BF16),
             "kg": (dk_f + dk_b).astype(BF16), "ka": d_ka, "va": d_va}
    d_proj = jnp.concatenate([d_seg[k] for k in sorted(SEGMENTS, key=lambda k: SEGMENTS[k][0])] + [d_lr], axis=1)
    dw_in_t = _in_proj_dw_lr(_in_proj_dw(d_proj, h1, tn=1024), d_lr, h1)
    if rs is not None:
        per_in = IN_TOTAL // N_DEV
        small_grad = _small_blocks(jnp.concatenate([dcw_g, dcw_v], axis=1), d_w2[:GLA_RANK, :GLA_KEY_WIDTH],
                                   d_w2[GLA_RANK:2 * GLA_RANK, GLA_KEY_WIDTH:])
        d_proj, d_lr = rs.push("w_in", [dw_in_t.reshape(N_DEV, per_in, D_MODEL), small_grad],
                               [(per_in, 512), small_grad.shape[1:]], (d_proj, d_lr), narrow=True)
    dh1 = _matmul(d_proj, w_in_k, "nn", tm=tm, tn=512, tk=IN_KERNEL, name="dh1")
    if rs is not None:
        dh1 = rs.push(None, [], [], dh1)
    grad_x, d_norm1 = _rmsnorm_bwd(xs, norm1_g, dh1, dx1, name="norm1_bwd", also_bf16=False)
    return (loss_part, grad_x, dw_in_t, dw_out, dw_up8, dw_down, dcw_g, dcw_v, dcb_g, dcb_v,
            d_w2, d_ba2, d_norm1, d_norm2, d_qn, d_kn, d_sink8, d_gla_norm)


def kernel(x, norm1_g, w_in, attn_q_norm_g, attn_k_norm_g, attn_sink, gla_wa2_fwd, gla_ba_fwd, gla_wa2_bwd, gla_ba_bwd, gla_out_norm_g, w_out, norm2_g, w_up, conv_w, conv_b, w_down, loss_target, m_norm1_g, m_w_in, m_attn_q_norm_g, m_attn_k_norm_g, m_attn_sink, m_gla_wa2_fwd, m_gla_ba_fwd, m_gla_wa2_bwd, m_gla_ba_bwd, m_gla_out_norm_g, m_w_out, m_norm2_g, m_w_up, m_conv_w, m_conv_b, m_w_down, v_norm1_g, v_w_in, v_attn_q_norm_g, v_attn_k_norm_g, v_attn_sink, v_gla_wa2_fwd, v_gla_ba_fwd, v_gla_wa2_bwd, v_gla_ba_bwd, v_gla_out_norm_g, v_w_out, v_norm2_g, v_w_up, v_conv_w, v_conv_b, v_w_down):
    t = x.shape[1]
    xs = x.reshape(t, D_MODEL)
    target = loss_target.reshape(t, D_MODEL)
    core = lax.axis_index("c").astype(jnp.int32).reshape(1)

    w_small = _small_block(conv_w[0], gla_wa2_fwd[0], gla_wa2_bwd[0])
    w_in_t, m_in_t, v_in_t = (jnp.swapaxes(a[0], 0, 1) for a in (w_in, m_w_in, v_w_in))
    g_in, g_small = _all_gather([w_in_t.astype(BF16), w_small], name="gather_w_in", collective_id=8)
    g_out, w_up8, g_down = _all_gather([w_out[0].astype(BF16), w_up[0].astype(BF16), w_down[0].astype(BF16)],
                                       name="gather_later_weights", collective_id=1)
    w_out_full = g_out.reshape(D_MODEL, D_MODEL)
    w_down_full = g_down.reshape(D_FF, D_MODEL)
    conv_w_full, wa2_f, wa2_b = _small_unblocks(g_small)
    cw_g, cw_v = conv_w_full[:, :D_FF], conv_w_full[:, D_FF:]
    cb_g, cb_v = conv_b[:, :D_FF], conv_b[:, D_FF:]
    w2 = jnp.zeros((LR_PAD, 2 * GLA_KEY_WIDTH), F32)
    w2 = w2.at[:GLA_RANK, :GLA_KEY_WIDTH].set(wa2_f).at[GLA_RANK:2 * GLA_RANK, GLA_KEY_WIDTH:].set(wa2_b)
    ba2 = jnp.concatenate([gla_ba_fwd, gla_ba_bwd], axis=1)
    rs = _ReduceScatter(core)
    (loss_part, grad_x, _, _, _, _, _, _, dcb_g, dcb_v, _, d_ba2,
     d_norm1, d_norm2, d_qn, d_kn, d_sink8, d_gla_norm) = _local_step(
        xs, target, norm1_g, g_in, attn_q_norm_g, attn_k_norm_g, attn_sink, w2, ba2, gla_out_norm_g,
        w_out_full, norm2_g, w_up8, cw_g, cw_v, cb_g, cb_v, w_down_full, rs=rs)

    (part_down,), (part_up,), (part_out,) = rs.result("w_down")[0], rs.result("w_up")[0], rs.result("w_out")[0]
    (part_in, part_small), (own_in, own_small) = rs.result("w_in")
    chip = (2 * lax.axis_index("x") + lax.axis_index("y")).astype(jnp.int32).reshape(1)
    m_small = _small_block(m_conv_w[0], m_gla_wa2_fwd[0], m_gla_wa2_bwd[0])
    v_small = _small_block(v_conv_w[0], v_gla_wa2_fwd[0], v_gla_wa2_bwd[0])
    upd_in = _adamw(part_in, w_in_t, m_in_t, v_in_t, tile=(IN_TOTAL // N_DEV, 512), name="adamw_w_in",
                    own=own_in, chip=chip)
    upd_in = [jnp.swapaxes(u, 0, 1) for u in upd_in]
    upd_out = _adamw(part_out, w_out[0], m_w_out[0], v_w_out[0], tile=(256, D_MODEL), name="adamw_w_out")
    upd_up = _adamw(part_up, w_up[0], m_w_up[0], v_w_up[0], tile=(256, UP_BLOCK), name="adamw_w_up")
    upd_down = _adamw(part_down, w_down[0], m_w_down[0], v_w_down[0], tile=(D_FF // N_DEV // 4, D_MODEL),
                      name="adamw_w_down")
    upd_small = _adamw(part_small, w_small, m_small, v_small, tile=part_small.shape[1:],
                       name="adamw_small", own=own_small, chip=chip)
    upd_small = [_small_unblock(u) for u in upd_small]

    rep_names = ["norm1_g", "attn_q_norm_g", "attn_k_norm_g", "attn_sink", "gla_ba_fwd", "gla_ba_bwd",
                 "gla_out_norm_g", "norm2_g", "conv_b"]
    def whole_lanes(sink_like):
        return jnp.pad(sink_like, ((0, 0), (0, LANES - ATTN_HEADS)))

    rep_w = [norm1_g, attn_q_norm_g, attn_k_norm_g, whole_lanes(attn_sink), gla_ba_fwd, gla_ba_bwd, gla_out_norm_g,
             norm2_g, conv_b]
    rep_m = [m_norm1_g, m_attn_q_norm_g, m_attn_k_norm_g, whole_lanes(m_attn_sink), m_gla_ba_fwd, m_gla_ba_bwd,
             m_gla_out_norm_g, m_norm2_g, m_conv_b]
    rep_v = [v_norm1_g, v_attn_q_norm_g, v_attn_k_norm_g, whole_lanes(v_attn_sink), v_gla_ba_fwd, v_gla_ba_bwd,
             v_gla_out_norm_g, v_norm2_g, v_conv_b]
    d_sink = whole_lanes(d_sink8[:, :GQA_GROUP, 0].reshape(1, ATTN_HEADS))
    rep_g = [d_norm1, d_qn, d_kn, d_sink, d_ba2[:, :GLA_KEY_WIDTH], d_ba2[:, GLA_KEY_WIDTH:], d_gla_norm, d_norm2,
             jnp.concatenate([dcb_g, dcb_v], axis=1)]
    (rep_terms,) = _all_gather([_pack_rows(rep_g + [loss_part])], name="gather_small_grads", collective_id=7)
    upd_rep, loss_row = _adamw_rows(rep_terms, rep_w, rep_m, rep_v)
    sink_at = rep_names.index("attn_sink")
    for kind in range(4):
        upd_rep[kind][sink_at] = upd_rep[kind][sink_at][:, :ATTN_HEADS]
    loss = loss_row[0, 0]

    order = ["norm1_g", "w_in", "attn_q_norm_g", "attn_k_norm_g", "attn_sink", "gla_wa2_fwd", "gla_ba_fwd",
             "gla_wa2_bwd", "gla_ba_bwd", "gla_out_norm_g", "w_out", "norm2_g", "w_up", "conv_w", "conv_b", "w_down"]
    outs = [loss, grad_x.reshape(1, t, D_MODEL)]
    for kind in range(4):
        by_name = {n: upd_rep[kind][i] for i, n in enumerate(rep_names)}
        by_name["w_in"] = upd_in[kind][None]
        by_name["w_out"] = upd_out[kind][None]
        by_name["w_up"] = upd_up[kind][None]
        by_name["w_down"] = upd_down[kind][None]
        by_name["conv_w"] = upd_small[kind][0][None]
        by_name["gla_wa2_fwd"] = upd_small[kind][1][None]
        by_name["gla_wa2_bwd"] = upd_small[kind][2][None]
        outs += [by_name[n] for n in order]
    return tuple(outs)
```

```python
import functools

import jax
import jax.numpy as jnp
from jax import lax
from jax.experimental import pallas as pl
from jax.experimental.pallas import tpu as pltpu
from jax.experimental.pallas import tpu_sc as plsc

F32 = jnp.float32
BF16 = jnp.bfloat16

D_MODEL = 2048
HEAD_DIM = 128
ATTN_WIDTH = 1024
ATTN_HEADS = 8
KV_HEADS = 2
GQA_GROUP = 4
KV_WIDTH = KV_HEADS * HEAD_DIM
ATTN_BLOCK = 128
WINDOW = 128
ROPE_THETA = 10000.0
GLA_HEADS = 4
GLA_DK = 128
GLA_DV = 256
GLA_KEY_WIDTH = 512
GLA_WIDTH = 1024
GLA_RANK = 16
GLA_GATE_NORMALIZER = 16.0
GLA_CHUNK = 64
GLA_PER_STEP = 4
D_FF = 5632
NORM_EPS = 1e-6
IN_TOTAL = 4640
IN_MAIN = 4608
LR_PAD = 128
N_DEV = 8
N_CHIP = 4

ADAM_LR = 0.001
ADAM_B1 = 0.9
ADAM_B2 = 0.999
ADAM_EPS = 1e-08
ADAM_WD = 0.01
ADAM_STEP = 10

SEGMENTS = {
    "qa": (0, 0, 1024),
    "gate": (1024, 3584, 1024),
    "vg": (2048, 2560, 1024),
    "qg": (3072, 1536, 512),
    "kg": (3584, 2048, 512),
    "ka": (4096, 1024, 256),
    "va": (4352, 1280, 256),
}

VMEM_LIMIT = 56 * 1024 * 1024
MESH = pl.DeviceIdType.MESH


def _params(semantics=None, vmem=None):
    return pltpu.CompilerParams(dimension_semantics=semantics, vmem_limit_bytes=vmem)


_DIMS = {
    "nn": (((1,), (0,)), ((), ())),
    "nt": (((1,), (1,)), ((), ())),
    "tn": (((0,), (0,)), ((), ())),
}


def _mxu(a, b, mode):
    return lax.dot_general(a.astype(BF16), b.astype(BF16), _DIMS[mode], preferred_element_type=F32)


@functools.partial(jax.custom_vjp, nondiff_argnums=(2,))
def bdot(a, b, mode):
    return _mxu(a, b, mode)


def _bdot_fwd(a, b, mode):
    return _mxu(a, b, mode), (a, b)


def _bdot_bwd(mode, res, g):
    a, b = res
    if mode == "nn":
        return _mxu(g, b, "nt"), _mxu(a, g, "tn")
    if mode == "nt":
        return _mxu(g, b, "nn"), _mxu(g, a, "tn")
    return _mxu(b, g, "nt"), _mxu(a, g, "nn")


bdot.defvjp(_bdot_fwd, _bdot_bwd)


def _rms(x, g):
    return x * lax.rsqrt(jnp.mean(x * x, axis=-1, keepdims=True) + NORM_EPS) * g


def _rope(x, cos, sin_signed):
    return x * cos + pltpu.roll(x, HEAD_DIM // 2, 1) * sin_signed


def _rope_transposed(d, cos, sin_signed):
    return d * cos + pltpu.roll(d * sin_signed, HEAD_DIM // 2, 1)


def _silu(x):
    return x * jax.nn.sigmoid(x)


def _log_sigmoid(z):
    return -(jnp.maximum(-z, 0.0) + jnp.log(1.0 + jnp.exp(-jnp.abs(z))))


def _matmul_call(args, in_specs, o_spec, out_shape, grid, mode, nk, acc_shape, *, name, has_res=False,
                 prefetch=None, load_b=lambda ref: ref[...]):
    dims = _DIMS[mode]
    out_dtype = out_shape.dtype
    n_pre = 0 if prefetch is None else 1

    def body(*refs):
        refs = refs[n_pre:]
        if has_res:
            a_ref, b_ref, r_ref, o_ref = refs[:4]
            rest = refs[4:]
        else:
            a_ref, b_ref, o_ref = refs[:3]
            r_ref = None
            rest = refs[3:]
        part = lax.dot_general(a_ref[...], load_b(b_ref), dims, preferred_element_type=F32)

        def finish(acc):
            if r_ref is not None:
                acc = acc + r_ref[...]
            o_ref[...] = acc.astype(out_dtype)

        if nk == 1:
            finish(part)
        else:
            acc_ref = rest[0]
            kk = pl.program_id(2)

            @pl.when(kk == 0)
            def _():
                acc_ref[...] = part

            @pl.when(kk > 0)
            def _():
                acc_ref[...] += part

            @pl.when(kk == nk - 1)
            def _():
                finish(acc_ref[...])

    scratch = [pltpu.VMEM(acc_shape, F32)] if nk > 1 else []
    params = _params(("parallel", "parallel", "arbitrary"), VMEM_LIMIT)
    if prefetch is None:
        return pl.pallas_call(body, name=name, grid=grid, in_specs=in_specs, out_specs=o_spec, out_shape=out_shape,
                              scratch_shapes=scratch, compiler_params=params)(*args)
    return pl.pallas_call(
        body, name=name,
        grid_spec=pltpu.PrefetchScalarGridSpec(num_scalar_prefetch=1, grid=grid, in_specs=in_specs,
                                               out_specs=o_spec, scratch_shapes=scratch),
        out_shape=out_shape, compiler_params=params)(prefetch, *args)


def _matmul(a, b, mode, *, tm, tn, tk, out_dtype=F32, res=None, name, n_out=None):
    if mode == "nn":
        (m, k), (k2, n) = a.shape, b.shape
    elif mode == "nt":
        (m, k), (n, k2) = a.shape, b.shape
    else:
        (k, m), (k2, n) = a.shape, b.shape
    n = n if n_out is None else n_out
    assert k == k2 and m % tm == 0 and n % tn == 0 and k % tk == 0, (name, a.shape, b.shape, tm, tn, tk)
    if mode == "tn":
        a_spec = pl.BlockSpec((tk, tm), lambda i, j, kk: (kk, i))
    else:
        a_spec = pl.BlockSpec((tm, tk), lambda i, j, kk: (i, kk))
    if mode == "nt":
        b_spec = pl.BlockSpec((tn, tk), lambda i, j, kk: (j, kk))
    else:
        b_spec = pl.BlockSpec((tk, tn), lambda i, j, kk: (kk, j))
    o_spec = pl.BlockSpec((tm, tn), lambda i, j, kk: (i, j))
    in_specs, args = [a_spec, b_spec], [a, b]
    if res is not None:
        in_specs.append(o_spec)
        args.append(res)
    return _matmul_call(args, in_specs, o_spec, jax.ShapeDtypeStruct((m, n), out_dtype),
                        (m // tm, n // tn, k // tk), mode, k // tk, (tm, tn), name=name, has_res=res is not None)


def _out_proj(o_attn, o_gla, w_out, x, *, tm, tn):
    t, ka = o_attn.shape
    kg = o_gla.shape[1]

    def body(a_ref, g_ref, w_ref, x_ref, o_ref):
        acc = lax.dot_general(a_ref[...], w_ref[:ka], _DIMS["nn"], preferred_element_type=F32)
        acc = acc + lax.dot_general(g_ref[...], w_ref[ka:], _DIMS["nn"], preferred_element_type=F32)
        o_ref[...] = acc + x_ref[...]

    tile = pl.BlockSpec((tm, tn), lambda i, j: (i, j))
    return pl.pallas_call(
        body, name="out_proj", grid=(t // tm, D_MODEL // tn),
        in_specs=[pl.BlockSpec((tm, ka), lambda i, j: (i, 0)), pl.BlockSpec((tm, kg), lambda i, j: (i, 0)),
                  pl.BlockSpec((ka + kg, tn), lambda i, j: (0, j)), tile],
        out_specs=tile, out_shape=jax.ShapeDtypeStruct((t, D_MODEL), F32),
        compiler_params=_params(("parallel", "parallel"), VMEM_LIMIT),
    )(o_attn, o_gla, w_out, x)


def _out_proj_dw(o_attn, o_gla, dx1, *, tn):
    t, ka = o_attn.shape
    assert o_gla.shape == (t, ka)

    def body(a_ref, g_ref, d_ref, o_ref):
        @pl.when(pl.program_id(0) == 0)
        def _():
            o_ref[...] = lax.dot_general(a_ref[...], d_ref[...], _DIMS["tn"], preferred_element_type=F32)

        @pl.when(pl.program_id(0) == 1)
        def _():
            o_ref[...] = lax.dot_general(g_ref[...], d_ref[...], _DIMS["tn"], preferred_element_type=F32)

    whole = pl.BlockSpec((t, ka), lambda i, j: (0, 0))
    return pl.pallas_call(
        body, name="dw_out", grid=(2, D_MODEL // tn),
        in_specs=[whole, whole, pl.BlockSpec((t, tn), lambda i, j: (0, j))],
        out_specs=pl.BlockSpec((ka, tn), lambda i, j: (i, j)),
        out_shape=jax.ShapeDtypeStruct((2 * ka, D_MODEL), F32),
        compiler_params=_params(("parallel", "parallel"), VMEM_LIMIT),
    )(o_attn, o_gla, dx1)


UP_BLOCK = 2 * D_FF // N_DEV


def _up_proj(h2, w_up8, *, tm):
    t = h2.shape[0]
    return _matmul_call(
        [h2, w_up8],
        [pl.BlockSpec((tm, D_MODEL), lambda i, j, kk: (i, 0)),
         pl.BlockSpec((None, D_MODEL, UP_BLOCK), lambda i, j, kk: (j, 0, 0))],
        pl.BlockSpec((None, tm, UP_BLOCK), lambda i, j, kk: (j // N_CHIP, i, j % N_CHIP)),
        jax.ShapeDtypeStruct((2, t, D_FF), F32), (t // tm, N_DEV, 1), "nn", 1, None, name="up_proj")


def _up_proj_dx(du, w_up8, *, tm, tn):
    t = du.shape[1]
    pair = 2
    return _matmul_call(
        [du, w_up8],
        [pl.BlockSpec((None, tm, pair * UP_BLOCK), lambda i, j, kk: (kk // 2, i, kk % 2)),
         pl.BlockSpec((pair, tn, UP_BLOCK), lambda i, j, kk: (kk, j, 0))],
        pl.BlockSpec((tm, tn), lambda i, j, kk: (i, j)),
        jax.ShapeDtypeStruct((t, D_MODEL), F32), (t // tm, D_MODEL // tn, N_DEV // pair), "nt", N_DEV // pair,
        (tm, tn), name="up_proj_dx", load_b=lambda ref: jnp.concatenate([ref[0], ref[1]], axis=1))


def _up_proj_dw(h2, du, *, tm, tk):
    t = h2.shape[0]
    return _matmul_call(
        [h2, du],
        [pl.BlockSpec((tk, tm), lambda j, i, kk: (kk, i)),
         pl.BlockSpec((None, tk, UP_BLOCK), lambda j, i, kk: (j // N_CHIP, kk, j % N_CHIP))],
        pl.BlockSpec((None, tm, UP_BLOCK), lambda j, i, kk: (j, i, 0)),
        jax.ShapeDtypeStruct((N_DEV, D_MODEL, UP_BLOCK), F32), (N_DEV, D_MODEL // tm, t // tk), "tn", t // tk,
        (tm, UP_BLOCK), name="up_proj_dw")


def _up_proj_dw_core(h2, du, core, *, tm, tk, res=None, name):
    t = h2.shape[0]

    def block(j, core_ref):
        return 2 * j + core_ref[0]

    out_spec = pl.BlockSpec((None, tm, UP_BLOCK), lambda j, i, kk, core_ref: (j, i, 0))
    return _matmul_call(
        [h2, du] + ([] if res is None else [res]),
        [pl.BlockSpec((tk, tm), lambda j, i, kk, core_ref: (kk, i)),
         pl.BlockSpec((None, tk, UP_BLOCK),
                      lambda j, i, kk, core_ref: (block(j, core_ref) // N_CHIP, kk, block(j, core_ref) % N_CHIP))]
        + ([] if res is None else [out_spec]),
        out_spec, jax.ShapeDtypeStruct((N_CHIP, D_MODEL, UP_BLOCK), F32), (N_CHIP, D_MODEL // tm, t // tk), "tn",
        t // tk, (tm, UP_BLOCK), name=name, has_res=res is not None, prefetch=core)


IN_TILE = 512


def _in_proj_dw(d_proj, h1, *, tn):
    t = h1.shape[0]
    table = []
    for tile in range(IN_MAIN // IN_TILE):
        dst, src, _ = max(s for s in SEGMENTS.values() if s[0] <= tile * IN_TILE)
        assert (src + tile * IN_TILE - dst) % IN_TILE == 0
        table.append((src + tile * IN_TILE - dst) // IN_TILE)
    assert sorted(table) == list(range(IN_MAIN // IN_TILE))
    return _matmul_call(
        [d_proj, h1],
        [pl.BlockSpec((t, IN_TILE), lambda j, i, kk, tab: (0, i)),
         pl.BlockSpec((t, tn), lambda j, i, kk, tab: (0, j))],
        pl.BlockSpec((IN_TILE, tn), lambda j, i, kk, tab: (tab[i], j)),
        jax.ShapeDtypeStruct((IN_TOTAL, D_MODEL), F32), (D_MODEL // tn, IN_MAIN // IN_TILE, 1), "tn", 1, None,
        name="in_proj_dw", prefetch=jnp.asarray(table, jnp.int32))


def _in_proj_dw_lr(dw_t, d_lr, h1):
    t = h1.shape[0]
    n_lr = IN_TOTAL - IN_MAIN
    tn = 512

    def body(dw_ref, dlr_ref, h1_ref, out_ref):
        full = lax.dot_general(dlr_ref[...], h1_ref[...], _DIMS["tn"], preferred_element_type=F32)
        out_ref[...] = full[:n_lr]

    return pl.pallas_call(
        body, name="in_proj_dw_lr", grid=(D_MODEL // tn,),
        in_specs=[pl.BlockSpec(memory_space=pl.ANY),
                  pl.BlockSpec((t, LR_PAD), lambda j: (0, 0)),
                  pl.BlockSpec((t, tn), lambda j: (0, j))],
        out_specs=pl.BlockSpec((n_lr, tn), lambda j: (IN_MAIN // n_lr, j)),
        out_shape=jax.ShapeDtypeStruct(dw_t.shape, F32),
        input_output_aliases={0: 0},
        compiler_params=_params(("parallel",), VMEM_LIMIT),
    )(dw_t, d_lr, h1)


def _rmsnorm_fwd(x, g, *, name, tr=512):
    t, d = x.shape

    def body(x_ref, g_ref, h_ref):
        h_ref[...] = _rms(x_ref[...], g_ref[...]).astype(BF16)

    return pl.pallas_call(
        body, name=name, grid=(t // tr,),
        in_specs=[pl.BlockSpec((tr, d), lambda i: (i, 0)), pl.BlockSpec((1, d), lambda i: (0, 0))],
        out_specs=pl.BlockSpec((tr, d), lambda i: (i, 0)),
        out_shape=jax.ShapeDtypeStruct((t, d), BF16),
        compiler_params=_params(("parallel",), VMEM_LIMIT),
    )(x, g)


def _rmsnorm_bwd(x, g, dh, dres, *, name, also_bf16, tr=512):
    t, d = x.shape

    def body(x_ref, g_ref, dh_ref, dres_ref, dx_ref, *rest):
        dg_ref = rest[-1]
        _, vjp = jax.vjp(_rms, x_ref[...], g_ref[...])
        dx, dg = vjp(dh_ref[...])
        dx = dx + dres_ref[...]
        dx_ref[...] = dx
        if also_bf16:
            rest[0][...] = dx.astype(BF16)

        @pl.when(pl.program_id(0) == 0)
        def _():
            dg_ref[...] = jnp.zeros_like(dg_ref)

        dg_ref[...] += dg

    row = pl.BlockSpec((tr, d), lambda i: (i, 0))
    vec = pl.BlockSpec((1, d), lambda i: (0, 0))
    return pl.pallas_call(
        body, name=name, grid=(t // tr,),
        in_specs=[row, vec, row, row],
        out_specs=[row] + [row] * also_bf16 + [vec],
        out_shape=[jax.ShapeDtypeStruct((t, d), F32)] + [jax.ShapeDtypeStruct((t, d), BF16)] * also_bf16
                  + [jax.ShapeDtypeStruct((1, d), F32)],
        compiler_params=_params(("arbitrary",), VMEM_LIMIT),
    )(x, g, dh, dres)


def _seg_block(name, width):
    off = SEGMENTS[name][0]
    assert off % width == 0
    return off // width


def _attn_prep_fwd(proj, cos, sin_signed, gq, gk, *, tr=512):
    t = proj.shape[0]

    def body(q_ref, k_ref, v_ref, cos_ref, sin_ref, gq_ref, gk_ref, qo_ref, ko_ref, vo_ref):
        cos_t, sin_t = cos_ref[...], sin_ref[...]
        for h in range(ATTN_HEADS):
            cols = slice(h * HEAD_DIM, (h + 1) * HEAD_DIM)
            qo_ref[:, cols] = _rope(_rms(q_ref[:, cols], gq_ref[...]), cos_t, sin_t).astype(BF16)
        for h in range(KV_HEADS):
            cols = slice(h * HEAD_DIM, (h + 1) * HEAD_DIM)
            ko_ref[:, cols] = _rope(_rms(k_ref[:, cols], gk_ref[...]), cos_t, sin_t).astype(BF16)
        vo_ref[...] = v_ref[...].astype(BF16)

    qb, kb, vb = _seg_block("qa", ATTN_WIDTH), _seg_block("ka", KV_WIDTH), _seg_block("va", KV_WIDTH)
    tab = pl.BlockSpec((tr, HEAD_DIM), lambda i: (i, 0))
    vec = pl.BlockSpec((1, HEAD_DIM), lambda i: (0, 0))
    return pl.pallas_call(
        body, name="attn_prep_fwd", grid=(t // tr,),
        in_specs=[pl.BlockSpec((tr, ATTN_WIDTH), lambda i: (i, qb)),
                  pl.BlockSpec((tr, KV_WIDTH), lambda i: (i, kb)),
                  pl.BlockSpec((tr, KV_WIDTH), lambda i: (i, vb)),
                  tab, tab, vec, vec],
        out_specs=[pl.BlockSpec((tr, ATTN_WIDTH), lambda i: (i, 0)),
                   pl.BlockSpec((tr, KV_WIDTH), lambda i: (i, 0)),
                   pl.BlockSpec((tr, KV_WIDTH), lambda i: (i, 0))],
        out_shape=[jax.ShapeDtypeStruct((t, ATTN_WIDTH), BF16),
                   jax.ShapeDtypeStruct((t, KV_WIDTH), BF16),
                   jax.ShapeDtypeStruct((t, KV_WIDTH), BF16)],
        compiler_params=_params(("parallel",), VMEM_LIMIT),
    )(proj, proj, proj, cos, sin_signed, gq, gk)


def _attn_heads(q, kcat, vcat, sink_col, bias):
    s = bdot(q, kcat, "nt") * (HEAD_DIM ** -0.5) + bias
    m = lax.stop_gradient(jnp.maximum(jnp.max(s, axis=-1, keepdims=True), sink_col))
    p = jnp.exp(s - m)
    p = p / (jnp.sum(p, axis=-1, keepdims=True) + jnp.exp(sink_col - m))
    return bdot(p, vcat, "nn")


def _attn_bias():
    rows, cols = GQA_GROUP * ATTN_BLOCK, 3 * ATTN_BLOCK
    qi = lax.broadcasted_iota(jnp.int32, (3, rows, cols), 1) % ATTN_BLOCK
    sj = lax.broadcasted_iota(jnp.int32, (3, rows, cols), 2)
    kind = lax.broadcasted_iota(jnp.int32, (3, rows, cols), 0)
    inside = jnp.where(kind == 0, sj >= ATTN_BLOCK, jnp.where(kind == 2, sj < 2 * ATTN_BLOCK, True))
    valid = (jnp.abs(sj - ATTN_BLOCK - qi) <= WINDOW) & inside
    return jnp.where(valid, 0.0, -jnp.inf).astype(F32)


def _attn_bias_spec(nb, block_of):
    def index(*ix):
        n = block_of(*ix)
        return (jnp.where(n == 0, 0, jnp.where(n == nb - 1, 2, 1)), 0, 0)
    return pl.BlockSpec((None, GQA_GROUP * ATTN_BLOCK, 3 * ATTN_BLOCK), index)


def _head_rows(g):
    return slice(g * ATTN_BLOCK, (g + 1) * ATTN_BLOCK)


def _head_cols(g):
    return slice(g * HEAD_DIM, (g + 1) * HEAD_DIM)


def _stack_heads(ref):
    return jnp.concatenate([ref[:, _head_cols(g)] for g in range(GQA_GROUP)], axis=0).astype(F32)


def _sink_column(sink_ref, h):
    return jnp.concatenate([jnp.full((ATTN_BLOCK, 1), sink_ref[h * GQA_GROUP + g], F32)
                            for g in range(GQA_GROUP)], axis=0)


def _attn_specs(nb):
    q_spec = pl.BlockSpec((ATTN_BLOCK, GQA_GROUP * HEAD_DIM), lambda h, n: (n, h))
    kv_specs = [
        pl.BlockSpec((ATTN_BLOCK, HEAD_DIM), lambda h, n: (jnp.maximum(n - 1, 0), h)),
        pl.BlockSpec((ATTN_BLOCK, HEAD_DIM), lambda h, n: (n, h)),
        pl.BlockSpec((ATTN_BLOCK, HEAD_DIM), lambda h, n: (jnp.minimum(n + 1, nb - 1), h)),
    ]
    return q_spec, kv_specs


def _attn_fwd(q, k, v, sink, bias):
    t = q.shape[0]
    nb = t // ATTN_BLOCK

    def body(sink_ref, bias_ref, q_ref, kp_ref, kc_ref, kn_ref, vp_ref, vc_ref, vn_ref, o_ref):
        bias = bias_ref[...]
        args = []
        for h in range(KV_HEADS):
            q = jnp.concatenate([q_ref[:, _head_cols(h * GQA_GROUP + g)] for g in range(GQA_GROUP)], axis=0)
            kcat = jnp.concatenate([r[:, _head_cols(h)] for r in (kp_ref, kc_ref, kn_ref)], axis=0)
            vcat = jnp.concatenate([r[:, _head_cols(h)] for r in (vp_ref, vc_ref, vn_ref)], axis=0)
            args.append((q.astype(F32), kcat.astype(F32), vcat.astype(F32), _sink_column(sink_ref, h)))
        outs = [_attn_heads(*a, bias=bias).astype(BF16) for a in args]
        for h, o in enumerate(outs):
            for g in range(GQA_GROUP):
                o_ref[:, _head_cols(h * GQA_GROUP + g)] = o[_head_rows(g)]

    q_spec = pl.BlockSpec((ATTN_BLOCK, ATTN_WIDTH), lambda n: (n, 0))
    kv_specs = [pl.BlockSpec((ATTN_BLOCK, KV_WIDTH), lambda n: (jnp.maximum(n - 1, 0), 0)),
                pl.BlockSpec((ATTN_BLOCK, KV_WIDTH), lambda n: (n, 0)),
                pl.BlockSpec((ATTN_BLOCK, KV_WIDTH), lambda n: (jnp.minimum(n + 1, nb - 1), 0))]
    return pl.pallas_call(
        body, name="attn_fwd", grid=(nb,),
        in_specs=[pl.BlockSpec(memory_space=pltpu.SMEM), _attn_bias_spec(nb, lambda n: n), q_spec] + kv_specs + kv_specs,
        out_specs=q_spec,
        out_shape=jax.ShapeDtypeStruct((t, ATTN_WIDTH), BF16),
        compiler_params=_params(("parallel",), VMEM_LIMIT),
    )(sink, bias, q, k, k, k, v, v, v)


def _attn_bwd(q, k, v, sink, bias, dmix):
    t = q.shape[0]
    nb = t // ATTN_BLOCK

    def body(sink_ref, bias_ref, q_ref, kp_ref, kc_ref, kn_ref, vp_ref, vc_ref, vn_ref, do_ref,
             dq_ref, dk_lo, dk_mid, dk_hi, dv_lo, dv_mid, dv_hi, dsink_ref):
        h, n = pl.program_id(0), pl.program_id(1)
        kcat = jnp.concatenate([kp_ref[...], kc_ref[...], kn_ref[...]], axis=0).astype(F32)
        vcat = jnp.concatenate([vp_ref[...], vc_ref[...], vn_ref[...]], axis=0).astype(F32)
        _, vjp = jax.vjp(functools.partial(_attn_heads, bias=bias_ref[...]),
                         _stack_heads(q_ref), kcat, vcat, _sink_column(sink_ref, h))
        dq, dk, dv, dsink_col = vjp(_stack_heads(do_ref))
        row = lax.broadcasted_iota(jnp.int32, (8, HEAD_DIM), 0)
        dsink = jnp.zeros((8, HEAD_DIM), F32)
        for g in range(GQA_GROUP):
            dq_ref[:, _head_cols(g)] = dq[_head_rows(g)]
            dsink = dsink + jnp.where(row == g, jnp.sum(dsink_col[_head_rows(g)]), 0.0)
        for i, (dk_ref, dv_ref) in enumerate(((dk_lo, dv_lo), (dk_mid, dv_mid), (dk_hi, dv_hi))):
            rows = slice(i * ATTN_BLOCK, (i + 1) * ATTN_BLOCK)
            dk_ref[...] = dk[rows]
            dv_ref[...] = dv[rows]

        @pl.when(n == 0)
        def _():
            dsink_ref[...] = jnp.zeros_like(dsink_ref)

        dsink_ref[...] += dsink

    q_spec, kv_specs = _attn_specs(nb)
    kv_out = pl.BlockSpec((ATTN_BLOCK, HEAD_DIM), lambda h, n: (n, h))
    kv_shape = jax.ShapeDtypeStruct((t, KV_WIDTH), F32)
    return pl.pallas_call(
        body, name="attn_bwd", grid=(KV_HEADS, nb),
        in_specs=[pl.BlockSpec(memory_space=pltpu.SMEM), _attn_bias_spec(nb, lambda h, n: n), q_spec]
                 + kv_specs + kv_specs + [q_spec],
        out_specs=[q_spec] + [kv_out] * 6 + [pl.BlockSpec((None, 8, HEAD_DIM), lambda h, n: (h, 0, 0))],
        out_shape=[jax.ShapeDtypeStruct((t, ATTN_WIDTH), F32)] + [kv_shape] * 6
                  + [jax.ShapeDtypeStruct((KV_HEADS, 8, HEAD_DIM), F32)],
        compiler_params=_params(("parallel", "arbitrary"), VMEM_LIMIT),
    )(sink, bias, q, k, k, k, v, v, v, dmix)


def _attn_prep_bwd(proj, cos, sin_signed, gq, gk, dq, dks, dvs):
    t = proj.shape[0]
    tr = ATTN_BLOCK
    nb = t // tr

    def body(q_ref, k_ref, cos_ref, sin_ref, gq_ref, gk_ref, dq_ref,
             dk_lo, dk_mid, dk_hi, dv_lo, dv_mid, dv_hi,
             dqo_ref, dko_ref, dvo_ref, dgq_ref, dgk_ref):
        n = pl.program_id(0)
        cos_t, sin_t = cos_ref[...], sin_ref[...]
        has_next = (n < nb - 1).astype(F32)
        has_prev = (n > 0).astype(F32)
        dk = dk_lo[...] * has_next + dk_mid[...] + dk_hi[...] * has_prev
        dv = dv_lo[...] * has_next + dv_mid[...] + dv_hi[...] * has_prev
        dvo_ref[...] = dv.astype(BF16)
        dgq = jnp.zeros((1, HEAD_DIM), F32)
        dgk = jnp.zeros((1, HEAD_DIM), F32)
        for h in range(ATTN_HEADS):
            cols = slice(h * HEAD_DIM, (h + 1) * HEAD_DIM)
            _, vjp = jax.vjp(_rms, q_ref[:, cols], gq_ref[...])
            dx, dg = vjp(_rope_transposed(dq_ref[:, cols], cos_t, sin_t))
            dqo_ref[:, cols] = dx.astype(BF16)
            dgq = dgq + dg
        for h in range(KV_HEADS):
            cols = slice(h * HEAD_DIM, (h + 1) * HEAD_DIM)
            _, vjp = jax.vjp(_rms, k_ref[:, cols], gk_ref[...])
            dx, dg = vjp(_rope_transposed(dk[:, cols], cos_t, sin_t))
            dko_ref[:, cols] = dx.astype(BF16)
            dgk = dgk + dg

        @pl.when(n == 0)
        def _():
            dgq_ref[...] = jnp.zeros_like(dgq_ref)
            dgk_ref[...] = jnp.zeros_like(dgk_ref)

        dgq_ref[...] += dgq
        dgk_ref[...] += dgk

    qb, kb = _seg_block("qa", ATTN_WIDTH), _seg_block("ka", KV_WIDTH)
    tab = pl.BlockSpec((tr, HEAD_DIM), lambda i: (i, 0))
    vec = pl.BlockSpec((1, HEAD_DIM), lambda i: (0, 0))
    kv = [pl.BlockSpec((tr, KV_WIDTH), lambda i: (jnp.minimum(i + 1, nb - 1), 0)),
          pl.BlockSpec((tr, KV_WIDTH), lambda i: (i, 0)),
          pl.BlockSpec((tr, KV_WIDTH), lambda i: (jnp.maximum(i - 1, 0), 0))]
    wide = pl.BlockSpec((tr, ATTN_WIDTH), lambda i: (i, 0))
    narrow = pl.BlockSpec((tr, KV_WIDTH), lambda i: (i, 0))
    return pl.pallas_call(
        body, name="attn_prep_bwd", grid=(nb,),
        in_specs=[pl.BlockSpec((tr, ATTN_WIDTH), lambda i: (i, qb)),
                  pl.BlockSpec((tr, KV_WIDTH), lambda i: (i, kb)),
                  tab, tab, vec, vec, wide] + kv + kv,
        out_specs=[wide, narrow, narrow, vec, vec],
        out_shape=[jax.ShapeDtypeStruct((t, ATTN_WIDTH), BF16),
                   jax.ShapeDtypeStruct((t, KV_WIDTH), BF16),
                   jax.ShapeDtypeStruct((t, KV_WIDTH), BF16),
                   jax.ShapeDtypeStruct((1, HEAD_DIM), F32),
                   jax.ShapeDtypeStruct((1, HEAD_DIM), F32)],
        compiler_params=_params(("arbitrary",), VMEM_LIMIT),
    )(proj, proj, cos, sin_signed, gq, gk, dq, *dks, *dvs)


def _decay_fn(lr, w2, ba):
    return _log_sigmoid(bdot(lr, w2, "nn") + ba) / GLA_GATE_NORMALIZER


def _gla_prep_fwd(proj_lr, w2, ba2, *, tr=512):
    t = proj_lr.shape[0]
    width = 2 * GLA_KEY_WIDTH

    def body(lr_ref, w2_ref, ba_ref, g_ref):
        g_ref[...] = _decay_fn(lr_ref[...], w2_ref[...], ba_ref[...])

    return pl.pallas_call(
        body, name="gla_prep_fwd", grid=(t // tr,),
        in_specs=[pl.BlockSpec((tr, LR_PAD), lambda i: (i, 0)),
                  pl.BlockSpec((LR_PAD, width), lambda i: (0, 0)),
                  pl.BlockSpec((1, width), lambda i: (0, 0))],
        out_specs=pl.BlockSpec((tr, width), lambda i: (i, 0)),
        out_shape=jax.ShapeDtypeStruct((t, width), F32),
        compiler_params=_params(("parallel",), VMEM_LIMIT),
    )(proj_lr, w2, ba2)


def _gla_prep_bwd(proj_lr, w2, ba2, dg_f, dg_b, *, tr=512):
    t = proj_lr.shape[0]
    width = 2 * GLA_KEY_WIDTH

    def body(lr_ref, w2_ref, ba_ref, dgf_ref, dgb_ref, dlr_ref, dw2_ref, dba_ref):
        _, vjp = jax.vjp(_decay_fn, lr_ref[...], w2_ref[...], ba_ref[...])
        dlr, dw2, dba = vjp(jnp.concatenate([dgf_ref[...], dgb_ref[...]], axis=1))
        dlr_ref[...] = dlr.astype(BF16)

        @pl.when(pl.program_id(0) == 0)
        def _():
            dw2_ref[...] = jnp.zeros_like(dw2_ref)
            dba_ref[...] = jnp.zeros_like(dba_ref)

        dw2_ref[...] += dw2
        dba_ref[...] += dba

    half = pl.BlockSpec((tr, GLA_KEY_WIDTH), lambda i: (i, 0))
    return pl.pallas_call(
        body, name="gla_prep_bwd", grid=(t // tr,),
        in_specs=[pl.BlockSpec((tr, LR_PAD), lambda i: (i, 0)),
                  pl.BlockSpec((LR_PAD, width), lambda i: (0, 0)),
                  pl.BlockSpec((1, width), lambda i: (0, 0)), half, half],
        out_specs=[pl.BlockSpec((tr, LR_PAD), lambda i: (i, 0)),
                   pl.BlockSpec((LR_PAD, width), lambda i: (0, 0)),
                   pl.BlockSpec((1, width), lambda i: (0, 0))],
        out_shape=[jax.ShapeDtypeStruct((t, LR_PAD), BF16),
                   jax.ShapeDtypeStruct((LR_PAD, width), F32),
                   jax.ShapeDtypeStruct((1, width), F32)],
        compiler_params=_params(("arbitrary",), VMEM_LIMIT),
    )(proj_lr, w2, ba2, dg_f, dg_b)


def _gla_k(h):
    return slice(h * GLA_DK, (h + 1) * GLA_DK)


def _gla_v(h):
    return slice(h * GLA_DV, (h + 1) * GLA_DV)


def _running_sum(x, downward):
    n = x.shape[0]
    row = lax.broadcasted_iota(jnp.int32, x.shape, 0)
    step = 1
    while step < n:
        if downward:
            x = x + jnp.where(row >= step, pltpu.roll(x, step, 0), 0.0)
        else:
            x = x + jnp.where(row < n - step, pltpu.roll(x, n - step, 0), 0.0)
        step *= 2
    return x


@functools.partial(jax.custom_vjp, nondiff_argnums=(1,))
def _cumsum_rows(x, downward):
    return _running_sum(x, downward)


def _cumsum_rows_fwd(x, downward):
    return _running_sum(x, downward), None


def _cumsum_rows_bwd(downward, _, ct):
    return (_running_sum(ct, not downward),)


_cumsum_rows.defvjp(_cumsum_rows_fwd, _cumsum_rows_bwd)


def _gla_chunk(q, k, v, g, state, forward):
    c = GLA_CHUNK
    row = lax.broadcasted_iota(jnp.int32, (c, c), 0)
    col = lax.broadcasted_iota(jnp.int32, (c, c), 1)
    rid = lax.broadcasted_iota(jnp.int32, (c, GLA_DK), 0)
    q = q * (GLA_DK ** -0.5)
    if forward:
        see = row >= col
        upto_ref = rid <= c // 2
    else:
        see = row < col
        upto_ref = rid >= c - 1 - c // 2
    b = _cumsum_rows(g, forward)
    b_last = jnp.sum(g, axis=0, keepdims=True)
    b_ref = jnp.sum(jnp.where(upto_ref, g, 0.0), axis=0, keepdims=True)
    a = bdot(q * jnp.exp(b - b_ref), k * jnp.exp(b_ref - b), "nt")
    a = jnp.where(see, a, 0.0)
    o = bdot(a, v, "nn") + bdot(q * jnp.exp(b), state, "nt")
    new_state = state * jnp.exp(b_last) + bdot(v, k * jnp.exp(b_last - b), "tn")
    return o, new_state


def _gla_fwd(proj, g):
    t = proj.shape[0]
    c, per = GLA_CHUNK, GLA_PER_STEP
    nchunk = t // c
    nstep = nchunk // per
    qb, kb, vb = _seg_block("qg", GLA_KEY_WIDTH), _seg_block("kg", GLA_KEY_WIDTH), _seg_block("vg", GLA_WIDTH)

    def body(qf, kf, vf, gf, qr, kr, vr, gr, of_ref, ob_ref, sf_ref, sb_ref, state):
        @pl.when(pl.program_id(0) == 0)
        def _():
            state[...] = jnp.zeros_like(state)

        dirs = ((qf, kf, vf, gf, of_ref, sf_ref), (qr, kr, vr, gr, ob_ref, sb_ref))
        states = [[state[d, h] for h in range(GLA_HEADS)] for d in range(2)]
        for turn in range(per):
            chunk = (turn, per - 1 - turn)
            args = []
            for d, (q_ref, k_ref, v_ref, g_ref, _, _) in enumerate(dirs):
                rows = pl.ds(chunk[d] * c, c)
                args += [(q_ref[rows, _gla_k(h)], k_ref[rows, _gla_k(h)], v_ref[rows, _gla_v(h)],
                          g_ref[rows, _gla_k(h)], states[d][h]) for h in range(GLA_HEADS)]
            results = [_gla_chunk(*a, forward=(i < GLA_HEADS)) for i, a in enumerate(args)]
            for i, (a, (o, s_out)) in enumerate(zip(args, results)):
                d, h = divmod(i, GLA_HEADS)
                dirs[d][5][chunk[d], h] = a[4]
                dirs[d][4][pl.ds(chunk[d] * c, c), _gla_v(h)] = o
                states[d][h] = s_out
        for d in range(2):
            for h in range(GLA_HEADS):
                state[d, h] = states[d][h]

    specs, outs = [], []
    for d in range(2):
        ci = (lambda i: i) if d == 0 else (lambda i: nstep - 1 - i)
        specs += [pl.BlockSpec((per * c, GLA_KEY_WIDTH), lambda i, ci=ci: (ci(i), qb)),
                  pl.BlockSpec((per * c, GLA_KEY_WIDTH), lambda i, ci=ci: (ci(i), kb)),
                  pl.BlockSpec((per * c, GLA_WIDTH), lambda i, ci=ci: (ci(i), vb)),
                  pl.BlockSpec((per * c, GLA_KEY_WIDTH), lambda i, ci=ci, d=d: (ci(i), d))]
        outs.append(pl.BlockSpec((per * c, GLA_WIDTH), lambda i, ci=ci: (ci(i), 0)))
    for d in range(2):
        ci = (lambda i: i) if d == 0 else (lambda i: nstep - 1 - i)
        outs.append(pl.BlockSpec((per, GLA_HEADS, GLA_DV, GLA_DK), lambda i, ci=ci: (ci(i), 0, 0, 0)))
    o_shape = jax.ShapeDtypeStruct((t, GLA_WIDTH), F32)
    s_shape = jax.ShapeDtypeStruct((nchunk, GLA_HEADS, GLA_DV, GLA_DK), F32)
    return pl.pallas_call(
        body, name="gla_fwd", grid=(nstep,),
        in_specs=specs, out_specs=outs,
        out_shape=[o_shape, o_shape, s_shape, s_shape],
        scratch_shapes=[pltpu.VMEM((2, GLA_HEADS, GLA_DV, GLA_DK), F32)],
        compiler_params=_params(("arbitrary",), VMEM_LIMIT),
    )(proj, proj, proj, g, proj, proj, proj, g)


def _gla_bwd(proj, g, s_f, s_b, do):
    t = proj.shape[0]
    c, per = GLA_CHUNK, GLA_PER_STEP
    nchunk = t // c
    nstep = nchunk // per
    qb, kb, vb = _seg_block("qg", GLA_KEY_WIDTH), _seg_block("kg", GLA_KEY_WIDTH), _seg_block("vg", GLA_WIDTH)

    def body(*refs):
        ins, outs, dstate = refs[:12], refs[12:20], refs[20]

        @pl.when(pl.program_id(0) == 0)
        def _():
            dstate[...] = jnp.zeros_like(dstate)

        dstates = [[dstate[d, h] for h in range(GLA_HEADS)] for d in range(2)]
        for turn in range(per):
            chunk = (per - 1 - turn, turn)
            loaded = []
            for d in range(2):
                q_ref, k_ref, v_ref, g_ref, s_ref, do_ref = ins[6 * d:6 * d + 6]
                rows = pl.ds(chunk[d] * c, c)
                for h in range(GLA_HEADS):
                    loaded.append(((q_ref[rows, _gla_k(h)], k_ref[rows, _gla_k(h)], v_ref[rows, _gla_v(h)],
                                    g_ref[rows, _gla_k(h)], s_ref[chunk[d], h]),
                                   (do_ref[rows, _gla_v(h)], dstates[d][h])))
            grads = []
            for i, (primals, cotangents) in enumerate(loaded):
                _, vjp = jax.vjp(functools.partial(_gla_chunk, forward=(i < GLA_HEADS)), *primals)
                grads.append(vjp(cotangents))
            for i, (dq, dk, dv, dg, ds) in enumerate(grads):
                d, h = divmod(i, GLA_HEADS)
                rows = pl.ds(chunk[d] * c, c)
                dq_ref, dk_ref, dv_ref, dg_ref = outs[4 * d:4 * d + 4]
                dq_ref[rows, _gla_k(h)] = dq
                dk_ref[rows, _gla_k(h)] = dk
                dv_ref[rows, _gla_v(h)] = dv
                dg_ref[rows, _gla_k(h)] = dg
                dstates[d][h] = ds
        for d in range(2):
            for h in range(GLA_HEADS):
                dstate[d, h] = dstates[d][h]

    specs, outs, shapes = [], [], []
    for d in range(2):
        ci = (lambda i: nstep - 1 - i) if d == 0 else (lambda i: i)
        specs += [pl.BlockSpec((per * c, GLA_KEY_WIDTH), lambda i, ci=ci: (ci(i), qb)),
                  pl.BlockSpec((per * c, GLA_KEY_WIDTH), lambda i, ci=ci: (ci(i), kb)),
                  pl.BlockSpec((per * c, GLA_WIDTH), lambda i, ci=ci: (ci(i), vb)),
                  pl.BlockSpec((per * c, GLA_KEY_WIDTH), lambda i, ci=ci, d=d: (ci(i), d)),
                  pl.BlockSpec((per, GLA_HEADS, GLA_DV, GLA_DK), lambda i, ci=ci: (ci(i), 0, 0, 0)),
                  pl.BlockSpec((per * c, GLA_WIDTH), lambda i, ci=ci: (ci(i), 0))]
        key = pl.BlockSpec((per * c, GLA_KEY_WIDTH), lambda i, ci=ci: (ci(i), 0))
        val = pl.BlockSpec((per * c, GLA_WIDTH), lambda i, ci=ci: (ci(i), 0))
        outs += [key, key, val, key]
        shapes += [jax.ShapeDtypeStruct((t, GLA_KEY_WIDTH), F32), jax.ShapeDtypeStruct((t, GLA_KEY_WIDTH), F32),
                   jax.ShapeDtypeStruct((t, GLA_WIDTH), F32), jax.ShapeDtypeStruct((t, GLA_KEY_WIDTH), F32)]
    return pl.pallas_call(
        body, name="gla_bwd", grid=(nstep,),
        in_specs=specs, out_specs=outs, out_shape=shapes,
        scratch_shapes=[pltpu.VMEM((2, GLA_HEADS, GLA_DV, GLA_DK), F32)],
        compiler_params=_params(("arbitrary",), VMEM_LIMIT),
    )(proj, proj, proj, g, s_f, do, proj, proj, proj, g, s_b, do)


def _gla_out_head(o_f, o_b, gate, gn):
    return _rms(o_f + o_b, gn) * _silu(gate)


def _gla_out_fwd(o_f, o_b, proj, gn, *, tr=512):
    t = o_f.shape[0]
    gb = _seg_block("gate", GLA_WIDTH)

    def body(of_ref, ob_ref, gate_ref, gn_ref, out_ref):
        for h in range(GLA_HEADS):
            vc = slice(h * GLA_DV, (h + 1) * GLA_DV)
            out_ref[:, vc] = _gla_out_head(of_ref[:, vc], ob_ref[:, vc], gate_ref[:, vc], gn_ref[...]).astype(BF16)

    wide = pl.BlockSpec((tr, GLA_WIDTH), lambda i: (i, 0))
    return pl.pallas_call(
        body, name="gla_out_fwd", grid=(t // tr,),
        in_specs=[wide, wide, pl.BlockSpec((tr, GLA_WIDTH), lambda i: (i, gb)),
                  pl.BlockSpec((1, GLA_DV), lambda i: (0, 0))],
        out_specs=wide,
        out_shape=jax.ShapeDtypeStruct((t, GLA_WIDTH), BF16),
        compiler_params=_params(("parallel",), VMEM_LIMIT),
    )(o_f, o_b, proj, gn)


def _gla_out_bwd(o_f, o_b, proj, gn, dmix, *, tr=512):
    t = o_f.shape[0]
    gb = _seg_block("gate", GLA_WIDTH)

    def body(of_ref, ob_ref, gate_ref, gn_ref, dout_ref, do_ref, dgate_ref, dgn_ref):
        dgn = jnp.zeros((1, GLA_DV), F32)
        for h in range(GLA_HEADS):
            vc = slice(h * GLA_DV, (h + 1) * GLA_DV)
            _, vjp = jax.vjp(_gla_out_head, of_ref[:, vc], ob_ref[:, vc], gate_ref[:, vc], gn_ref[...])
            do, _, dgate, dg = vjp(dout_ref[:, vc])
            do_ref[:, vc] = do
            dgate_ref[:, vc] = dgate.astype(BF16)
            dgn = dgn + dg

        @pl.when(pl.program_id(0) == 0)
        def _():
            dgn_ref[...] = jnp.zeros_like(dgn_ref)

        dgn_ref[...] += dgn

    wide = pl.BlockSpec((tr, GLA_WIDTH), lambda i: (i, 0))
    vec = pl.BlockSpec((1, GLA_DV), lambda i: (0, 0))
    return pl.pallas_call(
        body, name="gla_out_bwd", grid=(t // tr,),
        in_specs=[wide, wide, pl.BlockSpec((tr, GLA_WIDTH), lambda i: (i, gb)), vec,
                  pl.BlockSpec((tr, GLA_WIDTH), lambda i: (i, 1))],
        out_specs=[wide, wide, vec],
        out_shape=[jax.ShapeDtypeStruct((t, GLA_WIDTH), F32), jax.ShapeDtypeStruct((t, GLA_WIDTH), BF16),
                   jax.ShapeDtypeStruct((1, GLA_DV), F32)],
        compiler_params=_params(("arbitrary",), VMEM_LIMIT),
    )(o_f, o_b, proj, gn, dmix)


CONV_TR = 1024
CONV_TC = 512
HALO = 8
HALO16 = 16


def _conv3(u, w, b):
    n = u.shape[0]
    return pltpu.roll(u, 1, 0) * w[0:1] + u * w[1:2] + pltpu.roll(u, n - 1, 0) * w[2:3] + b


def _conv_ext(main_ref, prev_ref, next_ref, r, nr):
    prev = prev_ref[...].astype(F32)[-HALO:] * (r > 0).astype(F32)
    nxt = next_ref[...].astype(F32)[:HALO] * (r < nr - 1).astype(F32)
    return jnp.concatenate([prev, main_ref[...].astype(F32), nxt], axis=0)


def _conv_specs(t, halo, half=None):
    per = CONV_TR // halo
    last = t // halo - 1
    lead = () if half is None else (None,)
    at = (lambda *ix: ix) if half is None else (lambda *ix: (half,) + ix)
    return [pl.BlockSpec(lead + (CONV_TR, CONV_TC), lambda j, r: at(r, j)),
            pl.BlockSpec(lead + (halo, CONV_TC), lambda j, r: at(jnp.maximum(r * per - 1, 0), j)),
            pl.BlockSpec(lead + (halo, CONV_TC), lambda j, r: at(jnp.minimum((r + 1) * per, last), j))]


def _ffn_mid_fwd(u, cw_g, cw_v, cb_g, cb_v):
    _, t, f = u.shape
    nr = t // CONV_TR

    def body(ug, ugp, ugn, uv, uvp, uvn, wg, wv, bg, bv, a_ref):
        r = pl.program_id(1)
        gate = _conv3(_conv_ext(ug, ugp, ugn, r, nr), wg[...], bg[...])[HALO:HALO + CONV_TR]
        val = _conv3(_conv_ext(uv, uvp, uvn, r, nr), wv[...], bv[...])[HALO:HALO + CONV_TR]
        a_ref[...] = (_silu(gate) * val).astype(BF16)

    w_spec = pl.BlockSpec((3, CONV_TC), lambda j, r: (0, j))
    b_spec = pl.BlockSpec((1, CONV_TC), lambda j, r: (0, j))
    return pl.pallas_call(
        body, name="ffn_mid_fwd", grid=(f // CONV_TC, nr),
        in_specs=_conv_specs(t, HALO, 0) + _conv_specs(t, HALO, 1) + [w_spec, w_spec, b_spec, b_spec],
        out_specs=pl.BlockSpec((CONV_TR, CONV_TC), lambda j, r: (r, j)),
        out_shape=jax.ShapeDtypeStruct((t, f), BF16),
        compiler_params=_params(("parallel", "parallel"), VMEM_LIMIT),
    )(u, u, u, u, u, u, cw_g, cw_v, cb_g, cb_v)


def _ffn_mid_bwd(u, cw_g, cw_v, cb_g, cb_v, da):
    _, t, f = u.shape
    nr = t // CONV_TR
    ext = CONV_TR + 2 * HALO

    def body(ug, ugp, ugn, uv, uvp, uvn, dam, dap, dan, wg, wv, bg, bv,
             du_ref, dwg_ref, dwv_ref, dbg_ref, dbv_ref):
        r = pl.program_id(1)
        shifted = []
        for main, prev, nxt in ((ug, ugp, ugn), (uv, uvp, uvn)):
            x = _conv_ext(main, prev, nxt, r, nr)
            shifted.append((pltpu.roll(x, 1, 0), x, pltpu.roll(x, ext - 1, 0)))
        da_x = _conv_ext(dam, dap, dan, r, nr)
        wg_t, wv_t = wg[...], wv[...]
        gate = shifted[0][0] * wg_t[0:1] + shifted[0][1] * wg_t[1:2] + shifted[0][2] * wg_t[2:3] + bg[...]
        val = shifted[1][0] * wv_t[0:1] + shifted[1][1] * wv_t[1:2] + shifted[1][2] * wv_t[2:3] + bv[...]
        sig = jax.nn.sigmoid(gate)
        silu = gate * sig
        d_val = da_x * silu
        d_gate = da_x * val * (sig + silu * (1.0 - sig))
        own = slice(HALO, HALO + CONV_TR)
        for half, (xs3, d, wt, dw_ref, db_ref) in enumerate(((shifted[0], d_gate, wg_t, dwg_ref, dbg_ref),
                                                            (shifted[1], d_val, wv_t, dwv_ref, dbv_ref))):
            du = pltpu.roll(d, ext - 1, 0) * wt[0:1] + d * wt[1:2] + pltpu.roll(d, 1, 0) * wt[2:3]
            du_ref[half] = du[own].astype(BF16)
            d_own = d[own]
            dw = jnp.concatenate([jnp.sum(x[own] * d_own, axis=0, keepdims=True) for x in xs3], axis=0)
            db = jnp.sum(d_own, axis=0, keepdims=True)

            @pl.when(r == 0)
            def _():
                dw_ref[...] = jnp.zeros_like(dw_ref)
                db_ref[...] = jnp.zeros_like(db_ref)

            dw_ref[...] += dw
            db_ref[...] += db

    w_spec = pl.BlockSpec((3, CONV_TC), lambda j, r: (0, j))
    b_spec = pl.BlockSpec((1, CONV_TC), lambda j, r: (0, j))
    return pl.pallas_call(
        body, name="ffn_mid_bwd", grid=(f // CONV_TC, nr),
        in_specs=(_conv_specs(t, HALO, 0) + _conv_specs(t, HALO, 1) + _conv_specs(t, HALO16)
                  + [w_spec, w_spec, b_spec, b_spec]),
        out_specs=[pl.BlockSpec((2, CONV_TR, CONV_TC), lambda j, r: (0, r, j)), w_spec, w_spec, b_spec, b_spec],
        out_shape=[jax.ShapeDtypeStruct((2, t, f), BF16),
                   jax.ShapeDtypeStruct((3, f), F32), jax.ShapeDtypeStruct((3, f), F32),
                   jax.ShapeDtypeStruct((1, f), F32), jax.ShapeDtypeStruct((1, f), F32)],
        compiler_params=_params(("parallel", "arbitrary"), VMEM_LIMIT),
    )(u, u, u, u, u, u, da, da, da, cw_g, cw_v, cb_g, cb_v)


def _down_proj_loss(act, w_down, x1, target, *, tm, tn):
    t, f = act.shape
    d = w_down.shape[1]

    def body(a_ref, w_ref, x_ref, t_ref, loss_ref, dy_ref, dyb_ref):
        y = lax.dot_general(a_ref[...], w_ref[...], _DIMS["nn"], preferred_element_type=F32) + x_ref[...]
        err = y - t_ref[...]
        dy = err * (1.0 / d)
        dy_ref[...] = dy
        dyb_ref[...] = dy.astype(BF16)
        part = 0.5 * jnp.sum(jnp.sum(err * err, axis=-1, keepdims=True) * (1.0 / d), axis=0, keepdims=True)

        @pl.when((pl.program_id(0) == 0) & (pl.program_id(1) == 0))
        def _():
            loss_ref[...] = jnp.zeros_like(loss_ref)

        loss_ref[...] += jnp.broadcast_to(part, loss_ref.shape)

    tile = pl.BlockSpec((tm, tn), lambda i, j: (i, j))
    return pl.pallas_call(
        body, name="down_proj_loss", grid=(t // tm, d // tn),
        in_specs=[pl.BlockSpec((tm, f), lambda i, j: (i, 0)), pl.BlockSpec((f, tn), lambda i, j: (0, j)), tile, tile],
        out_specs=[pl.BlockSpec((1, 128), lambda i, j: (0, 0)), tile, tile],
        out_shape=[jax.ShapeDtypeStruct((1, 128), F32), jax.ShapeDtypeStruct((t, d), F32),
                   jax.ShapeDtypeStruct((t, d), BF16)],
        compiler_params=_params(("arbitrary", "arbitrary"), VMEM_LIMIT),
    )(act, w_down, x1, target)


ANY = pl.BlockSpec(memory_space=pl.ANY)


def _position():
    return lax.axis_index("x"), lax.axis_index("y"), lax.axis_index("c")


def _other_chips(x, y):
    return [(1 - x, y), (x, 1 - y), (1 - x, 1 - y)]


def _handshake(peers):
    barrier = pltpu.get_barrier_semaphore()
    for peer in peers:
        pl.semaphore_signal(barrier, inc=1, device_id=peer, device_id_type=MESH)
    pl.semaphore_wait(barrier, len(peers))


def _exchange(body, operands, out_shapes, sems, *, name, collective_id):
    n_in, n_out = len(operands), len(out_shapes)

    def run(*refs):
        body(refs[:n_in], refs[n_in:n_in + n_out], *refs[n_in + n_out:])

    if collective_id is None:
        return pl.pallas_call(run, name=name, in_specs=[ANY] * n_in, out_specs=[ANY] * n_out,
                              out_shape=out_shapes, scratch_shapes=sems)(*operands)
    return pl.kernel(run, name=name, out_type=out_shapes,
                     mesh=plsc.ScalarSubcoreMesh(axis_name="sequencer", num_cores=1), scratch_types=sems,
                     compiler_params=pltpu.CompilerParams(collective_id=collective_id))(*operands)


def _all_gather(blocks, *, name, collective_id=None):
    na = len(blocks)

    def body(ins, outs, send_sems, recv_sems, local_sems):
        x, y, c = _position()
        me, sibling = (x, y, c), (x, y, 1 - c)
        along_x, along_y, diagonal = (1 - x, y, c), (x, 1 - y, c), (1 - x, 1 - y, c)
        first_c = c == 0
        relay_from = (jnp.where(first_c, x, 1 - x), jnp.where(first_c, 1 - y, y), c)
        relay_to = (jnp.where(first_c, 1 - x, x), jnp.where(first_c, y, 1 - y), c)
        if collective_id is not None:
            _handshake([sibling, along_x, along_y])

        def index(px, py, pc):
            return 4 * px + 2 * py + pc

        def copy(a, k, block, to, src=None):
            dst = outs[a].at[index(*block)]
            return pltpu.make_async_remote_copy(
                src_ref=dst if src is None else src, dst_ref=dst,
                send_sem=send_sems.at[a, k], recv_sem=recv_sems.at[a, k],
                device_id=to, device_id_type=MESH)

        pending = []
        for a in range(na):
            mine = pltpu.make_async_copy(ins[a], outs[a].at[index(*me)], local_sems.at[a])
            mine.start()
            pending.append(mine)
        sent = []
        for a in range(na):
            sent += [copy(a, 0, me, sibling, src=ins[a]), copy(a, 1, me, along_x, src=ins[a]),
                     copy(a, 2, me, along_y, src=ins[a])]
        for cp in sent:
            cp.start()

        def passes_on(k_in, owner, k_out):
            for a in range(na):
                copy(a, k_in, owner, me).wait_recv()
                cp = copy(a, k_out, owner, sibling)
                cp.start()
                sent.append(cp)

        passes_on(1, along_x, 4)
        passes_on(2, along_y, 5)
        for a in range(na):
            cp = copy(a, 3, relay_from, relay_to)
            cp.start()
            sent.append(cp)
        passes_on(3, diagonal, 6)
        for a in range(na):
            copy(a, 0, sibling, me).wait_recv()
            for k, owner in ((4, along_x), (5, along_y), (6, diagonal)):
                copy(a, k, (owner[0], owner[1], 1 - c), me).wait_recv()
        for cp in sent:
            cp.wait_send()
        for cp in pending:
            cp.wait()

    return _exchange(
        body, blocks, [jax.ShapeDtypeStruct((N_DEV,) + b.shape, b.dtype) for b in blocks],
        [pltpu.SemaphoreType.DMA((na, 7)), pltpu.SemaphoreType.DMA((na, 7)), pltpu.SemaphoreType.DMA((na,))],
        name=name, collective_id=collective_id)


def _grad_exchange(grads, parts, *, name, collective_id):
    ng, npart = len(grads), len(parts)

    def body(ins, outs, core_send, core_recv, chip_send, chip_recv, local_sems):
        x, y, c = _position()
        sibling = (x, y, 1 - c)
        chips = _other_chips(x, y)
        _handshake([sibling] + [(px, py, c) for px, py in chips])
        me = 2 * x + y
        copies = []
        for b in range(npart):
            src, dst = ins[ng + b], outs[ng + b]
            own = pltpu.make_async_copy(src.at[me], dst.at[me], local_sems.at[b])
            own.start()
            copies.append(own)
            for j, (px, py) in enumerate(chips):
                cp = pltpu.make_async_remote_copy(
                    src_ref=src.at[2 * px + py], dst_ref=dst.at[me],
                    send_sem=chip_send.at[b, j], recv_sem=chip_recv.at[b, j],
                    device_id=(px, py, c), device_id_type=MESH)
                cp.start()
                copies.append(cp)
        for a in range(ng):
            for k in range(N_CHIP):
                src = ins[a].at[k, 1 - c] if len(grads[a].shape) == 4 else ins[a].at[k]
                cp = pltpu.make_async_remote_copy(
                    src_ref=src, dst_ref=outs[a].at[k],
                    send_sem=core_send.at[a, k], recv_sem=core_recv.at[a, k],
                    device_id=sibling, device_id_type=MESH)
                cp.start()
                copies.append(cp)
        for cp in copies:
            cp.wait()

    shapes = ([jax.ShapeDtypeStruct((N_CHIP,) + g.shape[-2:], g.dtype) for g in grads]
              + [jax.ShapeDtypeStruct(p.shape, p.dtype) for p in parts])
    sems = [pltpu.SemaphoreType.DMA((max(ng, 1), N_CHIP)), pltpu.SemaphoreType.DMA((max(ng, 1), N_CHIP)),
            pltpu.SemaphoreType.DMA((max(npart, 1), 3)), pltpu.SemaphoreType.DMA((max(npart, 1), 3)),
            pltpu.SemaphoreType.DMA((max(npart, 1),))]
    out = _exchange(body, list(grads) + list(parts), shapes, sems, name=name, collective_id=collective_id)
    return out[:ng], out[ng:]


def _pair_sum(grad, theirs, core, *, tile, name, narrow=False):
    _, _, r, w = grad.shape
    tr, tw = tile
    assert r % tr == 0 and w % tw == 0

    def body(core_ref, mine_ref, theirs_ref, out_ref, *narrow_ref):
        total = mine_ref[...] + theirs_ref[...]
        out_ref[...] = total
        if narrow:
            narrow_ref[0][...] = total.astype(BF16)

    spec = pl.BlockSpec((None, tr, tw), lambda k, i, j, core_ref: (k, i, j))
    shapes = [jax.ShapeDtypeStruct((N_CHIP, r, w), F32)] + [jax.ShapeDtypeStruct((N_CHIP, r, w), BF16)] * narrow
    out = pl.pallas_call(
        body, name=name,
        grid_spec=pltpu.PrefetchScalarGridSpec(
            num_scalar_prefetch=1, grid=(N_CHIP, r // tr, w // tw),
            in_specs=[pl.BlockSpec((None, None, tr, tw), lambda k, i, j, core_ref: (k, core_ref[0], i, j)), spec],
            out_specs=[spec] * len(shapes)),
        out_shape=shapes,
        compiler_params=_params(("parallel", "parallel", "parallel"), VMEM_LIMIT),
    )(core, grad, theirs)
    return tuple(out) if narrow else out[0]


class _ReduceScatter:
    def __init__(self, core):
        self.core = core
        self.pending = None
        self.results = {}
        self.launches = 0

    def push(self, tag, grads, rows, then, narrow=False, finish=None):
        pair, kept, prev_tag = [], [], None
        if self.pending is not None:
            prev_tag, prev, theirs, prev_rows, prev_narrow, prev_finish = self.pending
            if prev_finish is not None:
                pair = prev_finish(theirs)
            else:
                pair = [_pair_sum(g, s, self.core, tile=tile, name=f"pair_sum_{prev_tag}_{i}", narrow=prev_narrow)
                        for i, (g, s, tile) in enumerate(zip(prev, theirs, prev_rows))]
            if prev_narrow:
                kept, pair = [p[0] for p in pair], [p[1] for p in pair]
        grads, pair, then = lax.optimization_barrier((list(grads), pair, then))
        if finish is None:
            grads = [g.reshape((N_CHIP, 2) + g.shape[1:]) for g in grads]
        self.launches += 1
        theirs, parts = _grad_exchange(grads, pair, name=f"grad_exchange_{self.launches}",
                                       collective_id=1 + self.launches)
        if prev_tag is not None:
            self.results[prev_tag] = (parts, kept)
        self.pending = (tag, grads, theirs, rows, narrow, finish) if tag is not None else None
        return then

    def result(self, tag):
        return self.results[tag]


def _adamw(parts, w, m, v, *, tile, name, own=None, chip=None):
    n, r, cols = parts.shape
    tr, tw = tile
    assert r % tr == 0 and cols % tw == 0 and w.shape == (r, cols)
    c1 = 1.0 - ADAM_B1 ** ADAM_STEP
    c2 = 1.0 - ADAM_B2 ** ADAM_STEP

    def update(g, w_ref, m_ref, v_ref, g_ref, d_ref, nm_ref, nv_ref):
        new_m = ADAM_B1 * m_ref[...] + (1.0 - ADAM_B1) * g
        new_v = ADAM_B2 * v_ref[...] + (1.0 - ADAM_B2) * (g * g)
        m_hat = new_m / c1
        v_hat = new_v / c2
        g_ref[...] = g
        d_ref[...] = -ADAM_LR * (m_hat / (jnp.sqrt(v_hat) + ADAM_EPS) + ADAM_WD * w_ref[...])
        nm_ref[...] = new_m
        nv_ref[...] = new_v

    shape = jax.ShapeDtypeStruct((r, cols), F32)
    if own is None:
        def body(p_ref, *refs):
            g = p_ref[0]
            for k in range(1, n):
                g = g + p_ref[k]
            update(g, *refs)

        spec = pl.BlockSpec((tr, tw), lambda i, j: (i, j))
        return pl.pallas_call(
            body, name=name, grid=(r // tr, cols // tw),
            in_specs=[pl.BlockSpec((n, tr, tw), lambda i, j: (0, i, j)), spec, spec, spec],
            out_specs=[spec] * 4, out_shape=[shape] * 4,
            compiler_params=_params(("parallel", "parallel"), VMEM_LIMIT),
        )(parts, w, m, v)

    def body(chip_ref, p_ref, own_ref, *refs):
        g = None
        for k in range(n):
            term = jnp.where(chip_ref[0] == k, own_ref[...], p_ref[k].astype(F32))
            g = term if g is None else g + term
        update(g, *refs)

    spec = pl.BlockSpec((tr, tw), lambda i, j, chip_ref: (i, j))
    return pl.pallas_call(
        body, name=name,
        grid_spec=pltpu.PrefetchScalarGridSpec(
            num_scalar_prefetch=1, grid=(r // tr, cols // tw),
            in_specs=[pl.BlockSpec((n, tr, tw), lambda i, j, chip_ref: (0, i, j)),
                      pl.BlockSpec((None, tr, tw), lambda i, j, chip_ref: (chip_ref[0], i, j)), spec, spec, spec],
            out_specs=[spec] * 4),
        out_shape=[shape] * 4,
        compiler_params=_params(("parallel", "parallel"), VMEM_LIMIT),
    )(chip, parts, own, w, m, v)


LANES = 128


def _row_offsets(pieces):
    offsets, row = [], 0
    for p in pieces:
        assert p.shape[0] == 1 and p.shape[1] % LANES == 0, p.shape
        offsets.append(row)
        row += p.shape[1] // LANES
    return offsets, row


def _pack_rows(pieces):
    offsets, rows = _row_offsets(pieces)

    def body(*refs):
        out_ref = refs[-1]
        for ref, start in zip(refs[:-1], offsets):
            for j in range(ref.shape[1] // LANES):
                out_ref[start + j:start + j + 1, :] = ref[:, j * LANES:(j + 1) * LANES]

    return pl.pallas_call(body, name="pack_small_grads",
                          out_shape=jax.ShapeDtypeStruct((rows, LANES), F32))(*pieces)


def _adamw_rows(terms, ws, ms, vs):
    n_dev, rows, _ = terms.shape
    offsets, used = _row_offsets(ws)
    assert used + 1 == rows
    c1 = 1.0 - ADAM_B1 ** ADAM_STEP
    c2 = 1.0 - ADAM_B2 ** ADAM_STEP
    nw = len(ws)

    def body(*refs):
        t_ref = refs[0]
        w_refs, m_refs, v_refs = refs[1:1 + nw], refs[1 + nw:1 + 2 * nw], refs[1 + 2 * nw:1 + 3 * nw]
        outs = refs[1 + 3 * nw:]
        total = t_ref[0]
        for k in range(1, n_dev):
            total = total + t_ref[k]
        for i, start in enumerate(offsets):
            for j in range(ws[i].shape[1] // LANES):
                lanes = slice(j * LANES, (j + 1) * LANES)
                g = total[start + j:start + j + 1, :]
                new_m = ADAM_B1 * m_refs[i][:, lanes] + (1.0 - ADAM_B1) * g
                new_v = ADAM_B2 * v_refs[i][:, lanes] + (1.0 - ADAM_B2) * (g * g)
                delta = -ADAM_LR * ((new_m / c1) / (jnp.sqrt(new_v / c2) + ADAM_EPS) + ADAM_WD * w_refs[i][:, lanes])
                for kind, value in enumerate((g, delta, new_m, new_v)):
                    outs[kind * nw + i][:, lanes] = value
        outs[-1][...] = total[used:used + 1, :]

    shapes = [jax.ShapeDtypeStruct(w.shape, F32) for w in ws] * 4 + [jax.ShapeDtypeStruct((1, LANES), F32)]
    out = pl.pallas_call(body, name="adamw_replicated", out_shape=shapes,
                         compiler_params=_params(None, VMEM_LIMIT))(terms, *ws, *ms, *vs)
    return [list(out[kind * nw:(kind + 1) * nw]) for kind in range(4)], out[-1]


def _rope_tables(t):
    half = HEAD_DIM // 2
    inv = 1.0 / (ROPE_THETA ** (jnp.arange(half, dtype=F32) / half))
    ang = jnp.arange(t, dtype=jnp.int32).astype(F32)[:, None] * inv[None, :]
    cos, sin = jnp.cos(ang), jnp.sin(ang)
    return jnp.concatenate([cos, cos], axis=1), jnp.concatenate([-sin, sin], axis=1)


IN_KERNEL = IN_MAIN + LR_PAD


def _to_kernel_rows(w_t):
    order = sorted(SEGMENTS.values())
    pad = jnp.zeros((LR_PAD - (IN_TOTAL - IN_MAIN), w_t.shape[1]), w_t.dtype)
    return jnp.concatenate([w_t[src:src + width] for _, src, width in order] + [w_t[IN_MAIN:IN_TOTAL], pad], axis=0)


CONV_TAPS = 3
WA_BLOCK = GLA_KEY_WIDTH // N_DEV
SMALL_SIZES = (CONV_TAPS * UP_BLOCK, GLA_RANK * WA_BLOCK, GLA_RANK * WA_BLOCK)
SMALL_SHAPES = ((CONV_TAPS, UP_BLOCK), (GLA_RANK, WA_BLOCK), (GLA_RANK, WA_BLOCK))
SMALL_ROWS = sum(SMALL_SIZES) // LANES


def _small_block(conv, wa_f, wa_b):
    return jnp.concatenate([conv.reshape(-1), wa_f.reshape(-1), wa_b.reshape(-1)]).reshape(SMALL_ROWS, LANES)


def _small_unblock(block):
    flat, out, off = block.reshape(-1), [], 0
    for size, shape in zip(SMALL_SIZES, SMALL_SHAPES):
        out.append(flat[off:off + size].reshape(shape))
        off += size
    return out


def _small_blocks(conv_full, wa_f_full, wa_b_full):
    def by_device(a, width):
        return jnp.transpose(a.reshape(a.shape[0], N_DEV, width), (1, 0, 2)).reshape(N_DEV, -1)
    return jnp.concatenate([by_device(conv_full, UP_BLOCK), by_device(wa_f_full, WA_BLOCK),
                            by_device(wa_b_full, WA_BLOCK)], axis=1).reshape(N_DEV, SMALL_ROWS, LANES)


def _small_unblocks(blocks):
    flat, out, off = blocks.reshape(N_DEV, -1), [], 0
    for size, (rows, width) in zip(SMALL_SIZES, SMALL_SHAPES):
        part = flat[:, off:off + size].reshape(N_DEV, rows, width)
        out.append(jnp.transpose(part, (1, 0, 2)).reshape(rows, N_DEV * width))
        off += size
    return out


def _local_step(xs, target, norm1_g, w_in8, gq, gk, attn_sink, w2, ba2, gla_norm_g, w_out_full, norm2_g,
                w_up8, cw_g, cw_v, cb_g, cb_v, w_down_full, rs=None):
    t = xs.shape[0]
    tm = min(1024, t)
    tall = min(2048, t)
    cos, sin_signed = _rope_tables(t)
    sink = attn_sink.reshape(ATTN_HEADS)

    h1 = _rmsnorm_fwd(xs, norm1_g, name="norm1_fwd")
    w_in8, h1, cos, sin_signed = lax.optimization_barrier((w_in8, h1, cos, sin_signed))
    w_in_k = _to_kernel_rows(w_in8.reshape(IN_TOTAL, D_MODEL))
    proj = _matmul(h1, w_in_k, "nt", tm=tall, tn=IN_MAIN // 4, tk=D_MODEL, n_out=IN_MAIN, name="proj_main")
    proj_lr = _matmul(h1, w_in_k[IN_MAIN:], "nt", tm=tm, tn=LR_PAD, tk=D_MODEL, name="proj_lr")
    qa, ka, va = _attn_prep_fwd(proj, cos, sin_signed, gq, gk)
    assert t >= 2 * ATTN_BLOCK
    attn_bias = _attn_bias()
    o_attn = _attn_fwd(qa, ka, va, sink, attn_bias)
    g_dec = _gla_prep_fwd(proj_lr, w2, ba2)
    o_f, o_b, s_f, s_b = _gla_fwd(proj, g_dec)
    o_gla = _gla_out_fwd(o_f, o_b, proj, gla_norm_g)
    x1 = _out_proj(o_attn, o_gla, w_out_full, xs, tm=tall, tn=512)
    h2 = _rmsnorm_fwd(x1, norm2_g, name="norm2_fwd")
    u = _up_proj(h2, w_up8, tm=tm)
    act = _ffn_mid_fwd(u, cw_g, cw_v, cb_g, cb_v)
    loss_part, dy, dy_b = _down_proj_loss(act, w_down_full, x1, target, tm=tm, tn=512)

    d_act = _matmul(dy_b, w_down_full, "nt", tm=tall, tn=D_FF // 4, tk=D_MODEL, out_dtype=BF16, name="d_act")
    dw_down = _matmul(act, dy_b, "tn", tm=D_FF // 4, tn=512, tk=t, name="dw_down")
    if rs is not None:
        d_act = rs.push("w_down", [dw_down.reshape(N_DEV, D_FF // N_DEV, D_MODEL)],
                        [(D_FF // N_DEV // 2, D_MODEL)], d_act)
    du, dcw_g, dcw_v, dcb_g, dcb_v = _ffn_mid_bwd(u, cw_g, cw_v, cb_g, cb_v, d_act)
    if rs is None:
        dw_up8 = _up_proj_dw(h2, du, tm=512, tk=t)
    else:
        dw_up8 = None
        theirs_first = _up_proj_dw_core(h2, du, 1 - rs.core, tm=512, tk=t, name="up_proj_dw_sibling")
        du = rs.push("w_up", [theirs_first], None, du, finish=lambda got, h2=h2, du=du: [
            _up_proj_dw_core(h2, du, rs.core, tm=512, tk=t, res=got[0], name="up_proj_dw_own")])
    dh2 = _up_proj_dx(du, w_up8, tm=tm, tn=1024)
    dx1, dx1_b, d_norm2 = _rmsnorm_bwd(x1, norm2_g, dh2, dy, name="norm2_bwd", also_bf16=True)
    dmix = _matmul(dx1_b, w_out_full, "nt", tm=tall, tn=1024, tk=D_MODEL, name="d_mix")
    dw_out = _out_proj_dw(o_attn, o_gla, dx1_b, tn=512)
    if rs is not None:
        dmix = rs.push("w_out", [dw_out.reshape(N_DEV, D_MODEL // N_DEV, D_MODEL)], [(256, D_MODEL)], dmix)
    do_gla, d_gate, d_gla_norm = _gla_out_bwd(o_f, o_b, proj, gla_norm_g, dmix)
    (dq_f, dk_f, dv_f, dg_f, dq_b, dk_b, dv_b, dg_b) = _gla_bwd(proj, g_dec, s_f, s_b, do_gla)
    d_lr, d_w2, d_ba2 = _gla_prep_bwd(proj_lr, w2, ba2, dg_f, dg_b)
    dqa, dk_lo, dk_mid, dk_hi, dv_lo, dv_mid, dv_hi, d_sink8 = _attn_bwd(qa, ka, va, sink, attn_bias, dmix)
    d_qa, d_ka, d_va, d_qn, d_kn = _attn_prep_bwd(proj, cos, sin_signed, gq, gk, dqa,
                                                  (dk_lo, dk_mid, dk_hi), (dv_lo, dv_mid, dv_hi))
    d_seg = {"qa": d_qa, "gate": d_gate, "vg": (dv_f + dv_b).astype(BF16), "qg": (dq_f + dq_b).astype(BF16),
             "kg": (dk_f + dk_b).astype(BF16), "ka": d_ka, "va": d_va}
    d_proj = jnp.concatenate([d_seg[k] for k in sorted(SEGMENTS, key=lambda k: SEGMENTS[k][0])] + [d_lr], axis=1)
    dw_in_t = _in_proj_dw_lr(_in_proj_dw(d_proj, h1, tn=1024), d_lr, h1)
    if rs is not None:
        per_in = IN_TOTAL // N_DEV
        small_grad = _small_blocks(jnp.concatenate([dcw_g, dcw_v], axis=1), d_w2[:GLA_RANK, :GLA_KEY_WIDTH],
                                   d_w2[GLA_RANK:2 * GLA_RANK, GLA_KEY_WIDTH:])
        d_proj, d_lr = rs.push("w_in", [dw_in_t.reshape(N_DEV, per_in, D_MODEL), small_grad],
                               [(per_in, 1024), small_grad.shape[1:]], (d_proj, d_lr), narrow=True)
    dh1 = _matmul(d_proj, w_in_k, "nn", tm=tm, tn=1024, tk=IN_KERNEL, name="dh1")
    if rs is not None:
        dh1 = rs.push(None, [], [], dh1)
    grad_x, d_norm1 = _rmsnorm_bwd(xs, norm1_g, dh1, dx1, name="norm1_bwd", also_bf16=False)
    return (loss_part, grad_x, dw_in_t, dw_out, dw_up8, dw_down, dcw_g, dcw_v, dcb_g, dcb_v,
            d_w2, d_ba2, d_norm1, d_norm2, d_qn, d_kn, d_sink8, d_gla_norm)


def kernel(x, norm1_g, w_in, attn_q_norm_g, attn_k_norm_g, attn_sink, gla_wa2_fwd, gla_ba_fwd, gla_wa2_bwd, gla_ba_bwd, gla_out_norm_g, w_out, norm2_g, w_up, conv_w, conv_b, w_down, loss_target, m_norm1_g, m_w_in, m_attn_q_norm_g, m_attn_k_norm_g, m_attn_sink, m_gla_wa2_fwd, m_gla_ba_fwd, m_gla_wa2_bwd, m_gla_ba_bwd, m_gla_out_norm_g, m_w_out, m_norm2_g, m_w_up, m_conv_w, m_conv_b, m_w_down, v_norm1_g, v_w_in, v_attn_q_norm_g, v_attn_k_norm_g, v_attn_sink, v_gla_wa2_fwd, v_gla_ba_fwd, v_gla_wa2_bwd, v_gla_ba_bwd, v_gla_out_norm_g, v_w_out, v_norm2_g, v_w_up, v_conv_w, v_conv_b, v_w_down):
    t = x.shape[1]
    xs = x.reshape(t, D_MODEL)
    target = loss_target.reshape(t, D_MODEL)
    core = lax.axis_index("c").astype(jnp.int32).reshape(1)

    w_small = _small_block(conv_w[0], gla_wa2_fwd[0], gla_wa2_bwd[0])
    w_in_t, m_in_t, v_in_t = (jnp.swapaxes(a[0], 0, 1) for a in (w_in, m_w_in, v_w_in))
    g_in, g_small = _all_gather([w_in_t.astype(BF16), w_small], name="gather_w_in", collective_id=8)
    g_out, w_up8, g_down = _all_gather([w_out[0].astype(BF16), w_up[0].astype(BF16), w_down[0].astype(BF16)],
                                       name="gather_later_weights", collective_id=1)
    w_out_full = g_out.reshape(D_MODEL, D_MODEL)
    w_down_full = g_down.reshape(D_FF, D_MODEL)
    conv_w_full, wa2_f, wa2_b = _small_unblocks(g_small)
    cw_g, cw_v = conv_w_full[:, :D_FF], conv_w_full[:, D_FF:]
    cb_g, cb_v = conv_b[:, :D_FF], conv_b[:, D_FF:]
    w2 = jnp.zeros((LR_PAD, 2 * GLA_KEY_WIDTH), F32)
    w2 = w2.at[:GLA_RANK, :GLA_KEY_WIDTH].set(wa2_f).at[GLA_RANK:2 * GLA_RANK, GLA_KEY_WIDTH:].set(wa2_b)
    ba2 = jnp.concatenate([gla_ba_fwd, gla_ba_bwd], axis=1)
    rs = _ReduceScatter(core)
    (loss_part, grad_x, _, _, _, _, _, _, dcb_g, dcb_v, _, d_ba2,
     d_norm1, d_norm2, d_qn, d_kn, d_sink8, d_gla_norm) = _local_step(
        xs, target, norm1_g, g_in, attn_q_norm_g, attn_k_norm_g, attn_sink, w2, ba2, gla_out_norm_g,
        w_out_full, norm2_g, w_up8, cw_g, cw_v, cb_g, cb_v, w_down_full, rs=rs)

    (part_down,), (part_up,), (part_out,) = rs.result("w_down")[0], rs.result("w_up")[0], rs.result("w_out")[0]
    (part_in, part_small), (own_in, own_small) = rs.result("w_in")
    chip = (2 * lax.axis_index("x") + lax.axis_index("y")).astype(jnp.int32).reshape(1)
    m_small = _small_block(m_conv_w[0], m_gla_wa2_fwd[0], m_gla_wa2_bwd[0])
    v_small = _small_block(v_conv_w[0], v_gla_wa2_fwd[0], v_gla_wa2_bwd[0])
    upd_in = _adamw(part_in, w_in_t, m_in_t, v_in_t, tile=(IN_TOTAL // N_DEV, 512), name="adamw_w_in",
                    own=own_in, chip=chip)
    upd_in = [jnp.swapaxes(u, 0, 1) for u in upd_in]
    upd_out = _adamw(part_out, w_out[0], m_w_out[0], v_w_out[0], tile=(256, D_MODEL), name="adamw_w_out")
    upd_up = _adamw(part_up, w_up[0], m_w_up[0], v_w_up[0], tile=(256, UP_BLOCK), name="adamw_w_up")
    upd_down = _adamw(part_down, w_down[0], m_w_down[0], v_w_down[0], tile=(D_FF // N_DEV // 4, D_MODEL),
                      name="adamw_w_down")
    upd_small = _adamw(part_small, w_small, m_small, v_small, tile=part_small.shape[1:],
                       name="adamw_small", own=own_small, chip=chip)
    upd_small = [_small_unblock(u) for u in upd_small]

    rep_names = ["norm1_g", "attn_q_norm_g", "attn_k_norm_g", "attn_sink", "gla_ba_fwd", "gla_ba_bwd",
                 "gla_out_norm_g", "norm2_g", "conv_b"]
    def whole_lanes(sink_like):
        return jnp.pad(sink_like, ((0, 0), (0, LANES - ATTN_HEADS)))

    rep_w = [norm1_g, attn_q_norm_g, attn_k_norm_g, whole_lanes(attn_sink), gla_ba_fwd, gla_ba_bwd, gla_out_norm_g,
             norm2_g, conv_b]
    rep_m = [m_norm1_g, m_attn_q_norm_g, m_attn_k_norm_g, whole_lanes(m_attn_sink), m_gla_ba_fwd, m_gla_ba_bwd,
             m_gla_out_norm_g, m_norm2_g, m_conv_b]
    rep_v = [v_norm1_g, v_attn_q_norm_g, v_attn_k_norm_g, whole_lanes(v_attn_sink), v_gla_ba_fwd, v_gla_ba_bwd,
             v_gla_out_norm_g, v_norm2_g, v_conv_b]
    d_sink = whole_lanes(d_sink8[:, :GQA_GROUP, 0].reshape(1, ATTN_HEADS))
    rep_g = [d_norm1, d_qn, d_kn, d_sink, d_ba2[:, :GLA_KEY_WIDTH], d_ba2[:, GLA_KEY_WIDTH:], d_gla_norm, d_norm2,
             jnp.concatenate([dcb_g, dcb_v], axis=1)]
    (rep_terms,) = _all_gather([_pack_rows(rep_g + [loss_part])], name="gather_small_grads", collective_id=7)
    upd_rep, loss_row = _adamw_rows(rep_terms, rep_w, rep_m, rep_v)
    sink_at = rep_names.index("attn_sink")
    for kind in range(4):
        upd_rep[kind][sink_at] = upd_rep[kind][sink_at][:, :ATTN_HEADS]
    loss = loss_row[0, 0]

    order = ["norm1_g", "w_in", "attn_q_norm_g", "attn_k_norm_g", "attn_sink", "gla_wa2_fwd", "gla_ba_fwd",
             "gla_wa2_bwd", "gla_ba_bwd", "gla_out_norm_g", "w_out", "norm2_g", "w_up", "conv_w", "conv_b", "w_down"]
    outs = [loss, grad_x.reshape(1, t, D_MODEL)]
    for kind in range(4):
        by_name = {n: upd_rep[kind][i] for i, n in enumerate(rep_names)}
        by_name["w_in"] = upd_in[kind][None]
        by_name["w_out"] = upd_out[kind][None]
        by_name["w_up"] = upd_up[kind][None]
        by_name["w_down"] = upd_down[kind][None]
        by_name["conv_w"] = upd_small[kind][0][None]
        by_name["gla_wa2_fwd"] = upd_small[kind][1][None]
        by_name["gla_wa2_bwd"] = upd_small[kind][2][None]
        outs += [by_name[n] for n in order]
    return tuple(outs)
```

```python
import functools

import jax
import jax.numpy as jnp
from jax import lax
from jax.experimental import pallas as pl
from jax.experimental.pallas import tpu as pltpu
from jax.experimental.pallas import tpu_sc as plsc

F32 = jnp.float32
BF16 = jnp.bfloat16

D_MODEL = 2048
HEAD_DIM = 128
ATTN_WIDTH = 1024
ATTN_HEADS = 8
KV_HEADS = 2
GQA_GROUP = 4
KV_WIDTH = KV_HEADS * HEAD_DIM
ATTN_BLOCK = 128
WINDOW = 128
ROPE_THETA = 10000.0
GLA_HEADS = 4
GLA_DK = 128
GLA_DV = 256
GLA_KEY_WIDTH = 512
GLA_WIDTH = 1024
GLA_RANK = 16
GLA_GATE_NORMALIZER = 16.0
GLA_CHUNK = 64
GLA_PER_STEP = 4
D_FF = 5632
NORM_EPS = 1e-6
IN_TOTAL = 4640
IN_MAIN = 4608
LR_PAD = 128
N_DEV = 8
N_CHIP = 4

ADAM_LR = 0.001
ADAM_B1 = 0.9
ADAM_B2 = 0.999
ADAM_EPS = 1e-08
ADAM_WD = 0.01
ADAM_STEP = 10

SEGMENTS = {
    "qa": (0, 0, 1024),
    "gate": (1024, 3584, 1024),
    "vg": (2048, 2560, 1024),
    "qg": (3072, 1536, 512),
    "kg": (3584, 2048, 512),
    "ka": (4096, 1024, 256),
    "va": (4352, 1280, 256),
}

VMEM_LIMIT = 56 * 1024 * 1024
MESH = pl.DeviceIdType.MESH


def _params(semantics=None, vmem=None):
    return pltpu.CompilerParams(dimension_semantics=semantics, vmem_limit_bytes=vmem)


_DIMS = {
    "nn": (((1,), (0,)), ((), ())),
    "nt": (((1,), (1,)), ((), ())),
    "tn": (((0,), (0,)), ((), ())),
}


def _mxu(a, b, mode):
    return lax.dot_general(a.astype(BF16), b.astype(BF16), _DIMS[mode], preferred_element_type=F32)


@functools.partial(jax.custom_vjp, nondiff_argnums=(2,))
def bdot(a, b, mode):
    return _mxu(a, b, mode)


def _bdot_fwd(a, b, mode):
    return _mxu(a, b, mode), (a, b)


def _bdot_bwd(mode, res, g):
    a, b = res
    if mode == "nn":
        return _mxu(g, b, "nt"), _mxu(a, g, "tn")
    if mode == "nt":
        return _mxu(g, b, "nn"), _mxu(g, a, "tn")
    return _mxu(b, g, "nt"), _mxu(a, g, "nn")


bdot.defvjp(_bdot_fwd, _bdot_bwd)


def _rms(x, g):
    return x * lax.rsqrt(jnp.mean(x * x, axis=-1, keepdims=True) + NORM_EPS) * g


def _rope(x, cos, sin_signed):
    return x * cos + pltpu.roll(x, HEAD_DIM // 2, 1) * sin_signed


def _rope_transposed(d, cos, sin_signed):
    return d * cos + pltpu.roll(d * sin_signed, HEAD_DIM // 2, 1)


def _silu(x):
    return x * jax.nn.sigmoid(x)


def _log_sigmoid(z):
    return -(jnp.maximum(-z, 0.0) + jnp.log(1.0 + jnp.exp(-jnp.abs(z))))


def _matmul_call(args, in_specs, o_spec, out_shape, grid, mode, nk, acc_shape, *, name, has_res=False,
                 prefetch=None, load_b=lambda ref: ref[...]):
    dims = _DIMS[mode]
    out_dtype = out_shape.dtype
    n_pre = 0 if prefetch is None else 1

    def body(*refs):
        refs = refs[n_pre:]
        if has_res:
            a_ref, b_ref, r_ref, o_ref = refs[:4]
            rest = refs[4:]
        else:
            a_ref, b_ref, o_ref = refs[:3]
            r_ref = None
            rest = refs[3:]
        part = lax.dot_general(a_ref[...], load_b(b_ref), dims, preferred_element_type=F32)

        def finish(acc):
            if r_ref is not None:
                acc = acc + r_ref[...]
            o_ref[...] = acc.astype(out_dtype)

        if nk == 1:
            finish(part)
        else:
            acc_ref = rest[0]
            kk = pl.program_id(2)

            @pl.when(kk == 0)
            def _():
                acc_ref[...] = part

            @pl.when(kk > 0)
            def _():
                acc_ref[...] += part

            @pl.when(kk == nk - 1)
            def _():
                finish(acc_ref[...])

    scratch = [pltpu.VMEM(acc_shape, F32)] if nk > 1 else []
    params = _params(("parallel", "parallel", "arbitrary"), VMEM_LIMIT)
    if prefetch is None:
        return pl.pallas_call(body, name=name, grid=grid, in_specs=in_specs, out_specs=o_spec, out_shape=out_shape,
                              scratch_shapes=scratch, compiler_params=params)(*args)
    return pl.pallas_call(
        body, name=name,
        grid_spec=pltpu.PrefetchScalarGridSpec(num_scalar_prefetch=1, grid=grid, in_specs=in_specs,
                                               out_specs=o_spec, scratch_shapes=scratch),
        out_shape=out_shape, compiler_params=params)(prefetch, *args)


def _matmul(a, b, mode, *, tm, tn, tk, out_dtype=F32, res=None, name, n_out=None):
    if mode == "nn":
        (m, k), (k2, n) = a.shape, b.shape
    elif mode == "nt":
        (m, k), (n, k2) = a.shape, b.shape
    else:
        (k, m), (k2, n) = a.shape, b.shape
    n = n if n_out is None else n_out
    assert k == k2 and m % tm == 0 and n % tn == 0 and k % tk == 0, (name, a.shape, b.shape, tm, tn, tk)
    if mode == "tn":
        a_spec = pl.BlockSpec((tk, tm), lambda i, j, kk: (kk, i))
    else:
        a_spec = pl.BlockSpec((tm, tk), lambda i, j, kk: (i, kk))
    if mode == "nt":
        b_spec = pl.BlockSpec((tn, tk), lambda i, j, kk: (j, kk))
    else:
        b_spec = pl.BlockSpec((tk, tn), lambda i, j, kk: (kk, j))
    o_spec = pl.BlockSpec((tm, tn), lambda i, j, kk: (i, j))
    in_specs, args = [a_spec, b_spec], [a, b]
    if res is not None:
        in_specs.append(o_spec)
        args.append(res)
    return _matmul_call(args, in_specs, o_spec, jax.ShapeDtypeStruct((m, n), out_dtype),
                        (m // tm, n // tn, k // tk), mode, k // tk, (tm, tn), name=name, has_res=res is not None)


def _out_proj(o_attn, o_gla, w_out, x, *, tm, tn):
    t, ka = o_attn.shape
    kg = o_gla.shape[1]

    def body(a_ref, g_ref, w_ref, x_ref, o_ref):
        acc = lax.dot_general(a_ref[...], w_ref[:ka], _DIMS["nn"], preferred_element_type=F32)
        acc = acc + lax.dot_general(g_ref[...], w_ref[ka:], _DIMS["nn"], preferred_element_type=F32)
        o_ref[...] = acc + x_ref[...]

    tile = pl.BlockSpec((tm, tn), lambda i, j: (i, j))
    return pl.pallas_call(
        body, name="out_proj", grid=(t // tm, D_MODEL // tn),
        in_specs=[pl.BlockSpec((tm, ka), lambda i, j: (i, 0)), pl.BlockSpec((tm, kg), lambda i, j: (i, 0)),
                  pl.BlockSpec((ka + kg, tn), lambda i, j: (0, j)), tile],
        out_specs=tile, out_shape=jax.ShapeDtypeStruct((t, D_MODEL), F32),
        compiler_params=_params(("parallel", "parallel"), VMEM_LIMIT),
    )(o_attn, o_gla, w_out, x)


def _out_proj_dw(o_attn, o_gla, dx1, *, tn):
    t, ka = o_attn.shape
    assert o_gla.shape == (t, ka)

    def body(a_ref, g_ref, d_ref, o_ref):
        @pl.when(pl.program_id(0) == 0)
        def _():
            o_ref[...] = lax.dot_general(a_ref[...], d_ref[...], _DIMS["tn"], preferred_element_type=F32)

        @pl.when(pl.program_id(0) == 1)
        def _():
            o_ref[...] = lax.dot_general(g_ref[...], d_ref[...], _DIMS["tn"], preferred_element_type=F32)

    whole = pl.BlockSpec((t, ka), lambda i, j: (0, 0))
    return pl.pallas_call(
        body, name="dw_out", grid=(2, D_MODEL // tn),
        in_specs=[whole, whole, pl.BlockSpec((t, tn), lambda i, j: (0, j))],
        out_specs=pl.BlockSpec((ka, tn), lambda i, j: (i, j)),
        out_shape=jax.ShapeDtypeStruct((2 * ka, D_MODEL), F32),
        compiler_params=_params(("parallel", "parallel"), VMEM_LIMIT),
    )(o_attn, o_gla, dx1)


UP_BLOCK = 2 * D_FF // N_DEV


def _up_proj(h2, w_up8, *, tm):
    t = h2.shape[0]
    return _matmul_call(
        [h2, w_up8],
        [pl.BlockSpec((tm, D_MODEL), lambda i, j, kk: (i, 0)),
         pl.BlockSpec((None, D_MODEL, UP_BLOCK), lambda i, j, kk: (j, 0, 0))],
        pl.BlockSpec((None, tm, UP_BLOCK), lambda i, j, kk: (j // N_CHIP, i, j % N_CHIP)),
        jax.ShapeDtypeStruct((2, t, D_FF), F32), (t // tm, N_DEV, 1), "nn", 1, None, name="up_proj")


def _up_proj_dx(du, w_up8, *, tm, tn):
    t = du.shape[1]
    pair = 2
    return _matmul_call(
        [du, w_up8],
        [pl.BlockSpec((None, tm, pair * UP_BLOCK), lambda i, j, kk: (kk // 2, i, kk % 2)),
         pl.BlockSpec((pair, tn, UP_BLOCK), lambda i, j, kk: (kk, j, 0))],
        pl.BlockSpec((tm, tn), lambda i, j, kk: (i, j)),
        jax.ShapeDtypeStruct((t, D_MODEL), F32), (t // tm, D_MODEL // tn, N_DEV // pair), "nt", N_DEV // pair,
        (tm, tn), name="up_proj_dx", load_b=lambda ref: jnp.concatenate([ref[0], ref[1]], axis=1))


def _up_proj_dw(h2, du, *, tm, tk):
    t = h2.shape[0]
    return _matmul_call(
        [h2, du],
        [pl.BlockSpec((tk, tm), lambda j, i, kk: (kk, i)),
         pl.BlockSpec((None, tk, UP_BLOCK), lambda j, i, kk: (j // N_CHIP, kk, j % N_CHIP))],
        pl.BlockSpec((None, tm, UP_BLOCK), lambda j, i, kk: (j, i, 0)),
        jax.ShapeDtypeStruct((N_DEV, D_MODEL, UP_BLOCK), F32), (N_DEV, D_MODEL // tm, t // tk), "tn", t // tk,
        (tm, UP_BLOCK), name="up_proj_dw")


def _up_proj_dw_core(h2, du, core, *, tm, tk, res=None, name):
    t = h2.shape[0]

    def block(j, core_ref):
        return 2 * j + core_ref[0]

    out_spec = pl.BlockSpec((None, tm, UP_BLOCK), lambda j, i, kk, core_ref: (j, i, 0))
    return _matmul_call(
        [h2, du] + ([] if res is None else [res]),
        [pl.BlockSpec((tk, tm), lambda j, i, kk, core_ref: (kk, i)),
         pl.BlockSpec((None, tk, UP_BLOCK),
                      lambda j, i, kk, core_ref: (block(j, core_ref) // N_CHIP, kk, block(j, core_ref) % N_CHIP))]
        + ([] if res is None else [out_spec]),
        out_spec, jax.ShapeDtypeStruct((N_CHIP, D_MODEL, UP_BLOCK), F32), (N_CHIP, D_MODEL // tm, t // tk), "tn",
        t // tk, (tm, UP_BLOCK), name=name, has_res=res is not None, prefetch=core)


IN_TILE = 512


def _in_proj_dw(d_proj, h1, *, tn):
    t = h1.shape[0]
    table = []
    for tile in range(IN_MAIN // IN_TILE):
        dst, src, _ = max(s for s in SEGMENTS.values() if s[0] <= tile * IN_TILE)
        assert (src + tile * IN_TILE - dst) % IN_TILE == 0
        table.append((src + tile * IN_TILE - dst) // IN_TILE)
    assert sorted(table) == list(range(IN_MAIN // IN_TILE))
    return _matmul_call(
        [d_proj, h1],
        [pl.BlockSpec((t, IN_TILE), lambda j, i, kk, tab: (0, i)),
         pl.BlockSpec((t, tn), lambda j, i, kk, tab: (0, j))],
        pl.BlockSpec((IN_TILE, tn), lambda j, i, kk, tab: (tab[i], j)),
        jax.ShapeDtypeStruct((IN_TOTAL, D_MODEL), F32), (D_MODEL // tn, IN_MAIN // IN_TILE, 1), "tn", 1, None,
        name="in_proj_dw", prefetch=jnp.asarray(table, jnp.int32))


def _in_proj_dw_lr(dw_t, d_lr, h1):
    t = h1.shape[0]
    n_lr = IN_TOTAL - IN_MAIN
    tn = 512

    def body(dw_ref, dlr_ref, h1_ref, out_ref):
        full = lax.dot_general(dlr_ref[...], h1_ref[...], _DIMS["tn"], preferred_element_type=F32)
        out_ref[...] = full[:n_lr]

    return pl.pallas_call(
        body, name="in_proj_dw_lr", grid=(D_MODEL // tn,),
        in_specs=[pl.BlockSpec(memory_space=pl.ANY),
                  pl.BlockSpec((t, LR_PAD), lambda j: (0, 0)),
                  pl.BlockSpec((t, tn), lambda j: (0, j))],
        out_specs=pl.BlockSpec((n_lr, tn), lambda j: (IN_MAIN // n_lr, j)),
        out_shape=jax.ShapeDtypeStruct(dw_t.shape, F32),
        input_output_aliases={0: 0},
        compiler_params=_params(("parallel",), VMEM_LIMIT),
    )(dw_t, d_lr, h1)


def _rmsnorm_fwd(x, g, *, name, tr=1024):
    t, d = x.shape

    def body(x_ref, g_ref, h_ref):
        h_ref[...] = _rms(x_ref[...], g_ref[...]).astype(BF16)

    return pl.pallas_call(
        body, name=name, grid=(t // tr,),
        in_specs=[pl.BlockSpec((tr, d), lambda i: (i, 0)), pl.BlockSpec((1, d), lambda i: (0, 0))],
        out_specs=pl.BlockSpec((tr, d), lambda i: (i, 0)),
        out_shape=jax.ShapeDtypeStruct((t, d), BF16),
        compiler_params=_params(("parallel",), VMEM_LIMIT),
    )(x, g)


def _rmsnorm_bwd(x, g, dh, dres, *, name, also_bf16, tr=512):
    t, d = x.shape

    def body(x_ref, g_ref, dh_ref, dres_ref, dx_ref, *rest):
        dg_ref = rest[-1]
        _, vjp = jax.vjp(_rms, x_ref[...], g_ref[...])
        dx, dg = vjp(dh_ref[...])
        dx = dx + dres_ref[...]
        dx_ref[...] = dx
        if also_bf16:
            rest[0][...] = dx.astype(BF16)

        @pl.when(pl.program_id(0) == 0)
        def _():
            dg_ref[...] = jnp.zeros_like(dg_ref)

        dg_ref[...] += dg

    row = pl.BlockSpec((tr, d), lambda i: (i, 0))
    vec = pl.BlockSpec((1, d), lambda i: (0, 0))
    return pl.pallas_call(
        body, name=name, grid=(t // tr,),
        in_specs=[row, vec, row, row],
        out_specs=[row] + [row] * also_bf16 + [vec],
        out_shape=[jax.ShapeDtypeStruct((t, d), F32)] + [jax.ShapeDtypeStruct((t, d), BF16)] * also_bf16
                  + [jax.ShapeDtypeStruct((1, d), F32)],
        compiler_params=_params(("arbitrary",), VMEM_LIMIT),
    )(x, g, dh, dres)


def _seg_block(name, width):
    off = SEGMENTS[name][0]
    assert off % width == 0
    return off // width


def _attn_prep_fwd(proj, cos, sin_signed, gq, gk, *, tr=512):
    t = proj.shape[0]

    def body(q_ref, k_ref, v_ref, cos_ref, sin_ref, gq_ref, gk_ref, qo_ref, ko_ref, vo_ref):
        cos_t, sin_t = cos_ref[...], sin_ref[...]
        for h in range(ATTN_HEADS):
            cols = slice(h * HEAD_DIM, (h + 1) * HEAD_DIM)
            qo_ref[:, cols] = _rope(_rms(q_ref[:, cols], gq_ref[...]), cos_t, sin_t).astype(BF16)
        for h in range(KV_HEADS):
            cols = slice(h * HEAD_DIM, (h + 1) * HEAD_DIM)
            ko_ref[:, cols] = _rope(_rms(k_ref[:, cols], gk_ref[...]), cos_t, sin_t).astype(BF16)
        vo_ref[...] = v_ref[...].astype(BF16)

    qb, kb, vb = _seg_block("qa", ATTN_WIDTH), _seg_block("ka", KV_WIDTH), _seg_block("va", KV_WIDTH)
    tab = pl.BlockSpec((tr, HEAD_DIM), lambda i: (i, 0))
    vec = pl.BlockSpec((1, HEAD_DIM), lambda i: (0, 0))
    return pl.pallas_call(
        body, name="attn_prep_fwd", grid=(t // tr,),
        in_specs=[pl.BlockSpec((tr, ATTN_WIDTH), lambda i: (i, qb)),
                  pl.BlockSpec((tr, KV_WIDTH), lambda i: (i, kb)),
                  pl.BlockSpec((tr, KV_WIDTH), lambda i: (i, vb)),
                  tab, tab, vec, vec],
        out_specs=[pl.BlockSpec((tr, ATTN_WIDTH), lambda i: (i, 0)),
                   pl.BlockSpec((tr, KV_WIDTH), lambda i: (i, 0)),
                   pl.BlockSpec((tr, KV_WIDTH), lambda i: (i, 0))],
        out_shape=[jax.ShapeDtypeStruct((t, ATTN_WIDTH), BF16),
                   jax.ShapeDtypeStruct((t, KV_WIDTH), BF16),
                   jax.ShapeDtypeStruct((t, KV_WIDTH), BF16)],
        compiler_params=_params(("parallel",), VMEM_LIMIT),
    )(proj, proj, proj, cos, sin_signed, gq, gk)


def _attn_heads(q, kcat, vcat, sink_col, bias):
    s = bdot(q, kcat, "nt") * (HEAD_DIM ** -0.5) + bias
    m = lax.stop_gradient(jnp.maximum(jnp.max(s, axis=-1, keepdims=True), sink_col))
    p = jnp.exp(s - m)
    p = p / (jnp.sum(p, axis=-1, keepdims=True) + jnp.exp(sink_col - m))
    return bdot(p, vcat, "nn")


def _attn_bias():
    rows, cols = GQA_GROUP * ATTN_BLOCK, 3 * ATTN_BLOCK
    qi = lax.broadcasted_iota(jnp.int32, (3, rows, cols), 1) % ATTN_BLOCK
    sj = lax.broadcasted_iota(jnp.int32, (3, rows, cols), 2)
    kind = lax.broadcasted_iota(jnp.int32, (3, rows, cols), 0)
    inside = jnp.where(kind == 0, sj >= ATTN_BLOCK, jnp.where(kind == 2, sj < 2 * ATTN_BLOCK, True))
    valid = (jnp.abs(sj - ATTN_BLOCK - qi) <= WINDOW) & inside
    return jnp.where(valid, 0.0, -jnp.inf).astype(F32)


def _attn_bias_spec(nb, block_of):
    def index(*ix):
        n = block_of(*ix)
        return (jnp.where(n == 0, 0, jnp.where(n == nb - 1, 2, 1)), 0, 0)
    return pl.BlockSpec((None, GQA_GROUP * ATTN_BLOCK, 3 * ATTN_BLOCK), index)


def _head_rows(g):
    return slice(g * ATTN_BLOCK, (g + 1) * ATTN_BLOCK)


def _head_cols(g):
    return slice(g * HEAD_DIM, (g + 1) * HEAD_DIM)


def _stack_heads(ref):
    return jnp.concatenate([ref[:, _head_cols(g)] for g in range(GQA_GROUP)], axis=0).astype(F32)


def _sink_column(sink_ref, h):
    return jnp.concatenate([jnp.full((ATTN_BLOCK, 1), sink_ref[h * GQA_GROUP + g], F32)
                            for g in range(GQA_GROUP)], axis=0)


def _attn_specs(nb):
    q_spec = pl.BlockSpec((ATTN_BLOCK, GQA_GROUP * HEAD_DIM), lambda h, n: (n, h))
    kv_specs = [
        pl.BlockSpec((ATTN_BLOCK, HEAD_DIM), lambda h, n: (jnp.maximum(n - 1, 0), h)),
        pl.BlockSpec((ATTN_BLOCK, HEAD_DIM), lambda h, n: (n, h)),
        pl.BlockSpec((ATTN_BLOCK, HEAD_DIM), lambda h, n: (jnp.minimum(n + 1, nb - 1), h)),
    ]
    return q_spec, kv_specs


def _attn_fwd(q, k, v, sink, bias):
    t = q.shape[0]
    nb = t // ATTN_BLOCK

    def body(sink_ref, bias_ref, q_ref, kp_ref, kc_ref, kn_ref, vp_ref, vc_ref, vn_ref, o_ref):
        bias = bias_ref[...]
        args = []
        for h in range(KV_HEADS):
            q = jnp.concatenate([q_ref[:, _head_cols(h * GQA_GROUP + g)] for g in range(GQA_GROUP)], axis=0)
            kcat = jnp.concatenate([r[:, _head_cols(h)] for r in (kp_ref, kc_ref, kn_ref)], axis=0)
            vcat = jnp.concatenate([r[:, _head_cols(h)] for r in (vp_ref, vc_ref, vn_ref)], axis=0)
            args.append((q.astype(F32), kcat.astype(F32), vcat.astype(F32), _sink_column(sink_ref, h)))
        outs = [_attn_heads(*a, bias=bias).astype(BF16) for a in args]
        for h, o in enumerate(outs):
            for g in range(GQA_GROUP):
                o_ref[:, _head_cols(h * GQA_GROUP + g)] = o[_head_rows(g)]

    q_spec = pl.BlockSpec((ATTN_BLOCK, ATTN_WIDTH), lambda n: (n, 0))
    kv_specs = [pl.BlockSpec((ATTN_BLOCK, KV_WIDTH), lambda n: (jnp.maximum(n - 1, 0), 0)),
                pl.BlockSpec((ATTN_BLOCK, KV_WIDTH), lambda n: (n, 0)),
                pl.BlockSpec((ATTN_BLOCK, KV_WIDTH), lambda n: (jnp.minimum(n + 1, nb - 1), 0))]
    return pl.pallas_call(
        body, name="attn_fwd", grid=(nb,),
        in_specs=[pl.BlockSpec(memory_space=pltpu.SMEM), _attn_bias_spec(nb, lambda n: n), q_spec] + kv_specs + kv_specs,
        out_specs=q_spec,
        out_shape=jax.ShapeDtypeStruct((t, ATTN_WIDTH), BF16),
        compiler_params=_params(("parallel",), VMEM_LIMIT),
    )(sink, bias, q, k, k, k, v, v, v)


def _attn_bwd(q, k, v, sink, bias, dmix):
    t = q.shape[0]
    nb = t // ATTN_BLOCK

    def body(sink_ref, bias_ref, q_ref, kp_ref, kc_ref, kn_ref, vp_ref, vc_ref, vn_ref, do_ref,
             dq_ref, dk_lo, dk_mid, dk_hi, dv_lo, dv_mid, dv_hi, dsink_ref):
        h, n = pl.program_id(0), pl.program_id(1)
        kcat = jnp.concatenate([kp_ref[...], kc_ref[...], kn_ref[...]], axis=0).astype(F32)
        vcat = jnp.concatenate([vp_ref[...], vc_ref[...], vn_ref[...]], axis=0).astype(F32)
        _, vjp = jax.vjp(functools.partial(_attn_heads, bias=bias_ref[...]),
                         _stack_heads(q_ref), kcat, vcat, _sink_column(sink_ref, h))
        dq, dk, dv, dsink_col = vjp(_stack_heads(do_ref))
        row = lax.broadcasted_iota(jnp.int32, (8, HEAD_DIM), 0)
        dsink = jnp.zeros((8, HEAD_DIM), F32)
        for g in range(GQA_GROUP):
            dq_ref[:, _head_cols(g)] = dq[_head_rows(g)]
            dsink = dsink + jnp.where(row == g, jnp.sum(dsink_col[_head_rows(g)]), 0.0)
        for i, (dk_ref, dv_ref) in enumerate(((dk_lo, dv_lo), (dk_mid, dv_mid), (dk_hi, dv_hi))):
            rows = slice(i * ATTN_BLOCK, (i + 1) * ATTN_BLOCK)
            dk_ref[...] = dk[rows]
            dv_ref[...] = dv[rows]

        @pl.when(n == 0)
        def _():
            dsink_ref[...] = jnp.zeros_like(dsink_ref)

        dsink_ref[...] += dsink

    q_spec, kv_specs = _attn_specs(nb)
    kv_out = pl.BlockSpec((ATTN_BLOCK, HEAD_DIM), lambda h, n: (n, h))
    kv_shape = jax.ShapeDtypeStruct((t, KV_WIDTH), F32)
    return pl.pallas_call(
        body, name="attn_bwd", grid=(KV_HEADS, nb),
        in_specs=[pl.BlockSpec(memory_space=pltpu.SMEM), _attn_bias_spec(nb, lambda h, n: n), q_spec]
                 + kv_specs + kv_specs + [q_spec],
        out_specs=[q_spec] + [kv_out] * 6 + [pl.BlockSpec((None, 8, HEAD_DIM), lambda h, n: (h, 0, 0))],
        out_shape=[jax.ShapeDtypeStruct((t, ATTN_WIDTH), F32)] + [kv_shape] * 6
                  + [jax.ShapeDtypeStruct((KV_HEADS, 8, HEAD_DIM), F32)],
        compiler_params=_params(("parallel", "arbitrary"), VMEM_LIMIT),
    )(sink, bias, q, k, k, k, v, v, v, dmix)


def _attn_prep_bwd(proj, cos, sin_signed, gq, gk, dq, dks, dvs):
    t = proj.shape[0]
    tr = ATTN_BLOCK
    nb = t // tr

    def body(q_ref, k_ref, cos_ref, sin_ref, gq_ref, gk_ref, dq_ref,
             dk_lo, dk_mid, dk_hi, dv_lo, dv_mid, dv_hi,
             dqo_ref, dko_ref, dvo_ref, dgq_ref, dgk_ref):
        n = pl.program_id(0)
        cos_t, sin_t = cos_ref[...], sin_ref[...]
        has_next = (n < nb - 1).astype(F32)
        has_prev = (n > 0).astype(F32)
        dk = dk_lo[...] * has_next + dk_mid[...] + dk_hi[...] * has_prev
        dv = dv_lo[...] * has_next + dv_mid[...] + dv_hi[...] * has_prev
        dvo_ref[...] = dv.astype(BF16)
        dgq = jnp.zeros((1, HEAD_DIM), F32)
        dgk = jnp.zeros((1, HEAD_DIM), F32)
        for h in range(ATTN_HEADS):
            cols = slice(h * HEAD_DIM, (h + 1) * HEAD_DIM)
            _, vjp = jax.vjp(_rms, q_ref[:, cols], gq_ref[...])
            dx, dg = vjp(_rope_transposed(dq_ref[:, cols], cos_t, sin_t))
            dqo_ref[:, cols] = dx.astype(BF16)
            dgq = dgq + dg
        for h in range(KV_HEADS):
            cols = slice(h * HEAD_DIM, (h + 1) * HEAD_DIM)
            _, vjp = jax.vjp(_rms, k_ref[:, cols], gk_ref[...])
            dx, dg = vjp(_rope_transposed(dk[:, cols], cos_t, sin_t))
            dko_ref[:, cols] = dx.astype(BF16)
            dgk = dgk + dg

        @pl.when(n == 0)
        def _():
            dgq_ref[...] = jnp.zeros_like(dgq_ref)
            dgk_ref[...] = jnp.zeros_like(dgk_ref)

        dgq_ref[...] += dgq
        dgk_ref[...] += dgk

    qb, kb = _seg_block("qa", ATTN_WIDTH), _seg_block("ka", KV_WIDTH)
    tab = pl.BlockSpec((tr, HEAD_DIM), lambda i: (i, 0))
    vec = pl.BlockSpec((1, HEAD_DIM), lambda i: (0, 0))
    kv = [pl.BlockSpec((tr, KV_WIDTH), lambda i: (jnp.minimum(i + 1, nb - 1), 0)),
          pl.BlockSpec((tr, KV_WIDTH), lambda i: (i, 0)),
          pl.BlockSpec((tr, KV_WIDTH), lambda i: (jnp.maximum(i - 1, 0), 0))]
    wide = pl.BlockSpec((tr, ATTN_WIDTH), lambda i: (i, 0))
    narrow = pl.BlockSpec((tr, KV_WIDTH), lambda i: (i, 0))
    return pl.pallas_call(
        body, name="attn_prep_bwd", grid=(nb,),
        in_specs=[pl.BlockSpec((tr, ATTN_WIDTH), lambda i: (i, qb)),
                  pl.BlockSpec((tr, KV_WIDTH), lambda i: (i, kb)),
                  tab, tab, vec, vec, wide] + kv + kv,
        out_specs=[wide, narrow, narrow, vec, vec],
        out_shape=[jax.ShapeDtypeStruct((t, ATTN_WIDTH), BF16),
                   jax.ShapeDtypeStruct((t, KV_WIDTH), BF16),
                   jax.ShapeDtypeStruct((t, KV_WIDTH), BF16),
                   jax.ShapeDtypeStruct((1, HEAD_DIM), F32),
                   jax.ShapeDtypeStruct((1, HEAD_DIM), F32)],
        compiler_params=_params(("arbitrary",), VMEM_LIMIT),
    )(proj, proj, cos, sin_signed, gq, gk, dq, *dks, *dvs)


def _decay_fn(lr, w2, ba):
    return _log_sigmoid(bdot(lr, w2, "nn") + ba) / GLA_GATE_NORMALIZER


def _gla_prep_fwd(proj_lr, w2, ba2, *, tr=1024):
    t = proj_lr.shape[0]
    width = 2 * GLA_KEY_WIDTH

    def body(lr_ref, w2_ref, ba_ref, g_ref):
        g_ref[...] = _decay_fn(lr_ref[...], w2_ref[...], ba_ref[...])

    return pl.pallas_call(
        body, name="gla_prep_fwd", grid=(t // tr,),
        in_specs=[pl.BlockSpec((tr, LR_PAD), lambda i: (i, 0)),
                  pl.BlockSpec((LR_PAD, width), lambda i: (0, 0)),
                  pl.BlockSpec((1, width), lambda i: (0, 0))],
        out_specs=pl.BlockSpec((tr, width), lambda i: (i, 0)),
        out_shape=jax.ShapeDtypeStruct((t, width), F32),
        compiler_params=_params(("parallel",), VMEM_LIMIT),
    )(proj_lr, w2, ba2)


def _gla_prep_bwd(proj_lr, w2, ba2, dg_f, dg_b, *, tr=1024):
    t = proj_lr.shape[0]
    width = 2 * GLA_KEY_WIDTH

    def body(lr_ref, w2_ref, ba_ref, dgf_ref, dgb_ref, dlr_ref, dw2_ref, dba_ref):
        _, vjp = jax.vjp(_decay_fn, lr_ref[...], w2_ref[...], ba_ref[...])
        dlr, dw2, dba = vjp(jnp.concatenate([dgf_ref[...], dgb_ref[...]], axis=1))
        dlr_ref[...] = dlr.astype(BF16)

        @pl.when(pl.program_id(0) == 0)
        def _():
            dw2_ref[...] = jnp.zeros_like(dw2_ref)
            dba_ref[...] = jnp.zeros_like(dba_ref)

        dw2_ref[...] += dw2
        dba_ref[...] += dba

    half = pl.BlockSpec((tr, GLA_KEY_WIDTH), lambda i: (i, 0))
    return pl.pallas_call(
        body, name="gla_prep_bwd", grid=(t // tr,),
        in_specs=[pl.BlockSpec((tr, LR_PAD), lambda i: (i, 0)),
                  pl.BlockSpec((LR_PAD, width), lambda i: (0, 0)),
                  pl.BlockSpec((1, width), lambda i: (0, 0)), half, half],
        out_specs=[pl.BlockSpec((tr, LR_PAD), lambda i: (i, 0)),
                   pl.BlockSpec((LR_PAD, width), lambda i: (0, 0)),
                   pl.BlockSpec((1, width), lambda i: (0, 0))],
        out_shape=[jax.ShapeDtypeStruct((t, LR_PAD), BF16),
                   jax.ShapeDtypeStruct((LR_PAD, width), F32),
                   jax.ShapeDtypeStruct((1, width), F32)],
        compiler_params=_params(("arbitrary",), VMEM_LIMIT),
    )(proj_lr, w2, ba2, dg_f, dg_b)


def _gla_k(h):
    return slice(h * GLA_DK, (h + 1) * GLA_DK)


def _gla_v(h):
    return slice(h * GLA_DV, (h + 1) * GLA_DV)


def _running_sum(x, downward):
    n = x.shape[0]
    row = lax.broadcasted_iota(jnp.int32, x.shape, 0)
    step = 1
    while step < n:
        if downward:
            x = x + jnp.where(row >= step, pltpu.roll(x, step, 0), 0.0)
        else:
            x = x + jnp.where(row < n - step, pltpu.roll(x, n - step, 0), 0.0)
        step *= 2
    return x


@functools.partial(jax.custom_vjp, nondiff_argnums=(1,))
def _cumsum_rows(x, downward):
    return _running_sum(x, downward)


def _cumsum_rows_fwd(x, downward):
    return _running_sum(x, downward), None


def _cumsum_rows_bwd(downward, _, ct):
    return (_running_sum(ct, not downward),)


_cumsum_rows.defvjp(_cumsum_rows_fwd, _cumsum_rows_bwd)


def _gla_chunk(q, k, v, g, state, forward):
    c = GLA_CHUNK
    row = lax.broadcasted_iota(jnp.int32, (c, c), 0)
    col = lax.broadcasted_iota(jnp.int32, (c, c), 1)
    rid = lax.broadcasted_iota(jnp.int32, (c, GLA_DK), 0)
    q = q * (GLA_DK ** -0.5)
    if forward:
        see = row >= col
        upto_ref = rid <= c // 2
    else:
        see = row < col
        upto_ref = rid >= c - 1 - c // 2
    b = _cumsum_rows(g, forward)
    b_last = jnp.sum(g, axis=0, keepdims=True)
    b_ref = jnp.sum(jnp.where(upto_ref, g, 0.0), axis=0, keepdims=True)
    a = bdot(q * jnp.exp(b - b_ref), k * jnp.exp(b_ref - b), "nt")
    a = jnp.where(see, a, 0.0)
    o = bdot(a, v, "nn") + bdot(q * jnp.exp(b), state, "nt")
    new_state = state * jnp.exp(b_last) + bdot(v, k * jnp.exp(b_last - b), "tn")
    return o, new_state


def _gla_fwd(proj, g):
    t = proj.shape[0]
    c, per = GLA_CHUNK, GLA_PER_STEP
    nchunk = t // c
    nstep = nchunk // per
    qb, kb, vb = _seg_block("qg", GLA_KEY_WIDTH), _seg_block("kg", GLA_KEY_WIDTH), _seg_block("vg", GLA_WIDTH)

    def body(qf, kf, vf, gf, qr, kr, vr, gr, of_ref, ob_ref, sf_ref, sb_ref, state):
        @pl.when(pl.program_id(0) == 0)
        def _():
            state[...] = jnp.zeros_like(state)

        dirs = ((qf, kf, vf, gf, of_ref, sf_ref), (qr, kr, vr, gr, ob_ref, sb_ref))
        states = [[state[d, h] for h in range(GLA_HEADS)] for d in range(2)]
        for turn in range(per):
            chunk = (turn, per - 1 - turn)
            args = []
            for d, (q_ref, k_ref, v_ref, g_ref, _, _) in enumerate(dirs):
                rows = pl.ds(chunk[d] * c, c)
                args += [(q_ref[rows, _gla_k(h)], k_ref[rows, _gla_k(h)], v_ref[rows, _gla_v(h)],
                          g_ref[rows, _gla_k(h)], states[d][h]) for h in range(GLA_HEADS)]
            results = [_gla_chunk(*a, forward=(i < GLA_HEADS)) for i, a in enumerate(args)]
            for i, (a, (o, s_out)) in enumerate(zip(args, results)):
                d, h = divmod(i, GLA_HEADS)
                dirs[d][5][chunk[d], h] = a[4]
                dirs[d][4][pl.ds(chunk[d] * c, c), _gla_v(h)] = o
                states[d][h] = s_out
        for d in range(2):
            for h in range(GLA_HEADS):
                state[d, h] = states[d][h]

    specs, outs = [], []
    for d in range(2):
        ci = (lambda i: i) if d == 0 else (lambda i: nstep - 1 - i)
        specs += [pl.BlockSpec((per * c, GLA_KEY_WIDTH), lambda i, ci=ci: (ci(i), qb)),
                  pl.BlockSpec((per * c, GLA_KEY_WIDTH), lambda i, ci=ci: (ci(i), kb)),
                  pl.BlockSpec((per * c, GLA_WIDTH), lambda i, ci=ci: (ci(i), vb)),
                  pl.BlockSpec((per * c, GLA_KEY_WIDTH), lambda i, ci=ci, d=d: (ci(i), d))]
        outs.append(pl.BlockSpec((per * c, GLA_WIDTH), lambda i, ci=ci: (ci(i), 0)))
    for d in range(2):
        ci = (lambda i: i) if d == 0 else (lambda i: nstep - 1 - i)
        outs.append(pl.BlockSpec((per, GLA_HEADS, GLA_DV, GLA_DK), lambda i, ci=ci: (ci(i), 0, 0, 0)))
    o_shape = jax.ShapeDtypeStruct((t, GLA_WIDTH), F32)
    s_shape = jax.ShapeDtypeStruct((nchunk, GLA_HEADS, GLA_DV, GLA_DK), F32)
    return pl.pallas_call(
        body, name="gla_fwd", grid=(nstep,),
        in_specs=specs, out_specs=outs,
        out_shape=[o_shape, o_shape, s_shape, s_shape],
        scratch_shapes=[pltpu.VMEM((2, GLA_HEADS, GLA_DV, GLA_DK), F32)],
        compiler_params=_params(("arbitrary",), VMEM_LIMIT),
    )(proj, proj, proj, g, proj, proj, proj, g)


def _gla_bwd(proj, g, s_f, s_b, do):
    t = proj.shape[0]
    c, per = GLA_CHUNK, GLA_PER_STEP
    nchunk = t // c
    nstep = nchunk // per
    qb, kb, vb = _seg_block("qg", GLA_KEY_WIDTH), _seg_block("kg", GLA_KEY_WIDTH), _seg_block("vg", GLA_WIDTH)

    def body(*refs):
        ins, outs, dstate = refs[:12], refs[12:20], refs[20]

        @pl.when(pl.program_id(0) == 0)
        def _():
            dstate[...] = jnp.zeros_like(dstate)

        dstates = [[dstate[d, h] for h in range(GLA_HEADS)] for d in range(2)]
        for turn in range(per):
            chunk = (per - 1 - turn, turn)
            loaded = []
            for d in range(2):
                q_ref, k_ref, v_ref, g_ref, s_ref, do_ref = ins[6 * d:6 * d + 6]
                rows = pl.ds(chunk[d] * c, c)
                for h in range(GLA_HEADS):
                    loaded.append(((q_ref[rows, _gla_k(h)], k_ref[rows, _gla_k(h)], v_ref[rows, _gla_v(h)],
                                    g_ref[rows, _gla_k(h)], s_ref[chunk[d], h]),
                                   (do_ref[rows, _gla_v(h)], dstates[d][h])))
            grads = []
            for i, (primals, cotangents) in enumerate(loaded):
                _, vjp = jax.vjp(functools.partial(_gla_chunk, forward=(i < GLA_HEADS)), *primals)
                grads.append(vjp(cotangents))
            for i, (dq, dk, dv, dg, ds) in enumerate(grads):
                d, h = divmod(i, GLA_HEADS)
                rows = pl.ds(chunk[d] * c, c)
                dq_ref, dk_ref, dv_ref, dg_ref = outs[4 * d:4 * d + 4]
                dq_ref[rows, _gla_k(h)] = dq
                dk_ref[rows, _gla_k(h)] = dk
                dv_ref[rows, _gla_v(h)] = dv
                dg_ref[rows, _gla_k(h)] = dg
                dstates[d][h] = ds
        for d in range(2):
            for h in range(GLA_HEADS):
                dstate[d, h] = dstates[d][h]

    specs, outs, shapes = [], [], []
    for d in range(2):
        ci = (lambda i: nstep - 1 - i) if d == 0 else (lambda i: i)
        specs += [pl.BlockSpec((per * c, GLA_KEY_WIDTH), lambda i, ci=ci: (ci(i), qb)),
                  pl.BlockSpec((per * c, GLA_KEY_WIDTH), lambda i, ci=ci: (ci(i), kb)),
                  pl.BlockSpec((per * c, GLA_WIDTH), lambda i, ci=ci: (ci(i), vb)),
                  pl.BlockSpec((per * c, GLA_KEY_WIDTH), lambda i, ci=ci, d=d: (ci(i), d)),
                  pl.BlockSpec((per, GLA_HEADS, GLA_DV, GLA_DK), lambda i, ci=ci: (ci(i), 0, 0, 0)),
                  pl.BlockSpec((per * c, GLA_WIDTH), lambda i, ci=ci: (ci(i), 0))]
        key = pl.BlockSpec((per * c, GLA_KEY_WIDTH), lambda i, ci=ci: (ci(i), 0))
        val = pl.BlockSpec((per * c, GLA_WIDTH), lambda i, ci=ci: (ci(i), 0))
        outs += [key, key, val, key]
        shapes += [jax.ShapeDtypeStruct((t, GLA_KEY_WIDTH), F32), jax.ShapeDtypeStruct((t, GLA_KEY_WIDTH), F32),
                   jax.ShapeDtypeStruct((t, GLA_WIDTH), F32), jax.ShapeDtypeStruct((t, GLA_KEY_WIDTH), F32)]
    return pl.pallas_call(
        body, name="gla_bwd", grid=(nstep,),
        in_specs=specs, out_specs=outs, out_shape=shapes,
        scratch_shapes=[pltpu.VMEM((2, GLA_HEADS, GLA_DV, GLA_DK), F32)],
        compiler_params=_params(("arbitrary",), VMEM_LIMIT),
    )(proj, proj, proj, g, s_f, do, proj, proj, proj, g, s_b, do)


def _gla_out_head(o_f, o_b, gate, gn):
    return _rms(o_f + o_b, gn) * _silu(gate)


def _gla_out_fwd(o_f, o_b, proj, gn, *, tr=512):
    t = o_f.shape[0]
    gb = _seg_block("gate", GLA_WIDTH)

    def body(of_ref, ob_ref, gate_ref, gn_ref, out_ref):
        for h in range(GLA_HEADS):
            vc = slice(h * GLA_DV, (h + 1) * GLA_DV)
            out_ref[:, vc] = _gla_out_head(of_ref[:, vc], ob_ref[:, vc], gate_ref[:, vc], gn_ref[...]).astype(BF16)

    wide = pl.BlockSpec((tr, GLA_WIDTH), lambda i: (i, 0))
    return pl.pallas_call(
        body, name="gla_out_fwd", grid=(t // tr,),
        in_specs=[wide, wide, pl.BlockSpec((tr, GLA_WIDTH), lambda i: (i, gb)),
                  pl.BlockSpec((1, GLA_DV), lambda i: (0, 0))],
        out_specs=wide,
        out_shape=jax.ShapeDtypeStruct((t, GLA_WIDTH), BF16),
        compiler_params=_params(("parallel",), VMEM_LIMIT),
    )(o_f, o_b, proj, gn)


def _gla_out_bwd(o_f, o_b, proj, gn, dmix, *, tr=512):
    t = o_f.shape[0]
    gb = _seg_block("gate", GLA_WIDTH)

    def body(of_ref, ob_ref, gate_ref, gn_ref, dout_ref, do_ref, dgate_ref, dgn_ref):
        dgn = jnp.zeros((1, GLA_DV), F32)
        for h in range(GLA_HEADS):
            vc = slice(h * GLA_DV, (h + 1) * GLA_DV)
            _, vjp = jax.vjp(_gla_out_head, of_ref[:, vc], ob_ref[:, vc], gate_ref[:, vc], gn_ref[...])
            do, _, dgate, dg = vjp(dout_ref[:, vc])
            do_ref[:, vc] = do
            dgate_ref[:, vc] = dgate.astype(BF16)
            dgn = dgn + dg

        @pl.when(pl.program_id(0) == 0)
        def _():
            dgn_ref[...] = jnp.zeros_like(dgn_ref)

        dgn_ref[...] += dgn

    wide = pl.BlockSpec((tr, GLA_WIDTH), lambda i: (i, 0))
    vec = pl.BlockSpec((1, GLA_DV), lambda i: (0, 0))
    return pl.pallas_call(
        body, name="gla_out_bwd", grid=(t // tr,),
        in_specs=[wide, wide, pl.BlockSpec((tr, GLA_WIDTH), lambda i: (i, gb)), vec,
                  pl.BlockSpec((tr, GLA_WIDTH), lambda i: (i, 1))],
        out_specs=[wide, wide, vec],
        out_shape=[jax.ShapeDtypeStruct((t, GLA_WIDTH), F32), jax.ShapeDtypeStruct((t, GLA_WIDTH), BF16),
                   jax.ShapeDtypeStruct((1, GLA_DV), F32)],
        compiler_params=_params(("arbitrary",), VMEM_LIMIT),
    )(o_f, o_b, proj, gn, dmix)


CONV_TR = 1024
CONV_TC = 512
HALO = 8
HALO16 = 16


def _conv3(u, w, b):
    n = u.shape[0]
    return pltpu.roll(u, 1, 0) * w[0:1] + u * w[1:2] + pltpu.roll(u, n - 1, 0) * w[2:3] + b


def _conv_ext(main_ref, prev_ref, next_ref, r, nr):
    prev = prev_ref[...].astype(F32)[-HALO:] * (r > 0).astype(F32)
    nxt = next_ref[...].astype(F32)[:HALO] * (r < nr - 1).astype(F32)
    return jnp.concatenate([prev, main_ref[...].astype(F32), nxt], axis=0)


def _conv_specs(t, halo, half=None):
    per = CONV_TR // halo
    last = t // halo - 1
    lead = () if half is None else (None,)
    at = (lambda *ix: ix) if half is None else (lambda *ix: (half,) + ix)
    return [pl.BlockSpec(lead + (CONV_TR, CONV_TC), lambda j, r: at(r, j)),
            pl.BlockSpec(lead + (halo, CONV_TC), lambda j, r: at(jnp.maximum(r * per - 1, 0), j)),
            pl.BlockSpec(lead + (halo, CONV_TC), lambda j, r: at(jnp.minimum((r + 1) * per, last), j))]


def _ffn_mid_fwd(u, cw_g, cw_v, cb_g, cb_v):
    _, t, f = u.shape
    nr = t // CONV_TR

    def body(ug, ugp, ugn, uv, uvp, uvn, wg, wv, bg, bv, a_ref):
        r = pl.program_id(1)
        gate = _conv3(_conv_ext(ug, ugp, ugn, r, nr), wg[...], bg[...])[HALO:HALO + CONV_TR]
        val = _conv3(_conv_ext(uv, uvp, uvn, r, nr), wv[...], bv[...])[HALO:HALO + CONV_TR]
        a_ref[...] = (_silu(gate) * val).astype(BF16)

    w_spec = pl.BlockSpec((3, CONV_TC), lambda j, r: (0, j))
    b_spec = pl.BlockSpec((1, CONV_TC), lambda j, r: (0, j))
    return pl.pallas_call(
        body, name="ffn_mid_fwd", grid=(f // CONV_TC, nr),
        in_specs=_conv_specs(t, HALO, 0) + _conv_specs(t, HALO, 1) + [w_spec, w_spec, b_spec, b_spec],
        out_specs=pl.BlockSpec((CONV_TR, CONV_TC), lambda j, r: (r, j)),
        out_shape=jax.ShapeDtypeStruct((t, f), BF16),
        compiler_params=_params(("parallel", "parallel"), VMEM_LIMIT),
    )(u, u, u, u, u, u, cw_g, cw_v, cb_g, cb_v)


def _ffn_mid_bwd(u, cw_g, cw_v, cb_g, cb_v, da):
    _, t, f = u.shape
    nr = t // CONV_TR
    ext = CONV_TR + 2 * HALO

    def body(ug, ugp, ugn, uv, uvp, uvn, dam, dap, dan, wg, wv, bg, bv,
             du_ref, dwg_ref, dwv_ref, dbg_ref, dbv_ref):
        r = pl.program_id(1)
        shifted = []
        for main, prev, nxt in ((ug, ugp, ugn), (uv, uvp, uvn)):
            x = _conv_ext(main, prev, nxt, r, nr)
            shifted.append((pltpu.roll(x, 1, 0), x, pltpu.roll(x, ext - 1, 0)))
        da_x = _conv_ext(dam, dap, dan, r, nr)
        wg_t, wv_t = wg[...], wv[...]
        gate = shifted[0][0] * wg_t[0:1] + shifted[0][1] * wg_t[1:2] + shifted[0][2] * wg_t[2:3] + bg[...]
        val = shifted[1][0] * wv_t[0:1] + shifted[1][1] * wv_t[1:2] + shifted[1][2] * wv_t[2:3] + bv[...]
        sig = jax.nn.sigmoid(gate)
        silu = gate * sig
        d_val = da_x * silu
        d_gate = da_x * val * (sig + silu * (1.0 - sig))
        own = slice(HALO, HALO + CONV_TR)
        for half, (xs3, d, wt, dw_ref, db_ref) in enumerate(((shifted[0], d_gate, wg_t, dwg_ref, dbg_ref),
                                                            (shifted[1], d_val, wv_t, dwv_ref, dbv_ref))):
            du = pltpu.roll(d, ext - 1, 0) * wt[0:1] + d * wt[1:2] + pltpu.roll(d, 1, 0) * wt[2:3]
            du_ref[half] = du[own].astype(BF16)
            d_own = d[own]
            dw = jnp.concatenate([jnp.sum(x[own] * d_own, axis=0, keepdims=True) for x in xs3], axis=0)
            db = jnp.sum(d_own, axis=0, keepdims=True)

            @pl.when(r == 0)
            def _():
                dw_ref[...] = jnp.zeros_like(dw_ref)
                db_ref[...] = jnp.zeros_like(db_ref)

            dw_ref[...] += dw
            db_ref[...] += db

    w_spec = pl.BlockSpec((3, CONV_TC), lambda j, r: (0, j))
    b_spec = pl.BlockSpec((1, CONV_TC), lambda j, r: (0, j))
    return pl.pallas_call(
        body, name="ffn_mid_bwd", grid=(f // CONV_TC, nr),
        in_specs=(_conv_specs(t, HALO, 0) + _conv_specs(t, HALO, 1) + _conv_specs(t, HALO16)
                  + [w_spec, w_spec, b_spec, b_spec]),
        out_specs=[pl.BlockSpec((2, CONV_TR, CONV_TC), lambda j, r: (0, r, j)), w_spec, w_spec, b_spec, b_spec],
        out_shape=[jax.ShapeDtypeStruct((2, t, f), BF16),
                   jax.ShapeDtypeStruct((3, f), F32), jax.ShapeDtypeStruct((3, f), F32),
                   jax.ShapeDtypeStruct((1, f), F32), jax.ShapeDtypeStruct((1, f), F32)],
        compiler_params=_params(("parallel", "arbitrary"), VMEM_LIMIT),
    )(u, u, u, u, u, u, da, da, da, cw_g, cw_v, cb_g, cb_v)


def _down_proj_loss(act, w_down, x1, target, *, tm, tn):
    t, f = act.shape
    d = w_down.shape[1]

    def body(a_ref, w_ref, x_ref, t_ref, loss_ref, dy_ref, dyb_ref):
        y = lax.dot_general(a_ref[...], w_ref[...], _DIMS["nn"], preferred_element_type=F32) + x_ref[...]
        err = y - t_ref[...]
        dy = err * (1.0 / d)
        dy_ref[...] = dy
        dyb_ref[...] = dy.astype(BF16)
        part = 0.5 * jnp.sum(jnp.sum(err * err, axis=-1, keepdims=True) * (1.0 / d), axis=0, keepdims=True)

        @pl.when((pl.program_id(0) == 0) & (pl.program_id(1) == 0))
        def _():
            loss_ref[...] = jnp.zeros_like(loss_ref)

        loss_ref[...] += jnp.broadcast_to(part, loss_ref.shape)

    tile = pl.BlockSpec((tm, tn), lambda i, j: (i, j))
    return pl.pallas_call(
        body, name="down_proj_loss", grid=(t // tm, d // tn),
        in_specs=[pl.BlockSpec((tm, f), lambda i, j: (i, 0)), pl.BlockSpec((f, tn), lambda i, j: (0, j)), tile, tile],
        out_specs=[pl.BlockSpec((1, 128), lambda i, j: (0, 0)), tile, tile],
        out_shape=[jax.ShapeDtypeStruct((1, 128), F32), jax.ShapeDtypeStruct((t, d), F32),
                   jax.ShapeDtypeStruct((t, d), BF16)],
        compiler_params=_params(("arbitrary", "arbitrary"), VMEM_LIMIT),
    )(act, w_down, x1, target)


ANY = pl.BlockSpec(memory_space=pl.ANY)


def _position():
    return lax.axis_index("x"), lax.axis_index("y"), lax.axis_index("c")


def _other_chips(x, y):
    return [(1 - x, y), (x, 1 - y), (1 - x, 1 - y)]


def _handshake(peers):
    barrier = pltpu.get_barrier_semaphore()
    for peer in peers:
        pl.semaphore_signal(barrier, inc=1, device_id=peer, device_id_type=MESH)
    pl.semaphore_wait(barrier, len(peers))


def _exchange(body, operands, out_shapes, sems, *, name, collective_id):
    n_in, n_out = len(operands), len(out_shapes)

    def run(*refs):
        body(refs[:n_in], refs[n_in:n_in + n_out], *refs[n_in + n_out:])

    if collective_id is None:
        return pl.pallas_call(run, name=name, in_specs=[ANY] * n_in, out_specs=[ANY] * n_out,
                              out_shape=out_shapes, scratch_shapes=sems)(*operands)
    return pl.kernel(run, name=name, out_type=out_shapes,
                     mesh=plsc.ScalarSubcoreMesh(axis_name="sequencer", num_cores=1), scratch_types=sems,
                     compiler_params=pltpu.CompilerParams(collective_id=collective_id))(*operands)


def _all_gather(blocks, *, name, collective_id=None):
    na = len(blocks)

    def body(ins, outs, send_sems, recv_sems, local_sems):
        x, y, c = _position()
        me, sibling = (x, y, c), (x, y, 1 - c)
        along_x, along_y, diagonal = (1 - x, y, c), (x, 1 - y, c), (1 - x, 1 - y, c)
        first_c = c == 0
        relay_from = (jnp.where(first_c, x, 1 - x), jnp.where(first_c, 1 - y, y), c)
        relay_to = (jnp.where(first_c, 1 - x, x), jnp.where(first_c, y, 1 - y), c)
        if collective_id is not None:
            _handshake([sibling, along_x, along_y])

        def index(px, py, pc):
            return 4 * px + 2 * py + pc

        def copy(a, k, block, to, src=None):
            dst = outs[a].at[index(*block)]
            return pltpu.make_async_remote_copy(
                src_ref=dst if src is None else src, dst_ref=dst,
                send_sem=send_sems.at[a, k], recv_sem=recv_sems.at[a, k],
                device_id=to, device_id_type=MESH)

        pending = []
        for a in range(na):
            mine = pltpu.make_async_copy(ins[a], outs[a].at[index(*me)], local_sems.at[a])
            mine.start()
            pending.append(mine)
        sent = []
        for a in range(na):
            sent += [copy(a, 0, me, sibling, src=ins[a]), copy(a, 1, me, along_x, src=ins[a]),
                     copy(a, 2, me, along_y, src=ins[a])]
        for cp in sent:
            cp.start()

        def passes_on(k_in, owner, k_out):
            for a in range(na):
                copy(a, k_in, owner, me).wait_recv()
                cp = copy(a, k_out, owner, sibling)
                cp.start()
                sent.append(cp)

        passes_on(1, along_x, 4)
        passes_on(2, along_y, 5)
        for a in range(na):
            cp = copy(a, 3, relay_from, relay_to)
            cp.start()
            sent.append(cp)
        passes_on(3, diagonal, 6)
        for a in range(na):
            copy(a, 0, sibling, me).wait_recv()
            for k, owner in ((4, along_x), (5, along_y), (6, diagonal)):
                copy(a, k, (owner[0], owner[1], 1 - c), me).wait_recv()
        for cp in sent:
            cp.wait_send()
        for cp in pending:
            cp.wait()

    return _exchange(
        body, blocks, [jax.ShapeDtypeStruct((N_DEV,) + b.shape, b.dtype) for b in blocks],
        [pltpu.SemaphoreType.DMA((na, 7)), pltpu.SemaphoreType.DMA((na, 7)), pltpu.SemaphoreType.DMA((na,))],
        name=name, collective_id=collective_id)


def _grad_exchange(grads, parts, *, name, collective_id):
    ng, npart = len(grads), len(parts)

    def body(ins, outs, core_send, core_recv, chip_send, chip_recv, local_sems):
        x, y, c = _position()
        sibling = (x, y, 1 - c)
        chips = _other_chips(x, y)
        _handshake([sibling] + [(px, py, c) for px, py in chips])
        me = 2 * x + y
        copies = []
        for b in range(npart):
            src, dst = ins[ng + b], outs[ng + b]
            own = pltpu.make_async_copy(src.at[me], dst.at[me], local_sems.at[b])
            own.start()
            copies.append(own)
            for j, (px, py) in enumerate(chips):
                cp = pltpu.make_async_remote_copy(
                    src_ref=src.at[2 * px + py], dst_ref=dst.at[me],
                    send_sem=chip_send.at[b, j], recv_sem=chip_recv.at[b, j],
                    device_id=(px, py, c), device_id_type=MESH)
                cp.start()
                copies.append(cp)
        for a in range(ng):
            for k in range(N_CHIP):
                src = ins[a].at[k, 1 - c] if len(grads[a].shape) == 4 else ins[a].at[k]
                cp = pltpu.make_async_remote_copy(
                    src_ref=src, dst_ref=outs[a].at[k],
                    send_sem=core_send.at[a, k], recv_sem=core_recv.at[a, k],
                    device_id=sibling, device_id_type=MESH)
                cp.start()
                copies.append(cp)
        for cp in copies:
            cp.wait()

    shapes = ([jax.ShapeDtypeStruct((N_CHIP,) + g.shape[-2:], g.dtype) for g in grads]
              + [jax.ShapeDtypeStruct(p.shape, p.dtype) for p in parts])
    sems = [pltpu.SemaphoreType.DMA((max(ng, 1), N_CHIP)), pltpu.SemaphoreType.DMA((max(ng, 1), N_CHIP)),
            pltpu.SemaphoreType.DMA((max(npart, 1), 3)), pltpu.SemaphoreType.DMA((max(npart, 1), 3)),
            pltpu.SemaphoreType.DMA((max(npart, 1),))]
    out = _exchange(body, list(grads) + list(parts), shapes, sems, name=name, collective_id=collective_id)
    return out[:ng], out[ng:]


def _pair_sum(grad, theirs, core, *, tile, name, narrow=False):
    _, _, r, w = grad.shape
    tr, tw = tile
    assert r % tr == 0 and w % tw == 0

    def body(core_ref, mine_ref, theirs_ref, out_ref, *narrow_ref):
        total = mine_ref[...] + theirs_ref[...]
        out_ref[...] = total
        if narrow:
            narrow_ref[0][...] = total.astype(BF16)

    spec = pl.BlockSpec((None, tr, tw), lambda k, i, j, core_ref: (k, i, j))
    shapes = [jax.ShapeDtypeStruct((N_CHIP, r, w), F32)] + [jax.ShapeDtypeStruct((N_CHIP, r, w), BF16)] * narrow
    out = pl.pallas_call(
        body, name=name,
        grid_spec=pltpu.PrefetchScalarGridSpec(
            num_scalar_prefetch=1, grid=(N_CHIP, r // tr, w // tw),
            in_specs=[pl.BlockSpec((None, None, tr, tw), lambda k, i, j, core_ref: (k, core_ref[0], i, j)), spec],
            out_specs=[spec] * len(shapes)),
        out_shape=shapes,
        compiler_params=_params(("parallel", "parallel", "parallel"), VMEM_LIMIT),
    )(core, grad, theirs)
    return tuple(out) if narrow else out[0]


class _ReduceScatter:
    def __init__(self, core):
        self.core = core
        self.pending = None
        self.results = {}
        self.launches = 0

    def push(self, tag, grads, rows, then, narrow=False, finish=None):
        pair, kept, prev_tag = [], [], None
        if self.pending is not None:
            prev_tag, prev, theirs, prev_rows, prev_narrow, prev_finish = self.pending
            if prev_finish is not None:
                pair = prev_finish(theirs)
            else:
                pair = [_pair_sum(g, s, self.core, tile=tile, name=f"pair_sum_{prev_tag}_{i}", narrow=prev_narrow)
                        for i, (g, s, tile) in enumerate(zip(prev, theirs, prev_rows))]
            if prev_narrow:
                kept, pair = [p[0] for p in pair], [p[1] for p in pair]
        grads, pair, then = lax.optimization_barrier((list(grads), pair, then))
        if finish is None:
            grads = [g.reshape((N_CHIP, 2) + g.shape[1:]) for g in grads]
        self.launches += 1
        theirs, parts = _grad_exchange(grads, pair, name=f"grad_exchange_{self.launches}",
                                       collective_id=1 + self.launches)
        if prev_tag is not None:
            self.results[prev_tag] = (parts, kept)
        self.pending = (tag, grads, theirs, rows, narrow, finish) if tag is not None else None
        return then

    def result(self, tag):
        return self.results[tag]


def _adamw(parts, w, m, v, *, tile, name, own=None, chip=None):
    n, r, cols = parts.shape
    tr, tw = tile
    assert r % tr == 0 and cols % tw == 0 and w.shape == (r, cols)
    c1 = 1.0 - ADAM_B1 ** ADAM_STEP
    c2 = 1.0 - ADAM_B2 ** ADAM_STEP

    def update(g, w_ref, m_ref, v_ref, g_ref, d_ref, nm_ref, nv_ref):
        new_m = ADAM_B1 * m_ref[...] + (1.0 - ADAM_B1) * g
        new_v = ADAM_B2 * v_ref[...] + (1.0 - ADAM_B2) * (g * g)
        m_hat = new_m / c1
        v_hat = new_v / c2
        g_ref[...] = g
        d_ref[...] = -ADAM_LR * (m_hat / (jnp.sqrt(v_hat) + ADAM_EPS) + ADAM_WD * w_ref[...])
        nm_ref[...] = new_m
        nv_ref[...] = new_v

    shape = jax.ShapeDtypeStruct((r, cols), F32)
    if own is None:
        def body(p_ref, *refs):
            g = p_ref[0]
            for k in range(1, n):
                g = g + p_ref[k]
            update(g, *refs)

        spec = pl.BlockSpec((tr, tw), lambda i, j: (i, j))
        return pl.pallas_call(
            body, name=name, grid=(r // tr, cols // tw),
            in_specs=[pl.BlockSpec((n, tr, tw), lambda i, j: (0, i, j)), spec, spec, spec],
            out_specs=[spec] * 4, out_shape=[shape] * 4,
            compiler_params=_params(("parallel", "parallel"), VMEM_LIMIT),
        )(parts, w, m, v)

    def body(chip_ref, p_ref, own_ref, *refs):
        g = None
        for k in range(n):
            term = jnp.where(chip_ref[0] == k, own_ref[...], p_ref[k].astype(F32))
            g = term if g is None else g + term
        update(g, *refs)

    spec = pl.BlockSpec((tr, tw), lambda i, j, chip_ref: (i, j))
    return pl.pallas_call(
        body, name=name,
        grid_spec=pltpu.PrefetchScalarGridSpec(
            num_scalar_prefetch=1, grid=(r // tr, cols // tw),
            in_specs=[pl.BlockSpec((n, tr, tw), lambda i, j, chip_ref: (0, i, j)),
                      pl.BlockSpec((None, tr, tw), lambda i, j, chip_ref: (chip_ref[0], i, j)), spec, spec, spec],
            out_specs=[spec] * 4),
        out_shape=[shape] * 4,
        compiler_params=_params(("parallel", "parallel"), VMEM_LIMIT),
    )(chip, parts, own, w, m, v)


LANES = 128


def _row_offsets(pieces):
    offsets, row = [], 0
    for p in pieces:
        assert p.shape[0] == 1 and p.shape[1] % LANES == 0, p.shape
        offsets.append(row)
        row += p.shape[1] // LANES
    return offsets, row


def _pack_rows(pieces):
    offsets, rows = _row_offsets(pieces)

    def body(*refs):
        out_ref = refs[-1]
        for ref, start in zip(refs[:-1], offsets):
            for j in range(ref.shape[1] // LANES):
                out_ref[start + j:start + j + 1, :] = ref[:, j * LANES:(j + 1) * LANES]

    return pl.pallas_call(body, name="pack_small_grads",
                          out_shape=jax.ShapeDtypeStruct((rows, LANES), F32))(*pieces)


def _adamw_rows(terms, ws, ms, vs):
    n_dev, rows, _ = terms.shape
    offsets, used = _row_offsets(ws)
    assert used + 1 == rows
    c1 = 1.0 - ADAM_B1 ** ADAM_STEP
    c2 = 1.0 - ADAM_B2 ** ADAM_STEP
    nw = len(ws)

    def body(*refs):
        t_ref = refs[0]
        w_refs, m_refs, v_refs = refs[1:1 + nw], refs[1 + nw:1 + 2 * nw], refs[1 + 2 * nw:1 + 3 * nw]
        outs = refs[1 + 3 * nw:]
        total = t_ref[0]
        for k in range(1, n_dev):
            total = total + t_ref[k]
        for i, start in enumerate(offsets):
            for j in range(ws[i].shape[1] // LANES):
                lanes = slice(j * LANES, (j + 1) * LANES)
                g = total[start + j:start + j + 1, :]
                new_m = ADAM_B1 * m_refs[i][:, lanes] + (1.0 - ADAM_B1) * g
                new_v = ADAM_B2 * v_refs[i][:, lanes] + (1.0 - ADAM_B2) * (g * g)
                delta = -ADAM_LR * ((new_m / c1) / (jnp.sqrt(new_v / c2) + ADAM_EPS) + ADAM_WD * w_refs[i][:, lanes])
                for kind, value in enumerate((g, delta, new_m, new_v)):
                    outs[kind * nw + i][:, lanes] = value
        outs[-1][...] = total[used:used + 1, :]

    shapes = [jax.ShapeDtypeStruct(w.shape, F32) for w in ws] * 4 + [jax.ShapeDtypeStruct((1, LANES), F32)]
    out = pl.pallas_call(body, name="adamw_replicated", out_shape=shapes,
                         compiler_params=_params(None, VMEM_LIMIT))(terms, *ws, *ms, *vs)
    return [list(out[kind * nw:(kind + 1) * nw]) for kind in range(4)], out[-1]


def _rope_tables(t):
    half = HEAD_DIM // 2
    inv = 1.0 / (ROPE_THETA ** (jnp.arange(half, dtype=F32) / half))
    ang = jnp.arange(t, dtype=jnp.int32).astype(F32)[:, None] * inv[None, :]
    cos, sin = jnp.cos(ang), jnp.sin(ang)
    return jnp.concatenate([cos, cos], axis=1), jnp.concatenate([-sin, sin], axis=1)


IN_KERNEL = IN_MAIN + LR_PAD


def _to_kernel_rows(w_t):
    order = sorted(SEGMENTS.values())
    pad = jnp.zeros((LR_PAD - (IN_TOTAL - IN_MAIN), w_t.shape[1]), w_t.dtype)
    return jnp.concatenate([w_t[src:src + width] for _, src, width in order] + [w_t[IN_MAIN:IN_TOTAL], pad], axis=0)


CONV_TAPS = 3
WA_BLOCK = GLA_KEY_WIDTH // N_DEV
SMALL_SIZES = (CONV_TAPS * UP_BLOCK, GLA_RANK * WA_BLOCK, GLA_RANK * WA_BLOCK)
SMALL_SHAPES = ((CONV_TAPS, UP_BLOCK), (GLA_RANK, WA_BLOCK), (GLA_RANK, WA_BLOCK))
SMALL_ROWS = sum(SMALL_SIZES) // LANES


def _small_block(conv, wa_f, wa_b):
    return jnp.concatenate([conv.reshape(-1), wa_f.reshape(-1), wa_b.reshape(-1)]).reshape(SMALL_ROWS, LANES)


def _small_unblock(block):
    flat, out, off = block.reshape(-1), [], 0
    for size, shape in zip(SMALL_SIZES, SMALL_SHAPES):
        out.append(flat[off:off + size].reshape(shape))
        off += size
    return out


def _small_blocks(conv_full, wa_f_full, wa_b_full):
    def by_device(a, width):
        return jnp.transpose(a.reshape(a.shape[0], N_DEV, width), (1, 0, 2)).reshape(N_DEV, -1)
    return jnp.concatenate([by_device(conv_full, UP_BLOCK), by_device(wa_f_full, WA_BLOCK),
                            by_device(wa_b_full, WA_BLOCK)], axis=1).reshape(N_DEV, SMALL_ROWS, LANES)


def _small_unblocks(blocks):
    flat, out, off = blocks.reshape(N_DEV, -1), [], 0
    for size, (rows, width) in zip(SMALL_SIZES, SMALL_SHAPES):
        part = flat[:, off:off + size].reshape(N_DEV, rows, width)
        out.append(jnp.transpose(part, (1, 0, 2)).reshape(rows, N_DEV * width))
        off += size
    return out


def _local_step(xs, target, norm1_g, w_in8, gq, gk, attn_sink, w2, ba2, gla_norm_g, w_out_full, norm2_g,
                w_up8, cw_g, cw_v, cb_g, cb_v, w_down_full, rs=None):
    t = xs.shape[0]
    tm = min(1024, t)
    tall = min(2048, t)
    cos, sin_signed = _rope_tables(t)
    sink = attn_sink.reshape(ATTN_HEADS)

    h1 = _rmsnorm_fwd(xs, norm1_g, name="norm1_fwd")
    w_in8, h1, cos, sin_signed = lax.optimization_barrier((w_in8, h1, cos, sin_signed))
    w_in_k = _to_kernel_rows(w_in8.reshape(IN_TOTAL, D_MODEL))
    proj = _matmul(h1, w_in_k, "nt", tm=tall, tn=IN_MAIN // 4, tk=D_MODEL, n_out=IN_MAIN, name="proj_main")
    proj_lr = _matmul(h1, w_in_k[IN_MAIN:], "nt", tm=tm, tn=LR_PAD, tk=D_MODEL, name="proj_lr")
    qa, ka, va = _attn_prep_fwd(proj, cos, sin_signed, gq, gk)
    assert t >= 2 * ATTN_BLOCK
    attn_bias = _attn_bias()
    o_attn = _attn_fwd(qa, ka, va, sink, attn_bias)
    g_dec = _gla_prep_fwd(proj_lr, w2, ba2)
    o_f, o_b, s_f, s_b = _gla_fwd(proj, g_dec)
    o_gla = _gla_out_fwd(o_f, o_b, proj, gla_norm_g)
    x1 = _out_proj(o_attn, o_gla, w_out_full, xs, tm=tall, tn=512)
    h2 = _rmsnorm_fwd(x1, norm2_g, name="norm2_fwd")
    u = _up_proj(h2, w_up8, tm=tm)
    act = _ffn_mid_fwd(u, cw_g, cw_v, cb_g, cb_v)
    loss_part, dy, dy_b = _down_proj_loss(act, w_down_full, x1, target, tm=tm, tn=512)

    d_act = _matmul(dy_b, w_down_full, "nt", tm=tall, tn=D_FF // 4, tk=D_MODEL, out_dtype=BF16, name="d_act")
    dw_down = _matmul(act, dy_b, "tn", tm=D_FF // 4, tn=512, tk=t, name="dw_down")
    if rs is not None:
        d_act = rs.push("w_down", [dw_down.reshape(N_DEV, D_FF // N_DEV, D_MODEL)],
                        [(D_FF // N_DEV // 2, D_MODEL)], d_act)
    du, dcw_g, dcw_v, dcb_g, dcb_v = _ffn_mid_bwd(u, cw_g, cw_v, cb_g, cb_v, d_act)
    if rs is None:
        dw_up8 = _up_proj_dw(h2, du, tm=512, tk=t)
    else:
        dw_up8 = None
        theirs_first = _up_proj_dw_core(h2, du, 1 - rs.core, tm=512, tk=t, name="up_proj_dw_sibling")
        du = rs.push("w_up", [theirs_first], None, du, finish=lambda got, h2=h2, du=du: [
            _up_proj_dw_core(h2, du, rs.core, tm=512, tk=t, res=got[0], name="up_proj_dw_own")])
    dh2 = _up_proj_dx(du, w_up8, tm=tm, tn=1024)
    dx1, dx1_b, d_norm2 = _rmsnorm_bwd(x1, norm2_g, dh2, dy, name="norm2_bwd", also_bf16=True)
    dmix = _matmul(dx1_b, w_out_full, "nt", tm=tall, tn=1024, tk=D_MODEL, name="d_mix")
    dw_out = _out_proj_dw(o_attn, o_gla, dx1_b, tn=512)
    if rs is not None:
        dmix = rs.push("w_out", [dw_out.reshape(N_DEV, D_MODEL // N_DEV, D_MODEL)], [(256, D_MODEL)], dmix)
    do_gla, d_gate, d_gla_norm = _gla_out_bwd(o_f, o_b, proj, gla_norm_g, dmix)
    (dq_f, dk_f, dv_f, dg_f, dq_b, dk_b, dv_b, dg_b) = _gla_bwd(proj, g_dec, s_f, s_b, do_gla)
    d_lr, d_w2, d_ba2 = _gla_prep_bwd(proj_lr, w2, ba2, dg_f, dg_b)
    dqa, dk_lo, dk_mid, dk_hi, dv_lo, dv_mid, dv_hi, d_sink8 = _attn_bwd(qa, ka, va, sink, attn_bias, dmix)
    d_qa, d_ka, d_va, d_qn, d_kn = _attn_prep_bwd(proj, cos, sin_signed, gq, gk, dqa,
                                                  (dk_lo, dk_mid, dk_hi), (dv_lo, dv_mid, dv_hi))
    d_seg = {"qa": d_qa, "gate": d_gate, "vg": (dv_f + dv_b).astype(BF16), "qg": (dq_f + dq_b).astype(BF16),
             "kg": (dk_f + dk_b).astype(BF16), "ka": d_ka, "va": d_va}
    d_proj = jnp.concatenate([d_seg[k] for k in sorted(SEGMENTS, key=lambda k: SEGMENTS[k][0])] + [d_lr], axis=1)
    dw_in_t = _in_proj_dw_lr(_in_proj_dw(d_proj, h1, tn=1024), d_lr, h1)
    if rs is not None:
        per_in = IN_TOTAL // N_DEV
        small_grad = _small_blocks(jnp.concatenate([dcw_g, dcw_v], axis=1), d_w2[:GLA_RANK, :GLA_KEY_WIDTH],
                                   d_w2[GLA_RANK:2 * GLA_RANK, GLA_KEY_WIDTH:])
        d_proj, d_lr = rs.push("w_in", [dw_in_t.reshape(N_DEV, per_in, D_MODEL), small_grad],
                               [(per_in, 1024), small_grad.shape[1:]], (d_proj, d_lr), narrow=True)
    dh1 = _matmul(d_proj, w_in_k, "nn", tm=tm, tn=1024, tk=IN_KERNEL, name="dh1")
    if rs is not None:
        dh1 = rs.push(None, [], [], dh1)
    grad_x, d_norm1 = _rmsnorm_bwd(xs, norm1_g, dh1, dx1, name="norm1_bwd", also_bf16=False)
    return (loss_part, grad_x, dw_in_t, dw_out, dw_up8, dw_down, dcw_g, dcw_v, dcb_g, dcb_v,
            d_w2, d_ba2, d_norm1, d_norm2, d_qn, d_kn, d_sink8, d_gla_norm)


def kernel(x, norm1_g, w_in, attn_q_norm_g, attn_k_norm_g, attn_sink, gla_wa2_fwd, gla_ba_fwd, gla_wa2_bwd, gla_ba_bwd, gla_out_norm_g, w_out, norm2_g, w_up, conv_w, conv_b, w_down, loss_target, m_norm1_g, m_w_in, m_attn_q_norm_g, m_attn_k_norm_g, m_attn_sink, m_gla_wa2_fwd, m_gla_ba_fwd, m_gla_wa2_bwd, m_gla_ba_bwd, m_gla_out_norm_g, m_w_out, m_norm2_g, m_w_up, m_conv_w, m_conv_b, m_w_down, v_norm1_g, v_w_in, v_attn_q_norm_g, v_attn_k_norm_g, v_attn_sink, v_gla_wa2_fwd, v_gla_ba_fwd, v_gla_wa2_bwd, v_gla_ba_bwd, v_gla_out_norm_g, v_w_out, v_norm2_g, v_w_up, v_conv_w, v_conv_b, v_w_down):
    t = x.shape[1]
    xs = x.reshape(t, D_MODEL)
    target = loss_target.reshape(t, D_MODEL)
    core = lax.axis_index("c").astype(jnp.int32).reshape(1)

    w_small = _small_block(conv_w[0], gla_wa2_fwd[0], gla_wa2_bwd[0])
    w_in_t, m_in_t, v_in_t = (jnp.swapaxes(a[0], 0, 1) for a in (w_in, m_w_in, v_w_in))
    g_in, g_small = _all_gather([w_in_t.astype(BF16), w_small], name="gather_w_in", collective_id=8)
    g_out, w_up8, g_down = _all_gather([w_out[0].astype(BF16), w_up[0].astype(BF16), w_down[0].astype(BF16)],
                                       name="gather_later_weights", collective_id=1)
    w_out_full = g_out.reshape(D_MODEL, D_MODEL)
    w_down_full = g_down.reshape(D_FF, D_MODEL)
    conv_w_full, wa2_f, wa2_b = _small_unblocks(g_small)
    cw_g, cw_v = conv_w_full[:, :D_FF], conv_w_full[:, D_FF:]
    cb_g, cb_v = conv_b[:, :D_FF], conv_b[:, D_FF:]
    w2 = jnp.zeros((LR_PAD, 2 * GLA_KEY_WIDTH), F32)
    w2 = w2.at[:GLA_RANK, :GLA_KEY_WIDTH].set(wa2_f).at[GLA_RANK:2 * GLA_RANK, GLA_KEY_WIDTH:].set(wa2_b)
    ba2 = jnp.concatenate([gla_ba_fwd, gla_ba_bwd], axis=1)
    rs = _ReduceScatter(core)
    (loss_part, grad_x, _, _, _, _, _, _, dcb_g, dcb_v, _, d_ba2,
     d_norm1, d_norm2, d_qn, d_kn, d_sink8, d_gla_norm) = _local_step(
        xs, target, norm1_g, g_in, attn_q_norm_g, attn_k_norm_g, attn_sink, w2, ba2, gla_out_norm_g,
        w_out_full, norm2_g, w_up8, cw_g, cw_v, cb_g, cb_v, w_down_full, rs=rs)

    (part_down,), (part_up,), (part_out,) = rs.result("w_down")[0], rs.result("w_up")[0], rs.result("w_out")[0]
    (part_in, part_small), (own_in, own_small) = rs.result("w_in")
    chip = (2 * lax.axis_index("x") + lax.axis_index("y")).astype(jnp.int32).reshape(1)
    m_small = _small_block(m_conv_w[0], m_gla_wa2_fwd[0], m_gla_wa2_bwd[0])
    v_small = _small_block(v_conv_w[0], v_gla_wa2_fwd[0], v_gla_wa2_bwd[0])
    upd_in = _adamw(part_in, w_in_t, m_in_t, v_in_t, tile=(IN_TOTAL // N_DEV, 512), name="adamw_w_in",
                    own=own_in, chip=chip)
    upd_in = [jnp.swapaxes(u, 0, 1) for u in upd_in]
    upd_out = _adamw(part_out, w_out[0], m_w_out[0], v_w_out[0], tile=(256, D_MODEL), name="adamw_w_out")
    upd_up = _adamw(part_up, w_up[0], m_w_up[0], v_w_up[0], tile=(256, UP_BLOCK), name="adamw_w_up")
    upd_down = _adamw(part_down, w_down[0], m_w_down[0], v_w_down[0], tile=(D_FF // N_DEV // 4, D_MODEL),
                      name="adamw_w_down")
    upd_small = _adamw(part_small, w_small, m_small, v_small, tile=part_small.shape[1:],
                       name="adamw_small", own=own_small, chip=chip)
    upd_small = [_small_unblock(u) for u in upd_small]

    rep_names = ["norm1_g", "attn_q_norm_g", "attn_k_norm_g", "attn_sink", "gla_ba_fwd", "gla_ba_bwd",
                 "gla_out_norm_g", "norm2_g", "conv_b"]
    def whole_lanes(sink_like):
        return jnp.pad(sink_like, ((0, 0), (0, LANES - ATTN_HEADS)))

    rep_w = [norm1_g, attn_q_norm_g, attn_k_norm_g, whole_lanes(attn_sink), gla_ba_fwd, gla_ba_bwd, gla_out_norm_g,
             norm2_g, conv_b]
    rep_m = [m_norm1_g, m_attn_q_norm_g, m_attn_k_norm_g, whole_lanes(m_attn_sink), m_gla_ba_fwd, m_gla_ba_bwd,
             m_gla_out_norm_g, m_norm2_g, m_conv_b]
    rep_v = [v_norm1_g, v_attn_q_norm_g, v_attn_k_norm_g, whole_lanes(v_attn_sink), v_gla_ba_fwd, v_gla_ba_bwd,
             v_gla_out_norm_g, v_norm2_g, v_conv_b]
    d_sink = whole_lanes(d_sink8[:, :GQA_GROUP, 0].reshape(1, ATTN_HEADS))
    rep_g = [d_norm1, d_qn, d_kn, d_sink, d_ba2[:, :GLA_KEY_WIDTH], d_ba2[:, GLA_KEY_WIDTH:], d_gla_norm, d_norm2,
             jnp.concatenate([dcb_g, dcb_v], axis=1)]
    (rep_terms,) = _all_gather([_pack_rows(rep_g + [loss_part])], name="gather_small_grads", collective_id=7)
    upd_rep, loss_row = _adamw_rows(rep_terms, rep_w, rep_m, rep_v)
    sink_at = rep_names.index("attn_sink")
    for kind in range(4):
        upd_rep[kind][sink_at] = upd_rep[kind][sink_at][:, :ATTN_HEADS]
    loss = loss_row[0, 0]

    order = ["norm1_g", "w_in", "attn_q_norm_g", "attn_k_norm_g", "attn_sink", "gla_wa2_fwd", "gla_ba_fwd",
             "gla_wa2_bwd", "gla_ba_bwd", "gla_out_norm_g", "w_out", "norm2_g", "w_up", "conv_w", "conv_b", "w_down"]
    outs = [loss, grad_x.reshape(1, t, D_MODEL)]
    for kind in range(4):
        by_name = {n: upd_rep[kind][i] for i, n in enumerate(rep_names)}
        by_name["w_in"] = upd_in[kind][None]
        by_name["w_out"] = upd_out[kind][None]
        by_name["w_up"] = upd_up[kind][None]
        by_name["w_down"] = upd_down[kind][None]
        by_name["conv_w"] = upd_small[kind][0][None]
        by_name["gla_wa2_fwd"] = upd_small[kind][1][None]
        by_name["gla_wa2_bwd"] = upd_small[kind][2][None]
        outs += [by_name[n] for n in order]
    return tuple(outs)
```
